```python
import math
import jax, jax.numpy as jnp
from jax import lax
import numpy as np

D_MODEL = 1024
BATCH = 8
SEQ = 2048
DEPTH = 2

EPS = 1e-6
HEAD_DIM = 64
ATT_HEADS = 4
ATT_WIDTH = ATT_HEADS * HEAD_DIM
DILATED_PATTERNS = ((128, 1), (512, 4), (2048, 16))
ATT_BLOCK = 128
REL_BUCKETS = 32
REL_MAX_DISTANCE = 2048
GLA_HEADS = 4
GLA_DK = 64
GLA_DV = 128
GLA_RANK = 16
GLA_TAU = 16.0
GLA_CHUNK = 64
CONV_CH = 256
CONV_WIDTH = 3
MIX_WIDTH = ATT_WIDTH + GLA_HEADS * GLA_DV + CONV_CH
SPLIT_SIZES = (ATT_WIDTH, ATT_WIDTH, ATT_WIDTH,
               GLA_HEADS * GLA_DK, GLA_HEADS * GLA_DK,
               GLA_HEADS * GLA_DV, GLA_HEADS * GLA_DV,
               GLA_RANK,
               CONV_CH, CONV_CH, CONV_CH)
IN_COLS = 3 * ATT_WIDTH + 2 * GLA_HEADS * GLA_DK + 2 * GLA_HEADS * GLA_DV + GLA_RANK + 3 * CONV_CH
FFN_DENSE = 2816
N_EXPERTS = 8
TOP_K = 2
FFN_EXPERT = 3584
N_DENSE = (DEPTH + 1) // 2
N_MOE = DEPTH // 2

kernel_name = "hybrid_dilated_gla_shortconv_moe"


def rmsnorm(x, g):
    xf = x.astype(jnp.float32)
    y = xf * lax.rsqrt(jnp.mean(xf * xf, axis=-1, keepdims=True) + EPS)
    return (y * g.astype(jnp.float32)).astype(x.dtype)


def rel_bucket(dist):
    max_exact = REL_BUCKETS // 2
    d = jnp.maximum(dist, 0)
    log_ratio = jnp.log(jnp.maximum(d, 1).astype(jnp.float32) / max_exact) / math.log(REL_MAX_DISTANCE / max_exact)
    large = jnp.minimum(max_exact + (log_ratio * (REL_BUCKETS - max_exact)).astype(jnp.int32), REL_BUCKETS - 1)
    return jnp.where(d < max_exact, d, large)


def dilated_branch(q, k, v, rel_bias, window, dilation):
    b, s, h, hd = q.shape
    span = window // dilation
    L = s // dilation
    nblk = -(-L // ATT_BLOCK)
    Lp = nblk * ATT_BLOCK
    pad_end = ((0, 0), (0, Lp - L), (0, 0), (0, 0), (0, 0))

    def sub(t):
        return jnp.pad(t.reshape(b, L, dilation, h, hd), pad_end)

    qs = sub(q).reshape(b, nblk, ATT_BLOCK, dilation, h, hd)
    pad_front = ((0, 0), (ATT_BLOCK, 0), (0, 0), (0, 0), (0, 0))
    kp = jnp.pad(sub(k), pad_front)
    vp = jnp.pad(sub(v), pad_front)

    def span_blocks(t):
        prev = t[:, :Lp].reshape(b, nblk, ATT_BLOCK, dilation, h, hd)
        cur = t[:, ATT_BLOCK:].reshape(b, nblk, ATT_BLOCK, dilation, h, hd)
        return jnp.concatenate([prev, cur], axis=2)

    ks = span_blocks(kp)
    vs = span_blocks(vp)
    logits = jnp.einsum('bnqrhd,bnkrhd->bnrhqk', qs, ks).astype(jnp.float32) * (hd ** -0.5)

    qi = jnp.arange(ATT_BLOCK)[:, None]
    kj = jnp.arange(2 * ATT_BLOCK)[None, :]
    sub_dist = qi - kj + ATT_BLOCK
    band = (sub_dist >= 0) & (sub_dist <= span)
    bias = jnp.transpose(rel_bias[rel_bucket(sub_dist * dilation)].astype(jnp.float32), (2, 0, 1))
    key_exists = ~((jnp.arange(nblk)[:, None, None] == 0) & (kj[None] < ATT_BLOCK))
    valid = band[None] & key_exists
    logits = jnp.where(valid[None, :, None, None], logits + bias[None, None, None], -jnp.inf)

    lse = jax.nn.logsumexp(logits, axis=-1)
    p = jnp.exp(logits - lse[..., None]).astype(v.dtype)
    o = jnp.einsum('bnrhqk,bnkrhd->bnqrhd', p, vs)
    o = o.reshape(b, Lp, dilation, h, hd)[:, :L].reshape(b, s, h, hd)
    lse = jnp.transpose(lse, (0, 1, 4, 2, 3)).reshape(b, Lp, dilation, h)[:, :L].reshape(b, s, h)
    return o, lse


def dilated_attention(q, k, v, rel_bias):
    outs, lses = [], []
    for window, dilation in DILATED_PATTERNS:
        o, lse = dilated_branch(q, k, v, rel_bias, window, dilation)
        outs.append(o)
        lses.append(lse)
    alpha = jax.nn.softmax(jnp.stack(lses, axis=0), axis=0)
    o = jnp.sum(alpha[..., None] * jnp.stack(outs, axis=0).astype(jnp.float32), axis=0)
    return o.astype(q.dtype)


def gla_mixer(q, k, v, log_a):
    b, s, h, dk = q.shape
    dv = v.shape[-1]
    n = s // GLA_CHUNK

    def chunks(t):
        return jnp.transpose(t.astype(jnp.float32).reshape(b, n, GLA_CHUNK, h, t.shape[-1]), (1, 0, 3, 2, 4))

    qc, kc, vc, ac = chunks(q * (dk ** -0.5)), chunks(k), chunks(v), chunks(log_a)
    causal = jnp.tril(jnp.ones((GLA_CHUNK, GLA_CHUNK), dtype=bool))

    def step(state, inp):
        qt, kt, vt, at = inp
        cum = jnp.cumsum(at, axis=2)
        o_inter = jnp.einsum('bhck,bhkv->bhcv', qt * jnp.exp(cum), state)
        rel = cum[:, :, :, None, :] - cum[:, :, None, :, :]
        decay = jnp.exp(jnp.where(causal[:, :, None], rel, -jnp.inf))
        scores = jnp.einsum('bhtk,bhsk,bhtsk->bhts', qt, kt, decay)
        o_intra = jnp.einsum('bhts,bhsv->bhtv', scores, vt)
        last = cum[:, :, -1:, :]
        new_state = jnp.exp(last[:, :, 0, :, None]) * state + jnp.einsum('bhck,bhcv->bhkv', kt * jnp.exp(last - cum), vt)
        return new_state, o_inter + o_intra

    state0 = jnp.zeros((b, h, dk, dv), jnp.float32)
    _, o = lax.scan(step, state0, (qc, kc, vc, ac))
    return jnp.transpose(o, (1, 0, 3, 2, 4)).reshape(b, s, h, dv)


def short_conv_mixer(h_in, gate_b, gate_c, conv_w):
    u = gate_c * h_in
    y = lax.conv_general_dilated(u, conv_w[:, None, :].astype(u.dtype), window_strides=(1,),
                                 padding=[(CONV_WIDTH - 1, 0)],
                                 dimension_numbers=('NWC', 'WIO', 'NWC'),
                                 feature_group_count=CONV_CH)
    return gate_b * y


def swiglu(h, w1, w3, w2):
    return (jax.nn.silu(h @ w1) * (h @ w3)) @ w2


def moe_swiglu(h, w_router, w1, w3, w2):
    logits = jnp.einsum('bsd,de->bse', h, w_router).astype(jnp.float32)
    top_val, top_idx = lax.top_k(logits, TOP_K)
    top_w = jax.nn.softmax(top_val, axis=-1)
    gates = jnp.sum(jax.nn.one_hot(top_idx, N_EXPERTS, dtype=jnp.float32) * top_w[..., None], axis=-2)
    out = jnp.zeros_like(h)
    for e in range(N_EXPERTS):
        out = out + gates[..., e:e + 1].astype(h.dtype) * swiglu(h, w1[e], w3[e], w2[e])
    return out


def setup_inputs(seed: int = 0) -> dict:
    key = jax.random.key(seed)
    ks = jax.random.split(key, 20)
    f32 = jnp.float32

    def nrm(k, shape, fan_in):
        return jax.random.normal(k, shape, f32) * fan_in ** -0.5

    def gain(k, shape):
        return 1.0 + 0.02 * jax.random.normal(k, shape, f32)

    return {
        "x": jax.random.normal(ks[0], (BATCH, SEQ, D_MODEL), f32),
        "w_mix_in": nrm(ks[1], (DEPTH, D_MODEL, IN_COLS), D_MODEL),
        "w_mix_out": nrm(ks[2], (DEPTH, MIX_WIDTH, D_MODEL), MIX_WIDTH),
        "g_mix": gain(ks[3], (DEPTH, D_MODEL)),
        "rel_bias": 0.2 * jax.random.normal(ks[4], (REL_BUCKETS, ATT_HEADS), f32),
        "gla_w_gate": nrm(ks[5], (DEPTH, GLA_RANK, GLA_HEADS * GLA_DK), GLA_RANK),
        "gla_b_gate": 0.1 * jax.random.normal(ks[6], (DEPTH, GLA_HEADS * GLA_DK), f32),
        "gla_g_norm": gain(ks[7], (DEPTH, GLA_DV)),
        "conv_w": nrm(ks[8], (DEPTH, CONV_WIDTH, CONV_CH), CONV_WIDTH),
        "g_ffn": gain(ks[9], (DEPTH, D_MODEL)),
        "ffn_w1": nrm(ks[10], (N_DENSE, D_MODEL, FFN_DENSE), D_MODEL),
        "ffn_w3": nrm(ks[11], (N_DENSE, D_MODEL, FFN_DENSE), D_MODEL),
        "ffn_w2": nrm(ks[12], (N_DENSE, FFN_DENSE, D_MODEL), FFN_DENSE),
        "moe_router": nrm(ks[13], (N_MOE, D_MODEL, N_EXPERTS), D_MODEL),
        "moe_w1": nrm(ks[14], (N_MOE, N_EXPERTS, D_MODEL, FFN_EXPERT), D_MODEL),
        "moe_w3": nrm(ks[15], (N_MOE, N_EXPERTS, D_MODEL, FFN_EXPERT), D_MODEL),
        "moe_w2": nrm(ks[16], (N_MOE, N_EXPERTS, FFN_EXPERT, D_MODEL), FFN_EXPERT),
        "g_final": gain(ks[17], (D_MODEL,)),
    }


def reference(x, w_mix_in, w_mix_out, g_mix, rel_bias, gla_w_gate, gla_b_gate, gla_g_norm, conv_w,
              g_ffn, ffn_w1, ffn_w3, ffn_w2, moe_router, moe_w1, moe_w3, moe_w2, g_final):
    b, s, _ = x.shape
    split_at = np.cumsum(SPLIT_SIZES)[:-1].tolist()
    for layer in range(DEPTH):
        h = rmsnorm(x, g_mix[layer])
        u = h @ w_mix_in[layer]
        (aq, ak, av, gq, gk, gv, gr, glr, c_in, c_b, c_c) = jnp.split(u, split_at, axis=-1)

        att = dilated_attention(aq.reshape(b, s, ATT_HEADS, HEAD_DIM),
                                ak.reshape(b, s, ATT_HEADS, HEAD_DIM),
                                av.reshape(b, s, ATT_HEADS, HEAD_DIM), rel_bias)
        att = att.reshape(b, s, ATT_WIDTH)

        log_a = jax.nn.log_sigmoid((glr @ gla_w_gate[layer] + gla_b_gate[layer]).astype(jnp.float32)) / GLA_TAU
        go = gla_mixer(gq.reshape(b, s, GLA_HEADS, GLA_DK), gk.reshape(b, s, GLA_HEADS, GLA_DK),
                       gv.reshape(b, s, GLA_HEADS, GLA_DV), log_a.reshape(b, s, GLA_HEADS, GLA_DK))
        go = rmsnorm(go.astype(x.dtype), gla_g_norm[layer]).reshape(b, s, GLA_HEADS * GLA_DV) * jax.nn.silu(gr)

        cv = short_conv_mixer(c_in, c_b, c_c, conv_w[layer])

        mixed = jnp.concatenate([att, go, cv], axis=-1)
        x = x + mixed @ w_mix_out[layer]

        h = rmsnorm(x, g_ffn[layer])
        if layer % 2 == 0:
            i = layer // 2
            f = swiglu(h, ffn_w1[i], ffn_w3[i], ffn_w2[i])
        else:
            i = layer // 2
            f = moe_swiglu(h, moe_router[i], moe_w1[i], moe_w3[i], moe_w2[i])
        x = x + f
    return rmsnorm(x, g_final)
```

```python
import functools
import math

import jax
import jax.numpy as jnp
import numpy as np
from jax import lax
from jax.experimental import pallas as pl
from jax.experimental.pallas import tpu as pltpu

F32 = jnp.float32
BF16 = jnp.bfloat16

D_MODEL = 1024
BATCH = 8
SEQ = 2048
TOKENS = BATCH * SEQ
DEPTH = 2
EPS = 1e-6

HEAD_DIM = 64
ATT_HEADS = 4
ATT_WIDTH = ATT_HEADS * HEAD_DIM
DILATED_PATTERNS = ((128, 1), (512, 4), (2048, 16))
ATT_BLOCK = 128
REL_BUCKETS = 32
REL_MAX_DISTANCE = 2048

GLA_HEADS = 4
GLA_DK = 64
GLA_DV = 128
GLA_RANK = 16
GLA_CHUNK = 64
GLA_QK = GLA_HEADS * GLA_DK
GLA_V = GLA_HEADS * GLA_DV

CONV_CH = 256
CONV_WIDTH = 3
MIX_WIDTH = ATT_WIDTH + GLA_V + CONV_CH

SPLIT_SIZES = (ATT_WIDTH, ATT_WIDTH, ATT_WIDTH, GLA_QK, GLA_QK, GLA_V, GLA_V, GLA_RANK,
               CONV_CH, CONV_CH, CONV_CH)

FFN_DENSE = 2816
N_EXPERTS = 8
FFN_EXPERT = 3584

LANES = 128
MXU_WIDTH = 256
VMEM_LIMIT = 56 * 1024 * 1024

COL_AQ, COL_AK, COL_AV = 0, 256, 512
COL_GQ, COL_GV, COL_GR, COL_GK = 768, 1024, 1536, 2048
COL_CIN, COL_CB, COL_CC, COL_GLR = 2304, 2560, 2816, 3072
U_COLS = COL_GLR + LANES

NEG_BIG = -1e30

TM_PROJ = 512
TC_GLA = 512
TM_FFN = 1024
TF_MOE = 512
TF_DENSE = 1408
FFN_SUB = 512


def _params(sem):
    return pltpu.CompilerParams(dimension_semantics=sem, vmem_limit_bytes=VMEM_LIMIT)


def _split_bf16(a):
    hi = a.astype(BF16)
    lo = (a - hi.astype(F32)).astype(BF16)
    return hi, lo


def _dot(a, b):
    return jnp.dot(a, b, preferred_element_type=F32)


def _dot3(a, b):
    a_hi, a_lo = _split_bf16(a)
    b_hi, b_lo = _split_bf16(b)
    return _dot(a_hi, b_hi) + _dot(a_lo, b_hi) + _dot(a_hi, b_lo)


def _dot_nt(a, b):
    return lax.dot_general(a, b, (((1,), (1,)), ((), ())), preferred_element_type=F32)


def _dot_tn(a, b):
    return lax.dot_general(a, b, (((0,), (0,)), ((), ())), preferred_element_type=F32)


def _rms(x, g):
    ms = jnp.mean(x * x, axis=-1, keepdims=True)
    return x * lax.rsqrt(ms + EPS) * g


def _sigmoid(x):
    return 1.0 / (1.0 + jnp.exp(-x))


def _inproj_body(x_ref, g_ref, w_ref, o_ref):
    h = _rms(x_ref[...], g_ref[...]).astype(BF16)
    for c0 in range(0, U_COLS, MXU_WIDTH):
        c1 = min(c0 + MXU_WIDTH, U_COLS)
        o_ref[:, c0:c1] = _dot(h, w_ref[:, c0:c1])


def _inproj(x, g, w):
    return pl.pallas_call(
        _inproj_body,
        grid=(TOKENS // TM_PROJ,),
        in_specs=[
            pl.BlockSpec((TM_PROJ, D_MODEL), lambda i: (i, 0)),
            pl.BlockSpec((1, D_MODEL), lambda i: (0, 0)),
            pl.BlockSpec((D_MODEL, U_COLS), lambda i: (0, 0)),
        ],
        out_specs=pl.BlockSpec((TM_PROJ, U_COLS), lambda i: (i, 0)),
        out_shape=jax.ShapeDtypeStruct((TOKENS, U_COLS), F32),
        compiler_params=_params(("parallel",)),
        name="inproj",
    )(x, g, w)


def _rel_bucket(dist):
    max_exact = REL_BUCKETS // 2
    d = jnp.maximum(dist, 0)
    log_ratio = jnp.log(jnp.maximum(d, 1).astype(F32) / max_exact) / math.log(REL_MAX_DISTANCE / max_exact)
    large = jnp.minimum(max_exact + (log_ratio * (REL_BUCKETS - max_exact)).astype(jnp.int32), REL_BUCKETS - 1)
    return jnp.where(d < max_exact, d, large)


def _bucket_table(window, dilation):
    span = window // dilation
    qi = jnp.arange(ATT_BLOCK)[:, None]
    kj = jnp.arange(2 * ATT_BLOCK)[None, :]
    sub_dist = qi - kj + ATT_BLOCK
    band = (sub_dist >= 0) & (sub_dist <= span)
    return jnp.where(band, _rel_bucket(sub_dist * dilation), -1).astype(jnp.int32)


def _attn_body(rb_ref, bidx_ref, q_ref, k_ref, v_ref, o_ref, lse_ref, bias_ref, *, nblk):
    @pl.when((pl.program_id(0) == 0) & (pl.program_id(1) == 0))
    def _():
        bidx = bidx_ref[...]
        for h in range(ATT_HEADS):
            acc = jnp.full(bidx.shape, NEG_BIG, F32)
            for b in range(REL_BUCKETS):
                acc = jnp.where(bidx == b, rb_ref[b, h], acc)
            bias_ref[h] = acc

    def block(n, first):
        if first:
            q = q_ref[0:ATT_BLOCK, :]
            kk = k_ref[0:ATT_BLOCK, :]
            vv = v_ref[0:ATT_BLOCK, :]
            rows = slice(0, ATT_BLOCK)
        else:
            r0 = pl.multiple_of(n * ATT_BLOCK, ATT_BLOCK)
            rows = pl.ds(r0, ATT_BLOCK)
            q = q_ref[rows, :]
            kk = k_ref[pl.ds(r0 - ATT_BLOCK, 2 * ATT_BLOCK), :]
            vv = v_ref[pl.ds(r0 - ATT_BLOCK, 2 * ATT_BLOCK), :]
        outs, lses = [], []
        for h in range(ATT_HEADS):
            sl = slice(h * HEAD_DIM, (h + 1) * HEAD_DIM)
            qh = q[:, sl] * jnp.asarray(HEAD_DIM ** -0.5, BF16)
            bias = bias_ref[h, :, ATT_BLOCK:] if first else bias_ref[h]
            s = _dot_nt(qh, kk[:, sl]) + bias
            m = jnp.max(s, axis=-1, keepdims=True)
            p = jnp.exp(s - m)
            l = jnp.sum(p, axis=-1, keepdims=True)
            o = _dot(p.astype(BF16), vv[:, sl]) * (1.0 / l)
            outs.append(o)
            lses.append(jnp.broadcast_to(m + jnp.log(l), (ATT_BLOCK, HEAD_DIM)))
        o_ref[rows, :] = jnp.concatenate(outs, axis=-1).astype(o_ref.dtype)
        lse_ref[rows, :] = jnp.concatenate(lses, axis=-1)

    block(0, True)
    if nblk > 1:
        def loop_body(n, carry):
            block(n, False)
            return carry
        lax.fori_loop(1, nblk, loop_body, 0)


def _attention_pattern(ua, rel_bias, window, dilation):
    L = SEQ // dilation
    nblk = L // ATT_BLOCK
    qkv_spec = lambda j: pl.BlockSpec((None, None, L, ATT_WIDTH), lambda b, r: (b, r, 0, j))
    out_spec = pl.BlockSpec((None, None, L, ATT_WIDTH), lambda b, r: (b, r, 0, 0))
    return pl.pallas_call(
        functools.partial(_attn_body, nblk=nblk),
        grid=(BATCH, dilation),
        in_specs=[
            pl.BlockSpec(memory_space=pltpu.SMEM),
            pl.BlockSpec((ATT_BLOCK, 2 * ATT_BLOCK), lambda b, r: (0, 0)),
            qkv_spec(0), qkv_spec(1), qkv_spec(2),
        ],
        out_specs=[out_spec, out_spec],
        out_shape=[jax.ShapeDtypeStruct((BATCH, dilation, L, ATT_WIDTH), BF16),
                   jax.ShapeDtypeStruct((BATCH, dilation, L, ATT_WIDTH), F32)],
        scratch_shapes=[pltpu.VMEM((ATT_HEADS, ATT_BLOCK, 2 * ATT_BLOCK), F32)],
        compiler_params=_params(("arbitrary", "arbitrary")),
        name=f"attn_d{dilation}",
    )(rel_bias, _bucket_table(window, dilation), ua, ua, ua)


def _attention(u, rel_bias):
    res = []
    qkv = u[:, :3 * ATT_WIDTH].astype(BF16)
    for window, dilation in DILATED_PATTERNS:
        L = SEQ // dilation
        ua = qkv.reshape(BATCH, L, dilation, 3 * ATT_WIDTH).transpose(0, 2, 1, 3)
        o, lse = _attention_pattern(ua, rel_bias, window, dilation)
        o = o.transpose(0, 2, 1, 3).reshape(TOKENS, ATT_WIDTH)
        lse = lse.transpose(0, 2, 1, 3).reshape(TOKENS, ATT_WIDTH)
        res.append((o, lse))
    return res


def _gla_body(q_ref, k_ref, v_ref, gr_ref, glr_ref, wg_ref, bg_ref, gn_ref, o_ref, s_ref):
    @pl.when(pl.program_id(1) == 0)
    def _():
        s_ref[...] = jnp.zeros_like(s_ref)

    C = GLA_CHUNK
    row = lax.broadcasted_iota(jnp.int32, (C, C), 0)
    col = lax.broadcasted_iota(jnp.int32, (C, C), 1)
    tril = row >= col
    tril_bf = jnp.where(tril, 1.0, 0.0).astype(BF16)
    ones_bf = jnp.ones((C, GLA_DV), BF16)

    xg = _dot3(glr_ref[...], wg_ref[...]) + bg_ref[...]
    la_all = (jnp.minimum(xg, 0.0) - jnp.log(1.0 + jnp.exp(-jnp.abs(xg)))) * (1.0 / 16.0)

    for c in range(TC_GLA // C):
        rows = slice(c * C, (c + 1) * C)
        la_hi, la_lo = _split_bf16(la_all[rows])
        cum = _dot(tril_bf, la_hi) + _dot(tril_bf, la_lo)
        last = cum[C - 1:C, :]
        q = q_ref[rows, :]
        k = k_ref[rows, :]
        qt = (q * jnp.exp(cum) * (GLA_DK ** -0.5)).astype(BF16)
        kt = (k * jnp.exp(-cum)).astype(BF16)
        kl = (k * jnp.exp(last - cum)).astype(BF16)
        for h in range(GLA_HEADS):
            sl = slice(h * GLA_DK, (h + 1) * GLA_DK)
            vs = slice(h * GLA_DV, (h + 1) * GLA_DV)
            vh = v_ref[rows, vs].astype(BF16)
            state = s_ref[h]
            st_hi, st_lo = _split_bf16(state)
            sc = jnp.where(tril, _dot_nt(qt[:, sl], kt[:, sl]), 0.0).astype(BF16)
            o = _dot(qt[:, sl], st_hi) + _dot(qt[:, sl], st_lo) + _dot(sc, vh)
            ltot = _dot_tn(la_hi[:, sl], ones_bf) + _dot_tn(la_lo[:, sl], ones_bf)
            s_ref[h] = jnp.exp(ltot) * state + _dot_tn(kl[:, sl], vh)
            g = gr_ref[rows, vs]
            o_ref[rows, vs] = (_rms(o, gn_ref[...]) * (g * _sigmoid(g))).astype(o_ref.dtype)


def _gla(u, wg, bg, gn):
    nj = SEQ // TC_GLA
    row = lambda b, j: b * nj + j
    return pl.pallas_call(
        _gla_body,
        grid=(BATCH, nj),
        in_specs=[
            pl.BlockSpec((TC_GLA, GLA_QK), lambda b, j: (row(b, j), COL_GQ // GLA_QK)),
            pl.BlockSpec((TC_GLA, GLA_QK), lambda b, j: (row(b, j), COL_GK // GLA_QK)),
            pl.BlockSpec((TC_GLA, GLA_V), lambda b, j: (row(b, j), COL_GV // GLA_V)),
            pl.BlockSpec((TC_GLA, GLA_V), lambda b, j: (row(b, j), COL_GR // GLA_V)),
            pl.BlockSpec((TC_GLA, LANES), lambda b, j: (row(b, j), COL_GLR // LANES)),
            pl.BlockSpec((LANES, GLA_QK), lambda b, j: (0, 0)),
            pl.BlockSpec((1, GLA_QK), lambda b, j: (0, 0)),
            pl.BlockSpec((1, GLA_DV), lambda b, j: (0, 0)),
        ],
        out_specs=pl.BlockSpec((TC_GLA, GLA_V), lambda b, j: (row(b, j), 0)),
        out_shape=jax.ShapeDtypeStruct((TOKENS, GLA_V), BF16),
        scratch_shapes=[pltpu.VMEM((GLA_HEADS, GLA_DK, GLA_DV), F32)],
        compiler_params=_params(("arbitrary", "arbitrary")),
        name="gla",
    )(u, u, u, u, u, wg, bg, gn)


def _conv_body(cin_ref, cb_ref, cc_ref, w_ref, o_ref):
    uu = cc_ref[...] * cin_ref[...]
    t = lax.broadcasted_iota(jnp.int32, uu.shape, 0)
    y = uu * w_ref[CONV_WIDTH - 1:CONV_WIDTH, :]
    for shift in range(1, CONV_WIDTH):
        prev = jnp.where(t >= shift, pltpu.roll(uu, shift, axis=0), 0.0)
        y = y + prev * w_ref[CONV_WIDTH - 1 - shift:CONV_WIDTH - shift, :]
    o_ref[...] = (cb_ref[...] * y).astype(o_ref.dtype)


def _conv(u, w):
    spec = lambda col: pl.BlockSpec((SEQ, CONV_CH), lambda b: (b, col // CONV_CH))
    return pl.pallas_call(
        _conv_body,
        grid=(BATCH,),
        in_specs=[spec(COL_CIN), spec(COL_CB), spec(COL_CC),
                  pl.BlockSpec((8, CONV_CH), lambda b: (0, 0))],
        out_specs=pl.BlockSpec((SEQ, CONV_CH), lambda b: (b, 0)),
        out_shape=jax.ShapeDtypeStruct((TOKENS, CONV_CH), BF16),
        compiler_params=_params(("parallel",)),
        name="conv",
    )(u, u, u, w)


def _outproj_body(*refs, route):
    (o1, o4, o16, l1, l4, l16, go_ref, cv_ref, x_ref, w_ref, g_ref) = refs[:11]
    if route:
        wr_ref, xo_ref, ho_ref, gates_ref = refs[11:]
    else:
        xo_ref, ho_ref = refs[11:]
    la, lb, lc = l1[...], l4[...], l16[...]
    m = jnp.maximum(jnp.maximum(la, lb), lc)
    ea, eb, ec = jnp.exp(la - m), jnp.exp(lb - m), jnp.exp(lc - m)
    att = (ea * o1[...].astype(F32) + eb * o4[...].astype(F32) + ec * o16[...].astype(F32)) / (ea + eb + ec)
    y = (x_ref[...]
         + _dot(att.astype(BF16), w_ref[0:ATT_WIDTH, :])
         + _dot(go_ref[...], w_ref[ATT_WIDTH:ATT_WIDTH + GLA_V, :])
         + _dot(cv_ref[...], w_ref[ATT_WIDTH + GLA_V:MIX_WIDTH, :]))
    xo_ref[...] = y
    hf = _rms(y, g_ref[...])
    ho_ref[...] = hf.astype(ho_ref.dtype)
    if route:
        logits = _dot3(hf, wr_ref[...])
        lane = lax.broadcasted_iota(jnp.int32, logits.shape, 1).astype(F32)
        lg = jnp.where(lane < N_EXPERTS, logits, -jnp.inf)
        v1 = jnp.max(lg, axis=-1, keepdims=True)
        i1 = jnp.min(jnp.where(lg == v1, lane, float(LANES)), axis=-1, keepdims=True)
        lg2 = jnp.where(lane == i1, -jnp.inf, lg)
        v2 = jnp.max(lg2, axis=-1, keepdims=True)
        i2 = jnp.min(jnp.where(lg2 == v2, lane, float(LANES)), axis=-1, keepdims=True)
        e2 = jnp.exp(v2 - v1)
        w1 = 1.0 / (1.0 + e2)
        gates_ref[...] = jnp.where(lane == i1, w1, 0.0) + jnp.where(lane == i2, e2 * w1, 0.0)


def _outproj(att, go, cv, x, w, g, w_router=None):
    route = w_router is not None
    tm = TM_PROJ
    tile = lambda cols: pl.BlockSpec((tm, cols), lambda i: (i, 0))
    full = lambda a: pl.BlockSpec(a.shape, lambda i: (0, 0))
    (o1, l1), (o4, l4), (o16, l16) = att
    args = [o1, o4, o16, l1, l4, l16, go, cv, x, w, g]
    in_specs = [tile(ATT_WIDTH)] * 6 + [tile(GLA_V), tile(CONV_CH), tile(D_MODEL), full(w), full(g)]
    out_specs = [tile(D_MODEL), tile(D_MODEL)]
    out_shape = [jax.ShapeDtypeStruct((TOKENS, D_MODEL), F32), jax.ShapeDtypeStruct((TOKENS, D_MODEL), BF16)]
    if route:
        args.append(w_router)
        in_specs.append(full(w_router))
        out_specs.append(tile(LANES))
        out_shape.append(jax.ShapeDtypeStruct((TOKENS, LANES), F32))
    return pl.pallas_call(
        functools.partial(_outproj_body, route=route),
        grid=(TOKENS // tm,),
        in_specs=in_specs,
        out_specs=out_specs,
        out_shape=out_shape,
        compiler_params=_params(("parallel",)),
        name="outproj_route" if route else "outproj",
    )(*args)


def _ffn_body(*refs, gated, final, tf):
    if gated:
        x_ref, h_ref, gates_ref, w1_ref, w3_ref, w2_ref, gf_ref, o_ref, acc_ref = refs
    else:
        x_ref, h_ref, w1_ref, w3_ref, w2_ref, gf_ref, o_ref, acc_ref = refs
    e = pl.program_id(1)
    j = pl.program_id(2)

    @pl.when((e == 0) & (j == 0))
    def _():
        acc_ref[...] = x_ref[...]

    h = h_ref[...]
    if gated:
        lane = lax.broadcasted_iota(jnp.int32, gates_ref.shape, 1)
        gcol = jnp.sum(jnp.where(lane == e, gates_ref[...], 0.0), axis=-1, keepdims=True)
    for c0 in range(0, tf, FFN_SUB):
        c1 = min(c0 + FFN_SUB, tf)
        a = _dot(h, w1_ref[:, c0:c1])
        b = _dot(h, w3_ref[:, c0:c1])
        act = a * _sigmoid(a) * b
        if gated:
            act = act * gcol
        acc_ref[...] += _dot(act.astype(BF16), w2_ref[c0:c1, :])

    @pl.when((e == pl.num_programs(1) - 1) & (j == pl.num_programs(2) - 1))
    def _():
        y = acc_ref[...]
        o_ref[...] = _rms(y, gf_ref[...]) if final else y


def _ffn(x, h, w1, w3, w2, g_final, *, gates=None, final, tf):
    n_e, _, f = w1.shape
    gated = gates is not None
    tm = TM_FFN
    tile = lambda cols: pl.BlockSpec((tm, cols), lambda i, e, j: (i, 0))
    args = [x, h] + ([gates] if gated else []) + [w1, w3, w2, g_final]
    in_specs = ([tile(D_MODEL), tile(D_MODEL)] + ([tile(LANES)] if gated else []) + [
        pl.BlockSpec((None, D_MODEL, tf), lambda i, e, j: (e, 0, j)),
        pl.BlockSpec((None, D_MODEL, tf), lambda i, e, j: (e, 0, j)),
        pl.BlockSpec((None, tf, D_MODEL), lambda i, e, j: (e, j, 0)),
        pl.BlockSpec((1, D_MODEL), lambda i, e, j: (0, 0)),
    ])
    return pl.pallas_call(
        functools.partial(_ffn_body, gated=gated, final=final, tf=tf),
        grid=(TOKENS // tm, n_e, f // tf),
        in_specs=in_specs,
        out_specs=tile(D_MODEL),
        out_shape=jax.ShapeDtypeStruct((TOKENS, D_MODEL), F32),
        scratch_shapes=[pltpu.VMEM((tm, D_MODEL), F32)],
        compiler_params=_params(("parallel", "arbitrary", "arbitrary")),
        name="moe_ffn" if gated else "dense_ffn",
    )(*args)


def _prep_w_in(w):
    aq, ak, av, gq, gk, gv, gr, glr, c_in, c_b, c_c = jnp.split(w, np.cumsum(SPLIT_SIZES)[:-1].tolist(), axis=1)
    pad = jnp.zeros((D_MODEL, LANES - GLA_RANK), w.dtype)
    return jnp.concatenate([aq, ak, av, gq, gv, gr, gk, c_in, c_b, c_c, glr, pad], axis=1).astype(BF16)


def kernel(x, w_mix_in, w_mix_out, g_mix, rel_bias, gla_w_gate, gla_b_gate, gla_g_norm, conv_w,
           g_ffn, ffn_w1, ffn_w3, ffn_w2, moe_router, moe_w1, moe_w3, moe_w2, g_final):
    x = x.reshape(TOKENS, D_MODEL)
    g_fin = g_final.reshape(1, D_MODEL)
    for layer in range(DEPTH):
        u = _inproj(x, g_mix[layer].reshape(1, D_MODEL), _prep_w_in(w_mix_in[layer]))
        att = _attention(u, rel_bias)
        wg = jnp.pad(gla_w_gate[layer], ((0, LANES - GLA_RANK), (0, 0)))
        go = _gla(u, wg, gla_b_gate[layer].reshape(1, GLA_QK), gla_g_norm[layer].reshape(1, GLA_DV))
        cv = _conv(u, jnp.pad(conv_w[layer], ((0, 8 - CONV_WIDTH), (0, 0))))
        w_out = w_mix_out[layer].astype(BF16)
        g2 = g_ffn[layer].reshape(1, D_MODEL)
        i = layer // 2
        last = layer == DEPTH - 1
        if layer % 2 == 0:
            x, h = _outproj(att, go, cv, x, w_out, g2)
            x = _ffn(x, h, ffn_w1[i][None].astype(BF16), ffn_w3[i][None].astype(BF16),
                     ffn_w2[i][None].astype(BF16), g_fin, final=last, tf=TF_DENSE)
        else:
            wr = jnp.pad(moe_router[i], ((0, 0), (0, LANES - N_EXPERTS)))
            x, h, gates = _outproj(att, go, cv, x, w_out, g2, wr)
            x = _ffn(x, h, moe_w1[i].astype(BF16), moe_w3[i].astype(BF16), moe_w2[i].astype(BF16),
                     g_fin, gates=gates, final=last, tf=TF_MOE)
    return x.reshape(BATCH, SEQ, D_MODEL)
```

```python
import functools
import math

import jax
import jax.numpy as jnp
import numpy as np
from jax import lax
from jax.experimental import pallas as pl
from jax.experimental.pallas import tpu as pltpu

F32 = jnp.float32
BF16 = jnp.bfloat16

D_MODEL = 1024
BATCH = 8
SEQ = 2048
TOKENS = BATCH * SEQ
DEPTH = 2
EPS = 1e-6

HEAD_DIM = 64
ATT_HEADS = 4
ATT_WIDTH = ATT_HEADS * HEAD_DIM
DILATED_PATTERNS = ((128, 1), (512, 4), (2048, 16))
ATT_BLOCK = 128
REL_BUCKETS = 32
REL_MAX_DISTANCE = 2048

GLA_HEADS = 4
GLA_DK = 64
GLA_DV = 128
GLA_RANK = 16
GLA_CHUNK = 64
GLA_QK = GLA_HEADS * GLA_DK
GLA_V = GLA_HEADS * GLA_DV

CONV_CH = 256
CONV_WIDTH = 3
MIX_WIDTH = ATT_WIDTH + GLA_V + CONV_CH

SPLIT_SIZES = (ATT_WIDTH, ATT_WIDTH, ATT_WIDTH, GLA_QK, GLA_QK, GLA_V, GLA_V, GLA_RANK,
               CONV_CH, CONV_CH, CONV_CH)

FFN_DENSE = 2816
N_EXPERTS = 8
FFN_EXPERT = 3584

LANES = 128
MXU_WIDTH = 256
VMEM_LIMIT = 56 * 1024 * 1024

COL_AQ, COL_AK, COL_AV = 0, 256, 512
COL_GQ, COL_GV, COL_GR, COL_GK = 768, 1024, 1536, 2048
COL_CIN, COL_CB, COL_CC, COL_GLR = 2304, 2560, 2816, 3072
U_COLS = COL_GLR + LANES

NEG_BIG = -1e30

TM_PROJ = 512
TC_GLA = 512
TM_FFN = 1024
TF_DENSE = 1408
FFN_SUB = 512
TM_MOE = 512
TF_MOE = 512
N_TILES_MOE = 2 * TOKENS // TM_MOE + N_EXPERTS
N_SORTED = N_TILES_MOE * TM_MOE
DISPATCH_CHUNK = 256
TM_COMBINE = 256

ROUTE_I1, ROUTE_I2, ROUTE_W1, ROUTE_W2, ROUTE_R1, ROUTE_R2 = range(6)


def _params(sem):
    return pltpu.CompilerParams(dimension_semantics=sem, vmem_limit_bytes=VMEM_LIMIT)


def _split_bf16(a):
    hi = a.astype(BF16)
    lo = (a - hi.astype(F32)).astype(BF16)
    return hi, lo


def _dot(a, b):
    return jnp.dot(a, b, preferred_element_type=F32)


def _dot3(a, b):
    a_hi, a_lo = _split_bf16(a)
    b_hi, b_lo = _split_bf16(b)
    return _dot(a_hi, b_hi) + _dot(a_lo, b_hi) + _dot(a_hi, b_lo)


def _dot_nt(a, b):
    return lax.dot_general(a, b, (((1,), (1,)), ((), ())), preferred_element_type=F32)


def _dot_tn(a, b):
    return lax.dot_general(a, b, (((0,), (0,)), ((), ())), preferred_element_type=F32)


def _rms(x, g):
    ms = jnp.mean(x * x, axis=-1, keepdims=True)
    return x * lax.rsqrt(ms + EPS) * g


def _sigmoid(x):
    return 1.0 / (1.0 + jnp.exp(-x))


def _inproj_body(x_ref, g_ref, w_ref, o_ref):
    h = _rms(x_ref[...], g_ref[...]).astype(BF16)
    for c0 in range(0, U_COLS, MXU_WIDTH):
        c1 = min(c0 + MXU_WIDTH, U_COLS)
        o_ref[:, c0:c1] = _dot(h, w_ref[:, c0:c1])


def _inproj(x, g, w):
    return pl.pallas_call(
        _inproj_body,
        grid=(TOKENS // TM_PROJ,),
        in_specs=[
            pl.BlockSpec((TM_PROJ, D_MODEL), lambda i: (i, 0)),
            pl.BlockSpec((1, D_MODEL), lambda i: (0, 0)),
            pl.BlockSpec((D_MODEL, U_COLS), lambda i: (0, 0)),
        ],
        out_specs=pl.BlockSpec((TM_PROJ, U_COLS), lambda i: (i, 0)),
        out_shape=jax.ShapeDtypeStruct((TOKENS, U_COLS), F32),
        compiler_params=_params(("parallel",)),
        name="inproj",
    )(x, g, w)


def _rel_bucket(dist):
    max_exact = REL_BUCKETS // 2
    d = jnp.maximum(dist, 0)
    log_ratio = jnp.log(jnp.maximum(d, 1).astype(F32) / max_exact) / math.log(REL_MAX_DISTANCE / max_exact)
    large = jnp.minimum(max_exact + (log_ratio * (REL_BUCKETS - max_exact)).astype(jnp.int32), REL_BUCKETS - 1)
    return jnp.where(d < max_exact, d, large)


def _bucket_table(window, dilation):
    span = window // dilation
    qi = jnp.arange(ATT_BLOCK)[:, None]
    kj = jnp.arange(2 * ATT_BLOCK)[None, :]
    sub_dist = qi - kj + ATT_BLOCK
    band = (sub_dist >= 0) & (sub_dist <= span)
    return jnp.where(band, _rel_bucket(sub_dist * dilation), -1).astype(jnp.int32)


def _attn_body(rb_ref, bidx_ref, q_ref, k_ref, v_ref, o_ref, lse_ref, bias_ref, *, nblk):
    @pl.when((pl.program_id(0) == 0) & (pl.program_id(1) == 0))
    def _():
        bidx = bidx_ref[...]
        for h in range(ATT_HEADS):
            acc = jnp.full(bidx.shape, NEG_BIG, F32)
            for b in range(REL_BUCKETS):
                acc = jnp.where(bidx == b, rb_ref[b, h], acc)
            bias_ref[h] = acc

    def block(n, first):
        if first:
            q = q_ref[0:ATT_BLOCK, :]
            kk = k_ref[0:ATT_BLOCK, :]
            vv = v_ref[0:ATT_BLOCK, :]
            rows = slice(0, ATT_BLOCK)
        else:
            r0 = pl.multiple_of(n * ATT_BLOCK, ATT_BLOCK)
            rows = pl.ds(r0, ATT_BLOCK)
            q = q_ref[rows, :]
            kk = k_ref[pl.ds(r0 - ATT_BLOCK, 2 * ATT_BLOCK), :]
            vv = v_ref[pl.ds(r0 - ATT_BLOCK, 2 * ATT_BLOCK), :]
        outs, lses = [], []
        for h in range(ATT_HEADS):
            sl = slice(h * HEAD_DIM, (h + 1) * HEAD_DIM)
            qh = q[:, sl] * jnp.asarray(HEAD_DIM ** -0.5, BF16)
            bias = bias_ref[h, :, ATT_BLOCK:] if first else bias_ref[h]
            s = _dot_nt(qh, kk[:, sl]) + bias
            m = jnp.max(s, axis=-1, keepdims=True)
            p = jnp.exp(s - m)
            l = jnp.sum(p, axis=-1, keepdims=True)
            o = _dot(p.astype(BF16), vv[:, sl]) * (1.0 / l)
            outs.append(o)
            lses.append(jnp.broadcast_to(m + jnp.log(l), (ATT_BLOCK, HEAD_DIM)))
        o_ref[rows, :] = jnp.concatenate(outs, axis=-1).astype(o_ref.dtype)
        lse_ref[rows, :] = jnp.concatenate(lses, axis=-1)

    block(0, True)
    if nblk > 1:
        def loop_body(n, carry):
            block(n, False)
            return carry
        lax.fori_loop(1, nblk, loop_body, 0)


def _attention_pattern(ua, rel_bias, window, dilation):
    L = SEQ // dilation
    nblk = L // ATT_BLOCK
    qkv_spec = lambda j: pl.BlockSpec((None, None, L, ATT_WIDTH), lambda b, r: (b, r, 0, j))
    out_spec = pl.BlockSpec((None, None, L, ATT_WIDTH), lambda b, r: (b, r, 0, 0))
    return pl.pallas_call(
        functools.partial(_attn_body, nblk=nblk),
        grid=(BATCH, dilation),
        in_specs=[
            pl.BlockSpec(memory_space=pltpu.SMEM),
            pl.BlockSpec((ATT_BLOCK, 2 * ATT_BLOCK), lambda b, r: (0, 0)),
            qkv_spec(0), qkv_spec(1), qkv_spec(2),
        ],
        out_specs=[out_spec, out_spec],
        out_shape=[jax.ShapeDtypeStruct((BATCH, dilation, L, ATT_WIDTH), BF16),
                   jax.ShapeDtypeStruct((BATCH, dilation, L, ATT_WIDTH), F32)],
        scratch_shapes=[pltpu.VMEM((ATT_HEADS, ATT_BLOCK, 2 * ATT_BLOCK), F32)],
        compiler_params=_params(("arbitrary", "arbitrary")),
        name=f"attn_d{dilation}",
    )(rel_bias, _bucket_table(window, dilation), ua, ua, ua)


def _attention(u, rel_bias):
    res = []
    qkv = u[:, :3 * ATT_WIDTH].astype(BF16)
    for window, dilation in DILATED_PATTERNS:
        L = SEQ // dilation
        ua = qkv.reshape(BATCH, L, dilation, 3 * ATT_WIDTH).transpose(0, 2, 1, 3)
        o, lse = _attention_pattern(ua, rel_bias, window, dilation)
        o = o.transpose(0, 2, 1, 3).reshape(TOKENS, ATT_WIDTH)
        lse = lse.transpose(0, 2, 1, 3).reshape(TOKENS, ATT_WIDTH)
        res.append((o, lse))
    return res


def _gla_body(q_ref, k_ref, v_ref, gr_ref, glr_ref, wg_ref, bg_ref, gn_ref, o_ref, s_ref):
    @pl.when(pl.program_id(1) == 0)
    def _():
        s_ref[...] = jnp.zeros_like(s_ref)

    C = GLA_CHUNK
    row = lax.broadcasted_iota(jnp.int32, (C, C), 0)
    col = lax.broadcasted_iota(jnp.int32, (C, C), 1)
    tril = row >= col
    tril_bf = jnp.where(tril, 1.0, 0.0).astype(BF16)
    ones_bf = jnp.ones((C, GLA_DV), BF16)

    xg = _dot3(glr_ref[...], wg_ref[...]) + bg_ref[...]
    la_all = (jnp.minimum(xg, 0.0) - jnp.log(1.0 + jnp.exp(-jnp.abs(xg)))) * (1.0 / 16.0)

    for c in range(TC_GLA // C):
        rows = slice(c * C, (c + 1) * C)
        la_hi, la_lo = _split_bf16(la_all[rows])
        cum = _dot(tril_bf, la_hi) + _dot(tril_bf, la_lo)
        last = cum[C - 1:C, :]
        q = q_ref[rows, :]
        k = k_ref[rows, :]
        qt = (q * jnp.exp(cum) * (GLA_DK ** -0.5)).astype(BF16)
        kt = (k * jnp.exp(-cum)).astype(BF16)
        kl = (k * jnp.exp(last - cum)).astype(BF16)
        for h in range(GLA_HEADS):
            sl = slice(h * GLA_DK, (h + 1) * GLA_DK)
            vs = slice(h * GLA_DV, (h + 1) * GLA_DV)
            vh = v_ref[rows, vs].astype(BF16)
            state = s_ref[h]
            st_hi, st_lo = _split_bf16(state)
            sc = jnp.where(tril, _dot_nt(qt[:, sl], kt[:, sl]), 0.0).astype(BF16)
            o = _dot(qt[:, sl], st_hi) + _dot(qt[:, sl], st_lo) + _dot(sc, vh)
            ltot = _dot_tn(la_hi[:, sl], ones_bf) + _dot_tn(la_lo[:, sl], ones_bf)
            s_ref[h] = jnp.exp(ltot) * state + _dot_tn(kl[:, sl], vh)
            g = gr_ref[rows, vs]
            o_ref[rows, vs] = (_rms(o, gn_ref[...]) * (g * _sigmoid(g))).astype(o_ref.dtype)


def _gla(u, wg, bg, gn):
    nj = SEQ // TC_GLA
    row = lambda b, j: b * nj + j
    return pl.pallas_call(
        _gla_body,
        grid=(BATCH, nj),
        in_specs=[
            pl.BlockSpec((TC_GLA, GLA_QK), lambda b, j: (row(b, j), COL_GQ // GLA_QK)),
            pl.BlockSpec((TC_GLA, GLA_QK), lambda b, j: (row(b, j), COL_GK // GLA_QK)),
            pl.BlockSpec((TC_GLA, GLA_V), lambda b, j: (row(b, j), COL_GV // GLA_V)),
            pl.BlockSpec((TC_GLA, GLA_V), lambda b, j: (row(b, j), COL_GR // GLA_V)),
            pl.BlockSpec((TC_GLA, LANES), lambda b, j: (row(b, j), COL_GLR // LANES)),
            pl.BlockSpec((LANES, GLA_QK), lambda b, j: (0, 0)),
            pl.BlockSpec((1, GLA_QK), lambda b, j: (0, 0)),
            pl.BlockSpec((1, GLA_DV), lambda b, j: (0, 0)),
        ],
        out_specs=pl.BlockSpec((TC_GLA, GLA_V), lambda b, j: (row(b, j), 0)),
        out_shape=jax.ShapeDtypeStruct((TOKENS, GLA_V), BF16),
        scratch_shapes=[pltpu.VMEM((GLA_HEADS, GLA_DK, GLA_DV), F32)],
        compiler_params=_params(("arbitrary", "arbitrary")),
        name="gla",
    )(u, u, u, u, u, wg, bg, gn)


def _conv_body(cin_ref, cb_ref, cc_ref, w_ref, o_ref):
    uu = cc_ref[...] * cin_ref[...]
    t = lax.broadcasted_iota(jnp.int32, uu.shape, 0)
    y = uu * w_ref[CONV_WIDTH - 1:CONV_WIDTH, :]
    for shift in range(1, CONV_WIDTH):
        prev = jnp.where(t >= shift, pltpu.roll(uu, shift, axis=0), 0.0)
        y = y + prev * w_ref[CONV_WIDTH - 1 - shift:CONV_WIDTH - shift, :]
    o_ref[...] = (cb_ref[...] * y).astype(o_ref.dtype)


def _conv(u, w):
    spec = lambda col: pl.BlockSpec((SEQ, CONV_CH), lambda b: (b, col // CONV_CH))
    return pl.pallas_call(
        _conv_body,
        grid=(BATCH,),
        in_specs=[spec(COL_CIN), spec(COL_CB), spec(COL_CC),
                  pl.BlockSpec((8, CONV_CH), lambda b: (0, 0))],
        out_specs=pl.BlockSpec((SEQ, CONV_CH), lambda b: (b, 0)),
        out_shape=jax.ShapeDtypeStruct((TOKENS, CONV_CH), BF16),
        compiler_params=_params(("parallel",)),
        name="conv",
    )(u, u, u, w)


def _outproj_body(*refs, route):
    (o1, o4, o16, l1, l4, l16, go_ref, cv_ref, x_ref, w_ref, g_ref) = refs[:11]
    if route:
        wr_ref, xo_ref, ho_ref, route_ref, counts_ref, carry_ref = refs[11:]
    else:
        xo_ref, ho_ref = refs[11:]
    la, lb, lc = l1[...], l4[...], l16[...]
    m = jnp.maximum(jnp.maximum(la, lb), lc)
    ea, eb, ec = jnp.exp(la - m), jnp.exp(lb - m), jnp.exp(lc - m)
    att = (ea * o1[...].astype(F32) + eb * o4[...].astype(F32) + ec * o16[...].astype(F32)) / (ea + eb + ec)
    y = (x_ref[...]
         + _dot(att.astype(BF16), w_ref[0:ATT_WIDTH, :])
         + _dot(go_ref[...], w_ref[ATT_WIDTH:ATT_WIDTH + GLA_V, :])
         + _dot(cv_ref[...], w_ref[ATT_WIDTH + GLA_V:MIX_WIDTH, :]))
    xo_ref[...] = y
    hf = _rms(y, g_ref[...])
    ho_ref[...] = hf.astype(ho_ref.dtype)
    if route:
        @pl.when(pl.program_id(0) == 0)
        def _():
            carry_ref[...] = jnp.zeros_like(carry_ref)

        tm = hf.shape[0]
        logits = _dot3(hf, wr_ref[...])
        lane = lax.broadcasted_iota(jnp.int32, logits.shape, 1).astype(F32)
        lg = jnp.where(lane < N_EXPERTS, logits, -jnp.inf)
        v1 = jnp.max(lg, axis=-1, keepdims=True)
        i1 = jnp.min(jnp.where(lg == v1, lane, float(LANES)), axis=-1, keepdims=True)
        lg2 = jnp.where(lane == i1, -jnp.inf, lg)
        v2 = jnp.max(lg2, axis=-1, keepdims=True)
        i2 = jnp.min(jnp.where(lg2 == v2, lane, float(LANES)), axis=-1, keepdims=True)
        e2 = jnp.exp(v2 - v1)
        w1 = 1.0 / (1.0 + e2)
        w2 = e2 * w1
        sel1 = lane == i1
        sel2 = lane == i2
        onehot = jnp.where(sel1, 1.0, jnp.where(sel2, 1.0, 0.0))
        tri = (lax.broadcasted_iota(jnp.int32, (tm, tm), 0) >= lax.broadcasted_iota(jnp.int32, (tm, tm), 1))
        csum = _dot(jnp.where(tri, 1.0, 0.0).astype(BF16), onehot.astype(BF16))
        carry = carry_ref[0:1, :]
        rank = csum - onehot + carry
        r1 = jnp.sum(jnp.where(sel1, rank, 0.0), axis=-1, keepdims=True)
        r2 = jnp.sum(jnp.where(sel2, rank, 0.0), axis=-1, keepdims=True)
        total = jnp.broadcast_to(carry + csum[tm - 1:tm, :], carry_ref.shape)
        carry_ref[...] = total
        counts_ref[...] = total
        cols = {ROUTE_I1: i1, ROUTE_I2: i2, ROUTE_W1: w1, ROUTE_W2: w2, ROUTE_R1: r1, ROUTE_R2: r2}
        packed = jnp.zeros_like(logits)
        for c, val in cols.items():
            packed = jnp.where(lane == float(c), val, packed)
        route_ref[...] = packed


def _outproj(att, go, cv, x, w, g, w_router=None):
    route = w_router is not None
    tm = TM_PROJ
    tile = lambda cols: pl.BlockSpec((tm, cols), lambda i: (i, 0))
    full = lambda a: pl.BlockSpec(a.shape, lambda i: (0, 0))
    (o1, l1), (o4, l4), (o16, l16) = att
    args = [o1, o4, o16, l1, l4, l16, go, cv, x, w, g]
    in_specs = [tile(ATT_WIDTH)] * 6 + [tile(GLA_V), tile(CONV_CH), tile(D_MODEL), full(w), full(g)]
    out_specs = [tile(D_MODEL), tile(D_MODEL)]
    out_shape = [jax.ShapeDtypeStruct((TOKENS, D_MODEL), F32),
                 jax.ShapeDtypeStruct((TOKENS, D_MODEL), F32 if route else BF16)]
    scratch = []
    if route:
        args.append(w_router)
        in_specs.append(full(w_router))
        out_specs += [tile(LANES), pl.BlockSpec((8, LANES), lambda i: (0, 0))]
        out_shape += [jax.ShapeDtypeStruct((TOKENS, LANES), F32), jax.ShapeDtypeStruct((8, LANES), F32)]
        scratch = [pltpu.VMEM((8, LANES), F32)]
    return pl.pallas_call(
        functools.partial(_outproj_body, route=route),
        grid=(TOKENS // tm,),
        in_specs=in_specs,
        out_specs=out_specs,
        out_shape=out_shape,
        scratch_shapes=scratch,
        compiler_params=_params(("arbitrary",) if route else ("parallel",)),
        name="outproj_route" if route else "outproj",
    )(*args)


def _swiglu_accumulate(h, w1_ref, w3_ref, w2_ref, acc_ref, tf):
    for c0 in range(0, tf, FFN_SUB):
        c1 = min(c0 + FFN_SUB, tf)
        a = _dot(h, w1_ref[:, c0:c1])
        b = _dot(h, w3_ref[:, c0:c1])
        act = a * _sigmoid(a) * b
        acc_ref[...] += _dot(act.astype(BF16), w2_ref[c0:c1, :])


def _ffn_body(x_ref, h_ref, w1_ref, w3_ref, w2_ref, o_ref, *, tf):
    @pl.when(pl.program_id(1) == 0)
    def _():
        o_ref[...] = x_ref[...]

    _swiglu_accumulate(h_ref[...], w1_ref, w3_ref, w2_ref, o_ref, tf)


def _ffn(x, h, w1, w3, w2, *, tf):
    f = w1.shape[1]
    tm = TM_FFN
    tile = lambda cols: pl.BlockSpec((tm, cols), lambda i, j: (i, 0))
    return pl.pallas_call(
        functools.partial(_ffn_body, tf=tf),
        grid=(TOKENS // tm, f // tf),
        in_specs=[tile(D_MODEL), tile(D_MODEL),
                  pl.BlockSpec((D_MODEL, tf), lambda i, j: (0, j)),
                  pl.BlockSpec((D_MODEL, tf), lambda i, j: (0, j)),
                  pl.BlockSpec((tf, D_MODEL), lambda i, j: (j, 0))],
        out_specs=tile(D_MODEL),
        out_shape=jax.ShapeDtypeStruct((TOKENS, D_MODEL), F32),
        compiler_params=_params(("parallel", "arbitrary")),
        name="dense_ffn",
    )(x, h, w1, w3, w2)


def _dispatch_body(pos_ref, h_hbm, xs_init, xs_hbm, sem):
    del xs_init
    rows_per_chunk = 2 * DISPATCH_CHUNK

    def row_copy(tok, k):
        return pltpu.make_async_copy(h_hbm.at[pl.ds(tok, 1)], xs_hbm.at[pl.ds(pos_ref[2 * tok + k], 1)], sem)

    def issue(c):
        def body(t, carry):
            tok = c * DISPATCH_CHUNK + t
            row_copy(tok, 0).start()
            row_copy(tok, 1).start()
            return carry
        lax.fori_loop(0, DISPATCH_CHUNK, body, 0, unroll=8)

    def drain():
        pltpu.make_async_copy(h_hbm.at[pl.ds(0, rows_per_chunk)], xs_hbm.at[pl.ds(0, rows_per_chunk)], sem).wait()

    issue(0)

    def step(c, carry):
        issue(c)
        drain()
        return carry
    lax.fori_loop(1, TOKENS // DISPATCH_CHUNK, step, 0)
    drain()


def _dispatch(pos, h):
    xs_init = jnp.zeros((N_SORTED, D_MODEL), F32)
    return pl.pallas_call(
        _dispatch_body,
        in_specs=[pl.BlockSpec(memory_space=pltpu.SMEM),
                  pl.BlockSpec(memory_space=pl.ANY),
                  pl.BlockSpec(memory_space=pl.ANY)],
        out_specs=pl.BlockSpec(memory_space=pl.ANY),
        out_shape=jax.ShapeDtypeStruct((N_SORTED, D_MODEL), F32),
        scratch_shapes=[pltpu.SemaphoreType.DMA(())],
        input_output_aliases={2: 0},
        name="moe_dispatch",
    )(pos, h, xs_init)


def _gffn_body(te_ref, nu_ref, xs_ref, w1_ref, w3_ref, w2_ref, o_ref, hb_ref):
    del te_ref
    i = pl.program_id(0)
    j = pl.program_id(1)

    @pl.when(j == 0)
    def _():
        hb_ref[...] = xs_ref[...].astype(BF16)
        o_ref[...] = jnp.zeros_like(o_ref)

    @pl.when(i < nu_ref[0])
    def _():
        _swiglu_accumulate(hb_ref[...], w1_ref, w3_ref, w2_ref, o_ref, TF_MOE)


def _grouped_ffn(tile_expert, n_used, xs, w1, w3, w2):
    nj = FFN_EXPERT // TF_MOE
    col = lambda i, j, nu: jnp.where(i < nu[0], j, nj - 1)
    grid_spec = pltpu.PrefetchScalarGridSpec(
        num_scalar_prefetch=2,
        grid=(N_TILES_MOE, nj),
        in_specs=[
            pl.BlockSpec((TM_MOE, D_MODEL), lambda i, j, te, nu: (i, 0)),
            pl.BlockSpec((None, D_MODEL, TF_MOE), lambda i, j, te, nu: (te[i], 0, col(i, j, nu))),
            pl.BlockSpec((None, D_MODEL, TF_MOE), lambda i, j, te, nu: (te[i], 0, col(i, j, nu))),
            pl.BlockSpec((None, TF_MOE, D_MODEL), lambda i, j, te, nu: (te[i], col(i, j, nu), 0)),
        ],
        out_specs=pl.BlockSpec((TM_MOE, D_MODEL), lambda i, j, te, nu: (i, 0)),
        scratch_shapes=[pltpu.VMEM((TM_MOE, D_MODEL), BF16)],
    )
    return pl.pallas_call(
        _gffn_body,
        grid_spec=grid_spec,
        out_shape=jax.ShapeDtypeStruct((N_SORTED, D_MODEL), F32),
        compiler_params=_params(("arbitrary", "arbitrary")),
        name="moe_ffn",
    )(tile_expert, n_used, xs, w1, w3, w2)


def _combine_body(pos_ref, x_ref, route_ref, g_ref, ys_hbm, o_ref, buf_ref, sem):
    i = pl.program_id(0)
    n = pl.num_programs(0)
    tm = TM_COMBINE

    def issue(tile, slot):
        def body(t, carry):
            for k in range(2):
                src = pos_ref[2 * (tile * tm + t) + k]
                pltpu.make_async_copy(ys_hbm.at[pl.ds(src, 1)], buf_ref.at[slot, k, pl.ds(t, 1)],
                                      sem.at[slot]).start()
            return carry
        lax.fori_loop(0, tm, body, 0, unroll=8)

    @pl.when(i == 0)
    def _():
        issue(0, 0)

    @pl.when(i + 1 < n)
    def _():
        issue(i + 1, (i + 1) % 2)

    slot = i % 2
    for k in range(2):
        pltpu.make_async_copy(ys_hbm.at[pl.ds(0, tm)], buf_ref.at[slot, k], sem.at[slot]).wait()
    r = route_ref[...]
    lane = lax.broadcasted_iota(jnp.int32, r.shape, 1)
    w1 = jnp.sum(jnp.where(lane == ROUTE_W1, r, 0.0), axis=-1, keepdims=True)
    w2 = jnp.sum(jnp.where(lane == ROUTE_W2, r, 0.0), axis=-1, keepdims=True)
    y = x_ref[...] + w1 * buf_ref[slot, 0] + w2 * buf_ref[slot, 1]
    o_ref[...] = _rms(y, g_ref[...])


def _combine(pos, x, route, g, ys):
    tm = TM_COMBINE
    return pl.pallas_call(
        _combine_body,
        grid=(TOKENS // tm,),
        in_specs=[pl.BlockSpec(memory_space=pltpu.SMEM),
                  pl.BlockSpec((tm, D_MODEL), lambda i: (i, 0)),
                  pl.BlockSpec((tm, LANES), lambda i: (i, 0)),
                  pl.BlockSpec((1, D_MODEL), lambda i: (0, 0)),
                  pl.BlockSpec(memory_space=pl.ANY)],
        out_specs=pl.BlockSpec((tm, D_MODEL), lambda i: (i, 0)),
        out_shape=jax.ShapeDtypeStruct((TOKENS, D_MODEL), F32),
        scratch_shapes=[pltpu.VMEM((2, 2, tm, D_MODEL), F32), pltpu.SemaphoreType.DMA((2,))],
        compiler_params=_params(("arbitrary",)),
        name="moe_combine",
    )(pos, x, route, g, ys)


def _routing_tables(route, counts):
    cnt = counts[0, :N_EXPERTS].astype(jnp.int32)
    tiles = (cnt + TM_MOE - 1) // TM_MOE
    tile_end = jnp.cumsum(tiles)
    tile_start = tile_end - tiles
    n_used = tile_end[-1]
    expert = route[:, ROUTE_I1:ROUTE_I2 + 1].astype(jnp.int32)
    rank = route[:, ROUTE_R1:ROUTE_R2 + 1].astype(jnp.int32)
    pos = (tile_start * TM_MOE)[expert] + rank
    tile_id = jnp.minimum(jnp.arange(N_TILES_MOE, dtype=jnp.int32), n_used - 1)
    tile_expert = jnp.sum(tile_id[:, None] >= tile_end[None, :], axis=1).astype(jnp.int32)
    return pos.reshape(2 * TOKENS), tile_expert, n_used.reshape(1)


def _moe(x, h, route, counts, w1, w3, w2, g_final):
    pos, tile_expert, n_used = _routing_tables(route, counts)
    xs = _dispatch(pos, h)
    ys = _grouped_ffn(tile_expert, n_used, xs, w1, w3, w2)
    return _combine(pos, x, route, g_final, ys)


def _prep_w_in(w):
    aq, ak, av, gq, gk, gv, gr, glr, c_in, c_b, c_c = jnp.split(w, np.cumsum(SPLIT_SIZES)[:-1].tolist(), axis=1)
    pad = jnp.zeros((D_MODEL, LANES - GLA_RANK), w.dtype)
    return jnp.concatenate([aq, ak, av, gq, gv, gr, gk, c_in, c_b, c_c, glr, pad], axis=1).astype(BF16)


def kernel(x, w_mix_in, w_mix_out, g_mix, rel_bias, gla_w_gate, gla_b_gate, gla_g_norm, conv_w,
           g_ffn, ffn_w1, ffn_w3, ffn_w2, moe_router, moe_w1, moe_w3, moe_w2, g_final):
    assert DEPTH == 2
    x = x.reshape(TOKENS, D_MODEL)
    for layer in range(DEPTH):
        u = _inproj(x, g_mix[layer].reshape(1, D_MODEL), _prep_w_in(w_mix_in[layer]))
        att = _attention(u, rel_bias)
        wg = jnp.pad(gla_w_gate[layer], ((0, LANES - GLA_RANK), (0, 0)))
        go = _gla(u, wg, gla_b_gate[layer].reshape(1, GLA_QK), gla_g_norm[layer].reshape(1, GLA_DV))
        cv = _conv(u, jnp.pad(conv_w[layer], ((0, 8 - CONV_WIDTH), (0, 0))))
        w_out = w_mix_out[layer].astype(BF16)
        g2 = g_ffn[layer].reshape(1, D_MODEL)
        i = layer // 2
        if layer % 2 == 0:
            x, h = _outproj(att, go, cv, x, w_out, g2)
            x = _ffn(x, h, ffn_w1[i].astype(BF16), ffn_w3[i].astype(BF16), ffn_w2[i].astype(BF16), tf=TF_DENSE)
        else:
            wr = jnp.pad(moe_router[i], ((0, 0), (0, LANES - N_EXPERTS)))
            x, h, route, counts = _outproj(att, go, cv, x, w_out, g2, wr)
            x = _moe(x, h, route, counts, moe_w1[i].astype(BF16), moe_w3[i].astype(BF16),
                     moe_w2[i].astype(BF16), g_final.reshape(1, D_MODEL))
    return x.reshape(BATCH, SEQ, D_MODEL)
```

```python
import functools
import math

import jax
import jax.numpy as jnp
import numpy as np
from jax import lax
from jax.experimental import pallas as pl
from jax.experimental.pallas import tpu as pltpu

F32 = jnp.float32
BF16 = jnp.bfloat16

D_MODEL = 1024
BATCH = 8
SEQ = 2048
TOKENS = BATCH * SEQ
DEPTH = 2
EPS = 1e-6

HEAD_DIM = 64
ATT_HEADS = 4
ATT_WIDTH = ATT_HEADS * HEAD_DIM
DILATED_PATTERNS = ((128, 1), (512, 4), (2048, 16))
ATT_BLOCK = 128
REL_BUCKETS = 32
REL_MAX_DISTANCE = 2048

GLA_HEADS = 4
GLA_DK = 64
GLA_DV = 128
GLA_RANK = 16
GLA_CHUNK = 64
GLA_QK = GLA_HEADS * GLA_DK
GLA_V = GLA_HEADS * GLA_DV

CONV_CH = 256
CONV_WIDTH = 3
MIX_WIDTH = ATT_WIDTH + GLA_V + CONV_CH

SPLIT_SIZES = (ATT_WIDTH, ATT_WIDTH, ATT_WIDTH, GLA_QK, GLA_QK, GLA_V, GLA_V, GLA_RANK,
               CONV_CH, CONV_CH, CONV_CH)

FFN_DENSE = 2816
N_EXPERTS = 8
FFN_EXPERT = 3584

LANES = 128
MXU_WIDTH = 256
VMEM_LIMIT = 56 * 1024 * 1024

COL_AQ, COL_AK, COL_AV = 0, 256, 512
COL_GQ, COL_GV, COL_GR, COL_GK = 768, 1024, 1536, 2048
COL_CIN, COL_CB, COL_CC, COL_GLR = 2304, 2560, 2816, 3072
U_COLS = COL_GLR + LANES

NEG_BIG = -1e30

TM_PROJ = 512
TC_GLA = 512
TM_FFN = 1024
TF_DENSE = 1408
FFN_SUB = 512
TM_MOE = 512
TF_MOE = 512
N_TILES_MOE = 2 * TOKENS // TM_MOE + N_EXPERTS
N_SORTED = N_TILES_MOE * TM_MOE
DISPATCH_CHUNK = 256
TM_COMBINE = 256

ROUTE_I1, ROUTE_I2, ROUTE_W1, ROUTE_W2, ROUTE_R1, ROUTE_R2 = range(6)


def _params(sem):
    return pltpu.CompilerParams(dimension_semantics=sem, vmem_limit_bytes=VMEM_LIMIT)


def _split_bf16(a):
    hi = a.astype(BF16)
    lo = (a - hi.astype(F32)).astype(BF16)
    return hi, lo


def _dot(a, b):
    return jnp.dot(a, b, preferred_element_type=F32)


def _dot3(a, b):
    a_hi, a_lo = _split_bf16(a)
    b_hi, b_lo = _split_bf16(b)
    return _dot(a_hi, b_hi) + _dot(a_lo, b_hi) + _dot(a_hi, b_lo)


def _dot_nt(a, b):
    return lax.dot_general(a, b, (((1,), (1,)), ((), ())), preferred_element_type=F32)


def _dot_tn(a, b):
    return lax.dot_general(a, b, (((0,), (0,)), ((), ())), preferred_element_type=F32)


def _rms(x, g):
    ms = jnp.mean(x * x, axis=-1, keepdims=True)
    return x * lax.rsqrt(ms + EPS) * g


def _sigmoid(x):
    return 1.0 / (1.0 + jnp.exp(-x))


def _inproj_body(x_ref, g_ref, w_ref, o_ref):
    h = _rms(x_ref[...], g_ref[...]).astype(BF16)
    for c0 in range(0, U_COLS, MXU_WIDTH):
        c1 = min(c0 + MXU_WIDTH, U_COLS)
        o_ref[:, c0:c1] = _dot(h, w_ref[:, c0:c1])


def _inproj(x, g, w):
    return pl.pallas_call(
        _inproj_body,
        grid=(TOKENS // TM_PROJ,),
        in_specs=[
            pl.BlockSpec((TM_PROJ, D_MODEL), lambda i: (i, 0)),
            pl.BlockSpec((1, D_MODEL), lambda i: (0, 0)),
            pl.BlockSpec((D_MODEL, U_COLS), lambda i: (0, 0)),
        ],
        out_specs=pl.BlockSpec((TM_PROJ, U_COLS), lambda i: (i, 0)),
        out_shape=jax.ShapeDtypeStruct((TOKENS, U_COLS), F32),
        compiler_params=_params(("parallel",)),
        name="inproj",
    )(x, g, w)


def _rel_bucket(dist):
    max_exact = REL_BUCKETS // 2
    d = jnp.maximum(dist, 0)
    log_ratio = jnp.log(jnp.maximum(d, 1).astype(F32) / max_exact) / math.log(REL_MAX_DISTANCE / max_exact)
    large = jnp.minimum(max_exact + (log_ratio * (REL_BUCKETS - max_exact)).astype(jnp.int32), REL_BUCKETS - 1)
    return jnp.where(d < max_exact, d, large)


def _bucket_table(window, dilation):
    span = window // dilation
    qi = jnp.arange(ATT_BLOCK)[:, None]
    kj = jnp.arange(2 * ATT_BLOCK)[None, :]
    sub_dist = qi - kj + ATT_BLOCK
    band = (sub_dist >= 0) & (sub_dist <= span)
    return jnp.where(band, _rel_bucket(sub_dist * dilation), -1).astype(jnp.int32)


def _attn_body(rb_ref, bidx_ref, q_ref, k_ref, v_ref, o_ref, lse_ref, bias_ref, *, nblk):
    @pl.when((pl.program_id(0) == 0) & (pl.program_id(1) == 0))
    def _():
        bidx = bidx_ref[...]
        for h in range(ATT_HEADS):
            acc = jnp.full(bidx.shape, NEG_BIG, F32)
            for b in range(REL_BUCKETS):
                acc = jnp.where(bidx == b, rb_ref[b, h], acc)
            bias_ref[h] = acc

    def block(n, first):
        if first:
            q = q_ref[0:ATT_BLOCK, :]
            kk = k_ref[0:ATT_BLOCK, :]
            vv = v_ref[0:ATT_BLOCK, :]
            rows = slice(0, ATT_BLOCK)
        else:
            r0 = pl.multiple_of(n * ATT_BLOCK, ATT_BLOCK)
            rows = pl.ds(r0, ATT_BLOCK)
            q = q_ref[rows, :]
            kk = k_ref[pl.ds(r0 - ATT_BLOCK, 2 * ATT_BLOCK), :]
            vv = v_ref[pl.ds(r0 - ATT_BLOCK, 2 * ATT_BLOCK), :]
        outs, lses = [], []
        for h in range(ATT_HEADS):
            sl = slice(h * HEAD_DIM, (h + 1) * HEAD_DIM)
            qh = q[:, sl] * jnp.asarray(HEAD_DIM ** -0.5, BF16)
            bias = bias_ref[h, :, ATT_BLOCK:] if first else bias_ref[h]
            s = _dot_nt(qh, kk[:, sl]) + bias
            m = jnp.max(s, axis=-1, keepdims=True)
            p = jnp.exp(s - m)
            l = jnp.sum(p, axis=-1, keepdims=True)
            o = _dot(p.astype(BF16), vv[:, sl]) * (1.0 / l)
            outs.append(o)
            lses.append(jnp.broadcast_to(m + jnp.log(l), (ATT_BLOCK, HEAD_DIM)))
        o_ref[rows, :] = jnp.concatenate(outs, axis=-1).astype(o_ref.dtype)
        lse_ref[rows, :] = jnp.concatenate(lses, axis=-1)

    block(0, True)
    if nblk > 1:
        def loop_body(n, carry):
            block(n, False)
            return carry
        lax.fori_loop(1, nblk, loop_body, 0)


def _attention_pattern(ua, rel_bias, window, dilation):
    L = SEQ // dilation
    nblk = L // ATT_BLOCK
    qkv_spec = lambda j: pl.BlockSpec((None, None, L, ATT_WIDTH), lambda b, r: (b, r, 0, j))
    out_spec = pl.BlockSpec((None, None, L, ATT_WIDTH), lambda b, r: (b, r, 0, 0))
    return pl.pallas_call(
        functools.partial(_attn_body, nblk=nblk),
        grid=(BATCH, dilation),
        in_specs=[
            pl.BlockSpec(memory_space=pltpu.SMEM),
            pl.BlockSpec((ATT_BLOCK, 2 * ATT_BLOCK), lambda b, r: (0, 0)),
            qkv_spec(0), qkv_spec(1), qkv_spec(2),
        ],
        out_specs=[out_spec, out_spec],
        out_shape=[jax.ShapeDtypeStruct((BATCH, dilation, L, ATT_WIDTH), BF16),
                   jax.ShapeDtypeStruct((BATCH, dilation, L, ATT_WIDTH), F32)],
        scratch_shapes=[pltpu.VMEM((ATT_HEADS, ATT_BLOCK, 2 * ATT_BLOCK), F32)],
        compiler_params=_params(("arbitrary", "arbitrary")),
        name=f"attn_d{dilation}",
    )(rel_bias, _bucket_table(window, dilation), ua, ua, ua)


def _attention(u, rel_bias):
    res = []
    qkv = u[:, :3 * ATT_WIDTH].astype(BF16)
    for window, dilation in DILATED_PATTERNS:
        L = SEQ // dilation
        ua = qkv.reshape(BATCH, L, dilation, 3 * ATT_WIDTH).transpose(0, 2, 1, 3)
        o, lse = _attention_pattern(ua, rel_bias, window, dilation)
        o = o.transpose(0, 2, 1, 3).reshape(TOKENS, ATT_WIDTH)
        lse = lse.transpose(0, 2, 1, 3).reshape(TOKENS, ATT_WIDTH)
        res.append((o, lse))
    return res


def _gla_body(q_ref, k_ref, v_ref, gr_ref, glr_ref, wg_ref, bg_ref, gn_ref, o_ref, s_ref):
    @pl.when(pl.program_id(1) == 0)
    def _():
        s_ref[...] = jnp.zeros_like(s_ref)

    C = GLA_CHUNK
    row = lax.broadcasted_iota(jnp.int32, (C, C), 0)
    col = lax.broadcasted_iota(jnp.int32, (C, C), 1)
    tril = row >= col
    tril_bf = jnp.where(tril, 1.0, 0.0).astype(BF16)
    ones_bf = jnp.ones((C, GLA_DV), BF16)

    xg = _dot3(glr_ref[...], wg_ref[...]) + bg_ref[...]
    la_all = (jnp.minimum(xg, 0.0) - jnp.log(1.0 + jnp.exp(-jnp.abs(xg)))) * (1.0 / 16.0)

    for c in range(TC_GLA // C):
        rows = slice(c * C, (c + 1) * C)
        la_hi, la_lo = _split_bf16(la_all[rows])
        cum = _dot(tril_bf, la_hi) + _dot(tril_bf, la_lo)
        last = cum[C - 1:C, :]
        q = q_ref[rows, :]
        k = k_ref[rows, :]
        qt = (q * jnp.exp(cum) * (GLA_DK ** -0.5)).astype(BF16)
        kt = (k * jnp.exp(-cum)).astype(BF16)
        kl = (k * jnp.exp(last - cum)).astype(BF16)
        for h in range(GLA_HEADS):
            sl = slice(h * GLA_DK, (h + 1) * GLA_DK)
            vs = slice(h * GLA_DV, (h + 1) * GLA_DV)
            vh = v_ref[rows, vs].astype(BF16)
            state = s_ref[h]
            st_hi, st_lo = _split_bf16(state)
            sc = jnp.where(tril, _dot_nt(qt[:, sl], kt[:, sl]), 0.0).astype(BF16)
            o = _dot(qt[:, sl], st_hi) + _dot(qt[:, sl], st_lo) + _dot(sc, vh)
            ltot = _dot_tn(la_hi[:, sl], ones_bf) + _dot_tn(la_lo[:, sl], ones_bf)
            s_ref[h] = jnp.exp(ltot) * state + _dot_tn(kl[:, sl], vh)
            g = gr_ref[rows, vs]
            o_ref[rows, vs] = (_rms(o, gn_ref[...]) * (g * _sigmoid(g))).astype(o_ref.dtype)


def _gla(u, wg, bg, gn):
    nj = SEQ // TC_GLA
    row = lambda b, j: b * nj + j
    return pl.pallas_call(
        _gla_body,
        grid=(BATCH, nj),
        in_specs=[
            pl.BlockSpec((TC_GLA, GLA_QK), lambda b, j: (row(b, j), COL_GQ // GLA_QK)),
            pl.BlockSpec((TC_GLA, GLA_QK), lambda b, j: (row(b, j), COL_GK // GLA_QK)),
            pl.BlockSpec((TC_GLA, GLA_V), lambda b, j: (row(b, j), COL_GV // GLA_V)),
            pl.BlockSpec((TC_GLA, GLA_V), lambda b, j: (row(b, j), COL_GR // GLA_V)),
            pl.BlockSpec((TC_GLA, LANES), lambda b, j: (row(b, j), COL_GLR // LANES)),
            pl.BlockSpec((LANES, GLA_QK), lambda b, j: (0, 0)),
            pl.BlockSpec((1, GLA_QK), lambda b, j: (0, 0)),
            pl.BlockSpec((1, GLA_DV), lambda b, j: (0, 0)),
        ],
        out_specs=pl.BlockSpec((TC_GLA, GLA_V), lambda b, j: (row(b, j), 0)),
        out_shape=jax.ShapeDtypeStruct((TOKENS, GLA_V), BF16),
        scratch_shapes=[pltpu.VMEM((GLA_HEADS, GLA_DK, GLA_DV), F32)],
        compiler_params=_params(("arbitrary", "arbitrary")),
        name="gla",
    )(u, u, u, u, u, wg, bg, gn)


def _conv_body(cin_ref, cb_ref, cc_ref, w_ref, o_ref):
    uu = cc_ref[...] * cin_ref[...]
    t = lax.broadcasted_iota(jnp.int32, uu.shape, 0)
    y = uu * w_ref[CONV_WIDTH - 1:CONV_WIDTH, :]
    for shift in range(1, CONV_WIDTH):
        prev = jnp.where(t >= shift, pltpu.roll(uu, shift, axis=0), 0.0)
        y = y + prev * w_ref[CONV_WIDTH - 1 - shift:CONV_WIDTH - shift, :]
    o_ref[...] = (cb_ref[...] * y).astype(o_ref.dtype)


def _conv(u, w):
    spec = lambda col: pl.BlockSpec((SEQ, CONV_CH), lambda b: (b, col // CONV_CH))
    return pl.pallas_call(
        _conv_body,
        grid=(BATCH,),
        in_specs=[spec(COL_CIN), spec(COL_CB), spec(COL_CC),
                  pl.BlockSpec((8, CONV_CH), lambda b: (0, 0))],
        out_specs=pl.BlockSpec((SEQ, CONV_CH), lambda b: (b, 0)),
        out_shape=jax.ShapeDtypeStruct((TOKENS, CONV_CH), BF16),
        compiler_params=_params(("parallel",)),
        name="conv",
    )(u, u, u, w)


def _outproj_body(*refs, route):
    (o1, o4, o16, l1, l4, l16, go_ref, cv_ref, x_ref, w_ref, g_ref) = refs[:11]
    if route:
        wr_ref, xo_ref, ho_ref, route_ref, counts_ref, carry_ref = refs[11:]
    else:
        xo_ref, ho_ref = refs[11:]
    la, lb, lc = l1[...], l4[...], l16[...]
    m = jnp.maximum(jnp.maximum(la, lb), lc)
    ea, eb, ec = jnp.exp(la - m), jnp.exp(lb - m), jnp.exp(lc - m)
    att = (ea * o1[...].astype(F32) + eb * o4[...].astype(F32) + ec * o16[...].astype(F32)) / (ea + eb + ec)
    y = (x_ref[...]
         + _dot(att.astype(BF16), w_ref[0:ATT_WIDTH, :])
         + _dot(go_ref[...], w_ref[ATT_WIDTH:ATT_WIDTH + GLA_V, :])
         + _dot(cv_ref[...], w_ref[ATT_WIDTH + GLA_V:MIX_WIDTH, :]))
    xo_ref[...] = y
    hf = _rms(y, g_ref[...])
    ho_ref[...] = hf.astype(ho_ref.dtype)
    if route:
        @pl.when(pl.program_id(0) == 0)
        def _():
            carry_ref[...] = jnp.zeros_like(carry_ref)

        tm = hf.shape[0]
        logits = _dot3(hf, wr_ref[...])
        lane = lax.broadcasted_iota(jnp.int32, logits.shape, 1).astype(F32)
        lg = jnp.where(lane < N_EXPERTS, logits, -jnp.inf)
        v1 = jnp.max(lg, axis=-1, keepdims=True)
        i1 = jnp.min(jnp.where(lg == v1, lane, float(LANES)), axis=-1, keepdims=True)
        lg2 = jnp.where(lane == i1, -jnp.inf, lg)
        v2 = jnp.max(lg2, axis=-1, keepdims=True)
        i2 = jnp.min(jnp.where(lg2 == v2, lane, float(LANES)), axis=-1, keepdims=True)
        e2 = jnp.exp(v2 - v1)
        w1 = 1.0 / (1.0 + e2)
        w2 = e2 * w1
        sel1 = lane == i1
        sel2 = lane == i2
        onehot = jnp.where(sel1, 1.0, jnp.where(sel2, 1.0, 0.0))
        tri = (lax.broadcasted_iota(jnp.int32, (tm, tm), 0) >= lax.broadcasted_iota(jnp.int32, (tm, tm), 1))
        csum = _dot(jnp.where(tri, 1.0, 0.0).astype(BF16), onehot.astype(BF16))
        carry = carry_ref[0:1, :]
        rank = csum - onehot + carry
        r1 = jnp.sum(jnp.where(sel1, rank, 0.0), axis=-1, keepdims=True)
        r2 = jnp.sum(jnp.where(sel2, rank, 0.0), axis=-1, keepdims=True)
        total = jnp.broadcast_to(carry + csum[tm - 1:tm, :], carry_ref.shape)
        carry_ref[...] = total
        counts_ref[...] = total
        cols = {ROUTE_I1: i1, ROUTE_I2: i2, ROUTE_W1: w1, ROUTE_W2: w2, ROUTE_R1: r1, ROUTE_R2: r2}
        packed = jnp.zeros_like(logits)
        for c, val in cols.items():
            packed = jnp.where(lane == float(c), val, packed)
        route_ref[...] = packed


def _outproj(att, go, cv, x, w, g, w_router=None):
    route = w_router is not None
    tm = TM_PROJ
    tile = lambda cols: pl.BlockSpec((tm, cols), lambda i: (i, 0))
    full = lambda a: pl.BlockSpec(a.shape, lambda i: (0, 0))
    (o1, l1), (o4, l4), (o16, l16) = att
    args = [o1, o4, o16, l1, l4, l16, go, cv, x, w, g]
    in_specs = [tile(ATT_WIDTH)] * 6 + [tile(GLA_V), tile(CONV_CH), tile(D_MODEL), full(w), full(g)]
    out_specs = [tile(D_MODEL), tile(D_MODEL)]
    out_shape = [jax.ShapeDtypeStruct((TOKENS, D_MODEL), F32),
                 jax.ShapeDtypeStruct((TOKENS, D_MODEL), F32 if route else BF16)]
    scratch = []
    if route:
        args.append(w_router)
        in_specs.append(full(w_router))
        out_specs += [tile(LANES), pl.BlockSpec((8, LANES), lambda i: (0, 0))]
        out_shape += [jax.ShapeDtypeStruct((TOKENS, LANES), F32), jax.ShapeDtypeStruct((8, LANES), F32)]
        scratch = [pltpu.VMEM((8, LANES), F32)]
    return pl.pallas_call(
        functools.partial(_outproj_body, route=route),
        grid=(TOKENS // tm,),
        in_specs=in_specs,
        out_specs=out_specs,
        out_shape=out_shape,
        scratch_shapes=scratch,
        compiler_params=_params(("arbitrary",) if route else ("parallel",)),
        name="outproj_route" if route else "outproj",
    )(*args)


def _swiglu_accumulate(h, w1_ref, w3_ref, w2_ref, acc_ref, tf):
    for c0 in range(0, tf, FFN_SUB):
        c1 = min(c0 + FFN_SUB, tf)
        a = _dot(h, w1_ref[:, c0:c1])
        b = _dot(h, w3_ref[:, c0:c1])
        act = a * _sigmoid(a) * b
        acc_ref[...] += _dot(act.astype(BF16), w2_ref[c0:c1, :])


def _ffn_body(x_ref, h_ref, w1_ref, w3_ref, w2_ref, o_ref, *, tf):
    @pl.when(pl.program_id(1) == 0)
    def _():
        o_ref[...] = x_ref[...]

    _swiglu_accumulate(h_ref[...], w1_ref, w3_ref, w2_ref, o_ref, tf)


def _ffn(x, h, w1, w3, w2, *, tf):
    f = w1.shape[1]
    tm = TM_FFN
    tile = lambda cols: pl.BlockSpec((tm, cols), lambda i, j: (i, 0))
    return pl.pallas_call(
        functools.partial(_ffn_body, tf=tf),
        grid=(TOKENS // tm, f // tf),
        in_specs=[tile(D_MODEL), tile(D_MODEL),
                  pl.BlockSpec((D_MODEL, tf), lambda i, j: (0, j)),
                  pl.BlockSpec((D_MODEL, tf), lambda i, j: (0, j)),
                  pl.BlockSpec((tf, D_MODEL), lambda i, j: (j, 0))],
        out_specs=tile(D_MODEL),
        out_shape=jax.ShapeDtypeStruct((TOKENS, D_MODEL), F32),
        compiler_params=_params(("parallel", "arbitrary")),
        name="dense_ffn",
    )(x, h, w1, w3, w2)


def _dispatch_body(pos_ref, h_ref, xs_init, xs_hbm, stage_ref, sem):
    del xs_init
    i = pl.program_id(0)
    n = pl.num_programs(0)
    tm = DISPATCH_CHUNK
    slot = i % 2

    def drain(s):
        for _ in range(2):
            pltpu.make_async_copy(stage_ref.at[s], xs_hbm.at[pl.ds(0, tm)], sem.at[s]).wait()

    @pl.when(i >= 2)
    def _():
        drain(slot)

    stage_ref[slot] = h_ref[...]

    def body(t, carry):
        for k in range(2):
            dst = pos_ref[2 * (i * tm + t) + k]
            pltpu.make_async_copy(stage_ref.at[slot, pl.ds(t, 1)], xs_hbm.at[pl.ds(dst, 1)], sem.at[slot]).start()
        return carry
    lax.fori_loop(0, tm, body, 0, unroll=8)

    @pl.when(i == n - 1)
    def _():
        drain(1 - slot)
        drain(slot)


def _dispatch(pos, h):
    xs_init = jnp.zeros((N_SORTED, D_MODEL), F32)
    tm = DISPATCH_CHUNK
    return pl.pallas_call(
        _dispatch_body,
        grid=(TOKENS // tm,),
        in_specs=[pl.BlockSpec(memory_space=pltpu.SMEM),
                  pl.BlockSpec((tm, D_MODEL), lambda i: (i, 0)),
                  pl.BlockSpec(memory_space=pl.ANY)],
        out_specs=pl.BlockSpec(memory_space=pl.ANY),
        out_shape=jax.ShapeDtypeStruct((N_SORTED, D_MODEL), F32),
        scratch_shapes=[pltpu.VMEM((2, tm, D_MODEL), F32), pltpu.SemaphoreType.DMA((2,))],
        input_output_aliases={2: 0},
        compiler_params=_params(("arbitrary",)),
        name="moe_dispatch",
    )(pos, h, xs_init)


def _gffn_body(te_ref, nu_ref, xs_ref, w1_ref, w3_ref, w2_ref, o_ref, hb_ref):
    del te_ref
    i = pl.program_id(0)
    j = pl.program_id(1)

    @pl.when(j == 0)
    def _():
        hb_ref[...] = xs_ref[...].astype(BF16)
        o_ref[...] = jnp.zeros_like(o_ref)

    @pl.when(i < nu_ref[0])
    def _():
        _swiglu_accumulate(hb_ref[...], w1_ref, w3_ref, w2_ref, o_ref, TF_MOE)


def _grouped_ffn(tile_expert, n_used, xs, w1, w3, w2):
    nj = FFN_EXPERT // TF_MOE
    col = lambda i, j, nu: jnp.where(i < nu[0], j, nj - 1)
    grid_spec = pltpu.PrefetchScalarGridSpec(
        num_scalar_prefetch=2,
        grid=(N_TILES_MOE, nj),
        in_specs=[
            pl.BlockSpec((TM_MOE, D_MODEL), lambda i, j, te, nu: (i, 0)),
            pl.BlockSpec((None, D_MODEL, TF_MOE), lambda i, j, te, nu: (te[i], 0, col(i, j, nu))),
            pl.BlockSpec((None, D_MODEL, TF_MOE), lambda i, j, te, nu: (te[i], 0, col(i, j, nu))),
            pl.BlockSpec((None, TF_MOE, D_MODEL), lambda i, j, te, nu: (te[i], col(i, j, nu), 0)),
        ],
        out_specs=pl.BlockSpec((TM_MOE, D_MODEL), lambda i, j, te, nu: (i, 0)),
        scratch_shapes=[pltpu.VMEM((TM_MOE, D_MODEL), BF16)],
    )
    return pl.pallas_call(
        _gffn_body,
        grid_spec=grid_spec,
        out_shape=jax.ShapeDtypeStruct((N_SORTED, D_MODEL), F32),
        compiler_params=_params(("arbitrary", "arbitrary")),
        name="moe_ffn",
    )(tile_expert, n_used, xs, w1, w3, w2)


def _combine_body(pos_ref, x_ref, route_ref, g_ref, ys_hbm, o_ref, buf_ref, sem):
    i = pl.program_id(0)
    n = pl.num_programs(0)
    tm = TM_COMBINE

    def issue(tile, slot):
        def body(t, carry):
            for k in range(2):
                src = pos_ref[2 * (tile * tm + t) + k]
                pltpu.make_async_copy(ys_hbm.at[pl.ds(src, 1)], buf_ref.at[slot, k, pl.ds(t, 1)],
                                      sem.at[slot]).start()
            return carry
        lax.fori_loop(0, tm, body, 0, unroll=8)

    @pl.when(i == 0)
    def _():
        issue(0, 0)

    @pl.when(i + 1 < n)
    def _():
        issue(i + 1, (i + 1) % 2)

    slot = i % 2
    for k in range(2):
        pltpu.make_async_copy(ys_hbm.at[pl.ds(0, tm)], buf_ref.at[slot, k], sem.at[slot]).wait()
    r = route_ref[...]
    lane = lax.broadcasted_iota(jnp.int32, r.shape, 1)
    w1 = jnp.sum(jnp.where(lane == ROUTE_W1, r, 0.0), axis=-1, keepdims=True)
    w2 = jnp.sum(jnp.where(lane == ROUTE_W2, r, 0.0), axis=-1, keepdims=True)
    y = x_ref[...] + w1 * buf_ref[slot, 0] + w2 * buf_ref[slot, 1]
    o_ref[...] = _rms(y, g_ref[...])


def _combine(pos, x, route, g, ys):
    tm = TM_COMBINE
    return pl.pallas_call(
        _combine_body,
        grid=(TOKENS // tm,),
        in_specs=[pl.BlockSpec(memory_space=pltpu.SMEM),
                  pl.BlockSpec((tm, D_MODEL), lambda i: (i, 0)),
                  pl.BlockSpec((tm, LANES), lambda i: (i, 0)),
                  pl.BlockSpec((1, D_MODEL), lambda i: (0, 0)),
                  pl.BlockSpec(memory_space=pl.ANY)],
        out_specs=pl.BlockSpec((tm, D_MODEL), lambda i: (i, 0)),
        out_shape=jax.ShapeDtypeStruct((TOKENS, D_MODEL), F32),
        scratch_shapes=[pltpu.VMEM((2, 2, tm, D_MODEL), F32), pltpu.SemaphoreType.DMA((2,))],
        compiler_params=_params(("arbitrary",)),
        name="moe_combine",
    )(pos, x, route, g, ys)


def _routing_tables(route, counts):
    cnt = counts[0, :N_EXPERTS].astype(jnp.int32)
    tiles = (cnt + TM_MOE - 1) // TM_MOE
    tile_end = jnp.cumsum(tiles)
    tile_start = tile_end - tiles
    n_used = tile_end[-1]
    expert = route[:, ROUTE_I1:ROUTE_I2 + 1].astype(jnp.int32)
    rank = route[:, ROUTE_R1:ROUTE_R2 + 1].astype(jnp.int32)
    pos = (tile_start * TM_MOE)[expert] + rank
    tile_id = jnp.minimum(jnp.arange(N_TILES_MOE, dtype=jnp.int32), n_used - 1)
    tile_expert = jnp.sum(tile_id[:, None] >= tile_end[None, :], axis=1).astype(jnp.int32)
    return pos.reshape(2 * TOKENS), tile_expert, n_used.reshape(1)


def _moe(x, h, route, counts, w1, w3, w2, g_final):
    pos, tile_expert, n_used = _routing_tables(route, counts)
    xs = _dispatch(pos, h)
    ys = _grouped_ffn(tile_expert, n_used, xs, w1, w3, w2)
    return _combine(pos, x, route, g_final, ys)


def _prep_w_in(w):
    aq, ak, av, gq, gk, gv, gr, glr, c_in, c_b, c_c = jnp.split(w, np.cumsum(SPLIT_SIZES)[:-1].tolist(), axis=1)
    pad = jnp.zeros((D_MODEL, LANES - GLA_RANK), w.dtype)
    return jnp.concatenate([aq, ak, av, gq, gv, gr, gk, c_in, c_b, c_c, glr, pad], axis=1).astype(BF16)


def kernel(x, w_mix_in, w_mix_out, g_mix, rel_bias, gla_w_gate, gla_b_gate, gla_g_norm, conv_w,
           g_ffn, ffn_w1, ffn_w3, ffn_w2, moe_router, moe_w1, moe_w3, moe_w2, g_final):
    assert DEPTH == 2
    x = x.reshape(TOKENS, D_MODEL)
    for layer in range(DEPTH):
        u = _inproj(x, g_mix[layer].reshape(1, D_MODEL), _prep_w_in(w_mix_in[layer]))
        att = _attention(u, rel_bias)
        wg = jnp.pad(gla_w_gate[layer], ((0, LANES - GLA_RANK), (0, 0)))
        go = _gla(u, wg, gla_b_gate[layer].reshape(1, GLA_QK), gla_g_norm[layer].reshape(1, GLA_DV))
        cv = _conv(u, jnp.pad(conv_w[layer], ((0, 8 - CONV_WIDTH), (0, 0))))
        w_out = w_mix_out[layer].astype(BF16)
        g2 = g_ffn[layer].reshape(1, D_MODEL)
        i = layer // 2
        if layer % 2 == 0:
            x, h = _outproj(att, go, cv, x, w_out, g2)
            x = _ffn(x, h, ffn_w1[i].astype(BF16), ffn_w3[i].astype(BF16), ffn_w2[i].astype(BF16), tf=TF_DENSE)
        else:
            wr = jnp.pad(moe_router[i], ((0, 0), (0, LANES - N_EXPERTS)))
            x, h, route, counts = _outproj(att, go, cv, x, w_out, g2, wr)
            x = _moe(x, h, route, counts, moe_w1[i].astype(BF16), moe_w3[i].astype(BF16),
                     moe_w2[i].astype(BF16), g_final.reshape(1, D_MODEL))
    return x.reshape(BATCH, SEQ, D_MODEL)
```

```python
import functools
import math

import jax
import jax.numpy as jnp
import numpy as np
from jax import lax
from jax.experimental import pallas as pl
from jax.experimental.pallas import tpu as pltpu

F32 = jnp.float32
BF16 = jnp.bfloat16

D_MODEL = 1024
BATCH = 8
SEQ = 2048
TOKENS = BATCH * SEQ
DEPTH = 2
EPS = 1e-6

HEAD_DIM = 64
ATT_HEADS = 4
ATT_WIDTH = ATT_HEADS * HEAD_DIM
DILATED_PATTERNS = ((128, 1), (512, 4), (2048, 16))
ATT_BLOCK = 128
REL_BUCKETS = 32
REL_MAX_DISTANCE = 2048

GLA_HEADS = 4
GLA_DK = 64
GLA_DV = 128
GLA_RANK = 16
GLA_CHUNK = 64
GLA_QK = GLA_HEADS * GLA_DK
GLA_V = GLA_HEADS * GLA_DV

CONV_CH = 256
CONV_WIDTH = 3
MIX_WIDTH = ATT_WIDTH + GLA_V + CONV_CH

SPLIT_SIZES = (ATT_WIDTH, ATT_WIDTH, ATT_WIDTH, GLA_QK, GLA_QK, GLA_V, GLA_V, GLA_RANK,
               CONV_CH, CONV_CH, CONV_CH)

FFN_DENSE = 2816
N_EXPERTS = 8
FFN_EXPERT = 3584

LANES = 128
MXU_WIDTH = 256
VMEM_LIMIT = 56 * 1024 * 1024

QKV_COLS = 3 * ATT_WIDTH
COL_GV, COL_GR, COL_GQ, COL_GK = 0, 512, 1024, 1280
COL_CIN, COL_CB, COL_CC, COL_GLR = 1536, 1792, 2048, 2304
U_COLS = COL_GLR + LANES

NEG_BIG = -1e30

TM_PROJ = 512
TC_GLA = 512
TM_FFN = 1024
TF_DENSE = 1408
FFN_SUB = 512
TM_MOE = 512
TF_MOE = 512
N_TILES_MOE = 2 * TOKENS // TM_MOE + N_EXPERTS
N_SORTED = N_TILES_MOE * TM_MOE
DISPATCH_CHUNK = 256
TM_COMBINE = 256

ROUTE_I1, ROUTE_I2, ROUTE_W1, ROUTE_W2, ROUTE_R1, ROUTE_R2 = range(6)


def _params(sem):
    return pltpu.CompilerParams(dimension_semantics=sem, vmem_limit_bytes=VMEM_LIMIT)


def _split_bf16(a):
    hi = a.astype(BF16)
    lo = (a - hi.astype(F32)).astype(BF16)
    return hi, lo


def _dot(a, b):
    return jnp.dot(a, b, preferred_element_type=F32)


def _dot3(a, b):
    a_hi, a_lo = _split_bf16(a)
    b_hi, b_lo = _split_bf16(b)
    return _dot(a_hi, b_hi) + _dot(a_lo, b_hi) + _dot(a_hi, b_lo)


def _dot_nt(a, b):
    return lax.dot_general(a, b, (((1,), (1,)), ((), ())), preferred_element_type=F32)


def _dot_tn(a, b):
    return lax.dot_general(a, b, (((0,), (0,)), ((), ())), preferred_element_type=F32)


def _rms(x, g):
    ms = jnp.mean(x * x, axis=-1, keepdims=True)
    return x * lax.rsqrt(ms + EPS) * g


def _sigmoid(x):
    return 1.0 / (1.0 + jnp.exp(-x))


def _inproj_body(x_ref, g_ref, w_ref, o_ref, *rest):
    qkv_refs, qkv_f32 = rest[:-1], rest[-1]
    h = _rms(x_ref[...], g_ref[...]).astype(BF16)
    for c0 in range(0, QKV_COLS, MXU_WIDTH):
        res = _dot(h, w_ref[:, c0:c0 + MXU_WIDTH])
        for t in range(MXU_WIDTH // LANES):
            qkv_f32[c0 // LANES + t] = res[:, t * LANES:(t + 1) * LANES]
    for (_, dilation), ref in zip(DILATED_PATTERNS, qkv_refs):
        for r in range(dilation):
            rows = pl.ds(r, TM_PROJ // dilation, stride=dilation)
            ref[r] = jnp.concatenate([qkv_f32[t, rows, :] for t in range(QKV_COLS // LANES)],
                                     axis=-1).astype(ref.dtype)
    for c0 in range(0, U_COLS, MXU_WIDTH):
        c1 = min(c0 + MXU_WIDTH, U_COLS)
        o_ref[:, c0:c1] = _dot(h, w_ref[:, QKV_COLS + c0:QKV_COLS + c1])


def _subseq_spec(dilation, cols):
    tiles = SEQ // TM_PROJ
    return pl.BlockSpec((None, dilation, TM_PROJ // dilation, cols), lambda i: (i // tiles, 0, i % tiles, 0))


def _inproj(x, g, w):
    qkv_shapes = [jax.ShapeDtypeStruct((BATCH, d, SEQ // d, QKV_COLS), BF16) for _, d in DILATED_PATTERNS]
    return pl.pallas_call(
        _inproj_body,
        grid=(TOKENS // TM_PROJ,),
        in_specs=[
            pl.BlockSpec((TM_PROJ, D_MODEL), lambda i: (i, 0)),
            pl.BlockSpec((1, D_MODEL), lambda i: (0, 0)),
            pl.BlockSpec((D_MODEL, QKV_COLS + U_COLS), lambda i: (0, 0)),
        ],
        out_specs=[pl.BlockSpec((TM_PROJ, U_COLS), lambda i: (i, 0))]
                  + [_subseq_spec(d, QKV_COLS) for _, d in DILATED_PATTERNS],
        out_shape=[jax.ShapeDtypeStruct((TOKENS, U_COLS), F32)] + qkv_shapes,
        scratch_shapes=[pltpu.VMEM((QKV_COLS // LANES, TM_PROJ, LANES), F32)],
        compiler_params=_params(("parallel",)),
        name="inproj",
    )(x, g, w)


def _rel_bucket(dist):
    max_exact = REL_BUCKETS // 2
    d = jnp.maximum(dist, 0)
    log_ratio = jnp.log(jnp.maximum(d, 1).astype(F32) / max_exact) / math.log(REL_MAX_DISTANCE / max_exact)
    large = jnp.minimum(max_exact + (log_ratio * (REL_BUCKETS - max_exact)).astype(jnp.int32), REL_BUCKETS - 1)
    return jnp.where(d < max_exact, d, large)


def _bucket_table(window, dilation):
    span = window // dilation
    qi = jnp.arange(ATT_BLOCK)[:, None]
    kj = jnp.arange(2 * ATT_BLOCK)[None, :]
    sub_dist = qi - kj + ATT_BLOCK
    band = (sub_dist >= 0) & (sub_dist <= span)
    return jnp.where(band, _rel_bucket(sub_dist * dilation), -1).astype(jnp.int32)


def _attn_body(rb_ref, bidx_ref, q_ref, k_ref, v_ref, o_ref, lse_ref, bias_ref, *, nblk):
    @pl.when((pl.program_id(0) == 0) & (pl.program_id(1) == 0))
    def _():
        bidx = bidx_ref[...]
        for h in range(ATT_HEADS):
            acc = jnp.full(bidx.shape, NEG_BIG, F32)
            for b in range(REL_BUCKETS):
                acc = jnp.where(bidx == b, rb_ref[b, h], acc)
            bias_ref[h] = acc

    def block(n, first):
        if first:
            q = q_ref[0:ATT_BLOCK, :]
            kk = k_ref[0:ATT_BLOCK, :]
            vv = v_ref[0:ATT_BLOCK, :]
            rows = slice(0, ATT_BLOCK)
        else:
            r0 = pl.multiple_of(n * ATT_BLOCK, ATT_BLOCK)
            rows = pl.ds(r0, ATT_BLOCK)
            q = q_ref[rows, :]
            kk = k_ref[pl.ds(r0 - ATT_BLOCK, 2 * ATT_BLOCK), :]
            vv = v_ref[pl.ds(r0 - ATT_BLOCK, 2 * ATT_BLOCK), :]
        q = q * jnp.asarray(HEAD_DIM ** -0.5, BF16)
        head_of_lane = lax.broadcasted_iota(jnp.int32, (ATT_BLOCK, ATT_WIDTH), 1) // HEAD_DIM
        ones = jnp.ones((kk.shape[0], LANES), BF16)
        num = den = mx = None
        for h in range(ATT_HEADS):
            mine = head_of_lane == h
            bias = bias_ref[h, :, ATT_BLOCK:] if first else bias_ref[h]
            s = _dot_nt(jnp.where(mine, q, jnp.zeros_like(q)), kk) + bias
            m = jnp.max(s, axis=-1, keepdims=True)
            p = jnp.exp(s - m).astype(BF16)
            num_h = _dot(p, vv)
            den_h = jnp.tile(_dot(p, ones), (1, ATT_WIDTH // LANES))
            m_h = jnp.broadcast_to(m, (ATT_BLOCK, ATT_WIDTH))
            num = num_h if h == 0 else jnp.where(mine, num_h, num)
            den = den_h if h == 0 else jnp.where(mine, den_h, den)
            mx = m_h if h == 0 else jnp.where(mine, m_h, mx)
        o_ref[rows, :] = (num / den).astype(o_ref.dtype)
        lse_ref[rows, :] = mx + jnp.log(den)

    block(0, True)
    if nblk > 1:
        def loop_body(n, carry):
            block(n, False)
            return carry
        lax.fori_loop(1, nblk, loop_body, 0)


def _attention_pattern(ua, rel_bias, window, dilation):
    L = SEQ // dilation
    nblk = L // ATT_BLOCK
    qkv_spec = lambda j: pl.BlockSpec((None, None, L, ATT_WIDTH), lambda b, r: (b, r, 0, j))
    out_spec = pl.BlockSpec((None, None, L, ATT_WIDTH), lambda b, r: (b, r, 0, 0))
    return pl.pallas_call(
        functools.partial(_attn_body, nblk=nblk),
        grid=(BATCH, dilation),
        in_specs=[
            pl.BlockSpec(memory_space=pltpu.SMEM),
            pl.BlockSpec((ATT_BLOCK, 2 * ATT_BLOCK), lambda b, r: (0, 0)),
            qkv_spec(0), qkv_spec(1), qkv_spec(2),
        ],
        out_specs=[out_spec, out_spec],
        out_shape=[jax.ShapeDtypeStruct((BATCH, dilation, L, ATT_WIDTH), BF16),
                   jax.ShapeDtypeStruct((BATCH, dilation, L, ATT_WIDTH), F32)],
        scratch_shapes=[pltpu.VMEM((ATT_HEADS, ATT_BLOCK, 2 * ATT_BLOCK), F32)],
        compiler_params=_params(("arbitrary", "arbitrary")),
        name=f"attn_d{dilation}",
    )(rel_bias, _bucket_table(window, dilation), ua, ua, ua)


def _attention(qkvs, rel_bias):
    return [_attention_pattern(ua, rel_bias, window, dilation)
            for ua, (window, dilation) in zip(qkvs, DILATED_PATTERNS)]


def _gla_body(q_ref, k_ref, v_ref, gr_ref, glr_ref, wg_ref, bg_ref, gn_ref, o_ref, s_ref):
    @pl.when(pl.program_id(1) == 0)
    def _():
        s_ref[...] = jnp.zeros_like(s_ref)

    C = GLA_CHUNK
    row = lax.broadcasted_iota(jnp.int32, (C, C), 0)
    col = lax.broadcasted_iota(jnp.int32, (C, C), 1)
    tril = row >= col
    tril_bf = jnp.where(tril, 1.0, 0.0).astype(BF16)
    ones_bf = jnp.ones((C, GLA_DV), BF16)

    xg = _dot3(glr_ref[...], wg_ref[...]) + bg_ref[...]
    la_all = (jnp.minimum(xg, 0.0) - jnp.log(1.0 + jnp.exp(-jnp.abs(xg)))) * (1.0 / 16.0)

    for c in range(TC_GLA // C):
        rows = slice(c * C, (c + 1) * C)
        la_hi, la_lo = _split_bf16(la_all[rows])
        cum = _dot(tril_bf, la_hi) + _dot(tril_bf, la_lo)
        last = cum[C - 1:C, :]
        q = q_ref[rows, :]
        k = k_ref[rows, :]
        qt = (q * jnp.exp(cum) * (GLA_DK ** -0.5)).astype(BF16)
        kt = (k * jnp.exp(-cum)).astype(BF16)
        kl = (k * jnp.exp(last - cum)).astype(BF16)
        for h in range(GLA_HEADS):
            sl = slice(h * GLA_DK, (h + 1) * GLA_DK)
            vs = slice(h * GLA_DV, (h + 1) * GLA_DV)
            vh = v_ref[rows, vs].astype(BF16)
            state = s_ref[h]
            st_hi, st_lo = _split_bf16(state)
            sc = jnp.where(tril, _dot_nt(qt[:, sl], kt[:, sl]), 0.0).astype(BF16)
            o = _dot(qt[:, sl], st_hi) + _dot(qt[:, sl], st_lo) + _dot(sc, vh)
            ltot = _dot_tn(la_hi[:, sl], ones_bf) + _dot_tn(la_lo[:, sl], ones_bf)
            s_ref[h] = jnp.exp(ltot) * state + _dot_tn(kl[:, sl], vh)
            g = gr_ref[rows, vs]
            o_ref[rows, vs] = (_rms(o, gn_ref[...]) * (g * _sigmoid(g))).astype(o_ref.dtype)


def _gla(u, wg, bg, gn):
    nj = SEQ // TC_GLA
    row = lambda b, j: b * nj + j
    return pl.pallas_call(
        _gla_body,
        grid=(BATCH, nj),
        in_specs=[
            pl.BlockSpec((TC_GLA, GLA_QK), lambda b, j: (row(b, j), COL_GQ // GLA_QK)),
            pl.BlockSpec((TC_GLA, GLA_QK), lambda b, j: (row(b, j), COL_GK // GLA_QK)),
            pl.BlockSpec((TC_GLA, GLA_V), lambda b, j: (row(b, j), COL_GV // GLA_V)),
            pl.BlockSpec((TC_GLA, GLA_V), lambda b, j: (row(b, j), COL_GR // GLA_V)),
            pl.BlockSpec((TC_GLA, LANES), lambda b, j: (row(b, j), COL_GLR // LANES)),
            pl.BlockSpec((LANES, GLA_QK), lambda b, j: (0, 0)),
            pl.BlockSpec((1, GLA_QK), lambda b, j: (0, 0)),
            pl.BlockSpec((1, GLA_DV), lambda b, j: (0, 0)),
        ],
        out_specs=pl.BlockSpec((TC_GLA, GLA_V), lambda b, j: (row(b, j), 0)),
        out_shape=jax.ShapeDtypeStruct((TOKENS, GLA_V), BF16),
        scratch_shapes=[pltpu.VMEM((GLA_HEADS, GLA_DK, GLA_DV), F32)],
        compiler_params=_params(("arbitrary", "arbitrary")),
        name="gla",
    )(u, u, u, u, u, wg, bg, gn)


def _conv_body(cin_ref, cb_ref, cc_ref, w_ref, o_ref):
    uu = cc_ref[...] * cin_ref[...]
    t = lax.broadcasted_iota(jnp.int32, uu.shape, 0)
    y = uu * w_ref[CONV_WIDTH - 1:CONV_WIDTH, :]
    for shift in range(1, CONV_WIDTH):
        prev = jnp.where(t >= shift, pltpu.roll(uu, shift, axis=0), 0.0)
        y = y + prev * w_ref[CONV_WIDTH - 1 - shift:CONV_WIDTH - shift, :]
    o_ref[...] = (cb_ref[...] * y).astype(o_ref.dtype)


def _conv(u, w):
    spec = lambda col: pl.BlockSpec((SEQ, CONV_CH), lambda b: (b, col // CONV_CH))
    return pl.pallas_call(
        _conv_body,
        grid=(BATCH,),
        in_specs=[spec(COL_CIN), spec(COL_CB), spec(COL_CC),
                  pl.BlockSpec((8, CONV_CH), lambda b: (0, 0))],
        out_specs=pl.BlockSpec((SEQ, CONV_CH), lambda b: (b, 0)),
        out_shape=jax.ShapeDtypeStruct((TOKENS, CONV_CH), BF16),
        compiler_params=_params(("parallel",)),
        name="conv",
    )(u, u, u, w)


def _outproj_body(*refs, route):
    (o1, o4, o16, l1, l4, l16, go_ref, cv_ref, x_ref, w_ref, g_ref) = refs[:11]
    perm_ref = refs[-1]
    if route:
        wr_ref, xo_ref, ho_ref, route_ref, counts_ref, carry_ref = refs[11:-1]
    else:
        xo_ref, ho_ref = refs[11:-1]

    def token_order(ref, slot):
        dilation, rows, _ = ref.shape
        if dilation == 1:
            return ref[0].astype(F32)
        tiles = range(ATT_WIDTH // LANES)
        for r in range(dilation):
            val = ref[r].astype(F32)
            for t in tiles:
                perm_ref[slot, t, pl.ds(r, rows, stride=dilation), :] = val[:, t * LANES:(t + 1) * LANES]
        return jnp.concatenate([perm_ref[slot, t] for t in tiles], axis=-1)

    la, lb, lc = token_order(l1, 0), token_order(l4, 0), token_order(l16, 1)
    oa, ob, oc = token_order(o1, 0), token_order(o4, 2), token_order(o16, 3)
    m = jnp.maximum(jnp.maximum(la, lb), lc)
    ea, eb, ec = jnp.exp(la - m), jnp.exp(lb - m), jnp.exp(lc - m)
    att = (ea * oa + eb * ob + ec * oc) / (ea + eb + ec)
    y = (x_ref[...]
         + _dot(att.astype(BF16), w_ref[0:ATT_WIDTH, :])
         + _dot(go_ref[...], w_ref[ATT_WIDTH:ATT_WIDTH + GLA_V, :])
         + _dot(cv_ref[...], w_ref[ATT_WIDTH + GLA_V:MIX_WIDTH, :]))
    xo_ref[...] = y
    hf = _rms(y, g_ref[...])
    ho_ref[...] = hf.astype(ho_ref.dtype)
    if route:
        @pl.when(pl.program_id(0) == 0)
        def _():
            carry_ref[...] = jnp.zeros_like(carry_ref)

        tm = hf.shape[0]
        logits = _dot3(hf, wr_ref[...])
        lane = lax.broadcasted_iota(jnp.int32, logits.shape, 1).astype(F32)
        lg = jnp.where(lane < N_EXPERTS, logits, -jnp.inf)
        v1 = jnp.max(lg, axis=-1, keepdims=True)
        i1 = jnp.min(jnp.where(lg == v1, lane, float(LANES)), axis=-1, keepdims=True)
        lg2 = jnp.where(lane == i1, -jnp.inf, lg)
        v2 = jnp.max(lg2, axis=-1, keepdims=True)
        i2 = jnp.min(jnp.where(lg2 == v2, lane, float(LANES)), axis=-1, keepdims=True)
        e2 = jnp.exp(v2 - v1)
        w1 = 1.0 / (1.0 + e2)
        w2 = e2 * w1
        sel1 = lane == i1
        sel2 = lane == i2
        onehot = jnp.where(sel1, 1.0, jnp.where(sel2, 1.0, 0.0))
        tri = (lax.broadcasted_iota(jnp.int32, (tm, tm), 0) >= lax.broadcasted_iota(jnp.int32, (tm, tm), 1))
        csum = _dot(jnp.where(tri, 1.0, 0.0).astype(BF16), onehot.astype(BF16))
        carry = carry_ref[0:1, :]
        rank = csum - onehot + carry
        r1 = jnp.sum(jnp.where(sel1, rank, 0.0), axis=-1, keepdims=True)
        r2 = jnp.sum(jnp.where(sel2, rank, 0.0), axis=-1, keepdims=True)
        total = jnp.broadcast_to(carry + csum[tm - 1:tm, :], carry_ref.shape)
        carry_ref[...] = total
        counts_ref[...] = total
        cols = {ROUTE_I1: i1, ROUTE_I2: i2, ROUTE_W1: w1, ROUTE_W2: w2, ROUTE_R1: r1, ROUTE_R2: r2}
        packed = jnp.zeros_like(logits)
        for c, val in cols.items():
            packed = jnp.where(lane == float(c), val, packed)
        route_ref[...] = packed


def _outproj(att, go, cv, x, w, g, w_router=None):
    route = w_router is not None
    tm = TM_PROJ
    tile = lambda cols: pl.BlockSpec((tm, cols), lambda i: (i, 0))
    full = lambda a: pl.BlockSpec(a.shape, lambda i: (0, 0))
    (o1, l1), (o4, l4), (o16, l16) = att
    args = [o1, o4, o16, l1, l4, l16, go, cv, x, w, g]
    att_specs = [_subseq_spec(d, ATT_WIDTH) for _, d in DILATED_PATTERNS]
    in_specs = att_specs * 2 + [tile(GLA_V), tile(CONV_CH), tile(D_MODEL), full(w), full(g)]
    out_specs = [tile(D_MODEL), tile(D_MODEL)]
    out_shape = [jax.ShapeDtypeStruct((TOKENS, D_MODEL), F32),
                 jax.ShapeDtypeStruct((TOKENS, D_MODEL), F32 if route else BF16)]
    scratch = []
    if route:
        args.append(w_router)
        in_specs.append(full(w_router))
        out_specs += [tile(LANES), pl.BlockSpec((8, LANES), lambda i: (0, 0))]
        out_shape += [jax.ShapeDtypeStruct((TOKENS, LANES), F32), jax.ShapeDtypeStruct((8, LANES), F32)]
        scratch = [pltpu.VMEM((8, LANES), F32)]
    scratch.append(pltpu.VMEM((4, ATT_WIDTH // LANES, tm, LANES), F32))
    return pl.pallas_call(
        functools.partial(_outproj_body, route=route),
        grid=(TOKENS // tm,),
        in_specs=in_specs,
        out_specs=out_specs,
        out_shape=out_shape,
        scratch_shapes=scratch,
        compiler_params=_params(("arbitrary",) if route else ("parallel",)),
        name="outproj_route" if route else "outproj",
    )(*args)


def _swiglu_accumulate(h, w1_ref, w3_ref, w2_ref, acc_ref, tf):
    for c0 in range(0, tf, FFN_SUB):
        c1 = min(c0 + FFN_SUB, tf)
        a = _dot(h, w1_ref[:, c0:c1])
        b = _dot(h, w3_ref[:, c0:c1])
        act = a * _sigmoid(a) * b
        acc_ref[...] += _dot(act.astype(BF16), w2_ref[c0:c1, :])


def _ffn_body(x_ref, h_ref, w1_ref, w3_ref, w2_ref, o_ref, *, tf):
    @pl.when(pl.program_id(1) == 0)
    def _():
        o_ref[...] = x_ref[...]

    _swiglu_accumulate(h_ref[...], w1_ref, w3_ref, w2_ref, o_ref, tf)


def _ffn(x, h, w1, w3, w2, *, tf):
    f = w1.shape[1]
    tm = TM_FFN
    tile = lambda cols: pl.BlockSpec((tm, cols), lambda i, j: (i, 0))
    return pl.pallas_call(
        functools.partial(_ffn_body, tf=tf),
        grid=(TOKENS // tm, f // tf),
        in_specs=[tile(D_MODEL), tile(D_MODEL),
                  pl.BlockSpec((D_MODEL, tf), lambda i, j: (0, j)),
                  pl.BlockSpec((D_MODEL, tf), lambda i, j: (0, j)),
                  pl.BlockSpec((tf, D_MODEL), lambda i, j: (j, 0))],
        out_specs=tile(D_MODEL),
        out_shape=jax.ShapeDtypeStruct((TOKENS, D_MODEL), F32),
        compiler_params=_params(("parallel", "arbitrary")),
        name="dense_ffn",
    )(x, h, w1, w3, w2)


def _dispatch_body(pos_ref, h_ref, xs_init, xs_hbm, stage_ref, sem):
    del xs_init
    i = pl.program_id(0)
    n = pl.num_programs(0)
    tm = DISPATCH_CHUNK
    slot = i % 2

    def drain(s):
        for _ in range(2):
            pltpu.make_async_copy(stage_ref.at[s], xs_hbm.at[pl.ds(0, tm)], sem.at[s]).wait()

    @pl.when(i >= 2)
    def _():
        drain(slot)

    stage_ref[slot] = h_ref[...]

    def body(t, carry):
        for k in range(2):
            dst = pos_ref[2 * (i * tm + t) + k]
            pltpu.make_async_copy(stage_ref.at[slot, pl.ds(t, 1)], xs_hbm.at[pl.ds(dst, 1)], sem.at[slot]).start()
        return carry
    lax.fori_loop(0, tm, body, 0, unroll=8)

    @pl.when(i == n - 1)
    def _():
        drain(1 - slot)
        drain(slot)


def _dispatch(pos, h):
    xs_init = jnp.zeros((N_SORTED, D_MODEL), F32)
    tm = DISPATCH_CHUNK
    return pl.pallas_call(
        _dispatch_body,
        grid=(TOKENS // tm,),
        in_specs=[pl.BlockSpec(memory_space=pltpu.SMEM),
                  pl.BlockSpec((tm, D_MODEL), lambda i: (i, 0)),
                  pl.BlockSpec(memory_space=pl.ANY)],
        out_specs=pl.BlockSpec(memory_space=pl.ANY),
        out_shape=jax.ShapeDtypeStruct((N_SORTED, D_MODEL), F32),
        scratch_shapes=[pltpu.VMEM((2, tm, D_MODEL), F32), pltpu.SemaphoreType.DMA((2,))],
        input_output_aliases={2: 0},
        compiler_params=_params(("arbitrary",)),
        name="moe_dispatch",
    )(pos, h, xs_init)


def _gffn_body(te_ref, nu_ref, xs_ref, w1_ref, w3_ref, w2_ref, o_ref, hb_ref):
    del te_ref
    i = pl.program_id(0)
    j = pl.program_id(1)

    @pl.when(j == 0)
    def _():
        hb_ref[...] = xs_ref[...].astype(BF16)
        o_ref[...] = jnp.zeros_like(o_ref)

    @pl.when(i < nu_ref[0])
    def _():
        _swiglu_accumulate(hb_ref[...], w1_ref, w3_ref, w2_ref, o_ref, TF_MOE)


def _grouped_ffn(tile_expert, n_used, xs, w1, w3, w2):
    nj = FFN_EXPERT // TF_MOE
    col = lambda i, j, nu: jnp.where(i < nu[0], j, nj - 1)
    grid_spec = pltpu.PrefetchScalarGridSpec(
        num_scalar_prefetch=2,
        grid=(N_TILES_MOE, nj),
        in_specs=[
            pl.BlockSpec((TM_MOE, D_MODEL), lambda i, j, te, nu: (i, 0)),
            pl.BlockSpec((None, D_MODEL, TF_MOE), lambda i, j, te, nu: (te[i], 0, col(i, j, nu))),
            pl.BlockSpec((None, D_MODEL, TF_MOE), lambda i, j, te, nu: (te[i], 0, col(i, j, nu))),
            pl.BlockSpec((None, TF_MOE, D_MODEL), lambda i, j, te, nu: (te[i], col(i, j, nu), 0)),
        ],
        out_specs=pl.BlockSpec((TM_MOE, D_MODEL), lambda i, j, te, nu: (i, 0)),
        scratch_shapes=[pltpu.VMEM((TM_MOE, D_MODEL), BF16)],
    )
    return pl.pallas_call(
        _gffn_body,
        grid_spec=grid_spec,
        out_shape=jax.ShapeDtypeStruct((N_SORTED, D_MODEL), F32),
        compiler_params=_params(("arbitrary", "arbitrary")),
        name="moe_ffn",
    )(tile_expert, n_used, xs, w1, w3, w2)


def _combine_body(pos_ref, x_ref, route_ref, g_ref, ys_hbm, o_ref, buf_ref, sem):
    i = pl.program_id(0)
    n = pl.num_programs(0)
    tm = TM_COMBINE

    def issue(tile, slot):
        def body(t, carry):
            for k in range(2):
                src = pos_ref[2 * (tile * tm + t) + k]
                pltpu.make_async_copy(ys_hbm.at[pl.ds(src, 1)], buf_ref.at[slot, k, pl.ds(t, 1)],
                                      sem.at[slot]).start()
            return carry
        lax.fori_loop(0, tm, body, 0, unroll=8)

    @pl.when(i == 0)
    def _():
        issue(0, 0)

    @pl.when(i + 1 < n)
    def _():
        issue(i + 1, (i + 1) % 2)

    slot = i % 2
    for k in range(2):
        pltpu.make_async_copy(ys_hbm.at[pl.ds(0, tm)], buf_ref.at[slot, k], sem.at[slot]).wait()
    r = route_ref[...]
    lane = lax.broadcasted_iota(jnp.int32, r.shape, 1)
    w1 = jnp.sum(jnp.where(lane == ROUTE_W1, r, 0.0), axis=-1, keepdims=True)
    w2 = jnp.sum(jnp.where(lane == ROUTE_W2, r, 0.0), axis=-1, keepdims=True)
    y = x_ref[...] + w1 * buf_ref[slot, 0] + w2 * buf_ref[slot, 1]
    o_ref[...] = _rms(y, g_ref[...])


def _combine(pos, x, route, g, ys):
    tm = TM_COMBINE
    return pl.pallas_call(
        _combine_body,
        grid=(TOKENS // tm,),
        in_specs=[pl.BlockSpec(memory_space=pltpu.SMEM),
                  pl.BlockSpec((tm, D_MODEL), lambda i: (i, 0)),
                  pl.BlockSpec((tm, LANES), lambda i: (i, 0)),
                  pl.BlockSpec((1, D_MODEL), lambda i: (0, 0)),
                  pl.BlockSpec(memory_space=pl.ANY)],
        out_specs=pl.BlockSpec((tm, D_MODEL), lambda i: (i, 0)),
        out_shape=jax.ShapeDtypeStruct((TOKENS, D_MODEL), F32),
        scratch_shapes=[pltpu.VMEM((2, 2, tm, D_MODEL), F32), pltpu.SemaphoreType.DMA((2,))],
        compiler_params=_params(("arbitrary",)),
        name="moe_combine",
    )(pos, x, route, g, ys)


def _routing_tables(route, counts):
    cnt = counts[0, :N_EXPERTS].astype(jnp.int32)
    tiles = (cnt + TM_MOE - 1) // TM_MOE
    tile_end = jnp.cumsum(tiles)
    tile_start = tile_end - tiles
    n_used = tile_end[-1]
    expert = route[:, ROUTE_I1:ROUTE_I2 + 1].astype(jnp.int32)
    rank = route[:, ROUTE_R1:ROUTE_R2 + 1].astype(jnp.int32)
    pos = (tile_start * TM_MOE)[expert] + rank
    tile_id = jnp.minimum(jnp.arange(N_TILES_MOE, dtype=jnp.int32), n_used - 1)
    tile_expert = jnp.sum(tile_id[:, None] >= tile_end[None, :], axis=1).astype(jnp.int32)
    return pos.reshape(2 * TOKENS), tile_expert, n_used.reshape(1)


def _moe(x, h, route, counts, w1, w3, w2, g_final):
    pos, tile_expert, n_used = _routing_tables(route, counts)
    xs = _dispatch(pos, h)
    ys = _grouped_ffn(tile_expert, n_used, xs, w1, w3, w2)
    return _combine(pos, x, route, g_final, ys)


def _prep_w_in(w):
    aq, ak, av, gq, gk, gv, gr, glr, c_in, c_b, c_c = jnp.split(w, np.cumsum(SPLIT_SIZES)[:-1].tolist(), axis=1)
    pad = jnp.zeros((D_MODEL, LANES - GLA_RANK), w.dtype)
    return jnp.concatenate([aq, ak, av, gv, gr, gq, gk, c_in, c_b, c_c, glr, pad], axis=1).astype(BF16)


def kernel(x, w_mix_in, w_mix_out, g_mix, rel_bias, gla_w_gate, gla_b_gate, gla_g_norm, conv_w,
           g_ffn, ffn_w1, ffn_w3, ffn_w2, moe_router, moe_w1, moe_w3, moe_w2, g_final):
    assert DEPTH == 2
    x = x.reshape(TOKENS, D_MODEL)
    for layer in range(DEPTH):
        u, *qkvs = _inproj(x, g_mix[layer].reshape(1, D_MODEL), _prep_w_in(w_mix_in[layer]))
        att = _attention(qkvs, rel_bias)
        wg = jnp.pad(gla_w_gate[layer], ((0, LANES - GLA_RANK), (0, 0)))
        go = _gla(u, wg, gla_b_gate[layer].reshape(1, GLA_QK), gla_g_norm[layer].reshape(1, GLA_DV))
        cv = _conv(u, jnp.pad(conv_w[layer], ((0, 8 - CONV_WIDTH), (0, 0))))
        w_out = w_mix_out[layer].astype(BF16)
        g2 = g_ffn[layer].reshape(1, D_MODEL)
        i = layer // 2
        if layer % 2 == 0:
            x, h = _outproj(att, go, cv, x, w_out, g2)
            x = _ffn(x, h, ffn_w1[i].astype(BF16), ffn_w3[i].astype(BF16), ffn_w2[i].astype(BF16), tf=TF_DENSE)
        else:
            wr = jnp.pad(moe_router[i], ((0, 0), (0, LANES - N_EXPERTS)))
            x, h, route, counts = _outproj(att, go, cv, x, w_out, g2, wr)
            x = _moe(x, h, route, counts, moe_w1[i].astype(BF16), moe_w3[i].astype(BF16),
                     moe_w2[i].astype(BF16), g_final.reshape(1, D_MODEL))
    return x.reshape(BATCH, SEQ, D_MODEL)
```

```python
import functools
import math

import jax
import jax.numpy as jnp
import numpy as np
from jax import lax
from jax.experimental import pallas as pl
from jax.experimental.pallas import tpu as pltpu

F32 = jnp.float32
BF16 = jnp.bfloat16

D_MODEL = 1024
BATCH = 8
SEQ = 2048
TOKENS = BATCH * SEQ
DEPTH = 2
EPS = 1e-6

HEAD_DIM = 64
ATT_HEADS = 4
ATT_WIDTH = ATT_HEADS * HEAD_DIM
DILATED_PATTERNS = ((128, 1), (512, 4), (2048, 16))
ATT_BLOCK = 128
REL_BUCKETS = 32
REL_MAX_DISTANCE = 2048

GLA_HEADS = 4
GLA_DK = 64
GLA_DV = 128
GLA_RANK = 16
GLA_CHUNK = 64
GLA_QK = GLA_HEADS * GLA_DK
GLA_V = GLA_HEADS * GLA_DV

CONV_CH = 256
CONV_WIDTH = 3
MIX_WIDTH = ATT_WIDTH + GLA_V + CONV_CH

SPLIT_SIZES = (ATT_WIDTH, ATT_WIDTH, ATT_WIDTH, GLA_QK, GLA_QK, GLA_V, GLA_V, GLA_RANK,
               CONV_CH, CONV_CH, CONV_CH)

FFN_DENSE = 2816
N_EXPERTS = 8
FFN_EXPERT = 3584

LANES = 128
MXU_WIDTH = 256
VMEM_LIMIT = 56 * 1024 * 1024

QKV_COLS = 3 * ATT_WIDTH
COL_GV, COL_GR, COL_GQ, COL_GK = 0, 512, 1024, 1280
COL_CIN, COL_CB, COL_CC, COL_GLR = 1536, 1792, 2048, 2304
U_COLS = COL_GLR + LANES

NEG_BIG = -1e30

ATT_INFLIGHT = 4
TM_PROJ = 512
TC_GLA = 512
TM_FFN = 1024
TF_DENSE = 1408
FFN_SUB = 512
TM_MOE = 512
TF_MOE = 1792
N_TILES_MOE = 2 * TOKENS // TM_MOE + N_EXPERTS
N_SORTED = N_TILES_MOE * TM_MOE
DISPATCH_CHUNK = 256
TM_COMBINE = 256

ROUTE_I1, ROUTE_I2, ROUTE_W1, ROUTE_W2, ROUTE_R1, ROUTE_R2 = range(6)


def _params(sem):
    return pltpu.CompilerParams(dimension_semantics=sem, vmem_limit_bytes=VMEM_LIMIT)


def _split_bf16(a):
    hi = a.astype(BF16)
    lo = (a - hi.astype(F32)).astype(BF16)
    return hi, lo


def _dot(a, b):
    return jnp.dot(a, b, preferred_element_type=F32)


def _dot3(a, b):
    a_hi, a_lo = _split_bf16(a)
    b_hi, b_lo = _split_bf16(b)
    return _dot(a_hi, b_hi) + _dot(a_lo, b_hi) + _dot(a_hi, b_lo)


def _dot_nt(a, b):
    return lax.dot_general(a, b, (((1,), (1,)), ((), ())), preferred_element_type=F32)


def _dot_tn(a, b):
    return lax.dot_general(a, b, (((0,), (0,)), ((), ())), preferred_element_type=F32)


def _rms(x, g):
    ms = jnp.mean(x * x, axis=-1, keepdims=True)
    return x * lax.rsqrt(ms + EPS) * g


def _sigmoid(x):
    return 1.0 / (1.0 + jnp.exp(-x))


def _inproj_body(x_ref, g_ref, w_ref, o_ref, *rest):
    qkv_refs, qkv_f32 = rest[:-1], rest[-1]
    h = _rms(x_ref[...], g_ref[...]).astype(BF16)
    for c0 in range(0, QKV_COLS, MXU_WIDTH):
        res = _dot(h, w_ref[:, c0:c0 + MXU_WIDTH])
        for t in range(MXU_WIDTH // LANES):
            qkv_f32[c0 // LANES + t] = res[:, t * LANES:(t + 1) * LANES]
    for (_, dilation), ref in zip(DILATED_PATTERNS, qkv_refs):
        for r in range(dilation):
            rows = pl.ds(r, TM_PROJ // dilation, stride=dilation)
            ref[r] = jnp.concatenate([qkv_f32[t, rows, :] for t in range(QKV_COLS // LANES)],
                                     axis=-1).astype(ref.dtype)
    for c0 in range(0, U_COLS, MXU_WIDTH):
        c1 = min(c0 + MXU_WIDTH, U_COLS)
        o_ref[:, c0:c1] = _dot(h, w_ref[:, QKV_COLS + c0:QKV_COLS + c1])


def _subseq_spec(dilation, cols):
    tiles = SEQ // TM_PROJ
    return pl.BlockSpec((None, dilation, TM_PROJ // dilation, cols), lambda i: (i // tiles, 0, i % tiles, 0))


def _inproj(x, g, w):
    qkv_shapes = [jax.ShapeDtypeStruct((BATCH, d, SEQ // d, QKV_COLS), BF16) for _, d in DILATED_PATTERNS]
    return pl.pallas_call(
        _inproj_body,
        grid=(TOKENS // TM_PROJ,),
        in_specs=[
            pl.BlockSpec((TM_PROJ, D_MODEL), lambda i: (i, 0)),
            pl.BlockSpec((1, D_MODEL), lambda i: (0, 0)),
            pl.BlockSpec((D_MODEL, QKV_COLS + U_COLS), lambda i: (0, 0)),
        ],
        out_specs=[pl.BlockSpec((TM_PROJ, U_COLS), lambda i: (i, 0))]
                  + [_subseq_spec(d, QKV_COLS) for _, d in DILATED_PATTERNS],
        out_shape=[jax.ShapeDtypeStruct((TOKENS, U_COLS), F32)] + qkv_shapes,
        scratch_shapes=[pltpu.VMEM((QKV_COLS // LANES, TM_PROJ, LANES), F32)],
        compiler_params=_params(("parallel",)),
        name="inproj",
    )(x, g, w)


def _rel_bucket(dist):
    max_exact = REL_BUCKETS // 2
    d = jnp.maximum(dist, 0)
    log_ratio = jnp.log(jnp.maximum(d, 1).astype(F32) / max_exact) / math.log(REL_MAX_DISTANCE / max_exact)
    large = jnp.minimum(max_exact + (log_ratio * (REL_BUCKETS - max_exact)).astype(jnp.int32), REL_BUCKETS - 1)
    return jnp.where(d < max_exact, d, large)


def _bucket_table(window, dilation):
    span = window // dilation
    qi = jnp.arange(ATT_BLOCK)[:, None]
    kj = jnp.arange(2 * ATT_BLOCK)[None, :]
    sub_dist = qi - kj + ATT_BLOCK
    band = (sub_dist >= 0) & (sub_dist <= span)
    return jnp.where(band, _rel_bucket(sub_dist * dilation), -1).astype(jnp.int32)


def _attn_body(rb_ref, bidx_ref, q_ref, k_ref, v_ref, o_ref, lse_ref, bias_ref, *, nblk, unroll):
    @pl.when((pl.program_id(0) == 0) & (pl.program_id(1) == 0))
    def _():
        bidx = bidx_ref[...]
        for h in range(ATT_HEADS):
            acc = jnp.full(bidx.shape, NEG_BIG, F32)
            for b in range(REL_BUCKETS):
                acc = jnp.where(bidx == b, rb_ref[b, h], acc)
            bias_ref[h] = acc

    def block(sub, n, first):
        if first:
            q = q_ref[sub, 0:ATT_BLOCK, :]
            kk = k_ref[sub, 0:ATT_BLOCK, :]
            vv = v_ref[sub, 0:ATT_BLOCK, :]
            rows = slice(0, ATT_BLOCK)
        else:
            r0 = pl.multiple_of(n * ATT_BLOCK, ATT_BLOCK)
            rows = pl.ds(r0, ATT_BLOCK)
            q = q_ref[sub, rows, :]
            kk = k_ref[sub, pl.ds(r0 - ATT_BLOCK, 2 * ATT_BLOCK), :]
            vv = v_ref[sub, pl.ds(r0 - ATT_BLOCK, 2 * ATT_BLOCK), :]
        q = q * jnp.asarray(HEAD_DIM ** -0.5, BF16)
        head_of_lane = lax.broadcasted_iota(jnp.int32, (ATT_BLOCK, ATT_WIDTH), 1) // HEAD_DIM
        ones = jnp.ones((kk.shape[0], LANES), BF16)
        num = den = mx = None
        for h in range(ATT_HEADS):
            mine = head_of_lane == h
            bias = bias_ref[h, :, ATT_BLOCK:] if first else bias_ref[h]
            s = _dot_nt(jnp.where(mine, q, jnp.zeros_like(q)), kk) + bias
            m = jnp.max(s, axis=-1, keepdims=True)
            p = jnp.exp(s - m).astype(BF16)
            num_h = _dot(p, vv)
            den_h = jnp.tile(_dot(p, ones), (1, ATT_WIDTH // LANES))
            m_h = jnp.broadcast_to(m, (ATT_BLOCK, ATT_WIDTH))
            num = num_h if h == 0 else jnp.where(mine, num_h, num)
            den = den_h if h == 0 else jnp.where(mine, den_h, den)
            mx = m_h if h == 0 else jnp.where(mine, m_h, mx)
        o_ref[sub, rows, :] = (num / den).astype(o_ref.dtype)
        lse_ref[sub, rows, :] = mx + jnp.log(den)

    subs = range(q_ref.shape[0])
    for sub in subs:
        block(sub, 0, True)
    if nblk > 1:
        def loop_body(n, carry):
            for sub in subs:
                block(sub, n, False)
            return carry
        lax.fori_loop(1, nblk, loop_body, 0, unroll=unroll)


def _attention_pattern(ua, rel_bias, window, dilation):
    L = SEQ // dilation
    nblk = L // ATT_BLOCK
    per_step = min(dilation, ATT_INFLIGHT)
    unroll = max(1, ATT_INFLIGHT // per_step)
    while (nblk - 1) % unroll:
        unroll -= 1
    qkv_spec = lambda j: pl.BlockSpec((None, per_step, L, ATT_WIDTH), lambda b, r: (b, r, 0, j))
    out_spec = pl.BlockSpec((None, per_step, L, ATT_WIDTH), lambda b, r: (b, r, 0, 0))
    return pl.pallas_call(
        functools.partial(_attn_body, nblk=nblk, unroll=unroll),
        grid=(BATCH, dilation // per_step),
        in_specs=[
            pl.BlockSpec(memory_space=pltpu.SMEM),
            pl.BlockSpec((ATT_BLOCK, 2 * ATT_BLOCK), lambda b, r: (0, 0)),
            qkv_spec(0), qkv_spec(1), qkv_spec(2),
        ],
        out_specs=[out_spec, out_spec],
        out_shape=[jax.ShapeDtypeStruct((BATCH, dilation, L, ATT_WIDTH), BF16),
                   jax.ShapeDtypeStruct((BATCH, dilation, L, ATT_WIDTH), F32)],
        scratch_shapes=[pltpu.VMEM((ATT_HEADS, ATT_BLOCK, 2 * ATT_BLOCK), F32)],
        compiler_params=_params(("arbitrary", "arbitrary")),
        name=f"attn_d{dilation}",
    )(rel_bias, _bucket_table(window, dilation), ua, ua, ua)


def _attention(qkvs, rel_bias):
    return [_attention_pattern(ua, rel_bias, window, dilation)
            for ua, (window, dilation) in zip(qkvs, DILATED_PATTERNS)]


def _gla_body(q_ref, k_ref, v_ref, gr_ref, glr_ref, wg_ref, bg_ref, gn_ref, o_ref, s_ref):
    @pl.when(pl.program_id(1) == 0)
    def _():
        s_ref[...] = jnp.zeros_like(s_ref)

    C = GLA_CHUNK
    row = lax.broadcasted_iota(jnp.int32, (C, C), 0)
    col = lax.broadcasted_iota(jnp.int32, (C, C), 1)
    tril = row >= col
    tril_bf = jnp.where(tril, 1.0, 0.0).astype(BF16)
    ones_bf = jnp.ones((C, GLA_DV), BF16)

    xg = _dot3(glr_ref[...], wg_ref[...]) + bg_ref[...]
    la_all = (jnp.minimum(xg, 0.0) - jnp.log(1.0 + jnp.exp(-jnp.abs(xg)))) * (1.0 / 16.0)

    for c in range(TC_GLA // C):
        rows = slice(c * C, (c + 1) * C)
        la_hi, la_lo = _split_bf16(la_all[rows])
        cum = _dot(tril_bf, la_hi) + _dot(tril_bf, la_lo)
        last = cum[C - 1:C, :]
        q = q_ref[rows, :]
        k = k_ref[rows, :]
        qt = (q * jnp.exp(cum) * (GLA_DK ** -0.5)).astype(BF16)
        kt = (k * jnp.exp(-cum)).astype(BF16)
        kl = (k * jnp.exp(last - cum)).astype(BF16)
        for h in range(GLA_HEADS):
            sl = slice(h * GLA_DK, (h + 1) * GLA_DK)
            vs = slice(h * GLA_DV, (h + 1) * GLA_DV)
            vh = v_ref[rows, vs].astype(BF16)
            state = s_ref[h]
            st_hi, st_lo = _split_bf16(state)
            sc = jnp.where(tril, _dot_nt(qt[:, sl], kt[:, sl]), 0.0).astype(BF16)
            o = _dot(qt[:, sl], st_hi) + _dot(qt[:, sl], st_lo) + _dot(sc, vh)
            ltot = _dot_tn(la_hi[:, sl], ones_bf) + _dot_tn(la_lo[:, sl], ones_bf)
            s_ref[h] = jnp.exp(ltot) * state + _dot_tn(kl[:, sl], vh)
            g = gr_ref[rows, vs]
            o_ref[rows, vs] = (_rms(o, gn_ref[...]) * (g * _sigmoid(g))).astype(o_ref.dtype)


def _gla(u, wg, bg, gn):
    nj = SEQ // TC_GLA
    row = lambda b, j: b * nj + j
    return pl.pallas_call(
        _gla_body,
        grid=(BATCH, nj),
        in_specs=[
            pl.BlockSpec((TC_GLA, GLA_QK), lambda b, j: (row(b, j), COL_GQ // GLA_QK)),
            pl.BlockSpec((TC_GLA, GLA_QK), lambda b, j: (row(b, j), COL_GK // GLA_QK)),
            pl.BlockSpec((TC_GLA, GLA_V), lambda b, j: (row(b, j), COL_GV // GLA_V)),
            pl.BlockSpec((TC_GLA, GLA_V), lambda b, j: (row(b, j), COL_GR // GLA_V)),
            pl.BlockSpec((TC_GLA, LANES), lambda b, j: (row(b, j), COL_GLR // LANES)),
            pl.BlockSpec((LANES, GLA_QK), lambda b, j: (0, 0)),
            pl.BlockSpec((1, GLA_QK), lambda b, j: (0, 0)),
            pl.BlockSpec((1, GLA_DV), lambda b, j: (0, 0)),
        ],
        out_specs=pl.BlockSpec((TC_GLA, GLA_V), lambda b, j: (row(b, j), 0)),
        out_shape=jax.ShapeDtypeStruct((TOKENS, GLA_V), BF16),
        scratch_shapes=[pltpu.VMEM((GLA_HEADS, GLA_DK, GLA_DV), F32)],
        compiler_params=_params(("arbitrary", "arbitrary")),
        name="gla",
    )(u, u, u, u, u, wg, bg, gn)


def _conv_body(cin_ref, cb_ref, cc_ref, w_ref, o_ref):
    uu = cc_ref[...] * cin_ref[...]
    t = lax.broadcasted_iota(jnp.int32, uu.shape, 0)
    y = uu * w_ref[CONV_WIDTH - 1:CONV_WIDTH, :]
    for shift in range(1, CONV_WIDTH):
        prev = jnp.where(t >= shift, pltpu.roll(uu, shift, axis=0), 0.0)
        y = y + prev * w_ref[CONV_WIDTH - 1 - shift:CONV_WIDTH - shift, :]
    o_ref[...] = (cb_ref[...] * y).astype(o_ref.dtype)


def _conv(u, w):
    spec = lambda col: pl.BlockSpec((SEQ, CONV_CH), lambda b: (b, col // CONV_CH))
    return pl.pallas_call(
        _conv_body,
        grid=(BATCH,),
        in_specs=[spec(COL_CIN), spec(COL_CB), spec(COL_CC),
                  pl.BlockSpec((8, CONV_CH), lambda b: (0, 0))],
        out_specs=pl.BlockSpec((SEQ, CONV_CH), lambda b: (b, 0)),
        out_shape=jax.ShapeDtypeStruct((TOKENS, CONV_CH), BF16),
        compiler_params=_params(("parallel",)),
        name="conv",
    )(u, u, u, w)


def _outproj_body(*refs, route):
    (o1, o4, o16, l1, l4, l16, go_ref, cv_ref, x_ref, w_ref, g_ref) = refs[:11]
    perm_ref = refs[-1]
    if route:
        wr_ref, xo_ref, ho_ref, route_ref, counts_ref, carry_ref = refs[11:-1]
    else:
        xo_ref, ho_ref = refs[11:-1]

    def token_order(ref, slot):
        dilation, rows, _ = ref.shape
        if dilation == 1:
            return ref[0].astype(F32)
        tiles = range(ATT_WIDTH // LANES)
        for r in range(dilation):
            val = ref[r].astype(F32)
            for t in tiles:
                perm_ref[slot, t, pl.ds(r, rows, stride=dilation), :] = val[:, t * LANES:(t + 1) * LANES]
        return jnp.concatenate([perm_ref[slot, t] for t in tiles], axis=-1)

    la, lb, lc = token_order(l1, 0), token_order(l4, 0), token_order(l16, 1)
    oa, ob, oc = token_order(o1, 0), token_order(o4, 2), token_order(o16, 3)
    m = jnp.maximum(jnp.maximum(la, lb), lc)
    ea, eb, ec = jnp.exp(la - m), jnp.exp(lb - m), jnp.exp(lc - m)
    att = (ea * oa + eb * ob + ec * oc) / (ea + eb + ec)
    y = (x_ref[...]
         + _dot(att.astype(BF16), w_ref[0:ATT_WIDTH, :])
         + _dot(go_ref[...], w_ref[ATT_WIDTH:ATT_WIDTH + GLA_V, :])
         + _dot(cv_ref[...], w_ref[ATT_WIDTH + GLA_V:MIX_WIDTH, :]))
    xo_ref[...] = y
    hf = _rms(y, g_ref[...])
    ho_ref[...] = hf.astype(ho_ref.dtype)
    if route:
        @pl.when(pl.program_id(0) == 0)
        def _():
            carry_ref[...] = jnp.zeros_like(carry_ref)

        tm = hf.shape[0]
        logits = _dot3(hf, wr_ref[...])
        lane = lax.broadcasted_iota(jnp.int32, logits.shape, 1).astype(F32)
        lg = jnp.where(lane < N_EXPERTS, logits, -jnp.inf)
        v1 = jnp.max(lg, axis=-1, keepdims=True)
        i1 = jnp.min(jnp.where(lg == v1, lane, float(LANES)), axis=-1, keepdims=True)
        lg2 = jnp.where(lane == i1, -jnp.inf, lg)
        v2 = jnp.max(lg2, axis=-1, keepdims=True)
        i2 = jnp.min(jnp.where(lg2 == v2, lane, float(LANES)), axis=-1, keepdims=True)
        e2 = jnp.exp(v2 - v1)
        w1 = 1.0 / (1.0 + e2)
        w2 = e2 * w1
        sel1 = lane == i1
        sel2 = lane == i2
        onehot = jnp.where(sel1, 1.0, jnp.where(sel2, 1.0, 0.0))
        tri = (lax.broadcasted_iota(jnp.int32, (tm, tm), 0) >= lax.broadcasted_iota(jnp.int32, (tm, tm), 1))
        csum = _dot(jnp.where(tri, 1.0, 0.0).astype(BF16), onehot.astype(BF16))
        carry = carry_ref[0:1, :]
        rank = csum - onehot + carry
        r1 = jnp.sum(jnp.where(sel1, rank, 0.0), axis=-1, keepdims=True)
        r2 = jnp.sum(jnp.where(sel2, rank, 0.0), axis=-1, keepdims=True)
        total = jnp.broadcast_to(carry + csum[tm - 1:tm, :], carry_ref.shape)
        carry_ref[...] = total
        counts_ref[...] = total
        cols = {ROUTE_I1: i1, ROUTE_I2: i2, ROUTE_W1: w1, ROUTE_W2: w2, ROUTE_R1: r1, ROUTE_R2: r2}
        packed = jnp.zeros_like(logits)
        for c, val in cols.items():
            packed = jnp.where(lane == float(c), val, packed)
        route_ref[...] = packed


def _outproj(att, go, cv, x, w, g, w_router=None):
    route = w_router is not None
    tm = TM_PROJ
    tile = lambda cols: pl.BlockSpec((tm, cols), lambda i: (i, 0))
    full = lambda a: pl.BlockSpec(a.shape, lambda i: (0, 0))
    (o1, l1), (o4, l4), (o16, l16) = att
    args = [o1, o4, o16, l1, l4, l16, go, cv, x, w, g]
    att_specs = [_subseq_spec(d, ATT_WIDTH) for _, d in DILATED_PATTERNS]
    in_specs = att_specs * 2 + [tile(GLA_V), tile(CONV_CH), tile(D_MODEL), full(w), full(g)]
    out_specs = [tile(D_MODEL), tile(D_MODEL)]
    out_shape = [jax.ShapeDtypeStruct((TOKENS, D_MODEL), F32),
                 jax.ShapeDtypeStruct((TOKENS, D_MODEL), F32 if route else BF16)]
    scratch = []
    if route:
        args.append(w_router)
        in_specs.append(full(w_router))
        out_specs += [tile(LANES), pl.BlockSpec((8, LANES), lambda i: (0, 0))]
        out_shape += [jax.ShapeDtypeStruct((TOKENS, LANES), F32), jax.ShapeDtypeStruct((8, LANES), F32)]
        scratch = [pltpu.VMEM((8, LANES), F32)]
    scratch.append(pltpu.VMEM((4, ATT_WIDTH // LANES, tm, LANES), F32))
    return pl.pallas_call(
        functools.partial(_outproj_body, route=route),
        grid=(TOKENS // tm,),
        in_specs=in_specs,
        out_specs=out_specs,
        out_shape=out_shape,
        scratch_shapes=scratch,
        compiler_params=_params(("arbitrary",) if route else ("parallel",)),
        name="outproj_route" if route else "outproj",
    )(*args)


def _swiglu_accumulate(h, w1_ref, w3_ref, w2_ref, acc_ref, tf):
    for c0 in range(0, tf, FFN_SUB):
        c1 = min(c0 + FFN_SUB, tf)
        a = _dot(h, w1_ref[:, c0:c1])
        b = _dot(h, w3_ref[:, c0:c1])
        act = a * _sigmoid(a) * b
        acc_ref[...] += _dot(act.astype(BF16), w2_ref[c0:c1, :])


def _ffn_body(x_ref, h_ref, w1_ref, w3_ref, w2_ref, o_ref, *, tf):
    @pl.when(pl.program_id(1) == 0)
    def _():
        o_ref[...] = x_ref[...]

    _swiglu_accumulate(h_ref[...], w1_ref, w3_ref, w2_ref, o_ref, tf)


def _ffn(x, h, w1, w3, w2, *, tf):
    f = w1.shape[1]
    tm = TM_FFN
    tile = lambda cols: pl.BlockSpec((tm, cols), lambda i, j: (i, 0))
    return pl.pallas_call(
        functools.partial(_ffn_body, tf=tf),
        grid=(TOKENS // tm, f // tf),
        in_specs=[tile(D_MODEL), tile(D_MODEL),
                  pl.BlockSpec((D_MODEL, tf), lambda i, j: (0, j)),
                  pl.BlockSpec((D_MODEL, tf), lambda i, j: (0, j)),
                  pl.BlockSpec((tf, D_MODEL), lambda i, j: (j, 0))],
        out_specs=tile(D_MODEL),
        out_shape=jax.ShapeDtypeStruct((TOKENS, D_MODEL), F32),
        compiler_params=_params(("parallel", "arbitrary")),
        name="dense_ffn",
    )(x, h, w1, w3, w2)


def _dispatch_body(pos_ref, h_ref, xs_init, xs_hbm, stage_ref, sem):
    del xs_init
    i = pl.program_id(0)
    n = pl.num_programs(0)
    tm = DISPATCH_CHUNK
    slot = i % 2

    def drain(s):
        for _ in range(2):
            pltpu.make_async_copy(stage_ref.at[s], xs_hbm.at[pl.ds(0, tm)], sem.at[s]).wait()

    @pl.when(i >= 2)
    def _():
        drain(slot)

    stage_ref[slot] = h_ref[...]

    def body(t, carry):
        for k in range(2):
            dst = pos_ref[2 * (i * tm + t) + k]
            pltpu.make_async_copy(stage_ref.at[slot, pl.ds(t, 1)], xs_hbm.at[pl.ds(dst, 1)], sem.at[slot]).start()
        return carry
    lax.fori_loop(0, tm, body, 0, unroll=8)

    @pl.when(i == n - 1)
    def _():
        drain(1 - slot)
        drain(slot)


def _dispatch(pos, h):
    xs_init = jnp.zeros((N_SORTED, D_MODEL), F32)
    tm = DISPATCH_CHUNK
    return pl.pallas_call(
        _dispatch_body,
        grid=(TOKENS // tm,),
        in_specs=[pl.BlockSpec(memory_space=pltpu.SMEM),
                  pl.BlockSpec((tm, D_MODEL), lambda i: (i, 0)),
                  pl.BlockSpec(memory_space=pl.ANY)],
        out_specs=pl.BlockSpec(memory_space=pl.ANY),
        out_shape=jax.ShapeDtypeStruct((N_SORTED, D_MODEL), F32),
        scratch_shapes=[pltpu.VMEM((2, tm, D_MODEL), F32), pltpu.SemaphoreType.DMA((2,))],
        input_output_aliases={2: 0},
        compiler_params=_params(("arbitrary",)),
        name="moe_dispatch",
    )(pos, h, xs_init)


def _gffn_body(te_ref, nu_ref, xs_ref, w1_ref, w3_ref, w2_ref, o_ref, hb_ref):
    del te_ref
    i = pl.program_id(0)
    j = pl.program_id(1)

    @pl.when(j == 0)
    def _():
        hb_ref[...] = xs_ref[...].astype(BF16)
        o_ref[...] = jnp.zeros_like(o_ref)

    @pl.when(i < nu_ref[0])
    def _():
        _swiglu_accumulate(hb_ref[...], w1_ref, w3_ref, w2_ref, o_ref, TF_MOE)


def _grouped_ffn(tile_expert, n_used, xs, w1, w3, w2):
    nj = FFN_EXPERT // TF_MOE
    col = lambda i, j, nu: jnp.where(i < nu[0], j, nj - 1)
    grid_spec = pltpu.PrefetchScalarGridSpec(
        num_scalar_prefetch=2,
        grid=(N_TILES_MOE, nj),
        in_specs=[
            pl.BlockSpec((TM_MOE, D_MODEL), lambda i, j, te, nu: (i, 0)),
            pl.BlockSpec((None, D_MODEL, TF_MOE), lambda i, j, te, nu: (te[i], 0, col(i, j, nu))),
            pl.BlockSpec((None, D_MODEL, TF_MOE), lambda i, j, te, nu: (te[i], 0, col(i, j, nu))),
            pl.BlockSpec((None, TF_MOE, D_MODEL), lambda i, j, te, nu: (te[i], col(i, j, nu), 0)),
        ],
        out_specs=pl.BlockSpec((TM_MOE, D_MODEL), lambda i, j, te, nu: (i, 0)),
        scratch_shapes=[pltpu.VMEM((TM_MOE, D_MODEL), BF16)],
    )
    return pl.pallas_call(
        _gffn_body,
        grid_spec=grid_spec,
        out_shape=jax.ShapeDtypeStruct((N_SORTED, D_MODEL), F32),
        compiler_params=_params(("arbitrary", "arbitrary")),
        name="moe_ffn",
    )(tile_expert, n_used, xs, w1, w3, w2)


def _combine_body(pos_ref, x_ref, route_ref, g_ref, ys_hbm, o_ref, buf_ref, sem):
    i = pl.program_id(0)
    n = pl.num_programs(0)
    tm = TM_COMBINE

    def issue(tile, slot):
        def body(t, carry):
            for k in range(2):
                src = pos_ref[2 * (tile * tm + t) + k]
                pltpu.make_async_copy(ys_hbm.at[pl.ds(src, 1)], buf_ref.at[slot, k, pl.ds(t, 1)],
                                      sem.at[slot]).start()
            return carry
        lax.fori_loop(0, tm, body, 0, unroll=8)

    @pl.when(i == 0)
    def _():
        issue(0, 0)

    @pl.when(i + 1 < n)
    def _():
        issue(i + 1, (i + 1) % 2)

    slot = i % 2
    for k in range(2):
        pltpu.make_async_copy(ys_hbm.at[pl.ds(0, tm)], buf_ref.at[slot, k], sem.at[slot]).wait()
    r = route_ref[...]
    lane = lax.broadcasted_iota(jnp.int32, r.shape, 1)
    w1 = jnp.sum(jnp.where(lane == ROUTE_W1, r, 0.0), axis=-1, keepdims=True)
    w2 = jnp.sum(jnp.where(lane == ROUTE_W2, r, 0.0), axis=-1, keepdims=True)
    y = x_ref[...] + w1 * buf_ref[slot, 0] + w2 * buf_ref[slot, 1]
    o_ref[...] = _rms(y, g_ref[...])


def _combine(pos, x, route, g, ys):
    tm = TM_COMBINE
    return pl.pallas_call(
        _combine_body,
        grid=(TOKENS // tm,),
        in_specs=[pl.BlockSpec(memory_space=pltpu.SMEM),
                  pl.BlockSpec((tm, D_MODEL), lambda i: (i, 0)),
                  pl.BlockSpec((tm, LANES), lambda i: (i, 0)),
                  pl.BlockSpec((1, D_MODEL), lambda i: (0, 0)),
                  pl.BlockSpec(memory_space=pl.ANY)],
        out_specs=pl.BlockSpec((tm, D_MODEL), lambda i: (i, 0)),
        out_shape=jax.ShapeDtypeStruct((TOKENS, D_MODEL), F32),
        scratch_shapes=[pltpu.VMEM((2, 2, tm, D_MODEL), F32), pltpu.SemaphoreType.DMA((2,))],
        compiler_params=_params(("arbitrary",)),
        name="moe_combine",
    )(pos, x, route, g, ys)


def _routing_tables(route, counts):
    cnt = counts[0, :N_EXPERTS].astype(jnp.int32)
    tiles = (cnt + TM_MOE - 1) // TM_MOE
    tile_end = jnp.cumsum(tiles)
    tile_start = tile_end - tiles
    n_used = tile_end[-1]
    expert = route[:, ROUTE_I1:ROUTE_I2 + 1].astype(jnp.int32)
    rank = route[:, ROUTE_R1:ROUTE_R2 + 1].astype(jnp.int32)
    pos = (tile_start * TM_MOE)[expert] + rank
    tile_id = jnp.minimum(jnp.arange(N_TILES_MOE, dtype=jnp.int32), n_used - 1)
    tile_expert = jnp.sum(tile_id[:, None] >= tile_end[None, :], axis=1).astype(jnp.int32)
    return pos.reshape(2 * TOKENS), tile_expert, n_used.reshape(1)


def _moe(x, h, route, counts, w1, w3, w2, g_final):
    pos, tile_expert, n_used = _routing_tables(route, counts)
    xs = _dispatch(pos, h)
    ys = _grouped_ffn(tile_expert, n_used, xs, w1, w3, w2)
    return _combine(pos, x, route, g_final, ys)


def _prep_w_in(w):
    aq, ak, av, gq, gk, gv, gr, glr, c_in, c_b, c_c = jnp.split(w, np.cumsum(SPLIT_SIZES)[:-1].tolist(), axis=1)
    pad = jnp.zeros((D_MODEL, LANES - GLA_RANK), w.dtype)
    return jnp.concatenate([aq, ak, av, gv, gr, gq, gk, c_in, c_b, c_c, glr, pad], axis=1).astype(BF16)


def kernel(x, w_mix_in, w_mix_out, g_mix, rel_bias, gla_w_gate, gla_b_gate, gla_g_norm, conv_w,
           g_ffn, ffn_w1, ffn_w3, ffn_w2, moe_router, moe_w1, moe_w3, moe_w2, g_final):
    assert DEPTH == 2
    x = x.reshape(TOKENS, D_MODEL)
    for layer in range(DEPTH):
        u, *qkvs = _inproj(x, g_mix[layer].reshape(1, D_MODEL), _prep_w_in(w_mix_in[layer]))
        att = _attention(qkvs, rel_bias)
        wg = jnp.pad(gla_w_gate[layer], ((0, LANES - GLA_RANK), (0, 0)))
        go = _gla(u, wg, gla_b_gate[layer].reshape(1, GLA_QK), gla_g_norm[layer].reshape(1, GLA_DV))
        cv = _conv(u, jnp.pad(conv_w[layer], ((0, 8 - CONV_WIDTH), (0, 0))))
        w_out = w_mix_out[layer].astype(BF16)
        g2 = g_ffn[layer].reshape(1, D_MODEL)
        i = layer // 2
        if layer % 2 == 0:
            x, h = _outproj(att, go, cv, x, w_out, g2)
            x = _ffn(x, h, ffn_w1[i].astype(BF16), ffn_w3[i].astype(BF16), ffn_w2[i].astype(BF16), tf=TF_DENSE)
        else:
            wr = jnp.pad(moe_router[i], ((0, 0), (0, LANES - N_EXPERTS)))
            x, h, route, counts = _outproj(att, go, cv, x, w_out, g2, wr)
            x = _moe(x, h, route, counts, moe_w1[i].astype(BF16), moe_w3[i].astype(BF16),
                     moe_w2[i].astype(BF16), g_final.reshape(1, D_MODEL))
    return x.reshape(BATCH, SEQ, D_MODEL)
```

```python
import functools
import math

import jax
import jax.numpy as jnp
import numpy as np
from jax import lax
from jax.experimental import pallas as pl
from jax.experimental.pallas import tpu as pltpu

F32 = jnp.float32
BF16 = jnp.bfloat16

D_MODEL = 1024
BATCH = 8
SEQ = 2048
TOKENS = BATCH * SEQ
DEPTH = 2
EPS = 1e-6

HEAD_DIM = 64
ATT_HEADS = 4
ATT_WIDTH = ATT_HEADS * HEAD_DIM
DILATED_PATTERNS = ((128, 1), (512, 4), (2048, 16))
ATT_BLOCK = 128
REL_BUCKETS = 32
REL_MAX_DISTANCE = 2048

GLA_HEADS = 4
GLA_DK = 64
GLA_DV = 128
GLA_RANK = 16
GLA_CHUNK = 64
GLA_QK = GLA_HEADS * GLA_DK
GLA_V = GLA_HEADS * GLA_DV

CONV_CH = 256
CONV_WIDTH = 3
MIX_WIDTH = ATT_WIDTH + GLA_V + CONV_CH

SPLIT_SIZES = (ATT_WIDTH, ATT_WIDTH, ATT_WIDTH, GLA_QK, GLA_QK, GLA_V, GLA_V, GLA_RANK,
               CONV_CH, CONV_CH, CONV_CH)

FFN_DENSE = 2816
N_EXPERTS = 8
FFN_EXPERT = 3584

LANES = 128
MXU_WIDTH = 256
VMEM_LIMIT = 56 * 1024 * 1024

QKV_COLS = 3 * ATT_WIDTH
COL_GV, COL_GR, COL_GQ, COL_GK = 0, 512, 1024, 1280
COL_CIN, COL_CB, COL_CC, COL_GLR = 1536, 1792, 2048, 2304
U_COLS = COL_GLR + LANES

NEG_BIG = -1e30

ATT_INFLIGHT = 4
TM_PROJ = 512
TC_GLA = 512
TM_FFN = 1024
TF_DENSE = 1408
FFN_SUB = 512
TM_MOE = 512
TF_MOE = 1792
N_TILES_MOE = 2 * TOKENS // TM_MOE + N_EXPERTS
N_SORTED = N_TILES_MOE * TM_MOE
DISPATCH_CHUNK = 256
TM_COMBINE = 256

ROUTE_I1, ROUTE_I2, ROUTE_W1, ROUTE_W2, ROUTE_R1, ROUTE_R2 = range(6)


def _params(sem):
    return pltpu.CompilerParams(dimension_semantics=sem, vmem_limit_bytes=VMEM_LIMIT)


def _split_bf16(a):
    hi = a.astype(BF16)
    lo = (a - hi.astype(F32)).astype(BF16)
    return hi, lo


def _dot(a, b):
    return jnp.dot(a, b, preferred_element_type=F32)


def _dot3(a, b):
    a_hi, a_lo = _split_bf16(a)
    b_hi, b_lo = _split_bf16(b)
    return _dot(a_hi, b_hi) + _dot(a_lo, b_hi) + _dot(a_hi, b_lo)


def _dot_nt(a, b):
    return lax.dot_general(a, b, (((1,), (1,)), ((), ())), preferred_element_type=F32)


def _dot_tn(a, b):
    return lax.dot_general(a, b, (((0,), (0,)), ((), ())), preferred_element_type=F32)


def _rms(x, g):
    ms = jnp.mean(x * x, axis=-1, keepdims=True)
    return x * lax.rsqrt(ms + EPS) * g


def _sigmoid(x):
    return 1.0 / (1.0 + jnp.exp(-x))


def _inproj_body(x_ref, g_ref, w_ref, o_ref, *rest):
    qkv_refs, qkv_f32 = rest[:-1], rest[-1]
    h = _rms(x_ref[...], g_ref[...]).astype(BF16)
    for c0 in range(0, QKV_COLS, MXU_WIDTH):
        res = _dot(h, w_ref[:, c0:c0 + MXU_WIDTH])
        for t in range(MXU_WIDTH // LANES):
            qkv_f32[c0 // LANES + t] = res[:, t * LANES:(t + 1) * LANES]
    for (_, dilation), ref in zip(DILATED_PATTERNS, qkv_refs):
        for r in range(dilation):
            rows = pl.ds(r, TM_PROJ // dilation, stride=dilation)
            ref[r] = jnp.concatenate([qkv_f32[t, rows, :] for t in range(QKV_COLS // LANES)],
                                     axis=-1).astype(ref.dtype)
    for c0 in range(0, U_COLS, MXU_WIDTH):
        c1 = min(c0 + MXU_WIDTH, U_COLS)
        o_ref[:, c0:c1] = _dot(h, w_ref[:, QKV_COLS + c0:QKV_COLS + c1])


def _subseq_spec(dilation, cols):
    tiles = SEQ // TM_PROJ
    return pl.BlockSpec((None, dilation, TM_PROJ // dilation, cols), lambda i: (i // tiles, 0, i % tiles, 0))


def _inproj(x, g, w):
    qkv_shapes = [jax.ShapeDtypeStruct((BATCH, d, SEQ // d, QKV_COLS), BF16) for _, d in DILATED_PATTERNS]
    return pl.pallas_call(
        _inproj_body,
        grid=(TOKENS // TM_PROJ,),
        in_specs=[
            pl.BlockSpec((TM_PROJ, D_MODEL), lambda i: (i, 0)),
            pl.BlockSpec((1, D_MODEL), lambda i: (0, 0)),
            pl.BlockSpec((D_MODEL, QKV_COLS + U_COLS), lambda i: (0, 0)),
        ],
        out_specs=[pl.BlockSpec((TM_PROJ, U_COLS), lambda i: (i, 0))]
                  + [_subseq_spec(d, QKV_COLS) for _, d in DILATED_PATTERNS],
        out_shape=[jax.ShapeDtypeStruct((TOKENS, U_COLS), F32)] + qkv_shapes,
        scratch_shapes=[pltpu.VMEM((QKV_COLS // LANES, TM_PROJ, LANES), F32)],
        compiler_params=_params(("parallel",)),
        name="inproj",
    )(x, g, w)


def _rel_bucket(dist):
    max_exact = REL_BUCKETS // 2
    d = jnp.maximum(dist, 0)
    log_ratio = jnp.log(jnp.maximum(d, 1).astype(F32) / max_exact) / math.log(REL_MAX_DISTANCE / max_exact)
    large = jnp.minimum(max_exact + (log_ratio * (REL_BUCKETS - max_exact)).astype(jnp.int32), REL_BUCKETS - 1)
    return jnp.where(d < max_exact, d, large)


def _bucket_table(window, dilation):
    span = window // dilation
    qi = jnp.arange(ATT_BLOCK)[:, None]
    kj = jnp.arange(2 * ATT_BLOCK)[None, :]
    sub_dist = qi - kj + ATT_BLOCK
    band = (sub_dist >= 0) & (sub_dist <= span)
    return jnp.where(band, _rel_bucket(sub_dist * dilation), -1).astype(jnp.int32)


def _attn_body(rb_ref, bidx_ref, qkv_ref, o_ref, lse_ref, bias_ref, *, nblk, unroll):
    @pl.when((pl.program_id(0) == 0) & (pl.program_id(1) == 0))
    def _():
        bidx = bidx_ref[...]
        for h in range(ATT_HEADS):
            acc = jnp.full(bidx.shape, NEG_BIG, F32)
            for b in range(REL_BUCKETS):
                acc = jnp.where(bidx == b, rb_ref[b, h], acc)
            bias_ref[h] = acc

    def block(sub, n, first):
        if first:
            rows = krows = slice(0, ATT_BLOCK)
        else:
            r0 = pl.multiple_of(n * ATT_BLOCK, ATT_BLOCK)
            rows = pl.ds(r0, ATT_BLOCK)
            krows = pl.ds(r0 - ATT_BLOCK, 2 * ATT_BLOCK)
        q = qkv_ref[sub, rows, 0:ATT_WIDTH]
        kk = qkv_ref[sub, krows, ATT_WIDTH:2 * ATT_WIDTH]
        vv = qkv_ref[sub, krows, 2 * ATT_WIDTH:3 * ATT_WIDTH]
        q = q * jnp.asarray(HEAD_DIM ** -0.5, BF16)
        head_of_lane = lax.broadcasted_iota(jnp.int32, (ATT_BLOCK, ATT_WIDTH), 1) // HEAD_DIM
        ones = jnp.ones((kk.shape[0], LANES), BF16)
        num = den = mx = None
        for h in range(ATT_HEADS):
            mine = head_of_lane == h
            bias = bias_ref[h, :, ATT_BLOCK:] if first else bias_ref[h]
            s = _dot_nt(jnp.where(mine, q, jnp.zeros_like(q)), kk) + bias
            m = jnp.max(s, axis=-1, keepdims=True)
            p = jnp.exp(s - m).astype(BF16)
            num_h = _dot(p, vv)
            den_h = jnp.tile(_dot(p, ones), (1, ATT_WIDTH // LANES))
            m_h = jnp.broadcast_to(m, (ATT_BLOCK, ATT_WIDTH))
            num = num_h if h == 0 else jnp.where(mine, num_h, num)
            den = den_h if h == 0 else jnp.where(mine, den_h, den)
            mx = m_h if h == 0 else jnp.where(mine, m_h, mx)
        o_ref[sub, rows, :] = (num / den).astype(o_ref.dtype)
        lse_ref[sub, rows, :] = mx + jnp.log(den)

    subs = range(qkv_ref.shape[0])
    for sub in subs:
        block(sub, 0, True)
    if nblk > 1:
        def loop_body(n, carry):
            for sub in subs:
                block(sub, n, False)
            return carry
        lax.fori_loop(1, nblk, loop_body, 0, unroll=unroll)


def _attention_pattern(ua, rel_bias, window, dilation):
    L = SEQ // dilation
    nblk = L // ATT_BLOCK
    per_step = min(dilation, ATT_INFLIGHT)
    unroll = max(1, ATT_INFLIGHT // per_step)
    while (nblk - 1) % unroll:
        unroll -= 1
    qkv_spec = pl.BlockSpec((None, per_step, L, QKV_COLS), lambda b, r: (b, r, 0, 0))
    out_spec = pl.BlockSpec((None, per_step, L, ATT_WIDTH), lambda b, r: (b, r, 0, 0))
    return pl.pallas_call(
        functools.partial(_attn_body, nblk=nblk, unroll=unroll),
        grid=(BATCH, dilation // per_step),
        in_specs=[
            pl.BlockSpec(memory_space=pltpu.SMEM),
            pl.BlockSpec((ATT_BLOCK, 2 * ATT_BLOCK), lambda b, r: (0, 0)),
            qkv_spec,
        ],
        out_specs=[out_spec, out_spec],
        out_shape=[jax.ShapeDtypeStruct((BATCH, dilation, L, ATT_WIDTH), BF16),
                   jax.ShapeDtypeStruct((BATCH, dilation, L, ATT_WIDTH), F32)],
        scratch_shapes=[pltpu.VMEM((ATT_HEADS, ATT_BLOCK, 2 * ATT_BLOCK), F32)],
        compiler_params=_params(("arbitrary", "arbitrary")),
        name=f"attn_d{dilation}",
    )(rel_bias, _bucket_table(window, dilation), ua)


def _attention(qkvs, rel_bias):
    return [_attention_pattern(ua, rel_bias, window, dilation)
            for ua, (window, dilation) in zip(qkvs, DILATED_PATTERNS)]


def _gla_body(q_ref, k_ref, v_ref, gr_ref, glr_ref, wg_ref, bg_ref, gn_ref, o_ref, s_ref):
    @pl.when(pl.program_id(1) == 0)
    def _():
        s_ref[...] = jnp.zeros_like(s_ref)

    C = GLA_CHUNK
    row = lax.broadcasted_iota(jnp.int32, (C, C), 0)
    col = lax.broadcasted_iota(jnp.int32, (C, C), 1)
    tril = row >= col
    tril_bf = jnp.where(tril, 1.0, 0.0).astype(BF16)
    ones_bf = jnp.ones((C, GLA_DV), BF16)

    xg = _dot3(glr_ref[...], wg_ref[...]) + bg_ref[...]
    la_all = (jnp.minimum(xg, 0.0) - jnp.log(1.0 + jnp.exp(-jnp.abs(xg)))) * (1.0 / 16.0)

    for c in range(TC_GLA // C):
        rows = slice(c * C, (c + 1) * C)
        la_hi, la_lo = _split_bf16(la_all[rows])
        cum = _dot(tril_bf, la_hi) + _dot(tril_bf, la_lo)
        last = cum[C - 1:C, :]
        q = q_ref[rows, :]
        k = k_ref[rows, :]
        qt = (q * jnp.exp(cum) * (GLA_DK ** -0.5)).astype(BF16)
        kt = (k * jnp.exp(-cum)).astype(BF16)
        kl = (k * jnp.exp(last - cum)).astype(BF16)
        for h in range(GLA_HEADS):
            sl = slice(h * GLA_DK, (h + 1) * GLA_DK)
            vs = slice(h * GLA_DV, (h + 1) * GLA_DV)
            vh = v_ref[rows, vs].astype(BF16)
            state = s_ref[h]
            st_hi, st_lo = _split_bf16(state)
            sc = jnp.where(tril, _dot_nt(qt[:, sl], kt[:, sl]), 0.0).astype(BF16)
            o = _dot(qt[:, sl], st_hi) + _dot(qt[:, sl], st_lo) + _dot(sc, vh)
            ltot = _dot_tn(la_hi[:, sl], ones_bf) + _dot_tn(la_lo[:, sl], ones_bf)
            s_ref[h] = jnp.exp(ltot) * state + _dot_tn(kl[:, sl], vh)
            g = gr_ref[rows, vs]
            o_ref[rows, vs] = (_rms(o, gn_ref[...]) * (g * _sigmoid(g))).astype(o_ref.dtype)


def _gla(u, wg, bg, gn):
    nj = SEQ // TC_GLA
    row = lambda b, j: b * nj + j
    return pl.pallas_call(
        _gla_body,
        grid=(BATCH, nj),
        in_specs=[
            pl.BlockSpec((TC_GLA, GLA_QK), lambda b, j: (row(b, j), COL_GQ // GLA_QK)),
            pl.BlockSpec((TC_GLA, GLA_QK), lambda b, j: (row(b, j), COL_GK // GLA_QK)),
            pl.BlockSpec((TC_GLA, GLA_V), lambda b, j: (row(b, j), COL_GV // GLA_V)),
            pl.BlockSpec((TC_GLA, GLA_V), lambda b, j: (row(b, j), COL_GR // GLA_V)),
            pl.BlockSpec((TC_GLA, LANES), lambda b, j: (row(b, j), COL_GLR // LANES)),
            pl.BlockSpec((LANES, GLA_QK), lambda b, j: (0, 0)),
            pl.BlockSpec((1, GLA_QK), lambda b, j: (0, 0)),
            pl.BlockSpec((1, GLA_DV), lambda b, j: (0, 0)),
        ],
        out_specs=pl.BlockSpec((TC_GLA, GLA_V), lambda b, j: (row(b, j), 0)),
        out_shape=jax.ShapeDtypeStruct((TOKENS, GLA_V), BF16),
        scratch_shapes=[pltpu.VMEM((GLA_HEADS, GLA_DK, GLA_DV), F32)],
        compiler_params=_params(("arbitrary", "arbitrary")),
        name="gla",
    )(u, u, u, u, u, wg, bg, gn)


def _conv_body(cin_ref, cb_ref, cc_ref, w_ref, o_ref):
    uu = cc_ref[...] * cin_ref[...]
    t = lax.broadcasted_iota(jnp.int32, uu.shape, 0)
    y = uu * w_ref[CONV_WIDTH - 1:CONV_WIDTH, :]
    for shift in range(1, CONV_WIDTH):
        prev = jnp.where(t >= shift, pltpu.roll(uu, shift, axis=0), 0.0)
        y = y + prev * w_ref[CONV_WIDTH - 1 - shift:CONV_WIDTH - shift, :]
    o_ref[...] = (cb_ref[...] * y).astype(o_ref.dtype)


def _conv(u, w):
    spec = lambda col: pl.BlockSpec((SEQ, CONV_CH), lambda b: (b, col // CONV_CH))
    return pl.pallas_call(
        _conv_body,
        grid=(BATCH,),
        in_specs=[spec(COL_CIN), spec(COL_CB), spec(COL_CC),
                  pl.BlockSpec((8, CONV_CH), lambda b: (0, 0))],
        out_specs=pl.BlockSpec((SEQ, CONV_CH), lambda b: (b, 0)),
        out_shape=jax.ShapeDtypeStruct((TOKENS, CONV_CH), BF16),
        compiler_params=_params(("parallel",)),
        name="conv",
    )(u, u, u, w)


def _outproj_body(*refs, route):
    (o1, o4, o16, l1, l4, l16, go_ref, cv_ref, x_ref, w_ref, g_ref) = refs[:11]
    perm_ref = refs[-1]
    if route:
        wr_ref, xo_ref, ho_ref, route_ref, counts_ref, carry_ref = refs[11:-1]
    else:
        xo_ref, ho_ref = refs[11:-1]

    def token_order(ref, slot):
        dilation, rows, _ = ref.shape
        if dilation == 1:
            return ref[0].astype(F32)
        tiles = range(ATT_WIDTH // LANES)
        for r in range(dilation):
            val = ref[r].astype(F32)
            for t in tiles:
                perm_ref[slot, t, pl.ds(r, rows, stride=dilation), :] = val[:, t * LANES:(t + 1) * LANES]
        return jnp.concatenate([perm_ref[slot, t] for t in tiles], axis=-1)

    la, lb, lc = token_order(l1, 0), token_order(l4, 0), token_order(l16, 1)
    oa, ob, oc = token_order(o1, 0), token_order(o4, 2), token_order(o16, 3)
    m = jnp.maximum(jnp.maximum(la, lb), lc)
    ea, eb, ec = jnp.exp(la - m), jnp.exp(lb - m), jnp.exp(lc - m)
    att = (ea * oa + eb * ob + ec * oc) / (ea + eb + ec)
    y = (x_ref[...]
         + _dot(att.astype(BF16), w_ref[0:ATT_WIDTH, :])
         + _dot(go_ref[...], w_ref[ATT_WIDTH:ATT_WIDTH + GLA_V, :])
         + _dot(cv_ref[...], w_ref[ATT_WIDTH + GLA_V:MIX_WIDTH, :]))
    xo_ref[...] = y
    hf = _rms(y, g_ref[...])
    ho_ref[...] = hf.astype(ho_ref.dtype)
    if route:
        @pl.when(pl.program_id(0) == 0)
        def _():
            carry_ref[...] = jnp.zeros_like(carry_ref)

        tm = hf.shape[0]
        logits = _dot3(hf, wr_ref[...])
        lane = lax.broadcasted_iota(jnp.int32, logits.shape, 1).astype(F32)
        lg = jnp.where(lane < N_EXPERTS, logits, -jnp.inf)
        v1 = jnp.max(lg, axis=-1, keepdims=True)
        i1 = jnp.min(jnp.where(lg == v1, lane, float(LANES)), axis=-1, keepdims=True)
        lg2 = jnp.where(lane == i1, -jnp.inf, lg)
        v2 = jnp.max(lg2, axis=-1, keepdims=True)
        i2 = jnp.min(jnp.where(lg2 == v2, lane, float(LANES)), axis=-1, keepdims=True)
        e2 = jnp.exp(v2 - v1)
        w1 = 1.0 / (1.0 + e2)
        w2 = e2 * w1
        sel1 = lane == i1
        sel2 = lane == i2
        onehot = jnp.where(sel1, 1.0, jnp.where(sel2, 1.0, 0.0))
        tri = (lax.broadcasted_iota(jnp.int32, (tm, tm), 0) >= lax.broadcasted_iota(jnp.int32, (tm, tm), 1))
        csum = _dot(jnp.where(tri, 1.0, 0.0).astype(BF16), onehot.astype(BF16))
        carry = carry_ref[0:1, :]
        rank = csum - onehot + carry
        r1 = jnp.sum(jnp.where(sel1, rank, 0.0), axis=-1, keepdims=True)
        r2 = jnp.sum(jnp.where(sel2, rank, 0.0), axis=-1, keepdims=True)
        total = jnp.broadcast_to(carry + csum[tm - 1:tm, :], carry_ref.shape)
        carry_ref[...] = total
        counts_ref[...] = total
        cols = {ROUTE_I1: i1, ROUTE_I2: i2, ROUTE_W1: w1, ROUTE_W2: w2, ROUTE_R1: r1, ROUTE_R2: r2}
        packed = jnp.zeros_like(logits)
        for c, val in cols.items():
            packed = jnp.where(lane == float(c), val, packed)
        route_ref[...] = packed


def _outproj(att, go, cv, x, w, g, w_router=None):
    route = w_router is not None
    tm = TM_PROJ
    tile = lambda cols: pl.BlockSpec((tm, cols), lambda i: (i, 0))
    full = lambda a: pl.BlockSpec(a.shape, lambda i: (0, 0))
    (o1, l1), (o4, l4), (o16, l16) = att
    args = [o1, o4, o16, l1, l4, l16, go, cv, x, w, g]
    att_specs = [_subseq_spec(d, ATT_WIDTH) for _, d in DILATED_PATTERNS]
    in_specs = att_specs * 2 + [tile(GLA_V), tile(CONV_CH), tile(D_MODEL), full(w), full(g)]
    out_specs = [tile(D_MODEL), tile(D_MODEL)]
    out_shape = [jax.ShapeDtypeStruct((TOKENS, D_MODEL), F32),
                 jax.ShapeDtypeStruct((TOKENS, D_MODEL), F32 if route else BF16)]
    scratch = []
    if route:
        args.append(w_router)
        in_specs.append(full(w_router))
        out_specs += [tile(LANES), pl.BlockSpec((8, LANES), lambda i: (0, 0))]
        out_shape += [jax.ShapeDtypeStruct((TOKENS, LANES), F32), jax.ShapeDtypeStruct((8, LANES), F32)]
        scratch = [pltpu.VMEM((8, LANES), F32)]
    scratch.append(pltpu.VMEM((4, ATT_WIDTH // LANES, tm, LANES), F32))
    return pl.pallas_call(
        functools.partial(_outproj_body, route=route),
        grid=(TOKENS // tm,),
        in_specs=in_specs,
        out_specs=out_specs,
        out_shape=out_shape,
        scratch_shapes=scratch,
        compiler_params=_params(("arbitrary",) if route else ("parallel",)),
        name="outproj_route" if route else "outproj",
    )(*args)


def _swiglu_accumulate(h, w1_ref, w3_ref, w2_ref, acc_ref, tf):
    for c0 in range(0, tf, FFN_SUB):
        c1 = min(c0 + FFN_SUB, tf)
        a = _dot(h, w1_ref[:, c0:c1])
        b = _dot(h, w3_ref[:, c0:c1])
        act = a * _sigmoid(a) * b
        acc_ref[...] += _dot(act.astype(BF16), w2_ref[c0:c1, :])


def _ffn_body(x_ref, h_ref, w1_ref, w3_ref, w2_ref, o_ref, *, tf):
    @pl.when(pl.program_id(1) == 0)
    def _():
        o_ref[...] = x_ref[...]

    _swiglu_accumulate(h_ref[...], w1_ref, w3_ref, w2_ref, o_ref, tf)


def _ffn(x, h, w1, w3, w2, *, tf):
    f = w1.shape[1]
    tm = TM_FFN
    tile = lambda cols: pl.BlockSpec((tm, cols), lambda i, j: (i, 0))
    return pl.pallas_call(
        functools.partial(_ffn_body, tf=tf),
        grid=(TOKENS // tm, f // tf),
        in_specs=[tile(D_MODEL), tile(D_MODEL),
                  pl.BlockSpec((D_MODEL, tf), lambda i, j: (0, j)),
                  pl.BlockSpec((D_MODEL, tf), lambda i, j: (0, j)),
                  pl.BlockSpec((tf, D_MODEL), lambda i, j: (j, 0))],
        out_specs=tile(D_MODEL),
        out_shape=jax.ShapeDtypeStruct((TOKENS, D_MODEL), F32),
        compiler_params=_params(("parallel", "arbitrary")),
        name="dense_ffn",
    )(x, h, w1, w3, w2)


ROW_TILE = D_MODEL // LANES


def _to_row_tiled(dst_ref, lead, val):
    rows = val.shape[0]
    for s in range(ROW_TILE):
        dst_ref[(*lead, pl.ds(s, rows, stride=ROW_TILE), slice(None))] = val[:, s * LANES:(s + 1) * LANES]


def _from_row_tiled(src_ref, lead, rows):
    return jnp.concatenate([src_ref[(*lead, pl.ds(s, rows, stride=ROW_TILE), slice(None))]
                            for s in range(ROW_TILE)], axis=-1)


def _row_tile(idx):
    return pl.ds(pl.multiple_of(idx * ROW_TILE, ROW_TILE), ROW_TILE)


def _dispatch_body(pos_ref, h_ref, xs_init, xs_hbm, stage_ref, sem):
    del xs_init
    i = pl.program_id(0)
    n = pl.num_programs(0)
    tm = DISPATCH_CHUNK
    slot = i % 2

    def drain(s):
        for _ in range(2):
            pltpu.make_async_copy(stage_ref.at[s], xs_hbm.at[pl.ds(0, tm * ROW_TILE)], sem.at[s]).wait()

    @pl.when(i >= 2)
    def _():
        drain(slot)

    _to_row_tiled(stage_ref, (slot,), h_ref[...])

    def body(t, carry):
        for k in range(2):
            dst = pos_ref[2 * (i * tm + t) + k]
            pltpu.make_async_copy(stage_ref.at[slot, _row_tile(t)], xs_hbm.at[_row_tile(dst)], sem.at[slot]).start()
        return carry
    lax.fori_loop(0, tm, body, 0, unroll=8)

    @pl.when(i == n - 1)
    def _():
        drain(1 - slot)
        drain(slot)


def _dispatch(pos, h):
    xs_init = jnp.zeros((N_SORTED * ROW_TILE, LANES), F32)
    tm = DISPATCH_CHUNK
    return pl.pallas_call(
        _dispatch_body,
        grid=(TOKENS // tm,),
        in_specs=[pl.BlockSpec(memory_space=pltpu.SMEM),
                  pl.BlockSpec((tm, D_MODEL), lambda i: (i, 0)),
                  pl.BlockSpec(memory_space=pl.ANY)],
        out_specs=pl.BlockSpec(memory_space=pl.ANY),
        out_shape=jax.ShapeDtypeStruct((N_SORTED * ROW_TILE, LANES), F32),
        scratch_shapes=[pltpu.VMEM((2, tm * ROW_TILE, LANES), F32), pltpu.SemaphoreType.DMA((2,))],
        input_output_aliases={2: 0},
        compiler_params=_params(("arbitrary",)),
        name="moe_dispatch",
    )(pos, h, xs_init)


def _gffn_body(te_ref, nu_ref, xs_ref, w1_ref, w3_ref, w2_ref, o_ref, hb_ref, acc_ref):
    del te_ref
    i = pl.program_id(0)
    j = pl.program_id(1)

    used = i < nu_ref[0]

    @pl.when(j == 0)
    def _():
        hb_ref[...] = _from_row_tiled(xs_ref, (), TM_MOE).astype(BF16)
        acc_ref[...] = jnp.zeros_like(acc_ref)

    @pl.when(used)
    def _():
        _swiglu_accumulate(hb_ref[...], w1_ref, w3_ref, w2_ref, acc_ref, TF_MOE)

    @pl.when(j == pl.num_programs(1) - 1)
    def _():
        _to_row_tiled(o_ref, (), acc_ref[...])


def _grouped_ffn(tile_expert, n_used, xs, w1, w3, w2):
    nj = FFN_EXPERT // TF_MOE
    col = lambda i, j, nu: jnp.where(i < nu[0], j, nj - 1)
    grid_spec = pltpu.PrefetchScalarGridSpec(
        num_scalar_prefetch=2,
        grid=(N_TILES_MOE, nj),
        in_specs=[
            pl.BlockSpec((TM_MOE * ROW_TILE, LANES), lambda i, j, te, nu: (i, 0)),
            pl.BlockSpec((None, D_MODEL, TF_MOE), lambda i, j, te, nu: (te[i], 0, col(i, j, nu))),
            pl.BlockSpec((None, D_MODEL, TF_MOE), lambda i, j, te, nu: (te[i], 0, col(i, j, nu))),
            pl.BlockSpec((None, TF_MOE, D_MODEL), lambda i, j, te, nu: (te[i], col(i, j, nu), 0)),
        ],
        out_specs=pl.BlockSpec((TM_MOE * ROW_TILE, LANES), lambda i, j, te, nu: (i, 0)),
        scratch_shapes=[pltpu.VMEM((TM_MOE, D_MODEL), BF16), pltpu.VMEM((TM_MOE, D_MODEL), F32)],
    )
    return pl.pallas_call(
        _gffn_body,
        grid_spec=grid_spec,
        out_shape=jax.ShapeDtypeStruct((N_SORTED * ROW_TILE, LANES), F32),
        compiler_params=_params(("arbitrary", "arbitrary")),
        name="moe_ffn",
    )(tile_expert, n_used, xs, w1, w3, w2)


def _combine_body(pos_ref, x_ref, route_ref, g_ref, ys_hbm, o_ref, buf_ref, sem):
    i = pl.program_id(0)
    n = pl.num_programs(0)
    tm = TM_COMBINE

    def issue(tile, slot):
        def body(t, carry):
            for k in range(2):
                src = pos_ref[2 * (tile * tm + t) + k]
                pltpu.make_async_copy(ys_hbm.at[_row_tile(src)], buf_ref.at[slot, k, _row_tile(t)],
                                      sem.at[slot]).start()
            return carry
        lax.fori_loop(0, tm, body, 0, unroll=8)

    @pl.when(i == 0)
    def _():
        issue(0, 0)

    @pl.when(i + 1 < n)
    def _():
        issue(i + 1, (i + 1) % 2)

    slot = i % 2
    for k in range(2):
        pltpu.make_async_copy(ys_hbm.at[pl.ds(0, tm * ROW_TILE)], buf_ref.at[slot, k], sem.at[slot]).wait()
    r = route_ref[...]
    lane = lax.broadcasted_iota(jnp.int32, r.shape, 1)
    w1 = jnp.sum(jnp.where(lane == ROUTE_W1, r, 0.0), axis=-1, keepdims=True)
    w2 = jnp.sum(jnp.where(lane == ROUTE_W2, r, 0.0), axis=-1, keepdims=True)
    y = x_ref[...] + w1 * _from_row_tiled(buf_ref, (slot, 0), tm) + w2 * _from_row_tiled(buf_ref, (slot, 1), tm)
    o_ref[...] = _rms(y, g_ref[...])


def _combine(pos, x, route, g, ys):
    tm = TM_COMBINE
    return pl.pallas_call(
        _combine_body,
        grid=(TOKENS // tm,),
        in_specs=[pl.BlockSpec(memory_space=pltpu.SMEM),
                  pl.BlockSpec((tm, D_MODEL), lambda i: (i, 0)),
                  pl.BlockSpec((tm, LANES), lambda i: (i, 0)),
                  pl.BlockSpec((1, D_MODEL), lambda i: (0, 0)),
                  pl.BlockSpec(memory_space=pl.ANY)],
        out_specs=pl.BlockSpec((tm, D_MODEL), lambda i: (i, 0)),
        out_shape=jax.ShapeDtypeStruct((TOKENS, D_MODEL), F32),
        scratch_shapes=[pltpu.VMEM((2, 2, tm * ROW_TILE, LANES), F32), pltpu.SemaphoreType.DMA((2,))],
        compiler_params=_params(("arbitrary",)),
        name="moe_combine",
    )(pos, x, route, g, ys)


def _routing_tables(route, counts):
    cnt = counts[0, :N_EXPERTS].astype(jnp.int32)
    tiles = (cnt + TM_MOE - 1) // TM_MOE
    tile_end = jnp.cumsum(tiles)
    tile_start = tile_end - tiles
    n_used = tile_end[-1]
    expert = route[:, ROUTE_I1:ROUTE_I2 + 1].astype(jnp.int32)
    rank = route[:, ROUTE_R1:ROUTE_R2 + 1].astype(jnp.int32)
    pos = (tile_start * TM_MOE)[expert] + rank
    tile_id = jnp.minimum(jnp.arange(N_TILES_MOE, dtype=jnp.int32), n_used - 1)
    tile_expert = jnp.sum(tile_id[:, None] >= tile_end[None, :], axis=1).astype(jnp.int32)
    return pos.reshape(2 * TOKENS), tile_expert, n_used.reshape(1)


def _moe(x, h, route, counts, w1, w3, w2, g_final):
    pos, tile_expert, n_used = _routing_tables(route, counts)
    xs = _dispatch(pos, h)
    ys = _grouped_ffn(tile_expert, n_used, xs, w1, w3, w2)
    return _combine(pos, x, route, g_final, ys)


def _prep_w_in(w):
    aq, ak, av, gq, gk, gv, gr, glr, c_in, c_b, c_c = jnp.split(w, np.cumsum(SPLIT_SIZES)[:-1].tolist(), axis=1)
    pad = jnp.zeros((D_MODEL, LANES - GLA_RANK), w.dtype)
    return jnp.concatenate([aq, ak, av, gv, gr, gq, gk, c_in, c_b, c_c, glr, pad], axis=1).astype(BF16)


def kernel(x, w_mix_in, w_mix_out, g_mix, rel_bias, gla_w_gate, gla_b_gate, gla_g_norm, conv_w,
           g_ffn, ffn_w1, ffn_w3, ffn_w2, moe_router, moe_w1, moe_w3, moe_w2, g_final):
    assert DEPTH == 2
    x = x.reshape(TOKENS, D_MODEL)
    for layer in range(DEPTH):
        u, *qkvs = _inproj(x, g_mix[layer].reshape(1, D_MODEL), _prep_w_in(w_mix_in[layer]))
        att = _attention(qkvs, rel_bias)
        wg = jnp.pad(gla_w_gate[layer], ((0, LANES - GLA_RANK), (0, 0)))
        go = _gla(u, wg, gla_b_gate[layer].reshape(1, GLA_QK), gla_g_norm[layer].reshape(1, GLA_DV))
        cv = _conv(u, jnp.pad(conv_w[layer], ((0, 8 - CONV_WIDTH), (0, 0))))
        w_out = w_mix_out[layer].astype(BF16)
        g2 = g_ffn[layer].reshape(1, D_MODEL)
        i = layer // 2
        if layer % 2 == 0:
            x, h = _outproj(att, go, cv, x, w_out, g2)
            x = _ffn(x, h, ffn_w1[i].astype(BF16), ffn_w3[i].astype(BF16), ffn_w2[i].astype(BF16), tf=TF_DENSE)
        else:
            wr = jnp.pad(moe_router[i], ((0, 0), (0, LANES - N_EXPERTS)))
            x, h, route, counts = _outproj(att, go, cv, x, w_out, g2, wr)
            x = _moe(x, h, route, counts, moe_w1[i].astype(BF16), moe_w3[i].astype(BF16),
                     moe_w2[i].astype(BF16), g_final.reshape(1, D_MODEL))
    return x.reshape(BATCH, SEQ, D_MODEL)
```

```python
import functools
import math

import jax
import jax.numpy as jnp
import numpy as np
from jax import lax
from jax.experimental import pallas as pl
from jax.experimental.pallas import tpu as pltpu

F32 = jnp.float32
BF16 = jnp.bfloat16

D_MODEL = 1024
BATCH = 8
SEQ = 2048
TOKENS = BATCH * SEQ
DEPTH = 2
EPS = 1e-6

HEAD_DIM = 64
ATT_HEADS = 4
ATT_WIDTH = ATT_HEADS * HEAD_DIM
DILATED_PATTERNS = ((128, 1), (512, 4), (2048, 16))
ATT_BLOCK = 128
REL_BUCKETS = 32
REL_MAX_DISTANCE = 2048

GLA_HEADS = 4
GLA_DK = 64
GLA_DV = 128
GLA_RANK = 16
GLA_CHUNK = 64
GLA_QK = GLA_HEADS * GLA_DK
GLA_V = GLA_HEADS * GLA_DV

CONV_CH = 256
CONV_WIDTH = 3
MIX_WIDTH = ATT_WIDTH + GLA_V + CONV_CH

SPLIT_SIZES = (ATT_WIDTH, ATT_WIDTH, ATT_WIDTH, GLA_QK, GLA_QK, GLA_V, GLA_V, GLA_RANK,
               CONV_CH, CONV_CH, CONV_CH)

FFN_DENSE = 2816
N_EXPERTS = 8
FFN_EXPERT = 3584

LANES = 128
MXU_WIDTH = 256
VMEM_LIMIT = 56 * 1024 * 1024

QKV_COLS = 3 * ATT_WIDTH
COL_GV, COL_GR, COL_GQ, COL_GK = 0, 512, 1024, 1280
COL_CIN, COL_CB, COL_CC, COL_GLR = 1536, 1792, 2048, 2304
U_COLS = COL_GLR + LANES

NEG_BIG = -1e30

ATT_INFLIGHT = 4
TM_PROJ = 512
TC_GLA = 512
TM_FFN = 1024
TF_DENSE = 1408
FFN_SUB = 512
TM_MOE = 512
TF_MOE = 1792
N_TILES_MOE = 2 * TOKENS // TM_MOE + N_EXPERTS
N_SORTED = N_TILES_MOE * TM_MOE
DISPATCH_CHUNK = 256
TM_COMBINE = 256

ROUTE_I1, ROUTE_I2, ROUTE_W1, ROUTE_W2, ROUTE_R1, ROUTE_R2 = range(6)


def _params(sem):
    return pltpu.CompilerParams(dimension_semantics=sem, vmem_limit_bytes=VMEM_LIMIT)


def _split_bf16(a):
    hi = a.astype(BF16)
    lo = (a - hi.astype(F32)).astype(BF16)
    return hi, lo


def _dot(a, b):
    return jnp.dot(a, b, preferred_element_type=F32)


def _dot3(a, b):
    a_hi, a_lo = _split_bf16(a)
    b_hi, b_lo = _split_bf16(b)
    return _dot(a_hi, b_hi) + _dot(a_lo, b_hi) + _dot(a_hi, b_lo)


def _dot_nt(a, b):
    return lax.dot_general(a, b, (((1,), (1,)), ((), ())), preferred_element_type=F32)


def _dot_tn(a, b):
    return lax.dot_general(a, b, (((0,), (0,)), ((), ())), preferred_element_type=F32)


def _rms(x, g):
    ms = jnp.mean(x * x, axis=-1, keepdims=True)
    return x * lax.rsqrt(ms + EPS) * g


def _sigmoid(x):
    return 1.0 / (1.0 + jnp.exp(-x))


def _inproj_body(x_ref, g_ref, w_ref, o_ref, *rest):
    qkv_refs, qkv_f32 = rest[:-1], rest[-1]
    h = _rms(x_ref[...], g_ref[...]).astype(BF16)
    for c0 in range(0, QKV_COLS, MXU_WIDTH):
        res = _dot(h, w_ref[:, c0:c0 + MXU_WIDTH])
        for t in range(MXU_WIDTH // LANES):
            qkv_f32[c0 // LANES + t] = res[:, t * LANES:(t + 1) * LANES]
    for (_, dilation), ref in zip(DILATED_PATTERNS, qkv_refs):
        for r in range(dilation):
            rows = pl.ds(r, TM_PROJ // dilation, stride=dilation)
            ref[r] = jnp.concatenate([qkv_f32[t, rows, :] for t in range(QKV_COLS // LANES)],
                                     axis=-1).astype(ref.dtype)
    for c0 in range(0, U_COLS, MXU_WIDTH):
        c1 = min(c0 + MXU_WIDTH, U_COLS)
        o_ref[:, c0:c1] = _dot(h, w_ref[:, QKV_COLS + c0:QKV_COLS + c1])


def _subseq_spec(dilation, cols):
    tiles = SEQ // TM_PROJ
    return pl.BlockSpec((None, dilation, TM_PROJ // dilation, cols), lambda i: (i // tiles, 0, i % tiles, 0))


def _inproj(x, g, w):
    qkv_shapes = [jax.ShapeDtypeStruct((BATCH, d, SEQ // d, QKV_COLS), BF16) for _, d in DILATED_PATTERNS]
    return pl.pallas_call(
        _inproj_body,
        grid=(TOKENS // TM_PROJ,),
        in_specs=[
            pl.BlockSpec((TM_PROJ, D_MODEL), lambda i: (i, 0)),
            pl.BlockSpec((1, D_MODEL), lambda i: (0, 0)),
            pl.BlockSpec((D_MODEL, QKV_COLS + U_COLS), lambda i: (0, 0)),
        ],
        out_specs=[pl.BlockSpec((TM_PROJ, U_COLS), lambda i: (i, 0))]
                  + [_subseq_spec(d, QKV_COLS) for _, d in DILATED_PATTERNS],
        out_shape=[jax.ShapeDtypeStruct((TOKENS, U_COLS), F32)] + qkv_shapes,
        scratch_shapes=[pltpu.VMEM((QKV_COLS // LANES, TM_PROJ, LANES), F32)],
        compiler_params=_params(("parallel",)),
        name="inproj",
    )(x, g, w)


def _rel_bucket(dist):
    max_exact = REL_BUCKETS // 2
    d = jnp.maximum(dist, 0)
    log_ratio = jnp.log(jnp.maximum(d, 1).astype(F32) / max_exact) / math.log(REL_MAX_DISTANCE / max_exact)
    large = jnp.minimum(max_exact + (log_ratio * (REL_BUCKETS - max_exact)).astype(jnp.int32), REL_BUCKETS - 1)
    return jnp.where(d < max_exact, d, large)


def _bucket_table(window, dilation):
    span = window // dilation
    qi = jnp.arange(ATT_BLOCK)[:, None]
    kj = jnp.arange(2 * ATT_BLOCK)[None, :]
    sub_dist = qi - kj + ATT_BLOCK
    band = (sub_dist >= 0) & (sub_dist <= span)
    return jnp.where(band, _rel_bucket(sub_dist * dilation), -1).astype(jnp.int32)


def _attn_body(rb_ref, bidx_ref, qkv_ref, o_ref, lse_ref, bias_ref, *, nblk, unroll):
    @pl.when((pl.program_id(0) == 0) & (pl.program_id(1) == 0))
    def _():
        bidx = bidx_ref[...]
        for h in range(ATT_HEADS):
            acc = jnp.full(bidx.shape, NEG_BIG, F32)
            for b in range(REL_BUCKETS):
                acc = jnp.where(bidx == b, rb_ref[b, h], acc)
            bias_ref[h] = acc

    def block(sub, n, first):
        if first:
            rows = krows = slice(0, ATT_BLOCK)
        else:
            r0 = pl.multiple_of(n * ATT_BLOCK, ATT_BLOCK)
            rows = pl.ds(r0, ATT_BLOCK)
            krows = pl.ds(r0 - ATT_BLOCK, 2 * ATT_BLOCK)
        q = qkv_ref[sub, rows, 0:ATT_WIDTH]
        kk = qkv_ref[sub, krows, ATT_WIDTH:2 * ATT_WIDTH]
        vv = qkv_ref[sub, krows, 2 * ATT_WIDTH:3 * ATT_WIDTH]
        q = q * jnp.asarray(HEAD_DIM ** -0.5, BF16)
        head_of_lane = lax.broadcasted_iota(jnp.int32, (ATT_BLOCK, ATT_WIDTH), 1) // HEAD_DIM
        ones = jnp.ones((kk.shape[0], LANES), BF16)
        num = den = mx = None
        for h in range(ATT_HEADS):
            mine = head_of_lane == h
            bias = bias_ref[h, :, ATT_BLOCK:] if first else bias_ref[h]
            s = _dot_nt(jnp.where(mine, q, jnp.zeros_like(q)), kk) + bias
            m = jnp.max(s, axis=-1, keepdims=True)
            p = jnp.exp(s - m).astype(BF16)
            num_h = _dot(p, vv)
            den_h = jnp.tile(_dot(p, ones), (1, ATT_WIDTH // LANES))
            m_h = jnp.broadcast_to(m, (ATT_BLOCK, ATT_WIDTH))
            num = num_h if h == 0 else jnp.where(mine, num_h, num)
            den = den_h if h == 0 else jnp.where(mine, den_h, den)
            mx = m_h if h == 0 else jnp.where(mine, m_h, mx)
        o_ref[sub, rows, :] = (num / den).astype(o_ref.dtype)
        lse_ref[sub, rows, :] = mx + jnp.log(den)

    subs = range(qkv_ref.shape[0])
    for sub in subs:
        block(sub, 0, True)
    if nblk > 1:
        def loop_body(n, carry):
            for sub in subs:
                block(sub, n, False)
            return carry
        lax.fori_loop(1, nblk, loop_body, 0, unroll=unroll)


def _attention_pattern(ua, rel_bias, window, dilation):
    L = SEQ // dilation
    nblk = L // ATT_BLOCK
    per_step = min(dilation, ATT_INFLIGHT)
    unroll = max(1, ATT_INFLIGHT // per_step)
    while (nblk - 1) % unroll:
        unroll -= 1
    qkv_spec = pl.BlockSpec((None, per_step, L, QKV_COLS), lambda b, r: (b, r, 0, 0))
    out_spec = pl.BlockSpec((None, per_step, L, ATT_WIDTH), lambda b, r: (b, r, 0, 0))
    return pl.pallas_call(
        functools.partial(_attn_body, nblk=nblk, unroll=unroll),
        grid=(BATCH, dilation // per_step),
        in_specs=[
            pl.BlockSpec(memory_space=pltpu.SMEM),
            pl.BlockSpec((ATT_BLOCK, 2 * ATT_BLOCK), lambda b, r: (0, 0)),
            qkv_spec,
        ],
        out_specs=[out_spec, out_spec],
        out_shape=[jax.ShapeDtypeStruct((BATCH, dilation, L, ATT_WIDTH), BF16),
                   jax.ShapeDtypeStruct((BATCH, dilation, L, ATT_WIDTH), F32)],
        scratch_shapes=[pltpu.VMEM((ATT_HEADS, ATT_BLOCK, 2 * ATT_BLOCK), F32)],
        compiler_params=_params(("arbitrary", "arbitrary")),
        name=f"attn_d{dilation}",
    )(rel_bias, _bucket_table(window, dilation), ua)


def _attention(qkvs, rel_bias):
    return [_attention_pattern(ua, rel_bias, window, dilation)
            for ua, (window, dilation) in zip(qkvs, DILATED_PATTERNS)]


def _gla_body(q_ref, k_ref, v_ref, gr_ref, glr_ref, wg_ref, bg_ref, gn_ref, o_ref, s_ref):
    @pl.when(pl.program_id(1) == 0)
    def _():
        s_ref[...] = jnp.zeros_like(s_ref)

    C = GLA_CHUNK
    row = lax.broadcasted_iota(jnp.int32, (C, C), 0)
    col = lax.broadcasted_iota(jnp.int32, (C, C), 1)
    tril = row >= col
    tril_bf = jnp.where(tril, 1.0, 0.0).astype(BF16)
    ones_bf = jnp.ones((C, GLA_DV), BF16)

    xg = _dot3(glr_ref[...], wg_ref[...]) + bg_ref[...]
    la_all = (jnp.minimum(xg, 0.0) - jnp.log(1.0 + jnp.exp(-jnp.abs(xg)))) * (1.0 / 16.0)

    for c in range(TC_GLA // C):
        rows = slice(c * C, (c + 1) * C)
        la_hi, la_lo = _split_bf16(la_all[rows])
        cum = _dot(tril_bf, la_hi) + _dot(tril_bf, la_lo)
        last = cum[C - 1:C, :]
        q = q_ref[rows, :]
        k = k_ref[rows, :]
        qt = (q * jnp.exp(cum) * (GLA_DK ** -0.5)).astype(BF16)
        kt = (k * jnp.exp(-cum)).astype(BF16)
        kl = (k * jnp.exp(last - cum)).astype(BF16)
        for h in range(GLA_HEADS):
            sl = slice(h * GLA_DK, (h + 1) * GLA_DK)
            vs = slice(h * GLA_DV, (h + 1) * GLA_DV)
            vh = v_ref[rows, vs].astype(BF16)
            state = s_ref[h]
            st_hi, st_lo = _split_bf16(state)
            sc = jnp.where(tril, _dot_nt(qt[:, sl], kt[:, sl]), 0.0).astype(BF16)
            o = _dot(qt[:, sl], st_hi) + _dot(qt[:, sl], st_lo) + _dot(sc, vh)
            ltot = _dot_tn(la_hi[:, sl], ones_bf) + _dot_tn(la_lo[:, sl], ones_bf)
            s_ref[h] = jnp.exp(ltot) * state + _dot_tn(kl[:, sl], vh)
            g = gr_ref[rows, vs]
            o_ref[rows, vs] = (_rms(o, gn_ref[...]) * (g * _sigmoid(g))).astype(o_ref.dtype)


def _gla(u, wg, bg, gn):
    nj = SEQ // TC_GLA
    row = lambda b, j: b * nj + j
    return pl.pallas_call(
        _gla_body,
        grid=(BATCH, nj),
        in_specs=[
            pl.BlockSpec((TC_GLA, GLA_QK), lambda b, j: (row(b, j), COL_GQ // GLA_QK)),
            pl.BlockSpec((TC_GLA, GLA_QK), lambda b, j: (row(b, j), COL_GK // GLA_QK)),
            pl.BlockSpec((TC_GLA, GLA_V), lambda b, j: (row(b, j), COL_GV // GLA_V)),
            pl.BlockSpec((TC_GLA, GLA_V), lambda b, j: (row(b, j), COL_GR // GLA_V)),
            pl.BlockSpec((TC_GLA, LANES), lambda b, j: (row(b, j), COL_GLR // LANES)),
            pl.BlockSpec((LANES, GLA_QK), lambda b, j: (0, 0)),
            pl.BlockSpec((1, GLA_QK), lambda b, j: (0, 0)),
            pl.BlockSpec((1, GLA_DV), lambda b, j: (0, 0)),
        ],
        out_specs=pl.BlockSpec((TC_GLA, GLA_V), lambda b, j: (row(b, j), 0)),
        out_shape=jax.ShapeDtypeStruct((TOKENS, GLA_V), BF16),
        scratch_shapes=[pltpu.VMEM((GLA_HEADS, GLA_DK, GLA_DV), F32)],
        compiler_params=_params(("arbitrary", "arbitrary")),
        name="gla",
    )(u, u, u, u, u, wg, bg, gn)


def _conv_body(cin_ref, cb_ref, cc_ref, w_ref, o_ref):
    uu = cc_ref[...] * cin_ref[...]
    t = lax.broadcasted_iota(jnp.int32, uu.shape, 0)
    y = uu * w_ref[CONV_WIDTH - 1:CONV_WIDTH, :]
    for shift in range(1, CONV_WIDTH):
        prev = jnp.where(t >= shift, pltpu.roll(uu, shift, axis=0), 0.0)
        y = y + prev * w_ref[CONV_WIDTH - 1 - shift:CONV_WIDTH - shift, :]
    o_ref[...] = (cb_ref[...] * y).astype(o_ref.dtype)


def _conv(u, w):
    spec = lambda col: pl.BlockSpec((SEQ, CONV_CH), lambda b: (b, col // CONV_CH))
    return pl.pallas_call(
        _conv_body,
        grid=(BATCH,),
        in_specs=[spec(COL_CIN), spec(COL_CB), spec(COL_CC),
                  pl.BlockSpec((8, CONV_CH), lambda b: (0, 0))],
        out_specs=pl.BlockSpec((SEQ, CONV_CH), lambda b: (b, 0)),
        out_shape=jax.ShapeDtypeStruct((TOKENS, CONV_CH), BF16),
        compiler_params=_params(("parallel",)),
        name="conv",
    )(u, u, u, w)


def _outproj_body(*refs, route):
    (o1, o4, o16, l1, l4, l16, go_ref, cv_ref, x_ref, w_ref, g_ref) = refs[:11]
    perm_ref = refs[-1]
    if route:
        wr_ref, xo_ref, ho_ref, route_ref, counts_ref, carry_ref = refs[11:-1]
    else:
        xo_ref, ho_ref = refs[11:-1]

    def token_order(ref, slot):
        dilation, rows, _ = ref.shape
        if dilation == 1:
            return ref[0].astype(F32)
        tiles = range(ATT_WIDTH // LANES)
        for r in range(dilation):
            val = ref[r].astype(F32)
            for t in tiles:
                perm_ref[slot, t, pl.ds(r, rows, stride=dilation), :] = val[:, t * LANES:(t + 1) * LANES]
        return jnp.concatenate([perm_ref[slot, t] for t in tiles], axis=-1)

    la, lb, lc = token_order(l1, 0), token_order(l4, 0), token_order(l16, 1)
    oa, ob, oc = token_order(o1, 0), token_order(o4, 2), token_order(o16, 3)
    m = jnp.maximum(jnp.maximum(la, lb), lc)
    ea, eb, ec = jnp.exp(la - m), jnp.exp(lb - m), jnp.exp(lc - m)
    att = (ea * oa + eb * ob + ec * oc) / (ea + eb + ec)
    y = (x_ref[...]
         + _dot(att.astype(BF16), w_ref[0:ATT_WIDTH, :])
         + _dot(go_ref[...], w_ref[ATT_WIDTH:ATT_WIDTH + GLA_V, :])
         + _dot(cv_ref[...], w_ref[ATT_WIDTH + GLA_V:MIX_WIDTH, :]))
    xo_ref[...] = y
    hf = _rms(y, g_ref[...])
    ho_ref[...] = hf.astype(ho_ref.dtype)
    if route:
        @pl.when(pl.program_id(0) == 0)
        def _():
            carry_ref[...] = jnp.zeros_like(carry_ref)

        tm = hf.shape[0]
        logits = _dot3(hf, wr_ref[...])
        lane = lax.broadcasted_iota(jnp.int32, logits.shape, 1).astype(F32)
        lg = jnp.where(lane < N_EXPERTS, logits, -jnp.inf)
        v1 = jnp.max(lg, axis=-1, keepdims=True)
        i1 = jnp.min(jnp.where(lg == v1, lane, float(LANES)), axis=-1, keepdims=True)
        lg2 = jnp.where(lane == i1, -jnp.inf, lg)
        v2 = jnp.max(lg2, axis=-1, keepdims=True)
        i2 = jnp.min(jnp.where(lg2 == v2, lane, float(LANES)), axis=-1, keepdims=True)
        e2 = jnp.exp(v2 - v1)
        w1 = 1.0 / (1.0 + e2)
        w2 = e2 * w1
        sel1 = lane == i1
        sel2 = lane == i2
        onehot = jnp.where(sel1, 1.0, jnp.where(sel2, 1.0, 0.0))
        tri = (lax.broadcasted_iota(jnp.int32, (tm, tm), 0) >= lax.broadcasted_iota(jnp.int32, (tm, tm), 1))
        csum = _dot(jnp.where(tri, 1.0, 0.0).astype(BF16), onehot.astype(BF16))
        carry = carry_ref[0:1, :]
        rank = csum - onehot + carry
        r1 = jnp.sum(jnp.where(sel1, rank, 0.0), axis=-1, keepdims=True)
        r2 = jnp.sum(jnp.where(sel2, rank, 0.0), axis=-1, keepdims=True)
        total = jnp.broadcast_to(carry + csum[tm - 1:tm, :], carry_ref.shape)
        carry_ref[...] = total
        counts_ref[...] = total
        cols = {ROUTE_I1: i1, ROUTE_I2: i2, ROUTE_W1: w1, ROUTE_W2: w2, ROUTE_R1: r1, ROUTE_R2: r2}
        packed = jnp.zeros_like(logits)
        for c, val in cols.items():
            packed = jnp.where(lane == float(c), val, packed)
        route_ref[...] = packed


def _outproj(att, go, cv, x, w, g, w_router=None):
    route = w_router is not None
    tm = TM_PROJ
    tile = lambda cols: pl.BlockSpec((tm, cols), lambda i: (i, 0))
    full = lambda a: pl.BlockSpec(a.shape, lambda i: (0, 0))
    (o1, l1), (o4, l4), (o16, l16) = att
    args = [o1, o4, o16, l1, l4, l16, go, cv, x, w, g]
    att_specs = [_subseq_spec(d, ATT_WIDTH) for _, d in DILATED_PATTERNS]
    in_specs = att_specs * 2 + [tile(GLA_V), tile(CONV_CH), tile(D_MODEL), full(w), full(g)]
    out_specs = [tile(D_MODEL), tile(D_MODEL)]
    out_shape = [jax.ShapeDtypeStruct((TOKENS, D_MODEL), F32),
                 jax.ShapeDtypeStruct((TOKENS, D_MODEL), F32 if route else BF16)]
    scratch = []
    if route:
        args.append(w_router)
        in_specs.append(full(w_router))
        out_specs += [tile(LANES), pl.BlockSpec((8, LANES), lambda i: (0, 0))]
        out_shape += [jax.ShapeDtypeStruct((TOKENS, LANES), F32), jax.ShapeDtypeStruct((8, LANES), F32)]
        scratch = [pltpu.VMEM((8, LANES), F32)]
    scratch.append(pltpu.VMEM((4, ATT_WIDTH // LANES, tm, LANES), F32))
    return pl.pallas_call(
        functools.partial(_outproj_body, route=route),
        grid=(TOKENS // tm,),
        in_specs=in_specs,
        out_specs=out_specs,
        out_shape=out_shape,
        scratch_shapes=scratch,
        compiler_params=_params(("arbitrary",) if route else ("parallel",)),
        name="outproj_route" if route else "outproj",
    )(*args)


def _swiglu_accumulate(h, w1_ref, w3_ref, w2_ref, acc_ref, tf):
    for c0 in range(0, tf, FFN_SUB):
        c1 = min(c0 + FFN_SUB, tf)
        a = _dot(h, w1_ref[:, c0:c1])
        b = _dot(h, w3_ref[:, c0:c1])
        act = a * _sigmoid(a) * b
        acc_ref[...] += _dot(act.astype(BF16), w2_ref[c0:c1, :])


def _ffn_body(x_ref, h_ref, w1_ref, w3_ref, w2_ref, o_ref, *, tf):
    @pl.when(pl.program_id(1) == 0)
    def _():
        o_ref[...] = x_ref[...]

    _swiglu_accumulate(h_ref[...], w1_ref, w3_ref, w2_ref, o_ref, tf)


def _ffn(x, h, w1, w3, w2, *, tf):
    f = w1.shape[1]
    tm = TM_FFN
    tile = lambda cols: pl.BlockSpec((tm, cols), lambda i, j: (i, 0))
    return pl.pallas_call(
        functools.partial(_ffn_body, tf=tf),
        grid=(TOKENS // tm, f // tf),
        in_specs=[tile(D_MODEL), tile(D_MODEL),
                  pl.BlockSpec((D_MODEL, tf), lambda i, j: (0, j)),
                  pl.BlockSpec((D_MODEL, tf), lambda i, j: (0, j)),
                  pl.BlockSpec((tf, D_MODEL), lambda i, j: (j, 0))],
        out_specs=tile(D_MODEL),
        out_shape=jax.ShapeDtypeStruct((TOKENS, D_MODEL), F32),
        compiler_params=_params(("parallel", "arbitrary")),
        name="dense_ffn",
    )(x, h, w1, w3, w2)


ROW_TILE = D_MODEL // LANES


def _to_row_tiled(dst_ref, lead, val):
    rows = val.shape[0]
    for s in range(ROW_TILE):
        dst_ref[(*lead, pl.ds(s, rows, stride=ROW_TILE), slice(None))] = val[:, s * LANES:(s + 1) * LANES]


def _from_row_tiled(src_ref, lead, rows):
    return jnp.concatenate([src_ref[(*lead, pl.ds(s, rows, stride=ROW_TILE), slice(None))]
                            for s in range(ROW_TILE)], axis=-1)


def _row_tile(idx):
    return pl.ds(pl.multiple_of(idx * ROW_TILE, ROW_TILE), ROW_TILE)


def _dispatch_body(pos_ref, last_tile_ref, h_ref, xs_hbm, stage_ref, zero_ref, sem, zero_sem):
    i = pl.program_id(0)
    n = pl.num_programs(0)
    tm = DISPATCH_CHUNK
    slot = i % 2

    def drain(s):
        for _ in range(2):
            pltpu.make_async_copy(stage_ref.at[s], xs_hbm.at[pl.ds(0, tm * ROW_TILE)], sem.at[s]).wait()

    @pl.when(i == 0)
    def _():
        zero_ref[...] = jnp.zeros_like(zero_ref)

        def zero_copy(e):
            start = pl.multiple_of(last_tile_ref[e] * (TM_MOE * ROW_TILE), TM_MOE * ROW_TILE)
            return pltpu.make_async_copy(zero_ref, xs_hbm.at[pl.ds(start, TM_MOE * ROW_TILE)], zero_sem)

        for e in range(2 * N_EXPERTS):
            @pl.when(last_tile_ref[e] >= 0)
            def _():
                zero_copy(e).start()
        for e in range(2 * N_EXPERTS):
            @pl.when(last_tile_ref[e] >= 0)
            def _():
                zero_copy(e).wait()

    @pl.when(i >= 2)
    def _():
        drain(slot)

    _to_row_tiled(stage_ref, (slot,), h_ref[...])

    def body(t, carry):
        for k in range(2):
            dst = pos_ref[2 * (i * tm + t) + k]
            pltpu.make_async_copy(stage_ref.at[slot, _row_tile(t)], xs_hbm.at[_row_tile(dst)],
                                  sem.at[slot]).start(priority=k)
        return carry
    lax.fori_loop(0, tm, body, 0, unroll=8)

    @pl.when(i == n - 1)
    def _():
        drain(1 - slot)
        drain(slot)


def _dispatch(pos, last_tile, h):
    tm = DISPATCH_CHUNK
    return pl.pallas_call(
        _dispatch_body,
        grid=(TOKENS // tm,),
        in_specs=[pl.BlockSpec(memory_space=pltpu.SMEM),
                  pl.BlockSpec(memory_space=pltpu.SMEM),
                  pl.BlockSpec((tm, D_MODEL), lambda i: (i, 0))],
        out_specs=pl.BlockSpec(memory_space=pl.ANY),
        out_shape=jax.ShapeDtypeStruct((N_SORTED * ROW_TILE, LANES), F32),
        scratch_shapes=[pltpu.VMEM((2, tm * ROW_TILE, LANES), F32),
                        pltpu.VMEM((TM_MOE * ROW_TILE, LANES), F32),
                        pltpu.SemaphoreType.DMA((2,)), pltpu.SemaphoreType.DMA(())],
        compiler_params=_params(("arbitrary",)),
        name="moe_dispatch",
    )(pos, last_tile, h)


def _gffn_body(te_ref, nu_ref, xs_ref, w1_ref, w3_ref, w2_ref, o_ref, hb_ref, acc_ref):
    del te_ref
    i = pl.program_id(0)
    j = pl.program_id(1)

    used = i < nu_ref[0]

    @pl.when(j == 0)
    def _():
        acc_ref[...] = jnp.zeros_like(acc_ref)

    @pl.when(used & (j == 0))
    def _():
        hb_ref[...] = _from_row_tiled(xs_ref, (), TM_MOE).astype(BF16)

    @pl.when(used)
    def _():
        _swiglu_accumulate(hb_ref[...], w1_ref, w3_ref, w2_ref, acc_ref, TF_MOE)

    @pl.when(j == pl.num_programs(1) - 1)
    def _():
        _to_row_tiled(o_ref, (), acc_ref[...])


def _grouped_ffn(tile_expert, n_used, xs, w1, w3, w2):
    nj = FFN_EXPERT // TF_MOE
    col = lambda i, j, nu: jnp.where(i < nu[0], j, nj - 1)
    grid_spec = pltpu.PrefetchScalarGridSpec(
        num_scalar_prefetch=2,
        grid=(N_TILES_MOE, nj),
        in_specs=[
            pl.BlockSpec((TM_MOE * ROW_TILE, LANES), lambda i, j, te, nu: (jnp.minimum(i, nu[0] - 1), 0)),
            pl.BlockSpec((None, D_MODEL, TF_MOE), lambda i, j, te, nu: (te[i], 0, col(i, j, nu))),
            pl.BlockSpec((None, D_MODEL, TF_MOE), lambda i, j, te, nu: (te[i], 0, col(i, j, nu))),
            pl.BlockSpec((None, TF_MOE, D_MODEL), lambda i, j, te, nu: (te[i], col(i, j, nu), 0)),
        ],
        out_specs=pl.BlockSpec((TM_MOE * ROW_TILE, LANES), lambda i, j, te, nu: (i, 0)),
        scratch_shapes=[pltpu.VMEM((TM_MOE, D_MODEL), BF16), pltpu.VMEM((TM_MOE, D_MODEL), F32)],
    )
    return pl.pallas_call(
        _gffn_body,
        grid_spec=grid_spec,
        out_shape=jax.ShapeDtypeStruct((N_SORTED * ROW_TILE, LANES), F32),
        compiler_params=_params(("arbitrary", "arbitrary")),
        name="moe_ffn",
    )(tile_expert, n_used, xs, w1, w3, w2)


def _combine_body(pos_ref, x_ref, route_ref, g_ref, ys_hbm, o_ref, buf_ref, sem):
    i = pl.program_id(0)
    n = pl.num_programs(0)
    tm = TM_COMBINE

    def issue(tile, slot):
        def body(t, carry):
            for k in range(2):
                src = pos_ref[2 * (tile * tm + t) + k]
                pltpu.make_async_copy(ys_hbm.at[_row_tile(src)], buf_ref.at[slot, k, _row_tile(t)],
                                      sem.at[slot]).start(priority=k)
            return carry
        lax.fori_loop(0, tm, body, 0, unroll=8)

    @pl.when(i == 0)
    def _():
        issue(0, 0)

    @pl.when(i + 1 < n)
    def _():
        issue(i + 1, (i + 1) % 2)

    slot = i % 2
    for k in range(2):
        pltpu.make_async_copy(ys_hbm.at[pl.ds(0, tm * ROW_TILE)], buf_ref.at[slot, k], sem.at[slot]).wait()
    r = route_ref[...]
    lane = lax.broadcasted_iota(jnp.int32, r.shape, 1)
    w1 = jnp.sum(jnp.where(lane == ROUTE_W1, r, 0.0), axis=-1, keepdims=True)
    w2 = jnp.sum(jnp.where(lane == ROUTE_W2, r, 0.0), axis=-1, keepdims=True)
    y = x_ref[...] + w1 * _from_row_tiled(buf_ref, (slot, 0), tm) + w2 * _from_row_tiled(buf_ref, (slot, 1), tm)
    o_ref[...] = _rms(y, g_ref[...])


def _combine(pos, x, route, g, ys):
    tm = TM_COMBINE
    return pl.pallas_call(
        _combine_body,
        grid=(TOKENS // tm,),
        in_specs=[pl.BlockSpec(memory_space=pltpu.SMEM),
                  pl.BlockSpec((tm, D_MODEL), lambda i: (i, 0)),
                  pl.BlockSpec((tm, LANES), lambda i: (i, 0)),
                  pl.BlockSpec((1, D_MODEL), lambda i: (0, 0)),
                  pl.BlockSpec(memory_space=pl.ANY)],
        out_specs=pl.BlockSpec((tm, D_MODEL), lambda i: (i, 0)),
        out_shape=jax.ShapeDtypeStruct((TOKENS, D_MODEL), F32),
        scratch_shapes=[pltpu.VMEM((2, 2, tm * ROW_TILE, LANES), F32), pltpu.SemaphoreType.DMA((2,))],
        compiler_params=_params(("arbitrary",)),
        name="moe_combine",
    )(pos, x, route, g, ys)


def _routing_tables(route, counts):
    cnt = counts[0, :N_EXPERTS].astype(jnp.int32)
    tiles = (cnt + TM_MOE - 1) // TM_MOE
    tile_end = jnp.cumsum(tiles)
    tile_start = tile_end - tiles
    n_used = tile_end[-1]
    expert = route[:, ROUTE_I1:ROUTE_I2 + 1].astype(jnp.int32)
    rank = route[:, ROUTE_R1:ROUTE_R2 + 1].astype(jnp.int32)
    pos = (tile_start * TM_MOE)[expert] + rank
    tile_id = jnp.minimum(jnp.arange(N_TILES_MOE, dtype=jnp.int32), n_used - 1)
    tile_expert = jnp.sum(tile_id[:, None] >= tile_end[None, :], axis=1).astype(jnp.int32)
    last_tile = jnp.where(tiles > 0, tile_end - 1, -1)
    spare = n_used + jnp.arange(N_EXPERTS)
    zero_tiles = jnp.concatenate([last_tile, jnp.where(spare < N_TILES_MOE, spare, -1)]).astype(jnp.int32)
    return pos.reshape(2 * TOKENS), tile_expert, n_used.reshape(1), zero_tiles


def _moe(x, h, route, counts, w1, w3, w2, g_final):
    pos, tile_expert, n_used, last_tile = _routing_tables(route, counts)
    xs = _dispatch(pos, last_tile, h)
    ys = _grouped_ffn(tile_expert, n_used, xs, w1, w3, w2)
    return _combine(pos, x, route, g_final, ys)


def _prep_w_in(w):
    aq, ak, av, gq, gk, gv, gr, glr, c_in, c_b, c_c = jnp.split(w, np.cumsum(SPLIT_SIZES)[:-1].tolist(), axis=1)
    pad = jnp.zeros((D_MODEL, LANES - GLA_RANK), w.dtype)
    return jnp.concatenate([aq, ak, av, gv, gr, gq, gk, c_in, c_b, c_c, glr, pad], axis=1).astype(BF16)


def kernel(x, w_mix_in, w_mix_out, g_mix, rel_bias, gla_w_gate, gla_b_gate, gla_g_norm, conv_w,
           g_ffn, ffn_w1, ffn_w3, ffn_w2, moe_router, moe_w1, moe_w3, moe_w2, g_final):
    assert DEPTH == 2
    x = x.reshape(TOKENS, D_MODEL)
    for layer in range(DEPTH):
        u, *qkvs = _inproj(x, g_mix[layer].reshape(1, D_MODEL), _prep_w_in(w_mix_in[layer]))
        att = _attention(qkvs, rel_bias)
        wg = jnp.pad(gla_w_gate[layer], ((0, LANES - GLA_RANK), (0, 0)))
        go = _gla(u, wg, gla_b_gate[layer].reshape(1, GLA_QK), gla_g_norm[layer].reshape(1, GLA_DV))
        cv = _conv(u, jnp.pad(conv_w[layer], ((0, 8 - CONV_WIDTH), (0, 0))))
        w_out = w_mix_out[layer].astype(BF16)
        g2 = g_ffn[layer].reshape(1, D_MODEL)
        i = layer // 2
        if layer % 2 == 0:
            x, h = _outproj(att, go, cv, x, w_out, g2)
            x = _ffn(x, h, ffn_w1[i].astype(BF16), ffn_w3[i].astype(BF16), ffn_w2[i].astype(BF16), tf=TF_DENSE)
        else:
            wr = jnp.pad(moe_router[i], ((0, 0), (0, LANES - N_EXPERTS)))
            x, h, route, counts = _outproj(att, go, cv, x, w_out, g2, wr)
            x = _moe(x, h, route, counts, moe_w1[i].astype(BF16), moe_w3[i].astype(BF16),
                     moe_w2[i].astype(BF16), g_final.reshape(1, D_MODEL))
    return x.reshape(BATCH, SEQ, D_MODEL)
```

```python
import functools
import math

import jax
import jax.numpy as jnp
import numpy as np
from jax import lax
from jax.experimental import pallas as pl
from jax.experimental.pallas import tpu as pltpu

F32 = jnp.float32
BF16 = jnp.bfloat16

D_MODEL = 1024
BATCH = 8
SEQ = 2048
TOKENS = BATCH * SEQ
DEPTH = 2
EPS = 1e-6

HEAD_DIM = 64
ATT_HEADS = 4
ATT_WIDTH = ATT_HEADS * HEAD_DIM
DILATED_PATTERNS = ((128, 1), (512, 4), (2048, 16))
ATT_BLOCK = 128
REL_BUCKETS = 32
REL_MAX_DISTANCE = 2048

GLA_HEADS = 4
GLA_DK = 64
GLA_DV = 128
GLA_RANK = 16
GLA_CHUNK = 64
GLA_QK = GLA_HEADS * GLA_DK
GLA_V = GLA_HEADS * GLA_DV

CONV_CH = 256
CONV_WIDTH = 3
MIX_WIDTH = ATT_WIDTH + GLA_V + CONV_CH

SPLIT_SIZES = (ATT_WIDTH, ATT_WIDTH, ATT_WIDTH, GLA_QK, GLA_QK, GLA_V, GLA_V, GLA_RANK,
               CONV_CH, CONV_CH, CONV_CH)

FFN_DENSE = 2816
N_EXPERTS = 8
FFN_EXPERT = 3584

LANES = 128
MXU_WIDTH = 256
VMEM_LIMIT = 56 * 1024 * 1024

QKV_COLS = 3 * ATT_WIDTH
COL_GV, COL_GR, COL_GQ, COL_GK = 0, 512, 1024, 1280
COL_CIN, COL_CB, COL_CC, COL_GLR = 1536, 1792, 2048, 2304
U_COLS = COL_GLR + LANES

NEG_BIG = -1e30

ATT_INFLIGHT = 4
TM_PROJ = 512
TC_GLA = 512
TM_FFN = 1024
TF_DENSE = 1408
FFN_SUB = 512
TM_MOE = 512
TF_MOE = 1792
N_TILES_MOE = 2 * TOKENS // TM_MOE + N_EXPERTS
N_SORTED = N_TILES_MOE * TM_MOE
DISPATCH_CHUNK = 256
TM_COMBINE = 256

ROUTE_I1, ROUTE_I2, ROUTE_W1, ROUTE_W2, ROUTE_R1, ROUTE_R2 = range(6)


def _params(sem):
    return pltpu.CompilerParams(dimension_semantics=sem, vmem_limit_bytes=VMEM_LIMIT)


def _split_bf16(a):
    hi = a.astype(BF16)
    lo = (a - hi.astype(F32)).astype(BF16)
    return hi, lo


def _dot(a, b):
    return jnp.dot(a, b, preferred_element_type=F32)


def _dot3(a, b):
    a_hi, a_lo = _split_bf16(a)
    b_hi, b_lo = _split_bf16(b)
    return _dot(a_hi, b_hi) + _dot(a_lo, b_hi) + _dot(a_hi, b_lo)


def _dot_nt(a, b):
    return lax.dot_general(a, b, (((1,), (1,)), ((), ())), preferred_element_type=F32)


def _dot_tn(a, b):
    return lax.dot_general(a, b, (((0,), (0,)), ((), ())), preferred_element_type=F32)


def _rms(x, g):
    ms = jnp.mean(x * x, axis=-1, keepdims=True)
    return x * lax.rsqrt(ms + EPS) * g


def _sigmoid(x):
    return 1.0 / (1.0 + jnp.exp(-x))


def _inproj_body(x_ref, g_ref, w_ref, o_ref, *rest):
    qkv_refs, qkv_f32 = rest[:-1], rest[-1]
    h = _rms(x_ref[...], g_ref[...]).astype(BF16)
    for c0 in range(0, QKV_COLS, MXU_WIDTH):
        res = _dot(h, w_ref[:, c0:c0 + MXU_WIDTH])
        for t in range(MXU_WIDTH // LANES):
            qkv_f32[c0 // LANES + t] = res[:, t * LANES:(t + 1) * LANES]
    for (_, dilation), ref in zip(DILATED_PATTERNS, qkv_refs):
        for r in range(dilation):
            rows = pl.ds(r, TM_PROJ // dilation, stride=dilation)
            ref[r] = jnp.concatenate([qkv_f32[t, rows, :] for t in range(QKV_COLS // LANES)],
                                     axis=-1).astype(ref.dtype)
    for c0 in range(0, U_COLS, MXU_WIDTH):
        c1 = min(c0 + MXU_WIDTH, U_COLS)
        o_ref[:, c0:c1] = _dot(h, w_ref[:, QKV_COLS + c0:QKV_COLS + c1])


def _subseq_spec(dilation, cols):
    tiles = SEQ // TM_PROJ
    return pl.BlockSpec((None, dilation, TM_PROJ // dilation, cols), lambda i: (i // tiles, 0, i % tiles, 0))


def _inproj(x, g, w):
    qkv_shapes = [jax.ShapeDtypeStruct((BATCH, d, SEQ // d, QKV_COLS), BF16) for _, d in DILATED_PATTERNS]
    return pl.pallas_call(
        _inproj_body,
        grid=(TOKENS // TM_PROJ,),
        in_specs=[
            pl.BlockSpec((TM_PROJ, D_MODEL), lambda i: (i, 0)),
            pl.BlockSpec((1, D_MODEL), lambda i: (0, 0)),
            pl.BlockSpec((D_MODEL, QKV_COLS + U_COLS), lambda i: (0, 0)),
        ],
        out_specs=[pl.BlockSpec((TM_PROJ, U_COLS), lambda i: (i, 0))]
                  + [_subseq_spec(d, QKV_COLS) for _, d in DILATED_PATTERNS],
        out_shape=[jax.ShapeDtypeStruct((TOKENS, U_COLS), F32)] + qkv_shapes,
        scratch_shapes=[pltpu.VMEM((QKV_COLS // LANES, TM_PROJ, LANES), F32)],
        compiler_params=_params(("parallel",)),
        name="inproj",
    )(x, g, w)


def _rel_bucket(dist):
    max_exact = REL_BUCKETS // 2
    d = jnp.maximum(dist, 0)
    log_ratio = jnp.log(jnp.maximum(d, 1).astype(F32) / max_exact) / math.log(REL_MAX_DISTANCE / max_exact)
    large = jnp.minimum(max_exact + (log_ratio * (REL_BUCKETS - max_exact)).astype(jnp.int32), REL_BUCKETS - 1)
    return jnp.where(d < max_exact, d, large)


def _bucket_table(window, dilation):
    span = window // dilation
    qi = jnp.arange(ATT_BLOCK)[:, None]
    kj = jnp.arange(2 * ATT_BLOCK)[None, :]
    sub_dist = qi - kj + ATT_BLOCK
    band = (sub_dist >= 0) & (sub_dist <= span)
    return jnp.where(band, _rel_bucket(sub_dist * dilation), -1).astype(jnp.int32)


def _attn_body(rb_ref, bidx_ref, qkv_ref, o_ref, lse_ref, bias_ref, *, nblk, unroll):
    @pl.when((pl.program_id(0) == 0) & (pl.program_id(1) == 0))
    def _():
        bidx = bidx_ref[...]
        for h in range(ATT_HEADS):
            acc = jnp.full(bidx.shape, NEG_BIG, F32)
            for b in range(REL_BUCKETS):
                acc = jnp.where(bidx == b, rb_ref[b, h], acc)
            bias_ref[h] = acc

    def block(sub, n, first):
        if first:
            rows = krows = slice(0, ATT_BLOCK)
        else:
            r0 = pl.multiple_of(n * ATT_BLOCK, ATT_BLOCK)
            rows = pl.ds(r0, ATT_BLOCK)
            krows = pl.ds(r0 - ATT_BLOCK, 2 * ATT_BLOCK)
        q = qkv_ref[sub, rows, 0:ATT_WIDTH]
        kk = qkv_ref[sub, krows, ATT_WIDTH:2 * ATT_WIDTH]
        vv = qkv_ref[sub, krows, 2 * ATT_WIDTH:3 * ATT_WIDTH]
        q = q * jnp.asarray(HEAD_DIM ** -0.5, BF16)
        head_of_lane = lax.broadcasted_iota(jnp.int32, (ATT_BLOCK, ATT_WIDTH), 1) // HEAD_DIM
        ones = jnp.ones((kk.shape[0], LANES), BF16)
        num = den = mx = None
        for h in range(ATT_HEADS):
            mine = head_of_lane == h
            bias = bias_ref[h, :, ATT_BLOCK:] if first else bias_ref[h]
            s = _dot_nt(jnp.where(mine, q, jnp.zeros_like(q)), kk) + bias
            m = jnp.max(s, axis=-1, keepdims=True)
            p = jnp.exp(s - m).astype(BF16)
            num_h = _dot(p, vv)
            den_h = jnp.tile(_dot(p, ones), (1, ATT_WIDTH // LANES))
            m_h = jnp.broadcast_to(m, (ATT_BLOCK, ATT_WIDTH))
            num = num_h if h == 0 else jnp.where(mine, num_h, num)
            den = den_h if h == 0 else jnp.where(mine, den_h, den)
            mx = m_h if h == 0 else jnp.where(mine, m_h, mx)
        o_ref[sub, rows, :] = (num / den).astype(o_ref.dtype)
        lse_ref[sub, rows, :] = mx + jnp.log(den)

    subs = range(qkv_ref.shape[0])
    for sub in subs:
        block(sub, 0, True)
    if nblk > 1:
        def loop_body(n, carry):
            for sub in subs:
                block(sub, n, False)
            return carry
        lax.fori_loop(1, nblk, loop_body, 0, unroll=unroll)


def _attention_pattern(ua, rel_bias, window, dilation):
    L = SEQ // dilation
    nblk = L // ATT_BLOCK
    per_step = min(dilation, ATT_INFLIGHT)
    unroll = max(1, ATT_INFLIGHT // per_step)
    while (nblk - 1) % unroll:
        unroll -= 1
    qkv_spec = pl.BlockSpec((None, per_step, L, QKV_COLS), lambda b, r: (b, r, 0, 0))
    out_spec = pl.BlockSpec((None, per_step, L, ATT_WIDTH), lambda b, r: (b, r, 0, 0))
    return pl.pallas_call(
        functools.partial(_attn_body, nblk=nblk, unroll=unroll),
        grid=(BATCH, dilation // per_step),
        in_specs=[
            pl.BlockSpec(memory_space=pltpu.SMEM),
            pl.BlockSpec((ATT_BLOCK, 2 * ATT_BLOCK), lambda b, r: (0, 0)),
            qkv_spec,
        ],
        out_specs=[out_spec, out_spec],
        out_shape=[jax.ShapeDtypeStruct((BATCH, dilation, L, ATT_WIDTH), BF16),
                   jax.ShapeDtypeStruct((BATCH, dilation, L, ATT_WIDTH), F32)],
        scratch_shapes=[pltpu.VMEM((ATT_HEADS, ATT_BLOCK, 2 * ATT_BLOCK), F32)],
        compiler_params=_params(("arbitrary", "arbitrary")),
        name=f"attn_d{dilation}",
    )(rel_bias, _bucket_table(window, dilation), ua)


def _attention(qkvs, rel_bias):
    return [_attention_pattern(ua, rel_bias, window, dilation)
            for ua, (window, dilation) in zip(qkvs, DILATED_PATTERNS)]


def _gla_body(q_ref, k_ref, v_ref, gr_ref, glr_ref, wg_ref, bg_ref, gn_ref, o_ref, s_ref):
    @pl.when(pl.program_id(1) == 0)
    def _():
        s_ref[...] = jnp.zeros_like(s_ref)

    C = GLA_CHUNK
    row = lax.broadcasted_iota(jnp.int32, (C, C), 0)
    col = lax.broadcasted_iota(jnp.int32, (C, C), 1)
    tril = row >= col
    tril_bf = jnp.where(tril, 1.0, 0.0).astype(BF16)
    ones_bf = jnp.ones((C, GLA_DV), BF16)

    xg = _dot3(glr_ref[...], wg_ref[...]) + bg_ref[...]
    la_all = (jnp.minimum(xg, 0.0) - jnp.log(1.0 + jnp.exp(-jnp.abs(xg)))) * (1.0 / 16.0)

    for c in range(TC_GLA // C):
        rows = slice(c * C, (c + 1) * C)
        la_hi, la_lo = _split_bf16(la_all[rows])
        cum = _dot(tril_bf, la_hi) + _dot(tril_bf, la_lo)
        last = cum[C - 1:C, :]
        q = q_ref[rows, :]
        k = k_ref[rows, :]
        qt = (q * jnp.exp(cum) * (GLA_DK ** -0.5)).astype(BF16)
        kt = (k * jnp.exp(-cum)).astype(BF16)
        kl = (k * jnp.exp(last - cum)).astype(BF16)
        for h in range(GLA_HEADS):
            sl = slice(h * GLA_DK, (h + 1) * GLA_DK)
            vs = slice(h * GLA_DV, (h + 1) * GLA_DV)
            vh = v_ref[rows, vs].astype(BF16)
            state = s_ref[h]
            st_hi, st_lo = _split_bf16(state)
            sc = jnp.where(tril, _dot_nt(qt[:, sl], kt[:, sl]), 0.0).astype(BF16)
            o = _dot(qt[:, sl], st_hi) + _dot(qt[:, sl], st_lo) + _dot(sc, vh)
            ltot = _dot_tn(la_hi[:, sl], ones_bf) + _dot_tn(la_lo[:, sl], ones_bf)
            s_ref[h] = jnp.exp(ltot) * state + _dot_tn(kl[:, sl], vh)
            g = gr_ref[rows, vs]
            o_ref[rows, vs] = (_rms(o, gn_ref[...]) * (g * _sigmoid(g))).astype(o_ref.dtype)


def _gla(u, wg, bg, gn):
    nj = SEQ // TC_GLA
    row = lambda b, j: b * nj + j
    return pl.pallas_call(
        _gla_body,
        grid=(BATCH, nj),
        in_specs=[
            pl.BlockSpec((TC_GLA, GLA_QK), lambda b, j: (row(b, j), COL_GQ // GLA_QK)),
            pl.BlockSpec((TC_GLA, GLA_QK), lambda b, j: (row(b, j), COL_GK // GLA_QK)),
            pl.BlockSpec((TC_GLA, GLA_V), lambda b, j: (row(b, j), COL_GV // GLA_V)),
            pl.BlockSpec((TC_GLA, GLA_V), lambda b, j: (row(b, j), COL_GR // GLA_V)),
            pl.BlockSpec((TC_GLA, LANES), lambda b, j: (row(b, j), COL_GLR // LANES)),
            pl.BlockSpec((LANES, GLA_QK), lambda b, j: (0, 0)),
            pl.BlockSpec((1, GLA_QK), lambda b, j: (0, 0)),
            pl.BlockSpec((1, GLA_DV), lambda b, j: (0, 0)),
        ],
        out_specs=pl.BlockSpec((TC_GLA, GLA_V), lambda b, j: (row(b, j), 0)),
        out_shape=jax.ShapeDtypeStruct((TOKENS, GLA_V), BF16),
        scratch_shapes=[pltpu.VMEM((GLA_HEADS, GLA_DK, GLA_DV), F32)],
        compiler_params=_params(("arbitrary", "arbitrary")),
        name="gla",
    )(u, u, u, u, u, wg, bg, gn)


def _conv_body(cin_ref, cb_ref, cc_ref, w_ref, o_ref):
    uu = cc_ref[...] * cin_ref[...]
    t = lax.broadcasted_iota(jnp.int32, uu.shape, 0)
    y = uu * w_ref[CONV_WIDTH - 1:CONV_WIDTH, :]
    for shift in range(1, CONV_WIDTH):
        prev = jnp.where(t >= shift, pltpu.roll(uu, shift, axis=0), 0.0)
        y = y + prev * w_ref[CONV_WIDTH - 1 - shift:CONV_WIDTH - shift, :]
    o_ref[...] = (cb_ref[...] * y).astype(o_ref.dtype)


def _conv(u, w):
    spec = lambda col: pl.BlockSpec((SEQ, CONV_CH), lambda b: (b, col // CONV_CH))
    return pl.pallas_call(
        _conv_body,
        grid=(BATCH,),
        in_specs=[spec(COL_CIN), spec(COL_CB), spec(COL_CC),
                  pl.BlockSpec((8, CONV_CH), lambda b: (0, 0))],
        out_specs=pl.BlockSpec((SEQ, CONV_CH), lambda b: (b, 0)),
        out_shape=jax.ShapeDtypeStruct((TOKENS, CONV_CH), BF16),
        compiler_params=_params(("parallel",)),
        name="conv",
    )(u, u, u, w)


def _outproj_body(*refs, route):
    (o1, o4, o16, l1, l4, l16, go_ref, cv_ref, x_ref, w_ref, g_ref) = refs[:11]
    perm_ref = refs[-1]
    if route:
        wr_ref, xo_ref, ho_ref, route_ref, counts_ref, carry_ref = refs[11:-1]
    else:
        xo_ref, ho_ref = refs[11:-1]

    def token_order(ref, slot):
        dilation, rows, _ = ref.shape
        if dilation == 1:
            return ref[0].astype(F32)
        tiles = range(ATT_WIDTH // LANES)
        for r in range(dilation):
            val = ref[r].astype(F32)
            for t in tiles:
                perm_ref[slot, t, pl.ds(r, rows, stride=dilation), :] = val[:, t * LANES:(t + 1) * LANES]
        return jnp.concatenate([perm_ref[slot, t] for t in tiles], axis=-1)

    la, lb, lc = token_order(l1, 0), token_order(l4, 0), token_order(l16, 1)
    oa, ob, oc = token_order(o1, 0), token_order(o4, 2), token_order(o16, 3)
    m = jnp.maximum(jnp.maximum(la, lb), lc)
    ea, eb, ec = jnp.exp(la - m), jnp.exp(lb - m), jnp.exp(lc - m)
    att = (ea * oa + eb * ob + ec * oc) / (ea + eb + ec)
    y = (x_ref[...]
         + _dot(att.astype(BF16), w_ref[0:ATT_WIDTH, :])
         + _dot(go_ref[...], w_ref[ATT_WIDTH:ATT_WIDTH + GLA_V, :])
         + _dot(cv_ref[...], w_ref[ATT_WIDTH + GLA_V:MIX_WIDTH, :]))
    xo_ref[...] = y
    hf = _rms(y, g_ref[...])
    ho_ref[...] = hf.astype(ho_ref.dtype)
    if route:
        @pl.when(pl.program_id(0) == 0)
        def _():
            carry_ref[...] = jnp.zeros_like(carry_ref)

        tm = hf.shape[0]
        ne = N_EXPERTS
        hf_hi, hf_lo = _split_bf16(hf)
        part = _dot_nt(wr_ref[0], hf_hi) + _dot_nt(wr_ref[1], hf_lo)
        logits = part[0:ne] + part[ne:2 * ne]
        eidx = lax.broadcasted_iota(jnp.int32, logits.shape, 0).astype(F32)
        v1 = jnp.max(logits, axis=0, keepdims=True)
        i1 = jnp.min(jnp.where(logits == v1, eidx, float(ne)), axis=0, keepdims=True)
        lg2 = jnp.where(eidx == i1, -jnp.inf, logits)
        v2 = jnp.max(lg2, axis=0, keepdims=True)
        i2 = jnp.min(jnp.where(lg2 == v2, eidx, float(ne)), axis=0, keepdims=True)
        e2 = jnp.exp(v2 - v1)
        w1 = 1.0 / (1.0 + e2)
        w2 = e2 * w1
        sel1 = eidx == i1
        sel2 = eidx == i2
        onehot = jnp.where(sel1, 1.0, jnp.where(sel2, 1.0, 0.0))
        tri = (lax.broadcasted_iota(jnp.int32, (tm, tm), 0) <= lax.broadcasted_iota(jnp.int32, (tm, tm), 1))
        onehot16 = jnp.concatenate([onehot, jnp.zeros_like(onehot)], axis=0).astype(BF16)
        csum = _dot(onehot16, jnp.where(tri, 1.0, 0.0).astype(BF16))[0:ne]
        carry = carry_ref[:, 0:1]
        rank = csum - onehot + carry
        r1 = jnp.sum(jnp.where(sel1, rank, 0.0), axis=0, keepdims=True)
        r2 = jnp.sum(jnp.where(sel2, rank, 0.0), axis=0, keepdims=True)
        total = jnp.broadcast_to(carry + csum[:, tm - 1:tm], carry_ref.shape)
        carry_ref[...] = total
        counts_ref[...] = total
        rows = {ROUTE_I1: i1, ROUTE_I2: i2, ROUTE_W1: w1, ROUTE_W2: w2, ROUTE_R1: r1, ROUTE_R2: r2}
        zero = jnp.zeros_like(i1)
        route_ref[...] = jnp.concatenate([rows.get(r, zero) for r in range(8)], axis=0)


def _outproj(att, go, cv, x, w, g, w_router=None):
    route = w_router is not None
    tm = TM_PROJ
    tile = lambda cols: pl.BlockSpec((tm, cols), lambda i: (i, 0))
    full = lambda a: pl.BlockSpec(a.shape, lambda i: (0, 0))
    (o1, l1), (o4, l4), (o16, l16) = att
    args = [o1, o4, o16, l1, l4, l16, go, cv, x, w, g]
    att_specs = [_subseq_spec(d, ATT_WIDTH) for _, d in DILATED_PATTERNS]
    in_specs = att_specs * 2 + [tile(GLA_V), tile(CONV_CH), tile(D_MODEL), full(w), full(g)]
    out_specs = [tile(D_MODEL), tile(D_MODEL)]
    out_shape = [jax.ShapeDtypeStruct((TOKENS, D_MODEL), F32),
                 jax.ShapeDtypeStruct((TOKENS, D_MODEL), F32 if route else BF16)]
    scratch = []
    if route:
        args.append(w_router)
        in_specs.append(pl.BlockSpec(w_router.shape, lambda i: (0, 0, 0)))
        out_specs += [pl.BlockSpec((8, tm), lambda i: (0, i)), pl.BlockSpec((N_EXPERTS, LANES), lambda i: (0, 0))]
        out_shape += [jax.ShapeDtypeStruct((8, TOKENS), F32), jax.ShapeDtypeStruct((N_EXPERTS, LANES), F32)]
        scratch = [pltpu.VMEM((N_EXPERTS, LANES), F32)]
    scratch.append(pltpu.VMEM((4, ATT_WIDTH // LANES, tm, LANES), F32))
    return pl.pallas_call(
        functools.partial(_outproj_body, route=route),
        grid=(TOKENS // tm,),
        in_specs=in_specs,
        out_specs=out_specs,
        out_shape=out_shape,
        scratch_shapes=scratch,
        compiler_params=_params(("arbitrary",) if route else ("parallel",)),
        name="outproj_route" if route else "outproj",
    )(*args)


def _swiglu_accumulate(h, w1_ref, w3_ref, w2_ref, acc_ref, tf):
    for c0 in range(0, tf, FFN_SUB):
        c1 = min(c0 + FFN_SUB, tf)
        a = _dot(h, w1_ref[:, c0:c1])
        b = _dot(h, w3_ref[:, c0:c1])
        act = a * _sigmoid(a) * b
        acc_ref[...] += _dot(act.astype(BF16), w2_ref[c0:c1, :])


def _ffn_body(x_ref, h_ref, w1_ref, w3_ref, w2_ref, o_ref, *, tf):
    @pl.when(pl.program_id(1) == 0)
    def _():
        o_ref[...] = x_ref[...]

    _swiglu_accumulate(h_ref[...], w1_ref, w3_ref, w2_ref, o_ref, tf)


def _ffn(x, h, w1, w3, w2, *, tf):
    f = w1.shape[1]
    tm = TM_FFN
    tile = lambda cols: pl.BlockSpec((tm, cols), lambda i, j: (i, 0))
    return pl.pallas_call(
        functools.partial(_ffn_body, tf=tf),
        grid=(TOKENS // tm, f // tf),
        in_specs=[tile(D_MODEL), tile(D_MODEL),
                  pl.BlockSpec((D_MODEL, tf), lambda i, j: (0, j)),
                  pl.BlockSpec((D_MODEL, tf), lambda i, j: (0, j)),
                  pl.BlockSpec((tf, D_MODEL), lambda i, j: (j, 0))],
        out_specs=tile(D_MODEL),
        out_shape=jax.ShapeDtypeStruct((TOKENS, D_MODEL), F32),
        compiler_params=_params(("parallel", "arbitrary")),
        name="dense_ffn",
    )(x, h, w1, w3, w2)


ROW_TILE = D_MODEL // LANES


def _to_row_tiled(dst_ref, lead, val):
    rows = val.shape[0]
    for s in range(ROW_TILE):
        dst_ref[(*lead, pl.ds(s, rows, stride=ROW_TILE), slice(None))] = val[:, s * LANES:(s + 1) * LANES]


def _from_row_tiled(src_ref, lead, rows):
    return jnp.concatenate([src_ref[(*lead, pl.ds(s, rows, stride=ROW_TILE), slice(None))]
                            for s in range(ROW_TILE)], axis=-1)


def _row_tile(idx):
    return pl.ds(pl.multiple_of(idx * ROW_TILE, ROW_TILE), ROW_TILE)


def _dispatch_body(pos_ref, last_tile_ref, h_ref, xs_hbm, stage_ref, zero_ref, sem, zero_sem):
    i = pl.program_id(0)
    n = pl.num_programs(0)
    tm = DISPATCH_CHUNK
    slot = i % 2

    def drain(s):
        for _ in range(2):
            pltpu.make_async_copy(stage_ref.at[s], xs_hbm.at[pl.ds(0, tm * ROW_TILE)], sem.at[s]).wait()

    @pl.when(i == 0)
    def _():
        zero_ref[...] = jnp.zeros_like(zero_ref)

        def zero_copy(e):
            start = pl.multiple_of(last_tile_ref[e] * (TM_MOE * ROW_TILE), TM_MOE * ROW_TILE)
            return pltpu.make_async_copy(zero_ref, xs_hbm.at[pl.ds(start, TM_MOE * ROW_TILE)], zero_sem)

        for e in range(2 * N_EXPERTS):
            @pl.when(last_tile_ref[e] >= 0)
            def _():
                zero_copy(e).start()
        for e in range(2 * N_EXPERTS):
            @pl.when(last_tile_ref[e] >= 0)
            def _():
                zero_copy(e).wait()

    @pl.when(i >= 2)
    def _():
        drain(slot)

    _to_row_tiled(stage_ref, (slot,), h_ref[...])

    def body(t, carry):
        for k in range(2):
            dst = pos_ref[2 * (i * tm + t) + k]
            pltpu.make_async_copy(stage_ref.at[slot, _row_tile(t)], xs_hbm.at[_row_tile(dst)],
                                  sem.at[slot]).start(priority=k)
        return carry
    lax.fori_loop(0, tm, body, 0, unroll=8)

    @pl.when(i == n - 1)
    def _():
        drain(1 - slot)
        drain(slot)


def _dispatch(pos, last_tile, h):
    tm = DISPATCH_CHUNK
    return pl.pallas_call(
        _dispatch_body,
        grid=(TOKENS // tm,),
        in_specs=[pl.BlockSpec(memory_space=pltpu.SMEM),
                  pl.BlockSpec(memory_space=pltpu.SMEM),
                  pl.BlockSpec((tm, D_MODEL), lambda i: (i, 0))],
        out_specs=pl.BlockSpec(memory_space=pl.ANY),
        out_shape=jax.ShapeDtypeStruct((N_SORTED * ROW_TILE, LANES), F32),
        scratch_shapes=[pltpu.VMEM((2, tm * ROW_TILE, LANES), F32),
                        pltpu.VMEM((TM_MOE * ROW_TILE, LANES), F32),
                        pltpu.SemaphoreType.DMA((2,)), pltpu.SemaphoreType.DMA(())],
        compiler_params=_params(("arbitrary",)),
        name="moe_dispatch",
    )(pos, last_tile, h)


def _gffn_body(te_ref, nu_ref, xs_ref, w1_ref, w3_ref, w2_ref, o_ref, hb_ref, acc_ref):
    del te_ref
    i = pl.program_id(0)
    j = pl.program_id(1)

    used = i < nu_ref[0]

    @pl.when(j == 0)
    def _():
        acc_ref[...] = jnp.zeros_like(acc_ref)

    @pl.when(used & (j == 0))
    def _():
        hb_ref[...] = _from_row_tiled(xs_ref, (), TM_MOE).astype(BF16)

    @pl.when(used)
    def _():
        _swiglu_accumulate(hb_ref[...], w1_ref, w3_ref, w2_ref, acc_ref, TF_MOE)

    @pl.when(j == pl.num_programs(1) - 1)
    def _():
        _to_row_tiled(o_ref, (), acc_ref[...])


def _grouped_ffn(tile_expert, n_used, xs, w1, w3, w2):
    nj = FFN_EXPERT // TF_MOE
    col = lambda i, j, nu: jnp.where(i < nu[0], j, nj - 1)
    grid_spec = pltpu.PrefetchScalarGridSpec(
        num_scalar_prefetch=2,
        grid=(N_TILES_MOE, nj),
        in_specs=[
            pl.BlockSpec((TM_MOE * ROW_TILE, LANES), lambda i, j, te, nu: (jnp.minimum(i, nu[0] - 1), 0)),
            pl.BlockSpec((None, D_MODEL, TF_MOE), lambda i, j, te, nu: (te[i], 0, col(i, j, nu))),
            pl.BlockSpec((None, D_MODEL, TF_MOE), lambda i, j, te, nu: (te[i], 0, col(i, j, nu))),
            pl.BlockSpec((None, TF_MOE, D_MODEL), lambda i, j, te, nu: (te[i], col(i, j, nu), 0)),
        ],
        out_specs=pl.BlockSpec((TM_MOE * ROW_TILE, LANES), lambda i, j, te, nu: (i, 0)),
        scratch_shapes=[pltpu.VMEM((TM_MOE, D_MODEL), BF16), pltpu.VMEM((TM_MOE, D_MODEL), F32)],
    )
    return pl.pallas_call(
        _gffn_body,
        grid_spec=grid_spec,
        out_shape=jax.ShapeDtypeStruct((N_SORTED * ROW_TILE, LANES), F32),
        compiler_params=_params(("arbitrary", "arbitrary")),
        name="moe_ffn",
    )(tile_expert, n_used, xs, w1, w3, w2)


def _combine_body(pos_ref, x_ref, gate_ref, g_ref, ys_hbm, o_ref, buf_ref, sem):
    i = pl.program_id(0)
    n = pl.num_programs(0)
    tm = TM_COMBINE

    def issue(tile, slot):
        def body(t, carry):
            for k in range(2):
                src = pos_ref[2 * (tile * tm + t) + k]
                pltpu.make_async_copy(ys_hbm.at[_row_tile(src)], buf_ref.at[slot, k, _row_tile(t)],
                                      sem.at[slot]).start(priority=k)
            return carry
        lax.fori_loop(0, tm, body, 0, unroll=8)

    @pl.when(i == 0)
    def _():
        issue(0, 0)

    @pl.when(i + 1 < n)
    def _():
        issue(i + 1, (i + 1) % 2)

    slot = i % 2
    for k in range(2):
        pltpu.make_async_copy(ys_hbm.at[pl.ds(0, tm * ROW_TILE)], buf_ref.at[slot, k], sem.at[slot]).wait()
    w1 = gate_ref[:, 0:1]
    w2 = gate_ref[:, 1:2]
    y = x_ref[...] + w1 * _from_row_tiled(buf_ref, (slot, 0), tm) + w2 * _from_row_tiled(buf_ref, (slot, 1), tm)
    o_ref[...] = _rms(y, g_ref[...])


def _combine(pos, x, gates, g, ys):
    tm = TM_COMBINE
    return pl.pallas_call(
        _combine_body,
        grid=(TOKENS // tm,),
        in_specs=[pl.BlockSpec(memory_space=pltpu.SMEM),
                  pl.BlockSpec((tm, D_MODEL), lambda i: (i, 0)),
                  pl.BlockSpec((tm, 2), lambda i: (i, 0)),
                  pl.BlockSpec((1, D_MODEL), lambda i: (0, 0)),
                  pl.BlockSpec(memory_space=pl.ANY)],
        out_specs=pl.BlockSpec((tm, D_MODEL), lambda i: (i, 0)),
        out_shape=jax.ShapeDtypeStruct((TOKENS, D_MODEL), F32),
        scratch_shapes=[pltpu.VMEM((2, 2, tm * ROW_TILE, LANES), F32), pltpu.SemaphoreType.DMA((2,))],
        compiler_params=_params(("arbitrary",)),
        name="moe_combine",
    )(pos, x, gates, g, ys)


def _routing_tables(route, counts):
    cnt = counts[:, 0].astype(jnp.int32)
    tiles = (cnt + TM_MOE - 1) // TM_MOE
    tile_end = jnp.cumsum(tiles)
    tile_start = tile_end - tiles
    n_used = tile_end[-1]
    expert = route[ROUTE_I1:ROUTE_I2 + 1].astype(jnp.int32).T
    rank = route[ROUTE_R1:ROUTE_R2 + 1].astype(jnp.int32).T
    pos = (tile_start * TM_MOE)[expert] + rank
    tile_id = jnp.minimum(jnp.arange(N_TILES_MOE, dtype=jnp.int32), n_used - 1)
    tile_expert = jnp.sum(tile_id[:, None] >= tile_end[None, :], axis=1).astype(jnp.int32)
    last_tile = jnp.where(tiles > 0, tile_end - 1, -1)
    spare = n_used + jnp.arange(N_EXPERTS)
    zero_tiles = jnp.concatenate([last_tile, jnp.where(spare < N_TILES_MOE, spare, -1)]).astype(jnp.int32)
    return pos.reshape(2 * TOKENS), tile_expert, n_used.reshape(1), zero_tiles


def _moe(x, h, route, counts, w1, w3, w2, g_final):
    pos, tile_expert, n_used, last_tile = _routing_tables(route, counts)
    xs = _dispatch(pos, last_tile, h)
    ys = _grouped_ffn(tile_expert, n_used, xs, w1, w3, w2)
    return _combine(pos, x, route[ROUTE_W1:ROUTE_W2 + 1].T, g_final, ys)


def _prep_w_in(w):
    aq, ak, av, gq, gk, gv, gr, glr, c_in, c_b, c_c = jnp.split(w, np.cumsum(SPLIT_SIZES)[:-1].tolist(), axis=1)
    pad = jnp.zeros((D_MODEL, LANES - GLA_RANK), w.dtype)
    return jnp.concatenate([aq, ak, av, gv, gr, gq, gk, c_in, c_b, c_c, glr, pad], axis=1).astype(BF16)


def _prep_router(w):
    wt = w.T
    hi = wt.astype(BF16)
    lo = (wt - hi.astype(F32)).astype(BF16)
    return jnp.stack([jnp.concatenate([hi, lo]), jnp.concatenate([hi, jnp.zeros_like(hi)])])


def kernel(x, w_mix_in, w_mix_out, g_mix, rel_bias, gla_w_gate, gla_b_gate, gla_g_norm, conv_w,
           g_ffn, ffn_w1, ffn_w3, ffn_w2, moe_router, moe_w1, moe_w3, moe_w2, g_final):
    assert DEPTH == 2
    x = x.reshape(TOKENS, D_MODEL)
    for layer in range(DEPTH):
        u, *qkvs = _inproj(x, g_mix[layer].reshape(1, D_MODEL), _prep_w_in(w_mix_in[layer]))
        att = _attention(qkvs, rel_bias)
        wg = jnp.pad(gla_w_gate[layer], ((0, LANES - GLA_RANK), (0, 0)))
        go = _gla(u, wg, gla_b_gate[layer].reshape(1, GLA_QK), gla_g_norm[layer].reshape(1, GLA_DV))
        cv = _conv(u, jnp.pad(conv_w[layer], ((0, 8 - CONV_WIDTH), (0, 0))))
        w_out = w_mix_out[layer].astype(BF16)
        g2 = g_ffn[layer].reshape(1, D_MODEL)
        i = layer // 2
        if layer % 2 == 0:
            x, h = _outproj(att, go, cv, x, w_out, g2)
            x = _ffn(x, h, ffn_w1[i].astype(BF16), ffn_w3[i].astype(BF16), ffn_w2[i].astype(BF16), tf=TF_DENSE)
        else:
            x, h, route, counts = _outproj(att, go, cv, x, w_out, g2, _prep_router(moe_router[i]))
            x = _moe(x, h, route, counts, moe_w1[i].astype(BF16), moe_w3[i].astype(BF16),
                     moe_w2[i].astype(BF16), g_final.reshape(1, D_MODEL))
    return x.reshape(BATCH, SEQ, D_MODEL)
```

```python
import functools
import math

import jax
import jax.numpy as jnp
import numpy as np
from jax import lax
from jax.experimental import pallas as pl
from jax.experimental.pallas import tpu as pltpu

F32 = jnp.float32
BF16 = jnp.bfloat16

D_MODEL = 1024
BATCH = 8
SEQ = 2048
TOKENS = BATCH * SEQ
DEPTH = 2
EPS = 1e-6

HEAD_DIM = 64
ATT_HEADS = 4
ATT_WIDTH = ATT_HEADS * HEAD_DIM
DILATED_PATTERNS = ((128, 1), (512, 4), (2048, 16))
ATT_BLOCK = 128
REL_BUCKETS = 32
REL_MAX_DISTANCE = 2048

GLA_HEADS = 4
GLA_DK = 64
GLA_DV = 128
GLA_RANK = 16
GLA_CHUNK = 64
GLA_QK = GLA_HEADS * GLA_DK
GLA_V = GLA_HEADS * GLA_DV

CONV_CH = 256
CONV_WIDTH = 3
MIX_WIDTH = ATT_WIDTH + GLA_V + CONV_CH

SPLIT_SIZES = (ATT_WIDTH, ATT_WIDTH, ATT_WIDTH, GLA_QK, GLA_QK, GLA_V, GLA_V, GLA_RANK,
               CONV_CH, CONV_CH, CONV_CH)

FFN_DENSE = 2816
N_EXPERTS = 8
FFN_EXPERT = 3584

LANES = 128
MXU_WIDTH = 256
VMEM_LIMIT = 56 * 1024 * 1024

QKV_COLS = 3 * ATT_WIDTH
COL_GV, COL_GR, COL_GQ, COL_GK = 0, 512, 1024, 1280
COL_CIN, COL_CB, COL_CC, COL_GLR = 1536, 1792, 2048, 2304
U_COLS = COL_GLR + LANES

NEG_BIG = -1e30

ATT_UNROLL = 15
TM_PROJ = 512
TC_GLA = 512
TM_FFN = 1024
TF_DENSE = 1408
FFN_SUB = 512
TM_MOE = 512
TF_MOE = 1792
N_TILES_MOE = 2 * TOKENS // TM_MOE + N_EXPERTS
N_SORTED = N_TILES_MOE * TM_MOE
DISPATCH_CHUNK = 256
TM_COMBINE = 256

ROUTE_I1, ROUTE_I2, ROUTE_W1, ROUTE_W2, ROUTE_R1, ROUTE_R2 = range(6)


def _params(sem):
    return pltpu.CompilerParams(dimension_semantics=sem, vmem_limit_bytes=VMEM_LIMIT)


def _split_bf16(a):
    hi = a.astype(BF16)
    lo = (a - hi.astype(F32)).astype(BF16)
    return hi, lo


def _dot(a, b):
    return jnp.dot(a, b, preferred_element_type=F32)


def _dot3(a, b):
    a_hi, a_lo = _split_bf16(a)
    b_hi, b_lo = _split_bf16(b)
    return _dot(a_hi, b_hi) + _dot(a_lo, b_hi) + _dot(a_hi, b_lo)


def _dot_nt(a, b):
    return lax.dot_general(a, b, (((1,), (1,)), ((), ())), preferred_element_type=F32)


def _dot_tn(a, b):
    return lax.dot_general(a, b, (((0,), (0,)), ((), ())), preferred_element_type=F32)


def _rms(x, g):
    ms = jnp.mean(x * x, axis=-1, keepdims=True)
    return x * lax.rsqrt(ms + EPS) * g


def _sigmoid(x):
    return 1.0 / (1.0 + jnp.exp(-x))


def _inproj_body(x_ref, g_ref, w_ref, o_ref, *rest):
    qkv_refs, qkv_f32 = rest[:-1], rest[-1]
    h = _rms(x_ref[...], g_ref[...]).astype(BF16)
    for c0 in range(0, QKV_COLS, MXU_WIDTH):
        res = _dot(h, w_ref[:, c0:c0 + MXU_WIDTH])
        for t in range(MXU_WIDTH // LANES):
            qkv_f32[c0 // LANES + t] = res[:, t * LANES:(t + 1) * LANES]
    for (_, dilation), ref in zip(DILATED_PATTERNS, qkv_refs):
        for r in range(dilation):
            rows = pl.ds(r, TM_PROJ // dilation, stride=dilation)
            ref[r] = jnp.concatenate([qkv_f32[t, rows, :] for t in range(QKV_COLS // LANES)],
                                     axis=-1).astype(ref.dtype)
    for c0 in range(0, U_COLS, MXU_WIDTH):
        c1 = min(c0 + MXU_WIDTH, U_COLS)
        o_ref[:, c0:c1] = _dot(h, w_ref[:, QKV_COLS + c0:QKV_COLS + c1])


def _subseq_spec(dilation, cols):
    tiles = SEQ // TM_PROJ
    return pl.BlockSpec((None, dilation, TM_PROJ // dilation, cols), lambda i: (i // tiles, 0, i % tiles, 0))


def _inproj(x, g, w):
    qkv_shapes = [jax.ShapeDtypeStruct((BATCH, d, SEQ // d, QKV_COLS), BF16) for _, d in DILATED_PATTERNS]
    return pl.pallas_call(
        _inproj_body,
        grid=(TOKENS // TM_PROJ,),
        in_specs=[
            pl.BlockSpec((TM_PROJ, D_MODEL), lambda i: (i, 0)),
            pl.BlockSpec((1, D_MODEL), lambda i: (0, 0)),
            pl.BlockSpec((D_MODEL, QKV_COLS + U_COLS), lambda i: (0, 0)),
        ],
        out_specs=[pl.BlockSpec((TM_PROJ, U_COLS), lambda i: (i, 0))]
                  + [_subseq_spec(d, QKV_COLS) for _, d in DILATED_PATTERNS],
        out_shape=[jax.ShapeDtypeStruct((TOKENS, U_COLS), F32)] + qkv_shapes,
        scratch_shapes=[pltpu.VMEM((QKV_COLS // LANES, TM_PROJ, LANES), F32)],
        compiler_params=_params(("parallel",)),
        name="inproj",
    )(x, g, w)


def _rel_bucket(dist):
    max_exact = REL_BUCKETS // 2
    d = jnp.maximum(dist, 0)
    log_ratio = jnp.log(jnp.maximum(d, 1).astype(F32) / max_exact) / math.log(REL_MAX_DISTANCE / max_exact)
    large = jnp.minimum(max_exact + (log_ratio * (REL_BUCKETS - max_exact)).astype(jnp.int32), REL_BUCKETS - 1)
    return jnp.where(d < max_exact, d, large)


def _bucket_table(window, dilation):
    span = window // dilation
    qi = jnp.arange(ATT_BLOCK)[:, None]
    kj = jnp.arange(2 * ATT_BLOCK)[None, :]
    sub_dist = qi - kj + ATT_BLOCK
    band = (sub_dist >= 0) & (sub_dist <= span)
    return jnp.where(band, _rel_bucket(sub_dist * dilation), -1).astype(jnp.int32)


def _attn_body(rb_ref, bidx_ref, qkv_ref, o_ref, lse_ref, bias_ref, *, sub_blocks, unroll):
    nblk = SEQ // ATT_BLOCK

    @pl.when(pl.program_id(0) == 0)
    def _():
        bidx = bidx_ref[...]
        in_prev = lax.broadcasted_iota(jnp.int32, bidx.shape, 1) < ATT_BLOCK
        for h in range(ATT_HEADS):
            acc = jnp.full(bidx.shape, NEG_BIG, F32)
            for b in range(REL_BUCKETS):
                acc = jnp.where(bidx == b, rb_ref[b, h], acc)
            bias_ref[0, h] = acc
            bias_ref[1, h] = jnp.where(in_prev, NEG_BIG, acc)

    def block(n, first):
        if first:
            rows = krows = slice(0, ATT_BLOCK)
        else:
            r0 = pl.multiple_of(n * ATT_BLOCK, ATT_BLOCK)
            rows = pl.ds(r0, ATT_BLOCK)
            krows = pl.ds(r0 - ATT_BLOCK, 2 * ATT_BLOCK)
            if sub_blocks == nblk:
                variant = 0
            elif sub_blocks == 1:
                variant = 1
            else:
                variant = jnp.where(n % sub_blocks == 0, 1, 0)
        q = qkv_ref[rows, 0:ATT_WIDTH]
        kk = qkv_ref[krows, ATT_WIDTH:2 * ATT_WIDTH]
        vv = qkv_ref[krows, 2 * ATT_WIDTH:3 * ATT_WIDTH]
        q = q * jnp.asarray(HEAD_DIM ** -0.5, BF16)
        head_of_lane = lax.broadcasted_iota(jnp.int32, (ATT_BLOCK, ATT_WIDTH), 1) // HEAD_DIM
        ones = jnp.ones((kk.shape[0], LANES), BF16)
        num = den = mx = None
        for h in range(ATT_HEADS):
            mine = head_of_lane == h
            bias = bias_ref[0, h, :, ATT_BLOCK:] if first else bias_ref[variant, h]
            s = _dot_nt(jnp.where(mine, q, jnp.zeros_like(q)), kk) + bias
            m = jnp.max(s, axis=-1, keepdims=True)
            p = jnp.exp(s - m).astype(BF16)
            num_h = _dot(p, vv)
            den_h = jnp.tile(_dot(p, ones), (1, ATT_WIDTH // LANES))
            m_h = jnp.broadcast_to(m, (ATT_BLOCK, ATT_WIDTH))
            num = num_h if h == 0 else jnp.where(mine, num_h, num)
            den = den_h if h == 0 else jnp.where(mine, den_h, den)
            mx = m_h if h == 0 else jnp.where(mine, m_h, mx)
        o_ref[rows, :] = (num / den).astype(o_ref.dtype)
        lse_ref[rows, :] = mx + jnp.log(den)

    block(0, True)

    def loop_body(n, carry):
        block(n, False)
        return carry
    lax.fori_loop(1, nblk, loop_body, 0, unroll=unroll)


def _attention_pattern(ua, rel_bias, window, dilation):
    L = SEQ // dilation
    shape = (BATCH, dilation, L, ATT_WIDTH)
    qkv_spec = pl.BlockSpec((None, SEQ, QKV_COLS), lambda b: (b, 0, 0))
    out_spec = pl.BlockSpec((None, SEQ, ATT_WIDTH), lambda b: (b, 0, 0))
    o, lse = pl.pallas_call(
        functools.partial(_attn_body, sub_blocks=L // ATT_BLOCK, unroll=ATT_UNROLL),
        grid=(BATCH,),
        in_specs=[
            pl.BlockSpec(memory_space=pltpu.SMEM),
            pl.BlockSpec((ATT_BLOCK, 2 * ATT_BLOCK), lambda b: (0, 0)),
            qkv_spec,
        ],
        out_specs=[out_spec, out_spec],
        out_shape=[jax.ShapeDtypeStruct((BATCH, SEQ, ATT_WIDTH), BF16),
                   jax.ShapeDtypeStruct((BATCH, SEQ, ATT_WIDTH), F32)],
        scratch_shapes=[pltpu.VMEM((2, ATT_HEADS, ATT_BLOCK, 2 * ATT_BLOCK), F32)],
        compiler_params=_params(("arbitrary",)),
        name=f"attn_d{dilation}",
    )(rel_bias, _bucket_table(window, dilation), ua.reshape(BATCH, SEQ, QKV_COLS))
    return o.reshape(shape), lse.reshape(shape)


def _attention(qkvs, rel_bias):
    return [_attention_pattern(ua, rel_bias, window, dilation)
            for ua, (window, dilation) in zip(qkvs, DILATED_PATTERNS)]


def _gla_body(q_ref, k_ref, v_ref, gr_ref, glr_ref, wg_ref, bg_ref, gn_ref, o_ref, s_ref):
    @pl.when(pl.program_id(1) == 0)
    def _():
        s_ref[...] = jnp.zeros_like(s_ref)

    C = GLA_CHUNK
    row = lax.broadcasted_iota(jnp.int32, (C, C), 0)
    col = lax.broadcasted_iota(jnp.int32, (C, C), 1)
    tril = row >= col
    tril_bf = jnp.where(tril, 1.0, 0.0).astype(BF16)
    ones_bf = jnp.ones((C, GLA_DV), BF16)

    xg = _dot3(glr_ref[...], wg_ref[...]) + bg_ref[...]
    la_all = (jnp.minimum(xg, 0.0) - jnp.log(1.0 + jnp.exp(-jnp.abs(xg)))) * (1.0 / 16.0)

    for c in range(TC_GLA // C):
        rows = slice(c * C, (c + 1) * C)
        la_hi, la_lo = _split_bf16(la_all[rows])
        cum = _dot(tril_bf, la_hi) + _dot(tril_bf, la_lo)
        last = cum[C - 1:C, :]
        q = q_ref[rows, :]
        k = k_ref[rows, :]
        qt = (q * jnp.exp(cum) * (GLA_DK ** -0.5)).astype(BF16)
        kt = (k * jnp.exp(-cum)).astype(BF16)
        kl = (k * jnp.exp(last - cum)).astype(BF16)
        for h in range(GLA_HEADS):
            sl = slice(h * GLA_DK, (h + 1) * GLA_DK)
            vs = slice(h * GLA_DV, (h + 1) * GLA_DV)
            vh = v_ref[rows, vs].astype(BF16)
            state = s_ref[h]
            st_hi, st_lo = _split_bf16(state)
            sc = jnp.where(tril, _dot_nt(qt[:, sl], kt[:, sl]), 0.0).astype(BF16)
            o = _dot(qt[:, sl], st_hi) + _dot(qt[:, sl], st_lo) + _dot(sc, vh)
            ltot = _dot_tn(la_hi[:, sl], ones_bf) + _dot_tn(la_lo[:, sl], ones_bf)
            s_ref[h] = jnp.exp(ltot) * state + _dot_tn(kl[:, sl], vh)
            g = gr_ref[rows, vs]
            o_ref[rows, vs] = (_rms(o, gn_ref[...]) * (g * _sigmoid(g))).astype(o_ref.dtype)


def _gla(u, wg, bg, gn):
    nj = SEQ // TC_GLA
    row = lambda b, j: b * nj + j
    return pl.pallas_call(
        _gla_body,
        grid=(BATCH, nj),
        in_specs=[
            pl.BlockSpec((TC_GLA, GLA_QK), lambda b, j: (row(b, j), COL_GQ // GLA_QK)),
            pl.BlockSpec((TC_GLA, GLA_QK), lambda b, j: (row(b, j), COL_GK // GLA_QK)),
            pl.BlockSpec((TC_GLA, GLA_V), lambda b, j: (row(b, j), COL_GV // GLA_V)),
            pl.BlockSpec((TC_GLA, GLA_V), lambda b, j: (row(b, j), COL_GR // GLA_V)),
            pl.BlockSpec((TC_GLA, LANES), lambda b, j: (row(b, j), COL_GLR // LANES)),
            pl.BlockSpec((LANES, GLA_QK), lambda b, j: (0, 0)),
            pl.BlockSpec((1, GLA_QK), lambda b, j: (0, 0)),
            pl.BlockSpec((1, GLA_DV), lambda b, j: (0, 0)),
        ],
        out_specs=pl.BlockSpec((TC_GLA, GLA_V), lambda b, j: (row(b, j), 0)),
        out_shape=jax.ShapeDtypeStruct((TOKENS, GLA_V), BF16),
        scratch_shapes=[pltpu.VMEM((GLA_HEADS, GLA_DK, GLA_DV), F32)],
        compiler_params=_params(("arbitrary", "arbitrary")),
        name="gla",
    )(u, u, u, u, u, wg, bg, gn)


def _conv_body(cin_ref, cb_ref, cc_ref, w_ref, o_ref):
    uu = cc_ref[...] * cin_ref[...]
    t = lax.broadcasted_iota(jnp.int32, uu.shape, 0)
    y = uu * w_ref[CONV_WIDTH - 1:CONV_WIDTH, :]
    for shift in range(1, CONV_WIDTH):
        prev = jnp.where(t >= shift, pltpu.roll(uu, shift, axis=0), 0.0)
        y = y + prev * w_ref[CONV_WIDTH - 1 - shift:CONV_WIDTH - shift, :]
    o_ref[...] = (cb_ref[...] * y).astype(o_ref.dtype)


def _conv(u, w):
    spec = lambda col: pl.BlockSpec((SEQ, CONV_CH), lambda b: (b, col // CONV_CH))
    return pl.pallas_call(
        _conv_body,
        grid=(BATCH,),
        in_specs=[spec(COL_CIN), spec(COL_CB), spec(COL_CC),
                  pl.BlockSpec((8, CONV_CH), lambda b: (0, 0))],
        out_specs=pl.BlockSpec((SEQ, CONV_CH), lambda b: (b, 0)),
        out_shape=jax.ShapeDtypeStruct((TOKENS, CONV_CH), BF16),
        compiler_params=_params(("parallel",)),
        name="conv",
    )(u, u, u, w)


def _outproj_body(*refs, route):
    (o1, o4, o16, l1, l4, l16, go_ref, cv_ref, x_ref, w_ref, g_ref) = refs[:11]
    perm_ref = refs[-1]
    if route:
        wr_ref, xo_ref, ho_ref, route_ref, counts_ref, carry_ref = refs[11:-1]
    else:
        xo_ref, ho_ref = refs[11:-1]

    def token_order(ref, slot):
        dilation, rows, _ = ref.shape
        if dilation == 1:
            return ref[0].astype(F32)
        tiles = range(ATT_WIDTH // LANES)
        for r in range(dilation):
            val = ref[r].astype(F32)
            for t in tiles:
                perm_ref[slot, t, pl.ds(r, rows, stride=dilation), :] = val[:, t * LANES:(t + 1) * LANES]
        return jnp.concatenate([perm_ref[slot, t] for t in tiles], axis=-1)

    la, lb, lc = token_order(l1, 0), token_order(l4, 0), token_order(l16, 1)
    oa, ob, oc = token_order(o1, 0), token_order(o4, 2), token_order(o16, 3)
    m = jnp.maximum(jnp.maximum(la, lb), lc)
    ea, eb, ec = jnp.exp(la - m), jnp.exp(lb - m), jnp.exp(lc - m)
    att = (ea * oa + eb * ob + ec * oc) / (ea + eb + ec)
    y = (x_ref[...]
         + _dot(att.astype(BF16), w_ref[0:ATT_WIDTH, :])
         + _dot(go_ref[...], w_ref[ATT_WIDTH:ATT_WIDTH + GLA_V, :])
         + _dot(cv_ref[...], w_ref[ATT_WIDTH + GLA_V:MIX_WIDTH, :]))
    xo_ref[...] = y
    hf = _rms(y, g_ref[...])
    ho_ref[...] = hf.astype(ho_ref.dtype)
    if route:
        @pl.when(pl.program_id(0) == 0)
        def _():
            carry_ref[...] = jnp.zeros_like(carry_ref)

        tm = hf.shape[0]
        ne = N_EXPERTS
        hf_hi, hf_lo = _split_bf16(hf)
        part = _dot_nt(wr_ref[0], hf_hi) + _dot_nt(wr_ref[1], hf_lo)
        logits = part[0:ne] + part[ne:2 * ne]
        eidx = lax.broadcasted_iota(jnp.int32, logits.shape, 0).astype(F32)
        v1 = jnp.max(logits, axis=0, keepdims=True)
        i1 = jnp.min(jnp.where(logits == v1, eidx, float(ne)), axis=0, keepdims=True)
        lg2 = jnp.where(eidx == i1, -jnp.inf, logits)
        v2 = jnp.max(lg2, axis=0, keepdims=True)
        i2 = jnp.min(jnp.where(lg2 == v2, eidx, float(ne)), axis=0, keepdims=True)
        e2 = jnp.exp(v2 - v1)
        w1 = 1.0 / (1.0 + e2)
        w2 = e2 * w1
        sel1 = eidx == i1
        sel2 = eidx == i2
        onehot = jnp.where(sel1, 1.0, jnp.where(sel2, 1.0, 0.0))
        tri = (lax.broadcasted_iota(jnp.int32, (tm, tm), 0) <= lax.broadcasted_iota(jnp.int32, (tm, tm), 1))
        onehot16 = jnp.concatenate([onehot, jnp.zeros_like(onehot)], axis=0).astype(BF16)
        csum = _dot(onehot16, jnp.where(tri, 1.0, 0.0).astype(BF16))[0:ne]
        carry = carry_ref[:, 0:1]
        rank = csum - onehot + carry
        r1 = jnp.sum(jnp.where(sel1, rank, 0.0), axis=0, keepdims=True)
        r2 = jnp.sum(jnp.where(sel2, rank, 0.0), axis=0, keepdims=True)
        total = jnp.broadcast_to(carry + csum[:, tm - 1:tm], carry_ref.shape)
        carry_ref[...] = total
        counts_ref[...] = total
        rows = {ROUTE_I1: i1, ROUTE_I2: i2, ROUTE_W1: w1, ROUTE_W2: w2, ROUTE_R1: r1, ROUTE_R2: r2}
        zero = jnp.zeros_like(i1)
        route_ref[...] = jnp.concatenate([rows.get(r, zero) for r in range(8)], axis=0)


def _outproj(att, go, cv, x, w, g, w_router=None):
    route = w_router is not None
    tm = TM_PROJ
    tile = lambda cols: pl.BlockSpec((tm, cols), lambda i: (i, 0))
    full = lambda a: pl.BlockSpec(a.shape, lambda i: (0, 0))
    (o1, l1), (o4, l4), (o16, l16) = att
    args = [o1, o4, o16, l1, l4, l16, go, cv, x, w, g]
    att_specs = [_subseq_spec(d, ATT_WIDTH) for _, d in DILATED_PATTERNS]
    in_specs = att_specs * 2 + [tile(GLA_V), tile(CONV_CH), tile(D_MODEL), full(w), full(g)]
    out_specs = [tile(D_MODEL), tile(D_MODEL)]
    out_shape = [jax.ShapeDtypeStruct((TOKENS, D_MODEL), F32),
                 jax.ShapeDtypeStruct((TOKENS, D_MODEL), F32 if route else BF16)]
    scratch = []
    if route:
        args.append(w_router)
        in_specs.append(pl.BlockSpec(w_router.shape, lambda i: (0, 0, 0)))
        out_specs += [pl.BlockSpec((8, tm), lambda i: (0, i)), pl.BlockSpec((N_EXPERTS, LANES), lambda i: (0, 0))]
        out_shape += [jax.ShapeDtypeStruct((8, TOKENS), F32), jax.ShapeDtypeStruct((N_EXPERTS, LANES), F32)]
        scratch = [pltpu.VMEM((N_EXPERTS, LANES), F32)]
    scratch.append(pltpu.VMEM((4, ATT_WIDTH // LANES, tm, LANES), F32))
    return pl.pallas_call(
        functools.partial(_outproj_body, route=route),
        grid=(TOKENS // tm,),
        in_specs=in_specs,
        out_specs=out_specs,
        out_shape=out_shape,
        scratch_shapes=scratch,
        compiler_params=_params(("arbitrary",) if route else ("parallel",)),
        name="outproj_route" if route else "outproj",
    )(*args)


def _swiglu_accumulate(h, w1_ref, w3_ref, w2_ref, acc_ref, tf):
    for c0 in range(0, tf, FFN_SUB):
        c1 = min(c0 + FFN_SUB, tf)
        a = _dot(h, w1_ref[:, c0:c1])
        b = _dot(h, w3_ref[:, c0:c1])
        act = a * _sigmoid(a) * b
        acc_ref[...] += _dot(act.astype(BF16), w2_ref[c0:c1, :])


def _ffn_body(x_ref, h_ref, w1_ref, w3_ref, w2_ref, o_ref, *, tf):
    @pl.when(pl.program_id(1) == 0)
    def _():
        o_ref[...] = x_ref[...]

    _swiglu_accumulate(h_ref[...], w1_ref, w3_ref, w2_ref, o_ref, tf)


def _ffn(x, h, w1, w3, w2, *, tf):
    f = w1.shape[1]
    tm = TM_FFN
    tile = lambda cols: pl.BlockSpec((tm, cols), lambda i, j: (i, 0))
    return pl.pallas_call(
        functools.partial(_ffn_body, tf=tf),
        grid=(TOKENS // tm, f // tf),
        in_specs=[tile(D_MODEL), tile(D_MODEL),
                  pl.BlockSpec((D_MODEL, tf), lambda i, j: (0, j)),
                  pl.BlockSpec((D_MODEL, tf), lambda i, j: (0, j)),
                  pl.BlockSpec((tf, D_MODEL), lambda i, j: (j, 0))],
        out_specs=tile(D_MODEL),
        out_shape=jax.ShapeDtypeStruct((TOKENS, D_MODEL), F32),
        compiler_params=_params(("parallel", "arbitrary")),
        name="dense_ffn",
    )(x, h, w1, w3, w2)


ROW_TILE = D_MODEL // LANES


def _to_row_tiled(dst_ref, lead, val):
    rows = val.shape[0]
    for s in range(ROW_TILE):
        dst_ref[(*lead, pl.ds(s, rows, stride=ROW_TILE), slice(None))] = val[:, s * LANES:(s + 1) * LANES]


def _from_row_tiled(src_ref, lead, rows):
    return jnp.concatenate([src_ref[(*lead, pl.ds(s, rows, stride=ROW_TILE), slice(None))]
                            for s in range(ROW_TILE)], axis=-1)


def _row_tile(idx):
    return pl.ds(pl.multiple_of(idx * ROW_TILE, ROW_TILE), ROW_TILE)


def _dispatch_body(pos_ref, last_tile_ref, h_ref, xs_hbm, stage_ref, zero_ref, sem, zero_sem):
    i = pl.program_id(0)
    n = pl.num_programs(0)
    tm = DISPATCH_CHUNK
    slot = i % 2

    def drain(s):
        for _ in range(2):
            pltpu.make_async_copy(stage_ref.at[s], xs_hbm.at[pl.ds(0, tm * ROW_TILE)], sem.at[s]).wait()

    @pl.when(i == 0)
    def _():
        zero_ref[...] = jnp.zeros_like(zero_ref)

        def zero_copy(e):
            start = pl.multiple_of(last_tile_ref[e] * (TM_MOE * ROW_TILE), TM_MOE * ROW_TILE)
            return pltpu.make_async_copy(zero_ref, xs_hbm.at[pl.ds(start, TM_MOE * ROW_TILE)], zero_sem)

        for e in range(2 * N_EXPERTS):
            @pl.when(last_tile_ref[e] >= 0)
            def _():
                zero_copy(e).start()
        for e in range(2 * N_EXPERTS):
            @pl.when(last_tile_ref[e] >= 0)
            def _():
                zero_copy(e).wait()

    @pl.when(i >= 2)
    def _():
        drain(slot)

    _to_row_tiled(stage_ref, (slot,), h_ref[...])

    def body(t, carry):
        for k in range(2):
            dst = pos_ref[2 * (i * tm + t) + k]
            pltpu.make_async_copy(stage_ref.at[slot, _row_tile(t)], xs_hbm.at[_row_tile(dst)],
                                  sem.at[slot]).start(priority=k)
        return carry
    lax.fori_loop(0, tm, body, 0, unroll=8)

    @pl.when(i == n - 1)
    def _():
        drain(1 - slot)
        drain(slot)


def _dispatch(pos, last_tile, h):
    tm = DISPATCH_CHUNK
    return pl.pallas_call(
        _dispatch_body,
        grid=(TOKENS // tm,),
        in_specs=[pl.BlockSpec(memory_space=pltpu.SMEM),
                  pl.BlockSpec(memory_space=pltpu.SMEM),
                  pl.BlockSpec((tm, D_MODEL), lambda i: (i, 0))],
        out_specs=pl.BlockSpec(memory_space=pl.ANY),
        out_shape=jax.ShapeDtypeStruct((N_SORTED * ROW_TILE, LANES), F32),
        scratch_shapes=[pltpu.VMEM((2, tm * ROW_TILE, LANES), F32),
                        pltpu.VMEM((TM_MOE * ROW_TILE, LANES), F32),
                        pltpu.SemaphoreType.DMA((2,)), pltpu.SemaphoreType.DMA(())],
        compiler_params=_params(("arbitrary",)),
        name="moe_dispatch",
    )(pos, last_tile, h)


def _gffn_body(te_ref, nu_ref, xs_ref, w1_ref, w3_ref, w2_ref, o_ref, hb_ref, acc_ref):
    del te_ref
    i = pl.program_id(0)
    j = pl.program_id(1)

    used = i < nu_ref[0]

    @pl.when(j == 0)
    def _():
        acc_ref[...] = jnp.zeros_like(acc_ref)

    @pl.when(used & (j == 0))
    def _():
        hb_ref[...] = _from_row_tiled(xs_ref, (), TM_MOE).astype(BF16)

    @pl.when(used)
    def _():
        _swiglu_accumulate(hb_ref[...], w1_ref, w3_ref, w2_ref, acc_ref, TF_MOE)

    @pl.when(j == pl.num_programs(1) - 1)
    def _():
        _to_row_tiled(o_ref, (), acc_ref[...])


def _grouped_ffn(tile_expert, n_used, xs, w1, w3, w2):
    nj = FFN_EXPERT // TF_MOE
    col = lambda i, j, nu: jnp.where(i < nu[0], j, nj - 1)
    grid_spec = pltpu.PrefetchScalarGridSpec(
        num_scalar_prefetch=2,
        grid=(N_TILES_MOE, nj),
        in_specs=[
            pl.BlockSpec((TM_MOE * ROW_TILE, LANES), lambda i, j, te, nu: (jnp.minimum(i, nu[0] - 1), 0)),
            pl.BlockSpec((None, D_MODEL, TF_MOE), lambda i, j, te, nu: (te[i], 0, col(i, j, nu))),
            pl.BlockSpec((None, D_MODEL, TF_MOE), lambda i, j, te, nu: (te[i], 0, col(i, j, nu))),
            pl.BlockSpec((None, TF_MOE, D_MODEL), lambda i, j, te, nu: (te[i], col(i, j, nu), 0)),
        ],
        out_specs=pl.BlockSpec((TM_MOE * ROW_TILE, LANES), lambda i, j, te, nu: (i, 0)),
        scratch_shapes=[pltpu.VMEM((TM_MOE, D_MODEL), BF16), pltpu.VMEM((TM_MOE, D_MODEL), F32)],
    )
    return pl.pallas_call(
        _gffn_body,
        grid_spec=grid_spec,
        out_shape=jax.ShapeDtypeStruct((N_SORTED * ROW_TILE, LANES), F32),
        compiler_params=_params(("arbitrary", "arbitrary")),
        name="moe_ffn",
    )(tile_expert, n_used, xs, w1, w3, w2)


def _combine_body(pos_ref, x_ref, gate_ref, g_ref, ys_hbm, o_ref, buf_ref, sem):
    i = pl.program_id(0)
    n = pl.num_programs(0)
    tm = TM_COMBINE

    def issue(tile, slot):
        def body(t, carry):
            for k in range(2):
                src = pos_ref[2 * (tile * tm + t) + k]
                pltpu.make_async_copy(ys_hbm.at[_row_tile(src)], buf_ref.at[slot, k, _row_tile(t)],
                                      sem.at[slot]).start(priority=k)
            return carry
        lax.fori_loop(0, tm, body, 0, unroll=8)

    @pl.when(i == 0)
    def _():
        issue(0, 0)

    @pl.when(i + 1 < n)
    def _():
        issue(i + 1, (i + 1) % 2)

    slot = i % 2
    for k in range(2):
        pltpu.make_async_copy(ys_hbm.at[pl.ds(0, tm * ROW_TILE)], buf_ref.at[slot, k], sem.at[slot]).wait()
    w1 = gate_ref[:, 0:1]
    w2 = gate_ref[:, 1:2]
    y = x_ref[...] + w1 * _from_row_tiled(buf_ref, (slot, 0), tm) + w2 * _from_row_tiled(buf_ref, (slot, 1), tm)
    o_ref[...] = _rms(y, g_ref[...])


def _combine(pos, x, gates, g, ys):
    tm = TM_COMBINE
    return pl.pallas_call(
        _combine_body,
        grid=(TOKENS // tm,),
        in_specs=[pl.BlockSpec(memory_space=pltpu.SMEM),
                  pl.BlockSpec((tm, D_MODEL), lambda i: (i, 0)),
                  pl.BlockSpec((tm, 2), lambda i: (i, 0)),
                  pl.BlockSpec((1, D_MODEL), lambda i: (0, 0)),
                  pl.BlockSpec(memory_space=pl.ANY)],
        out_specs=pl.BlockSpec((tm, D_MODEL), lambda i: (i, 0)),
        out_shape=jax.ShapeDtypeStruct((TOKENS, D_MODEL), F32),
        scratch_shapes=[pltpu.VMEM((2, 2, tm * ROW_TILE, LANES), F32), pltpu.SemaphoreType.DMA((2,))],
        compiler_params=_params(("arbitrary",)),
        name="moe_combine",
    )(pos, x, gates, g, ys)


def _routing_tables(route, counts):
    cnt = counts[:, 0].astype(jnp.int32)
    tiles = (cnt + TM_MOE - 1) // TM_MOE
    tile_end = jnp.cumsum(tiles)
    tile_start = tile_end - tiles
    n_used = tile_end[-1]
    expert = route[ROUTE_I1:ROUTE_I2 + 1].astype(jnp.int32).T
    rank = route[ROUTE_R1:ROUTE_R2 + 1].astype(jnp.int32).T
    pos = (tile_start * TM_MOE)[expert] + rank
    tile_id = jnp.minimum(jnp.arange(N_TILES_MOE, dtype=jnp.int32), n_used - 1)
    tile_expert = jnp.sum(tile_id[:, None] >= tile_end[None, :], axis=1).astype(jnp.int32)
    last_tile = jnp.where(tiles > 0, tile_end - 1, -1)
    spare = n_used + jnp.arange(N_EXPERTS)
    zero_tiles = jnp.concatenate([last_tile, jnp.where(spare < N_TILES_MOE, spare, -1)]).astype(jnp.int32)
    return pos.reshape(2 * TOKENS), tile_expert, n_used.reshape(1), zero_tiles


def _moe(x, h, route, counts, w1, w3, w2, g_final):
    pos, tile_expert, n_used, last_tile = _routing_tables(route, counts)
    xs = _dispatch(pos, last_tile, h)
    ys = _grouped_ffn(tile_expert, n_used, xs, w1, w3, w2)
    return _combine(pos, x, route[ROUTE_W1:ROUTE_W2 + 1].T, g_final, ys)


def _prep_w_in(w):
    aq, ak, av, gq, gk, gv, gr, glr, c_in, c_b, c_c = jnp.split(w, np.cumsum(SPLIT_SIZES)[:-1].tolist(), axis=1)
    pad = jnp.zeros((D_MODEL, LANES - GLA_RANK), w.dtype)
    return jnp.concatenate([aq, ak, av, gv, gr, gq, gk, c_in, c_b, c_c, glr, pad], axis=1).astype(BF16)


def _prep_router(w):
    wt = w.T
    hi = wt.astype(BF16)
    lo = (wt - hi.astype(F32)).astype(BF16)
    return jnp.stack([jnp.concatenate([hi, lo]), jnp.concatenate([hi, jnp.zeros_like(hi)])])


def kernel(x, w_mix_in, w_mix_out, g_mix, rel_bias, gla_w_gate, gla_b_gate, gla_g_norm, conv_w,
           g_ffn, ffn_w1, ffn_w3, ffn_w2, moe_router, moe_w1, moe_w3, moe_w2, g_final):
    assert DEPTH == 2
    x = x.reshape(TOKENS, D_MODEL)
    for layer in range(DEPTH):
        u, *qkvs = _inproj(x, g_mix[layer].reshape(1, D_MODEL), _prep_w_in(w_mix_in[layer]))
        att = _attention(qkvs, rel_bias)
        wg = jnp.pad(gla_w_gate[layer], ((0, LANES - GLA_RANK), (0, 0)))
        go = _gla(u, wg, gla_b_gate[layer].reshape(1, GLA_QK), gla_g_norm[layer].reshape(1, GLA_DV))
        cv = _conv(u, jnp.pad(conv_w[layer], ((0, 8 - CONV_WIDTH), (0, 0))))
        w_out = w_mix_out[layer].astype(BF16)
        g2 = g_ffn[layer].reshape(1, D_MODEL)
        i = layer // 2
        if layer % 2 == 0:
            x, h = _outproj(att, go, cv, x, w_out, g2)
            x = _ffn(x, h, ffn_w1[i].astype(BF16), ffn_w3[i].astype(BF16), ffn_w2[i].astype(BF16), tf=TF_DENSE)
        else:
            x, h, route, counts = _outproj(att, go, cv, x, w_out, g2, _prep_router(moe_router[i]))
            x = _moe(x, h, route, counts, moe_w1[i].astype(BF16), moe_w3[i].astype(BF16),
                     moe_w2[i].astype(BF16), g_final.reshape(1, D_MODEL))
    return x.reshape(BATCH, SEQ, D_MODEL)
```

```python
import functools
import math

import jax
import jax.numpy as jnp
import numpy as np
from jax import lax
from jax.experimental import pallas as pl
from jax.experimental.pallas import tpu as pltpu

F32 = jnp.float32
BF16 = jnp.bfloat16

D_MODEL = 1024
BATCH = 8
SEQ = 2048
TOKENS = BATCH * SEQ
DEPTH = 2
EPS = 1e-6

HEAD_DIM = 64
ATT_HEADS = 4
ATT_WIDTH = ATT_HEADS * HEAD_DIM
DILATED_PATTERNS = ((128, 1), (512, 4), (2048, 16))
ATT_BLOCK = 128
REL_BUCKETS = 32
REL_MAX_DISTANCE = 2048

GLA_HEADS = 4
GLA_DK = 64
GLA_DV = 128
GLA_RANK = 16
GLA_CHUNK = 64
GLA_QK = GLA_HEADS * GLA_DK
GLA_V = GLA_HEADS * GLA_DV

CONV_CH = 256
CONV_WIDTH = 3
MIX_WIDTH = ATT_WIDTH + GLA_V + CONV_CH

SPLIT_SIZES = (ATT_WIDTH, ATT_WIDTH, ATT_WIDTH, GLA_QK, GLA_QK, GLA_V, GLA_V, GLA_RANK,
               CONV_CH, CONV_CH, CONV_CH)

FFN_DENSE = 2816
N_EXPERTS = 8
FFN_EXPERT = 3584

LANES = 128
MXU_WIDTH = 256
VMEM_LIMIT = 56 * 1024 * 1024

QKV_COLS = 3 * ATT_WIDTH
COL_GV, COL_GR, COL_GQ, COL_GK = 0, 512, 1024, 1280
COL_CIN, COL_CB, COL_CC, COL_GLR = 1536, 1792, 2048, 2304
U_COLS = COL_GLR + LANES

NEG_BIG = -1e30

ATT_UNROLL = 15
TM_PROJ = 512
TC_GLA = 512
TM_FFN = 1024
TF_DENSE = 1408
FFN_SUB = 512
TM_MOE = 512
TF_MOE = 1792
N_TILES_MOE = 2 * TOKENS // TM_MOE + N_EXPERTS
N_SORTED = N_TILES_MOE * TM_MOE
DISPATCH_CHUNK = 256
TM_COMBINE = 256

ROUTE_I1, ROUTE_I2, ROUTE_W1, ROUTE_W2, ROUTE_R1, ROUTE_R2 = range(6)


def _params(sem):
    return pltpu.CompilerParams(dimension_semantics=sem, vmem_limit_bytes=VMEM_LIMIT)


def _split_bf16(a):
    hi = a.astype(BF16)
    lo = (a - hi.astype(F32)).astype(BF16)
    return hi, lo


def _dot(a, b):
    return jnp.dot(a, b, preferred_element_type=F32)


def _dot3(a, b):
    a_hi, a_lo = _split_bf16(a)
    b_hi, b_lo = _split_bf16(b)
    return _dot(a_hi, b_hi) + _dot(a_lo, b_hi) + _dot(a_hi, b_lo)


def _dot_nt(a, b):
    return lax.dot_general(a, b, (((1,), (1,)), ((), ())), preferred_element_type=F32)


def _dot_tn(a, b):
    return lax.dot_general(a, b, (((0,), (0,)), ((), ())), preferred_element_type=F32)


def _rms(x, g):
    ms = jnp.mean(x * x, axis=-1, keepdims=True)
    return x * lax.rsqrt(ms + EPS) * g


def _sigmoid(x):
    return 1.0 / (1.0 + jnp.exp(-x))


def _inproj_body(x_ref, g_ref, w_ref, o_ref, *rest):
    qkv_refs, qkv_f32 = rest[:-1], rest[-1]
    h = _rms(x_ref[...], g_ref[...]).astype(BF16)
    for c0 in range(0, QKV_COLS, MXU_WIDTH):
        res = _dot(h, w_ref[:, c0:c0 + MXU_WIDTH])
        for t in range(MXU_WIDTH // LANES):
            qkv_f32[c0 // LANES + t] = res[:, t * LANES:(t + 1) * LANES]
    for (_, dilation), ref in zip(DILATED_PATTERNS, qkv_refs):
        for r in range(dilation):
            rows = pl.ds(r, TM_PROJ // dilation, stride=dilation)
            ref[r] = jnp.concatenate([qkv_f32[t, rows, :] for t in range(QKV_COLS // LANES)],
                                     axis=-1).astype(ref.dtype)
    for c0 in range(0, U_COLS, MXU_WIDTH):
        c1 = min(c0 + MXU_WIDTH, U_COLS)
        o_ref[:, c0:c1] = _dot(h, w_ref[:, QKV_COLS + c0:QKV_COLS + c1])


def _subseq_spec(dilation, cols):
    tiles = SEQ // TM_PROJ
    return pl.BlockSpec((None, dilation, TM_PROJ // dilation, cols), lambda i: (i // tiles, 0, i % tiles, 0))


def _inproj(x, g, w):
    qkv_shapes = [jax.ShapeDtypeStruct((BATCH, d, SEQ // d, QKV_COLS), BF16) for _, d in DILATED_PATTERNS]
    return pl.pallas_call(
        _inproj_body,
        grid=(TOKENS // TM_PROJ,),
        in_specs=[
            pl.BlockSpec((TM_PROJ, D_MODEL), lambda i: (i, 0)),
            pl.BlockSpec((1, D_MODEL), lambda i: (0, 0)),
            pl.BlockSpec((D_MODEL, QKV_COLS + U_COLS), lambda i: (0, 0)),
        ],
        out_specs=[pl.BlockSpec((TM_PROJ, U_COLS), lambda i: (i, 0))]
                  + [_subseq_spec(d, QKV_COLS) for _, d in DILATED_PATTERNS],
        out_shape=[jax.ShapeDtypeStruct((TOKENS, U_COLS), F32)] + qkv_shapes,
        scratch_shapes=[pltpu.VMEM((QKV_COLS // LANES, TM_PROJ, LANES), F32)],
        compiler_params=_params(("parallel",)),
        name="inproj",
    )(x, g, w)


def _rel_bucket(dist):
    max_exact = REL_BUCKETS // 2
    d = jnp.maximum(dist, 0)
    log_ratio = jnp.log(jnp.maximum(d, 1).astype(F32) / max_exact) / math.log(REL_MAX_DISTANCE / max_exact)
    large = jnp.minimum(max_exact + (log_ratio * (REL_BUCKETS - max_exact)).astype(jnp.int32), REL_BUCKETS - 1)
    return jnp.where(d < max_exact, d, large)


def _bucket_table(window, dilation):
    span = window // dilation
    qi = jnp.arange(ATT_BLOCK)[:, None]
    kj = jnp.arange(2 * ATT_BLOCK)[None, :]
    sub_dist = qi - kj + ATT_BLOCK
    band = (sub_dist >= 0) & (sub_dist <= span)
    return jnp.where(band, _rel_bucket(sub_dist * dilation), -1).astype(jnp.int32)


def _attn_body(rb_ref, bidx_ref, qkv_ref, o_ref, lse_ref, bias_ref, *, sub_blocks, unroll):
    nblk = SEQ // ATT_BLOCK

    @pl.when(pl.program_id(0) == 0)
    def _():
        bidx = bidx_ref[...]
        in_prev = lax.broadcasted_iota(jnp.int32, bidx.shape, 1) < ATT_BLOCK
        for h in range(ATT_HEADS):
            acc = jnp.full(bidx.shape, NEG_BIG, F32)
            for b in range(REL_BUCKETS):
                acc = jnp.where(bidx == b, rb_ref[b, h], acc)
            bias_ref[0, h] = acc
            bias_ref[1, h] = jnp.where(in_prev, NEG_BIG, acc)
            bias_ref[2, h] = jnp.concatenate([acc[:, ATT_BLOCK:], jnp.full_like(acc[:, ATT_BLOCK:], NEG_BIG)], axis=1)

    def block(n, first):
        if first:
            rows, krows, variant = slice(0, ATT_BLOCK), slice(0, 2 * ATT_BLOCK), 2
        else:
            r0 = pl.multiple_of(n * ATT_BLOCK, ATT_BLOCK)
            rows = pl.ds(r0, ATT_BLOCK)
            krows = pl.ds(r0 - ATT_BLOCK, 2 * ATT_BLOCK)
            if sub_blocks == nblk:
                variant = 0
            elif sub_blocks == 1:
                variant = 1
            else:
                variant = jnp.where(n % sub_blocks == 0, 1, 0)
        q = qkv_ref[rows, 0:ATT_WIDTH]
        kk = qkv_ref[krows, ATT_WIDTH:2 * ATT_WIDTH]
        vv = qkv_ref[krows, 2 * ATT_WIDTH:3 * ATT_WIDTH]
        q = q * jnp.asarray(HEAD_DIM ** -0.5, BF16)
        head_of_lane = lax.broadcasted_iota(jnp.int32, (ATT_BLOCK, ATT_WIDTH), 1) // HEAD_DIM
        ones = jnp.ones((kk.shape[0], LANES), BF16)
        num = den = mx = None
        for h in range(ATT_HEADS):
            mine = head_of_lane == h
            bias = bias_ref[variant, h]
            s = _dot_nt(jnp.where(mine, q, jnp.zeros_like(q)), kk) + bias
            m = jnp.max(s, axis=-1, keepdims=True)
            p = jnp.exp(s - m).astype(BF16)
            num_h = _dot(p, vv)
            den_h = jnp.tile(_dot(p, ones), (1, ATT_WIDTH // LANES))
            m_h = jnp.broadcast_to(m, (ATT_BLOCK, ATT_WIDTH))
            num = num_h if h == 0 else jnp.where(mine, num_h, num)
            den = den_h if h == 0 else jnp.where(mine, den_h, den)
            mx = m_h if h == 0 else jnp.where(mine, m_h, mx)
        o_ref[rows, :] = (num / den).astype(o_ref.dtype)
        lse_ref[rows, :] = mx + jnp.log(den)

    block(0, True)

    def loop_body(n, carry):
        block(n, False)
        return carry
    lax.fori_loop(1, nblk, loop_body, 0, unroll=unroll)


def _attention_pattern(ua, rel_bias, window, dilation):
    L = SEQ // dilation
    shape = (BATCH, dilation, L, ATT_WIDTH)
    qkv_spec = pl.BlockSpec((None, SEQ, QKV_COLS), lambda b: (b, 0, 0))
    out_spec = pl.BlockSpec((None, SEQ, ATT_WIDTH), lambda b: (b, 0, 0))
    o, lse = pl.pallas_call(
        functools.partial(_attn_body, sub_blocks=L // ATT_BLOCK, unroll=ATT_UNROLL),
        grid=(BATCH,),
        in_specs=[
            pl.BlockSpec(memory_space=pltpu.SMEM),
            pl.BlockSpec((ATT_BLOCK, 2 * ATT_BLOCK), lambda b: (0, 0)),
            qkv_spec,
        ],
        out_specs=[out_spec, out_spec],
        out_shape=[jax.ShapeDtypeStruct((BATCH, SEQ, ATT_WIDTH), BF16),
                   jax.ShapeDtypeStruct((BATCH, SEQ, ATT_WIDTH), F32)],
        scratch_shapes=[pltpu.VMEM((3, ATT_HEADS, ATT_BLOCK, 2 * ATT_BLOCK), F32)],
        compiler_params=_params(("arbitrary",)),
        name=f"attn_d{dilation}",
    )(rel_bias, _bucket_table(window, dilation), ua.reshape(BATCH, SEQ, QKV_COLS))
    return o.reshape(shape), lse.reshape(shape)


def _attention(qkvs, rel_bias):
    return [_attention_pattern(ua, rel_bias, window, dilation)
            for ua, (window, dilation) in zip(qkvs, DILATED_PATTERNS)]


def _short_conv(cin_ref, cb_ref, cc_ref, w_ref, o_ref, tail_ref):
    uu = cc_ref[...] * cin_ref[...]
    t = lax.broadcasted_iota(jnp.int32, uu.shape, 0)
    y = uu * w_ref[CONV_WIDTH - 1:CONV_WIDTH, :]
    for shift in range(1, CONV_WIDTH):
        prev = pltpu.roll(uu, shift, axis=0)
        for r in range(shift):
            prev = jnp.where(t == r, tail_ref[8 - shift + r:8 - shift + r + 1, :], prev)
        y = y + prev * w_ref[CONV_WIDTH - 1 - shift:CONV_WIDTH - shift, :]
    o_ref[...] = (cb_ref[...] * y).astype(o_ref.dtype)
    tail_ref[...] = uu[uu.shape[0] - 8:, :]


def _gla_body(q_ref, k_ref, v_ref, gr_ref, glr_ref, wg_ref, bg_ref, gn_ref, cin_ref, cb_ref, cc_ref, cw_ref,
              o_ref, cv_ref, s_ref, tail_ref):
    @pl.when(pl.program_id(1) == 0)
    def _():
        s_ref[...] = jnp.zeros_like(s_ref)
        tail_ref[...] = jnp.zeros_like(tail_ref)

    _short_conv(cin_ref, cb_ref, cc_ref, cw_ref, cv_ref, tail_ref)

    C = GLA_CHUNK
    row = lax.broadcasted_iota(jnp.int32, (C, C), 0)
    col = lax.broadcasted_iota(jnp.int32, (C, C), 1)
    tril = row >= col
    tril_bf = jnp.where(tril, 1.0, 0.0).astype(BF16)
    ones_bf = jnp.ones((C, GLA_DV), BF16)

    xg = _dot3(glr_ref[...], wg_ref[...]) + bg_ref[...]
    la_all = (jnp.minimum(xg, 0.0) - jnp.log(1.0 + jnp.exp(-jnp.abs(xg)))) * (1.0 / 16.0)

    for c in range(TC_GLA // C):
        rows = slice(c * C, (c + 1) * C)
        la_hi, la_lo = _split_bf16(la_all[rows])
        cum = _dot(tril_bf, la_hi) + _dot(tril_bf, la_lo)
        last = cum[C - 1:C, :]
        q = q_ref[rows, :]
        k = k_ref[rows, :]
        qt = (q * jnp.exp(cum) * (GLA_DK ** -0.5)).astype(BF16)
        kt = (k * jnp.exp(-cum)).astype(BF16)
        kl = (k * jnp.exp(last - cum)).astype(BF16)
        for h in range(GLA_HEADS):
            sl = slice(h * GLA_DK, (h + 1) * GLA_DK)
            vs = slice(h * GLA_DV, (h + 1) * GLA_DV)
            vh = v_ref[rows, vs].astype(BF16)
            state = s_ref[h]
            st_hi, st_lo = _split_bf16(state)
            sc = jnp.where(tril, _dot_nt(qt[:, sl], kt[:, sl]), 0.0).astype(BF16)
            o = _dot(qt[:, sl], st_hi) + _dot(qt[:, sl], st_lo) + _dot(sc, vh)
            ltot = _dot_tn(la_hi[:, sl], ones_bf) + _dot_tn(la_lo[:, sl], ones_bf)
            s_ref[h] = jnp.exp(ltot) * state + _dot_tn(kl[:, sl], vh)
            g = gr_ref[rows, vs]
            o_ref[rows, vs] = (_rms(o, gn_ref[...]) * (g * _sigmoid(g))).astype(o_ref.dtype)


def _gla_conv(u, wg, bg, gn, conv_w):
    nj = SEQ // TC_GLA
    row = lambda b, j: b * nj + j
    tile = lambda width, col: pl.BlockSpec((TC_GLA, width), lambda b, j: (row(b, j), col // width))
    full = lambda a: pl.BlockSpec(a.shape, lambda b, j: (0, 0))
    return pl.pallas_call(
        _gla_body,
        grid=(BATCH, nj),
        in_specs=[
            tile(GLA_QK, COL_GQ), tile(GLA_QK, COL_GK), tile(GLA_V, COL_GV), tile(GLA_V, COL_GR),
            tile(LANES, COL_GLR), full(wg), full(bg), full(gn),
            tile(CONV_CH, COL_CIN), tile(CONV_CH, COL_CB), tile(CONV_CH, COL_CC), full(conv_w),
        ],
        out_specs=[pl.BlockSpec((TC_GLA, GLA_V), lambda b, j: (row(b, j), 0)),
                   pl.BlockSpec((TC_GLA, CONV_CH), lambda b, j: (row(b, j), 0))],
        out_shape=[jax.ShapeDtypeStruct((TOKENS, GLA_V), BF16), jax.ShapeDtypeStruct((TOKENS, CONV_CH), BF16)],
        scratch_shapes=[pltpu.VMEM((GLA_HEADS, GLA_DK, GLA_DV), F32), pltpu.VMEM((8, CONV_CH), F32)],
        compiler_params=_params(("arbitrary", "arbitrary")),
        name="gla_conv",
    )(u, u, u, u, u, wg, bg, gn, u, u, u, conv_w)


def _outproj_body(*refs, route):
    (o1, o4, o16, l1, l4, l16, go_ref, cv_ref, x_ref, w_ref, g_ref) = refs[:11]
    perm_ref = refs[-1]
    if route:
        wr_ref, xo_ref, ho_ref, route_ref, counts_ref, carry_ref = refs[11:-1]
    else:
        xo_ref, ho_ref = refs[11:-1]

    def token_order(ref, slot):
        dilation, rows, _ = ref.shape
        if dilation == 1:
            return ref[0].astype(F32)
        tiles = range(ATT_WIDTH // LANES)
        for r in range(dilation):
            val = ref[r].astype(F32)
            for t in tiles:
                perm_ref[slot, t, pl.ds(r, rows, stride=dilation), :] = val[:, t * LANES:(t + 1) * LANES]
        return jnp.concatenate([perm_ref[slot, t] for t in tiles], axis=-1)

    la, lb, lc = token_order(l1, 0), token_order(l4, 0), token_order(l16, 1)
    oa, ob, oc = token_order(o1, 0), token_order(o4, 2), token_order(o16, 3)
    m = jnp.maximum(jnp.maximum(la, lb), lc)
    ea, eb, ec = jnp.exp(la - m), jnp.exp(lb - m), jnp.exp(lc - m)
    att = (ea * oa + eb * ob + ec * oc) / (ea + eb + ec)
    y = (x_ref[...]
         + _dot(att.astype(BF16), w_ref[0:ATT_WIDTH, :])
         + _dot(go_ref[...], w_ref[ATT_WIDTH:ATT_WIDTH + GLA_V, :])
         + _dot(cv_ref[...], w_ref[ATT_WIDTH + GLA_V:MIX_WIDTH, :]))
    xo_ref[...] = y
    hf = _rms(y, g_ref[...])
    ho_ref[...] = hf.astype(ho_ref.dtype)
    if route:
        @pl.when(pl.program_id(0) == 0)
        def _():
            carry_ref[...] = jnp.zeros_like(carry_ref)

        tm = hf.shape[0]
        ne = N_EXPERTS
        hf_hi, hf_lo = _split_bf16(hf)
        part = _dot_nt(wr_ref[0], hf_hi) + _dot_nt(wr_ref[1], hf_lo)
        logits = part[0:ne] + part[ne:2 * ne]
        eidx = lax.broadcasted_iota(jnp.int32, logits.shape, 0).astype(F32)
        v1 = jnp.max(logits, axis=0, keepdims=True)
        i1 = jnp.min(jnp.where(logits == v1, eidx, float(ne)), axis=0, keepdims=True)
        lg2 = jnp.where(eidx == i1, -jnp.inf, logits)
        v2 = jnp.max(lg2, axis=0, keepdims=True)
        i2 = jnp.min(jnp.where(lg2 == v2, eidx, float(ne)), axis=0, keepdims=True)
        e2 = jnp.exp(v2 - v1)
        w1 = 1.0 / (1.0 + e2)
        w2 = e2 * w1
        sel1 = eidx == i1
        sel2 = eidx == i2
        onehot = jnp.where(sel1, 1.0, jnp.where(sel2, 1.0, 0.0))
        tri = (lax.broadcasted_iota(jnp.int32, (tm, tm), 0) <= lax.broadcasted_iota(jnp.int32, (tm, tm), 1))
        onehot16 = jnp.concatenate([onehot, jnp.zeros_like(onehot)], axis=0).astype(BF16)
        csum = _dot(onehot16, jnp.where(tri, 1.0, 0.0).astype(BF16))[0:ne]
        carry = carry_ref[:, 0:1]
        rank = csum - onehot + carry
        r1 = jnp.sum(jnp.where(sel1, rank, 0.0), axis=0, keepdims=True)
        r2 = jnp.sum(jnp.where(sel2, rank, 0.0), axis=0, keepdims=True)
        total = jnp.broadcast_to(carry + csum[:, tm - 1:tm], carry_ref.shape)
        carry_ref[...] = total
        counts_ref[...] = total
        rows = {ROUTE_I1: i1, ROUTE_I2: i2, ROUTE_W1: w1, ROUTE_W2: w2, ROUTE_R1: r1, ROUTE_R2: r2}
        zero = jnp.zeros_like(i1)
        route_ref[...] = jnp.concatenate([rows.get(r, zero) for r in range(8)], axis=0)


def _outproj(att, go, cv, x, w, g, w_router=None):
    route = w_router is not None
    tm = TM_PROJ
    tile = lambda cols: pl.BlockSpec((tm, cols), lambda i: (i, 0))
    full = lambda a: pl.BlockSpec(a.shape, lambda i: (0, 0))
    (o1, l1), (o4, l4), (o16, l16) = att
    args = [o1, o4, o16, l1, l4, l16, go, cv, x, w, g]
    att_specs = [_subseq_spec(d, ATT_WIDTH) for _, d in DILATED_PATTERNS]
    in_specs = att_specs * 2 + [tile(GLA_V), tile(CONV_CH), tile(D_MODEL), full(w), full(g)]
    out_specs = [tile(D_MODEL), tile(D_MODEL)]
    out_shape = [jax.ShapeDtypeStruct((TOKENS, D_MODEL), F32),
                 jax.ShapeDtypeStruct((TOKENS, D_MODEL), F32 if route else BF16)]
    scratch = []
    if route:
        args.append(w_router)
        in_specs.append(pl.BlockSpec(w_router.shape, lambda i: (0, 0, 0)))
        out_specs += [pl.BlockSpec((8, tm), lambda i: (0, i)), pl.BlockSpec((N_EXPERTS, LANES), lambda i: (0, 0))]
        out_shape += [jax.ShapeDtypeStruct((8, TOKENS), F32), jax.ShapeDtypeStruct((N_EXPERTS, LANES), F32)]
        scratch = [pltpu.VMEM((N_EXPERTS, LANES), F32)]
    scratch.append(pltpu.VMEM((4, ATT_WIDTH // LANES, tm, LANES), F32))
    return pl.pallas_call(
        functools.partial(_outproj_body, route=route),
        grid=(TOKENS // tm,),
        in_specs=in_specs,
        out_specs=out_specs,
        out_shape=out_shape,
        scratch_shapes=scratch,
        compiler_params=_params(("arbitrary",) if route else ("parallel",)),
        name="outproj_route" if route else "outproj",
    )(*args)


def _swiglu_accumulate(h, w1_ref, w3_ref, w2_ref, acc_ref, tf):
    for c0 in range(0, tf, FFN_SUB):
        c1 = min(c0 + FFN_SUB, tf)
        a = _dot(h, w1_ref[:, c0:c1])
        b = _dot(h, w3_ref[:, c0:c1])
        act = a * _sigmoid(a) * b
        acc_ref[...] += _dot(act.astype(BF16), w2_ref[c0:c1, :])


def _ffn_body(x_ref, h_ref, w1_ref, w3_ref, w2_ref, o_ref, *, tf):
    @pl.when(pl.program_id(1) == 0)
    def _():
        o_ref[...] = x_ref[...]

    _swiglu_accumulate(h_ref[...], w1_ref, w3_ref, w2_ref, o_ref, tf)


def _ffn(x, h, w1, w3, w2, *, tf):
    f = w1.shape[1]
    tm = TM_FFN
    tile = lambda cols: pl.BlockSpec((tm, cols), lambda i, j: (i, 0))
    return pl.pallas_call(
        functools.partial(_ffn_body, tf=tf),
        grid=(TOKENS // tm, f // tf),
        in_specs=[tile(D_MODEL), tile(D_MODEL),
                  pl.BlockSpec((D_MODEL, tf), lambda i, j: (0, j)),
                  pl.BlockSpec((D_MODEL, tf), lambda i, j: (0, j)),
                  pl.BlockSpec((tf, D_MODEL), lambda i, j: (j, 0))],
        out_specs=tile(D_MODEL),
        out_shape=jax.ShapeDtypeStruct((TOKENS, D_MODEL), F32),
        compiler_params=_params(("parallel", "arbitrary")),
        name="dense_ffn",
    )(x, h, w1, w3, w2)


ROW_TILE = D_MODEL // LANES


def _to_row_tiled(dst_ref, lead, val):
    rows = val.shape[0]
    for s in range(ROW_TILE):
        dst_ref[(*lead, pl.ds(s, rows, stride=ROW_TILE), slice(None))] = val[:, s * LANES:(s + 1) * LANES]


def _from_row_tiled(src_ref, lead, rows):
    return jnp.concatenate([src_ref[(*lead, pl.ds(s, rows, stride=ROW_TILE), slice(None))]
                            for s in range(ROW_TILE)], axis=-1)


def _row_tile(idx):
    return pl.ds(pl.multiple_of(idx * ROW_TILE, ROW_TILE), ROW_TILE)


def _dispatch_body(pos_ref, last_tile_ref, h_ref, xs_hbm, stage_ref, zero_ref, sem, zero_sem):
    i = pl.program_id(0)
    n = pl.num_programs(0)
    tm = DISPATCH_CHUNK
    slot = i % 2

    def drain(s):
        for _ in range(2):
            pltpu.make_async_copy(stage_ref.at[s], xs_hbm.at[pl.ds(0, tm * ROW_TILE)], sem.at[s]).wait()

    @pl.when(i == 0)
    def _():
        zero_ref[...] = jnp.zeros_like(zero_ref)

        def zero_copy(e):
            start = pl.multiple_of(last_tile_ref[e] * (TM_MOE * ROW_TILE), TM_MOE * ROW_TILE)
            return pltpu.make_async_copy(zero_ref, xs_hbm.at[pl.ds(start, TM_MOE * ROW_TILE)], zero_sem)

        for e in range(2 * N_EXPERTS):
            @pl.when(last_tile_ref[e] >= 0)
            def _():
                zero_copy(e).start()
        for e in range(2 * N_EXPERTS):
            @pl.when(last_tile_ref[e] >= 0)
            def _():
                zero_copy(e).wait()

    @pl.when(i >= 2)
    def _():
        drain(slot)

    _to_row_tiled(stage_ref, (slot,), h_ref[...])

    def body(t, carry):
        for k in range(2):
            dst = pos_ref[2 * (i * tm + t) + k]
            pltpu.make_async_copy(stage_ref.at[slot, _row_tile(t)], xs_hbm.at[_row_tile(dst)],
                                  sem.at[slot]).start(priority=k)
        return carry
    lax.fori_loop(0, tm, body, 0, unroll=8)

    @pl.when(i == n - 1)
    def _():
        drain(1 - slot)
        drain(slot)


def _dispatch(pos, last_tile, h):
    tm = DISPATCH_CHUNK
    return pl.pallas_call(
        _dispatch_body,
        grid=(TOKENS // tm,),
        in_specs=[pl.BlockSpec(memory_space=pltpu.SMEM),
                  pl.BlockSpec(memory_space=pltpu.SMEM),
                  pl.BlockSpec((tm, D_MODEL), lambda i: (i, 0))],
        out_specs=pl.BlockSpec(memory_space=pl.ANY),
        out_shape=jax.ShapeDtypeStruct((N_SORTED * ROW_TILE, LANES), F32),
        scratch_shapes=[pltpu.VMEM((2, tm * ROW_TILE, LANES), F32),
                        pltpu.VMEM((TM_MOE * ROW_TILE, LANES), F32),
                        pltpu.SemaphoreType.DMA((2,)), pltpu.SemaphoreType.DMA(())],
        compiler_params=_params(("arbitrary",)),
        name="moe_dispatch",
    )(pos, last_tile, h)


def _gffn_body(te_ref, nu_ref, xs_ref, w1_ref, w3_ref, w2_ref, o_ref, hb_ref, acc_ref):
    del te_ref
    i = pl.program_id(0)
    j = pl.program_id(1)

    used = i < nu_ref[0]

    @pl.when(j == 0)
    def _():
        acc_ref[...] = jnp.zeros_like(acc_ref)

    @pl.when(used & (j == 0))
    def _():
        hb_ref[...] = _from_row_tiled(xs_ref, (), TM_MOE).astype(BF16)

    @pl.when(used)
    def _():
        _swiglu_accumulate(hb_ref[...], w1_ref, w3_ref, w2_ref, acc_ref, TF_MOE)

    @pl.when(j == pl.num_programs(1) - 1)
    def _():
        _to_row_tiled(o_ref, (), acc_ref[...])


def _grouped_ffn(tile_expert, n_used, xs, w1, w3, w2):
    nj = FFN_EXPERT // TF_MOE
    col = lambda i, j, nu: jnp.where(i < nu[0], j, nj - 1)
    grid_spec = pltpu.PrefetchScalarGridSpec(
        num_scalar_prefetch=2,
        grid=(N_TILES_MOE, nj),
        in_specs=[
            pl.BlockSpec((TM_MOE * ROW_TILE, LANES), lambda i, j, te, nu: (jnp.minimum(i, nu[0] - 1), 0)),
            pl.BlockSpec((None, D_MODEL, TF_MOE), lambda i, j, te, nu: (te[i], 0, col(i, j, nu))),
            pl.BlockSpec((None, D_MODEL, TF_MOE), lambda i, j, te, nu: (te[i], 0, col(i, j, nu))),
            pl.BlockSpec((None, TF_MOE, D_MODEL), lambda i, j, te, nu: (te[i], col(i, j, nu), 0)),
        ],
        out_specs=pl.BlockSpec((TM_MOE * ROW_TILE, LANES), lambda i, j, te, nu: (i, 0)),
        scratch_shapes=[pltpu.VMEM((TM_MOE, D_MODEL), BF16), pltpu.VMEM((TM_MOE, D_MODEL), F32)],
    )
    return pl.pallas_call(
        _gffn_body,
        grid_spec=grid_spec,
        out_shape=jax.ShapeDtypeStruct((N_SORTED * ROW_TILE, LANES), F32),
        compiler_params=_params(("arbitrary", "arbitrary")),
        name="moe_ffn",
    )(tile_expert, n_used, xs, w1, w3, w2)


def _combine_body(pos_ref, x_ref, gate_ref, g_ref, ys_hbm, o_ref, buf_ref, sem):
    i = pl.program_id(0)
    n = pl.num_programs(0)
    tm = TM_COMBINE

    def issue(tile, slot):
        def body(t, carry):
            for k in range(2):
                src = pos_ref[2 * (tile * tm + t) + k]
                pltpu.make_async_copy(ys_hbm.at[_row_tile(src)], buf_ref.at[slot, k, _row_tile(t)],
                                      sem.at[slot]).start(priority=k)
            return carry
        lax.fori_loop(0, tm, body, 0, unroll=8)

    @pl.when(i == 0)
    def _():
        issue(0, 0)

    @pl.when(i + 1 < n)
    def _():
        issue(i + 1, (i + 1) % 2)

    slot = i % 2
    for k in range(2):
        pltpu.make_async_copy(ys_hbm.at[pl.ds(0, tm * ROW_TILE)], buf_ref.at[slot, k], sem.at[slot]).wait()
    w1 = gate_ref[:, 0:1]
    w2 = gate_ref[:, 1:2]
    y = x_ref[...] + w1 * _from_row_tiled(buf_ref, (slot, 0), tm) + w2 * _from_row_tiled(buf_ref, (slot, 1), tm)
    o_ref[...] = _rms(y, g_ref[...])


def _combine(pos, x, gates, g, ys):
    tm = TM_COMBINE
    return pl.pallas_call(
        _combine_body,
        grid=(TOKENS // tm,),
        in_specs=[pl.BlockSpec(memory_space=pltpu.SMEM),
                  pl.BlockSpec((tm, D_MODEL), lambda i: (i, 0)),
                  pl.BlockSpec((tm, 2), lambda i: (i, 0)),
                  pl.BlockSpec((1, D_MODEL), lambda i: (0, 0)),
                  pl.BlockSpec(memory_space=pl.ANY)],
        out_specs=pl.BlockSpec((tm, D_MODEL), lambda i: (i, 0)),
        out_shape=jax.ShapeDtypeStruct((TOKENS, D_MODEL), F32),
        scratch_shapes=[pltpu.VMEM((2, 2, tm * ROW_TILE, LANES), F32), pltpu.SemaphoreType.DMA((2,))],
        compiler_params=_params(("arbitrary",)),
        name="moe_combine",
    )(pos, x, gates, g, ys)


def _routing_tables(route, counts):
    cnt = counts[:, 0].astype(jnp.int32)
    tiles = (cnt + TM_MOE - 1) // TM_MOE
    tile_end = jnp.cumsum(tiles)
    tile_start = tile_end - tiles
    n_used = tile_end[-1]
    expert = route[ROUTE_I1:ROUTE_I2 + 1].astype(jnp.int32).T
    rank = route[ROUTE_R1:ROUTE_R2 + 1].astype(jnp.int32).T
    pos = (tile_start * TM_MOE)[expert] + rank
    tile_id = jnp.minimum(jnp.arange(N_TILES_MOE, dtype=jnp.int32), n_used - 1)
    tile_expert = jnp.sum(tile_id[:, None] >= tile_end[None, :], axis=1).astype(jnp.int32)
    last_tile = jnp.where(tiles > 0, tile_end - 1, -1)
    spare = n_used + jnp.arange(N_EXPERTS)
    zero_tiles = jnp.concatenate([last_tile, jnp.where(spare < N_TILES_MOE, spare, -1)]).astype(jnp.int32)
    return pos.reshape(2 * TOKENS), tile_expert, n_used.reshape(1), zero_tiles


def _moe(x, h, route, counts, w1, w3, w2, g_final):
    pos, tile_expert, n_used, last_tile = _routing_tables(route, counts)
    xs = _dispatch(pos, last_tile, h)
    ys = _grouped_ffn(tile_expert, n_used, xs, w1, w3, w2)
    return _combine(pos, x, route[ROUTE_W1:ROUTE_W2 + 1].T, g_final, ys)


def _prep_w_in(w):
    aq, ak, av, gq, gk, gv, gr, glr, c_in, c_b, c_c = jnp.split(w, np.cumsum(SPLIT_SIZES)[:-1].tolist(), axis=1)
    pad = jnp.zeros((D_MODEL, LANES - GLA_RANK), w.dtype)
    return jnp.concatenate([aq, ak, av, gv, gr, gq, gk, c_in, c_b, c_c, glr, pad], axis=1).astype(BF16)


def _prep_router(w):
    wt = w.T
    hi = wt.astype(BF16)
    lo = (wt - hi.astype(F32)).astype(BF16)
    return jnp.stack([jnp.concatenate([hi, lo]), jnp.concatenate([hi, jnp.zeros_like(hi)])])


def kernel(x, w_mix_in, w_mix_out, g_mix, rel_bias, gla_w_gate, gla_b_gate, gla_g_norm, conv_w,
           g_ffn, ffn_w1, ffn_w3, ffn_w2, moe_router, moe_w1, moe_w3, moe_w2, g_final):
    assert DEPTH == 2
    x = x.reshape(TOKENS, D_MODEL)
    for layer in range(DEPTH):
        u, *qkvs = _inproj(x, g_mix[layer].reshape(1, D_MODEL), _prep_w_in(w_mix_in[layer]))
        att = _attention(qkvs, rel_bias)
        wg = jnp.pad(gla_w_gate[layer], ((0, LANES - GLA_RANK), (0, 0)))
        go, cv = _gla_conv(u, wg, gla_b_gate[layer].reshape(1, GLA_QK), gla_g_norm[layer].reshape(1, GLA_DV),
                           jnp.pad(conv_w[layer], ((0, 8 - CONV_WIDTH), (0, 0))))
        w_out = w_mix_out[layer].astype(BF16)
        g2 = g_ffn[layer].reshape(1, D_MODEL)
        i = layer // 2
        if layer % 2 == 0:
            x, h = _outproj(att, go, cv, x, w_out, g2)
            x = _ffn(x, h, ffn_w1[i].astype(BF16), ffn_w3[i].astype(BF16), ffn_w2[i].astype(BF16), tf=TF_DENSE)
        else:
            x, h, route, counts = _outproj(att, go, cv, x, w_out, g2, _prep_router(moe_router[i]))
            x = _moe(x, h, route, counts, moe_w1[i].astype(BF16), moe_w3[i].astype(BF16),
                     moe_w2[i].astype(BF16), g_final.reshape(1, D_MODEL))
    return x.reshape(BATCH, SEQ, D_MODEL)
```

```python
import functools
import math

import jax
import jax.numpy as jnp
import numpy as np
from jax import lax
from jax.experimental import pallas as pl
from jax.experimental.pallas import tpu as pltpu

F32 = jnp.float32
BF16 = jnp.bfloat16

D_MODEL = 1024
BATCH = 8
SEQ = 2048
TOKENS = BATCH * SEQ
DEPTH = 2
EPS = 1e-6

HEAD_DIM = 64
ATT_HEADS = 4
ATT_WIDTH = ATT_HEADS * HEAD_DIM
DILATED_PATTERNS = ((128, 1), (512, 4), (2048, 16))
ATT_BLOCK = 128
REL_BUCKETS = 32
REL_MAX_DISTANCE = 2048

GLA_HEADS = 4
GLA_DK = 64
GLA_DV = 128
GLA_RANK = 16
GLA_CHUNK = 64
GLA_QK = GLA_HEADS * GLA_DK
GLA_V = GLA_HEADS * GLA_DV

CONV_CH = 256
CONV_WIDTH = 3
MIX_WIDTH = ATT_WIDTH + GLA_V + CONV_CH

SPLIT_SIZES = (ATT_WIDTH, ATT_WIDTH, ATT_WIDTH, GLA_QK, GLA_QK, GLA_V, GLA_V, GLA_RANK,
               CONV_CH, CONV_CH, CONV_CH)

FFN_DENSE = 2816
N_EXPERTS = 8
FFN_EXPERT = 3584

LANES = 128
MXU_WIDTH = 256
VMEM_LIMIT = 56 * 1024 * 1024

QKV_COLS = 3 * ATT_WIDTH
COL_GV, COL_GR, COL_GQ, COL_GK = 0, 512, 1024, 1280
COL_CIN, COL_CB, COL_CC, COL_GLR = 1536, 1792, 2048, 2304
U_COLS = COL_GLR + LANES

NEG_BIG = -1e30

ATT_UNROLL = 15
TM_PROJ = 512
TC_GLA = 512
TM_FFN = 512
TF_DENSE = 1408
FFN_SUB = 512
TM_MOE = 512
TF_MOE = 1792
N_TILES_MOE = 2 * TOKENS // TM_MOE + N_EXPERTS
N_SORTED = N_TILES_MOE * TM_MOE
DISPATCH_CHUNK = 256
TM_COMBINE = 256

ROUTE_I1, ROUTE_I2, ROUTE_W1, ROUTE_W2, ROUTE_R1, ROUTE_R2 = range(6)


def _params(sem):
    return pltpu.CompilerParams(dimension_semantics=sem, vmem_limit_bytes=VMEM_LIMIT)


def _split_bf16(a):
    hi = a.astype(BF16)
    lo = (a - hi.astype(F32)).astype(BF16)
    return hi, lo


def _dot(a, b):
    return jnp.dot(a, b, preferred_element_type=F32)


def _dot3(a, b):
    a_hi, a_lo = _split_bf16(a)
    b_hi, b_lo = _split_bf16(b)
    return _dot(a_hi, b_hi) + _dot(a_lo, b_hi) + _dot(a_hi, b_lo)


def _dot_nt(a, b):
    return lax.dot_general(a, b, (((1,), (1,)), ((), ())), preferred_element_type=F32)


def _dot_tn(a, b):
    return lax.dot_general(a, b, (((0,), (0,)), ((), ())), preferred_element_type=F32)


def _rms(x, g):
    ms = jnp.mean(x * x, axis=-1, keepdims=True)
    return x * lax.rsqrt(ms + EPS) * g


def _sigmoid(x):
    return 1.0 / (1.0 + jnp.exp(-x))


def _inproj_body(x_ref, g_ref, w_ref, o_ref, *rest):
    qkv_refs, qkv_f32 = rest[:-1], rest[-1]
    h = _rms(x_ref[...], g_ref[...]).astype(BF16)
    for c0 in range(0, QKV_COLS, MXU_WIDTH):
        res = _dot(h, w_ref[:, c0:c0 + MXU_WIDTH])
        for t in range(MXU_WIDTH // LANES):
            qkv_f32[c0 // LANES + t] = res[:, t * LANES:(t + 1) * LANES]
    for (_, dilation), ref in zip(DILATED_PATTERNS, qkv_refs):
        for r in range(dilation):
            rows = pl.ds(r, TM_PROJ // dilation, stride=dilation)
            ref[r] = jnp.concatenate([qkv_f32[t, rows, :] for t in range(QKV_COLS // LANES)],
                                     axis=-1).astype(ref.dtype)
    for c0 in range(0, U_COLS, MXU_WIDTH):
        c1 = min(c0 + MXU_WIDTH, U_COLS)
        o_ref[:, c0:c1] = _dot(h, w_ref[:, QKV_COLS + c0:QKV_COLS + c1])


def _subseq_spec(dilation, cols):
    tiles = SEQ // TM_PROJ
    return pl.BlockSpec((None, dilation, TM_PROJ // dilation, cols), lambda i: (i // tiles, 0, i % tiles, 0))


def _inproj(x, g, w):
    qkv_shapes = [jax.ShapeDtypeStruct((BATCH, d, SEQ // d, QKV_COLS), BF16) for _, d in DILATED_PATTERNS]
    return pl.pallas_call(
        _inproj_body,
        grid=(TOKENS // TM_PROJ,),
        in_specs=[
            pl.BlockSpec((TM_PROJ, D_MODEL), lambda i: (i, 0)),
            pl.BlockSpec((1, D_MODEL), lambda i: (0, 0)),
            pl.BlockSpec((D_MODEL, QKV_COLS + U_COLS), lambda i: (0, 0)),
        ],
        out_specs=[pl.BlockSpec((TM_PROJ, U_COLS), lambda i: (i, 0))]
                  + [_subseq_spec(d, QKV_COLS) for _, d in DILATED_PATTERNS],
        out_shape=[jax.ShapeDtypeStruct((TOKENS, U_COLS), F32)] + qkv_shapes,
        scratch_shapes=[pltpu.VMEM((QKV_COLS // LANES, TM_PROJ, LANES), F32)],
        compiler_params=_params(("parallel",)),
        name="inproj",
    )(x, g, w)


def _rel_bucket(dist):
    max_exact = REL_BUCKETS // 2
    d = jnp.maximum(dist, 0)
    log_ratio = jnp.log(jnp.maximum(d, 1).astype(F32) / max_exact) / math.log(REL_MAX_DISTANCE / max_exact)
    large = jnp.minimum(max_exact + (log_ratio * (REL_BUCKETS - max_exact)).astype(jnp.int32), REL_BUCKETS - 1)
    return jnp.where(d < max_exact, d, large)


def _bucket_table(window, dilation):
    span = window // dilation
    qi = jnp.arange(ATT_BLOCK)[:, None]
    kj = jnp.arange(2 * ATT_BLOCK)[None, :]
    sub_dist = qi - kj + ATT_BLOCK
    band = (sub_dist >= 0) & (sub_dist <= span)
    return jnp.where(band, _rel_bucket(sub_dist * dilation), -1).astype(jnp.int32)


def _attn_body(rb_ref, bidx_ref, qkv_ref, o_ref, lse_ref, bias_ref, *, sub_blocks, unroll):
    nblk = SEQ // ATT_BLOCK

    @pl.when(pl.program_id(0) == 0)
    def _():
        bidx = bidx_ref[...]
        in_prev = lax.broadcasted_iota(jnp.int32, bidx.shape, 1) < ATT_BLOCK
        for h in range(ATT_HEADS):
            acc = jnp.full(bidx.shape, NEG_BIG, F32)
            for b in range(REL_BUCKETS):
                acc = jnp.where(bidx == b, rb_ref[b, h], acc)
            bias_ref[0, h] = acc
            bias_ref[1, h] = jnp.where(in_prev, NEG_BIG, acc)
            bias_ref[2, h] = jnp.concatenate([acc[:, ATT_BLOCK:], jnp.full_like(acc[:, ATT_BLOCK:], NEG_BIG)], axis=1)

    def block(n, first):
        if first:
            rows, krows, variant = slice(0, ATT_BLOCK), slice(0, 2 * ATT_BLOCK), 2
        else:
            r0 = pl.multiple_of(n * ATT_BLOCK, ATT_BLOCK)
            rows = pl.ds(r0, ATT_BLOCK)
            krows = pl.ds(r0 - ATT_BLOCK, 2 * ATT_BLOCK)
            if sub_blocks == nblk:
                variant = 0
            elif sub_blocks == 1:
                variant = 1
            else:
                variant = jnp.where(n % sub_blocks == 0, 1, 0)
        q = qkv_ref[rows, 0:ATT_WIDTH]
        kk = qkv_ref[krows, ATT_WIDTH:2 * ATT_WIDTH]
        vv = qkv_ref[krows, 2 * ATT_WIDTH:3 * ATT_WIDTH]
        q = q * jnp.asarray(HEAD_DIM ** -0.5, BF16)
        head_of_lane = lax.broadcasted_iota(jnp.int32, (ATT_BLOCK, ATT_WIDTH), 1) // HEAD_DIM
        ones = jnp.ones((kk.shape[0], LANES), BF16)
        num = den = mx = None
        for h in range(ATT_HEADS):
            mine = head_of_lane == h
            bias = bias_ref[variant, h]
            s = _dot_nt(jnp.where(mine, q, jnp.zeros_like(q)), kk) + bias
            m = jnp.max(s, axis=-1, keepdims=True)
            p = jnp.exp(s - m).astype(BF16)
            num_h = _dot(p, vv)
            den_h = jnp.tile(_dot(p, ones), (1, ATT_WIDTH // LANES))
            m_h = jnp.broadcast_to(m, (ATT_BLOCK, ATT_WIDTH))
            num = num_h if h == 0 else jnp.where(mine, num_h, num)
            den = den_h if h == 0 else jnp.where(mine, den_h, den)
            mx = m_h if h == 0 else jnp.where(mine, m_h, mx)
        o_ref[rows, :] = (num / den).astype(o_ref.dtype)
        lse_ref[rows, :] = mx + jnp.log(den)

    block(0, True)

    def loop_body(n, carry):
        block(n, False)
        return carry
    lax.fori_loop(1, nblk, loop_body, 0, unroll=unroll)


def _attention_pattern(ua, rel_bias, window, dilation):
    L = SEQ // dilation
    shape = (BATCH, dilation, L, ATT_WIDTH)
    qkv_spec = pl.BlockSpec((None, SEQ, QKV_COLS), lambda b: (b, 0, 0))
    out_spec = pl.BlockSpec((None, SEQ, ATT_WIDTH), lambda b: (b, 0, 0))
    o, lse = pl.pallas_call(
        functools.partial(_attn_body, sub_blocks=L // ATT_BLOCK, unroll=ATT_UNROLL),
        grid=(BATCH,),
        in_specs=[
            pl.BlockSpec(memory_space=pltpu.SMEM),
            pl.BlockSpec((ATT_BLOCK, 2 * ATT_BLOCK), lambda b: (0, 0)),
            qkv_spec,
        ],
        out_specs=[out_spec, out_spec],
        out_shape=[jax.ShapeDtypeStruct((BATCH, SEQ, ATT_WIDTH), BF16),
                   jax.ShapeDtypeStruct((BATCH, SEQ, ATT_WIDTH), F32)],
        scratch_shapes=[pltpu.VMEM((3, ATT_HEADS, ATT_BLOCK, 2 * ATT_BLOCK), F32)],
        compiler_params=_params(("arbitrary",)),
        name=f"attn_d{dilation}",
    )(rel_bias, _bucket_table(window, dilation), ua.reshape(BATCH, SEQ, QKV_COLS))
    return o.reshape(shape), lse.reshape(shape)


def _attention(qkvs, rel_bias):
    return [_attention_pattern(ua, rel_bias, window, dilation)
            for ua, (window, dilation) in zip(qkvs, DILATED_PATTERNS)]


def _short_conv(cin_ref, cb_ref, cc_ref, w_ref, o_ref, tail_ref):
    uu = cc_ref[...] * cin_ref[...]
    t = lax.broadcasted_iota(jnp.int32, uu.shape, 0)
    y = uu * w_ref[CONV_WIDTH - 1:CONV_WIDTH, :]
    for shift in range(1, CONV_WIDTH):
        prev = pltpu.roll(uu, shift, axis=0)
        for r in range(shift):
            prev = jnp.where(t == r, tail_ref[8 - shift + r:8 - shift + r + 1, :], prev)
        y = y + prev * w_ref[CONV_WIDTH - 1 - shift:CONV_WIDTH - shift, :]
    o_ref[...] = (cb_ref[...] * y).astype(o_ref.dtype)
    tail_ref[...] = uu[uu.shape[0] - 8:, :]


def _gla_body(q_ref, k_ref, v_ref, gr_ref, glr_ref, wg_ref, bg_ref, gn_ref, cin_ref, cb_ref, cc_ref, cw_ref,
              o_ref, cv_ref, s_ref, tail_ref):
    @pl.when(pl.program_id(1) == 0)
    def _():
        s_ref[...] = jnp.zeros_like(s_ref)
        tail_ref[...] = jnp.zeros_like(tail_ref)

    _short_conv(cin_ref, cb_ref, cc_ref, cw_ref, cv_ref, tail_ref)

    C = GLA_CHUNK
    row = lax.broadcasted_iota(jnp.int32, (C, C), 0)
    col = lax.broadcasted_iota(jnp.int32, (C, C), 1)
    tril = row >= col
    tril_bf = jnp.where(tril, 1.0, 0.0).astype(BF16)
    ones_bf = jnp.ones((C, GLA_DV), BF16)

    xg = _dot3(glr_ref[...], wg_ref[...]) + bg_ref[...]
    la_all = (jnp.minimum(xg, 0.0) - jnp.log(1.0 + jnp.exp(-jnp.abs(xg)))) * (1.0 / 16.0)

    for c in range(TC_GLA // C):
        rows = slice(c * C, (c + 1) * C)
        la_hi, la_lo = _split_bf16(la_all[rows])
        cum = _dot(tril_bf, la_hi) + _dot(tril_bf, la_lo)
        last = cum[C - 1:C, :]
        q = q_ref[rows, :]
        k = k_ref[rows, :]
        qt = (q * jnp.exp(cum) * (GLA_DK ** -0.5)).astype(BF16)
        kt = (k * jnp.exp(-cum)).astype(BF16)
        kl = (k * jnp.exp(last - cum)).astype(BF16)
        for h in range(GLA_HEADS):
            sl = slice(h * GLA_DK, (h + 1) * GLA_DK)
            vs = slice(h * GLA_DV, (h + 1) * GLA_DV)
            vh = v_ref[rows, vs].astype(BF16)
            state = s_ref[h]
            st_hi, st_lo = _split_bf16(state)
            sc = jnp.where(tril, _dot_nt(qt[:, sl], kt[:, sl]), 0.0).astype(BF16)
            o = _dot(qt[:, sl], st_hi) + _dot(qt[:, sl], st_lo) + _dot(sc, vh)
            ltot = _dot_tn(la_hi[:, sl], ones_bf) + _dot_tn(la_lo[:, sl], ones_bf)
            s_ref[h] = jnp.exp(ltot) * state + _dot_tn(kl[:, sl], vh)
            g = gr_ref[rows, vs]
            o_ref[rows, vs] = (_rms(o, gn_ref[...]) * (g * _sigmoid(g))).astype(o_ref.dtype)


def _gla_conv(u, wg, bg, gn, conv_w):
    nj = SEQ // TC_GLA
    row = lambda b, j: b * nj + j
    tile = lambda width, col: pl.BlockSpec((TC_GLA, width), lambda b, j: (row(b, j), col // width))
    full = lambda a: pl.BlockSpec(a.shape, lambda b, j: (0, 0))
    return pl.pallas_call(
        _gla_body,
        grid=(BATCH, nj),
        in_specs=[
            tile(GLA_QK, COL_GQ), tile(GLA_QK, COL_GK), tile(GLA_V, COL_GV), tile(GLA_V, COL_GR),
            tile(LANES, COL_GLR), full(wg), full(bg), full(gn),
            tile(CONV_CH, COL_CIN), tile(CONV_CH, COL_CB), tile(CONV_CH, COL_CC), full(conv_w),
        ],
        out_specs=[pl.BlockSpec((TC_GLA, GLA_V), lambda b, j: (row(b, j), 0)),
                   pl.BlockSpec((TC_GLA, CONV_CH), lambda b, j: (row(b, j), 0))],
        out_shape=[jax.ShapeDtypeStruct((TOKENS, GLA_V), BF16), jax.ShapeDtypeStruct((TOKENS, CONV_CH), BF16)],
        scratch_shapes=[pltpu.VMEM((GLA_HEADS, GLA_DK, GLA_DV), F32), pltpu.VMEM((8, CONV_CH), F32)],
        compiler_params=_params(("arbitrary", "arbitrary")),
        name="gla_conv",
    )(u, u, u, u, u, wg, bg, gn, u, u, u, conv_w)


def _outproj_body(*refs, route):
    (o1, o4, o16, l1, l4, l16, go_ref, cv_ref, x_ref, w_ref, g_ref) = refs[:11]
    perm_ref = refs[-1]
    if route:
        wr_ref, xo_ref, ho_ref, route_ref, counts_ref, carry_ref = refs[11:-1]
    else:
        xo_ref, ho_ref = refs[11:-1]

    def token_order(ref, slot):
        dilation, rows, _ = ref.shape
        if dilation == 1:
            return ref[0].astype(F32)
        tiles = range(ATT_WIDTH // LANES)
        for r in range(dilation):
            val = ref[r].astype(F32)
            for t in tiles:
                perm_ref[slot, t, pl.ds(r, rows, stride=dilation), :] = val[:, t * LANES:(t + 1) * LANES]
        return jnp.concatenate([perm_ref[slot, t] for t in tiles], axis=-1)

    la, lb, lc = token_order(l1, 0), token_order(l4, 0), token_order(l16, 1)
    oa, ob, oc = token_order(o1, 0), token_order(o4, 2), token_order(o16, 3)
    m = jnp.maximum(jnp.maximum(la, lb), lc)
    ea, eb, ec = jnp.exp(la - m), jnp.exp(lb - m), jnp.exp(lc - m)
    att = (ea * oa + eb * ob + ec * oc) / (ea + eb + ec)
    y = (x_ref[...]
         + _dot(att.astype(BF16), w_ref[0:ATT_WIDTH, :])
         + _dot(go_ref[...], w_ref[ATT_WIDTH:ATT_WIDTH + GLA_V, :])
         + _dot(cv_ref[...], w_ref[ATT_WIDTH + GLA_V:MIX_WIDTH, :]))
    xo_ref[...] = y
    hf = _rms(y, g_ref[...])
    ho_ref[...] = hf.astype(ho_ref.dtype)
    if route:
        @pl.when(pl.program_id(0) == 0)
        def _():
            carry_ref[...] = jnp.zeros_like(carry_ref)

        tm = hf.shape[0]
        ne = N_EXPERTS
        hf_hi, hf_lo = _split_bf16(hf)
        part = _dot_nt(wr_ref[0], hf_hi) + _dot_nt(wr_ref[1], hf_lo)
        logits = part[0:ne] + part[ne:2 * ne]
        eidx = lax.broadcasted_iota(jnp.int32, logits.shape, 0).astype(F32)
        v1 = jnp.max(logits, axis=0, keepdims=True)
        i1 = jnp.min(jnp.where(logits == v1, eidx, float(ne)), axis=0, keepdims=True)
        lg2 = jnp.where(eidx == i1, -jnp.inf, logits)
        v2 = jnp.max(lg2, axis=0, keepdims=True)
        i2 = jnp.min(jnp.where(lg2 == v2, eidx, float(ne)), axis=0, keepdims=True)
        e2 = jnp.exp(v2 - v1)
        w1 = 1.0 / (1.0 + e2)
        w2 = e2 * w1
        sel1 = eidx == i1
        sel2 = eidx == i2
        onehot = jnp.where(sel1, 1.0, jnp.where(sel2, 1.0, 0.0))
        tri = (lax.broadcasted_iota(jnp.int32, (tm, tm), 0) <= lax.broadcasted_iota(jnp.int32, (tm, tm), 1))
        onehot16 = jnp.concatenate([onehot, jnp.zeros_like(onehot)], axis=0).astype(BF16)
        csum = _dot(onehot16, jnp.where(tri, 1.0, 0.0).astype(BF16))[0:ne]
        carry = carry_ref[:, 0:1]
        rank = csum - onehot + carry
        r1 = jnp.sum(jnp.where(sel1, rank, 0.0), axis=0, keepdims=True)
        r2 = jnp.sum(jnp.where(sel2, rank, 0.0), axis=0, keepdims=True)
        total = jnp.broadcast_to(carry + csum[:, tm - 1:tm], carry_ref.shape)
        carry_ref[...] = total
        counts_ref[...] = total
        rows = {ROUTE_I1: i1, ROUTE_I2: i2, ROUTE_W1: w1, ROUTE_W2: w2, ROUTE_R1: r1, ROUTE_R2: r2}
        zero = jnp.zeros_like(i1)
        route_ref[...] = jnp.concatenate([rows.get(r, zero) for r in range(8)], axis=0)


def _outproj(att, go, cv, x, w, g, w_router=None):
    route = w_router is not None
    tm = TM_PROJ
    tile = lambda cols: pl.BlockSpec((tm, cols), lambda i: (i, 0))
    full = lambda a: pl.BlockSpec(a.shape, lambda i: (0, 0))
    (o1, l1), (o4, l4), (o16, l16) = att
    args = [o1, o4, o16, l1, l4, l16, go, cv, x, w, g]
    att_specs = [_subseq_spec(d, ATT_WIDTH) for _, d in DILATED_PATTERNS]
    in_specs = att_specs * 2 + [tile(GLA_V), tile(CONV_CH), tile(D_MODEL), full(w), full(g)]
    out_specs = [tile(D_MODEL), tile(D_MODEL)]
    out_shape = [jax.ShapeDtypeStruct((TOKENS, D_MODEL), F32),
                 jax.ShapeDtypeStruct((TOKENS, D_MODEL), F32 if route else BF16)]
    scratch = []
    if route:
        args.append(w_router)
        in_specs.append(pl.BlockSpec(w_router.shape, lambda i: (0, 0, 0)))
        out_specs += [pl.BlockSpec((8, tm), lambda i: (0, i)), pl.BlockSpec((N_EXPERTS, LANES), lambda i: (0, 0))]
        out_shape += [jax.ShapeDtypeStruct((8, TOKENS), F32), jax.ShapeDtypeStruct((N_EXPERTS, LANES), F32)]
        scratch = [pltpu.VMEM((N_EXPERTS, LANES), F32)]
    scratch.append(pltpu.VMEM((4, ATT_WIDTH // LANES, tm, LANES), F32))
    return pl.pallas_call(
        functools.partial(_outproj_body, route=route),
        grid=(TOKENS // tm,),
        in_specs=in_specs,
        out_specs=out_specs,
        out_shape=out_shape,
        scratch_shapes=scratch,
        compiler_params=_params(("arbitrary",) if route else ("parallel",)),
        name="outproj_route" if route else "outproj",
    )(*args)


def _swiglu_accumulate(h, w1_ref, w3_ref, w2_ref, acc_ref, tf):
    for c0 in range(0, tf, FFN_SUB):
        c1 = min(c0 + FFN_SUB, tf)
        a = _dot(h, w1_ref[:, c0:c1])
        b = _dot(h, w3_ref[:, c0:c1])
        act = a * _sigmoid(a) * b
        acc_ref[...] += _dot(act.astype(BF16), w2_ref[c0:c1, :])


def _ffn_body(x_ref, h_ref, w1_ref, w3_ref, w2_ref, *rest, tf):
    n_cast = (len(rest) - 1) // 2
    o_ref = rest[n_cast]

    @pl.when(pl.program_id(1) == 0)
    def _():
        o_ref[...] = x_ref[...]

    _swiglu_accumulate(h_ref[...], w1_ref, w3_ref, w2_ref, o_ref, tf)
    for src, dst in zip(rest[:n_cast], rest[n_cast + 1:]):
        dst[...] = src[...].astype(dst.dtype)


def _ffn(x, h, w1, w3, w2, *, tf, side_casts=()):
    f = w1.shape[1]
    tm = TM_FFN
    nj = f // tf
    steps = (TOKENS // tm) * nj
    tile = lambda cols: pl.BlockSpec((tm, cols), lambda i, j: (i, 0))
    cast_specs = [pl.BlockSpec((a.shape[0] // steps, a.shape[1]), lambda i, j: (i * nj + j, 0)) for a in side_casts]
    res = pl.pallas_call(
        functools.partial(_ffn_body, tf=tf),
        grid=(TOKENS // tm, nj),
        in_specs=[tile(D_MODEL), tile(D_MODEL),
                  pl.BlockSpec((D_MODEL, tf), lambda i, j: (0, j)),
                  pl.BlockSpec((D_MODEL, tf), lambda i, j: (0, j)),
                  pl.BlockSpec((tf, D_MODEL), lambda i, j: (j, 0))] + cast_specs,
        out_specs=[tile(D_MODEL)] + cast_specs,
        out_shape=[jax.ShapeDtypeStruct((TOKENS, D_MODEL), F32)]
                  + [jax.ShapeDtypeStruct(a.shape, BF16) for a in side_casts],
        compiler_params=_params(("parallel", "arbitrary")),
        name="dense_ffn",
    )(x, h, w1, w3, w2, *side_casts)
    return res[0], res[1:]


ROW_TILE = D_MODEL // LANES


def _to_row_tiled(dst_ref, lead, val):
    rows = val.shape[0]
    for s in range(ROW_TILE):
        dst_ref[(*lead, pl.ds(s, rows, stride=ROW_TILE), slice(None))] = val[:, s * LANES:(s + 1) * LANES]


def _from_row_tiled(src_ref, lead, rows):
    return jnp.concatenate([src_ref[(*lead, pl.ds(s, rows, stride=ROW_TILE), slice(None))]
                            for s in range(ROW_TILE)], axis=-1)


def _row_tile(idx):
    return pl.ds(pl.multiple_of(idx * ROW_TILE, ROW_TILE), ROW_TILE)


def _dispatch_body(pos_ref, last_tile_ref, h_ref, xs_hbm, stage_ref, zero_ref, sem, zero_sem):
    i = pl.program_id(0)
    n = pl.num_programs(0)
    tm = DISPATCH_CHUNK
    slot = i % 2

    def drain(s):
        for _ in range(2):
            pltpu.make_async_copy(stage_ref.at[s], xs_hbm.at[pl.ds(0, tm * ROW_TILE)], sem.at[s]).wait()

    @pl.when(i == 0)
    def _():
        zero_ref[...] = jnp.zeros_like(zero_ref)

        def zero_copy(e):
            start = pl.multiple_of(last_tile_ref[e] * (TM_MOE * ROW_TILE), TM_MOE * ROW_TILE)
            return pltpu.make_async_copy(zero_ref, xs_hbm.at[pl.ds(start, TM_MOE * ROW_TILE)], zero_sem)

        for e in range(2 * N_EXPERTS):
            @pl.when(last_tile_ref[e] >= 0)
            def _():
                zero_copy(e).start()
        for e in range(2 * N_EXPERTS):
            @pl.when(last_tile_ref[e] >= 0)
            def _():
                zero_copy(e).wait()

    @pl.when(i >= 2)
    def _():
        drain(slot)

    _to_row_tiled(stage_ref, (slot,), h_ref[...])

    def body(t, carry):
        for k in range(2):
            dst = pos_ref[2 * (i * tm + t) + k]
            pltpu.make_async_copy(stage_ref.at[slot, _row_tile(t)], xs_hbm.at[_row_tile(dst)],
                                  sem.at[slot]).start(priority=k)
        return carry
    lax.fori_loop(0, tm, body, 0, unroll=8)

    @pl.when(i == n - 1)
    def _():
        drain(1 - slot)
        drain(slot)


def _dispatch(pos, last_tile, h):
    tm = DISPATCH_CHUNK
    return pl.pallas_call(
        _dispatch_body,
        grid=(TOKENS // tm,),
        in_specs=[pl.BlockSpec(memory_space=pltpu.SMEM),
                  pl.BlockSpec(memory_space=pltpu.SMEM),
                  pl.BlockSpec((tm, D_MODEL), lambda i: (i, 0))],
        out_specs=pl.BlockSpec(memory_space=pl.ANY),
        out_shape=jax.ShapeDtypeStruct((N_SORTED * ROW_TILE, LANES), F32),
        scratch_shapes=[pltpu.VMEM((2, tm * ROW_TILE, LANES), F32),
                        pltpu.VMEM((TM_MOE * ROW_TILE, LANES), F32),
                        pltpu.SemaphoreType.DMA((2,)), pltpu.SemaphoreType.DMA(())],
        compiler_params=_params(("arbitrary",)),
        name="moe_dispatch",
    )(pos, last_tile, h)


def _gffn_body(te_ref, nu_ref, xs_ref, w1_ref, w3_ref, w2_ref, o_ref, hb_ref, acc_ref):
    del te_ref
    i = pl.program_id(0)
    j = pl.program_id(1)

    used = i < nu_ref[0]

    @pl.when(j == 0)
    def _():
        acc_ref[...] = jnp.zeros_like(acc_ref)

    @pl.when(used & (j == 0))
    def _():
        hb_ref[...] = _from_row_tiled(xs_ref, (), TM_MOE).astype(BF16)

    @pl.when(used)
    def _():
        _swiglu_accumulate(hb_ref[...], w1_ref, w3_ref, w2_ref, acc_ref, TF_MOE)

    @pl.when(j == pl.num_programs(1) - 1)
    def _():
        _to_row_tiled(o_ref, (), acc_ref[...])


def _grouped_ffn(tile_expert, n_used, xs, w1, w3, w2):
    nj = FFN_EXPERT // TF_MOE
    col = lambda i, j, nu: jnp.where(i < nu[0], j, nj - 1)
    grid_spec = pltpu.PrefetchScalarGridSpec(
        num_scalar_prefetch=2,
        grid=(N_TILES_MOE, nj),
        in_specs=[
            pl.BlockSpec((TM_MOE * ROW_TILE, LANES), lambda i, j, te, nu: (jnp.minimum(i, nu[0] - 1), 0)),
            pl.BlockSpec((None, D_MODEL, TF_MOE), lambda i, j, te, nu: (te[i], 0, col(i, j, nu))),
            pl.BlockSpec((None, D_MODEL, TF_MOE), lambda i, j, te, nu: (te[i], 0, col(i, j, nu))),
            pl.BlockSpec((None, TF_MOE, D_MODEL), lambda i, j, te, nu: (te[i], col(i, j, nu), 0)),
        ],
        out_specs=pl.BlockSpec((TM_MOE * ROW_TILE, LANES), lambda i, j, te, nu: (i, 0)),
        scratch_shapes=[pltpu.VMEM((TM_MOE, D_MODEL), BF16), pltpu.VMEM((TM_MOE, D_MODEL), F32)],
    )
    return pl.pallas_call(
        _gffn_body,
        grid_spec=grid_spec,
        out_shape=jax.ShapeDtypeStruct((N_SORTED * ROW_TILE, LANES), F32),
        compiler_params=_params(("arbitrary", "arbitrary")),
        name="moe_ffn",
    )(tile_expert, n_used, xs, w1, w3, w2)


def _combine_body(pos_ref, x_ref, gate_ref, g_ref, ys_hbm, o_ref, buf_ref, sem):
    i = pl.program_id(0)
    n = pl.num_programs(0)
    tm = TM_COMBINE

    def issue(tile, slot):
        def body(t, carry):
            for k in range(2):
                src = pos_ref[2 * (tile * tm + t) + k]
                pltpu.make_async_copy(ys_hbm.at[_row_tile(src)], buf_ref.at[slot, k, _row_tile(t)],
                                      sem.at[slot]).start(priority=k)
            return carry
        lax.fori_loop(0, tm, body, 0, unroll=8)

    @pl.when(i == 0)
    def _():
        issue(0, 0)

    @pl.when(i + 1 < n)
    def _():
        issue(i + 1, (i + 1) % 2)

    slot = i % 2
    for k in range(2):
        pltpu.make_async_copy(ys_hbm.at[pl.ds(0, tm * ROW_TILE)], buf_ref.at[slot, k], sem.at[slot]).wait()
    w1 = gate_ref[:, 0:1]
    w2 = gate_ref[:, 1:2]
    y = x_ref[...] + w1 * _from_row_tiled(buf_ref, (slot, 0), tm) + w2 * _from_row_tiled(buf_ref, (slot, 1), tm)
    o_ref[...] = _rms(y, g_ref[...])


def _combine(pos, x, gates, g, ys):
    tm = TM_COMBINE
    return pl.pallas_call(
        _combine_body,
        grid=(TOKENS // tm,),
        in_specs=[pl.BlockSpec(memory_space=pltpu.SMEM),
                  pl.BlockSpec((tm, D_MODEL), lambda i: (i, 0)),
                  pl.BlockSpec((tm, 2), lambda i: (i, 0)),
                  pl.BlockSpec((1, D_MODEL), lambda i: (0, 0)),
                  pl.BlockSpec(memory_space=pl.ANY)],
        out_specs=pl.BlockSpec((tm, D_MODEL), lambda i: (i, 0)),
        out_shape=jax.ShapeDtypeStruct((TOKENS, D_MODEL), F32),
        scratch_shapes=[pltpu.VMEM((2, 2, tm * ROW_TILE, LANES), F32), pltpu.SemaphoreType.DMA((2,))],
        compiler_params=_params(("arbitrary",)),
        name="moe_combine",
    )(pos, x, gates, g, ys)


def _routing_tables(route, counts):
    cnt = counts[:, 0].astype(jnp.int32)
    tiles = (cnt + TM_MOE - 1) // TM_MOE
    tile_end = jnp.cumsum(tiles)
    tile_start = tile_end - tiles
    n_used = tile_end[-1]
    expert = route[ROUTE_I1:ROUTE_I2 + 1].astype(jnp.int32).T
    rank = route[ROUTE_R1:ROUTE_R2 + 1].astype(jnp.int32).T
    pos = (tile_start * TM_MOE)[expert] + rank
    tile_id = jnp.minimum(jnp.arange(N_TILES_MOE, dtype=jnp.int32), n_used - 1)
    tile_expert = jnp.sum(tile_id[:, None] >= tile_end[None, :], axis=1).astype(jnp.int32)
    last_tile = jnp.where(tiles > 0, tile_end - 1, -1)
    spare = n_used + jnp.arange(N_EXPERTS)
    zero_tiles = jnp.concatenate([last_tile, jnp.where(spare < N_TILES_MOE, spare, -1)]).astype(jnp.int32)
    return pos.reshape(2 * TOKENS), tile_expert, n_used.reshape(1), zero_tiles


def _moe(x, h, route, counts, w1, w3, w2, g_final):
    pos, tile_expert, n_used, last_tile = _routing_tables(route, counts)
    xs = _dispatch(pos, last_tile, h)
    ys = _grouped_ffn(tile_expert, n_used, xs, w1, w3, w2)
    return _combine(pos, x, route[ROUTE_W1:ROUTE_W2 + 1].T, g_final, ys)


def _prep_w_in(w):
    aq, ak, av, gq, gk, gv, gr, glr, c_in, c_b, c_c = jnp.split(w, np.cumsum(SPLIT_SIZES)[:-1].tolist(), axis=1)
    pad = jnp.zeros((D_MODEL, LANES - GLA_RANK), w.dtype)
    return jnp.concatenate([aq, ak, av, gv, gr, gq, gk, c_in, c_b, c_c, glr, pad], axis=1).astype(BF16)


def _prep_router(w):
    wt = w.T
    hi = wt.astype(BF16)
    lo = (wt - hi.astype(F32)).astype(BF16)
    return jnp.stack([jnp.concatenate([hi, lo]), jnp.concatenate([hi, jnp.zeros_like(hi)])])


def kernel(x, w_mix_in, w_mix_out, g_mix, rel_bias, gla_w_gate, gla_b_gate, gla_g_norm, conv_w,
           g_ffn, ffn_w1, ffn_w3, ffn_w2, moe_router, moe_w1, moe_w3, moe_w2, g_final):
    assert DEPTH == 2
    x = x.reshape(TOKENS, D_MODEL)
    for layer in range(DEPTH):
        u, *qkvs = _inproj(x, g_mix[layer].reshape(1, D_MODEL), _prep_w_in(w_mix_in[layer]))
        att = _attention(qkvs, rel_bias)
        wg = jnp.pad(gla_w_gate[layer], ((0, LANES - GLA_RANK), (0, 0)))
        go, cv = _gla_conv(u, wg, gla_b_gate[layer].reshape(1, GLA_QK), gla_g_norm[layer].reshape(1, GLA_DV),
                           jnp.pad(conv_w[layer], ((0, 8 - CONV_WIDTH), (0, 0))))
        w_out = w_mix_out[layer].astype(BF16)
        g2 = g_ffn[layer].reshape(1, D_MODEL)
        i = layer // 2
        if layer % 2 == 0:
            x, h = _outproj(att, go, cv, x, w_out, g2)
            experts = (moe_w1[i], moe_w3[i], moe_w2[i])
            x, experts_bf16 = _ffn(x, h, ffn_w1[i].astype(BF16), ffn_w3[i].astype(BF16), ffn_w2[i].astype(BF16),
                                   tf=TF_DENSE, side_casts=[w.reshape(-1, w.shape[-1]) for w in experts])
            experts_bf16 = [c.reshape(w.shape) for c, w in zip(experts_bf16, experts)]
        else:
            x, h, route, counts = _outproj(att, go, cv, x, w_out, g2, _prep_router(moe_router[i]))
            x = _moe(x, h, route, counts, *experts_bf16, g_final.reshape(1, D_MODEL))
    return x.reshape(BATCH, SEQ, D_MODEL)
```

```python
import functools
import math

import jax
import jax.numpy as jnp
import numpy as np
from jax import lax
from jax.experimental import pallas as pl
from jax.experimental.pallas import tpu as pltpu

F32 = jnp.float32
BF16 = jnp.bfloat16

D_MODEL = 1024
BATCH = 8
SEQ = 2048
TOKENS = BATCH * SEQ
DEPTH = 2
EPS = 1e-6

HEAD_DIM = 64
ATT_HEADS = 4
ATT_WIDTH = ATT_HEADS * HEAD_DIM
DILATED_PATTERNS = ((128, 1), (512, 4), (2048, 16))
ATT_BLOCK = 128
REL_BUCKETS = 32
REL_MAX_DISTANCE = 2048

GLA_HEADS = 4
GLA_DK = 64
GLA_DV = 128
GLA_RANK = 16
GLA_CHUNK = 64
GLA_QK = GLA_HEADS * GLA_DK
GLA_V = GLA_HEADS * GLA_DV

CONV_CH = 256
CONV_WIDTH = 3
MIX_WIDTH = ATT_WIDTH + GLA_V + CONV_CH

SPLIT_SIZES = (ATT_WIDTH, ATT_WIDTH, ATT_WIDTH, GLA_QK, GLA_QK, GLA_V, GLA_V, GLA_RANK,
               CONV_CH, CONV_CH, CONV_CH)

FFN_DENSE = 2816
N_EXPERTS = 8
FFN_EXPERT = 3584

LANES = 128
MXU_WIDTH = 256
VMEM_LIMIT = 56 * 1024 * 1024

QKV_COLS = 3 * ATT_WIDTH
COL_GV, COL_GR, COL_GQ, COL_GK = 0, 512, 1024, 1280
COL_CIN, COL_CB, COL_CC, COL_GLR = 1536, 1792, 2048, 2304
U_COLS = COL_GLR + LANES

NEG_BIG = -1e30

ATT_UNROLL = 15
TM_PROJ = 512
TC_GLA = 512
TM_FFN = 512
TF_DENSE = 1408
FFN_SUB = 512
TM_MOE = 512
TF_MOE = 1792
N_TILES_MOE = 2 * TOKENS // TM_MOE + N_EXPERTS
N_SORTED = N_TILES_MOE * TM_MOE
DISPATCH_CHUNK = 256
TM_COMBINE = 256

ROUTE_I1, ROUTE_I2, ROUTE_W1, ROUTE_W2, ROUTE_R1, ROUTE_R2 = range(6)


def _params(sem):
    return pltpu.CompilerParams(dimension_semantics=sem, vmem_limit_bytes=VMEM_LIMIT)


def _split_bf16(a):
    hi = a.astype(BF16)
    lo = (a - hi.astype(F32)).astype(BF16)
    return hi, lo


def _dot(a, b):
    return jnp.dot(a, b, preferred_element_type=F32)


def _dot3(a, b):
    a_hi, a_lo = _split_bf16(a)
    b_hi, b_lo = _split_bf16(b)
    return _dot(a_hi, b_hi) + _dot(a_lo, b_hi) + _dot(a_hi, b_lo)


def _dot_nt(a, b):
    return lax.dot_general(a, b, (((1,), (1,)), ((), ())), preferred_element_type=F32)


def _dot_tn(a, b):
    return lax.dot_general(a, b, (((0,), (0,)), ((), ())), preferred_element_type=F32)


def _rms(x, g):
    ms = jnp.mean(x * x, axis=-1, keepdims=True)
    return x * lax.rsqrt(ms + EPS) * g


def _sigmoid(x):
    return 1.0 / (1.0 + jnp.exp(-x))


def _inproj_body(x_ref, g_ref, w_ref, o_ref, *rest):
    qkv_refs, qkv_f32 = rest[:-1], rest[-1]
    h = _rms(x_ref[...], g_ref[...]).astype(BF16)
    for c0 in range(0, QKV_COLS, MXU_WIDTH):
        res = _dot(h, w_ref[:, c0:c0 + MXU_WIDTH])
        for t in range(MXU_WIDTH // LANES):
            qkv_f32[c0 // LANES + t] = res[:, t * LANES:(t + 1) * LANES]
    for (_, dilation), ref in zip(DILATED_PATTERNS, qkv_refs):
        for r in range(dilation):
            rows = pl.ds(r, TM_PROJ // dilation, stride=dilation)
            ref[r] = jnp.concatenate([qkv_f32[t, rows, :] for t in range(QKV_COLS // LANES)],
                                     axis=-1).astype(ref.dtype)
    for c0 in range(0, U_COLS, MXU_WIDTH):
        c1 = min(c0 + MXU_WIDTH, U_COLS)
        o_ref[:, c0:c1] = _dot(h, w_ref[:, QKV_COLS + c0:QKV_COLS + c1])


def _subseq_spec(dilation, cols):
    tiles = SEQ // TM_PROJ
    return pl.BlockSpec((None, dilation, TM_PROJ // dilation, cols), lambda i: (i // tiles, 0, i % tiles, 0))


def _inproj(x, g, w):
    qkv_shapes = [jax.ShapeDtypeStruct((BATCH, d, SEQ // d, QKV_COLS), BF16) for _, d in DILATED_PATTERNS]
    return pl.pallas_call(
        _inproj_body,
        grid=(TOKENS // TM_PROJ,),
        in_specs=[
            pl.BlockSpec((TM_PROJ, D_MODEL), lambda i: (i, 0)),
            pl.BlockSpec((1, D_MODEL), lambda i: (0, 0)),
            pl.BlockSpec((D_MODEL, QKV_COLS + U_COLS), lambda i: (0, 0)),
        ],
        out_specs=[pl.BlockSpec((TM_PROJ, U_COLS), lambda i: (i, 0))]
                  + [_subseq_spec(d, QKV_COLS) for _, d in DILATED_PATTERNS],
        out_shape=[jax.ShapeDtypeStruct((TOKENS, U_COLS), F32)] + qkv_shapes,
        scratch_shapes=[pltpu.VMEM((QKV_COLS // LANES, TM_PROJ, LANES), F32)],
        compiler_params=_params(("parallel",)),
        name="inproj",
    )(x, g, w)


def _rel_bucket(dist):
    max_exact = REL_BUCKETS // 2
    d = jnp.maximum(dist, 0)
    log_ratio = jnp.log(jnp.maximum(d, 1).astype(F32) / max_exact) / math.log(REL_MAX_DISTANCE / max_exact)
    large = jnp.minimum(max_exact + (log_ratio * (REL_BUCKETS - max_exact)).astype(jnp.int32), REL_BUCKETS - 1)
    return jnp.where(d < max_exact, d, large)


def _bucket_table(window, dilation):
    span = window // dilation
    qi = jnp.arange(ATT_BLOCK)[:, None]
    kj = jnp.arange(2 * ATT_BLOCK)[None, :]
    sub_dist = qi - kj + ATT_BLOCK
    band = (sub_dist >= 0) & (sub_dist <= span)
    return jnp.where(band, _rel_bucket(sub_dist * dilation), -1).astype(jnp.int32)


def _attn_body(rb_ref, bidx_ref, qkv_ref, o_ref, lse_ref, bias_ref, *, sub_blocks, unroll):
    nblk = SEQ // ATT_BLOCK

    @pl.when(pl.program_id(0) == 0)
    def _():
        bidx = bidx_ref[...]
        in_prev = lax.broadcasted_iota(jnp.int32, bidx.shape, 1) < ATT_BLOCK
        for h in range(ATT_HEADS):
            acc = jnp.full(bidx.shape, NEG_BIG, F32)
            for b in range(REL_BUCKETS):
                acc = jnp.where(bidx == b, rb_ref[b, h], acc)
            bias_ref[0, h] = acc
            bias_ref[1, h] = jnp.where(in_prev, NEG_BIG, acc)
            bias_ref[2, h] = jnp.concatenate([acc[:, ATT_BLOCK:], jnp.full_like(acc[:, ATT_BLOCK:], NEG_BIG)], axis=1)

    def block(n, first):
        if first:
            rows, krows, variant = slice(0, ATT_BLOCK), slice(0, 2 * ATT_BLOCK), 2
        else:
            r0 = pl.multiple_of(n * ATT_BLOCK, ATT_BLOCK)
            rows = pl.ds(r0, ATT_BLOCK)
            krows = pl.ds(r0 - ATT_BLOCK, 2 * ATT_BLOCK)
            if sub_blocks == nblk:
                variant = 0
            elif sub_blocks == 1:
                variant = 1
            else:
                variant = jnp.where(n % sub_blocks == 0, 1, 0)
        q = qkv_ref[rows, 0:ATT_WIDTH]
        kk = qkv_ref[krows, ATT_WIDTH:2 * ATT_WIDTH]
        vv = qkv_ref[krows, 2 * ATT_WIDTH:3 * ATT_WIDTH]
        q = q * jnp.asarray(HEAD_DIM ** -0.5, BF16)
        head_of_lane = lax.broadcasted_iota(jnp.int32, (ATT_BLOCK, ATT_WIDTH), 1) // HEAD_DIM
        ones = jnp.ones((kk.shape[0], LANES), BF16)
        num = den = mx = None
        for h in range(ATT_HEADS):
            mine = head_of_lane == h
            bias = bias_ref[variant, h]
            s = _dot_nt(jnp.where(mine, q, jnp.zeros_like(q)), kk) + bias
            m = jnp.max(s, axis=-1, keepdims=True)
            p = jnp.exp(s - m).astype(BF16)
            num_h = _dot(p, vv)
            den_h = jnp.tile(_dot(p, ones), (1, ATT_WIDTH // LANES))
            m_h = jnp.broadcast_to(m, (ATT_BLOCK, ATT_WIDTH))
            num = num_h if h == 0 else jnp.where(mine, num_h, num)
            den = den_h if h == 0 else jnp.where(mine, den_h, den)
            mx = m_h if h == 0 else jnp.where(mine, m_h, mx)
        o_ref[rows, :] = (num / den).astype(o_ref.dtype)
        lse_ref[rows, :] = mx + jnp.log(den)

    block(0, True)

    def loop_body(n, carry):
        block(n, False)
        return carry
    lax.fori_loop(1, nblk, loop_body, 0, unroll=unroll)


def _attention_pattern(ua, rel_bias, window, dilation):
    L = SEQ // dilation
    shape = (BATCH, dilation, L, ATT_WIDTH)
    qkv_spec = pl.BlockSpec((None, SEQ, QKV_COLS), lambda b: (b, 0, 0))
    out_spec = pl.BlockSpec((None, SEQ, ATT_WIDTH), lambda b: (b, 0, 0))
    o, lse = pl.pallas_call(
        functools.partial(_attn_body, sub_blocks=L // ATT_BLOCK, unroll=ATT_UNROLL),
        grid=(BATCH,),
        in_specs=[
            pl.BlockSpec(memory_space=pltpu.SMEM),
            pl.BlockSpec((ATT_BLOCK, 2 * ATT_BLOCK), lambda b: (0, 0)),
            qkv_spec,
        ],
        out_specs=[out_spec, out_spec],
        out_shape=[jax.ShapeDtypeStruct((BATCH, SEQ, ATT_WIDTH), BF16),
                   jax.ShapeDtypeStruct((BATCH, SEQ, ATT_WIDTH), F32)],
        scratch_shapes=[pltpu.VMEM((3, ATT_HEADS, ATT_BLOCK, 2 * ATT_BLOCK), F32)],
        compiler_params=_params(("arbitrary",)),
        name=f"attn_d{dilation}",
    )(rel_bias, _bucket_table(window, dilation), ua.reshape(BATCH, SEQ, QKV_COLS))
    return o.reshape(shape), lse.reshape(shape)


def _attention(qkvs, rel_bias):
    return [_attention_pattern(ua, rel_bias, window, dilation)
            for ua, (window, dilation) in zip(qkvs, DILATED_PATTERNS)]


def _short_conv(cin_ref, cb_ref, cc_ref, w_ref, o_ref, tail_ref):
    uu = cc_ref[...] * cin_ref[...]
    t = lax.broadcasted_iota(jnp.int32, uu.shape, 0)
    y = uu * w_ref[CONV_WIDTH - 1:CONV_WIDTH, :]
    for shift in range(1, CONV_WIDTH):
        prev = pltpu.roll(uu, shift, axis=0)
        for r in range(shift):
            prev = jnp.where(t == r, tail_ref[8 - shift + r:8 - shift + r + 1, :], prev)
        y = y + prev * w_ref[CONV_WIDTH - 1 - shift:CONV_WIDTH - shift, :]
    o_ref[...] = (cb_ref[...] * y).astype(o_ref.dtype)
    tail_ref[...] = uu[uu.shape[0] - 8:, :]


def _side_cast(srcs, dsts):
    for src, dst in zip(srcs, dsts):
        dst[...] = src[...].astype(dst.dtype)


def _side_cast_specs(arrays, steps, step_of):
    return [pl.BlockSpec((a.shape[0] // steps, a.shape[1]), lambda *ids: (step_of(*ids), 0)) for a in arrays]


def _gla_body(q_ref, k_ref, v_ref, gr_ref, glr_ref, wg_ref, bg_ref, gn_ref, cin_ref, cb_ref, cc_ref, cw_ref,
              *rest):
    n_cast = (len(rest) - 4) // 2
    o_ref, cv_ref = rest[n_cast:n_cast + 2]
    s_ref, tail_ref = rest[-2:]
    _side_cast(rest[:n_cast], rest[n_cast + 2:-2])

    @pl.when(pl.program_id(1) == 0)
    def _():
        s_ref[...] = jnp.zeros_like(s_ref)
        tail_ref[...] = jnp.zeros_like(tail_ref)

    _short_conv(cin_ref, cb_ref, cc_ref, cw_ref, cv_ref, tail_ref)

    C = GLA_CHUNK
    row = lax.broadcasted_iota(jnp.int32, (C, C), 0)
    col = lax.broadcasted_iota(jnp.int32, (C, C), 1)
    tril = row >= col
    tril_bf = jnp.where(tril, 1.0, 0.0).astype(BF16)
    ones_bf = jnp.ones((C, GLA_DV), BF16)

    xg = _dot3(glr_ref[...], wg_ref[...]) + bg_ref[...]
    la_all = (jnp.minimum(xg, 0.0) - jnp.log(1.0 + jnp.exp(-jnp.abs(xg)))) * (1.0 / 16.0)

    for c in range(TC_GLA // C):
        rows = slice(c * C, (c + 1) * C)
        la_hi, la_lo = _split_bf16(la_all[rows])
        cum = _dot(tril_bf, la_hi) + _dot(tril_bf, la_lo)
        last = cum[C - 1:C, :]
        q = q_ref[rows, :]
        k = k_ref[rows, :]
        qt = (q * jnp.exp(cum) * (GLA_DK ** -0.5)).astype(BF16)
        kt = (k * jnp.exp(-cum)).astype(BF16)
        kl = (k * jnp.exp(last - cum)).astype(BF16)
        for h in range(GLA_HEADS):
            sl = slice(h * GLA_DK, (h + 1) * GLA_DK)
            vs = slice(h * GLA_DV, (h + 1) * GLA_DV)
            vh = v_ref[rows, vs].astype(BF16)
            state = s_ref[h]
            st_hi, st_lo = _split_bf16(state)
            sc = jnp.where(tril, _dot_nt(qt[:, sl], kt[:, sl]), 0.0).astype(BF16)
            o = _dot(qt[:, sl], st_hi) + _dot(qt[:, sl], st_lo) + _dot(sc, vh)
            ltot = _dot_tn(la_hi[:, sl], ones_bf) + _dot_tn(la_lo[:, sl], ones_bf)
            s_ref[h] = jnp.exp(ltot) * state + _dot_tn(kl[:, sl], vh)
            g = gr_ref[rows, vs]
            o_ref[rows, vs] = (_rms(o, gn_ref[...]) * (g * _sigmoid(g))).astype(o_ref.dtype)


def _gla_conv(u, wg, bg, gn, conv_w, side_casts=()):
    nj = SEQ // TC_GLA
    row = lambda b, j: b * nj + j
    tile = lambda width, col: pl.BlockSpec((TC_GLA, width), lambda b, j: (row(b, j), col // width))
    full = lambda a: pl.BlockSpec(a.shape, lambda b, j: (0, 0))
    cast_specs = _side_cast_specs(side_casts, BATCH * nj, row)
    go, cv, *casts = pl.pallas_call(
        _gla_body,
        grid=(BATCH, nj),
        in_specs=[
            tile(GLA_QK, COL_GQ), tile(GLA_QK, COL_GK), tile(GLA_V, COL_GV), tile(GLA_V, COL_GR),
            tile(LANES, COL_GLR), full(wg), full(bg), full(gn),
            tile(CONV_CH, COL_CIN), tile(CONV_CH, COL_CB), tile(CONV_CH, COL_CC), full(conv_w),
        ] + cast_specs,
        out_specs=[pl.BlockSpec((TC_GLA, GLA_V), lambda b, j: (row(b, j), 0)),
                   pl.BlockSpec((TC_GLA, CONV_CH), lambda b, j: (row(b, j), 0))] + cast_specs,
        out_shape=[jax.ShapeDtypeStruct((TOKENS, GLA_V), BF16), jax.ShapeDtypeStruct((TOKENS, CONV_CH), BF16)]
                  + [jax.ShapeDtypeStruct(a.shape, BF16) for a in side_casts],
        scratch_shapes=[pltpu.VMEM((GLA_HEADS, GLA_DK, GLA_DV), F32), pltpu.VMEM((8, CONV_CH), F32)],
        compiler_params=_params(("arbitrary", "arbitrary")),
        name="gla_conv",
    )(u, u, u, u, u, wg, bg, gn, u, u, u, conv_w, *side_casts)
    return go, cv, casts


def _outproj_body(*refs, route):
    (o1, o4, o16, l1, l4, l16, go_ref, cv_ref, x_ref, w_ref, g_ref) = refs[:11]
    perm_ref = refs[-1]
    if route:
        wr_ref, xo_ref, ho_ref, route_ref, counts_ref, carry_ref = refs[11:-1]
    else:
        xo_ref, ho_ref = refs[11:-1]

    def token_order(ref, slot):
        dilation, rows, _ = ref.shape
        if dilation == 1:
            return ref[0].astype(F32)
        tiles = range(ATT_WIDTH // LANES)
        for r in range(dilation):
            val = ref[r].astype(F32)
            for t in tiles:
                perm_ref[slot, t, pl.ds(r, rows, stride=dilation), :] = val[:, t * LANES:(t + 1) * LANES]
        return jnp.concatenate([perm_ref[slot, t] for t in tiles], axis=-1)

    la, lb, lc = token_order(l1, 0), token_order(l4, 0), token_order(l16, 1)
    oa, ob, oc = token_order(o1, 0), token_order(o4, 2), token_order(o16, 3)
    m = jnp.maximum(jnp.maximum(la, lb), lc)
    ea, eb, ec = jnp.exp(la - m), jnp.exp(lb - m), jnp.exp(lc - m)
    att = (ea * oa + eb * ob + ec * oc) / (ea + eb + ec)
    y = (x_ref[...]
         + _dot(att.astype(BF16), w_ref[0:ATT_WIDTH, :])
         + _dot(go_ref[...], w_ref[ATT_WIDTH:ATT_WIDTH + GLA_V, :])
         + _dot(cv_ref[...], w_ref[ATT_WIDTH + GLA_V:MIX_WIDTH, :]))
    xo_ref[...] = y
    hf = _rms(y, g_ref[...])
    ho_ref[...] = hf.astype(ho_ref.dtype)
    if route:
        @pl.when(pl.program_id(0) == 0)
        def _():
            carry_ref[...] = jnp.zeros_like(carry_ref)

        tm = hf.shape[0]
        ne = N_EXPERTS
        hf_hi, hf_lo = _split_bf16(hf)
        part = _dot_nt(wr_ref[0], hf_hi) + _dot_nt(wr_ref[1], hf_lo)
        logits = part[0:ne] + part[ne:2 * ne]
        eidx = lax.broadcasted_iota(jnp.int32, logits.shape, 0).astype(F32)
        v1 = jnp.max(logits, axis=0, keepdims=True)
        i1 = jnp.min(jnp.where(logits == v1, eidx, float(ne)), axis=0, keepdims=True)
        lg2 = jnp.where(eidx == i1, -jnp.inf, logits)
        v2 = jnp.max(lg2, axis=0, keepdims=True)
        i2 = jnp.min(jnp.where(lg2 == v2, eidx, float(ne)), axis=0, keepdims=True)
        e2 = jnp.exp(v2 - v1)
        w1 = 1.0 / (1.0 + e2)
        w2 = e2 * w1
        sel1 = eidx == i1
        sel2 = eidx == i2
        onehot = jnp.where(sel1, 1.0, jnp.where(sel2, 1.0, 0.0))
        tri = (lax.broadcasted_iota(jnp.int32, (tm, tm), 0) <= lax.broadcasted_iota(jnp.int32, (tm, tm), 1))
        onehot16 = jnp.concatenate([onehot, jnp.zeros_like(onehot)], axis=0).astype(BF16)
        csum = _dot(onehot16, jnp.where(tri, 1.0, 0.0).astype(BF16))[0:ne]
        carry = carry_ref[:, 0:1]
        rank = csum - onehot + carry
        r1 = jnp.sum(jnp.where(sel1, rank, 0.0), axis=0, keepdims=True)
        r2 = jnp.sum(jnp.where(sel2, rank, 0.0), axis=0, keepdims=True)
        total = jnp.broadcast_to(carry + csum[:, tm - 1:tm], carry_ref.shape)
        carry_ref[...] = total
        counts_ref[...] = total
        rows = {ROUTE_I1: i1, ROUTE_I2: i2, ROUTE_W1: w1, ROUTE_W2: w2, ROUTE_R1: r1, ROUTE_R2: r2}
        zero = jnp.zeros_like(i1)
        route_ref[...] = jnp.concatenate([rows.get(r, zero) for r in range(8)], axis=0)


def _outproj(att, go, cv, x, w, g, w_router=None):
    route = w_router is not None
    tm = TM_PROJ
    tile = lambda cols: pl.BlockSpec((tm, cols), lambda i: (i, 0))
    full = lambda a: pl.BlockSpec(a.shape, lambda i: (0, 0))
    (o1, l1), (o4, l4), (o16, l16) = att
    args = [o1, o4, o16, l1, l4, l16, go, cv, x, w, g]
    att_specs = [_subseq_spec(d, ATT_WIDTH) for _, d in DILATED_PATTERNS]
    in_specs = att_specs * 2 + [tile(GLA_V), tile(CONV_CH), tile(D_MODEL), full(w), full(g)]
    out_specs = [tile(D_MODEL), tile(D_MODEL)]
    out_shape = [jax.ShapeDtypeStruct((TOKENS, D_MODEL), F32),
                 jax.ShapeDtypeStruct((TOKENS, D_MODEL), F32 if route else BF16)]
    scratch = []
    if route:
        args.append(w_router)
        in_specs.append(pl.BlockSpec(w_router.shape, lambda i: (0, 0, 0)))
        out_specs += [pl.BlockSpec((8, tm), lambda i: (0, i)), pl.BlockSpec((N_EXPERTS, LANES), lambda i: (0, 0))]
        out_shape += [jax.ShapeDtypeStruct((8, TOKENS), F32), jax.ShapeDtypeStruct((N_EXPERTS, LANES), F32)]
        scratch = [pltpu.VMEM((N_EXPERTS, LANES), F32)]
    scratch.append(pltpu.VMEM((4, ATT_WIDTH // LANES, tm, LANES), F32))
    return pl.pallas_call(
        functools.partial(_outproj_body, route=route),
        grid=(TOKENS // tm,),
        in_specs=in_specs,
        out_specs=out_specs,
        out_shape=out_shape,
        scratch_shapes=scratch,
        compiler_params=_params(("arbitrary",) if route else ("parallel",)),
        name="outproj_route" if route else "outproj",
    )(*args)


def _swiglu_accumulate(h, w1_ref, w3_ref, w2_ref, acc_ref, tf):
    for c0 in range(0, tf, FFN_SUB):
        c1 = min(c0 + FFN_SUB, tf)
        a = _dot(h, w1_ref[:, c0:c1])
        b = _dot(h, w3_ref[:, c0:c1])
        act = a * _sigmoid(a) * b
        acc_ref[...] += _dot(act.astype(BF16), w2_ref[c0:c1, :])


def _ffn_body(x_ref, h_ref, w1_ref, w3_ref, w2_ref, *rest, tf):
    n_cast = (len(rest) - 1) // 2
    o_ref = rest[n_cast]

    @pl.when(pl.program_id(1) == 0)
    def _():
        o_ref[...] = x_ref[...]

    _swiglu_accumulate(h_ref[...], w1_ref, w3_ref, w2_ref, o_ref, tf)
    _side_cast(rest[:n_cast], rest[n_cast + 1:])


def _ffn(x, h, w1, w3, w2, *, tf, side_casts=()):
    f = w1.shape[1]
    tm = TM_FFN
    nj = f // tf
    tile = lambda cols: pl.BlockSpec((tm, cols), lambda i, j: (i, 0))
    cast_specs = _side_cast_specs(side_casts, (TOKENS // tm) * nj, lambda i, j: i * nj + j)
    res = pl.pallas_call(
        functools.partial(_ffn_body, tf=tf),
        grid=(TOKENS // tm, nj),
        in_specs=[tile(D_MODEL), tile(D_MODEL),
                  pl.BlockSpec((D_MODEL, tf), lambda i, j: (0, j)),
                  pl.BlockSpec((D_MODEL, tf), lambda i, j: (0, j)),
                  pl.BlockSpec((tf, D_MODEL), lambda i, j: (j, 0))] + cast_specs,
        out_specs=[tile(D_MODEL)] + cast_specs,
        out_shape=[jax.ShapeDtypeStruct((TOKENS, D_MODEL), F32)]
                  + [jax.ShapeDtypeStruct(a.shape, BF16) for a in side_casts],
        compiler_params=_params(("parallel", "arbitrary")),
        name="dense_ffn",
    )(x, h, w1, w3, w2, *side_casts)
    return res[0], res[1:]


ROW_TILE = D_MODEL // LANES


def _to_row_tiled(dst_ref, lead, val):
    rows = val.shape[0]
    for s in range(ROW_TILE):
        dst_ref[(*lead, pl.ds(s, rows, stride=ROW_TILE), slice(None))] = val[:, s * LANES:(s + 1) * LANES]


def _from_row_tiled(src_ref, lead, rows):
    return jnp.concatenate([src_ref[(*lead, pl.ds(s, rows, stride=ROW_TILE), slice(None))]
                            for s in range(ROW_TILE)], axis=-1)


def _row_tile(idx):
    return pl.ds(pl.multiple_of(idx * ROW_TILE, ROW_TILE), ROW_TILE)


def _dispatch_body(pos_ref, last_tile_ref, h_ref, xs_hbm, stage_ref, zero_ref, sem, zero_sem):
    i = pl.program_id(0)
    n = pl.num_programs(0)
    tm = DISPATCH_CHUNK
    slot = i % 2

    def drain(s):
        for _ in range(2):
            pltpu.make_async_copy(stage_ref.at[s], xs_hbm.at[pl.ds(0, tm * ROW_TILE)], sem.at[s]).wait()

    @pl.when(i == 0)
    def _():
        zero_ref[...] = jnp.zeros_like(zero_ref)

        def zero_copy(e):
            start = pl.multiple_of(last_tile_ref[e] * (TM_MOE * ROW_TILE), TM_MOE * ROW_TILE)
            return pltpu.make_async_copy(zero_ref, xs_hbm.at[pl.ds(start, TM_MOE * ROW_TILE)], zero_sem)

        for e in range(2 * N_EXPERTS):
            @pl.when(last_tile_ref[e] >= 0)
            def _():
                zero_copy(e).start()
        for e in range(2 * N_EXPERTS):
            @pl.when(last_tile_ref[e] >= 0)
            def _():
                zero_copy(e).wait()

    @pl.when(i >= 2)
    def _():
        drain(slot)

    _to_row_tiled(stage_ref, (slot,), h_ref[...])

    def body(t, carry):
        for k in range(2):
            dst = pos_ref[2 * (i * tm + t) + k]
            pltpu.make_async_copy(stage_ref.at[slot, _row_tile(t)], xs_hbm.at[_row_tile(dst)],
                                  sem.at[slot]).start(priority=k)
        return carry
    lax.fori_loop(0, tm, body, 0, unroll=8)

    @pl.when(i == n - 1)
    def _():
        drain(1 - slot)
        drain(slot)


def _dispatch(pos, last_tile, h):
    tm = DISPATCH_CHUNK
    return pl.pallas_call(
        _dispatch_body,
        grid=(TOKENS // tm,),
        in_specs=[pl.BlockSpec(memory_space=pltpu.SMEM),
                  pl.BlockSpec(memory_space=pltpu.SMEM),
                  pl.BlockSpec((tm, D_MODEL), lambda i: (i, 0))],
        out_specs=pl.BlockSpec(memory_space=pl.ANY),
        out_shape=jax.ShapeDtypeStruct((N_SORTED * ROW_TILE, LANES), F32),
        scratch_shapes=[pltpu.VMEM((2, tm * ROW_TILE, LANES), F32),
                        pltpu.VMEM((TM_MOE * ROW_TILE, LANES), F32),
                        pltpu.SemaphoreType.DMA((2,)), pltpu.SemaphoreType.DMA(())],
        compiler_params=_params(("arbitrary",)),
        name="moe_dispatch",
    )(pos, last_tile, h)


def _gffn_body(te_ref, nu_ref, xs_ref, w1_ref, w3_ref, w2_ref, o_ref, hb_ref, acc_ref):
    del te_ref
    i = pl.program_id(0)
    j = pl.program_id(1)

    used = i < nu_ref[0]

    @pl.when(j == 0)
    def _():
        acc_ref[...] = jnp.zeros_like(acc_ref)

    @pl.when(used & (j == 0))
    def _():
        hb_ref[...] = _from_row_tiled(xs_ref, (), TM_MOE).astype(BF16)

    @pl.when(used)
    def _():
        _swiglu_accumulate(hb_ref[...], w1_ref, w3_ref, w2_ref, acc_ref, TF_MOE)

    @pl.when(j == pl.num_programs(1) - 1)
    def _():
        _to_row_tiled(o_ref, (), acc_ref[...])


def _grouped_ffn(tile_expert, n_used, xs, w1, w3, w2):
    nj = FFN_EXPERT // TF_MOE
    col = lambda i, j, nu: jnp.where(i < nu[0], j, nj - 1)
    grid_spec = pltpu.PrefetchScalarGridSpec(
        num_scalar_prefetch=2,
        grid=(N_TILES_MOE, nj),
        in_specs=[
            pl.BlockSpec((TM_MOE * ROW_TILE, LANES), lambda i, j, te, nu: (jnp.minimum(i, nu[0] - 1), 0)),
            pl.BlockSpec((None, D_MODEL, TF_MOE), lambda i, j, te, nu: (te[i], 0, col(i, j, nu))),
            pl.BlockSpec((None, D_MODEL, TF_MOE), lambda i, j, te, nu: (te[i], 0, col(i, j, nu))),
            pl.BlockSpec((None, TF_MOE, D_MODEL), lambda i, j, te, nu: (te[i], col(i, j, nu), 0)),
        ],
        out_specs=pl.BlockSpec((TM_MOE * ROW_TILE, LANES), lambda i, j, te, nu: (i, 0)),
        scratch_shapes=[pltpu.VMEM((TM_MOE, D_MODEL), BF16), pltpu.VMEM((TM_MOE, D_MODEL), F32)],
    )
    return pl.pallas_call(
        _gffn_body,
        grid_spec=grid_spec,
        out_shape=jax.ShapeDtypeStruct((N_SORTED * ROW_TILE, LANES), F32),
        compiler_params=_params(("arbitrary", "arbitrary")),
        name="moe_ffn",
    )(tile_expert, n_used, xs, w1, w3, w2)


def _combine_body(pos_ref, x_ref, gate_ref, g_ref, ys_hbm, o_ref, buf_ref, sem):
    i = pl.program_id(0)
    n = pl.num_programs(0)
    tm = TM_COMBINE

    def issue(tile, slot):
        def body(t, carry):
            for k in range(2):
                src = pos_ref[2 * (tile * tm + t) + k]
                pltpu.make_async_copy(ys_hbm.at[_row_tile(src)], buf_ref.at[slot, k, _row_tile(t)],
                                      sem.at[slot]).start(priority=k)
            return carry
        lax.fori_loop(0, tm, body, 0, unroll=8)

    @pl.when(i == 0)
    def _():
        issue(0, 0)

    @pl.when(i + 1 < n)
    def _():
        issue(i + 1, (i + 1) % 2)

    slot = i % 2
    for k in range(2):
        pltpu.make_async_copy(ys_hbm.at[pl.ds(0, tm * ROW_TILE)], buf_ref.at[slot, k], sem.at[slot]).wait()
    w1 = gate_ref[:, 0:1]
    w2 = gate_ref[:, 1:2]
    y = x_ref[...] + w1 * _from_row_tiled(buf_ref, (slot, 0), tm) + w2 * _from_row_tiled(buf_ref, (slot, 1), tm)
    o_ref[...] = _rms(y, g_ref[...])


def _combine(pos, x, gates, g, ys):
    tm = TM_COMBINE
    return pl.pallas_call(
        _combine_body,
        grid=(TOKENS // tm,),
        in_specs=[pl.BlockSpec(memory_space=pltpu.SMEM),
                  pl.BlockSpec((tm, D_MODEL), lambda i: (i, 0)),
                  pl.BlockSpec((tm, 2), lambda i: (i, 0)),
                  pl.BlockSpec((1, D_MODEL), lambda i: (0, 0)),
                  pl.BlockSpec(memory_space=pl.ANY)],
        out_specs=pl.BlockSpec((tm, D_MODEL), lambda i: (i, 0)),
        out_shape=jax.ShapeDtypeStruct((TOKENS, D_MODEL), F32),
        scratch_shapes=[pltpu.VMEM((2, 2, tm * ROW_TILE, LANES), F32), pltpu.SemaphoreType.DMA((2,))],
        compiler_params=_params(("arbitrary",)),
        name="moe_combine",
    )(pos, x, gates, g, ys)


def _routing_tables(route, counts):
    cnt = counts[:, 0].astype(jnp.int32)
    tiles = (cnt + TM_MOE - 1) // TM_MOE
    tile_end = jnp.cumsum(tiles)
    tile_start = tile_end - tiles
    n_used = tile_end[-1]
    expert = route[ROUTE_I1:ROUTE_I2 + 1].astype(jnp.int32).T
    rank = route[ROUTE_R1:ROUTE_R2 + 1].astype(jnp.int32).T
    pos = (tile_start * TM_MOE)[expert] + rank
    tile_id = jnp.minimum(jnp.arange(N_TILES_MOE, dtype=jnp.int32), n_used - 1)
    tile_expert = jnp.sum(tile_id[:, None] >= tile_end[None, :], axis=1).astype(jnp.int32)
    last_tile = jnp.where(tiles > 0, tile_end - 1, -1)
    spare = n_used + jnp.arange(N_EXPERTS)
    zero_tiles = jnp.concatenate([last_tile, jnp.where(spare < N_TILES_MOE, spare, -1)]).astype(jnp.int32)
    return pos.reshape(2 * TOKENS), tile_expert, n_used.reshape(1), zero_tiles


def _moe(x, h, route, counts, w1, w3, w2, g_final):
    pos, tile_expert, n_used, last_tile = _routing_tables(route, counts)
    xs = _dispatch(pos, last_tile, h)
    ys = _grouped_ffn(tile_expert, n_used, xs, w1, w3, w2)
    return _combine(pos, x, route[ROUTE_W1:ROUTE_W2 + 1].T, g_final, ys)


def _prep_w_in(w):
    aq, ak, av, gq, gk, gv, gr, glr, c_in, c_b, c_c = jnp.split(w, np.cumsum(SPLIT_SIZES)[:-1].tolist(), axis=1)
    pad = jnp.zeros((D_MODEL, LANES - GLA_RANK), w.dtype)
    return jnp.concatenate([aq, ak, av, gv, gr, gq, gk, c_in, c_b, c_c, glr, pad], axis=1).astype(BF16)


def _prep_router(w):
    wt = w.T
    hi = wt.astype(BF16)
    lo = (wt - hi.astype(F32)).astype(BF16)
    return jnp.stack([jnp.concatenate([hi, lo]), jnp.concatenate([hi, jnp.zeros_like(hi)])])


def kernel(x, w_mix_in, w_mix_out, g_mix, rel_bias, gla_w_gate, gla_b_gate, gla_g_norm, conv_w,
           g_ffn, ffn_w1, ffn_w3, ffn_w2, moe_router, moe_w1, moe_w3, moe_w2, g_final):
    assert DEPTH == 2
    x = x.reshape(TOKENS, D_MODEL)
    flat = lambda w: w.reshape(-1, w.shape[-1])
    gla_casts = {0: moe_w1[0], 1: moe_w3[0]}
    experts = {}
    for layer in range(DEPTH):
        u, *qkvs = _inproj(x, g_mix[layer].reshape(1, D_MODEL), _prep_w_in(w_mix_in[layer]))
        att = _attention(qkvs, rel_bias)
        wg = jnp.pad(gla_w_gate[layer], ((0, LANES - GLA_RANK), (0, 0)))
        go, cv, (cast,) = _gla_conv(u, wg, gla_b_gate[layer].reshape(1, GLA_QK),
                                    gla_g_norm[layer].reshape(1, GLA_DV),
                                    jnp.pad(conv_w[layer], ((0, 8 - CONV_WIDTH), (0, 0))),
                                    side_casts=[flat(gla_casts[layer])])
        experts[layer] = cast.reshape(gla_casts[layer].shape)
        w_out = w_mix_out[layer].astype(BF16)
        g2 = g_ffn[layer].reshape(1, D_MODEL)
        i = layer // 2
        if layer % 2 == 0:
            x, h = _outproj(att, go, cv, x, w_out, g2)
            x, (cast,) = _ffn(x, h, ffn_w1[i].astype(BF16), ffn_w3[i].astype(BF16), ffn_w2[i].astype(BF16),
                              tf=TF_DENSE, side_casts=[flat(moe_w2[i])])
            experts["w2"] = cast.reshape(moe_w2[i].shape)
        else:
            x, h, route, counts = _outproj(att, go, cv, x, w_out, g2, _prep_router(moe_router[i]))
            x = _moe(x, h, route, counts, experts[0], experts[1], experts["w2"], g_final.reshape(1, D_MODEL))
    return x.reshape(BATCH, SEQ, D_MODEL)
```

```python
import functools
import math

import jax
import jax.numpy as jnp
import numpy as np
from jax import lax
from jax.experimental import pallas as pl
from jax.experimental.pallas import tpu as pltpu

F32 = jnp.float32
BF16 = jnp.bfloat16

D_MODEL = 1024
BATCH = 8
SEQ = 2048
TOKENS = BATCH * SEQ
DEPTH = 2
EPS = 1e-6

HEAD_DIM = 64
ATT_HEADS = 4
ATT_WIDTH = ATT_HEADS * HEAD_DIM
DILATED_PATTERNS = ((128, 1), (512, 4), (2048, 16))
ATT_BLOCK = 128
REL_BUCKETS = 32
REL_MAX_DISTANCE = 2048

GLA_HEADS = 4
GLA_DK = 64
GLA_DV = 128
GLA_RANK = 16
GLA_CHUNK = 64
GLA_QK = GLA_HEADS * GLA_DK
GLA_V = GLA_HEADS * GLA_DV

CONV_CH = 256
CONV_WIDTH = 3
MIX_WIDTH = ATT_WIDTH + GLA_V + CONV_CH

SPLIT_SIZES = (ATT_WIDTH, ATT_WIDTH, ATT_WIDTH, GLA_QK, GLA_QK, GLA_V, GLA_V, GLA_RANK,
               CONV_CH, CONV_CH, CONV_CH)

FFN_DENSE = 2816
N_EXPERTS = 8
FFN_EXPERT = 3584

LANES = 128
MXU_WIDTH = 256
VMEM_LIMIT = 56 * 1024 * 1024

QKV_COLS = 3 * ATT_WIDTH
COL_GV, COL_GR, COL_GQ, COL_GK = 0, 512, 1024, 1280
COL_CIN, COL_CB, COL_CC, COL_GLR = 1536, 1792, 2048, 2304
U_COLS = COL_GLR + LANES

NEG_BIG = -1e30

ATT_UNROLL = 15
TM_PROJ = 512
TC_GLA = 512
TM_FFN = 512
TF_DENSE = 1408
FFN_SUB = 512
TM_MOE = 512
TF_MOE = 1792
N_TILES_MOE = 2 * TOKENS // TM_MOE + N_EXPERTS
N_SORTED = N_TILES_MOE * TM_MOE
DISPATCH_CHUNK = 256
TM_COMBINE = 256

ROUTE_I1, ROUTE_I2, ROUTE_W1, ROUTE_W2, ROUTE_R1, ROUTE_R2 = range(6)


def _params(sem):
    return pltpu.CompilerParams(dimension_semantics=sem, vmem_limit_bytes=VMEM_LIMIT)


def _split_bf16(a):
    hi = a.astype(BF16)
    lo = (a - hi.astype(F32)).astype(BF16)
    return hi, lo


def _dot(a, b):
    return jnp.dot(a, b, preferred_element_type=F32)


def _dot3(a, b):
    a_hi, a_lo = _split_bf16(a)
    b_hi, b_lo = _split_bf16(b)
    return _dot(a_hi, b_hi) + _dot(a_lo, b_hi) + _dot(a_hi, b_lo)


def _dot_nt(a, b):
    return lax.dot_general(a, b, (((1,), (1,)), ((), ())), preferred_element_type=F32)


def _dot_tn(a, b):
    return lax.dot_general(a, b, (((0,), (0,)), ((), ())), preferred_element_type=F32)


def _rms(x, g):
    ms = jnp.mean(x * x, axis=-1, keepdims=True)
    return x * lax.rsqrt(ms + EPS) * g


def _sigmoid(x):
    return 1.0 / (1.0 + jnp.exp(-x))


def _inproj_body(x_ref, g_ref, w_ref, o_ref, *rest):
    qkv_refs, qkv_f32 = rest[:-1], rest[-1]
    h = _rms(x_ref[...], g_ref[...]).astype(BF16)
    for c0 in range(0, QKV_COLS, MXU_WIDTH):
        res = _dot(h, w_ref[:, c0:c0 + MXU_WIDTH])
        for t in range(MXU_WIDTH // LANES):
            qkv_f32[c0 // LANES + t] = res[:, t * LANES:(t + 1) * LANES]
    for (_, dilation), ref in zip(DILATED_PATTERNS, qkv_refs):
        for r in range(dilation):
            rows = pl.ds(r, TM_PROJ // dilation, stride=dilation)
            ref[r] = jnp.concatenate([qkv_f32[t, rows, :] for t in range(QKV_COLS // LANES)],
                                     axis=-1).astype(ref.dtype)
    for c0 in range(0, U_COLS, MXU_WIDTH):
        c1 = min(c0 + MXU_WIDTH, U_COLS)
        o_ref[:, c0:c1] = _dot(h, w_ref[:, QKV_COLS + c0:QKV_COLS + c1])


def _subseq_spec(dilation, cols):
    tiles = SEQ // TM_PROJ
    return pl.BlockSpec((None, dilation, TM_PROJ // dilation, cols), lambda i: (i // tiles, 0, i % tiles, 0))


def _inproj(x, g, w):
    qkv_shapes = [jax.ShapeDtypeStruct((BATCH, d, SEQ // d, QKV_COLS), BF16) for _, d in DILATED_PATTERNS]
    return pl.pallas_call(
        _inproj_body,
        grid=(TOKENS // TM_PROJ,),
        in_specs=[
            pl.BlockSpec((TM_PROJ, D_MODEL), lambda i: (i, 0)),
            pl.BlockSpec((1, D_MODEL), lambda i: (0, 0)),
            pl.BlockSpec((D_MODEL, QKV_COLS + U_COLS), lambda i: (0, 0)),
        ],
        out_specs=[pl.BlockSpec((TM_PROJ, U_COLS), lambda i: (i, 0))]
                  + [_subseq_spec(d, QKV_COLS) for _, d in DILATED_PATTERNS],
        out_shape=[jax.ShapeDtypeStruct((TOKENS, U_COLS), F32)] + qkv_shapes,
        scratch_shapes=[pltpu.VMEM((QKV_COLS // LANES, TM_PROJ, LANES), F32)],
        compiler_params=_params(("parallel",)),
        name="inproj",
    )(x, g, w)


def _rel_bucket(dist):
    max_exact = REL_BUCKETS // 2
    d = jnp.maximum(dist, 0)
    log_ratio = jnp.log(jnp.maximum(d, 1).astype(F32) / max_exact) / math.log(REL_MAX_DISTANCE / max_exact)
    large = jnp.minimum(max_exact + (log_ratio * (REL_BUCKETS - max_exact)).astype(jnp.int32), REL_BUCKETS - 1)
    return jnp.where(d < max_exact, d, large)


def _bucket_table(window, dilation):
    span = window // dilation
    qi = jnp.arange(ATT_BLOCK)[:, None]
    kj = jnp.arange(2 * ATT_BLOCK)[None, :]
    sub_dist = qi - kj + ATT_BLOCK
    band = (sub_dist >= 0) & (sub_dist <= span)
    return jnp.where(band, _rel_bucket(sub_dist * dilation), -1).astype(jnp.int32)


def _attn_body(rb_ref, bidx_ref, qkv_ref, o_ref, lse_ref, bias_ref, *, sub_blocks, unroll):
    nblk = SEQ // ATT_BLOCK

    @pl.when(pl.program_id(0) == 0)
    def _():
        bidx = bidx_ref[...]
        in_prev = lax.broadcasted_iota(jnp.int32, bidx.shape, 1) < ATT_BLOCK
        for h in range(ATT_HEADS):
            acc = jnp.full(bidx.shape, NEG_BIG, F32)
            for b in range(REL_BUCKETS):
                acc = jnp.where(bidx == b, rb_ref[b, h], acc)
            bias_ref[0, h] = acc
            bias_ref[1, h] = jnp.where(in_prev, NEG_BIG, acc)
            bias_ref[2, h] = jnp.concatenate([acc[:, ATT_BLOCK:], jnp.full_like(acc[:, ATT_BLOCK:], NEG_BIG)], axis=1)

    def block(n, first):
        if first:
            rows, krows, variant = slice(0, ATT_BLOCK), slice(0, 2 * ATT_BLOCK), 2
        else:
            r0 = pl.multiple_of(n * ATT_BLOCK, ATT_BLOCK)
            rows = pl.ds(r0, ATT_BLOCK)
            krows = pl.ds(r0 - ATT_BLOCK, 2 * ATT_BLOCK)
            if sub_blocks == nblk:
                variant = 0
            elif sub_blocks == 1:
                variant = 1
            else:
                variant = jnp.where(n % sub_blocks == 0, 1, 0)
        q = qkv_ref[rows, 0:ATT_WIDTH]
        kk = qkv_ref[krows, ATT_WIDTH:2 * ATT_WIDTH]
        vv = qkv_ref[krows, 2 * ATT_WIDTH:3 * ATT_WIDTH]
        q = q * jnp.asarray(HEAD_DIM ** -0.5, BF16)
        head_of_lane = lax.broadcasted_iota(jnp.int32, (ATT_BLOCK, ATT_WIDTH), 1) // HEAD_DIM
        ones = jnp.ones((kk.shape[0], LANES), BF16)
        num = den = mx = None
        for h in range(ATT_HEADS):
            mine = head_of_lane == h
            bias = bias_ref[variant, h]
            s = _dot_nt(jnp.where(mine, q, jnp.zeros_like(q)), kk) + bias
            m = jnp.max(s, axis=-1, keepdims=True)
            p = jnp.exp(s - m).astype(BF16)
            num_h = _dot(p, vv)
            den_h = jnp.tile(_dot(p, ones), (1, ATT_WIDTH // LANES))
            m_h = jnp.broadcast_to(m, (ATT_BLOCK, ATT_WIDTH))
            num = num_h if h == 0 else jnp.where(mine, num_h, num)
            den = den_h if h == 0 else jnp.where(mine, den_h, den)
            mx = m_h if h == 0 else jnp.where(mine, m_h, mx)
        o_ref[rows, :] = (num / den).astype(o_ref.dtype)
        lse_ref[rows, :] = mx + jnp.log(den)

    block(0, True)

    def loop_body(n, carry):
        block(n, False)
        return carry
    lax.fori_loop(1, nblk, loop_body, 0, unroll=unroll)


def _attention_pattern(ua, rel_bias, window, dilation):
    L = SEQ // dilation
    shape = (BATCH, dilation, L, ATT_WIDTH)
    qkv_spec = pl.BlockSpec((None, SEQ, QKV_COLS), lambda b: (b, 0, 0))
    out_spec = pl.BlockSpec((None, SEQ, ATT_WIDTH), lambda b: (b, 0, 0))
    o, lse = pl.pallas_call(
        functools.partial(_attn_body, sub_blocks=L // ATT_BLOCK, unroll=ATT_UNROLL),
        grid=(BATCH,),
        in_specs=[
            pl.BlockSpec(memory_space=pltpu.SMEM),
            pl.BlockSpec((ATT_BLOCK, 2 * ATT_BLOCK), lambda b: (0, 0)),
            qkv_spec,
        ],
        out_specs=[out_spec, out_spec],
        out_shape=[jax.ShapeDtypeStruct((BATCH, SEQ, ATT_WIDTH), BF16),
                   jax.ShapeDtypeStruct((BATCH, SEQ, ATT_WIDTH), F32)],
        scratch_shapes=[pltpu.VMEM((3, ATT_HEADS, ATT_BLOCK, 2 * ATT_BLOCK), F32)],
        compiler_params=_params(("arbitrary",)),
        name=f"attn_d{dilation}",
    )(rel_bias, _bucket_table(window, dilation), ua.reshape(BATCH, SEQ, QKV_COLS))
    return o.reshape(shape), lse.reshape(shape)


def _attention(qkvs, rel_bias):
    return [_attention_pattern(ua, rel_bias, window, dilation)
            for ua, (window, dilation) in zip(qkvs, DILATED_PATTERNS)]


def _short_conv(cin_ref, cb_ref, cc_ref, w_ref, o_ref, tail_ref):
    uu = cc_ref[...] * cin_ref[...]
    t = lax.broadcasted_iota(jnp.int32, uu.shape, 0)
    y = uu * w_ref[CONV_WIDTH - 1:CONV_WIDTH, :]
    for shift in range(1, CONV_WIDTH):
        prev = pltpu.roll(uu, shift, axis=0)
        for r in range(shift):
            prev = jnp.where(t == r, tail_ref[8 - shift + r:8 - shift + r + 1, :], prev)
        y = y + prev * w_ref[CONV_WIDTH - 1 - shift:CONV_WIDTH - shift, :]
    o_ref[...] = (cb_ref[...] * y).astype(o_ref.dtype)
    tail_ref[...] = uu[uu.shape[0] - 8:, :]


def _side_cast(srcs, dsts):
    for src, dst in zip(srcs, dsts):
        dst[...] = src[...].astype(dst.dtype)


def _side_cast_specs(arrays, steps, step_of):
    return [pl.BlockSpec((a.shape[0] // steps, a.shape[1]), lambda *ids: (step_of(*ids), 0)) for a in arrays]


def _gla_body(q_ref, k_ref, v_ref, gr_ref, glr_ref, wg_ref, bg_ref, gn_ref, cin_ref, cb_ref, cc_ref, cw_ref,
              *rest):
    n_cast = (len(rest) - 4) // 2
    o_ref, cv_ref = rest[n_cast:n_cast + 2]
    s_ref, tail_ref = rest[-2:]
    _side_cast(rest[:n_cast], rest[n_cast + 2:-2])

    @pl.when(pl.program_id(1) == 0)
    def _():
        s_ref[...] = jnp.zeros_like(s_ref)
        tail_ref[...] = jnp.zeros_like(tail_ref)

    _short_conv(cin_ref, cb_ref, cc_ref, cw_ref, cv_ref, tail_ref)

    C = GLA_CHUNK
    row = lax.broadcasted_iota(jnp.int32, (C, C), 0)
    col = lax.broadcasted_iota(jnp.int32, (C, C), 1)
    tril = row >= col
    n_chunks = TC_GLA // C

    xg = _dot3(glr_ref[...], wg_ref[...]) + bg_ref[...]
    la_all = (jnp.minimum(xg, 0.0) - jnp.log(1.0 + jnp.exp(-jnp.abs(xg)))) * (1.0 / 16.0)

    trow = lax.broadcasted_iota(jnp.int32, (TC_GLA, TC_GLA), 0)
    tcol = lax.broadcasted_iota(jnp.int32, (TC_GLA, TC_GLA), 1)
    chunk_tril = jnp.where((trow // C == tcol // C) & (trow >= tcol), 1.0, 0.0).astype(BF16)
    la_hi, la_lo = _split_bf16(la_all)
    cum_all = _dot(chunk_tril, la_hi) + _dot(chunk_tril, la_lo)
    totals = jnp.concatenate([cum_all[(c + 1) * C - 1:(c + 1) * C, :] for c in range(n_chunks)]
                             + [jnp.zeros((LANES - n_chunks, GLA_QK), F32)], axis=0)
    decay_cols = jnp.exp(totals.T)

    for c in range(n_chunks):
        rows = slice(c * C, (c + 1) * C)
        cum = cum_all[rows]
        last = cum[C - 1:C, :]
        q = q_ref[rows, :]
        k = k_ref[rows, :]
        qt = (q * jnp.exp(cum) * (GLA_DK ** -0.5)).astype(BF16)
        kt = (k * jnp.exp(-cum)).astype(BF16)
        kl_t = (k * jnp.exp(last - cum)).T.astype(BF16)
        for h in range(GLA_HEADS):
            sl = slice(h * GLA_DK, (h + 1) * GLA_DK)
            vs = slice(h * GLA_DV, (h + 1) * GLA_DV)
            vh = v_ref[rows, vs].astype(BF16)
            state = s_ref[h]
            st_hi, st_lo = _split_bf16(state)
            sc = jnp.where(tril, _dot_nt(qt[:, sl], kt[:, sl]), 0.0).astype(BF16)
            o = _dot(qt[:, sl], st_hi) + _dot(qt[:, sl], st_lo) + _dot(sc, vh)
            decay = jnp.broadcast_to(decay_cols[sl, c:c + 1], state.shape)
            s_ref[h] = decay * state + _dot(kl_t[sl, :], vh)
            g = gr_ref[rows, vs]
            o_ref[rows, vs] = (_rms(o, gn_ref[...]) * (g * _sigmoid(g))).astype(o_ref.dtype)


def _gla_conv(u, wg, bg, gn, conv_w, side_casts=()):
    nj = SEQ // TC_GLA
    row = lambda b, j: b * nj + j
    tile = lambda width, col: pl.BlockSpec((TC_GLA, width), lambda b, j: (row(b, j), col // width))
    full = lambda a: pl.BlockSpec(a.shape, lambda b, j: (0, 0))
    cast_specs = _side_cast_specs(side_casts, BATCH * nj, row)
    go, cv, *casts = pl.pallas_call(
        _gla_body,
        grid=(BATCH, nj),
        in_specs=[
            tile(GLA_QK, COL_GQ), tile(GLA_QK, COL_GK), tile(GLA_V, COL_GV), tile(GLA_V, COL_GR),
            tile(LANES, COL_GLR), full(wg), full(bg), full(gn),
            tile(CONV_CH, COL_CIN), tile(CONV_CH, COL_CB), tile(CONV_CH, COL_CC), full(conv_w),
        ] + cast_specs,
        out_specs=[pl.BlockSpec((TC_GLA, GLA_V), lambda b, j: (row(b, j), 0)),
                   pl.BlockSpec((TC_GLA, CONV_CH), lambda b, j: (row(b, j), 0))] + cast_specs,
        out_shape=[jax.ShapeDtypeStruct((TOKENS, GLA_V), BF16), jax.ShapeDtypeStruct((TOKENS, CONV_CH), BF16)]
                  + [jax.ShapeDtypeStruct(a.shape, BF16) for a in side_casts],
        scratch_shapes=[pltpu.VMEM((GLA_HEADS, GLA_DK, GLA_DV), F32), pltpu.VMEM((8, CONV_CH), F32)],
        compiler_params=_params(("arbitrary", "arbitrary")),
        name="gla_conv",
    )(u, u, u, u, u, wg, bg, gn, u, u, u, conv_w, *side_casts)
    return go, cv, casts


def _outproj_body(*refs, route):
    (o1, o4, o16, l1, l4, l16, go_ref, cv_ref, x_ref, w_ref, g_ref) = refs[:11]
    perm_ref = refs[-1]
    if route:
        wr_ref, xo_ref, ho_ref, route_ref, counts_ref, carry_ref = refs[11:-1]
    else:
        xo_ref, ho_ref = refs[11:-1]

    def token_order(ref, slot):
        dilation, rows, _ = ref.shape
        if dilation == 1:
            return ref[0].astype(F32)
        tiles = range(ATT_WIDTH // LANES)
        for r in range(dilation):
            val = ref[r].astype(F32)
            for t in tiles:
                perm_ref[slot, t, pl.ds(r, rows, stride=dilation), :] = val[:, t * LANES:(t + 1) * LANES]
        return jnp.concatenate([perm_ref[slot, t] for t in tiles], axis=-1)

    la, lb, lc = token_order(l1, 0), token_order(l4, 0), token_order(l16, 1)
    oa, ob, oc = token_order(o1, 0), token_order(o4, 2), token_order(o16, 3)
    m = jnp.maximum(jnp.maximum(la, lb), lc)
    ea, eb, ec = jnp.exp(la - m), jnp.exp(lb - m), jnp.exp(lc - m)
    att = (ea * oa + eb * ob + ec * oc) / (ea + eb + ec)
    y = (x_ref[...]
         + _dot(att.astype(BF16), w_ref[0:ATT_WIDTH, :])
         + _dot(go_ref[...], w_ref[ATT_WIDTH:ATT_WIDTH + GLA_V, :])
         + _dot(cv_ref[...], w_ref[ATT_WIDTH + GLA_V:MIX_WIDTH, :]))
    xo_ref[...] = y
    hf = _rms(y, g_ref[...])
    ho_ref[...] = hf.astype(ho_ref.dtype)
    if route:
        @pl.when(pl.program_id(0) == 0)
        def _():
            carry_ref[...] = jnp.zeros_like(carry_ref)

        tm = hf.shape[0]
        ne = N_EXPERTS
        hf_hi, hf_lo = _split_bf16(hf)
        part = _dot_nt(wr_ref[0], hf_hi) + _dot_nt(wr_ref[1], hf_lo)
        logits = part[0:ne] + part[ne:2 * ne]
        eidx = lax.broadcasted_iota(jnp.int32, logits.shape, 0).astype(F32)
        v1 = jnp.max(logits, axis=0, keepdims=True)
        i1 = jnp.min(jnp.where(logits == v1, eidx, float(ne)), axis=0, keepdims=True)
        lg2 = jnp.where(eidx == i1, -jnp.inf, logits)
        v2 = jnp.max(lg2, axis=0, keepdims=True)
        i2 = jnp.min(jnp.where(lg2 == v2, eidx, float(ne)), axis=0, keepdims=True)
        e2 = jnp.exp(v2 - v1)
        w1 = 1.0 / (1.0 + e2)
        w2 = e2 * w1
        sel1 = eidx == i1
        sel2 = eidx == i2
        onehot = jnp.where(sel1, 1.0, jnp.where(sel2, 1.0, 0.0))
        tri = (lax.broadcasted_iota(jnp.int32, (tm, tm), 0) <= lax.broadcasted_iota(jnp.int32, (tm, tm), 1))
        onehot16 = jnp.concatenate([onehot, jnp.zeros_like(onehot)], axis=0).astype(BF16)
        csum = _dot(onehot16, jnp.where(tri, 1.0, 0.0).astype(BF16))[0:ne]
        carry = carry_ref[:, 0:1]
        rank = csum - onehot + carry
        r1 = jnp.sum(jnp.where(sel1, rank, 0.0), axis=0, keepdims=True)
        r2 = jnp.sum(jnp.where(sel2, rank, 0.0), axis=0, keepdims=True)
        total = jnp.broadcast_to(carry + csum[:, tm - 1:tm], carry_ref.shape)
        carry_ref[...] = total
        counts_ref[...] = total
        rows = {ROUTE_I1: i1, ROUTE_I2: i2, ROUTE_W1: w1, ROUTE_W2: w2, ROUTE_R1: r1, ROUTE_R2: r2}
        zero = jnp.zeros_like(i1)
        route_ref[...] = jnp.concatenate([rows.get(r, zero) for r in range(8)], axis=0)


def _outproj(att, go, cv, x, w, g, w_router=None):
    route = w_router is not None
    tm = TM_PROJ
    tile = lambda cols: pl.BlockSpec((tm, cols), lambda i: (i, 0))
    full = lambda a: pl.BlockSpec(a.shape, lambda i: (0, 0))
    (o1, l1), (o4, l4), (o16, l16) = att
    args = [o1, o4, o16, l1, l4, l16, go, cv, x, w, g]
    att_specs = [_subseq_spec(d, ATT_WIDTH) for _, d in DILATED_PATTERNS]
    in_specs = att_specs * 2 + [tile(GLA_V), tile(CONV_CH), tile(D_MODEL), full(w), full(g)]
    out_specs = [tile(D_MODEL), tile(D_MODEL)]
    out_shape = [jax.ShapeDtypeStruct((TOKENS, D_MODEL), F32),
                 jax.ShapeDtypeStruct((TOKENS, D_MODEL), F32 if route else BF16)]
    scratch = []
    if route:
        args.append(w_router)
        in_specs.append(pl.BlockSpec(w_router.shape, lambda i: (0, 0, 0)))
        out_specs += [pl.BlockSpec((8, tm), lambda i: (0, i)), pl.BlockSpec((N_EXPERTS, LANES), lambda i: (0, 0))]
        out_shape += [jax.ShapeDtypeStruct((8, TOKENS), F32), jax.ShapeDtypeStruct((N_EXPERTS, LANES), F32)]
        scratch = [pltpu.VMEM((N_EXPERTS, LANES), F32)]
    scratch.append(pltpu.VMEM((4, ATT_WIDTH // LANES, tm, LANES), F32))
    return pl.pallas_call(
        functools.partial(_outproj_body, route=route),
        grid=(TOKENS // tm,),
        in_specs=in_specs,
        out_specs=out_specs,
        out_shape=out_shape,
        scratch_shapes=scratch,
        compiler_params=_params(("arbitrary",) if route else ("parallel",)),
        name="outproj_route" if route else "outproj",
    )(*args)


def _swiglu_accumulate(h, w1_ref, w3_ref, w2_ref, acc_ref, tf):
    for c0 in range(0, tf, FFN_SUB):
        c1 = min(c0 + FFN_SUB, tf)
        a = _dot(h, w1_ref[:, c0:c1])
        b = _dot(h, w3_ref[:, c0:c1])
        act = a * _sigmoid(a) * b
        acc_ref[...] += _dot(act.astype(BF16), w2_ref[c0:c1, :])


def _ffn_body(x_ref, h_ref, w1_ref, w3_ref, w2_ref, *rest, tf):
    n_cast = (len(rest) - 1) // 2
    o_ref = rest[n_cast]

    @pl.when(pl.program_id(1) == 0)
    def _():
        o_ref[...] = x_ref[...]

    _swiglu_accumulate(h_ref[...], w1_ref, w3_ref, w2_ref, o_ref, tf)
    _side_cast(rest[:n_cast], rest[n_cast + 1:])


def _ffn(x, h, w1, w3, w2, *, tf, side_casts=()):
    f = w1.shape[1]
    tm = TM_FFN
    nj = f // tf
    tile = lambda cols: pl.BlockSpec((tm, cols), lambda i, j: (i, 0))
    cast_specs = _side_cast_specs(side_casts, (TOKENS // tm) * nj, lambda i, j: i * nj + j)
    res = pl.pallas_call(
        functools.partial(_ffn_body, tf=tf),
        grid=(TOKENS // tm, nj),
        in_specs=[tile(D_MODEL), tile(D_MODEL),
                  pl.BlockSpec((D_MODEL, tf), lambda i, j: (0, j)),
                  pl.BlockSpec((D_MODEL, tf), lambda i, j: (0, j)),
                  pl.BlockSpec((tf, D_MODEL), lambda i, j: (j, 0))] + cast_specs,
        out_specs=[tile(D_MODEL)] + cast_specs,
        out_shape=[jax.ShapeDtypeStruct((TOKENS, D_MODEL), F32)]
                  + [jax.ShapeDtypeStruct(a.shape, BF16) for a in side_casts],
        compiler_params=_params(("parallel", "arbitrary")),
        name="dense_ffn",
    )(x, h, w1, w3, w2, *side_casts)
    return res[0], res[1:]


ROW_TILE = D_MODEL // LANES


def _to_row_tiled(dst_ref, lead, val):
    rows = val.shape[0]
    for s in range(ROW_TILE):
        dst_ref[(*lead, pl.ds(s, rows, stride=ROW_TILE), slice(None))] = val[:, s * LANES:(s + 1) * LANES]


def _from_row_tiled(src_ref, lead, rows):
    return jnp.concatenate([src_ref[(*lead, pl.ds(s, rows, stride=ROW_TILE), slice(None))]
                            for s in range(ROW_TILE)], axis=-1)


def _row_tile(idx):
    return pl.ds(pl.multiple_of(idx * ROW_TILE, ROW_TILE), ROW_TILE)


def _dispatch_body(pos_ref, last_tile_ref, h_ref, xs_hbm, stage_ref, zero_ref, sem, zero_sem):
    i = pl.program_id(0)
    n = pl.num_programs(0)
    tm = DISPATCH_CHUNK
    slot = i % 2

    def drain(s):
        for _ in range(2):
            pltpu.make_async_copy(stage_ref.at[s], xs_hbm.at[pl.ds(0, tm * ROW_TILE)], sem.at[s]).wait()

    @pl.when(i == 0)
    def _():
        zero_ref[...] = jnp.zeros_like(zero_ref)

        def zero_copy(e):
            start = pl.multiple_of(last_tile_ref[e] * (TM_MOE * ROW_TILE), TM_MOE * ROW_TILE)
            return pltpu.make_async_copy(zero_ref, xs_hbm.at[pl.ds(start, TM_MOE * ROW_TILE)], zero_sem)

        for e in range(2 * N_EXPERTS):
            @pl.when(last_tile_ref[e] >= 0)
            def _():
                zero_copy(e).start()
        for e in range(2 * N_EXPERTS):
            @pl.when(last_tile_ref[e] >= 0)
            def _():
                zero_copy(e).wait()

    @pl.when(i >= 2)
    def _():
        drain(slot)

    _to_row_tiled(stage_ref, (slot,), h_ref[...])

    def body(t, carry):
        for k in range(2):
            dst = pos_ref[2 * (i * tm + t) + k]
            pltpu.make_async_copy(stage_ref.at[slot, _row_tile(t)], xs_hbm.at[_row_tile(dst)],
                                  sem.at[slot]).start(priority=k)
        return carry
    lax.fori_loop(0, tm, body, 0, unroll=8)

    @pl.when(i == n - 1)
    def _():
        drain(1 - slot)
        drain(slot)


def _dispatch(pos, last_tile, h):
    tm = DISPATCH_CHUNK
    return pl.pallas_call(
        _dispatch_body,
        grid=(TOKENS // tm,),
        in_specs=[pl.BlockSpec(memory_space=pltpu.SMEM),
                  pl.BlockSpec(memory_space=pltpu.SMEM),
                  pl.BlockSpec((tm, D_MODEL), lambda i: (i, 0))],
        out_specs=pl.BlockSpec(memory_space=pl.ANY),
        out_shape=jax.ShapeDtypeStruct((N_SORTED * ROW_TILE, LANES), F32),
        scratch_shapes=[pltpu.VMEM((2, tm * ROW_TILE, LANES), F32),
                        pltpu.VMEM((TM_MOE * ROW_TILE, LANES), F32),
                        pltpu.SemaphoreType.DMA((2,)), pltpu.SemaphoreType.DMA(())],
        compiler_params=_params(("arbitrary",)),
        name="moe_dispatch",
    )(pos, last_tile, h)


def _gffn_body(te_ref, nu_ref, xs_ref, w1_ref, w3_ref, w2_ref, o_ref, hb_ref, acc_ref):
    del te_ref
    i = pl.program_id(0)
    j = pl.program_id(1)

    used = i < nu_ref[0]

    @pl.when(j == 0)
    def _():
        acc_ref[...] = jnp.zeros_like(acc_ref)

    @pl.when(used & (j == 0))
    def _():
        hb_ref[...] = _from_row_tiled(xs_ref, (), TM_MOE).astype(BF16)

    @pl.when(used)
    def _():
        _swiglu_accumulate(hb_ref[...], w1_ref, w3_ref, w2_ref, acc_ref, TF_MOE)

    @pl.when(j == pl.num_programs(1) - 1)
    def _():
        _to_row_tiled(o_ref, (), acc_ref[...])


def _grouped_ffn(tile_expert, n_used, xs, w1, w3, w2):
    nj = FFN_EXPERT // TF_MOE
    col = lambda i, j, nu: jnp.where(i < nu[0], j, nj - 1)
    grid_spec = pltpu.PrefetchScalarGridSpec(
        num_scalar_prefetch=2,
        grid=(N_TILES_MOE, nj),
        in_specs=[
            pl.BlockSpec((TM_MOE * ROW_TILE, LANES), lambda i, j, te, nu: (jnp.minimum(i, nu[0] - 1), 0)),
            pl.BlockSpec((None, D_MODEL, TF_MOE), lambda i, j, te, nu: (te[i], 0, col(i, j, nu))),
            pl.BlockSpec((None, D_MODEL, TF_MOE), lambda i, j, te, nu: (te[i], 0, col(i, j, nu))),
            pl.BlockSpec((None, TF_MOE, D_MODEL), lambda i, j, te, nu: (te[i], col(i, j, nu), 0)),
        ],
        out_specs=pl.BlockSpec((TM_MOE * ROW_TILE, LANES), lambda i, j, te, nu: (i, 0)),
        scratch_shapes=[pltpu.VMEM((TM_MOE, D_MODEL), BF16), pltpu.VMEM((TM_MOE, D_MODEL), F32)],
    )
    return pl.pallas_call(
        _gffn_body,
        grid_spec=grid_spec,
        out_shape=jax.ShapeDtypeStruct((N_SORTED * ROW_TILE, LANES), F32),
        compiler_params=_params(("arbitrary", "arbitrary")),
        name="moe_ffn",
    )(tile_expert, n_used, xs, w1, w3, w2)


def _combine_body(pos_ref, x_ref, gate_ref, g_ref, ys_hbm, o_ref, buf_ref, sem):
    i = pl.program_id(0)
    n = pl.num_programs(0)
    tm = TM_COMBINE

    def issue(tile, slot):
        def body(t, carry):
            for k in range(2):
                src = pos_ref[2 * (tile * tm + t) + k]
                pltpu.make_async_copy(ys_hbm.at[_row_tile(src)], buf_ref.at[slot, k, _row_tile(t)],
                                      sem.at[slot]).start(priority=k)
            return carry
        lax.fori_loop(0, tm, body, 0, unroll=8)

    @pl.when(i == 0)
    def _():
        issue(0, 0)

    @pl.when(i + 1 < n)
    def _():
        issue(i + 1, (i + 1) % 2)

    slot = i % 2
    for k in range(2):
        pltpu.make_async_copy(ys_hbm.at[pl.ds(0, tm * ROW_TILE)], buf_ref.at[slot, k], sem.at[slot]).wait()
    w1 = gate_ref[:, 0:1]
    w2 = gate_ref[:, 1:2]
    y = x_ref[...] + w1 * _from_row_tiled(buf_ref, (slot, 0), tm) + w2 * _from_row_tiled(buf_ref, (slot, 1), tm)
    o_ref[...] = _rms(y, g_ref[...])


def _combine(pos, x, gates, g, ys):
    tm = TM_COMBINE
    return pl.pallas_call(
        _combine_body,
        grid=(TOKENS // tm,),
        in_specs=[pl.BlockSpec(memory_space=pltpu.SMEM),
                  pl.BlockSpec((tm, D_MODEL), lambda i: (i, 0)),
                  pl.BlockSpec((tm, 2), lambda i: (i, 0)),
                  pl.BlockSpec((1, D_MODEL), lambda i: (0, 0)),
                  pl.BlockSpec(memory_space=pl.ANY)],
        out_specs=pl.BlockSpec((tm, D_MODEL), lambda i: (i, 0)),
        out_shape=jax.ShapeDtypeStruct((TOKENS, D_MODEL), F32),
        scratch_shapes=[pltpu.VMEM((2, 2, tm * ROW_TILE, LANES), F32), pltpu.SemaphoreType.DMA((2,))],
        compiler_params=_params(("arbitrary",)),
        name="moe_combine",
    )(pos, x, gates, g, ys)


def _routing_tables(route, counts):
    cnt = counts[:, 0].astype(jnp.int32)
    tiles = (cnt + TM_MOE - 1) // TM_MOE
    tile_end = jnp.cumsum(tiles)
    tile_start = tile_end - tiles
    n_used = tile_end[-1]
    expert = route[ROUTE_I1:ROUTE_I2 + 1].astype(jnp.int32).T
    rank = route[ROUTE_R1:ROUTE_R2 + 1].astype(jnp.int32).T
    pos = (tile_start * TM_MOE)[expert] + rank
    tile_id = jnp.minimum(jnp.arange(N_TILES_MOE, dtype=jnp.int32), n_used - 1)
    tile_expert = jnp.sum(tile_id[:, None] >= tile_end[None, :], axis=1).astype(jnp.int32)
    last_tile = jnp.where(tiles > 0, tile_end - 1, -1)
    spare = n_used + jnp.arange(N_EXPERTS)
    zero_tiles = jnp.concatenate([last_tile, jnp.where(spare < N_TILES_MOE, spare, -1)]).astype(jnp.int32)
    return pos.reshape(2 * TOKENS), tile_expert, n_used.reshape(1), zero_tiles


def _moe(x, h, route, counts, w1, w3, w2, g_final):
    pos, tile_expert, n_used, last_tile = _routing_tables(route, counts)
    xs = _dispatch(pos, last_tile, h)
    ys = _grouped_ffn(tile_expert, n_used, xs, w1, w3, w2)
    return _combine(pos, x, route[ROUTE_W1:ROUTE_W2 + 1].T, g_final, ys)


def _prep_w_in(w):
    aq, ak, av, gq, gk, gv, gr, glr, c_in, c_b, c_c = jnp.split(w, np.cumsum(SPLIT_SIZES)[:-1].tolist(), axis=1)
    pad = jnp.zeros((D_MODEL, LANES - GLA_RANK), w.dtype)
    return jnp.concatenate([aq, ak, av, gv, gr, gq, gk, c_in, c_b, c_c, glr, pad], axis=1).astype(BF16)


def _prep_router(w):
    wt = w.T
    hi = wt.astype(BF16)
    lo = (wt - hi.astype(F32)).astype(BF16)
    return jnp.stack([jnp.concatenate([hi, lo]), jnp.concatenate([hi, jnp.zeros_like(hi)])])


def kernel(x, w_mix_in, w_mix_out, g_mix, rel_bias, gla_w_gate, gla_b_gate, gla_g_norm, conv_w,
           g_ffn, ffn_w1, ffn_w3, ffn_w2, moe_router, moe_w1, moe_w3, moe_w2, g_final):
    assert DEPTH == 2
    x = x.reshape(TOKENS, D_MODEL)
    flat = lambda w: w.reshape(-1, w.shape[-1])
    gla_casts = {0: moe_w1[0], 1: moe_w3[0]}
    experts = {}
    for layer in range(DEPTH):
        u, *qkvs = _inproj(x, g_mix[layer].reshape(1, D_MODEL), _prep_w_in(w_mix_in[layer]))
        att = _attention(qkvs, rel_bias)
        wg = jnp.pad(gla_w_gate[layer], ((0, LANES - GLA_RANK), (0, 0)))
        go, cv, (cast,) = _gla_conv(u, wg, gla_b_gate[layer].reshape(1, GLA_QK),
                                    gla_g_norm[layer].reshape(1, GLA_DV),
                                    jnp.pad(conv_w[layer], ((0, 8 - CONV_WIDTH), (0, 0))),
                                    side_casts=[flat(gla_casts[layer])])
        experts[layer] = cast.reshape(gla_casts[layer].shape)
        w_out = w_mix_out[layer].astype(BF16)
        g2 = g_ffn[layer].reshape(1, D_MODEL)
        i = layer // 2
        if layer % 2 == 0:
            x, h = _outproj(att, go, cv, x, w_out, g2)
            x, (cast,) = _ffn(x, h, ffn_w1[i].astype(BF16), ffn_w3[i].astype(BF16), ffn_w2[i].astype(BF16),
                              tf=TF_DENSE, side_casts=[flat(moe_w2[i])])
            experts["w2"] = cast.reshape(moe_w2[i].shape)
        else:
            x, h, route, counts = _outproj(att, go, cv, x, w_out, g2, _prep_router(moe_router[i]))
            x = _moe(x, h, route, counts, experts[0], experts[1], experts["w2"], g_final.reshape(1, D_MODEL))
    return x.reshape(BATCH, SEQ, D_MODEL)
```

```python
import functools
import math

import jax
import jax.numpy as jnp
import numpy as np
from jax import lax
from jax.experimental import pallas as pl
from jax.experimental.pallas import tpu as pltpu

F32 = jnp.float32
BF16 = jnp.bfloat16

D_MODEL = 1024
BATCH = 8
SEQ = 2048
TOKENS = BATCH * SEQ
DEPTH = 2
EPS = 1e-6

HEAD_DIM = 64
ATT_HEADS = 4
ATT_WIDTH = ATT_HEADS * HEAD_DIM
DILATED_PATTERNS = ((128, 1), (512, 4), (2048, 16))
ATT_BLOCK = 128
REL_BUCKETS = 32
REL_MAX_DISTANCE = 2048

GLA_HEADS = 4
GLA_DK = 64
GLA_DV = 128
GLA_RANK = 16
GLA_CHUNK = 64
GLA_QK = GLA_HEADS * GLA_DK
GLA_V = GLA_HEADS * GLA_DV

CONV_CH = 256
CONV_WIDTH = 3
MIX_WIDTH = ATT_WIDTH + GLA_V + CONV_CH

SPLIT_SIZES = (ATT_WIDTH, ATT_WIDTH, ATT_WIDTH, GLA_QK, GLA_QK, GLA_V, GLA_V, GLA_RANK,
               CONV_CH, CONV_CH, CONV_CH)

FFN_DENSE = 2816
N_EXPERTS = 8
FFN_EXPERT = 3584

LANES = 128
MXU_WIDTH = 256
VMEM_LIMIT = 56 * 1024 * 1024

QKV_COLS = 3 * ATT_WIDTH
COL_GV, COL_GR, COL_GQ, COL_GK = 0, 512, 1024, 1280
COL_CIN, COL_CB, COL_CC, COL_GLR = 1536, 1792, 2048, 2304
U_COLS = COL_GLR + LANES

NEG_BIG = -1e30

ATT_UNROLL = 15
TM_PROJ = 512
TC_GLA = 512
TM_FFN = 512
TF_DENSE = 1408
FFN_SUB = 256
TM_MOE = 512
TF_MOE = 1792
N_TILES_MOE = 2 * TOKENS // TM_MOE + N_EXPERTS
N_SORTED = N_TILES_MOE * TM_MOE
DISPATCH_CHUNK = 256
TM_COMBINE = 256

ROUTE_I1, ROUTE_I2, ROUTE_W1, ROUTE_W2, ROUTE_R1, ROUTE_R2 = range(6)


def _params(sem):
    return pltpu.CompilerParams(dimension_semantics=sem, vmem_limit_bytes=VMEM_LIMIT)


def _split_bf16(a):
    hi = a.astype(BF16)
    lo = (a - hi.astype(F32)).astype(BF16)
    return hi, lo


def _dot(a, b):
    return jnp.dot(a, b, preferred_element_type=F32)


def _dot3(a, b):
    a_hi, a_lo = _split_bf16(a)
    b_hi, b_lo = _split_bf16(b)
    return _dot(a_hi, b_hi) + _dot(a_lo, b_hi) + _dot(a_hi, b_lo)


def _dot_nt(a, b):
    return lax.dot_general(a, b, (((1,), (1,)), ((), ())), preferred_element_type=F32)


def _dot_tn(a, b):
    return lax.dot_general(a, b, (((0,), (0,)), ((), ())), preferred_element_type=F32)


def _rms(x, g):
    ms = jnp.mean(x * x, axis=-1, keepdims=True)
    return x * lax.rsqrt(ms + EPS) * g


def _sigmoid(x):
    return 1.0 / (1.0 + jnp.exp(-x))


def _inproj_body(x_ref, g_ref, w_ref, o_ref, *rest):
    qkv_refs, qkv_f32 = rest[:-1], rest[-1]
    h = _rms(x_ref[...], g_ref[...]).astype(BF16)
    for c0 in range(0, QKV_COLS, MXU_WIDTH):
        res = _dot(h, w_ref[:, c0:c0 + MXU_WIDTH])
        for t in range(MXU_WIDTH // LANES):
            qkv_f32[c0 // LANES + t] = res[:, t * LANES:(t + 1) * LANES]
    for (_, dilation), ref in zip(DILATED_PATTERNS, qkv_refs):
        for r in range(dilation):
            rows = pl.ds(r, TM_PROJ // dilation, stride=dilation)
            ref[r] = jnp.concatenate([qkv_f32[t, rows, :] for t in range(QKV_COLS // LANES)],
                                     axis=-1).astype(ref.dtype)
    for c0 in range(0, U_COLS, MXU_WIDTH):
        c1 = min(c0 + MXU_WIDTH, U_COLS)
        o_ref[:, c0:c1] = _dot(h, w_ref[:, QKV_COLS + c0:QKV_COLS + c1])


def _subseq_spec(dilation, cols):
    tiles = SEQ // TM_PROJ
    return pl.BlockSpec((None, dilation, TM_PROJ // dilation, cols), lambda i: (i // tiles, 0, i % tiles, 0))


def _inproj(x, g, w):
    qkv_shapes = [jax.ShapeDtypeStruct((BATCH, d, SEQ // d, QKV_COLS), BF16) for _, d in DILATED_PATTERNS]
    return pl.pallas_call(
        _inproj_body,
        grid=(TOKENS // TM_PROJ,),
        in_specs=[
            pl.BlockSpec((TM_PROJ, D_MODEL), lambda i: (i, 0)),
            pl.BlockSpec((1, D_MODEL), lambda i: (0, 0)),
            pl.BlockSpec((D_MODEL, QKV_COLS + U_COLS), lambda i: (0, 0)),
        ],
        out_specs=[pl.BlockSpec((TM_PROJ, U_COLS), lambda i: (i, 0))]
                  + [_subseq_spec(d, QKV_COLS) for _, d in DILATED_PATTERNS],
        out_shape=[jax.ShapeDtypeStruct((TOKENS, U_COLS), F32)] + qkv_shapes,
        scratch_shapes=[pltpu.VMEM((QKV_COLS // LANES, TM_PROJ, LANES), F32)],
        compiler_params=_params(("parallel",)),
        name="inproj",
    )(x, g, w)


def _rel_bucket(dist):
    max_exact = REL_BUCKETS // 2
    d = jnp.maximum(dist, 0)
    log_ratio = jnp.log(jnp.maximum(d, 1).astype(F32) / max_exact) / math.log(REL_MAX_DISTANCE / max_exact)
    large = jnp.minimum(max_exact + (log_ratio * (REL_BUCKETS - max_exact)).astype(jnp.int32), REL_BUCKETS - 1)
    return jnp.where(d < max_exact, d, large)


def _bucket_table(window, dilation):
    span = window // dilation
    qi = jnp.arange(ATT_BLOCK)[:, None]
    kj = jnp.arange(2 * ATT_BLOCK)[None, :]
    sub_dist = qi - kj + ATT_BLOCK
    band = (sub_dist >= 0) & (sub_dist <= span)
    return jnp.where(band, _rel_bucket(sub_dist * dilation), -1).astype(jnp.int32)


def _attn_body(rb_ref, bidx_ref, qkv_ref, o_ref, lse_ref, bias_ref, *, sub_blocks, unroll):
    nblk = SEQ // ATT_BLOCK

    @pl.when(pl.program_id(0) == 0)
    def _():
        bidx = bidx_ref[...]
        in_prev = lax.broadcasted_iota(jnp.int32, bidx.shape, 1) < ATT_BLOCK
        for h in range(ATT_HEADS):
            acc = jnp.full(bidx.shape, NEG_BIG, F32)
            for b in range(REL_BUCKETS):
                acc = jnp.where(bidx == b, rb_ref[b, h], acc)
            bias_ref[0, h] = acc
            bias_ref[1, h] = jnp.where(in_prev, NEG_BIG, acc)
            bias_ref[2, h] = jnp.concatenate([acc[:, ATT_BLOCK:], jnp.full_like(acc[:, ATT_BLOCK:], NEG_BIG)], axis=1)

    def block(n, first):
        if first:
            rows, krows, variant = slice(0, ATT_BLOCK), slice(0, 2 * ATT_BLOCK), 2
        else:
            r0 = pl.multiple_of(n * ATT_BLOCK, ATT_BLOCK)
            rows = pl.ds(r0, ATT_BLOCK)
            krows = pl.ds(r0 - ATT_BLOCK, 2 * ATT_BLOCK)
            if sub_blocks == nblk:
                variant = 0
            elif sub_blocks == 1:
                variant = 1
            else:
                variant = jnp.where(n % sub_blocks == 0, 1, 0)
        q = qkv_ref[rows, 0:ATT_WIDTH]
        kk = qkv_ref[krows, ATT_WIDTH:2 * ATT_WIDTH]
        vv = qkv_ref[krows, 2 * ATT_WIDTH:3 * ATT_WIDTH]
        q = q * jnp.asarray(HEAD_DIM ** -0.5, BF16)
        head_of_lane = lax.broadcasted_iota(jnp.int32, (ATT_BLOCK, ATT_WIDTH), 1) // HEAD_DIM
        ones = jnp.ones((kk.shape[0], LANES), BF16)
        num = den = mx = None
        for h in range(ATT_HEADS):
            mine = head_of_lane == h
            bias = bias_ref[variant, h]
            s = _dot_nt(jnp.where(mine, q, jnp.zeros_like(q)), kk) + bias
            m = jnp.max(s, axis=-1, keepdims=True)
            p = jnp.exp(s - m).astype(BF16)
            num_h = _dot(p, vv)
            den_h = jnp.tile(_dot(p, ones), (1, ATT_WIDTH // LANES))
            m_h = jnp.broadcast_to(m, (ATT_BLOCK, ATT_WIDTH))
            num = num_h if h == 0 else jnp.where(mine, num_h, num)
            den = den_h if h == 0 else jnp.where(mine, den_h, den)
            mx = m_h if h == 0 else jnp.where(mine, m_h, mx)
        o_ref[rows, :] = (num / den).astype(o_ref.dtype)
        lse_ref[rows, :] = mx + jnp.log(den)

    block(0, True)

    def loop_body(n, carry):
        block(n, False)
        return carry
    lax.fori_loop(1, nblk, loop_body, 0, unroll=unroll)


def _attention_pattern(ua, rel_bias, window, dilation):
    L = SEQ // dilation
    shape = (BATCH, dilation, L, ATT_WIDTH)
    qkv_spec = pl.BlockSpec((None, SEQ, QKV_COLS), lambda b: (b, 0, 0))
    out_spec = pl.BlockSpec((None, SEQ, ATT_WIDTH), lambda b: (b, 0, 0))
    o, lse = pl.pallas_call(
        functools.partial(_attn_body, sub_blocks=L // ATT_BLOCK, unroll=ATT_UNROLL),
        grid=(BATCH,),
        in_specs=[
            pl.BlockSpec(memory_space=pltpu.SMEM),
            pl.BlockSpec((ATT_BLOCK, 2 * ATT_BLOCK), lambda b: (0, 0)),
            qkv_spec,
        ],
        out_specs=[out_spec, out_spec],
        out_shape=[jax.ShapeDtypeStruct((BATCH, SEQ, ATT_WIDTH), BF16),
                   jax.ShapeDtypeStruct((BATCH, SEQ, ATT_WIDTH), F32)],
        scratch_shapes=[pltpu.VMEM((3, ATT_HEADS, ATT_BLOCK, 2 * ATT_BLOCK), F32)],
        compiler_params=_params(("arbitrary",)),
        name=f"attn_d{dilation}",
    )(rel_bias, _bucket_table(window, dilation), ua.reshape(BATCH, SEQ, QKV_COLS))
    return o.reshape(shape), lse.reshape(shape)


def _attention(qkvs, rel_bias):
    return [_attention_pattern(ua, rel_bias, window, dilation)
            for ua, (window, dilation) in zip(qkvs, DILATED_PATTERNS)]


def _short_conv(cin_ref, cb_ref, cc_ref, w_ref, o_ref, tail_ref):
    uu = cc_ref[...] * cin_ref[...]
    t = lax.broadcasted_iota(jnp.int32, uu.shape, 0)
    y = uu * w_ref[CONV_WIDTH - 1:CONV_WIDTH, :]
    for shift in range(1, CONV_WIDTH):
        prev = pltpu.roll(uu, shift, axis=0)
        for r in range(shift):
            prev = jnp.where(t == r, tail_ref[8 - shift + r:8 - shift + r + 1, :], prev)
        y = y + prev * w_ref[CONV_WIDTH - 1 - shift:CONV_WIDTH - shift, :]
    o_ref[...] = (cb_ref[...] * y).astype(o_ref.dtype)
    tail_ref[...] = uu[uu.shape[0] - 8:, :]


def _side_cast(srcs, dsts):
    for src, dst in zip(srcs, dsts):
        dst[...] = src[...].astype(dst.dtype)


def _side_cast_specs(arrays, steps, step_of):
    return [pl.BlockSpec((a.shape[0] // steps, a.shape[1]), lambda *ids: (step_of(*ids), 0)) for a in arrays]


def _gla_body(q_ref, k_ref, v_ref, gr_ref, glr_ref, wg_ref, bg_ref, gn_ref, ctril_ref,
              cin_ref, cb_ref, cc_ref, cw_ref, *rest):
    n_cast = (len(rest) - 4) // 2
    o_ref, cv_ref = rest[n_cast:n_cast + 2]
    s_ref, tail_ref = rest[-2:]
    _side_cast(rest[:n_cast], rest[n_cast + 2:-2])

    @pl.when(pl.program_id(1) == 0)
    def _():
        s_ref[...] = jnp.zeros_like(s_ref)
        tail_ref[...] = jnp.zeros_like(tail_ref)

    _short_conv(cin_ref, cb_ref, cc_ref, cw_ref, cv_ref, tail_ref)

    C = GLA_CHUNK
    row = lax.broadcasted_iota(jnp.int32, (C, C), 0)
    col = lax.broadcasted_iota(jnp.int32, (C, C), 1)
    tril = row >= col
    n_chunks = TC_GLA // C

    xg = _dot3(glr_ref[...], wg_ref[...]) + bg_ref[...]
    la_all = (jnp.minimum(xg, 0.0) - jnp.log(1.0 + jnp.exp(-jnp.abs(xg)))) * (1.0 / 16.0)

    la_hi, la_lo = _split_bf16(la_all)
    cum_all = _dot(ctril_ref[...], la_hi) + _dot(ctril_ref[...], la_lo)
    totals = jnp.concatenate([cum_all[(c + 1) * C - 1:(c + 1) * C, :] for c in range(n_chunks)]
                             + [jnp.zeros((LANES - n_chunks, GLA_QK), F32)], axis=0)
    decay_cols = jnp.exp(totals.T)

    for c in range(n_chunks):
        rows = slice(c * C, (c + 1) * C)
        cum = cum_all[rows]
        last = cum[C - 1:C, :]
        q = q_ref[rows, :]
        k = k_ref[rows, :]
        qt = (q * jnp.exp(cum) * (GLA_DK ** -0.5)).astype(BF16)
        kt = (k * jnp.exp(-cum)).astype(BF16)
        kl_t = (k * jnp.exp(last - cum)).T.astype(BF16)
        for h in range(GLA_HEADS):
            sl = slice(h * GLA_DK, (h + 1) * GLA_DK)
            vs = slice(h * GLA_DV, (h + 1) * GLA_DV)
            vh = v_ref[rows, vs].astype(BF16)
            state = s_ref[h]
            st_hi, st_lo = _split_bf16(state)
            sc = jnp.where(tril, _dot_nt(qt[:, sl], kt[:, sl]), 0.0).astype(BF16)
            o = _dot(qt[:, sl], st_hi) + _dot(qt[:, sl], st_lo) + _dot(sc, vh)
            decay = jnp.broadcast_to(decay_cols[sl, c:c + 1], state.shape)
            s_ref[h] = decay * state + _dot(kl_t[sl, :], vh)
            g = gr_ref[rows, vs]
            o_ref[rows, vs] = (_rms(o, gn_ref[...]) * (g * _sigmoid(g))).astype(o_ref.dtype)


def _gla_conv(u, wg, bg, gn, conv_w, side_casts=()):
    nj = SEQ // TC_GLA
    row = lambda b, j: b * nj + j
    tile = lambda width, col: pl.BlockSpec((TC_GLA, width), lambda b, j: (row(b, j), col // width))
    full = lambda a: pl.BlockSpec(a.shape, lambda b, j: (0, 0))
    cast_specs = _side_cast_specs(side_casts, BATCH * nj, row)
    t = np.arange(TC_GLA)
    same_chunk = (t[:, None] // GLA_CHUNK) == (t[None, :] // GLA_CHUNK)
    chunk_tril = jnp.asarray(same_chunk & (t[:, None] >= t[None, :]), BF16)
    go, cv, *casts = pl.pallas_call(
        _gla_body,
        grid=(BATCH, nj),
        in_specs=[
            tile(GLA_QK, COL_GQ), tile(GLA_QK, COL_GK), tile(GLA_V, COL_GV), tile(GLA_V, COL_GR),
            tile(LANES, COL_GLR), full(wg), full(bg), full(gn), full(chunk_tril),
            tile(CONV_CH, COL_CIN), tile(CONV_CH, COL_CB), tile(CONV_CH, COL_CC), full(conv_w),
        ] + cast_specs,
        out_specs=[pl.BlockSpec((TC_GLA, GLA_V), lambda b, j: (row(b, j), 0)),
                   pl.BlockSpec((TC_GLA, CONV_CH), lambda b, j: (row(b, j), 0))] + cast_specs,
        out_shape=[jax.ShapeDtypeStruct((TOKENS, GLA_V), BF16), jax.ShapeDtypeStruct((TOKENS, CONV_CH), BF16)]
                  + [jax.ShapeDtypeStruct(a.shape, BF16) for a in side_casts],
        scratch_shapes=[pltpu.VMEM((GLA_HEADS, GLA_DK, GLA_DV), F32), pltpu.VMEM((8, CONV_CH), F32)],
        compiler_params=_params(("arbitrary", "arbitrary")),
        name="gla_conv",
    )(u, u, u, u, u, wg, bg, gn, chunk_tril, u, u, u, conv_w, *side_casts)
    return go, cv, casts


def _outproj_body(*refs, route):
    (o1, o4, o16, l1, l4, l16, go_ref, cv_ref, x_ref, w_ref, g_ref) = refs[:11]
    perm_ref = refs[-1]
    if route:
        wr_ref, xo_ref, ho_ref, route_ref, counts_ref, carry_ref = refs[11:-1]
    else:
        xo_ref, ho_ref = refs[11:-1]

    def token_order(ref, slot):
        dilation, rows, _ = ref.shape
        if dilation == 1:
            return ref[0].astype(F32)
        tiles = range(ATT_WIDTH // LANES)
        for r in range(dilation):
            val = ref[r].astype(F32)
            for t in tiles:
                perm_ref[slot, t, pl.ds(r, rows, stride=dilation), :] = val[:, t * LANES:(t + 1) * LANES]
        return jnp.concatenate([perm_ref[slot, t] for t in tiles], axis=-1)

    la, lb, lc = token_order(l1, 0), token_order(l4, 0), token_order(l16, 1)
    oa, ob, oc = token_order(o1, 0), token_order(o4, 2), token_order(o16, 3)
    m = jnp.maximum(jnp.maximum(la, lb), lc)
    ea, eb, ec = jnp.exp(la - m), jnp.exp(lb - m), jnp.exp(lc - m)
    att = (ea * oa + eb * ob + ec * oc) / (ea + eb + ec)
    y = (x_ref[...]
         + _dot(att.astype(BF16), w_ref[0:ATT_WIDTH, :])
         + _dot(go_ref[...], w_ref[ATT_WIDTH:ATT_WIDTH + GLA_V, :])
         + _dot(cv_ref[...], w_ref[ATT_WIDTH + GLA_V:MIX_WIDTH, :]))
    xo_ref[...] = y
    hf = _rms(y, g_ref[...])
    ho_ref[...] = hf.astype(ho_ref.dtype)
    if route:
        @pl.when(pl.program_id(0) == 0)
        def _():
            carry_ref[...] = jnp.zeros_like(carry_ref)

        tm = hf.shape[0]
        ne = N_EXPERTS
        hf_hi, hf_lo = _split_bf16(hf)
        part = _dot_nt(wr_ref[0], hf_hi) + _dot_nt(wr_ref[1], hf_lo)
        logits = part[0:ne] + part[ne:2 * ne]
        eidx = lax.broadcasted_iota(jnp.int32, logits.shape, 0).astype(F32)
        v1 = jnp.max(logits, axis=0, keepdims=True)
        i1 = jnp.min(jnp.where(logits == v1, eidx, float(ne)), axis=0, keepdims=True)
        lg2 = jnp.where(eidx == i1, -jnp.inf, logits)
        v2 = jnp.max(lg2, axis=0, keepdims=True)
        i2 = jnp.min(jnp.where(lg2 == v2, eidx, float(ne)), axis=0, keepdims=True)
        e2 = jnp.exp(v2 - v1)
        w1 = 1.0 / (1.0 + e2)
        w2 = e2 * w1
        sel1 = eidx == i1
        sel2 = eidx == i2
        onehot = jnp.where(sel1, 1.0, jnp.where(sel2, 1.0, 0.0))
        tri = (lax.broadcasted_iota(jnp.int32, (tm, tm), 0) <= lax.broadcasted_iota(jnp.int32, (tm, tm), 1))
        onehot16 = jnp.concatenate([onehot, jnp.zeros_like(onehot)], axis=0).astype(BF16)
        csum = _dot(onehot16, jnp.where(tri, 1.0, 0.0).astype(BF16))[0:ne]
        carry = carry_ref[:, 0:1]
        rank = csum - onehot + carry
        r1 = jnp.sum(jnp.where(sel1, rank, 0.0), axis=0, keepdims=True)
        r2 = jnp.sum(jnp.where(sel2, rank, 0.0), axis=0, keepdims=True)
        total = jnp.broadcast_to(carry + csum[:, tm - 1:tm], carry_ref.shape)
        carry_ref[...] = total
        counts_ref[...] = total
        rows = {ROUTE_I1: i1, ROUTE_I2: i2, ROUTE_W1: w1, ROUTE_W2: w2, ROUTE_R1: r1, ROUTE_R2: r2}
        zero = jnp.zeros_like(i1)
        route_ref[...] = jnp.concatenate([rows.get(r, zero) for r in range(8)], axis=0)


def _outproj(att, go, cv, x, w, g, w_router=None):
    route = w_router is not None
    tm = TM_PROJ
    tile = lambda cols: pl.BlockSpec((tm, cols), lambda i: (i, 0))
    full = lambda a: pl.BlockSpec(a.shape, lambda i: (0, 0))
    (o1, l1), (o4, l4), (o16, l16) = att
    args = [o1, o4, o16, l1, l4, l16, go, cv, x, w, g]
    att_specs = [_subseq_spec(d, ATT_WIDTH) for _, d in DILATED_PATTERNS]
    in_specs = att_specs * 2 + [tile(GLA_V), tile(CONV_CH), tile(D_MODEL), full(w), full(g)]
    out_specs = [tile(D_MODEL), tile(D_MODEL)]
    out_shape = [jax.ShapeDtypeStruct((TOKENS, D_MODEL), F32),
                 jax.ShapeDtypeStruct((TOKENS, D_MODEL), F32 if route else BF16)]
    scratch = []
    if route:
        args.append(w_router)
        in_specs.append(pl.BlockSpec(w_router.shape, lambda i: (0, 0, 0)))
        out_specs += [pl.BlockSpec((8, tm), lambda i: (0, i)), pl.BlockSpec((N_EXPERTS, LANES), lambda i: (0, 0))]
        out_shape += [jax.ShapeDtypeStruct((8, TOKENS), F32), jax.ShapeDtypeStruct((N_EXPERTS, LANES), F32)]
        scratch = [pltpu.VMEM((N_EXPERTS, LANES), F32)]
    scratch.append(pltpu.VMEM((4, ATT_WIDTH // LANES, tm, LANES), F32))
    return pl.pallas_call(
        functools.partial(_outproj_body, route=route),
        grid=(TOKENS // tm,),
        in_specs=in_specs,
        out_specs=out_specs,
        out_shape=out_shape,
        scratch_shapes=scratch,
        compiler_params=_params(("arbitrary",) if route else ("parallel",)),
        name="outproj_route" if route else "outproj",
    )(*args)


def _swiglu_accumulate(h, w1_ref, w3_ref, w2_ref, acc_ref, tf):
    for c0 in range(0, tf, FFN_SUB):
        c1 = min(c0 + FFN_SUB, tf)
        a = _dot(h, w1_ref[:, c0:c1])
        b = _dot(h, w3_ref[:, c0:c1])
        act = a * _sigmoid(a) * b
        acc_ref[...] += _dot(act.astype(BF16), w2_ref[c0:c1, :])


def _ffn_body(x_ref, h_ref, w1_ref, w3_ref, w2_ref, *rest, tf):
    n_cast = (len(rest) - 1) // 2
    o_ref = rest[n_cast]

    @pl.when(pl.program_id(1) == 0)
    def _():
        o_ref[...] = x_ref[...]

    _swiglu_accumulate(h_ref[...], w1_ref, w3_ref, w2_ref, o_ref, tf)
    _side_cast(rest[:n_cast], rest[n_cast + 1:])


def _ffn(x, h, w1, w3, w2, *, tf, side_casts=()):
    f = w1.shape[1]
    tm = TM_FFN
    nj = f // tf
    tile = lambda cols: pl.BlockSpec((tm, cols), lambda i, j: (i, 0))
    cast_specs = _side_cast_specs(side_casts, (TOKENS // tm) * nj, lambda i, j: i * nj + j)
    res = pl.pallas_call(
        functools.partial(_ffn_body, tf=tf),
        grid=(TOKENS // tm, nj),
        in_specs=[tile(D_MODEL), tile(D_MODEL),
                  pl.BlockSpec((D_MODEL, tf), lambda i, j: (0, j)),
                  pl.BlockSpec((D_MODEL, tf), lambda i, j: (0, j)),
                  pl.BlockSpec((tf, D_MODEL), lambda i, j: (j, 0))] + cast_specs,
        out_specs=[tile(D_MODEL)] + cast_specs,
        out_shape=[jax.ShapeDtypeStruct((TOKENS, D_MODEL), F32)]
                  + [jax.ShapeDtypeStruct(a.shape, BF16) for a in side_casts],
        compiler_params=_params(("parallel", "arbitrary")),
        name="dense_ffn",
    )(x, h, w1, w3, w2, *side_casts)
    return res[0], res[1:]


ROW_TILE = D_MODEL // LANES


def _to_row_tiled(dst_ref, lead, val):
    rows = val.shape[0]
    for s in range(ROW_TILE):
        dst_ref[(*lead, pl.ds(s, rows, stride=ROW_TILE), slice(None))] = val[:, s * LANES:(s + 1) * LANES]


def _from_row_tiled(src_ref, lead, rows):
    return jnp.concatenate([src_ref[(*lead, pl.ds(s, rows, stride=ROW_TILE), slice(None))]
                            for s in range(ROW_TILE)], axis=-1)


def _row_tile(idx):
    return pl.ds(pl.multiple_of(idx * ROW_TILE, ROW_TILE), ROW_TILE)


def _dispatch_body(pos_ref, last_tile_ref, h_ref, xs_hbm, stage_ref, zero_ref, sem, zero_sem):
    i = pl.program_id(0)
    n = pl.num_programs(0)
    tm = DISPATCH_CHUNK
    slot = i % 2

    def drain(s):
        for _ in range(2):
            pltpu.make_async_copy(stage_ref.at[s], xs_hbm.at[pl.ds(0, tm * ROW_TILE)], sem.at[s]).wait()

    @pl.when(i == 0)
    def _():
        zero_ref[...] = jnp.zeros_like(zero_ref)

        def zero_copy(e):
            start = pl.multiple_of(last_tile_ref[e] * (TM_MOE * ROW_TILE), TM_MOE * ROW_TILE)
            return pltpu.make_async_copy(zero_ref, xs_hbm.at[pl.ds(start, TM_MOE * ROW_TILE)], zero_sem)

        for e in range(2 * N_EXPERTS):
            @pl.when(last_tile_ref[e] >= 0)
            def _():
                zero_copy(e).start()
        for e in range(2 * N_EXPERTS):
            @pl.when(last_tile_ref[e] >= 0)
            def _():
                zero_copy(e).wait()

    @pl.when(i >= 2)
    def _():
        drain(slot)

    _to_row_tiled(stage_ref, (slot,), h_ref[...])

    def body(t, carry):
        for k in range(2):
            dst = pos_ref[2 * (i * tm + t) + k]
            pltpu.make_async_copy(stage_ref.at[slot, _row_tile(t)], xs_hbm.at[_row_tile(dst)],
                                  sem.at[slot]).start(priority=k)
        return carry
    lax.fori_loop(0, tm, body, 0, unroll=8)

    @pl.when(i == n - 1)
    def _():
        drain(1 - slot)
        drain(slot)


def _dispatch(pos, last_tile, h):
    tm = DISPATCH_CHUNK
    return pl.pallas_call(
        _dispatch_body,
        grid=(TOKENS // tm,),
        in_specs=[pl.BlockSpec(memory_space=pltpu.SMEM),
                  pl.BlockSpec(memory_space=pltpu.SMEM),
                  pl.BlockSpec((tm, D_MODEL), lambda i: (i, 0))],
        out_specs=pl.BlockSpec(memory_space=pl.ANY),
        out_shape=jax.ShapeDtypeStruct((N_SORTED * ROW_TILE, LANES), F32),
        scratch_shapes=[pltpu.VMEM((2, tm * ROW_TILE, LANES), F32),
                        pltpu.VMEM((TM_MOE * ROW_TILE, LANES), F32),
                        pltpu.SemaphoreType.DMA((2,)), pltpu.SemaphoreType.DMA(())],
        compiler_params=_params(("arbitrary",)),
        name="moe_dispatch",
    )(pos, last_tile, h)


def _gffn_body(te_ref, nu_ref, xs_ref, w1_ref, w3_ref, w2_ref, o_ref, hb_ref, acc_ref):
    del te_ref
    i = pl.program_id(0)
    j = pl.program_id(1)

    used = i < nu_ref[0]

    @pl.when(j == 0)
    def _():
        acc_ref[...] = jnp.zeros_like(acc_ref)

    @pl.when(used & (j == 0))
    def _():
        hb_ref[...] = _from_row_tiled(xs_ref, (), TM_MOE).astype(BF16)

    @pl.when(used)
    def _():
        _swiglu_accumulate(hb_ref[...], w1_ref, w3_ref, w2_ref, acc_ref, TF_MOE)

    @pl.when(j == pl.num_programs(1) - 1)
    def _():
        _to_row_tiled(o_ref, (), acc_ref[...])


def _grouped_ffn(tile_expert, n_used, xs, w1, w3, w2):
    nj = FFN_EXPERT // TF_MOE
    col = lambda i, j, nu: jnp.where(i < nu[0], j, nj - 1)
    grid_spec = pltpu.PrefetchScalarGridSpec(
        num_scalar_prefetch=2,
        grid=(N_TILES_MOE, nj),
        in_specs=[
            pl.BlockSpec((TM_MOE * ROW_TILE, LANES), lambda i, j, te, nu: (jnp.minimum(i, nu[0] - 1), 0)),
            pl.BlockSpec((None, D_MODEL, TF_MOE), lambda i, j, te, nu: (te[i], 0, col(i, j, nu))),
            pl.BlockSpec((None, D_MODEL, TF_MOE), lambda i, j, te, nu: (te[i], 0, col(i, j, nu))),
            pl.BlockSpec((None, TF_MOE, D_MODEL), lambda i, j, te, nu: (te[i], col(i, j, nu), 0)),
        ],
        out_specs=pl.BlockSpec((TM_MOE * ROW_TILE, LANES), lambda i, j, te, nu: (i, 0)),
        scratch_shapes=[pltpu.VMEM((TM_MOE, D_MODEL), BF16), pltpu.VMEM((TM_MOE, D_MODEL), F32)],
    )
    return pl.pallas_call(
        _gffn_body,
        grid_spec=grid_spec,
        out_shape=jax.ShapeDtypeStruct((N_SORTED * ROW_TILE, LANES), F32),
        compiler_params=_params(("arbitrary", "arbitrary")),
        name="moe_ffn",
    )(tile_expert, n_used, xs, w1, w3, w2)


def _combine_body(pos_ref, x_ref, gate_ref, g_ref, ys_hbm, o_ref, buf_ref, sem):
    i = pl.program_id(0)
    n = pl.num_programs(0)
    tm = TM_COMBINE

    def issue(tile, slot):
        def body(t, carry):
            for k in range(2):
                src = pos_ref[2 * (tile * tm + t) + k]
                pltpu.make_async_copy(ys_hbm.at[_row_tile(src)], buf_ref.at[slot, k, _row_tile(t)],
                                      sem.at[slot]).start(priority=k)
            return carry
        lax.fori_loop(0, tm, body, 0, unroll=8)

    @pl.when(i == 0)
    def _():
        issue(0, 0)

    @pl.when(i + 1 < n)
    def _():
        issue(i + 1, (i + 1) % 2)

    slot = i % 2
    for k in range(2):
        pltpu.make_async_copy(ys_hbm.at[pl.ds(0, tm * ROW_TILE)], buf_ref.at[slot, k], sem.at[slot]).wait()
    w1 = gate_ref[:, 0:1]
    w2 = gate_ref[:, 1:2]
    y = x_ref[...] + w1 * _from_row_tiled(buf_ref, (slot, 0), tm) + w2 * _from_row_tiled(buf_ref, (slot, 1), tm)
    o_ref[...] = _rms(y, g_ref[...])


def _combine(pos, x, gates, g, ys):
    tm = TM_COMBINE
    return pl.pallas_call(
        _combine_body,
        grid=(TOKENS // tm,),
        in_specs=[pl.BlockSpec(memory_space=pltpu.SMEM),
                  pl.BlockSpec((tm, D_MODEL), lambda i: (i, 0)),
                  pl.BlockSpec((tm, 2), lambda i: (i, 0)),
                  pl.BlockSpec((1, D_MODEL), lambda i: (0, 0)),
                  pl.BlockSpec(memory_space=pl.ANY)],
        out_specs=pl.BlockSpec((tm, D_MODEL), lambda i: (i, 0)),
        out_shape=jax.ShapeDtypeStruct((TOKENS, D_MODEL), F32),
        scratch_shapes=[pltpu.VMEM((2, 2, tm * ROW_TILE, LANES), F32), pltpu.SemaphoreType.DMA((2,))],
        compiler_params=_params(("arbitrary",)),
        name="moe_combine",
    )(pos, x, gates, g, ys)


def _routing_tables(route, counts):
    cnt = counts[:, 0].astype(jnp.int32)
    tiles = (cnt + TM_MOE - 1) // TM_MOE
    tile_end = jnp.cumsum(tiles)
    tile_start = tile_end - tiles
    n_used = tile_end[-1]
    expert = route[ROUTE_I1:ROUTE_I2 + 1].astype(jnp.int32).T
    rank = route[ROUTE_R1:ROUTE_R2 + 1].astype(jnp.int32).T
    pos = (tile_start * TM_MOE)[expert] + rank
    tile_id = jnp.minimum(jnp.arange(N_TILES_MOE, dtype=jnp.int32), n_used - 1)
    tile_expert = jnp.sum(tile_id[:, None] >= tile_end[None, :], axis=1).astype(jnp.int32)
    last_tile = jnp.where(tiles > 0, tile_end - 1, -1)
    spare = n_used + jnp.arange(N_EXPERTS)
    zero_tiles = jnp.concatenate([last_tile, jnp.where(spare < N_TILES_MOE, spare, -1)]).astype(jnp.int32)
    return pos.reshape(2 * TOKENS), tile_expert, n_used.reshape(1), zero_tiles


def _moe(x, h, route, counts, w1, w3, w2, g_final):
    pos, tile_expert, n_used, last_tile = _routing_tables(route, counts)
    xs = _dispatch(pos, last_tile, h)
    ys = _grouped_ffn(tile_expert, n_used, xs, w1, w3, w2)
    return _combine(pos, x, route[ROUTE_W1:ROUTE_W2 + 1].T, g_final, ys)


def _prep_w_in(w):
    aq, ak, av, gq, gk, gv, gr, glr, c_in, c_b, c_c = jnp.split(w, np.cumsum(SPLIT_SIZES)[:-1].tolist(), axis=1)
    pad = jnp.zeros((D_MODEL, LANES - GLA_RANK), w.dtype)
    return jnp.concatenate([aq, ak, av, gv, gr, gq, gk, c_in, c_b, c_c, glr, pad], axis=1).astype(BF16)


def _prep_router(w):
    wt = w.T
    hi = wt.astype(BF16)
    lo = (wt - hi.astype(F32)).astype(BF16)
    return jnp.stack([jnp.concatenate([hi, lo]), jnp.concatenate([hi, jnp.zeros_like(hi)])])


def kernel(x, w_mix_in, w_mix_out, g_mix, rel_bias, gla_w_gate, gla_b_gate, gla_g_norm, conv_w,
           g_ffn, ffn_w1, ffn_w3, ffn_w2, moe_router, moe_w1, moe_w3, moe_w2, g_final):
    assert DEPTH == 2
    x = x.reshape(TOKENS, D_MODEL)
    flat = lambda w: w.reshape(-1, w.shape[-1])
    gla_casts = {0: moe_w1[0], 1: moe_w3[0]}
    experts = {}
    for layer in range(DEPTH):
        u, *qkvs = _inproj(x, g_mix[layer].reshape(1, D_MODEL), _prep_w_in(w_mix_in[layer]))
        att = _attention(qkvs, rel_bias)
        wg = jnp.pad(gla_w_gate[layer], ((0, LANES - GLA_RANK), (0, 0)))
        go, cv, (cast,) = _gla_conv(u, wg, gla_b_gate[layer].reshape(1, GLA_QK),
                                    gla_g_norm[layer].reshape(1, GLA_DV),
                                    jnp.pad(conv_w[layer], ((0, 8 - CONV_WIDTH), (0, 0))),
                                    side_casts=[flat(gla_casts[layer])])
        experts[layer] = cast.reshape(gla_casts[layer].shape)
        w_out = w_mix_out[layer].astype(BF16)
        g2 = g_ffn[layer].reshape(1, D_MODEL)
        i = layer // 2
        if layer % 2 == 0:
            x, h = _outproj(att, go, cv, x, w_out, g2)
            x, (cast,) = _ffn(x, h, ffn_w1[i].astype(BF16), ffn_w3[i].astype(BF16), ffn_w2[i].astype(BF16),
                              tf=TF_DENSE, side_casts=[flat(moe_w2[i])])
            experts["w2"] = cast.reshape(moe_w2[i].shape)
        else:
            x, h, route, counts = _outproj(att, go, cv, x, w_out, g2, _prep_router(moe_router[i]))
            x = _moe(x, h, route, counts, experts[0], experts[1], experts["w2"], g_final.reshape(1, D_MODEL))
    return x.reshape(BATCH, SEQ, D_MODEL)
```

```python
import functools
import math

import jax
import jax.numpy as jnp
import numpy as np
from jax import lax
from jax.experimental import pallas as pl
from jax.experimental.pallas import tpu as pltpu

F32 = jnp.float32
BF16 = jnp.bfloat16

D_MODEL = 1024
BATCH = 8
SEQ = 2048
TOKENS = BATCH * SEQ
DEPTH = 2
EPS = 1e-6

HEAD_DIM = 64
ATT_HEADS = 4
ATT_WIDTH = ATT_HEADS * HEAD_DIM
DILATED_PATTERNS = ((128, 1), (512, 4), (2048, 16))
ATT_BLOCK = 128
REL_BUCKETS = 32
REL_MAX_DISTANCE = 2048

GLA_HEADS = 4
GLA_DK = 64
GLA_DV = 128
GLA_RANK = 16
GLA_CHUNK = 64
GLA_QK = GLA_HEADS * GLA_DK
GLA_V = GLA_HEADS * GLA_DV

CONV_CH = 256
CONV_WIDTH = 3
MIX_WIDTH = ATT_WIDTH + GLA_V + CONV_CH

SPLIT_SIZES = (ATT_WIDTH, ATT_WIDTH, ATT_WIDTH, GLA_QK, GLA_QK, GLA_V, GLA_V, GLA_RANK,
               CONV_CH, CONV_CH, CONV_CH)

FFN_DENSE = 2816
N_EXPERTS = 8
FFN_EXPERT = 3584

LANES = 128
MXU_WIDTH = 256
VMEM_LIMIT = 56 * 1024 * 1024

QKV_COLS = 3 * ATT_WIDTH
COL_GV, COL_GR, COL_GQ, COL_GK = 0, 512, 1024, 1280
COL_CIN, COL_CB, COL_CC, COL_GLR = 1536, 1792, 2048, 2304
U_COLS = COL_GLR + LANES

NEG_BIG = -1e30

ATT_UNROLL = 15
TM_PROJ = 512
TC_GLA = 512
TM_FFN = 512
TF_DENSE = 1408
FFN_SUB = 256
TM_MOE = 512
TF_MOE = 1792
N_TILES_MOE = 2 * TOKENS // TM_MOE + N_EXPERTS
N_SORTED = N_TILES_MOE * TM_MOE
DISPATCH_CHUNK = 256
TM_COMBINE = 256

ROUTE_I1, ROUTE_I2, ROUTE_W1, ROUTE_W2, ROUTE_R1, ROUTE_R2 = range(6)


def _params(sem):
    return pltpu.CompilerParams(dimension_semantics=sem, vmem_limit_bytes=VMEM_LIMIT)


def _split_bf16(a):
    hi = a.astype(BF16)
    lo = (a - hi.astype(F32)).astype(BF16)
    return hi, lo


def _dot(a, b):
    return jnp.dot(a, b, preferred_element_type=F32)


def _dot3(a, b):
    a_hi, a_lo = _split_bf16(a)
    b_hi, b_lo = _split_bf16(b)
    return _dot(a_hi, b_hi) + _dot(a_lo, b_hi) + _dot(a_hi, b_lo)


def _dot_nt(a, b):
    return lax.dot_general(a, b, (((1,), (1,)), ((), ())), preferred_element_type=F32)


def _dot_tn(a, b):
    return lax.dot_general(a, b, (((0,), (0,)), ((), ())), preferred_element_type=F32)


def _rms(x, g):
    ms = jnp.mean(x * x, axis=-1, keepdims=True)
    return x * lax.rsqrt(ms + EPS) * g


def _sigmoid(x):
    return 1.0 / (1.0 + jnp.exp(-x))


def _inproj_body(x_ref, g_ref, w_ref, o_ref, *rest):
    qkv_refs, qkv_f32 = rest[:-1], rest[-1]
    h = _rms(x_ref[...], g_ref[...]).astype(BF16)
    for c0 in range(0, QKV_COLS, MXU_WIDTH):
        res = _dot(h, w_ref[:, c0:c0 + MXU_WIDTH])
        for t in range(MXU_WIDTH // LANES):
            qkv_f32[c0 // LANES + t] = res[:, t * LANES:(t + 1) * LANES]
    for (_, dilation), ref in zip(DILATED_PATTERNS, qkv_refs):
        for r in range(dilation):
            rows = pl.ds(r, TM_PROJ // dilation, stride=dilation)
            ref[r] = jnp.concatenate([qkv_f32[t, rows, :] for t in range(QKV_COLS // LANES)],
                                     axis=-1).astype(ref.dtype)
    for c0 in range(0, U_COLS, MXU_WIDTH):
        c1 = min(c0 + MXU_WIDTH, U_COLS)
        o_ref[:, c0:c1] = _dot(h, w_ref[:, QKV_COLS + c0:QKV_COLS + c1])


def _subseq_spec(dilation, cols):
    tiles = SEQ // TM_PROJ
    return pl.BlockSpec((None, dilation, TM_PROJ // dilation, cols), lambda i: (i // tiles, 0, i % tiles, 0))


def _inproj(x, g, w):
    qkv_shapes = [jax.ShapeDtypeStruct((BATCH, d, SEQ // d, QKV_COLS), BF16) for _, d in DILATED_PATTERNS]
    return pl.pallas_call(
        _inproj_body,
        grid=(TOKENS // TM_PROJ,),
        in_specs=[
            pl.BlockSpec((TM_PROJ, D_MODEL), lambda i: (i, 0)),
            pl.BlockSpec((1, D_MODEL), lambda i: (0, 0)),
            pl.BlockSpec((D_MODEL, QKV_COLS + U_COLS), lambda i: (0, 0)),
        ],
        out_specs=[pl.BlockSpec((TM_PROJ, U_COLS), lambda i: (i, 0))]
                  + [_subseq_spec(d, QKV_COLS) for _, d in DILATED_PATTERNS],
        out_shape=[jax.ShapeDtypeStruct((TOKENS, U_COLS), F32)] + qkv_shapes,
        scratch_shapes=[pltpu.VMEM((QKV_COLS // LANES, TM_PROJ, LANES), F32)],
        compiler_params=_params(("parallel",)),
        name="inproj",
    )(x, g, w)


def _rel_bucket(dist):
    max_exact = REL_BUCKETS // 2
    d = jnp.maximum(dist, 0)
    log_ratio = jnp.log(jnp.maximum(d, 1).astype(F32) / max_exact) / math.log(REL_MAX_DISTANCE / max_exact)
    large = jnp.minimum(max_exact + (log_ratio * (REL_BUCKETS - max_exact)).astype(jnp.int32), REL_BUCKETS - 1)
    return jnp.where(d < max_exact, d, large)


def _bucket_table(window, dilation):
    span = window // dilation
    qi = jnp.arange(ATT_BLOCK)[:, None]
    kj = jnp.arange(2 * ATT_BLOCK)[None, :]
    sub_dist = qi - kj + ATT_BLOCK
    band = (sub_dist >= 0) & (sub_dist <= span)
    return jnp.where(band, _rel_bucket(sub_dist * dilation), -1).astype(jnp.int32)


def _attn_body(rb_ref, bidx_ref, qkv_ref, o_ref, lse_ref, bias_ref, *, sub_blocks, unroll):
    nblk = SEQ // ATT_BLOCK

    @pl.when(pl.program_id(0) == 0)
    def _():
        bidx = bidx_ref[...]
        in_prev = lax.broadcasted_iota(jnp.int32, bidx.shape, 1) < ATT_BLOCK
        for h in range(ATT_HEADS):
            acc = jnp.full(bidx.shape, NEG_BIG, F32)
            for b in range(REL_BUCKETS):
                acc = jnp.where(bidx == b, rb_ref[b, h], acc)
            bias_ref[0, h] = acc
            bias_ref[1, h] = jnp.where(in_prev, NEG_BIG, acc)
            bias_ref[2, h] = jnp.concatenate([acc[:, ATT_BLOCK:], jnp.full_like(acc[:, ATT_BLOCK:], NEG_BIG)], axis=1)

    def block(n, first):
        if first:
            rows, krows, variant = slice(0, ATT_BLOCK), slice(0, 2 * ATT_BLOCK), 2
        else:
            r0 = pl.multiple_of(n * ATT_BLOCK, ATT_BLOCK)
            rows = pl.ds(r0, ATT_BLOCK)
            krows = pl.ds(r0 - ATT_BLOCK, 2 * ATT_BLOCK)
            if sub_blocks == nblk:
                variant = 0
            elif sub_blocks == 1:
                variant = 1
            else:
                variant = jnp.where(n % sub_blocks == 0, 1, 0)
        q = qkv_ref[rows, 0:ATT_WIDTH]
        kk = qkv_ref[krows, ATT_WIDTH:2 * ATT_WIDTH]
        vv = qkv_ref[krows, 2 * ATT_WIDTH:3 * ATT_WIDTH]
        q = q * jnp.asarray(HEAD_DIM ** -0.5, BF16)
        head_of_lane = lax.broadcasted_iota(jnp.int32, (ATT_BLOCK, ATT_WIDTH), 1) // HEAD_DIM
        ones = jnp.ones((kk.shape[0], LANES), BF16)
        num = den = mx = None
        for h in range(ATT_HEADS):
            mine = head_of_lane == h
            bias = bias_ref[variant, h]
            s = _dot_nt(jnp.where(mine, q, jnp.zeros_like(q)), kk) + bias
            m = jnp.max(s, axis=-1, keepdims=True)
            p = jnp.exp(s - m).astype(BF16)
            num_h = _dot(p, vv)
            den_h = jnp.tile(_dot(p, ones), (1, ATT_WIDTH // LANES))
            m_h = jnp.broadcast_to(m, (ATT_BLOCK, ATT_WIDTH))
            num = num_h if h == 0 else jnp.where(mine, num_h, num)
            den = den_h if h == 0 else jnp.where(mine, den_h, den)
            mx = m_h if h == 0 else jnp.where(mine, m_h, mx)
        o_ref[rows, :] = (num / den).astype(o_ref.dtype)
        lse_ref[rows, :] = mx + jnp.log(den)

    block(0, True)

    def loop_body(n, carry):
        block(n, False)
        return carry
    lax.fori_loop(1, nblk, loop_body, 0, unroll=unroll)


def _attention_pattern(ua, rel_bias, window, dilation):
    L = SEQ // dilation
    shape = (BATCH, dilation, L, ATT_WIDTH)
    qkv_spec = pl.BlockSpec((None, SEQ, QKV_COLS), lambda b: (b, 0, 0))
    out_spec = pl.BlockSpec((None, SEQ, ATT_WIDTH), lambda b: (b, 0, 0))
    o, lse = pl.pallas_call(
        functools.partial(_attn_body, sub_blocks=L // ATT_BLOCK, unroll=ATT_UNROLL),
        grid=(BATCH,),
        in_specs=[
            pl.BlockSpec(memory_space=pltpu.SMEM),
            pl.BlockSpec((ATT_BLOCK, 2 * ATT_BLOCK), lambda b: (0, 0)),
            qkv_spec,
        ],
        out_specs=[out_spec, out_spec],
        out_shape=[jax.ShapeDtypeStruct((BATCH, SEQ, ATT_WIDTH), BF16),
                   jax.ShapeDtypeStruct((BATCH, SEQ, ATT_WIDTH), F32)],
        scratch_shapes=[pltpu.VMEM((3, ATT_HEADS, ATT_BLOCK, 2 * ATT_BLOCK), F32)],
        compiler_params=_params(("arbitrary",)),
        name=f"attn_d{dilation}",
    )(rel_bias, _bucket_table(window, dilation), ua.reshape(BATCH, SEQ, QKV_COLS))
    return o.reshape(shape), lse.reshape(shape)


def _attention(qkvs, rel_bias):
    return [_attention_pattern(ua, rel_bias, window, dilation)
            for ua, (window, dilation) in zip(qkvs, DILATED_PATTERNS)]


def _short_conv(cin_ref, cb_ref, cc_ref, w_ref, o_ref, tail_ref):
    uu = cc_ref[...] * cin_ref[...]
    t = lax.broadcasted_iota(jnp.int32, uu.shape, 0)
    y = uu * w_ref[CONV_WIDTH - 1:CONV_WIDTH, :]
    for shift in range(1, CONV_WIDTH):
        prev = pltpu.roll(uu, shift, axis=0)
        for r in range(shift):
            prev = jnp.where(t == r, tail_ref[8 - shift + r:8 - shift + r + 1, :], prev)
        y = y + prev * w_ref[CONV_WIDTH - 1 - shift:CONV_WIDTH - shift, :]
    o_ref[...] = (cb_ref[...] * y).astype(o_ref.dtype)
    tail_ref[...] = uu[uu.shape[0] - 8:, :]


def _side_cast(srcs, dsts):
    for src, dst in zip(srcs, dsts):
        dst[...] = src[...].astype(dst.dtype)


def _side_cast_specs(arrays, steps, step_of):
    return [pl.BlockSpec((a.shape[0] // steps, a.shape[1]), lambda *ids: (step_of(*ids), 0)) for a in arrays]


def _gla_body(u_ref, wg_ref, bg_ref, gn_ref, ctril_ref, cw_ref, *rest):
    cols = lambda col, width: u_ref.at[:, col:col + width]
    q_ref, k_ref = cols(COL_GQ, GLA_QK), cols(COL_GK, GLA_QK)
    v_ref, gr_ref, glr_ref = cols(COL_GV, GLA_V), cols(COL_GR, GLA_V), cols(COL_GLR, LANES)
    cin_ref, cb_ref, cc_ref = cols(COL_CIN, CONV_CH), cols(COL_CB, CONV_CH), cols(COL_CC, CONV_CH)
    _gla_tile(q_ref, k_ref, v_ref, gr_ref, glr_ref, wg_ref, bg_ref, gn_ref, ctril_ref,
              cin_ref, cb_ref, cc_ref, cw_ref, *rest)


def _gla_tile(q_ref, k_ref, v_ref, gr_ref, glr_ref, wg_ref, bg_ref, gn_ref, ctril_ref,
              cin_ref, cb_ref, cc_ref, cw_ref, *rest):
    n_cast = (len(rest) - 4) // 2
    o_ref, cv_ref = rest[n_cast:n_cast + 2]
    s_ref, tail_ref = rest[-2:]
    _side_cast(rest[:n_cast], rest[n_cast + 2:-2])

    @pl.when(pl.program_id(1) == 0)
    def _():
        s_ref[...] = jnp.zeros_like(s_ref)
        tail_ref[...] = jnp.zeros_like(tail_ref)

    _short_conv(cin_ref, cb_ref, cc_ref, cw_ref, cv_ref, tail_ref)

    C = GLA_CHUNK
    row = lax.broadcasted_iota(jnp.int32, (C, C), 0)
    col = lax.broadcasted_iota(jnp.int32, (C, C), 1)
    tril = row >= col
    n_chunks = TC_GLA // C

    xg = _dot3(glr_ref[...], wg_ref[...]) + bg_ref[...]
    la_all = (jnp.minimum(xg, 0.0) - jnp.log(1.0 + jnp.exp(-jnp.abs(xg)))) * (1.0 / 16.0)

    la_hi, la_lo = _split_bf16(la_all)
    cum_all = _dot(ctril_ref[...], la_hi) + _dot(ctril_ref[...], la_lo)
    totals = jnp.concatenate([cum_all[(c + 1) * C - 1:(c + 1) * C, :] for c in range(n_chunks)]
                             + [jnp.zeros((LANES - n_chunks, GLA_QK), F32)], axis=0)
    decay_cols = jnp.exp(totals.T)

    for c in range(n_chunks):
        rows = slice(c * C, (c + 1) * C)
        cum = cum_all[rows]
        last = cum[C - 1:C, :]
        q = q_ref[rows, :]
        k = k_ref[rows, :]
        qt = (q * jnp.exp(cum) * (GLA_DK ** -0.5)).astype(BF16)
        kt = (k * jnp.exp(-cum)).astype(BF16)
        kl_t = (k * jnp.exp(last - cum)).T.astype(BF16)
        for h in range(GLA_HEADS):
            sl = slice(h * GLA_DK, (h + 1) * GLA_DK)
            vs = slice(h * GLA_DV, (h + 1) * GLA_DV)
            vh = v_ref[rows, vs].astype(BF16)
            state = s_ref[h]
            st_hi, st_lo = _split_bf16(state)
            sc = jnp.where(tril, _dot_nt(qt[:, sl], kt[:, sl]), 0.0).astype(BF16)
            o = _dot(qt[:, sl], st_hi) + _dot(qt[:, sl], st_lo) + _dot(sc, vh)
            decay = jnp.broadcast_to(decay_cols[sl, c:c + 1], state.shape)
            s_ref[h] = decay * state + _dot(kl_t[sl, :], vh)
            g = gr_ref[rows, vs]
            o_ref[rows, vs] = (_rms(o, gn_ref[...]) * (g * _sigmoid(g))).astype(o_ref.dtype)


def _gla_conv(u, wg, bg, gn, conv_w, side_casts=()):
    nj = SEQ // TC_GLA
    row = lambda b, j: b * nj + j
    full = lambda a: pl.BlockSpec(a.shape, lambda b, j: (0, 0))
    cast_specs = _side_cast_specs(side_casts, BATCH * nj, row)
    t = np.arange(TC_GLA)
    same_chunk = (t[:, None] // GLA_CHUNK) == (t[None, :] // GLA_CHUNK)
    chunk_tril = jnp.asarray(same_chunk & (t[:, None] >= t[None, :]), BF16)
    go, cv, *casts = pl.pallas_call(
        _gla_body,
        grid=(BATCH, nj),
        in_specs=[pl.BlockSpec((TC_GLA, U_COLS), lambda b, j: (row(b, j), 0)),
                  full(wg), full(bg), full(gn), full(chunk_tril), full(conv_w)] + cast_specs,
        out_specs=[pl.BlockSpec((TC_GLA, GLA_V), lambda b, j: (row(b, j), 0)),
                   pl.BlockSpec((TC_GLA, CONV_CH), lambda b, j: (row(b, j), 0))] + cast_specs,
        out_shape=[jax.ShapeDtypeStruct((TOKENS, GLA_V), BF16), jax.ShapeDtypeStruct((TOKENS, CONV_CH), BF16)]
                  + [jax.ShapeDtypeStruct(a.shape, BF16) for a in side_casts],
        scratch_shapes=[pltpu.VMEM((GLA_HEADS, GLA_DK, GLA_DV), F32), pltpu.VMEM((8, CONV_CH), F32)],
        compiler_params=_params(("arbitrary", "arbitrary")),
        name="gla_conv",
    )(u, wg, bg, gn, chunk_tril, conv_w, *side_casts)
    return go, cv, casts


def _outproj_body(*refs, route):
    (o1, o4, o16, l1, l4, l16, go_ref, cv_ref, x_ref, w_ref, g_ref) = refs[:11]
    perm_ref = refs[-1]
    if route:
        wr_ref, xo_ref, ho_ref, route_ref, counts_ref, carry_ref = refs[11:-1]
    else:
        xo_ref, ho_ref = refs[11:-1]

    def token_order(ref, slot):
        dilation, rows, _ = ref.shape
        if dilation == 1:
            return ref[0].astype(F32)
        tiles = range(ATT_WIDTH // LANES)
        for r in range(dilation):
            val = ref[r].astype(F32)
            for t in tiles:
                perm_ref[slot, t, pl.ds(r, rows, stride=dilation), :] = val[:, t * LANES:(t + 1) * LANES]
        return jnp.concatenate([perm_ref[slot, t] for t in tiles], axis=-1)

    la, lb, lc = token_order(l1, 0), token_order(l4, 0), token_order(l16, 1)
    oa, ob, oc = token_order(o1, 0), token_order(o4, 2), token_order(o16, 3)
    m = jnp.maximum(jnp.maximum(la, lb), lc)
    ea, eb, ec = jnp.exp(la - m), jnp.exp(lb - m), jnp.exp(lc - m)
    att = (ea * oa + eb * ob + ec * oc) / (ea + eb + ec)
    y = (x_ref[...]
         + _dot(att.astype(BF16), w_ref[0:ATT_WIDTH, :])
         + _dot(go_ref[...], w_ref[ATT_WIDTH:ATT_WIDTH + GLA_V, :])
         + _dot(cv_ref[...], w_ref[ATT_WIDTH + GLA_V:MIX_WIDTH, :]))
    xo_ref[...] = y
    hf = _rms(y, g_ref[...])
    ho_ref[...] = hf.astype(ho_ref.dtype)
    if route:
        @pl.when(pl.program_id(0) == 0)
        def _():
            carry_ref[...] = jnp.zeros_like(carry_ref)

        tm = hf.shape[0]
        ne = N_EXPERTS
        hf_hi, hf_lo = _split_bf16(hf)
        part = _dot_nt(wr_ref[0], hf_hi) + _dot_nt(wr_ref[1], hf_lo)
        logits = part[0:ne] + part[ne:2 * ne]
        eidx = lax.broadcasted_iota(jnp.int32, logits.shape, 0).astype(F32)
        v1 = jnp.max(logits, axis=0, keepdims=True)
        i1 = jnp.min(jnp.where(logits == v1, eidx, float(ne)), axis=0, keepdims=True)
        lg2 = jnp.where(eidx == i1, -jnp.inf, logits)
        v2 = jnp.max(lg2, axis=0, keepdims=True)
        i2 = jnp.min(jnp.where(lg2 == v2, eidx, float(ne)), axis=0, keepdims=True)
        e2 = jnp.exp(v2 - v1)
        w1 = 1.0 / (1.0 + e2)
        w2 = e2 * w1
        sel1 = eidx == i1
        sel2 = eidx == i2
        onehot = jnp.where(sel1, 1.0, jnp.where(sel2, 1.0, 0.0))
        tri = (lax.broadcasted_iota(jnp.int32, (tm, tm), 0) <= lax.broadcasted_iota(jnp.int32, (tm, tm), 1))
        onehot16 = jnp.concatenate([onehot, jnp.zeros_like(onehot)], axis=0).astype(BF16)
        csum = _dot(onehot16, jnp.where(tri, 1.0, 0.0).astype(BF16))[0:ne]
        carry = carry_ref[:, 0:1]
        rank = csum - onehot + carry
        r1 = jnp.sum(jnp.where(sel1, rank, 0.0), axis=0, keepdims=True)
        r2 = jnp.sum(jnp.where(sel2, rank, 0.0), axis=0, keepdims=True)
        total = jnp.broadcast_to(carry + csum[:, tm - 1:tm], carry_ref.shape)
        carry_ref[...] = total
        counts_ref[...] = total
        rows = {ROUTE_I1: i1, ROUTE_I2: i2, ROUTE_W1: w1, ROUTE_W2: w2, ROUTE_R1: r1, ROUTE_R2: r2}
        zero = jnp.zeros_like(i1)
        route_ref[...] = jnp.concatenate([rows.get(r, zero) for r in range(8)], axis=0)


def _outproj(att, go, cv, x, w, g, w_router=None):
    route = w_router is not None
    tm = TM_PROJ
    tile = lambda cols: pl.BlockSpec((tm, cols), lambda i: (i, 0))
    full = lambda a: pl.BlockSpec(a.shape, lambda i: (0, 0))
    (o1, l1), (o4, l4), (o16, l16) = att
    args = [o1, o4, o16, l1, l4, l16, go, cv, x, w, g]
    att_specs = [_subseq_spec(d, ATT_WIDTH) for _, d in DILATED_PATTERNS]
    in_specs = att_specs * 2 + [tile(GLA_V), tile(CONV_CH), tile(D_MODEL), full(w), full(g)]
    out_specs = [tile(D_MODEL), tile(D_MODEL)]
    out_shape = [jax.ShapeDtypeStruct((TOKENS, D_MODEL), F32),
                 jax.ShapeDtypeStruct((TOKENS, D_MODEL), F32 if route else BF16)]
    scratch = []
    if route:
        args.append(w_router)
        in_specs.append(pl.BlockSpec(w_router.shape, lambda i: (0, 0, 0)))
        out_specs += [pl.BlockSpec((8, tm), lambda i: (0, i)), pl.BlockSpec((N_EXPERTS, LANES), lambda i: (0, 0))]
        out_shape += [jax.ShapeDtypeStruct((8, TOKENS), F32), jax.ShapeDtypeStruct((N_EXPERTS, LANES), F32)]
        scratch = [pltpu.VMEM((N_EXPERTS, LANES), F32)]
    scratch.append(pltpu.VMEM((4, ATT_WIDTH // LANES, tm, LANES), F32))
    return pl.pallas_call(
        functools.partial(_outproj_body, route=route),
        grid=(TOKENS // tm,),
        in_specs=in_specs,
        out_specs=out_specs,
        out_shape=out_shape,
        scratch_shapes=scratch,
        compiler_params=_params(("arbitrary",) if route else ("parallel",)),
        name="outproj_route" if route else "outproj",
    )(*args)


def _swiglu_accumulate(h, w1_ref, w3_ref, w2_ref, acc_ref, tf):
    for c0 in range(0, tf, FFN_SUB):
        c1 = min(c0 + FFN_SUB, tf)
        a = _dot(h, w1_ref[:, c0:c1])
        b = _dot(h, w3_ref[:, c0:c1])
        act = a * _sigmoid(a) * b
        acc_ref[...] += _dot(act.astype(BF16), w2_ref[c0:c1, :])


def _ffn_body(x_ref, h_ref, w1_ref, w3_ref, w2_ref, *rest, tf):
    n_cast = (len(rest) - 1) // 2
    o_ref = rest[n_cast]

    @pl.when(pl.program_id(1) == 0)
    def _():
        o_ref[...] = x_ref[...]

    _swiglu_accumulate(h_ref[...], w1_ref, w3_ref, w2_ref, o_ref, tf)
    _side_cast(rest[:n_cast], rest[n_cast + 1:])


def _ffn(x, h, w1, w3, w2, *, tf, side_casts=()):
    f = w1.shape[1]
    tm = TM_FFN
    nj = f // tf
    tile = lambda cols: pl.BlockSpec((tm, cols), lambda i, j: (i, 0))
    cast_specs = _side_cast_specs(side_casts, (TOKENS // tm) * nj, lambda i, j: i * nj + j)
    res = pl.pallas_call(
        functools.partial(_ffn_body, tf=tf),
        grid=(TOKENS // tm, nj),
        in_specs=[tile(D_MODEL), tile(D_MODEL),
                  pl.BlockSpec((D_MODEL, tf), lambda i, j: (0, j)),
                  pl.BlockSpec((D_MODEL, tf), lambda i, j: (0, j)),
                  pl.BlockSpec((tf, D_MODEL), lambda i, j: (j, 0))] + cast_specs,
        out_specs=[tile(D_MODEL)] + cast_specs,
        out_shape=[jax.ShapeDtypeStruct((TOKENS, D_MODEL), F32)]
                  + [jax.ShapeDtypeStruct(a.shape, BF16) for a in side_casts],
        compiler_params=_params(("parallel", "arbitrary")),
        name="dense_ffn",
    )(x, h, w1, w3, w2, *side_casts)
    return res[0], res[1:]


ROW_TILE = D_MODEL // LANES


def _to_row_tiled(dst_ref, lead, val):
    rows = val.shape[0]
    for s in range(ROW_TILE):
        dst_ref[(*lead, pl.ds(s, rows, stride=ROW_TILE), slice(None))] = val[:, s * LANES:(s + 1) * LANES]


def _from_row_tiled(src_ref, lead, rows):
    return jnp.concatenate([src_ref[(*lead, pl.ds(s, rows, stride=ROW_TILE), slice(None))]
                            for s in range(ROW_TILE)], axis=-1)


def _row_tile(idx):
    return pl.ds(pl.multiple_of(idx * ROW_TILE, ROW_TILE), ROW_TILE)


def _dispatch_body(pos_ref, last_tile_ref, h_ref, xs_hbm, stage_ref, zero_ref, sem, zero_sem):
    i = pl.program_id(0)
    n = pl.num_programs(0)
    tm = DISPATCH_CHUNK
    slot = i % 2

    def drain(s):
        for _ in range(2):
            pltpu.make_async_copy(stage_ref.at[s], xs_hbm.at[pl.ds(0, tm * ROW_TILE)], sem.at[s]).wait()

    @pl.when(i == 0)
    def _():
        zero_ref[...] = jnp.zeros_like(zero_ref)

        def zero_copy(e):
            start = pl.multiple_of(last_tile_ref[e] * (TM_MOE * ROW_TILE), TM_MOE * ROW_TILE)
            return pltpu.make_async_copy(zero_ref, xs_hbm.at[pl.ds(start, TM_MOE * ROW_TILE)], zero_sem)

        for e in range(2 * N_EXPERTS):
            @pl.when(last_tile_ref[e] >= 0)
            def _():
                zero_copy(e).start()
        for e in range(2 * N_EXPERTS):
            @pl.when(last_tile_ref[e] >= 0)
            def _():
                zero_copy(e).wait()

    @pl.when(i >= 2)
    def _():
        drain(slot)

    _to_row_tiled(stage_ref, (slot,), h_ref[...])

    def body(t, carry):
        for k in range(2):
            dst = pos_ref[2 * (i * tm + t) + k]
            pltpu.make_async_copy(stage_ref.at[slot, _row_tile(t)], xs_hbm.at[_row_tile(dst)],
                                  sem.at[slot]).start(priority=k)
        return carry
    lax.fori_loop(0, tm, body, 0, unroll=8)

    @pl.when(i == n - 1)
    def _():
        drain(1 - slot)
        drain(slot)


def _dispatch(pos, last_tile, h):
    tm = DISPATCH_CHUNK
    return pl.pallas_call(
        _dispatch_body,
        grid=(TOKENS // tm,),
        in_specs=[pl.BlockSpec(memory_space=pltpu.SMEM),
                  pl.BlockSpec(memory_space=pltpu.SMEM),
                  pl.BlockSpec((tm, D_MODEL), lambda i: (i, 0))],
        out_specs=pl.BlockSpec(memory_space=pl.ANY),
        out_shape=jax.ShapeDtypeStruct((N_SORTED * ROW_TILE, LANES), F32),
        scratch_shapes=[pltpu.VMEM((2, tm * ROW_TILE, LANES), F32),
                        pltpu.VMEM((TM_MOE * ROW_TILE, LANES), F32),
                        pltpu.SemaphoreType.DMA((2,)), pltpu.SemaphoreType.DMA(())],
        compiler_params=_params(("arbitrary",)),
        name="moe_dispatch",
    )(pos, last_tile, h)


def _gffn_body(te_ref, nu_ref, xs_ref, w1_ref, w3_ref, w2_ref, o_ref, hb_ref, acc_ref):
    del te_ref
    i = pl.program_id(0)
    j = pl.program_id(1)

    used = i < nu_ref[0]

    @pl.when(j == 0)
    def _():
        acc_ref[...] = jnp.zeros_like(acc_ref)

    @pl.when(used & (j == 0))
    def _():
        hb_ref[...] = _from_row_tiled(xs_ref, (), TM_MOE).astype(BF16)

    @pl.when(used)
    def _():
        _swiglu_accumulate(hb_ref[...], w1_ref, w3_ref, w2_ref, acc_ref, TF_MOE)

    @pl.when(j == pl.num_programs(1) - 1)
    def _():
        _to_row_tiled(o_ref, (), acc_ref[...])


def _grouped_ffn(tile_expert, n_used, xs, w1, w3, w2):
    nj = FFN_EXPERT // TF_MOE
    col = lambda i, j, nu: jnp.where(i < nu[0], j, nj - 1)
    grid_spec = pltpu.PrefetchScalarGridSpec(
        num_scalar_prefetch=2,
        grid=(N_TILES_MOE, nj),
        in_specs=[
            pl.BlockSpec((TM_MOE * ROW_TILE, LANES), lambda i, j, te, nu: (jnp.minimum(i, nu[0] - 1), 0)),
            pl.BlockSpec((None, D_MODEL, TF_MOE), lambda i, j, te, nu: (te[i], 0, col(i, j, nu))),
            pl.BlockSpec((None, D_MODEL, TF_MOE), lambda i, j, te, nu: (te[i], 0, col(i, j, nu))),
            pl.BlockSpec((None, TF_MOE, D_MODEL), lambda i, j, te, nu: (te[i], col(i, j, nu), 0)),
        ],
        out_specs=pl.BlockSpec((TM_MOE * ROW_TILE, LANES), lambda i, j, te, nu: (i, 0)),
        scratch_shapes=[pltpu.VMEM((TM_MOE, D_MODEL), BF16), pltpu.VMEM((TM_MOE, D_MODEL), F32)],
    )
    return pl.pallas_call(
        _gffn_body,
        grid_spec=grid_spec,
        out_shape=jax.ShapeDtypeStruct((N_SORTED * ROW_TILE, LANES), F32),
        compiler_params=_params(("arbitrary", "arbitrary")),
        name="moe_ffn",
    )(tile_expert, n_used, xs, w1, w3, w2)


def _combine_body(pos_ref, x_ref, gate_ref, g_ref, ys_hbm, o_ref, buf_ref, sem):
    i = pl.program_id(0)
    n = pl.num_programs(0)
    tm = TM_COMBINE

    def issue(tile, slot):
        def body(t, carry):
            for k in range(2):
                src = pos_ref[2 * (tile * tm + t) + k]
                pltpu.make_async_copy(ys_hbm.at[_row_tile(src)], buf_ref.at[slot, k, _row_tile(t)],
                                      sem.at[slot]).start(priority=k)
            return carry
        lax.fori_loop(0, tm, body, 0, unroll=8)

    @pl.when(i == 0)
    def _():
        issue(0, 0)

    @pl.when(i + 1 < n)
    def _():
        issue(i + 1, (i + 1) % 2)

    slot = i % 2
    for k in range(2):
        pltpu.make_async_copy(ys_hbm.at[pl.ds(0, tm * ROW_TILE)], buf_ref.at[slot, k], sem.at[slot]).wait()
    w1 = gate_ref[:, 0:1]
    w2 = gate_ref[:, 1:2]
    y = x_ref[...] + w1 * _from_row_tiled(buf_ref, (slot, 0), tm) + w2 * _from_row_tiled(buf_ref, (slot, 1), tm)
    o_ref[...] = _rms(y, g_ref[...])


def _combine(pos, x, gates, g, ys):
    tm = TM_COMBINE
    return pl.pallas_call(
        _combine_body,
        grid=(TOKENS // tm,),
        in_specs=[pl.BlockSpec(memory_space=pltpu.SMEM),
                  pl.BlockSpec((tm, D_MODEL), lambda i: (i, 0)),
                  pl.BlockSpec((tm, 2), lambda i: (i, 0)),
                  pl.BlockSpec((1, D_MODEL), lambda i: (0, 0)),
                  pl.BlockSpec(memory_space=pl.ANY)],
        out_specs=pl.BlockSpec((tm, D_MODEL), lambda i: (i, 0)),
        out_shape=jax.ShapeDtypeStruct((TOKENS, D_MODEL), F32),
        scratch_shapes=[pltpu.VMEM((2, 2, tm * ROW_TILE, LANES), F32), pltpu.SemaphoreType.DMA((2,))],
        compiler_params=_params(("arbitrary",)),
        name="moe_combine",
    )(pos, x, gates, g, ys)


def _routing_tables(route, counts):
    cnt = counts[:, 0].astype(jnp.int32)
    tiles = (cnt + TM_MOE - 1) // TM_MOE
    tile_end = jnp.cumsum(tiles)
    tile_start = tile_end - tiles
    n_used = tile_end[-1]
    expert = route[ROUTE_I1:ROUTE_I2 + 1].astype(jnp.int32)
    rank = route[ROUTE_R1:ROUTE_R2 + 1].astype(jnp.int32)
    group_start = jnp.sum(jnp.where(expert[..., None] == jnp.arange(N_EXPERTS), tile_start * TM_MOE, 0), axis=-1)
    pos = (group_start + rank).T
    tile_id = jnp.minimum(jnp.arange(N_TILES_MOE, dtype=jnp.int32), n_used - 1)
    tile_expert = jnp.sum(tile_id[:, None] >= tile_end[None, :], axis=1).astype(jnp.int32)
    last_tile = jnp.where(tiles > 0, tile_end - 1, -1)
    spare = n_used + jnp.arange(N_EXPERTS)
    zero_tiles = jnp.concatenate([last_tile, jnp.where(spare < N_TILES_MOE, spare, -1)]).astype(jnp.int32)
    return pos.reshape(2 * TOKENS), tile_expert, n_used.reshape(1), zero_tiles


def _moe(x, h, route, counts, w1, w3, w2, g_final):
    pos, tile_expert, n_used, last_tile = _routing_tables(route, counts)
    xs = _dispatch(pos, last_tile, h)
    ys = _grouped_ffn(tile_expert, n_used, xs, w1, w3, w2)
    return _combine(pos, x, route[ROUTE_W1:ROUTE_W2 + 1].T, g_final, ys)


def _prep_w_in(w):
    aq, ak, av, gq, gk, gv, gr, glr, c_in, c_b, c_c = jnp.split(w, np.cumsum(SPLIT_SIZES)[:-1].tolist(), axis=1)
    pad = jnp.zeros((D_MODEL, LANES - GLA_RANK), w.dtype)
    return jnp.concatenate([aq, ak, av, gv, gr, gq, gk, c_in, c_b, c_c, glr, pad], axis=1).astype(BF16)


def _prep_router(w):
    wt = w.T
    hi = wt.astype(BF16)
    lo = (wt - hi.astype(F32)).astype(BF16)
    return jnp.stack([jnp.concatenate([hi, lo]), jnp.concatenate([hi, jnp.zeros_like(hi)])])


def kernel(x, w_mix_in, w_mix_out, g_mix, rel_bias, gla_w_gate, gla_b_gate, gla_g_norm, conv_w,
           g_ffn, ffn_w1, ffn_w3, ffn_w2, moe_router, moe_w1, moe_w3, moe_w2, g_final):
    assert DEPTH == 2
    x = x.reshape(TOKENS, D_MODEL)
    flat = lambda w: w.reshape(-1, w.shape[-1])
    gla_casts = {0: moe_w1[0], 1: moe_w3[0]}
    experts = {}
    for layer in range(DEPTH):
        u, *qkvs = _inproj(x, g_mix[layer].reshape(1, D_MODEL), _prep_w_in(w_mix_in[layer]))
        att = _attention(qkvs, rel_bias)
        wg = jnp.pad(gla_w_gate[layer], ((0, LANES - GLA_RANK), (0, 0)))
        go, cv, (cast,) = _gla_conv(u, wg, gla_b_gate[layer].reshape(1, GLA_QK),
                                    gla_g_norm[layer].reshape(1, GLA_DV),
                                    jnp.pad(conv_w[layer], ((0, 8 - CONV_WIDTH), (0, 0))),
                                    side_casts=[flat(gla_casts[layer])])
        experts[layer] = cast.reshape(gla_casts[layer].shape)
        w_out = w_mix_out[layer].astype(BF16)
        g2 = g_ffn[layer].reshape(1, D_MODEL)
        i = layer // 2
        if layer % 2 == 0:
            x, h = _outproj(att, go, cv, x, w_out, g2)
            x, (cast,) = _ffn(x, h, ffn_w1[i].astype(BF16), ffn_w3[i].astype(BF16), ffn_w2[i].astype(BF16),
                              tf=TF_DENSE, side_casts=[flat(moe_w2[i])])
            experts["w2"] = cast.reshape(moe_w2[i].shape)
        else:
            x, h, route, counts = _outproj(att, go, cv, x, w_out, g2, _prep_router(moe_router[i]))
            x = _moe(x, h, route, counts, experts[0], experts[1], experts["w2"], g_final.reshape(1, D_MODEL))
    return x.reshape(BATCH, SEQ, D_MODEL)
```

```python
import functools
import math

import jax
import jax.numpy as jnp
import numpy as np
from jax import lax
from jax.experimental import pallas as pl
from jax.experimental.pallas import tpu as pltpu

F32 = jnp.float32
BF16 = jnp.bfloat16

D_MODEL = 1024
BATCH = 8
SEQ = 2048
TOKENS = BATCH * SEQ
DEPTH = 2
EPS = 1e-6

HEAD_DIM = 64
ATT_HEADS = 4
ATT_WIDTH = ATT_HEADS * HEAD_DIM
DILATED_PATTERNS = ((128, 1), (512, 4), (2048, 16))
ATT_BLOCK = 128
REL_BUCKETS = 32
REL_MAX_DISTANCE = 2048

GLA_HEADS = 4
GLA_DK = 64
GLA_DV = 128
GLA_RANK = 16
GLA_CHUNK = 64
GLA_QK = GLA_HEADS * GLA_DK
GLA_V = GLA_HEADS * GLA_DV

CONV_CH = 256
CONV_WIDTH = 3
MIX_WIDTH = ATT_WIDTH + GLA_V + CONV_CH

SPLIT_SIZES = (ATT_WIDTH, ATT_WIDTH, ATT_WIDTH, GLA_QK, GLA_QK, GLA_V, GLA_V, GLA_RANK,
               CONV_CH, CONV_CH, CONV_CH)

FFN_DENSE = 2816
N_EXPERTS = 8
FFN_EXPERT = 3584

LANES = 128
MXU_WIDTH = 256
VMEM_LIMIT = 56 * 1024 * 1024

QKV_COLS = 3 * ATT_WIDTH
COL_GV, COL_GR, COL_GQ, COL_GK = 0, 512, 1024, 1280
COL_CIN, COL_CB, COL_CC, COL_GLR = 1536, 1792, 2048, 2304
U_COLS = COL_GLR + LANES

NEG_BIG = -1e30

ATT_UNROLL = 15
TM_PROJ = 512
TC_GLA = 512
TM_FFN = 512
TF_DENSE = 1408
FFN_SUB = 256
TM_MOE = 512
TF_MOE = 1792
N_TILES_MOE = 2 * TOKENS // TM_MOE + N_EXPERTS
N_SORTED = N_TILES_MOE * TM_MOE
DISPATCH_CHUNK = 512
TM_COMBINE = 512

ROUTE_I1, ROUTE_I2, ROUTE_W1, ROUTE_W2, ROUTE_R1, ROUTE_R2 = range(6)


def _params(sem):
    return pltpu.CompilerParams(dimension_semantics=sem, vmem_limit_bytes=VMEM_LIMIT)


def _split_bf16(a):
    hi = a.astype(BF16)
    lo = (a - hi.astype(F32)).astype(BF16)
    return hi, lo


def _dot(a, b):
    return jnp.dot(a, b, preferred_element_type=F32)


def _dot3(a, b):
    a_hi, a_lo = _split_bf16(a)
    b_hi, b_lo = _split_bf16(b)
    return _dot(a_hi, b_hi) + _dot(a_lo, b_hi) + _dot(a_hi, b_lo)


def _dot_nt(a, b):
    return lax.dot_general(a, b, (((1,), (1,)), ((), ())), preferred_element_type=F32)


def _dot_tn(a, b):
    return lax.dot_general(a, b, (((0,), (0,)), ((), ())), preferred_element_type=F32)


def _rms(x, g):
    ms = jnp.mean(x * x, axis=-1, keepdims=True)
    return x * lax.rsqrt(ms + EPS) * g


def _sigmoid(x):
    return 1.0 / (1.0 + jnp.exp(-x))


def _side_cast(srcs, dsts):
    for src, dst in zip(srcs, dsts):
        dst[...] = src[...].astype(dst.dtype)


def _side_cast_specs(arrays, steps, step_of):
    return [pl.BlockSpec((a.shape[0] // steps, a.shape[1]), lambda *ids: (step_of(*ids), 0)) for a in arrays]


def _inproj_body(x_ref, g_ref, w_ref, o_ref, *rest):
    qkv_refs, qkv_f32 = rest[:-1], rest[-1]
    h = _rms(x_ref[...], g_ref[...]).astype(BF16)
    for c0 in range(0, QKV_COLS, MXU_WIDTH):
        res = _dot(h, w_ref[:, c0:c0 + MXU_WIDTH])
        for t in range(MXU_WIDTH // LANES):
            qkv_f32[c0 // LANES + t] = res[:, t * LANES:(t + 1) * LANES]
    for (_, dilation), ref in zip(DILATED_PATTERNS, qkv_refs):
        for r in range(dilation):
            rows = pl.ds(r, TM_PROJ // dilation, stride=dilation)
            ref[r] = jnp.concatenate([qkv_f32[t, rows, :] for t in range(QKV_COLS // LANES)],
                                     axis=-1).astype(ref.dtype)
    for c0 in range(0, U_COLS, MXU_WIDTH):
        c1 = min(c0 + MXU_WIDTH, U_COLS)
        o_ref[:, c0:c1] = _dot(h, w_ref[:, QKV_COLS + c0:QKV_COLS + c1])


def _subseq_spec(dilation, cols):
    tiles = SEQ // TM_PROJ
    return pl.BlockSpec((None, dilation, TM_PROJ // dilation, cols), lambda i: (i // tiles, 0, i % tiles, 0))


def _inproj(x, g, w):
    qkv_shapes = [jax.ShapeDtypeStruct((BATCH, d, SEQ // d, QKV_COLS), BF16) for _, d in DILATED_PATTERNS]
    return pl.pallas_call(
        _inproj_body,
        grid=(TOKENS // TM_PROJ,),
        in_specs=[
            pl.BlockSpec((TM_PROJ, D_MODEL), lambda i: (i, 0)),
            pl.BlockSpec((1, D_MODEL), lambda i: (0, 0)),
            pl.BlockSpec((D_MODEL, QKV_COLS + U_COLS), lambda i: (0, 0)),
        ],
        out_specs=[pl.BlockSpec((TM_PROJ, U_COLS), lambda i: (i, 0))]
                  + [_subseq_spec(d, QKV_COLS) for _, d in DILATED_PATTERNS],
        out_shape=[jax.ShapeDtypeStruct((TOKENS, U_COLS), F32)] + qkv_shapes,
        scratch_shapes=[pltpu.VMEM((QKV_COLS // LANES, TM_PROJ, LANES), F32)],
        compiler_params=_params(("parallel",)),
        name="inproj",
    )(x, g, w)


def _rel_bucket(dist):
    max_exact = REL_BUCKETS // 2
    d = jnp.maximum(dist, 0)
    log_ratio = jnp.log(jnp.maximum(d, 1).astype(F32) / max_exact) / math.log(REL_MAX_DISTANCE / max_exact)
    large = jnp.minimum(max_exact + (log_ratio * (REL_BUCKETS - max_exact)).astype(jnp.int32), REL_BUCKETS - 1)
    return jnp.where(d < max_exact, d, large)


def _bucket_table(window, dilation):
    span = window // dilation
    qi = jnp.arange(ATT_BLOCK)[:, None]
    kj = jnp.arange(2 * ATT_BLOCK)[None, :]
    sub_dist = qi - kj + ATT_BLOCK
    band = (sub_dist >= 0) & (sub_dist <= span)
    return jnp.where(band, _rel_bucket(sub_dist * dilation), -1).astype(jnp.int32)


def _attn_body(rb_ref, bidx_ref, qkv_ref, *rest, sub_blocks, unroll):
    n_cast = (len(rest) - 3) // 2
    o_ref, lse_ref = rest[n_cast:n_cast + 2]
    bias_ref = rest[-1]
    _side_cast(rest[:n_cast], rest[n_cast + 2:-1])
    nblk = SEQ // ATT_BLOCK

    @pl.when(pl.program_id(0) == 0)
    def _():
        bidx = bidx_ref[...]
        in_prev = lax.broadcasted_iota(jnp.int32, bidx.shape, 1) < ATT_BLOCK
        for h in range(ATT_HEADS):
            acc = jnp.full(bidx.shape, NEG_BIG, F32)
            for b in range(REL_BUCKETS):
                acc = jnp.where(bidx == b, rb_ref[b, h], acc)
            bias_ref[0, h] = acc
            bias_ref[1, h] = jnp.where(in_prev, NEG_BIG, acc)
            bias_ref[2, h] = jnp.concatenate([acc[:, ATT_BLOCK:], jnp.full_like(acc[:, ATT_BLOCK:], NEG_BIG)], axis=1)

    def block(n, first):
        if first:
            rows, krows, variant = slice(0, ATT_BLOCK), slice(0, 2 * ATT_BLOCK), 2
        else:
            r0 = pl.multiple_of(n * ATT_BLOCK, ATT_BLOCK)
            rows = pl.ds(r0, ATT_BLOCK)
            krows = pl.ds(r0 - ATT_BLOCK, 2 * ATT_BLOCK)
            if sub_blocks == nblk:
                variant = 0
            elif sub_blocks == 1:
                variant = 1
            else:
                variant = jnp.where(n % sub_blocks == 0, 1, 0)
        q = qkv_ref[rows, 0:ATT_WIDTH]
        kk = qkv_ref[krows, ATT_WIDTH:2 * ATT_WIDTH]
        vv = qkv_ref[krows, 2 * ATT_WIDTH:3 * ATT_WIDTH]
        q = q * jnp.asarray(HEAD_DIM ** -0.5, BF16)
        head_of_lane = lax.broadcasted_iota(jnp.int32, (ATT_BLOCK, ATT_WIDTH), 1) // HEAD_DIM
        ones = jnp.ones((kk.shape[0], LANES), BF16)
        num = den = mx = None
        for h in range(ATT_HEADS):
            mine = head_of_lane == h
            bias = bias_ref[variant, h]
            s = _dot_nt(jnp.where(mine, q, jnp.zeros_like(q)), kk) + bias
            m = jnp.max(s, axis=-1, keepdims=True)
            p = jnp.exp(s - m).astype(BF16)
            num_h = _dot(p, vv)
            den_h = jnp.tile(_dot(p, ones), (1, ATT_WIDTH // LANES))
            m_h = jnp.broadcast_to(m, (ATT_BLOCK, ATT_WIDTH))
            num = num_h if h == 0 else jnp.where(mine, num_h, num)
            den = den_h if h == 0 else jnp.where(mine, den_h, den)
            mx = m_h if h == 0 else jnp.where(mine, m_h, mx)
        o_ref[rows, :] = (num / den).astype(o_ref.dtype)
        lse_ref[rows, :] = mx + jnp.log(den)

    block(0, True)

    def loop_body(n, carry):
        block(n, False)
        return carry
    lax.fori_loop(1, nblk, loop_body, 0, unroll=unroll)


def _attention_pattern(ua, rel_bias, window, dilation, side_casts=()):
    L = SEQ // dilation
    shape = (BATCH, dilation, L, ATT_WIDTH)
    qkv_spec = pl.BlockSpec((None, SEQ, QKV_COLS), lambda b: (b, 0, 0))
    out_spec = pl.BlockSpec((None, SEQ, ATT_WIDTH), lambda b: (b, 0, 0))
    cast_specs = _side_cast_specs(side_casts, BATCH, lambda b: b)
    o, lse, *casts = pl.pallas_call(
        functools.partial(_attn_body, sub_blocks=L // ATT_BLOCK, unroll=ATT_UNROLL),
        grid=(BATCH,),
        in_specs=[
            pl.BlockSpec(memory_space=pltpu.SMEM),
            pl.BlockSpec((ATT_BLOCK, 2 * ATT_BLOCK), lambda b: (0, 0)),
            qkv_spec,
        ] + cast_specs,
        out_specs=[out_spec, out_spec] + cast_specs,
        out_shape=[jax.ShapeDtypeStruct((BATCH, SEQ, ATT_WIDTH), BF16),
                   jax.ShapeDtypeStruct((BATCH, SEQ, ATT_WIDTH), F32)]
                  + [jax.ShapeDtypeStruct(a.shape, BF16) for a in side_casts],
        scratch_shapes=[pltpu.VMEM((3, ATT_HEADS, ATT_BLOCK, 2 * ATT_BLOCK), F32)],
        compiler_params=_params(("arbitrary",)),
        name=f"attn_d{dilation}",
    )(rel_bias, _bucket_table(window, dilation), ua.reshape(BATCH, SEQ, QKV_COLS), *side_casts)
    return (o.reshape(shape), lse.reshape(shape)), casts


def _attention(qkvs, rel_bias, side_casts=()):
    side_casts = list(side_casts) + [None] * (len(DILATED_PATTERNS) - len(side_casts))
    res = [_attention_pattern(ua, rel_bias, window, dilation, [] if w is None else [w])
           for ua, (window, dilation), w in zip(qkvs, DILATED_PATTERNS, side_casts)]
    return [r[0] for r in res], [c for r in res for c in r[1]]


def _short_conv(cin_ref, cb_ref, cc_ref, w_ref, o_ref, tail_ref):
    uu = cc_ref[...] * cin_ref[...]
    t = lax.broadcasted_iota(jnp.int32, uu.shape, 0)
    y = uu * w_ref[CONV_WIDTH - 1:CONV_WIDTH, :]
    for shift in range(1, CONV_WIDTH):
        prev = pltpu.roll(uu, shift, axis=0)
        for r in range(shift):
            prev = jnp.where(t == r, tail_ref[8 - shift + r:8 - shift + r + 1, :], prev)
        y = y + prev * w_ref[CONV_WIDTH - 1 - shift:CONV_WIDTH - shift, :]
    o_ref[...] = (cb_ref[...] * y).astype(o_ref.dtype)
    tail_ref[...] = uu[uu.shape[0] - 8:, :]


def _gla_body(u_ref, wg_ref, bg_ref, gn_ref, ctril_ref, cw_ref, *rest):
    cols = lambda col, width: u_ref.at[:, col:col + width]
    q_ref, k_ref = cols(COL_GQ, GLA_QK), cols(COL_GK, GLA_QK)
    v_ref, gr_ref, glr_ref = cols(COL_GV, GLA_V), cols(COL_GR, GLA_V), cols(COL_GLR, LANES)
    cin_ref, cb_ref, cc_ref = cols(COL_CIN, CONV_CH), cols(COL_CB, CONV_CH), cols(COL_CC, CONV_CH)
    _gla_tile(q_ref, k_ref, v_ref, gr_ref, glr_ref, wg_ref, bg_ref, gn_ref, ctril_ref,
              cin_ref, cb_ref, cc_ref, cw_ref, *rest)


def _gla_tile(q_ref, k_ref, v_ref, gr_ref, glr_ref, wg_ref, bg_ref, gn_ref, ctril_ref,
              cin_ref, cb_ref, cc_ref, cw_ref, *rest):
    n_cast = (len(rest) - 4) // 2
    o_ref, cv_ref = rest[n_cast:n_cast + 2]
    s_ref, tail_ref = rest[-2:]
    _side_cast(rest[:n_cast], rest[n_cast + 2:-2])

    @pl.when(pl.program_id(1) == 0)
    def _():
        s_ref[...] = jnp.zeros_like(s_ref)
        tail_ref[...] = jnp.zeros_like(tail_ref)

    _short_conv(cin_ref, cb_ref, cc_ref, cw_ref, cv_ref, tail_ref)

    C = GLA_CHUNK
    row = lax.broadcasted_iota(jnp.int32, (C, C), 0)
    col = lax.broadcasted_iota(jnp.int32, (C, C), 1)
    tril = row >= col
    n_chunks = TC_GLA // C

    xg = _dot3(glr_ref[...], wg_ref[...]) + bg_ref[...]
    la_all = (jnp.minimum(xg, 0.0) - jnp.log(1.0 + jnp.exp(-jnp.abs(xg)))) * (1.0 / 16.0)

    la_hi, la_lo = _split_bf16(la_all)
    cum_all = _dot(ctril_ref[...], la_hi) + _dot(ctril_ref[...], la_lo)
    totals = jnp.concatenate([cum_all[(c + 1) * C - 1:(c + 1) * C, :] for c in range(n_chunks)]
                             + [jnp.zeros((LANES - n_chunks, GLA_QK), F32)], axis=0)
    decay_cols = jnp.exp(totals.T)

    for c in range(n_chunks):
        rows = slice(c * C, (c + 1) * C)
        cum = cum_all[rows]
        last = cum[C - 1:C, :]
        q = q_ref[rows, :]
        k = k_ref[rows, :]
        qt = (q * jnp.exp(cum) * (GLA_DK ** -0.5)).astype(BF16)
        kt = (k * jnp.exp(-cum)).astype(BF16)
        kl_t = (k * jnp.exp(last - cum)).T.astype(BF16)
        for h in range(GLA_HEADS):
            sl = slice(h * GLA_DK, (h + 1) * GLA_DK)
            vs = slice(h * GLA_DV, (h + 1) * GLA_DV)
            vh = v_ref[rows, vs].astype(BF16)
            state = s_ref[h]
            st_hi, st_lo = _split_bf16(state)
            sc = jnp.where(tril, _dot_nt(qt[:, sl], kt[:, sl]), 0.0).astype(BF16)
            o = _dot(qt[:, sl], st_hi) + _dot(qt[:, sl], st_lo) + _dot(sc, vh)
            decay = jnp.broadcast_to(decay_cols[sl, c:c + 1], state.shape)
            s_ref[h] = decay * state + _dot(kl_t[sl, :], vh)
            g = gr_ref[rows, vs]
            o_ref[rows, vs] = (_rms(o, gn_ref[...]) * (g * _sigmoid(g))).astype(o_ref.dtype)


def _gla_conv(u, wg, bg, gn, conv_w, side_casts=()):
    nj = SEQ // TC_GLA
    row = lambda b, j: b * nj + j
    full = lambda a: pl.BlockSpec(a.shape, lambda b, j: (0, 0))
    cast_specs = _side_cast_specs(side_casts, BATCH * nj, row)
    t = np.arange(TC_GLA)
    same_chunk = (t[:, None] // GLA_CHUNK) == (t[None, :] // GLA_CHUNK)
    chunk_tril = jnp.asarray(same_chunk & (t[:, None] >= t[None, :]), BF16)
    go, cv, *casts = pl.pallas_call(
        _gla_body,
        grid=(BATCH, nj),
        in_specs=[pl.BlockSpec((TC_GLA, U_COLS), lambda b, j: (row(b, j), 0)),
                  full(wg), full(bg), full(gn), full(chunk_tril), full(conv_w)] + cast_specs,
        out_specs=[pl.BlockSpec((TC_GLA, GLA_V), lambda b, j: (row(b, j), 0)),
                   pl.BlockSpec((TC_GLA, CONV_CH), lambda b, j: (row(b, j), 0))] + cast_specs,
        out_shape=[jax.ShapeDtypeStruct((TOKENS, GLA_V), BF16), jax.ShapeDtypeStruct((TOKENS, CONV_CH), BF16)]
                  + [jax.ShapeDtypeStruct(a.shape, BF16) for a in side_casts],
        scratch_shapes=[pltpu.VMEM((GLA_HEADS, GLA_DK, GLA_DV), F32), pltpu.VMEM((8, CONV_CH), F32)],
        compiler_params=_params(("arbitrary", "arbitrary")),
        name="gla_conv",
    )(u, wg, bg, gn, chunk_tril, conv_w, *side_casts)
    return go, cv, casts


def _outproj_body(*refs, route):
    (o1, o4, o16, l1, l4, l16, go_ref, cv_ref, x_ref, w_ref, g_ref) = refs[:11]
    perm_ref = refs[-1]
    if route:
        wr_ref, xo_ref, ho_ref, route_ref, counts_ref, carry_ref = refs[11:-1]
    else:
        xo_ref, ho_ref = refs[11:-1]

    def token_order(ref, slot):
        dilation, rows, _ = ref.shape
        if dilation == 1:
            return ref[0].astype(F32)
        tiles = range(ATT_WIDTH // LANES)
        for r in range(dilation):
            val = ref[r].astype(F32)
            for t in tiles:
                perm_ref[slot, t, pl.ds(r, rows, stride=dilation), :] = val[:, t * LANES:(t + 1) * LANES]
        return jnp.concatenate([perm_ref[slot, t] for t in tiles], axis=-1)

    la, lb, lc = token_order(l1, 0), token_order(l4, 0), token_order(l16, 1)
    oa, ob, oc = token_order(o1, 0), token_order(o4, 2), token_order(o16, 3)
    m = jnp.maximum(jnp.maximum(la, lb), lc)
    ea, eb, ec = jnp.exp(la - m), jnp.exp(lb - m), jnp.exp(lc - m)
    att = (ea * oa + eb * ob + ec * oc) / (ea + eb + ec)
    y = (x_ref[...]
         + _dot(att.astype(BF16), w_ref[0:ATT_WIDTH, :])
         + _dot(go_ref[...], w_ref[ATT_WIDTH:ATT_WIDTH + GLA_V, :])
         + _dot(cv_ref[...], w_ref[ATT_WIDTH + GLA_V:MIX_WIDTH, :]))
    xo_ref[...] = y
    hf = _rms(y, g_ref[...])
    ho_ref[...] = hf.astype(ho_ref.dtype)
    if route:
        @pl.when(pl.program_id(0) == 0)
        def _():
            carry_ref[...] = jnp.zeros_like(carry_ref)

        tm = hf.shape[0]
        ne = N_EXPERTS
        hf_hi, hf_lo = _split_bf16(hf)
        part = _dot_nt(wr_ref[0], hf_hi) + _dot_nt(wr_ref[1], hf_lo)
        logits = part[0:ne] + part[ne:2 * ne]
        eidx = lax.broadcasted_iota(jnp.int32, logits.shape, 0).astype(F32)
        v1 = jnp.max(logits, axis=0, keepdims=True)
        i1 = jnp.min(jnp.where(logits == v1, eidx, float(ne)), axis=0, keepdims=True)
        lg2 = jnp.where(eidx == i1, -jnp.inf, logits)
        v2 = jnp.max(lg2, axis=0, keepdims=True)
        i2 = jnp.min(jnp.where(lg2 == v2, eidx, float(ne)), axis=0, keepdims=True)
        e2 = jnp.exp(v2 - v1)
        w1 = 1.0 / (1.0 + e2)
        w2 = e2 * w1
        sel1 = eidx == i1
        sel2 = eidx == i2
        onehot = jnp.where(sel1, 1.0, jnp.where(sel2, 1.0, 0.0))
        tri = (lax.broadcasted_iota(jnp.int32, (tm, tm), 0) <= lax.broadcasted_iota(jnp.int32, (tm, tm), 1))
        onehot16 = jnp.concatenate([onehot, jnp.zeros_like(onehot)], axis=0).astype(BF16)
        csum = _dot(onehot16, jnp.where(tri, 1.0, 0.0).astype(BF16))[0:ne]
        carry = carry_ref[:, 0:1]
        rank = csum - onehot + carry
        r1 = jnp.sum(jnp.where(sel1, rank, 0.0), axis=0, keepdims=True)
        r2 = jnp.sum(jnp.where(sel2, rank, 0.0), axis=0, keepdims=True)
        total = jnp.broadcast_to(carry + csum[:, tm - 1:tm], carry_ref.shape)
        carry_ref[...] = total
        counts_ref[...] = total
        rows = {ROUTE_I1: i1, ROUTE_I2: i2, ROUTE_W1: w1, ROUTE_W2: w2, ROUTE_R1: r1, ROUTE_R2: r2}
        zero = jnp.zeros_like(i1)
        route_ref[...] = jnp.concatenate([rows.get(r, zero) for r in range(8)], axis=0)


def _outproj(att, go, cv, x, w, g, w_router=None):
    route = w_router is not None
    tm = TM_PROJ
    tile = lambda cols: pl.BlockSpec((tm, cols), lambda i: (i, 0))
    full = lambda a: pl.BlockSpec(a.shape, lambda i: (0, 0))
    (o1, l1), (o4, l4), (o16, l16) = att
    args = [o1, o4, o16, l1, l4, l16, go, cv, x, w, g]
    att_specs = [_subseq_spec(d, ATT_WIDTH) for _, d in DILATED_PATTERNS]
    in_specs = att_specs * 2 + [tile(GLA_V), tile(CONV_CH), tile(D_MODEL), full(w), full(g)]
    out_specs = [tile(D_MODEL), tile(D_MODEL)]
    out_shape = [jax.ShapeDtypeStruct((TOKENS, D_MODEL), F32),
                 jax.ShapeDtypeStruct((TOKENS, D_MODEL), F32 if route else BF16)]
    scratch = []
    if route:
        args.append(w_router)
        in_specs.append(pl.BlockSpec(w_router.shape, lambda i: (0, 0, 0)))
        out_specs += [pl.BlockSpec((8, tm), lambda i: (0, i)), pl.BlockSpec((N_EXPERTS, LANES), lambda i: (0, 0))]
        out_shape += [jax.ShapeDtypeStruct((8, TOKENS), F32), jax.ShapeDtypeStruct((N_EXPERTS, LANES), F32)]
        scratch = [pltpu.VMEM((N_EXPERTS, LANES), F32)]
    scratch.append(pltpu.VMEM((4, ATT_WIDTH // LANES, tm, LANES), F32))
    return pl.pallas_call(
        functools.partial(_outproj_body, route=route),
        grid=(TOKENS // tm,),
        in_specs=in_specs,
        out_specs=out_specs,
        out_shape=out_shape,
        scratch_shapes=scratch,
        compiler_params=_params(("arbitrary",) if route else ("parallel",)),
        name="outproj_route" if route else "outproj",
    )(*args)


def _swiglu_accumulate(h, w1_ref, w3_ref, w2_ref, acc_ref, tf):
    for c0 in range(0, tf, FFN_SUB):
        c1 = min(c0 + FFN_SUB, tf)
        a = _dot(h, w1_ref[:, c0:c1])
        b = _dot(h, w3_ref[:, c0:c1])
        act = a * _sigmoid(a) * b
        acc_ref[...] += _dot(act.astype(BF16), w2_ref[c0:c1, :])


def _ffn_body(x_ref, h_ref, w1_ref, w3_ref, w2_ref, *rest, tf):
    n_cast = (len(rest) - 1) // 2
    o_ref = rest[n_cast]

    @pl.when(pl.program_id(1) == 0)
    def _():
        o_ref[...] = x_ref[...]

    _swiglu_accumulate(h_ref[...], w1_ref, w3_ref, w2_ref, o_ref, tf)
    _side_cast(rest[:n_cast], rest[n_cast + 1:])


def _ffn(x, h, w1, w3, w2, *, tf, side_casts=()):
    f = w1.shape[1]
    tm = TM_FFN
    nj = f // tf
    tile = lambda cols: pl.BlockSpec((tm, cols), lambda i, j: (i, 0))
    cast_specs = _side_cast_specs(side_casts, (TOKENS // tm) * nj, lambda i, j: i * nj + j)
    res = pl.pallas_call(
        functools.partial(_ffn_body, tf=tf),
        grid=(TOKENS // tm, nj),
        in_specs=[tile(D_MODEL), tile(D_MODEL),
                  pl.BlockSpec((D_MODEL, tf), lambda i, j: (0, j)),
                  pl.BlockSpec((D_MODEL, tf), lambda i, j: (0, j)),
                  pl.BlockSpec((tf, D_MODEL), lambda i, j: (j, 0))] + cast_specs,
        out_specs=[tile(D_MODEL)] + cast_specs,
        out_shape=[jax.ShapeDtypeStruct((TOKENS, D_MODEL), F32)]
                  + [jax.ShapeDtypeStruct(a.shape, BF16) for a in side_casts],
        compiler_params=_params(("parallel", "arbitrary")),
        name="dense_ffn",
    )(x, h, w1, w3, w2, *side_casts)
    return res[0], res[1:]


ROW_TILE = D_MODEL // LANES


def _to_row_tiled(dst_ref, lead, val):
    rows = val.shape[0]
    for s in range(ROW_TILE):
        dst_ref[(*lead, pl.ds(s, rows, stride=ROW_TILE), slice(None))] = val[:, s * LANES:(s + 1) * LANES]


def _from_row_tiled(src_ref, lead, rows):
    return jnp.concatenate([src_ref[(*lead, pl.ds(s, rows, stride=ROW_TILE), slice(None))]
                            for s in range(ROW_TILE)], axis=-1)


def _row_tile(idx):
    return pl.ds(pl.multiple_of(idx * ROW_TILE, ROW_TILE), ROW_TILE)


def _dispatch_body(pos_ref, last_tile_ref, h_ref, xs_hbm, stage_ref, zero_ref, sem, zero_sem):
    i = pl.program_id(0)
    n = pl.num_programs(0)
    tm = DISPATCH_CHUNK
    slot = i % 2

    def drain(s):
        for _ in range(2):
            pltpu.make_async_copy(stage_ref.at[s], xs_hbm.at[pl.ds(0, tm * ROW_TILE)], sem.at[s]).wait()

    @pl.when(i == 0)
    def _():
        zero_ref[...] = jnp.zeros_like(zero_ref)

        def zero_copy(e):
            start = pl.multiple_of(last_tile_ref[e] * (TM_MOE * ROW_TILE), TM_MOE * ROW_TILE)
            return pltpu.make_async_copy(zero_ref, xs_hbm.at[pl.ds(start, TM_MOE * ROW_TILE)], zero_sem)

        for e in range(2 * N_EXPERTS):
            @pl.when(last_tile_ref[e] >= 0)
            def _():
                zero_copy(e).start()
        for e in range(2 * N_EXPERTS):
            @pl.when(last_tile_ref[e] >= 0)
            def _():
                zero_copy(e).wait()

    @pl.when(i >= 2)
    def _():
        drain(slot)

    _to_row_tiled(stage_ref, (slot,), h_ref[...])

    def body(t, carry):
        for k in range(2):
            dst = pos_ref[2 * (i * tm + t) + k]
            pltpu.make_async_copy(stage_ref.at[slot, _row_tile(t)], xs_hbm.at[_row_tile(dst)],
                                  sem.at[slot]).start(priority=k)
        return carry
    lax.fori_loop(0, tm, body, 0, unroll=8)

    @pl.when(i == n - 1)
    def _():
        drain(1 - slot)
        drain(slot)


def _dispatch(pos, last_tile, h):
    tm = DISPATCH_CHUNK
    return pl.pallas_call(
        _dispatch_body,
        grid=(TOKENS // tm,),
        in_specs=[pl.BlockSpec(memory_space=pltpu.SMEM),
                  pl.BlockSpec(memory_space=pltpu.SMEM),
                  pl.BlockSpec((tm, D_MODEL), lambda i: (i, 0))],
        out_specs=pl.BlockSpec(memory_space=pl.ANY),
        out_shape=jax.ShapeDtypeStruct((N_SORTED * ROW_TILE, LANES), F32),
        scratch_shapes=[pltpu.VMEM((2, tm * ROW_TILE, LANES), F32),
                        pltpu.VMEM((TM_MOE * ROW_TILE, LANES), F32),
                        pltpu.SemaphoreType.DMA((2,)), pltpu.SemaphoreType.DMA(())],
        compiler_params=_params(("arbitrary",)),
        name="moe_dispatch",
    )(pos, last_tile, h)


def _gffn_body(te_ref, nu_ref, xs_ref, w1_ref, w3_ref, w2_ref, o_ref, hb_ref, acc_ref):
    del te_ref
    i = pl.program_id(0)
    j = pl.program_id(1)

    used = i < nu_ref[0]

    @pl.when(j == 0)
    def _():
        acc_ref[...] = jnp.zeros_like(acc_ref)

    @pl.when(used & (j == 0))
    def _():
        hb_ref[...] = _from_row_tiled(xs_ref, (), TM_MOE).astype(BF16)

    @pl.when(used)
    def _():
        _swiglu_accumulate(hb_ref[...], w1_ref, w3_ref, w2_ref, acc_ref, TF_MOE)

    @pl.when(j == pl.num_programs(1) - 1)
    def _():
        _to_row_tiled(o_ref, (), acc_ref[...])


def _grouped_ffn(tile_expert, n_used, xs, w1, w3, w2):
    nj = FFN_EXPERT // TF_MOE
    col = lambda i, j, nu: jnp.where(i < nu[0], j, nj - 1)
    grid_spec = pltpu.PrefetchScalarGridSpec(
        num_scalar_prefetch=2,
        grid=(N_TILES_MOE, nj),
        in_specs=[
            pl.BlockSpec((TM_MOE * ROW_TILE, LANES), lambda i, j, te, nu: (jnp.minimum(i, nu[0] - 1), 0)),
            pl.BlockSpec((None, D_MODEL, TF_MOE), lambda i, j, te, nu: (te[i], 0, col(i, j, nu))),
            pl.BlockSpec((None, D_MODEL, TF_MOE), lambda i, j, te, nu: (te[i], 0, col(i, j, nu))),
            pl.BlockSpec((None, TF_MOE, D_MODEL), lambda i, j, te, nu: (te[i], col(i, j, nu), 0)),
        ],
        out_specs=pl.BlockSpec((TM_MOE * ROW_TILE, LANES), lambda i, j, te, nu: (i, 0)),
        scratch_shapes=[pltpu.VMEM((TM_MOE, D_MODEL), BF16), pltpu.VMEM((TM_MOE, D_MODEL), F32)],
    )
    return pl.pallas_call(
        _gffn_body,
        grid_spec=grid_spec,
        out_shape=jax.ShapeDtypeStruct((N_SORTED * ROW_TILE, LANES), F32),
        compiler_params=_params(("arbitrary", "arbitrary")),
        name="moe_ffn",
    )(tile_expert, n_used, xs, w1, w3, w2)


def _combine_body(pos_ref, x_ref, gate_ref, g_ref, ys_hbm, o_ref, buf_ref, sem):
    i = pl.program_id(0)
    n = pl.num_programs(0)
    tm = TM_COMBINE

    def issue(tile, slot):
        def body(t, carry):
            for k in range(2):
                src = pos_ref[2 * (tile * tm + t) + k]
                pltpu.make_async_copy(ys_hbm.at[_row_tile(src)], buf_ref.at[slot, k, _row_tile(t)],
                                      sem.at[slot]).start(priority=k)
            return carry
        lax.fori_loop(0, tm, body, 0, unroll=8)

    @pl.when(i == 0)
    def _():
        issue(0, 0)

    @pl.when(i + 1 < n)
    def _():
        issue(i + 1, (i + 1) % 2)

    slot = i % 2
    for k in range(2):
        pltpu.make_async_copy(ys_hbm.at[pl.ds(0, tm * ROW_TILE)], buf_ref.at[slot, k], sem.at[slot]).wait()
    w1 = gate_ref[:, 0:1]
    w2 = gate_ref[:, 1:2]
    y = x_ref[...] + w1 * _from_row_tiled(buf_ref, (slot, 0), tm) + w2 * _from_row_tiled(buf_ref, (slot, 1), tm)
    o_ref[...] = _rms(y, g_ref[...])


def _combine(pos, x, gates, g, ys):
    tm = TM_COMBINE
    return pl.pallas_call(
        _combine_body,
        grid=(TOKENS // tm,),
        in_specs=[pl.BlockSpec(memory_space=pltpu.SMEM),
                  pl.BlockSpec((tm, D_MODEL), lambda i: (i, 0)),
                  pl.BlockSpec((tm, 2), lambda i: (i, 0)),
                  pl.BlockSpec((1, D_MODEL), lambda i: (0, 0)),
                  pl.BlockSpec(memory_space=pl.ANY)],
        out_specs=pl.BlockSpec((tm, D_MODEL), lambda i: (i, 0)),
        out_shape=jax.ShapeDtypeStruct((TOKENS, D_MODEL), F32),
        scratch_shapes=[pltpu.VMEM((2, 2, tm * ROW_TILE, LANES), F32), pltpu.SemaphoreType.DMA((2,))],
        compiler_params=_params(("arbitrary",)),
        name="moe_combine",
    )(pos, x, gates, g, ys)


def _routing_tables(route, counts):
    cnt = counts[:, 0].astype(jnp.int32)
    tiles = (cnt + TM_MOE - 1) // TM_MOE
    tile_end = jnp.cumsum(tiles)
    tile_start = tile_end - tiles
    n_used = tile_end[-1]
    expert = route[ROUTE_I1:ROUTE_I2 + 1].astype(jnp.int32)
    rank = route[ROUTE_R1:ROUTE_R2 + 1].astype(jnp.int32)
    group_start = jnp.sum(jnp.where(expert[..., None] == jnp.arange(N_EXPERTS), tile_start * TM_MOE, 0), axis=-1)
    pos = (group_start + rank).T
    tile_id = jnp.minimum(jnp.arange(N_TILES_MOE, dtype=jnp.int32), n_used - 1)
    tile_expert = jnp.sum(tile_id[:, None] >= tile_end[None, :], axis=1).astype(jnp.int32)
    last_tile = jnp.where(tiles > 0, tile_end - 1, -1)
    spare = n_used + jnp.arange(N_EXPERTS)
    zero_tiles = jnp.concatenate([last_tile, jnp.where(spare < N_TILES_MOE, spare, -1)]).astype(jnp.int32)
    return pos.reshape(2 * TOKENS), tile_expert, n_used.reshape(1), zero_tiles


def _moe(x, h, route, counts, w1, w3, w2, g_final):
    pos, tile_expert, n_used, last_tile = _routing_tables(route, counts)
    xs = _dispatch(pos, last_tile, h)
    ys = _grouped_ffn(tile_expert, n_used, xs, w1, w3, w2)
    return _combine(pos, x, route[ROUTE_W1:ROUTE_W2 + 1].T, g_final, ys)


def _prep_w_in(w):
    aq, ak, av, gq, gk, gv, gr, glr, c_in, c_b, c_c = jnp.split(w, np.cumsum(SPLIT_SIZES)[:-1].tolist(), axis=1)
    pad = jnp.zeros((D_MODEL, LANES - GLA_RANK), w.dtype)
    return jnp.concatenate([aq, ak, av, gv, gr, gq, gk, c_in, c_b, c_c, glr, pad], axis=1).astype(BF16)


def _prep_router(w):
    wt = w.T
    hi = wt.astype(BF16)
    lo = (wt - hi.astype(F32)).astype(BF16)
    return jnp.stack([jnp.concatenate([hi, lo]), jnp.concatenate([hi, jnp.zeros_like(hi)])])


def kernel(x, w_mix_in, w_mix_out, g_mix, rel_bias, gla_w_gate, gla_b_gate, gla_g_norm, conv_w,
           g_ffn, ffn_w1, ffn_w3, ffn_w2, moe_router, moe_w1, moe_w3, moe_w2, g_final):
    assert DEPTH == 2
    x = x.reshape(TOKENS, D_MODEL)
    flat = lambda w: w.reshape(-1, w.shape[-1])
    gla_casts = {0: moe_w1[0], 1: moe_w3[0]}
    experts = {}
    for layer in range(DEPTH):
        u, *qkvs = _inproj(x, g_mix[layer].reshape(1, D_MODEL), _prep_w_in(w_mix_in[layer]))
        dense = [ffn_w1[0], ffn_w3[0], ffn_w2[0]] if layer == 0 else []
        att, dense_bf16 = _attention(qkvs, rel_bias, dense)
        if layer == 0:
            ffn_bf16 = dense_bf16
        wg = jnp.pad(gla_w_gate[layer], ((0, LANES - GLA_RANK), (0, 0)))
        go, cv, (cast,) = _gla_conv(u, wg, gla_b_gate[layer].reshape(1, GLA_QK),
                                    gla_g_norm[layer].reshape(1, GLA_DV),
                                    jnp.pad(conv_w[layer], ((0, 8 - CONV_WIDTH), (0, 0))),
                                    side_casts=[flat(gla_casts[layer])])
        experts[layer] = cast.reshape(gla_casts[layer].shape)
        w_out = w_mix_out[layer].astype(BF16)
        g2 = g_ffn[layer].reshape(1, D_MODEL)
        i = layer // 2
        if layer % 2 == 0:
            x, h = _outproj(att, go, cv, x, w_out, g2)
            x, (cast,) = _ffn(x, h, *ffn_bf16, tf=TF_DENSE, side_casts=[flat(moe_w2[i])])
            experts["w2"] = cast.reshape(moe_w2[i].shape)
        else:
            x, h, route, counts = _outproj(att, go, cv, x, w_out, g2, _prep_router(moe_router[i]))
            x = _moe(x, h, route, counts, experts[0], experts[1], experts["w2"], g_final.reshape(1, D_MODEL))
    return x.reshape(BATCH, SEQ, D_MODEL)
```

```python
import functools
import math

import jax
import jax.numpy as jnp
import numpy as np
from jax import lax
from jax.experimental import pallas as pl
from jax.experimental.pallas import tpu as pltpu

F32 = jnp.float32
BF16 = jnp.bfloat16

D_MODEL = 1024
BATCH = 8
SEQ = 2048
TOKENS = BATCH * SEQ
DEPTH = 2
EPS = 1e-6

HEAD_DIM = 64
ATT_HEADS = 4
ATT_WIDTH = ATT_HEADS * HEAD_DIM
DILATED_PATTERNS = ((128, 1), (512, 4), (2048, 16))
ATT_BLOCK = 128
REL_BUCKETS = 32
REL_MAX_DISTANCE = 2048

GLA_HEADS = 4
GLA_DK = 64
GLA_DV = 128
GLA_RANK = 16
GLA_CHUNK = 64
GLA_QK = GLA_HEADS * GLA_DK
GLA_V = GLA_HEADS * GLA_DV

CONV_CH = 256
CONV_WIDTH = 3
MIX_WIDTH = ATT_WIDTH + GLA_V + CONV_CH

SPLIT_SIZES = (ATT_WIDTH, ATT_WIDTH, ATT_WIDTH, GLA_QK, GLA_QK, GLA_V, GLA_V, GLA_RANK,
               CONV_CH, CONV_CH, CONV_CH)

FFN_DENSE = 2816
N_EXPERTS = 8
FFN_EXPERT = 3584

LANES = 128
MXU_WIDTH = 256
VMEM_LIMIT = 56 * 1024 * 1024

QKV_COLS = 3 * ATT_WIDTH
COL_GV, COL_GR, COL_GQ, COL_GK = 0, 512, 1024, 1280
COL_CIN, COL_CB, COL_CC, COL_GLR = 1536, 1792, 2048, 2304
U_COLS = COL_GLR + LANES

NEG_BIG = -1e30

ATT_UNROLL = 15
TM_PROJ = 512
TC_GLA = 512
TM_FFN = 512
TF_DENSE = 1408
FFN_SUB = 1792
TM_MOE = 512
TF_MOE = 1792
N_TILES_MOE = 2 * TOKENS // TM_MOE + N_EXPERTS
N_SORTED = N_TILES_MOE * TM_MOE
DISPATCH_CHUNK = 512
TM_COMBINE = 512

ROUTE_I1, ROUTE_I2, ROUTE_W1, ROUTE_W2, ROUTE_R1, ROUTE_R2 = range(6)


def _params(sem):
    return pltpu.CompilerParams(dimension_semantics=sem, vmem_limit_bytes=VMEM_LIMIT)


def _split_bf16(a):
    hi = a.astype(BF16)
    lo = (a - hi.astype(F32)).astype(BF16)
    return hi, lo


def _dot(a, b):
    return jnp.dot(a, b, preferred_element_type=F32)


def _dot3(a, b):
    a_hi, a_lo = _split_bf16(a)
    b_hi, b_lo = _split_bf16(b)
    return _dot(a_hi, b_hi) + _dot(a_lo, b_hi) + _dot(a_hi, b_lo)


def _dot_nt(a, b):
    return lax.dot_general(a, b, (((1,), (1,)), ((), ())), preferred_element_type=F32)


def _dot_tn(a, b):
    return lax.dot_general(a, b, (((0,), (0,)), ((), ())), preferred_element_type=F32)


def _rms(x, g):
    ms = jnp.mean(x * x, axis=-1, keepdims=True)
    return x * lax.rsqrt(ms + EPS) * g


def _sigmoid(x):
    return 1.0 / (1.0 + jnp.exp(-x))


def _side_cast(srcs, dsts):
    for src, dst in zip(srcs, dsts):
        dst[...] = src[...].astype(dst.dtype)


def _side_cast_specs(arrays, steps, step_of):
    return [pl.BlockSpec((a.shape[0] // steps, a.shape[1]), lambda *ids: (step_of(*ids), 0)) for a in arrays]


def _inproj_body(x_ref, g_ref, w_ref, o_ref, *rest):
    qkv_refs, qkv_f32 = rest[:-1], rest[-1]
    h = _rms(x_ref[...], g_ref[...]).astype(BF16)
    for c0 in range(0, QKV_COLS, MXU_WIDTH):
        res = _dot(h, w_ref[:, c0:c0 + MXU_WIDTH])
        for t in range(MXU_WIDTH // LANES):
            qkv_f32[c0 // LANES + t] = res[:, t * LANES:(t + 1) * LANES]
    for (_, dilation), ref in zip(DILATED_PATTERNS, qkv_refs):
        for r in range(dilation):
            rows = pl.ds(r, TM_PROJ // dilation, stride=dilation)
            ref[r] = jnp.concatenate([qkv_f32[t, rows, :] for t in range(QKV_COLS // LANES)],
                                     axis=-1).astype(ref.dtype)
    for c0 in range(0, U_COLS, MXU_WIDTH):
        c1 = min(c0 + MXU_WIDTH, U_COLS)
        o_ref[:, c0:c1] = _dot(h, w_ref[:, QKV_COLS + c0:QKV_COLS + c1])


def _subseq_spec(dilation, cols):
    tiles = SEQ // TM_PROJ
    return pl.BlockSpec((None, dilation, TM_PROJ // dilation, cols), lambda i: (i // tiles, 0, i % tiles, 0))


def _inproj(x, g, w):
    qkv_shapes = [jax.ShapeDtypeStruct((BATCH, d, SEQ // d, QKV_COLS), BF16) for _, d in DILATED_PATTERNS]
    return pl.pallas_call(
        _inproj_body,
        grid=(TOKENS // TM_PROJ,),
        in_specs=[
            pl.BlockSpec((TM_PROJ, D_MODEL), lambda i: (i, 0)),
            pl.BlockSpec((1, D_MODEL), lambda i: (0, 0)),
            pl.BlockSpec((D_MODEL, QKV_COLS + U_COLS), lambda i: (0, 0)),
        ],
        out_specs=[pl.BlockSpec((TM_PROJ, U_COLS), lambda i: (i, 0))]
                  + [_subseq_spec(d, QKV_COLS) for _, d in DILATED_PATTERNS],
        out_shape=[jax.ShapeDtypeStruct((TOKENS, U_COLS), F32)] + qkv_shapes,
        scratch_shapes=[pltpu.VMEM((QKV_COLS // LANES, TM_PROJ, LANES), F32)],
        compiler_params=_params(("parallel",)),
        name="inproj",
    )(x, g, w)


def _rel_bucket(dist):
    max_exact = REL_BUCKETS // 2
    d = jnp.maximum(dist, 0)
    log_ratio = jnp.log(jnp.maximum(d, 1).astype(F32) / max_exact) / math.log(REL_MAX_DISTANCE / max_exact)
    large = jnp.minimum(max_exact + (log_ratio * (REL_BUCKETS - max_exact)).astype(jnp.int32), REL_BUCKETS - 1)
    return jnp.where(d < max_exact, d, large)


def _bucket_table(window, dilation):
    span = window // dilation
    qi = jnp.arange(ATT_BLOCK)[:, None]
    kj = jnp.arange(2 * ATT_BLOCK)[None, :]
    sub_dist = qi - kj + ATT_BLOCK
    band = (sub_dist >= 0) & (sub_dist <= span)
    return jnp.where(band, _rel_bucket(sub_dist * dilation), -1).astype(jnp.int32)


def _attn_body(rb_ref, bidx_ref, qkv_ref, *rest, sub_blocks, unroll):
    n_cast = (len(rest) - 3) // 2
    o_ref, lse_ref = rest[n_cast:n_cast + 2]
    bias_ref = rest[-1]
    _side_cast(rest[:n_cast], rest[n_cast + 2:-1])
    nblk = SEQ // ATT_BLOCK

    @pl.when(pl.program_id(0) == 0)
    def _():
        bidx = bidx_ref[...]
        in_prev = lax.broadcasted_iota(jnp.int32, bidx.shape, 1) < ATT_BLOCK
        for h in range(ATT_HEADS):
            acc = jnp.full(bidx.shape, NEG_BIG, F32)
            for b in range(REL_BUCKETS):
                acc = jnp.where(bidx == b, rb_ref[b, h], acc)
            bias_ref[0, h] = acc
            bias_ref[1, h] = jnp.where(in_prev, NEG_BIG, acc)
            bias_ref[2, h] = jnp.concatenate([acc[:, ATT_BLOCK:], jnp.full_like(acc[:, ATT_BLOCK:], NEG_BIG)], axis=1)

    def block(n, first):
        if first:
            rows, krows, variant = slice(0, ATT_BLOCK), slice(0, 2 * ATT_BLOCK), 2
        else:
            r0 = pl.multiple_of(n * ATT_BLOCK, ATT_BLOCK)
            rows = pl.ds(r0, ATT_BLOCK)
            krows = pl.ds(r0 - ATT_BLOCK, 2 * ATT_BLOCK)
            if sub_blocks == nblk:
                variant = 0
            elif sub_blocks == 1:
                variant = 1
            else:
                variant = jnp.where(n % sub_blocks == 0, 1, 0)
        q = qkv_ref[rows, 0:ATT_WIDTH]
        kk = qkv_ref[krows, ATT_WIDTH:2 * ATT_WIDTH]
        vv = qkv_ref[krows, 2 * ATT_WIDTH:3 * ATT_WIDTH]
        q = q * jnp.asarray(HEAD_DIM ** -0.5, BF16)
        head_of_lane = lax.broadcasted_iota(jnp.int32, (ATT_BLOCK, ATT_WIDTH), 1) // HEAD_DIM
        ones = jnp.ones((kk.shape[0], LANES), BF16)
        num = den = mx = None
        for h in range(ATT_HEADS):
            mine = head_of_lane == h
            bias = bias_ref[variant, h]
            s = _dot_nt(jnp.where(mine, q, jnp.zeros_like(q)), kk) + bias
            m = jnp.max(s, axis=-1, keepdims=True)
            p = jnp.exp(s - m).astype(BF16)
            num_h = _dot(p, vv)
            den_h = jnp.tile(_dot(p, ones), (1, ATT_WIDTH // LANES))
            m_h = jnp.broadcast_to(m, (ATT_BLOCK, ATT_WIDTH))
            num = num_h if h == 0 else jnp.where(mine, num_h, num)
            den = den_h if h == 0 else jnp.where(mine, den_h, den)
            mx = m_h if h == 0 else jnp.where(mine, m_h, mx)
        o_ref[rows, :] = (num / den).astype(o_ref.dtype)
        lse_ref[rows, :] = mx + jnp.log(den)

    block(0, True)

    def loop_body(n, carry):
        block(n, False)
        return carry
    lax.fori_loop(1, nblk, loop_body, 0, unroll=unroll)


def _attention_pattern(ua, rel_bias, window, dilation, side_casts=()):
    L = SEQ // dilation
    shape = (BATCH, dilation, L, ATT_WIDTH)
    qkv_spec = pl.BlockSpec((None, SEQ, QKV_COLS), lambda b: (b, 0, 0))
    out_spec = pl.BlockSpec((None, SEQ, ATT_WIDTH), lambda b: (b, 0, 0))
    cast_specs = _side_cast_specs(side_casts, BATCH, lambda b: b)
    o, lse, *casts = pl.pallas_call(
        functools.partial(_attn_body, sub_blocks=L // ATT_BLOCK, unroll=ATT_UNROLL),
        grid=(BATCH,),
        in_specs=[
            pl.BlockSpec(memory_space=pltpu.SMEM),
            pl.BlockSpec((ATT_BLOCK, 2 * ATT_BLOCK), lambda b: (0, 0)),
            qkv_spec,
        ] + cast_specs,
        out_specs=[out_spec, out_spec] + cast_specs,
        out_shape=[jax.ShapeDtypeStruct((BATCH, SEQ, ATT_WIDTH), BF16),
                   jax.ShapeDtypeStruct((BATCH, SEQ, ATT_WIDTH), F32)]
                  + [jax.ShapeDtypeStruct(a.shape, BF16) for a in side_casts],
        scratch_shapes=[pltpu.VMEM((3, ATT_HEADS, ATT_BLOCK, 2 * ATT_BLOCK), F32)],
        compiler_params=_params(("arbitrary",)),
        name=f"attn_d{dilation}",
    )(rel_bias, _bucket_table(window, dilation), ua.reshape(BATCH, SEQ, QKV_COLS), *side_casts)
    return (o.reshape(shape), lse.reshape(shape)), casts


def _attention(qkvs, rel_bias, side_casts=()):
    side_casts = list(side_casts) + [None] * (len(DILATED_PATTERNS) - len(side_casts))
    res = [_attention_pattern(ua, rel_bias, window, dilation, [] if w is None else [w])
           for ua, (window, dilation), w in zip(qkvs, DILATED_PATTERNS, side_casts)]
    return [r[0] for r in res], [c for r in res for c in r[1]]


def _short_conv(cin_ref, cb_ref, cc_ref, w_ref, o_ref, tail_ref):
    uu = cc_ref[...] * cin_ref[...]
    t = lax.broadcasted_iota(jnp.int32, uu.shape, 0)
    y = uu * w_ref[CONV_WIDTH - 1:CONV_WIDTH, :]
    for shift in range(1, CONV_WIDTH):
        prev = pltpu.roll(uu, shift, axis=0)
        for r in range(shift):
            prev = jnp.where(t == r, tail_ref[8 - shift + r:8 - shift + r + 1, :], prev)
        y = y + prev * w_ref[CONV_WIDTH - 1 - shift:CONV_WIDTH - shift, :]
    o_ref[...] = (cb_ref[...] * y).astype(o_ref.dtype)
    tail_ref[...] = uu[uu.shape[0] - 8:, :]


def _gla_body(u_ref, wg_ref, bg_ref, gn_ref, ctril_ref, cw_ref, *rest):
    cols = lambda col, width: u_ref.at[:, col:col + width]
    q_ref, k_ref = cols(COL_GQ, GLA_QK), cols(COL_GK, GLA_QK)
    v_ref, gr_ref, glr_ref = cols(COL_GV, GLA_V), cols(COL_GR, GLA_V), cols(COL_GLR, LANES)
    cin_ref, cb_ref, cc_ref = cols(COL_CIN, CONV_CH), cols(COL_CB, CONV_CH), cols(COL_CC, CONV_CH)
    _gla_tile(q_ref, k_ref, v_ref, gr_ref, glr_ref, wg_ref, bg_ref, gn_ref, ctril_ref,
              cin_ref, cb_ref, cc_ref, cw_ref, *rest)


def _gla_tile(q_ref, k_ref, v_ref, gr_ref, glr_ref, wg_ref, bg_ref, gn_ref, ctril_ref,
              cin_ref, cb_ref, cc_ref, cw_ref, *rest):
    n_cast = (len(rest) - 4) // 2
    o_ref, cv_ref = rest[n_cast:n_cast + 2]
    s_ref, tail_ref = rest[-2:]
    _side_cast(rest[:n_cast], rest[n_cast + 2:-2])

    @pl.when(pl.program_id(1) == 0)
    def _():
        s_ref[...] = jnp.zeros_like(s_ref)
        tail_ref[...] = jnp.zeros_like(tail_ref)

    _short_conv(cin_ref, cb_ref, cc_ref, cw_ref, cv_ref, tail_ref)

    C = GLA_CHUNK
    row = lax.broadcasted_iota(jnp.int32, (C, C), 0)
    col = lax.broadcasted_iota(jnp.int32, (C, C), 1)
    tril = row >= col
    n_chunks = TC_GLA // C

    xg = _dot3(glr_ref[...], wg_ref[...]) + bg_ref[...]
    la_all = (jnp.minimum(xg, 0.0) - jnp.log(1.0 + jnp.exp(-jnp.abs(xg)))) * (1.0 / 16.0)

    la_hi, la_lo = _split_bf16(la_all)
    cum_all = _dot(ctril_ref[...], la_hi) + _dot(ctril_ref[...], la_lo)
    totals = jnp.concatenate([cum_all[(c + 1) * C - 1:(c + 1) * C, :] for c in range(n_chunks)]
                             + [jnp.zeros((LANES - n_chunks, GLA_QK), F32)], axis=0)
    decay_cols = jnp.exp(totals.T)

    for c in range(n_chunks):
        rows = slice(c * C, (c + 1) * C)
        cum = cum_all[rows]
        last = cum[C - 1:C, :]
        q = q_ref[rows, :]
        k = k_ref[rows, :]
        qt = (q * jnp.exp(cum) * (GLA_DK ** -0.5)).astype(BF16)
        kt = (k * jnp.exp(-cum)).astype(BF16)
        kl_t = (k * jnp.exp(last - cum)).T.astype(BF16)
        for h in range(GLA_HEADS):
            sl = slice(h * GLA_DK, (h + 1) * GLA_DK)
            vs = slice(h * GLA_DV, (h + 1) * GLA_DV)
            vh = v_ref[rows, vs].astype(BF16)
            state = s_ref[h]
            st_hi, st_lo = _split_bf16(state)
            sc = jnp.where(tril, _dot_nt(qt[:, sl], kt[:, sl]), 0.0).astype(BF16)
            o = _dot(qt[:, sl], st_hi) + _dot(qt[:, sl], st_lo) + _dot(sc, vh)
            decay = jnp.broadcast_to(decay_cols[sl, c:c + 1], state.shape)
            s_ref[h] = decay * state + _dot(kl_t[sl, :], vh)
            g = gr_ref[rows, vs]
            o_ref[rows, vs] = (_rms(o, gn_ref[...]) * (g * _sigmoid(g))).astype(o_ref.dtype)


def _gla_conv(u, wg, bg, gn, conv_w, side_casts=()):
    nj = SEQ // TC_GLA
    row = lambda b, j: b * nj + j
    full = lambda a: pl.BlockSpec(a.shape, lambda b, j: (0, 0))
    cast_specs = _side_cast_specs(side_casts, BATCH * nj, row)
    t = np.arange(TC_GLA)
    same_chunk = (t[:, None] // GLA_CHUNK) == (t[None, :] // GLA_CHUNK)
    chunk_tril = jnp.asarray(same_chunk & (t[:, None] >= t[None, :]), BF16)
    go, cv, *casts = pl.pallas_call(
        _gla_body,
        grid=(BATCH, nj),
        in_specs=[pl.BlockSpec((TC_GLA, U_COLS), lambda b, j: (row(b, j), 0)),
                  full(wg), full(bg), full(gn), full(chunk_tril), full(conv_w)] + cast_specs,
        out_specs=[pl.BlockSpec((TC_GLA, GLA_V), lambda b, j: (row(b, j), 0)),
                   pl.BlockSpec((TC_GLA, CONV_CH), lambda b, j: (row(b, j), 0))] + cast_specs,
        out_shape=[jax.ShapeDtypeStruct((TOKENS, GLA_V), BF16), jax.ShapeDtypeStruct((TOKENS, CONV_CH), BF16)]
                  + [jax.ShapeDtypeStruct(a.shape, BF16) for a in side_casts],
        scratch_shapes=[pltpu.VMEM((GLA_HEADS, GLA_DK, GLA_DV), F32), pltpu.VMEM((8, CONV_CH), F32)],
        compiler_params=_params(("arbitrary", "arbitrary")),
        name="gla_conv",
    )(u, wg, bg, gn, chunk_tril, conv_w, *side_casts)
    return go, cv, casts


def _outproj_body(*refs, route):
    (o1, o4, o16, l1, l4, l16, go_ref, cv_ref, x_ref, w_ref, g_ref) = refs[:11]
    perm_ref = refs[-1]
    if route:
        wr_ref, xo_ref, ho_ref, route_ref, counts_ref, carry_ref = refs[11:-1]
    else:
        xo_ref, ho_ref = refs[11:-1]

    def token_order(ref, slot):
        dilation, rows, _ = ref.shape
        if dilation == 1:
            return ref[0].astype(F32)
        tiles = range(ATT_WIDTH // LANES)
        for r in range(dilation):
            val = ref[r].astype(F32)
            for t in tiles:
                perm_ref[slot, t, pl.ds(r, rows, stride=dilation), :] = val[:, t * LANES:(t + 1) * LANES]
        return jnp.concatenate([perm_ref[slot, t] for t in tiles], axis=-1)

    la, lb, lc = token_order(l1, 0), token_order(l4, 0), token_order(l16, 1)
    oa, ob, oc = token_order(o1, 0), token_order(o4, 2), token_order(o16, 3)
    m = jnp.maximum(jnp.maximum(la, lb), lc)
    ea, eb, ec = jnp.exp(la - m), jnp.exp(lb - m), jnp.exp(lc - m)
    att = (ea * oa + eb * ob + ec * oc) / (ea + eb + ec)
    y = (x_ref[...]
         + _dot(att.astype(BF16), w_ref[0:ATT_WIDTH, :])
         + _dot(go_ref[...], w_ref[ATT_WIDTH:ATT_WIDTH + GLA_V, :])
         + _dot(cv_ref[...], w_ref[ATT_WIDTH + GLA_V:MIX_WIDTH, :]))
    xo_ref[...] = y
    hf = _rms(y, g_ref[...])
    ho_ref[...] = hf.astype(ho_ref.dtype)
    if route:
        @pl.when(pl.program_id(0) == 0)
        def _():
            carry_ref[...] = jnp.zeros_like(carry_ref)

        tm = hf.shape[0]
        ne = N_EXPERTS
        hf_hi, hf_lo = _split_bf16(hf)
        part = _dot_nt(wr_ref[0], hf_hi) + _dot_nt(wr_ref[1], hf_lo)
        logits = part[0:ne] + part[ne:2 * ne]
        eidx = lax.broadcasted_iota(jnp.int32, logits.shape, 0).astype(F32)
        v1 = jnp.max(logits, axis=0, keepdims=True)
        i1 = jnp.min(jnp.where(logits == v1, eidx, float(ne)), axis=0, keepdims=True)
        lg2 = jnp.where(eidx == i1, -jnp.inf, logits)
        v2 = jnp.max(lg2, axis=0, keepdims=True)
        i2 = jnp.min(jnp.where(lg2 == v2, eidx, float(ne)), axis=0, keepdims=True)
        e2 = jnp.exp(v2 - v1)
        w1 = 1.0 / (1.0 + e2)
        w2 = e2 * w1
        sel1 = eidx == i1
        sel2 = eidx == i2
        onehot = jnp.where(sel1, 1.0, jnp.where(sel2, 1.0, 0.0))
        tri = (lax.broadcasted_iota(jnp.int32, (tm, tm), 0) <= lax.broadcasted_iota(jnp.int32, (tm, tm), 1))
        onehot16 = jnp.concatenate([onehot, jnp.zeros_like(onehot)], axis=0).astype(BF16)
        csum = _dot(onehot16, jnp.where(tri, 1.0, 0.0).astype(BF16))[0:ne]
        carry = carry_ref[:, 0:1]
        rank = csum - onehot + carry
        r1 = jnp.sum(jnp.where(sel1, rank, 0.0), axis=0, keepdims=True)
        r2 = jnp.sum(jnp.where(sel2, rank, 0.0), axis=0, keepdims=True)
        total = jnp.broadcast_to(carry + csum[:, tm - 1:tm], carry_ref.shape)
        carry_ref[...] = total
        counts_ref[...] = total
        rows = {ROUTE_I1: i1, ROUTE_I2: i2, ROUTE_W1: w1, ROUTE_W2: w2, ROUTE_R1: r1, ROUTE_R2: r2}
        zero = jnp.zeros_like(i1)
        route_ref[...] = jnp.concatenate([rows.get(r, zero) for r in range(8)], axis=0)


def _outproj(att, go, cv, x, w, g, w_router=None):
    route = w_router is not None
    tm = TM_PROJ
    tile = lambda cols: pl.BlockSpec((tm, cols), lambda i: (i, 0))
    full = lambda a: pl.BlockSpec(a.shape, lambda i: (0, 0))
    (o1, l1), (o4, l4), (o16, l16) = att
    args = [o1, o4, o16, l1, l4, l16, go, cv, x, w, g]
    att_specs = [_subseq_spec(d, ATT_WIDTH) for _, d in DILATED_PATTERNS]
    in_specs = att_specs * 2 + [tile(GLA_V), tile(CONV_CH), tile(D_MODEL), full(w), full(g)]
    out_specs = [tile(D_MODEL), tile(D_MODEL)]
    out_shape = [jax.ShapeDtypeStruct((TOKENS, D_MODEL), F32),
                 jax.ShapeDtypeStruct((TOKENS, D_MODEL), F32 if route else BF16)]
    scratch = []
    if route:
        args.append(w_router)
        in_specs.append(pl.BlockSpec(w_router.shape, lambda i: (0, 0, 0)))
        out_specs += [pl.BlockSpec((8, tm), lambda i: (0, i)), pl.BlockSpec((N_EXPERTS, LANES), lambda i: (0, 0))]
        out_shape += [jax.ShapeDtypeStruct((8, TOKENS), F32), jax.ShapeDtypeStruct((N_EXPERTS, LANES), F32)]
        scratch = [pltpu.VMEM((N_EXPERTS, LANES), F32)]
    scratch.append(pltpu.VMEM((4, ATT_WIDTH // LANES, tm, LANES), F32))
    return pl.pallas_call(
        functools.partial(_outproj_body, route=route),
        grid=(TOKENS // tm,),
        in_specs=in_specs,
        out_specs=out_specs,
        out_shape=out_shape,
        scratch_shapes=scratch,
        compiler_params=_params(("arbitrary",) if route else ("parallel",)),
        name="outproj_route" if route else "outproj",
    )(*args)


def _swiglu_accumulate(h, w1_ref, w3_ref, w2_ref, acc_ref, tf):
    for c0 in range(0, tf, FFN_SUB):
        c1 = min(c0 + FFN_SUB, tf)
        a = _dot(h, w1_ref[:, c0:c1])
        b = _dot(h, w3_ref[:, c0:c1])
        act = a * _sigmoid(a) * b
        acc_ref[...] += _dot(act.astype(BF16), w2_ref[c0:c1, :])


def _ffn_body(x_ref, h_ref, w1_ref, w3_ref, w2_ref, *rest, tf):
    n_cast = (len(rest) - 1) // 2
    o_ref = rest[n_cast]

    @pl.when(pl.program_id(1) == 0)
    def _():
        o_ref[...] = x_ref[...]

    _swiglu_accumulate(h_ref[...], w1_ref, w3_ref, w2_ref, o_ref, tf)
    _side_cast(rest[:n_cast], rest[n_cast + 1:])


def _ffn(x, h, w1, w3, w2, *, tf, side_casts=()):
    f = w1.shape[1]
    tm = TM_FFN
    nj = f // tf
    tile = lambda cols: pl.BlockSpec((tm, cols), lambda i, j: (i, 0))
    cast_specs = _side_cast_specs(side_casts, (TOKENS // tm) * nj, lambda i, j: i * nj + j)
    res = pl.pallas_call(
        functools.partial(_ffn_body, tf=tf),
        grid=(TOKENS // tm, nj),
        in_specs=[tile(D_MODEL), tile(D_MODEL),
                  pl.BlockSpec((D_MODEL, tf), lambda i, j: (0, j)),
                  pl.BlockSpec((D_MODEL, tf), lambda i, j: (0, j)),
                  pl.BlockSpec((tf, D_MODEL), lambda i, j: (j, 0))] + cast_specs,
        out_specs=[tile(D_MODEL)] + cast_specs,
        out_shape=[jax.ShapeDtypeStruct((TOKENS, D_MODEL), F32)]
                  + [jax.ShapeDtypeStruct(a.shape, BF16) for a in side_casts],
        compiler_params=_params(("parallel", "arbitrary")),
        name="dense_ffn",
    )(x, h, w1, w3, w2, *side_casts)
    return res[0], res[1:]


ROW_TILE = D_MODEL // LANES


def _to_row_tiled(dst_ref, lead, val):
    rows = val.shape[0]
    for s in range(ROW_TILE):
        dst_ref[(*lead, pl.ds(s, rows, stride=ROW_TILE), slice(None))] = val[:, s * LANES:(s + 1) * LANES]


def _from_row_tiled(src_ref, lead, rows):
    return jnp.concatenate([src_ref[(*lead, pl.ds(s, rows, stride=ROW_TILE), slice(None))]
                            for s in range(ROW_TILE)], axis=-1)


def _row_tile(idx):
    return pl.ds(pl.multiple_of(idx * ROW_TILE, ROW_TILE), ROW_TILE)


def _dispatch_body(pos_ref, last_tile_ref, h_ref, xs_hbm, stage_ref, zero_ref, sem, zero_sem):
    i = pl.program_id(0)
    n = pl.num_programs(0)
    tm = DISPATCH_CHUNK
    slot = i % 2

    def drain(s):
        for _ in range(2):
            pltpu.make_async_copy(stage_ref.at[s], xs_hbm.at[pl.ds(0, tm * ROW_TILE)], sem.at[s]).wait()

    @pl.when(i == 0)
    def _():
        zero_ref[...] = jnp.zeros_like(zero_ref)

        def zero_copy(e):
            start = pl.multiple_of(last_tile_ref[e] * (TM_MOE * ROW_TILE), TM_MOE * ROW_TILE)
            return pltpu.make_async_copy(zero_ref, xs_hbm.at[pl.ds(start, TM_MOE * ROW_TILE)], zero_sem)

        for e in range(2 * N_EXPERTS):
            @pl.when(last_tile_ref[e] >= 0)
            def _():
                zero_copy(e).start()
        for e in range(2 * N_EXPERTS):
            @pl.when(last_tile_ref[e] >= 0)
            def _():
                zero_copy(e).wait()

    @pl.when(i >= 2)
    def _():
        drain(slot)

    _to_row_tiled(stage_ref, (slot,), h_ref[...])

    def body(t, carry):
        for k in range(2):
            dst = pos_ref[2 * (i * tm + t) + k]
            pltpu.make_async_copy(stage_ref.at[slot, _row_tile(t)], xs_hbm.at[_row_tile(dst)],
                                  sem.at[slot]).start(priority=k)
        return carry
    lax.fori_loop(0, tm, body, 0, unroll=8)

    @pl.when(i == n - 1)
    def _():
        drain(1 - slot)
        drain(slot)


def _dispatch(pos, last_tile, h):
    tm = DISPATCH_CHUNK
    return pl.pallas_call(
        _dispatch_body,
        grid=(TOKENS // tm,),
        in_specs=[pl.BlockSpec(memory_space=pltpu.SMEM),
                  pl.BlockSpec(memory_space=pltpu.SMEM),
                  pl.BlockSpec((tm, D_MODEL), lambda i: (i, 0))],
        out_specs=pl.BlockSpec(memory_space=pl.ANY),
        out_shape=jax.ShapeDtypeStruct((N_SORTED * ROW_TILE, LANES), F32),
        scratch_shapes=[pltpu.VMEM((2, tm * ROW_TILE, LANES), F32),
                        pltpu.VMEM((TM_MOE * ROW_TILE, LANES), F32),
                        pltpu.SemaphoreType.DMA((2,)), pltpu.SemaphoreType.DMA(())],
        compiler_params=_params(("arbitrary",)),
        name="moe_dispatch",
    )(pos, last_tile, h)


def _gffn_body(te_ref, nu_ref, xs_ref, w1_ref, w3_ref, w2_ref, o_ref, hb_ref, acc_ref):
    del te_ref
    i = pl.program_id(0)
    j = pl.program_id(1)

    used = i < nu_ref[0]

    @pl.when(j == 0)
    def _():
        acc_ref[...] = jnp.zeros_like(acc_ref)

    @pl.when(used & (j == 0))
    def _():
        hb_ref[...] = _from_row_tiled(xs_ref, (), TM_MOE).astype(BF16)

    @pl.when(used)
    def _():
        _swiglu_accumulate(hb_ref[...], w1_ref, w3_ref, w2_ref, acc_ref, TF_MOE)

    @pl.when(j == pl.num_programs(1) - 1)
    def _():
        _to_row_tiled(o_ref, (), acc_ref[...])


def _grouped_ffn(tile_expert, n_used, xs, w1, w3, w2):
    nj = FFN_EXPERT // TF_MOE
    col = lambda i, j, nu: jnp.where(i < nu[0], j, nj - 1)
    grid_spec = pltpu.PrefetchScalarGridSpec(
        num_scalar_prefetch=2,
        grid=(N_TILES_MOE, nj),
        in_specs=[
            pl.BlockSpec((TM_MOE * ROW_TILE, LANES), lambda i, j, te, nu: (jnp.minimum(i, nu[0] - 1), 0)),
            pl.BlockSpec((None, D_MODEL, TF_MOE), lambda i, j, te, nu: (te[i], 0, col(i, j, nu))),
            pl.BlockSpec((None, D_MODEL, TF_MOE), lambda i, j, te, nu: (te[i], 0, col(i, j, nu))),
            pl.BlockSpec((None, TF_MOE, D_MODEL), lambda i, j, te, nu: (te[i], col(i, j, nu), 0)),
        ],
        out_specs=pl.BlockSpec((TM_MOE * ROW_TILE, LANES), lambda i, j, te, nu: (i, 0)),
        scratch_shapes=[pltpu.VMEM((TM_MOE, D_MODEL), BF16), pltpu.VMEM((TM_MOE, D_MODEL), F32)],
    )
    return pl.pallas_call(
        _gffn_body,
        grid_spec=grid_spec,
        out_shape=jax.ShapeDtypeStruct((N_SORTED * ROW_TILE, LANES), F32),
        compiler_params=_params(("arbitrary", "arbitrary")),
        name="moe_ffn",
    )(tile_expert, n_used, xs, w1, w3, w2)


def _combine_body(pos_ref, x_ref, gate_ref, g_ref, ys_hbm, o_ref, buf_ref, sem):
    i = pl.program_id(0)
    n = pl.num_programs(0)
    tm = TM_COMBINE

    def issue(tile, slot):
        def body(t, carry):
            for k in range(2):
                src = pos_ref[2 * (tile * tm + t) + k]
                pltpu.make_async_copy(ys_hbm.at[_row_tile(src)], buf_ref.at[slot, k, _row_tile(t)],
                                      sem.at[slot]).start(priority=k)
            return carry
        lax.fori_loop(0, tm, body, 0, unroll=8)

    @pl.when(i == 0)
    def _():
        issue(0, 0)

    @pl.when(i + 1 < n)
    def _():
        issue(i + 1, (i + 1) % 2)

    slot = i % 2
    for k in range(2):
        pltpu.make_async_copy(ys_hbm.at[pl.ds(0, tm * ROW_TILE)], buf_ref.at[slot, k], sem.at[slot]).wait()
    w1 = gate_ref[:, 0:1]
    w2 = gate_ref[:, 1:2]
    y = x_ref[...] + w1 * _from_row_tiled(buf_ref, (slot, 0), tm) + w2 * _from_row_tiled(buf_ref, (slot, 1), tm)
    o_ref[...] = _rms(y, g_ref[...])


def _combine(pos, x, gates, g, ys):
    tm = TM_COMBINE
    return pl.pallas_call(
        _combine_body,
        grid=(TOKENS // tm,),
        in_specs=[pl.BlockSpec(memory_space=pltpu.SMEM),
                  pl.BlockSpec((tm, D_MODEL), lambda i: (i, 0)),
                  pl.BlockSpec((tm, 2), lambda i: (i, 0)),
                  pl.BlockSpec((1, D_MODEL), lambda i: (0, 0)),
                  pl.BlockSpec(memory_space=pl.ANY)],
        out_specs=pl.BlockSpec((tm, D_MODEL), lambda i: (i, 0)),
        out_shape=jax.ShapeDtypeStruct((TOKENS, D_MODEL), F32),
        scratch_shapes=[pltpu.VMEM((2, 2, tm * ROW_TILE, LANES), F32), pltpu.SemaphoreType.DMA((2,))],
        compiler_params=_params(("arbitrary",)),
        name="moe_combine",
    )(pos, x, gates, g, ys)


def _routing_tables(route, counts):
    cnt = counts[:, 0].astype(jnp.int32)
    tiles = (cnt + TM_MOE - 1) // TM_MOE
    tile_end = jnp.cumsum(tiles)
    tile_start = tile_end - tiles
    n_used = tile_end[-1]
    expert = route[ROUTE_I1:ROUTE_I2 + 1].astype(jnp.int32)
    rank = route[ROUTE_R1:ROUTE_R2 + 1].astype(jnp.int32)
    group_start = jnp.sum(jnp.where(expert[..., None] == jnp.arange(N_EXPERTS), tile_start * TM_MOE, 0), axis=-1)
    pos = (group_start + rank).T
    tile_id = jnp.minimum(jnp.arange(N_TILES_MOE, dtype=jnp.int32), n_used - 1)
    tile_expert = jnp.sum(tile_id[:, None] >= tile_end[None, :], axis=1).astype(jnp.int32)
    last_tile = jnp.where(tiles > 0, tile_end - 1, -1)
    spare = n_used + jnp.arange(N_EXPERTS)
    zero_tiles = jnp.concatenate([last_tile, jnp.where(spare < N_TILES_MOE, spare, -1)]).astype(jnp.int32)
    return pos.reshape(2 * TOKENS), tile_expert, n_used.reshape(1), zero_tiles


def _moe(x, h, route, counts, w1, w3, w2, g_final):
    pos, tile_expert, n_used, last_tile = _routing_tables(route, counts)
    xs = _dispatch(pos, last_tile, h)
    ys = _grouped_ffn(tile_expert, n_used, xs, w1, w3, w2)
    return _combine(pos, x, route[ROUTE_W1:ROUTE_W2 + 1].T, g_final, ys)


def _prep_w_in(w):
    aq, ak, av, gq, gk, gv, gr, glr, c_in, c_b, c_c = jnp.split(w, np.cumsum(SPLIT_SIZES)[:-1].tolist(), axis=1)
    pad = jnp.zeros((D_MODEL, LANES - GLA_RANK), w.dtype)
    return jnp.concatenate([aq, ak, av, gv, gr, gq, gk, c_in, c_b, c_c, glr, pad], axis=1).astype(BF16)


def _prep_router(w):
    wt = w.T
    hi = wt.astype(BF16)
    lo = (wt - hi.astype(F32)).astype(BF16)
    return jnp.stack([jnp.concatenate([hi, lo]), jnp.concatenate([hi, jnp.zeros_like(hi)])])


def kernel(x, w_mix_in, w_mix_out, g_mix, rel_bias, gla_w_gate, gla_b_gate, gla_g_norm, conv_w,
           g_ffn, ffn_w1, ffn_w3, ffn_w2, moe_router, moe_w1, moe_w3, moe_w2, g_final):
    assert DEPTH == 2
    x = x.reshape(TOKENS, D_MODEL)
    flat = lambda w: w.reshape(-1, w.shape[-1])
    gla_casts = {0: moe_w1[0], 1: moe_w3[0]}
    experts = {}
    for layer in range(DEPTH):
        u, *qkvs = _inproj(x, g_mix[layer].reshape(1, D_MODEL), _prep_w_in(w_mix_in[layer]))
        dense = [ffn_w1[0], ffn_w3[0], ffn_w2[0]] if layer == 0 else []
        att, dense_bf16 = _attention(qkvs, rel_bias, dense)
        if layer == 0:
            ffn_bf16 = dense_bf16
        wg = jnp.pad(gla_w_gate[layer], ((0, LANES - GLA_RANK), (0, 0)))
        go, cv, (cast,) = _gla_conv(u, wg, gla_b_gate[layer].reshape(1, GLA_QK),
                                    gla_g_norm[layer].reshape(1, GLA_DV),
                                    jnp.pad(conv_w[layer], ((0, 8 - CONV_WIDTH), (0, 0))),
                                    side_casts=[flat(gla_casts[layer])])
        experts[layer] = cast.reshape(gla_casts[layer].shape)
        w_out = w_mix_out[layer].astype(BF16)
        g2 = g_ffn[layer].reshape(1, D_MODEL)
        i = layer // 2
        if layer % 2 == 0:
            x, h = _outproj(att, go, cv, x, w_out, g2)
            x, (cast,) = _ffn(x, h, *ffn_bf16, tf=TF_DENSE, side_casts=[flat(moe_w2[i])])
            experts["w2"] = cast.reshape(moe_w2[i].shape)
        else:
            x, h, route, counts = _outproj(att, go, cv, x, w_out, g2, _prep_router(moe_router[i]))
            x = _moe(x, h, route, counts, experts[0], experts[1], experts["w2"], g_final.reshape(1, D_MODEL))
    return x.reshape(BATCH, SEQ, D_MODEL)
```

```python
import functools
import math

import jax
import jax.numpy as jnp
import numpy as np
from jax import lax
from jax.experimental import pallas as pl
from jax.experimental.pallas import tpu as pltpu

F32 = jnp.float32
BF16 = jnp.bfloat16

D_MODEL = 1024
BATCH = 8
SEQ = 2048
TOKENS = BATCH * SEQ
DEPTH = 2
EPS = 1e-6

HEAD_DIM = 64
ATT_HEADS = 4
ATT_WIDTH = ATT_HEADS * HEAD_DIM
DILATED_PATTERNS = ((128, 1), (512, 4), (2048, 16))
ATT_BLOCK = 128
REL_BUCKETS = 32
REL_MAX_DISTANCE = 2048

GLA_HEADS = 4
GLA_DK = 64
GLA_DV = 128
GLA_RANK = 16
GLA_CHUNK = 64
GLA_QK = GLA_HEADS * GLA_DK
GLA_V = GLA_HEADS * GLA_DV

CONV_CH = 256
CONV_WIDTH = 3
MIX_WIDTH = ATT_WIDTH + GLA_V + CONV_CH

SPLIT_SIZES = (ATT_WIDTH, ATT_WIDTH, ATT_WIDTH, GLA_QK, GLA_QK, GLA_V, GLA_V, GLA_RANK,
               CONV_CH, CONV_CH, CONV_CH)

FFN_DENSE = 2816
N_EXPERTS = 8
FFN_EXPERT = 3584

LANES = 128
MXU_WIDTH = 256
VMEM_LIMIT = 56 * 1024 * 1024

QKV_COLS = 3 * ATT_WIDTH
COL_GV, COL_GR, COL_GQ, COL_GK = 0, 512, 1024, 1280
COL_CIN, COL_CB, COL_CC, COL_GLR = 1536, 1792, 2048, 2304
U_COLS = COL_GLR + LANES

NEG_BIG = -1e30

ATT_UNROLL = 15
TM_PROJ = 512
TC_GLA = 512
TM_FFN = 512
TF_DENSE = 1408
FFN_SUB = 256
TM_MOE = 512
TF_MOE = 1792
N_TILES_MOE = 2 * TOKENS // TM_MOE + N_EXPERTS
N_SORTED = N_TILES_MOE * TM_MOE
DISPATCH_CHUNK = 512
TM_COMBINE = 512

ROUTE_I1, ROUTE_I2, ROUTE_W1, ROUTE_W2, ROUTE_R1, ROUTE_R2 = range(6)


def _params(sem):
    return pltpu.CompilerParams(dimension_semantics=sem, vmem_limit_bytes=VMEM_LIMIT)


def _split_bf16(a):
    hi = a.astype(BF16)
    lo = (a - hi.astype(F32)).astype(BF16)
    return hi, lo


def _dot(a, b):
    return jnp.dot(a, b, preferred_element_type=F32)


def _dot3(a, b):
    a_hi, a_lo = _split_bf16(a)
    b_hi, b_lo = _split_bf16(b)
    return _dot(a_hi, b_hi) + _dot(a_lo, b_hi) + _dot(a_hi, b_lo)


def _dot_nt(a, b):
    return lax.dot_general(a, b, (((1,), (1,)), ((), ())), preferred_element_type=F32)


def _dot_tn(a, b):
    return lax.dot_general(a, b, (((0,), (0,)), ((), ())), preferred_element_type=F32)


def _rms(x, g):
    ms = jnp.mean(x * x, axis=-1, keepdims=True)
    return x * lax.rsqrt(ms + EPS) * g


def _sigmoid(x):
    return 1.0 / (1.0 + jnp.exp(-x))


def _side_cast(srcs, dsts):
    for src, dst in zip(srcs, dsts):
        width = dst.shape[-1]
        for s in range(dst.shape[0]):
            dst[s] = src[:, s * width:(s + 1) * width].astype(dst.dtype)


def _side_cast_io(jobs, steps, step_of):
    in_specs, out_specs, out_shapes = [], [], []
    for w, splits in jobs:
        g, r, c = w.shape
        rb = g * r // steps
        per_group = r // rb
        in_specs.append(pl.BlockSpec(
            (None, rb, c), lambda *ids, pg=per_group: (step_of(*ids) // pg, step_of(*ids) % pg, 0)))
        out_specs.append(pl.BlockSpec(
            (None, splits, rb, c // splits),
            lambda *ids, pg=per_group: (step_of(*ids) // pg, 0, step_of(*ids) % pg, 0)))
        out_shapes.append(jax.ShapeDtypeStruct((g, splits, r, c // splits), BF16))
    return in_specs, out_specs, out_shapes


def _inproj_body(x_ref, g_ref, w_ref, o_ref, *rest):
    qkv_refs, qkv_f32 = rest[:-1], rest[-1]
    h = _rms(x_ref[...], g_ref[...]).astype(BF16)
    for c0 in range(0, QKV_COLS, MXU_WIDTH):
        res = _dot(h, w_ref[:, c0:c0 + MXU_WIDTH])
        for t in range(MXU_WIDTH // LANES):
            qkv_f32[c0 // LANES + t] = res[:, t * LANES:(t + 1) * LANES]
    for (_, dilation), ref in zip(DILATED_PATTERNS, qkv_refs):
        for r in range(dilation):
            rows = pl.ds(r, TM_PROJ // dilation, stride=dilation)
            ref[r] = jnp.concatenate([qkv_f32[t, rows, :] for t in range(QKV_COLS // LANES)],
                                     axis=-1).astype(ref.dtype)
    for c0 in range(0, U_COLS, MXU_WIDTH):
        c1 = min(c0 + MXU_WIDTH, U_COLS)
        o_ref[:, c0:c1] = _dot(h, w_ref[:, QKV_COLS + c0:QKV_COLS + c1])


def _subseq_spec(dilation, cols):
    tiles = SEQ // TM_PROJ
    return pl.BlockSpec((None, dilation, TM_PROJ // dilation, cols), lambda i: (i // tiles, 0, i % tiles, 0))


def _inproj(x, g, w):
    qkv_shapes = [jax.ShapeDtypeStruct((BATCH, d, SEQ // d, QKV_COLS), BF16) for _, d in DILATED_PATTERNS]
    return pl.pallas_call(
        _inproj_body,
        grid=(TOKENS // TM_PROJ,),
        in_specs=[
            pl.BlockSpec((TM_PROJ, D_MODEL), lambda i: (i, 0)),
            pl.BlockSpec((1, D_MODEL), lambda i: (0, 0)),
            pl.BlockSpec((D_MODEL, QKV_COLS + U_COLS), lambda i: (0, 0)),
        ],
        out_specs=[pl.BlockSpec((TM_PROJ, U_COLS), lambda i: (i, 0))]
                  + [_subseq_spec(d, QKV_COLS) for _, d in DILATED_PATTERNS],
        out_shape=[jax.ShapeDtypeStruct((TOKENS, U_COLS), F32)] + qkv_shapes,
        scratch_shapes=[pltpu.VMEM((QKV_COLS // LANES, TM_PROJ, LANES), F32)],
        compiler_params=_params(("parallel",)),
        name="inproj",
    )(x, g, w)


def _rel_bucket(dist):
    max_exact = REL_BUCKETS // 2
    d = jnp.maximum(dist, 0)
    log_ratio = jnp.log(jnp.maximum(d, 1).astype(F32) / max_exact) / math.log(REL_MAX_DISTANCE / max_exact)
    large = jnp.minimum(max_exact + (log_ratio * (REL_BUCKETS - max_exact)).astype(jnp.int32), REL_BUCKETS - 1)
    return jnp.where(d < max_exact, d, large)


def _bucket_table(window, dilation):
    span = window // dilation
    qi = jnp.arange(ATT_BLOCK)[:, None]
    kj = jnp.arange(2 * ATT_BLOCK)[None, :]
    sub_dist = qi - kj + ATT_BLOCK
    band = (sub_dist >= 0) & (sub_dist <= span)
    return jnp.where(band, _rel_bucket(sub_dist * dilation), -1).astype(jnp.int32)


def _attn_body(rb_ref, bidx_ref, qkv_ref, *rest, sub_blocks, unroll):
    n_cast = (len(rest) - 3) // 2
    o_ref, lse_ref = rest[n_cast:n_cast + 2]
    bias_ref = rest[-1]
    _side_cast(rest[:n_cast], rest[n_cast + 2:-1])
    nblk = SEQ // ATT_BLOCK

    @pl.when(pl.program_id(0) == 0)
    def _():
        bidx = bidx_ref[...]
        in_prev = lax.broadcasted_iota(jnp.int32, bidx.shape, 1) < ATT_BLOCK
        for h in range(ATT_HEADS):
            acc = jnp.full(bidx.shape, NEG_BIG, F32)
            for b in range(REL_BUCKETS):
                acc = jnp.where(bidx == b, rb_ref[b, h], acc)
            bias_ref[0, h] = acc
            bias_ref[1, h] = jnp.where(in_prev, NEG_BIG, acc)
            bias_ref[2, h] = jnp.concatenate([acc[:, ATT_BLOCK:], jnp.full_like(acc[:, ATT_BLOCK:], NEG_BIG)], axis=1)

    def block(n, first):
        if first:
            rows, krows, variant = slice(0, ATT_BLOCK), slice(0, 2 * ATT_BLOCK), 2
        else:
            r0 = pl.multiple_of(n * ATT_BLOCK, ATT_BLOCK)
            rows = pl.ds(r0, ATT_BLOCK)
            krows = pl.ds(r0 - ATT_BLOCK, 2 * ATT_BLOCK)
            if sub_blocks == nblk:
                variant = 0
            elif sub_blocks == 1:
                variant = 1
            else:
                variant = jnp.where(n % sub_blocks == 0, 1, 0)
        q = qkv_ref[rows, 0:ATT_WIDTH]
        kk = qkv_ref[krows, ATT_WIDTH:2 * ATT_WIDTH]
        vv = qkv_ref[krows, 2 * ATT_WIDTH:3 * ATT_WIDTH]
        q = q * jnp.asarray(HEAD_DIM ** -0.5, BF16)
        head_of_lane = lax.broadcasted_iota(jnp.int32, (ATT_BLOCK, ATT_WIDTH), 1) // HEAD_DIM
        ones = jnp.ones((kk.shape[0], LANES), BF16)
        num = den = mx = None
        for h in range(ATT_HEADS):
            mine = head_of_lane == h
            bias = bias_ref[variant, h]
            s = _dot_nt(jnp.where(mine, q, jnp.zeros_like(q)), kk) + bias
            m = jnp.max(s, axis=-1, keepdims=True)
            p = jnp.exp(s - m).astype(BF16)
            num_h = _dot(p, vv)
            den_h = jnp.tile(_dot(p, ones), (1, ATT_WIDTH // LANES))
            m_h = jnp.broadcast_to(m, (ATT_BLOCK, ATT_WIDTH))
            num = num_h if h == 0 else jnp.where(mine, num_h, num)
            den = den_h if h == 0 else jnp.where(mine, den_h, den)
            mx = m_h if h == 0 else jnp.where(mine, m_h, mx)
        o_ref[rows, :] = (num / den).astype(o_ref.dtype)
        lse_ref[rows, :] = mx + jnp.log(den)

    block(0, True)

    def loop_body(n, carry):
        block(n, False)
        return carry
    lax.fori_loop(1, nblk, loop_body, 0, unroll=unroll)


def _attention_pattern(ua, rel_bias, window, dilation, side_casts=()):
    L = SEQ // dilation
    shape = (BATCH, dilation, L, ATT_WIDTH)
    qkv_spec = pl.BlockSpec((None, SEQ, QKV_COLS), lambda b: (b, 0, 0))
    out_spec = pl.BlockSpec((None, SEQ, ATT_WIDTH), lambda b: (b, 0, 0))
    cast_in, cast_out, cast_shapes = _side_cast_io(side_casts, BATCH, lambda b: b)
    o, lse, *casts = pl.pallas_call(
        functools.partial(_attn_body, sub_blocks=L // ATT_BLOCK, unroll=ATT_UNROLL),
        grid=(BATCH,),
        in_specs=[
            pl.BlockSpec(memory_space=pltpu.SMEM),
            pl.BlockSpec((ATT_BLOCK, 2 * ATT_BLOCK), lambda b: (0, 0)),
            qkv_spec,
        ] + cast_in,
        out_specs=[out_spec, out_spec] + cast_out,
        out_shape=[jax.ShapeDtypeStruct((BATCH, SEQ, ATT_WIDTH), BF16),
                   jax.ShapeDtypeStruct((BATCH, SEQ, ATT_WIDTH), F32)] + cast_shapes,
        scratch_shapes=[pltpu.VMEM((3, ATT_HEADS, ATT_BLOCK, 2 * ATT_BLOCK), F32)],
        compiler_params=_params(("arbitrary",)),
        name=f"attn_d{dilation}",
    )(rel_bias, _bucket_table(window, dilation), ua.reshape(BATCH, SEQ, QKV_COLS), *[w for w, _ in side_casts])
    return (o.reshape(shape), lse.reshape(shape)), casts


def _attention(qkvs, rel_bias, side_casts=()):
    side_casts = list(side_casts) + [None] * (len(DILATED_PATTERNS) - len(side_casts))
    res = [_attention_pattern(ua, rel_bias, window, dilation, [] if w is None else [w])
           for ua, (window, dilation), w in zip(qkvs, DILATED_PATTERNS, side_casts)]
    return [r[0] for r in res], [c for r in res for c in r[1]]


def _short_conv(cin_ref, cb_ref, cc_ref, w_ref, o_ref, tail_ref):
    uu = cc_ref[...] * cin_ref[...]
    t = lax.broadcasted_iota(jnp.int32, uu.shape, 0)
    y = uu * w_ref[CONV_WIDTH - 1:CONV_WIDTH, :]
    for shift in range(1, CONV_WIDTH):
        prev = pltpu.roll(uu, shift, axis=0)
        for r in range(shift):
            prev = jnp.where(t == r, tail_ref[8 - shift + r:8 - shift + r + 1, :], prev)
        y = y + prev * w_ref[CONV_WIDTH - 1 - shift:CONV_WIDTH - shift, :]
    o_ref[...] = (cb_ref[...] * y).astype(o_ref.dtype)
    tail_ref[...] = uu[uu.shape[0] - 8:, :]


def _gla_body(u_ref, wg_ref, bg_ref, gn_ref, ctril_ref, cw_ref, *rest):
    cols = lambda col, width: u_ref.at[:, col:col + width]
    q_ref, k_ref = cols(COL_GQ, GLA_QK), cols(COL_GK, GLA_QK)
    v_ref, gr_ref, glr_ref = cols(COL_GV, GLA_V), cols(COL_GR, GLA_V), cols(COL_GLR, LANES)
    cin_ref, cb_ref, cc_ref = cols(COL_CIN, CONV_CH), cols(COL_CB, CONV_CH), cols(COL_CC, CONV_CH)
    _gla_tile(q_ref, k_ref, v_ref, gr_ref, glr_ref, wg_ref, bg_ref, gn_ref, ctril_ref,
              cin_ref, cb_ref, cc_ref, cw_ref, *rest)


def _gla_tile(q_ref, k_ref, v_ref, gr_ref, glr_ref, wg_ref, bg_ref, gn_ref, ctril_ref,
              cin_ref, cb_ref, cc_ref, cw_ref, *rest):
    n_cast = (len(rest) - 4) // 2
    o_ref, cv_ref = rest[n_cast:n_cast + 2]
    s_ref, tail_ref = rest[-2:]
    _side_cast(rest[:n_cast], rest[n_cast + 2:-2])

    @pl.when(pl.program_id(1) == 0)
    def _():
        s_ref[...] = jnp.zeros_like(s_ref)
        tail_ref[...] = jnp.zeros_like(tail_ref)

    _short_conv(cin_ref, cb_ref, cc_ref, cw_ref, cv_ref, tail_ref)

    C = GLA_CHUNK
    row = lax.broadcasted_iota(jnp.int32, (C, C), 0)
    col = lax.broadcasted_iota(jnp.int32, (C, C), 1)
    tril = row >= col
    n_chunks = TC_GLA // C

    xg = _dot3(glr_ref[...], wg_ref[...]) + bg_ref[...]
    la_all = (jnp.minimum(xg, 0.0) - jnp.log(1.0 + jnp.exp(-jnp.abs(xg)))) * (1.0 / 16.0)

    la_hi, la_lo = _split_bf16(la_all)
    cum_all = _dot(ctril_ref[...], la_hi) + _dot(ctril_ref[...], la_lo)
    totals = jnp.concatenate([cum_all[(c + 1) * C - 1:(c + 1) * C, :] for c in range(n_chunks)]
                             + [jnp.zeros((LANES - n_chunks, GLA_QK), F32)], axis=0)
    decay_cols = jnp.exp(totals.T)

    for c in range(n_chunks):
        rows = slice(c * C, (c + 1) * C)
        cum = cum_all[rows]
        last = cum[C - 1:C, :]
        q = q_ref[rows, :]
        k = k_ref[rows, :]
        qt = (q * jnp.exp(cum) * (GLA_DK ** -0.5)).astype(BF16)
        kt = (k * jnp.exp(-cum)).astype(BF16)
        kl_t = (k * jnp.exp(last - cum)).T.astype(BF16)
        for h in range(GLA_HEADS):
            sl = slice(h * GLA_DK, (h + 1) * GLA_DK)
            vs = slice(h * GLA_DV, (h + 1) * GLA_DV)
            vh = v_ref[rows, vs].astype(BF16)
            state = s_ref[h]
            st_hi, st_lo = _split_bf16(state)
            sc = jnp.where(tril, _dot_nt(qt[:, sl], kt[:, sl]), 0.0).astype(BF16)
            o = _dot(qt[:, sl], st_hi) + _dot(qt[:, sl], st_lo) + _dot(sc, vh)
            decay = jnp.broadcast_to(decay_cols[sl, c:c + 1], state.shape)
            s_ref[h] = decay * state + _dot(kl_t[sl, :], vh)
            g = gr_ref[rows, vs]
            o_ref[rows, vs] = (_rms(o, gn_ref[...]) * (g * _sigmoid(g))).astype(o_ref.dtype)


def _gla_conv(u, wg, bg, gn, conv_w, side_casts=()):
    nj = SEQ // TC_GLA
    row = lambda b, j: b * nj + j
    full = lambda a: pl.BlockSpec(a.shape, lambda b, j: (0, 0))
    cast_in, cast_out, cast_shapes = _side_cast_io(side_casts, BATCH * nj, row)
    t = np.arange(TC_GLA)
    same_chunk = (t[:, None] // GLA_CHUNK) == (t[None, :] // GLA_CHUNK)
    chunk_tril = jnp.asarray(same_chunk & (t[:, None] >= t[None, :]), BF16)
    go, cv, *casts = pl.pallas_call(
        _gla_body,
        grid=(BATCH, nj),
        in_specs=[pl.BlockSpec((TC_GLA, U_COLS), lambda b, j: (row(b, j), 0)),
                  full(wg), full(bg), full(gn), full(chunk_tril), full(conv_w)] + cast_in,
        out_specs=[pl.BlockSpec((TC_GLA, GLA_V), lambda b, j: (row(b, j), 0)),
                   pl.BlockSpec((TC_GLA, CONV_CH), lambda b, j: (row(b, j), 0))] + cast_out,
        out_shape=[jax.ShapeDtypeStruct((TOKENS, GLA_V), BF16), jax.ShapeDtypeStruct((TOKENS, CONV_CH), BF16)]
                  + cast_shapes,
        scratch_shapes=[pltpu.VMEM((GLA_HEADS, GLA_DK, GLA_DV), F32), pltpu.VMEM((8, CONV_CH), F32)],
        compiler_params=_params(("arbitrary", "arbitrary")),
        name="gla_conv",
    )(u, wg, bg, gn, chunk_tril, conv_w, *[w for w, _ in side_casts])
    return go, cv, casts


def _outproj_body(*refs, route):
    (o1, o4, o16, l1, l4, l16, go_ref, cv_ref, x_ref, w_ref, g_ref) = refs[:11]
    perm_ref = refs[-1]
    if route:
        wr_ref, xo_ref, ho_ref, route_ref, counts_ref, carry_ref = refs[11:-1]
    else:
        xo_ref, ho_ref = refs[11:-1]

    def token_order(ref, slot):
        dilation, rows, _ = ref.shape
        if dilation == 1:
            return ref[0].astype(F32)
        tiles = range(ATT_WIDTH // LANES)
        for r in range(dilation):
            val = ref[r].astype(F32)
            for t in tiles:
                perm_ref[slot, t, pl.ds(r, rows, stride=dilation), :] = val[:, t * LANES:(t + 1) * LANES]
        return jnp.concatenate([perm_ref[slot, t] for t in tiles], axis=-1)

    la, lb, lc = token_order(l1, 0), token_order(l4, 0), token_order(l16, 1)
    oa, ob, oc = token_order(o1, 0), token_order(o4, 2), token_order(o16, 3)
    m = jnp.maximum(jnp.maximum(la, lb), lc)
    ea, eb, ec = jnp.exp(la - m), jnp.exp(lb - m), jnp.exp(lc - m)
    att = (ea * oa + eb * ob + ec * oc) / (ea + eb + ec)
    y = (x_ref[...]
         + _dot(att.astype(BF16), w_ref[0:ATT_WIDTH, :])
         + _dot(go_ref[...], w_ref[ATT_WIDTH:ATT_WIDTH + GLA_V, :])
         + _dot(cv_ref[...], w_ref[ATT_WIDTH + GLA_V:MIX_WIDTH, :]))
    xo_ref[...] = y
    hf = _rms(y, g_ref[...])
    ho_ref[...] = hf.astype(ho_ref.dtype)
    if route:
        @pl.when(pl.program_id(0) == 0)
        def _():
            carry_ref[...] = jnp.zeros_like(carry_ref)

        tm = hf.shape[0]
        ne = N_EXPERTS
        hf_hi, hf_lo = _split_bf16(hf)
        part = _dot_nt(wr_ref[0], hf_hi) + _dot_nt(wr_ref[1], hf_lo)
        logits = part[0:ne] + part[ne:2 * ne]
        eidx = lax.broadcasted_iota(jnp.int32, logits.shape, 0).astype(F32)
        v1 = jnp.max(logits, axis=0, keepdims=True)
        i1 = jnp.min(jnp.where(logits == v1, eidx, float(ne)), axis=0, keepdims=True)
        lg2 = jnp.where(eidx == i1, -jnp.inf, logits)
        v2 = jnp.max(lg2, axis=0, keepdims=True)
        i2 = jnp.min(jnp.where(lg2 == v2, eidx, float(ne)), axis=0, keepdims=True)
        e2 = jnp.exp(v2 - v1)
        w1 = 1.0 / (1.0 + e2)
        w2 = e2 * w1
        sel1 = eidx == i1
        sel2 = eidx == i2
        onehot = jnp.where(sel1, 1.0, jnp.where(sel2, 1.0, 0.0))
        tri = (lax.broadcasted_iota(jnp.int32, (tm, tm), 0) <= lax.broadcasted_iota(jnp.int32, (tm, tm), 1))
        onehot16 = jnp.concatenate([onehot, jnp.zeros_like(onehot)], axis=0).astype(BF16)
        csum = _dot(onehot16, jnp.where(tri, 1.0, 0.0).astype(BF16))[0:ne]
        carry = carry_ref[:, 0:1]
        rank = csum - onehot + carry
        r1 = jnp.sum(jnp.where(sel1, rank, 0.0), axis=0, keepdims=True)
        r2 = jnp.sum(jnp.where(sel2, rank, 0.0), axis=0, keepdims=True)
        total = jnp.broadcast_to(carry + csum[:, tm - 1:tm], carry_ref.shape)
        carry_ref[...] = total
        counts_ref[...] = total
        rows = {ROUTE_I1: i1, ROUTE_I2: i2, ROUTE_W1: w1, ROUTE_W2: w2, ROUTE_R1: r1, ROUTE_R2: r2}
        zero = jnp.zeros_like(i1)
        route_ref[...] = jnp.concatenate([rows.get(r, zero) for r in range(8)], axis=0)


def _outproj(att, go, cv, x, w, g, w_router=None):
    route = w_router is not None
    tm = TM_PROJ
    tile = lambda cols: pl.BlockSpec((tm, cols), lambda i: (i, 0))
    full = lambda a: pl.BlockSpec(a.shape, lambda i: (0, 0))
    (o1, l1), (o4, l4), (o16, l16) = att
    args = [o1, o4, o16, l1, l4, l16, go, cv, x, w, g]
    att_specs = [_subseq_spec(d, ATT_WIDTH) for _, d in DILATED_PATTERNS]
    in_specs = att_specs * 2 + [tile(GLA_V), tile(CONV_CH), tile(D_MODEL), full(w), full(g)]
    out_specs = [tile(D_MODEL), tile(D_MODEL)]
    out_shape = [jax.ShapeDtypeStruct((TOKENS, D_MODEL), F32),
                 jax.ShapeDtypeStruct((TOKENS, D_MODEL), F32 if route else BF16)]
    scratch = []
    if route:
        args.append(w_router)
        in_specs.append(pl.BlockSpec(w_router.shape, lambda i: (0, 0, 0)))
        out_specs += [pl.BlockSpec((8, tm), lambda i: (0, i)), pl.BlockSpec((N_EXPERTS, LANES), lambda i: (0, 0))]
        out_shape += [jax.ShapeDtypeStruct((8, TOKENS), F32), jax.ShapeDtypeStruct((N_EXPERTS, LANES), F32)]
        scratch = [pltpu.VMEM((N_EXPERTS, LANES), F32)]
    scratch.append(pltpu.VMEM((4, ATT_WIDTH // LANES, tm, LANES), F32))
    return pl.pallas_call(
        functools.partial(_outproj_body, route=route),
        grid=(TOKENS // tm,),
        in_specs=in_specs,
        out_specs=out_specs,
        out_shape=out_shape,
        scratch_shapes=scratch,
        compiler_params=_params(("arbitrary",) if route else ("parallel",)),
        name="outproj_route" if route else "outproj",
    )(*args)


def _swiglu_accumulate(h, w1_ref, w3_ref, w2_ref, acc_ref, tf):
    for c0 in range(0, tf, FFN_SUB):
        c1 = min(c0 + FFN_SUB, tf)
        a = _dot(h, w1_ref[:, c0:c1])
        b = _dot(h, w3_ref[:, c0:c1])
        act = a * _sigmoid(a) * b
        acc_ref[...] += _dot(act.astype(BF16), w2_ref[c0:c1, :])


def _ffn_body(x_ref, h_ref, w1_ref, w3_ref, w2_ref, *rest, tf):
    n_cast = (len(rest) - 1) // 2
    o_ref = rest[n_cast]

    @pl.when(pl.program_id(1) == 0)
    def _():
        o_ref[...] = x_ref[...]

    _swiglu_accumulate(h_ref[...], w1_ref, w3_ref, w2_ref, o_ref, tf)
    _side_cast(rest[:n_cast], rest[n_cast + 1:])


def _ffn(x, h, w1, w3, w2, *, side_casts=()):
    nj, _, tf = w1.shape
    tm = TM_FFN
    tile = lambda cols: pl.BlockSpec((tm, cols), lambda i, j: (i, 0))
    cast_in, cast_out, cast_shapes = _side_cast_io(side_casts, (TOKENS // tm) * nj, lambda i, j: i * nj + j)
    res = pl.pallas_call(
        functools.partial(_ffn_body, tf=tf),
        grid=(TOKENS // tm, nj),
        in_specs=[tile(D_MODEL), tile(D_MODEL),
                  pl.BlockSpec((None, D_MODEL, tf), lambda i, j: (j, 0, 0)),
                  pl.BlockSpec((None, D_MODEL, tf), lambda i, j: (j, 0, 0)),
                  pl.BlockSpec((tf, D_MODEL), lambda i, j: (j, 0))] + cast_in,
        out_specs=[tile(D_MODEL)] + cast_out,
        out_shape=[jax.ShapeDtypeStruct((TOKENS, D_MODEL), F32)] + cast_shapes,
        compiler_params=_params(("parallel", "arbitrary")),
        name="dense_ffn",
    )(x, h, w1, w3, w2, *[w for w, _ in side_casts])
    return res[0], res[1:]


ROW_TILE = D_MODEL // LANES


def _to_row_tiled(dst_ref, lead, val):
    rows = val.shape[0]
    for s in range(ROW_TILE):
        dst_ref[(*lead, pl.ds(s, rows, stride=ROW_TILE), slice(None))] = val[:, s * LANES:(s + 1) * LANES]


def _from_row_tiled(src_ref, lead, rows):
    return jnp.concatenate([src_ref[(*lead, pl.ds(s, rows, stride=ROW_TILE), slice(None))]
                            for s in range(ROW_TILE)], axis=-1)


def _row_tile(idx):
    return pl.ds(pl.multiple_of(idx * ROW_TILE, ROW_TILE), ROW_TILE)


def _dispatch_body(pos_ref, last_tile_ref, h_ref, xs_hbm, stage_ref, zero_ref, sem, zero_sem):
    i = pl.program_id(0)
    n = pl.num_programs(0)
    tm = DISPATCH_CHUNK
    slot = i % 2

    def drain(s):
        for _ in range(2):
            pltpu.make_async_copy(stage_ref.at[s], xs_hbm.at[pl.ds(0, tm * ROW_TILE)], sem.at[s]).wait()

    @pl.when(i == 0)
    def _():
        zero_ref[...] = jnp.zeros_like(zero_ref)

        def zero_copy(e):
            start = pl.multiple_of(last_tile_ref[e] * (TM_MOE * ROW_TILE), TM_MOE * ROW_TILE)
            return pltpu.make_async_copy(zero_ref, xs_hbm.at[pl.ds(start, TM_MOE * ROW_TILE)], zero_sem)

        for e in range(2 * N_EXPERTS):
            @pl.when(last_tile_ref[e] >= 0)
            def _():
                zero_copy(e).start()
        for e in range(2 * N_EXPERTS):
            @pl.when(last_tile_ref[e] >= 0)
            def _():
                zero_copy(e).wait()

    @pl.when(i >= 2)
    def _():
        drain(slot)

    _to_row_tiled(stage_ref, (slot,), h_ref[...])

    def body(t, carry):
        for k in range(2):
            dst = pos_ref[2 * (i * tm + t) + k]
            pltpu.make_async_copy(stage_ref.at[slot, _row_tile(t)], xs_hbm.at[_row_tile(dst)],
                                  sem.at[slot]).start(priority=k)
        return carry
    lax.fori_loop(0, tm, body, 0, unroll=8)

    @pl.when(i == n - 1)
    def _():
        drain(1 - slot)
        drain(slot)


def _dispatch(pos, last_tile, h):
    tm = DISPATCH_CHUNK
    return pl.pallas_call(
        _dispatch_body,
        grid=(TOKENS // tm,),
        in_specs=[pl.BlockSpec(memory_space=pltpu.SMEM),
                  pl.BlockSpec(memory_space=pltpu.SMEM),
                  pl.BlockSpec((tm, D_MODEL), lambda i: (i, 0))],
        out_specs=pl.BlockSpec(memory_space=pl.ANY),
        out_shape=jax.ShapeDtypeStruct((N_SORTED * ROW_TILE, LANES), F32),
        scratch_shapes=[pltpu.VMEM((2, tm * ROW_TILE, LANES), F32),
                        pltpu.VMEM((TM_MOE * ROW_TILE, LANES), F32),
                        pltpu.SemaphoreType.DMA((2,)), pltpu.SemaphoreType.DMA(())],
        compiler_params=_params(("arbitrary",)),
        name="moe_dispatch",
    )(pos, last_tile, h)


def _gffn_body(te_ref, nu_ref, xs_ref, w1_ref, w3_ref, w2_ref, o_ref, hb_ref, acc_ref):
    del te_ref
    i = pl.program_id(0)
    j = pl.program_id(1)

    used = i < nu_ref[0]

    @pl.when(j == 0)
    def _():
        acc_ref[...] = jnp.zeros_like(acc_ref)

    @pl.when(used & (j == 0))
    def _():
        hb_ref[...] = _from_row_tiled(xs_ref, (), TM_MOE).astype(BF16)

    @pl.when(used)
    def _():
        _swiglu_accumulate(hb_ref[...], w1_ref, w3_ref, w2_ref, acc_ref, TF_MOE)

    @pl.when(j == pl.num_programs(1) - 1)
    def _():
        _to_row_tiled(o_ref, (), acc_ref[...])


def _grouped_ffn(tile_expert, n_used, xs, w1, w3, w2):
    nj = FFN_EXPERT // TF_MOE
    col = lambda i, j, nu: jnp.where(i < nu[0], j, nj - 1)
    grid_spec = pltpu.PrefetchScalarGridSpec(
        num_scalar_prefetch=2,
        grid=(N_TILES_MOE, nj),
        in_specs=[
            pl.BlockSpec((TM_MOE * ROW_TILE, LANES), lambda i, j, te, nu: (jnp.minimum(i, nu[0] - 1), 0)),
            pl.BlockSpec((None, None, D_MODEL, TF_MOE), lambda i, j, te, nu: (te[i], col(i, j, nu), 0, 0)),
            pl.BlockSpec((None, None, D_MODEL, TF_MOE), lambda i, j, te, nu: (te[i], col(i, j, nu), 0, 0)),
            pl.BlockSpec((None, TF_MOE, D_MODEL), lambda i, j, te, nu: (te[i], col(i, j, nu), 0)),
        ],
        out_specs=pl.BlockSpec((TM_MOE * ROW_TILE, LANES), lambda i, j, te, nu: (i, 0)),
        scratch_shapes=[pltpu.VMEM((TM_MOE, D_MODEL), BF16), pltpu.VMEM((TM_MOE, D_MODEL), F32)],
    )
    return pl.pallas_call(
        _gffn_body,
        grid_spec=grid_spec,
        out_shape=jax.ShapeDtypeStruct((N_SORTED * ROW_TILE, LANES), F32),
        compiler_params=_params(("arbitrary", "arbitrary")),
        name="moe_ffn",
    )(tile_expert, n_used, xs, w1, w3, w2)


def _combine_body(pos_ref, x_ref, gate_ref, g_ref, ys_hbm, o_ref, buf_ref, sem):
    i = pl.program_id(0)
    n = pl.num_programs(0)
    tm = TM_COMBINE

    def issue(tile, slot):
        def body(t, carry):
            for k in range(2):
                src = pos_ref[2 * (tile * tm + t) + k]
                pltpu.make_async_copy(ys_hbm.at[_row_tile(src)], buf_ref.at[slot, k, _row_tile(t)],
                                      sem.at[slot]).start(priority=k)
            return carry
        lax.fori_loop(0, tm, body, 0, unroll=8)

    @pl.when(i == 0)
    def _():
        issue(0, 0)

    @pl.when(i + 1 < n)
    def _():
        issue(i + 1, (i + 1) % 2)

    slot = i % 2
    for k in range(2):
        pltpu.make_async_copy(ys_hbm.at[pl.ds(0, tm * ROW_TILE)], buf_ref.at[slot, k], sem.at[slot]).wait()
    w1 = gate_ref[:, 0:1]
    w2 = gate_ref[:, 1:2]
    y = x_ref[...] + w1 * _from_row_tiled(buf_ref, (slot, 0), tm) + w2 * _from_row_tiled(buf_ref, (slot, 1), tm)
    o_ref[...] = _rms(y, g_ref[...])


def _combine(pos, x, gates, g, ys):
    tm = TM_COMBINE
    return pl.pallas_call(
        _combine_body,
        grid=(TOKENS // tm,),
        in_specs=[pl.BlockSpec(memory_space=pltpu.SMEM),
                  pl.BlockSpec((tm, D_MODEL), lambda i: (i, 0)),
                  pl.BlockSpec((tm, 2), lambda i: (i, 0)),
                  pl.BlockSpec((1, D_MODEL), lambda i: (0, 0)),
                  pl.BlockSpec(memory_space=pl.ANY)],
        out_specs=pl.BlockSpec((tm, D_MODEL), lambda i: (i, 0)),
        out_shape=jax.ShapeDtypeStruct((TOKENS, D_MODEL), F32),
        scratch_shapes=[pltpu.VMEM((2, 2, tm * ROW_TILE, LANES), F32), pltpu.SemaphoreType.DMA((2,))],
        compiler_params=_params(("arbitrary",)),
        name="moe_combine",
    )(pos, x, gates, g, ys)


def _routing_tables(route, counts):
    cnt = counts[:, 0].astype(jnp.int32)
    tiles = (cnt + TM_MOE - 1) // TM_MOE
    tile_end = jnp.cumsum(tiles)
    tile_start = tile_end - tiles
    n_used = tile_end[-1]
    expert = route[ROUTE_I1:ROUTE_I2 + 1].astype(jnp.int32)
    rank = route[ROUTE_R1:ROUTE_R2 + 1].astype(jnp.int32)
    group_start = jnp.sum(jnp.where(expert[..., None] == jnp.arange(N_EXPERTS), tile_start * TM_MOE, 0), axis=-1)
    pos = (group_start + rank).T
    tile_id = jnp.minimum(jnp.arange(N_TILES_MOE, dtype=jnp.int32), n_used - 1)
    tile_expert = jnp.sum(tile_id[:, None] >= tile_end[None, :], axis=1).astype(jnp.int32)
    last_tile = jnp.where(tiles > 0, tile_end - 1, -1)
    spare = n_used + jnp.arange(N_EXPERTS)
    zero_tiles = jnp.concatenate([last_tile, jnp.where(spare < N_TILES_MOE, spare, -1)]).astype(jnp.int32)
    return pos.reshape(2 * TOKENS), tile_expert, n_used.reshape(1), zero_tiles


def _moe(x, h, route, counts, w1, w3, w2, g_final):
    pos, tile_expert, n_used, last_tile = _routing_tables(route, counts)
    xs = _dispatch(pos, last_tile, h)
    ys = _grouped_ffn(tile_expert, n_used, xs, w1, w3, w2)
    return _combine(pos, x, route[ROUTE_W1:ROUTE_W2 + 1].T, g_final, ys)


def _prep_w_in(w):
    aq, ak, av, gq, gk, gv, gr, glr, c_in, c_b, c_c = jnp.split(w, np.cumsum(SPLIT_SIZES)[:-1].tolist(), axis=1)
    pad = jnp.zeros((D_MODEL, LANES - GLA_RANK), w.dtype)
    return jnp.concatenate([aq, ak, av, gv, gr, gq, gk, c_in, c_b, c_c, glr, pad], axis=1).astype(BF16)


def _prep_router(w):
    wt = w.T
    hi = wt.astype(BF16)
    lo = (wt - hi.astype(F32)).astype(BF16)
    return jnp.stack([jnp.concatenate([hi, lo]), jnp.concatenate([hi, jnp.zeros_like(hi)])])


def kernel(x, w_mix_in, w_mix_out, g_mix, rel_bias, gla_w_gate, gla_b_gate, gla_g_norm, conv_w,
           g_ffn, ffn_w1, ffn_w3, ffn_w2, moe_router, moe_w1, moe_w3, moe_w2, g_final):
    assert DEPTH == 2
    x = x.reshape(TOKENS, D_MODEL)
    up_job = lambda w, tf: (w, w.shape[-1] // tf)
    down = lambda c: c.reshape(c.shape[0], c.shape[2], c.shape[3])
    gla_jobs = {0: up_job(moe_w1[0], TF_MOE), 1: up_job(moe_w3[0], TF_MOE)}
    experts = {}
    for layer in range(DEPTH):
        u, *qkvs = _inproj(x, g_mix[layer].reshape(1, D_MODEL), _prep_w_in(w_mix_in[layer]))
        dense_jobs = [up_job(ffn_w1[:1], TF_DENSE), up_job(ffn_w3[:1], TF_DENSE), (ffn_w2[:1], 1)] if layer == 0 else []
        att, dense_bf16 = _attention(qkvs, rel_bias, dense_jobs)
        if layer == 0:
            ffn_bf16 = (dense_bf16[0][0], dense_bf16[1][0], down(dense_bf16[2])[0])
        wg = jnp.pad(gla_w_gate[layer], ((0, LANES - GLA_RANK), (0, 0)))
        go, cv, (experts[layer],) = _gla_conv(u, wg, gla_b_gate[layer].reshape(1, GLA_QK),
                                              gla_g_norm[layer].reshape(1, GLA_DV),
                                              jnp.pad(conv_w[layer], ((0, 8 - CONV_WIDTH), (0, 0))),
                                              side_casts=[gla_jobs[layer]])
        w_out = w_mix_out[layer].astype(BF16)
        g2 = g_ffn[layer].reshape(1, D_MODEL)
        i = layer // 2
        if layer % 2 == 0:
            x, h = _outproj(att, go, cv, x, w_out, g2)
            x, (cast,) = _ffn(x, h, *ffn_bf16, side_casts=[(moe_w2[i], 1)])
            experts["w2"] = down(cast)
        else:
            x, h, route, counts = _outproj(att, go, cv, x, w_out, g2, _prep_router(moe_router[i]))
            x = _moe(x, h, route, counts, experts[0], experts[1], experts["w2"], g_final.reshape(1, D_MODEL))
    return x.reshape(BATCH, SEQ, D_MODEL)
```

```python
import functools
import math

import jax
import jax.numpy as jnp
import numpy as np
from jax import lax
from jax.experimental import pallas as pl
from jax.experimental.pallas import tpu as pltpu

F32 = jnp.float32
BF16 = jnp.bfloat16

D_MODEL = 1024
BATCH = 8
SEQ = 2048
TOKENS = BATCH * SEQ
DEPTH = 2
EPS = 1e-6

HEAD_DIM = 64
ATT_HEADS = 4
ATT_WIDTH = ATT_HEADS * HEAD_DIM
DILATED_PATTERNS = ((128, 1), (512, 4), (2048, 16))
ATT_BLOCK = 128
REL_BUCKETS = 32
REL_MAX_DISTANCE = 2048

GLA_HEADS = 4
GLA_DK = 64
GLA_DV = 128
GLA_RANK = 16
GLA_CHUNK = 64
GLA_QK = GLA_HEADS * GLA_DK
GLA_V = GLA_HEADS * GLA_DV

CONV_CH = 256
CONV_WIDTH = 3
MIX_WIDTH = ATT_WIDTH + GLA_V + CONV_CH

SPLIT_SIZES = (ATT_WIDTH, ATT_WIDTH, ATT_WIDTH, GLA_QK, GLA_QK, GLA_V, GLA_V, GLA_RANK,
               CONV_CH, CONV_CH, CONV_CH)

FFN_DENSE = 2816
N_EXPERTS = 8
FFN_EXPERT = 3584

LANES = 128
MXU_WIDTH = 256
VMEM_LIMIT = 56 * 1024 * 1024

QKV_COLS = 3 * ATT_WIDTH
COL_GV, COL_GR, COL_GQ, COL_GK = 0, 512, 1024, 1280
COL_CIN, COL_CB, COL_CC, COL_GLR = 1536, 1792, 2048, 2304
U_COLS = COL_GLR + LANES

NEG_BIG = -1e30

ATT_UNROLL = 15
TM_PROJ = 512
TC_GLA = 512
TM_FFN = 512
TF_DENSE = 1408
FFN_SUB = 256
TM_MOE = 512
TF_MOE = 1792
N_TILES_MOE = 2 * TOKENS // TM_MOE + N_EXPERTS
N_SORTED = N_TILES_MOE * TM_MOE
DISPATCH_CHUNK = 512
TM_COMBINE = 512

ROUTE_I1, ROUTE_I2, ROUTE_W1, ROUTE_W2, ROUTE_R1, ROUTE_R2 = range(6)


def _params(sem):
    return pltpu.CompilerParams(dimension_semantics=sem, vmem_limit_bytes=VMEM_LIMIT)


def _split_bf16(a):
    hi = a.astype(BF16)
    lo = (a - hi.astype(F32)).astype(BF16)
    return hi, lo


def _dot(a, b):
    return jnp.dot(a, b, preferred_element_type=F32)


def _dot3(a, b):
    a_hi, a_lo = _split_bf16(a)
    b_hi, b_lo = _split_bf16(b)
    return _dot(a_hi, b_hi) + _dot(a_lo, b_hi) + _dot(a_hi, b_lo)


def _dot_nt(a, b):
    return lax.dot_general(a, b, (((1,), (1,)), ((), ())), preferred_element_type=F32)


def _dot_tn(a, b):
    return lax.dot_general(a, b, (((0,), (0,)), ((), ())), preferred_element_type=F32)


def _rms(x, g):
    ms = jnp.mean(x * x, axis=-1, keepdims=True)
    return x * lax.rsqrt(ms + EPS) * g


def _sigmoid(x):
    return 1.0 / (1.0 + jnp.exp(-x))


def _side_cast(srcs, dsts):
    for src, dst in zip(srcs, dsts):
        width = dst.shape[-1]
        for s in range(dst.shape[0]):
            dst[s] = src[:, s * width:(s + 1) * width].astype(dst.dtype)


def _side_cast_io(jobs, steps, step_of):
    in_specs, out_specs, out_shapes = [], [], []
    for w, splits in jobs:
        g, r, c = w.shape
        rb = g * r // steps
        per_group = r // rb
        in_specs.append(pl.BlockSpec(
            (None, rb, c), lambda *ids, pg=per_group: (step_of(*ids) // pg, step_of(*ids) % pg, 0)))
        out_specs.append(pl.BlockSpec(
            (None, splits, rb, c // splits),
            lambda *ids, pg=per_group: (step_of(*ids) // pg, 0, step_of(*ids) % pg, 0)))
        out_shapes.append(jax.ShapeDtypeStruct((g, splits, r, c // splits), BF16))
    return in_specs, out_specs, out_shapes


def _inproj_body(x_ref, g_ref, w_ref, o_ref, *rest):
    qkv_refs, qkv_f32 = rest[:-1], rest[-1]
    h = _rms(x_ref[...], g_ref[...]).astype(BF16)
    for c0 in range(0, QKV_COLS, MXU_WIDTH):
        res = _dot(h, w_ref[:, c0:c0 + MXU_WIDTH])
        for t in range(MXU_WIDTH // LANES):
            qkv_f32[c0 // LANES + t] = res[:, t * LANES:(t + 1) * LANES]
    for (_, dilation), ref in zip(DILATED_PATTERNS, qkv_refs):
        for r in range(dilation):
            rows = pl.ds(r, TM_PROJ // dilation, stride=dilation)
            ref[r] = jnp.concatenate([qkv_f32[t, rows, :] for t in range(QKV_COLS // LANES)],
                                     axis=-1).astype(ref.dtype)
    for c0 in range(0, U_COLS, MXU_WIDTH):
        c1 = min(c0 + MXU_WIDTH, U_COLS)
        o_ref[:, c0:c1] = _dot(h, w_ref[:, QKV_COLS + c0:QKV_COLS + c1])


def _subseq_spec(dilation, cols):
    tiles = SEQ // TM_PROJ
    return pl.BlockSpec((None, dilation, TM_PROJ // dilation, cols), lambda i: (i // tiles, 0, i % tiles, 0))


def _inproj(x, g, w_all, layer):
    qkv_shapes = [jax.ShapeDtypeStruct((BATCH, d, SEQ // d, QKV_COLS), BF16) for _, d in DILATED_PATTERNS]
    return pl.pallas_call(
        _inproj_body,
        grid=(TOKENS // TM_PROJ,),
        in_specs=[
            pl.BlockSpec((TM_PROJ, D_MODEL), lambda i: (i, 0)),
            pl.BlockSpec((1, D_MODEL), lambda i: (0, 0)),
            pl.BlockSpec((None, D_MODEL, QKV_COLS + U_COLS), lambda i: (layer, 0, 0)),
        ],
        out_specs=[pl.BlockSpec((TM_PROJ, U_COLS), lambda i: (i, 0))]
                  + [_subseq_spec(d, QKV_COLS) for _, d in DILATED_PATTERNS],
        out_shape=[jax.ShapeDtypeStruct((TOKENS, U_COLS), F32)] + qkv_shapes,
        scratch_shapes=[pltpu.VMEM((QKV_COLS // LANES, TM_PROJ, LANES), F32)],
        compiler_params=_params(("parallel",)),
        name="inproj",
    )(x, g, w_all)


def _rel_bucket(dist):
    max_exact = REL_BUCKETS // 2
    d = jnp.maximum(dist, 0)
    log_ratio = jnp.log(jnp.maximum(d, 1).astype(F32) / max_exact) / math.log(REL_MAX_DISTANCE / max_exact)
    large = jnp.minimum(max_exact + (log_ratio * (REL_BUCKETS - max_exact)).astype(jnp.int32), REL_BUCKETS - 1)
    return jnp.where(d < max_exact, d, large)


def _bucket_table(window, dilation):
    span = window // dilation
    qi = jnp.arange(ATT_BLOCK)[:, None]
    kj = jnp.arange(2 * ATT_BLOCK)[None, :]
    sub_dist = qi - kj + ATT_BLOCK
    band = (sub_dist >= 0) & (sub_dist <= span)
    return jnp.where(band, _rel_bucket(sub_dist * dilation), -1).astype(jnp.int32)


def _attn_body(rb_ref, bidx_ref, qkv_ref, *rest, sub_blocks, unroll):
    n_cast = (len(rest) - 3) // 2
    o_ref, lse_ref = rest[n_cast:n_cast + 2]
    bias_ref = rest[-1]
    _side_cast(rest[:n_cast], rest[n_cast + 2:-1])
    nblk = SEQ // ATT_BLOCK

    @pl.when(pl.program_id(0) == 0)
    def _():
        bidx = bidx_ref[...]
        in_prev = lax.broadcasted_iota(jnp.int32, bidx.shape, 1) < ATT_BLOCK
        for h in range(ATT_HEADS):
            acc = jnp.full(bidx.shape, NEG_BIG, F32)
            for b in range(REL_BUCKETS):
                acc = jnp.where(bidx == b, rb_ref[b, h], acc)
            bias_ref[0, h] = acc
            bias_ref[1, h] = jnp.where(in_prev, NEG_BIG, acc)
            bias_ref[2, h] = jnp.concatenate([acc[:, ATT_BLOCK:], jnp.full_like(acc[:, ATT_BLOCK:], NEG_BIG)], axis=1)

    def block(n, first):
        if first:
            rows, krows, variant = slice(0, ATT_BLOCK), slice(0, 2 * ATT_BLOCK), 2
        else:
            r0 = pl.multiple_of(n * ATT_BLOCK, ATT_BLOCK)
            rows = pl.ds(r0, ATT_BLOCK)
            krows = pl.ds(r0 - ATT_BLOCK, 2 * ATT_BLOCK)
            if sub_blocks == nblk:
                variant = 0
            elif sub_blocks == 1:
                variant = 1
            else:
                variant = jnp.where(n % sub_blocks == 0, 1, 0)
        q = qkv_ref[rows, 0:ATT_WIDTH]
        kk = qkv_ref[krows, ATT_WIDTH:2 * ATT_WIDTH]
        vv = qkv_ref[krows, 2 * ATT_WIDTH:3 * ATT_WIDTH]
        q = q * jnp.asarray(HEAD_DIM ** -0.5, BF16)
        head_of_lane = lax.broadcasted_iota(jnp.int32, (ATT_BLOCK, ATT_WIDTH), 1) // HEAD_DIM
        ones = jnp.ones((kk.shape[0], LANES), BF16)
        num = den = mx = None
        for h in range(ATT_HEADS):
            mine = head_of_lane == h
            bias = bias_ref[variant, h]
            s = _dot_nt(jnp.where(mine, q, jnp.zeros_like(q)), kk) + bias
            m = jnp.max(s, axis=-1, keepdims=True)
            p = jnp.exp(s - m).astype(BF16)
            num_h = _dot(p, vv)
            den_h = jnp.tile(_dot(p, ones), (1, ATT_WIDTH // LANES))
            m_h = jnp.broadcast_to(m, (ATT_BLOCK, ATT_WIDTH))
            num = num_h if h == 0 else jnp.where(mine, num_h, num)
            den = den_h if h == 0 else jnp.where(mine, den_h, den)
            mx = m_h if h == 0 else jnp.where(mine, m_h, mx)
        o_ref[rows, :] = (num / den).astype(o_ref.dtype)
        lse_ref[rows, :] = mx + jnp.log(den)

    block(0, True)

    def loop_body(n, carry):
        block(n, False)
        return carry
    lax.fori_loop(1, nblk, loop_body, 0, unroll=unroll)


def _attention_pattern(ua, rel_bias, window, dilation, side_casts=()):
    L = SEQ // dilation
    shape = (BATCH, dilation, L, ATT_WIDTH)
    qkv_spec = pl.BlockSpec((None, SEQ, QKV_COLS), lambda b: (b, 0, 0))
    out_spec = pl.BlockSpec((None, SEQ, ATT_WIDTH), lambda b: (b, 0, 0))
    cast_in, cast_out, cast_shapes = _side_cast_io(side_casts, BATCH, lambda b: b)
    o, lse, *casts = pl.pallas_call(
        functools.partial(_attn_body, sub_blocks=L // ATT_BLOCK, unroll=ATT_UNROLL),
        grid=(BATCH,),
        in_specs=[
            pl.BlockSpec(memory_space=pltpu.SMEM),
            pl.BlockSpec((ATT_BLOCK, 2 * ATT_BLOCK), lambda b: (0, 0)),
            qkv_spec,
        ] + cast_in,
        out_specs=[out_spec, out_spec] + cast_out,
        out_shape=[jax.ShapeDtypeStruct((BATCH, SEQ, ATT_WIDTH), BF16),
                   jax.ShapeDtypeStruct((BATCH, SEQ, ATT_WIDTH), F32)] + cast_shapes,
        scratch_shapes=[pltpu.VMEM((3, ATT_HEADS, ATT_BLOCK, 2 * ATT_BLOCK), F32)],
        compiler_params=_params(("arbitrary",)),
        name=f"attn_d{dilation}",
    )(rel_bias, _bucket_table(window, dilation), ua.reshape(BATCH, SEQ, QKV_COLS), *[w for w, _ in side_casts])
    return (o.reshape(shape), lse.reshape(shape)), casts


def _attention(qkvs, rel_bias, side_casts=()):
    side_casts = list(side_casts) + [None] * (len(DILATED_PATTERNS) - len(side_casts))
    res = [_attention_pattern(ua, rel_bias, window, dilation, [] if w is None else [w])
           for ua, (window, dilation), w in zip(qkvs, DILATED_PATTERNS, side_casts)]
    return [r[0] for r in res], [c for r in res for c in r[1]]


def _short_conv(cin_ref, cb_ref, cc_ref, w_ref, o_ref, tail_ref):
    uu = cc_ref[...] * cin_ref[...]
    t = lax.broadcasted_iota(jnp.int32, uu.shape, 0)
    y = uu * w_ref[CONV_WIDTH - 1:CONV_WIDTH, :]
    for shift in range(1, CONV_WIDTH):
        prev = pltpu.roll(uu, shift, axis=0)
        for r in range(shift):
            prev = jnp.where(t == r, tail_ref[8 - shift + r:8 - shift + r + 1, :], prev)
        y = y + prev * w_ref[CONV_WIDTH - 1 - shift:CONV_WIDTH - shift, :]
    o_ref[...] = (cb_ref[...] * y).astype(o_ref.dtype)
    tail_ref[...] = uu[uu.shape[0] - 8:, :]


def _gla_body(u_ref, wg_ref, bg_ref, gn_ref, ctril_ref, cw_ref, *rest):
    cols = lambda col, width: u_ref.at[:, col:col + width]
    q_ref, k_ref = cols(COL_GQ, GLA_QK), cols(COL_GK, GLA_QK)
    v_ref, gr_ref, glr_ref = cols(COL_GV, GLA_V), cols(COL_GR, GLA_V), cols(COL_GLR, LANES)
    cin_ref, cb_ref, cc_ref = cols(COL_CIN, CONV_CH), cols(COL_CB, CONV_CH), cols(COL_CC, CONV_CH)
    _gla_tile(q_ref, k_ref, v_ref, gr_ref, glr_ref, wg_ref, bg_ref, gn_ref, ctril_ref,
              cin_ref, cb_ref, cc_ref, cw_ref, *rest)


def _gla_tile(q_ref, k_ref, v_ref, gr_ref, glr_ref, wg_ref, bg_ref, gn_ref, ctril_ref,
              cin_ref, cb_ref, cc_ref, cw_ref, *rest):
    n_cast = (len(rest) - 4) // 2
    o_ref, cv_ref = rest[n_cast:n_cast + 2]
    s_ref, tail_ref = rest[-2:]
    _side_cast(rest[:n_cast], rest[n_cast + 2:-2])

    @pl.when(pl.program_id(1) == 0)
    def _():
        s_ref[...] = jnp.zeros_like(s_ref)
        tail_ref[...] = jnp.zeros_like(tail_ref)

    _short_conv(cin_ref, cb_ref, cc_ref, cw_ref, cv_ref, tail_ref)

    C = GLA_CHUNK
    row = lax.broadcasted_iota(jnp.int32, (C, C), 0)
    col = lax.broadcasted_iota(jnp.int32, (C, C), 1)
    tril = row >= col
    n_chunks = TC_GLA // C

    xg = _dot3(glr_ref[...], wg_ref[...]) + bg_ref[...]
    la_all = (jnp.minimum(xg, 0.0) - jnp.log(1.0 + jnp.exp(-jnp.abs(xg)))) * (1.0 / 16.0)

    la_hi, la_lo = _split_bf16(la_all)
    cum_all = _dot(ctril_ref[...], la_hi) + _dot(ctril_ref[...], la_lo)
    totals = jnp.concatenate([cum_all[(c + 1) * C - 1:(c + 1) * C, :] for c in range(n_chunks)]
                             + [jnp.zeros((LANES - n_chunks, GLA_QK), F32)], axis=0)
    decay_cols = jnp.exp(totals.T)

    for c in range(n_chunks):
        rows = slice(c * C, (c + 1) * C)
        cum = cum_all[rows]
        last = cum[C - 1:C, :]
        q = q_ref[rows, :]
        k = k_ref[rows, :]
        qt = (q * jnp.exp(cum) * (GLA_DK ** -0.5)).astype(BF16)
        kt = (k * jnp.exp(-cum)).astype(BF16)
        kl_t = (k * jnp.exp(last - cum)).T.astype(BF16)
        for h in range(GLA_HEADS):
            sl = slice(h * GLA_DK, (h + 1) * GLA_DK)
            vs = slice(h * GLA_DV, (h + 1) * GLA_DV)
            vh = v_ref[rows, vs].astype(BF16)
            state = s_ref[h]
            st_hi, st_lo = _split_bf16(state)
            sc = jnp.where(tril, _dot_nt(qt[:, sl], kt[:, sl]), 0.0).astype(BF16)
            o = _dot(qt[:, sl], st_hi) + _dot(qt[:, sl], st_lo) + _dot(sc, vh)
            decay = jnp.broadcast_to(decay_cols[sl, c:c + 1], state.shape)
            s_ref[h] = decay * state + _dot(kl_t[sl, :], vh)
            g = gr_ref[rows, vs]
            o_ref[rows, vs] = (_rms(o, gn_ref[...]) * (g * _sigmoid(g))).astype(o_ref.dtype)


def _gla_conv(u, wg, bg, gn, conv_w, side_casts=()):
    nj = SEQ // TC_GLA
    row = lambda b, j: b * nj + j
    full = lambda a: pl.BlockSpec(a.shape, lambda b, j: (0, 0))
    cast_in, cast_out, cast_shapes = _side_cast_io(side_casts, BATCH * nj, row)
    t = np.arange(TC_GLA)
    same_chunk = (t[:, None] // GLA_CHUNK) == (t[None, :] // GLA_CHUNK)
    chunk_tril = jnp.asarray(same_chunk & (t[:, None] >= t[None, :]), BF16)
    go, cv, *casts = pl.pallas_call(
        _gla_body,
        grid=(BATCH, nj),
        in_specs=[pl.BlockSpec((TC_GLA, U_COLS), lambda b, j: (row(b, j), 0)),
                  full(wg), full(bg), full(gn), full(chunk_tril), full(conv_w)] + cast_in,
        out_specs=[pl.BlockSpec((TC_GLA, GLA_V), lambda b, j: (row(b, j), 0)),
                   pl.BlockSpec((TC_GLA, CONV_CH), lambda b, j: (row(b, j), 0))] + cast_out,
        out_shape=[jax.ShapeDtypeStruct((TOKENS, GLA_V), BF16), jax.ShapeDtypeStruct((TOKENS, CONV_CH), BF16)]
                  + cast_shapes,
        scratch_shapes=[pltpu.VMEM((GLA_HEADS, GLA_DK, GLA_DV), F32), pltpu.VMEM((8, CONV_CH), F32)],
        compiler_params=_params(("arbitrary", "arbitrary")),
        name="gla_conv",
    )(u, wg, bg, gn, chunk_tril, conv_w, *[w for w, _ in side_casts])
    return go, cv, casts


def _outproj_body(*refs, route):
    (o1, o4, o16, l1, l4, l16, go_ref, cv_ref, x_ref, w_ref, g_ref) = refs[:11]
    perm_ref = refs[-1]
    if route:
        wr_ref, xo_ref, ho_ref, route_ref, counts_ref, carry_ref = refs[11:-1]
    else:
        xo_ref, ho_ref = refs[11:-1]

    def token_order(ref, slot):
        dilation, rows, _ = ref.shape
        if dilation == 1:
            return ref[0].astype(F32)
        tiles = range(ATT_WIDTH // LANES)
        for r in range(dilation):
            val = ref[r].astype(F32)
            for t in tiles:
                perm_ref[slot, t, pl.ds(r, rows, stride=dilation), :] = val[:, t * LANES:(t + 1) * LANES]
        return jnp.concatenate([perm_ref[slot, t] for t in tiles], axis=-1)

    la, lb, lc = token_order(l1, 0), token_order(l4, 0), token_order(l16, 1)
    oa, ob, oc = token_order(o1, 0), token_order(o4, 2), token_order(o16, 3)
    m = jnp.maximum(jnp.maximum(la, lb), lc)
    ea, eb, ec = jnp.exp(la - m), jnp.exp(lb - m), jnp.exp(lc - m)
    att = (ea * oa + eb * ob + ec * oc) / (ea + eb + ec)
    y = (x_ref[...]
         + _dot(att.astype(BF16), w_ref[0:ATT_WIDTH, :])
         + _dot(go_ref[...], w_ref[ATT_WIDTH:ATT_WIDTH + GLA_V, :])
         + _dot(cv_ref[...], w_ref[ATT_WIDTH + GLA_V:MIX_WIDTH, :]))
    xo_ref[...] = y
    hf = _rms(y, g_ref[...])
    ho_ref[...] = hf.astype(ho_ref.dtype)
    if route:
        @pl.when(pl.program_id(0) == 0)
        def _():
            carry_ref[...] = jnp.zeros_like(carry_ref)

        tm = hf.shape[0]
        ne = N_EXPERTS
        hf_hi, hf_lo = _split_bf16(hf)
        part = _dot_nt(wr_ref[0], hf_hi) + _dot_nt(wr_ref[1], hf_lo)
        logits = part[0:ne] + part[ne:2 * ne]
        eidx = lax.broadcasted_iota(jnp.int32, logits.shape, 0).astype(F32)
        v1 = jnp.max(logits, axis=0, keepdims=True)
        i1 = jnp.min(jnp.where(logits == v1, eidx, float(ne)), axis=0, keepdims=True)
        lg2 = jnp.where(eidx == i1, -jnp.inf, logits)
        v2 = jnp.max(lg2, axis=0, keepdims=True)
        i2 = jnp.min(jnp.where(lg2 == v2, eidx, float(ne)), axis=0, keepdims=True)
        e2 = jnp.exp(v2 - v1)
        w1 = 1.0 / (1.0 + e2)
        w2 = e2 * w1
        sel1 = eidx == i1
        sel2 = eidx == i2
        onehot = jnp.where(sel1, 1.0, jnp.where(sel2, 1.0, 0.0))
        tri = (lax.broadcasted_iota(jnp.int32, (tm, tm), 0) <= lax.broadcasted_iota(jnp.int32, (tm, tm), 1))
        onehot16 = jnp.concatenate([onehot, jnp.zeros_like(onehot)], axis=0).astype(BF16)
        csum = _dot(onehot16, jnp.where(tri, 1.0, 0.0).astype(BF16))[0:ne]
        carry = carry_ref[:, 0:1]
        rank = csum - onehot + carry
        r1 = jnp.sum(jnp.where(sel1, rank, 0.0), axis=0, keepdims=True)
        r2 = jnp.sum(jnp.where(sel2, rank, 0.0), axis=0, keepdims=True)
        total = jnp.broadcast_to(carry + csum[:, tm - 1:tm], carry_ref.shape)
        carry_ref[...] = total
        counts_ref[...] = total
        rows = {ROUTE_I1: i1, ROUTE_I2: i2, ROUTE_W1: w1, ROUTE_W2: w2, ROUTE_R1: r1, ROUTE_R2: r2}
        zero = jnp.zeros_like(i1)
        route_ref[...] = jnp.concatenate([rows.get(r, zero) for r in range(8)], axis=0)


def _outproj(att, go, cv, x, w, g, w_router=None):
    route = w_router is not None
    tm = TM_PROJ
    tile = lambda cols: pl.BlockSpec((tm, cols), lambda i: (i, 0))
    full = lambda a: pl.BlockSpec(a.shape, lambda i: (0, 0))
    (o1, l1), (o4, l4), (o16, l16) = att
    args = [o1, o4, o16, l1, l4, l16, go, cv, x, w, g]
    att_specs = [_subseq_spec(d, ATT_WIDTH) for _, d in DILATED_PATTERNS]
    in_specs = att_specs * 2 + [tile(GLA_V), tile(CONV_CH), tile(D_MODEL), full(w), full(g)]
    out_specs = [tile(D_MODEL), tile(D_MODEL)]
    out_shape = [jax.ShapeDtypeStruct((TOKENS, D_MODEL), F32),
                 jax.ShapeDtypeStruct((TOKENS, D_MODEL), F32 if route else BF16)]
    scratch = []
    if route:
        args.append(w_router)
        in_specs.append(pl.BlockSpec(w_router.shape, lambda i: (0, 0, 0)))
        out_specs += [pl.BlockSpec((8, tm), lambda i: (0, i)), pl.BlockSpec((N_EXPERTS, LANES), lambda i: (0, 0))]
        out_shape += [jax.ShapeDtypeStruct((8, TOKENS), F32), jax.ShapeDtypeStruct((N_EXPERTS, LANES), F32)]
        scratch = [pltpu.VMEM((N_EXPERTS, LANES), F32)]
    scratch.append(pltpu.VMEM((4, ATT_WIDTH // LANES, tm, LANES), F32))
    return pl.pallas_call(
        functools.partial(_outproj_body, route=route),
        grid=(TOKENS // tm,),
        in_specs=in_specs,
        out_specs=out_specs,
        out_shape=out_shape,
        scratch_shapes=scratch,
        compiler_params=_params(("arbitrary",) if route else ("parallel",)),
        name="outproj_route" if route else "outproj",
    )(*args)


def _swiglu_accumulate(h, w1_ref, w3_ref, w2_ref, acc_ref, tf):
    for c0 in range(0, tf, FFN_SUB):
        c1 = min(c0 + FFN_SUB, tf)
        a = _dot(h, w1_ref[:, c0:c1])
        b = _dot(h, w3_ref[:, c0:c1])
        act = a * _sigmoid(a) * b
        acc_ref[...] += _dot(act.astype(BF16), w2_ref[c0:c1, :])


def _ffn_body(x_ref, h_ref, w1_ref, w3_ref, w2_ref, *rest, tf):
    n_cast = (len(rest) - 1) // 2
    o_ref = rest[n_cast]

    @pl.when(pl.program_id(1) == 0)
    def _():
        o_ref[...] = x_ref[...]

    _swiglu_accumulate(h_ref[...], w1_ref, w3_ref, w2_ref, o_ref, tf)
    _side_cast(rest[:n_cast], rest[n_cast + 1:])


def _ffn(x, h, w1, w3, w2, *, side_casts=()):
    nj, _, tf = w1.shape
    tm = TM_FFN
    tile = lambda cols: pl.BlockSpec((tm, cols), lambda i, j: (i, 0))
    cast_in, cast_out, cast_shapes = _side_cast_io(side_casts, (TOKENS // tm) * nj, lambda i, j: i * nj + j)
    res = pl.pallas_call(
        functools.partial(_ffn_body, tf=tf),
        grid=(TOKENS // tm, nj),
        in_specs=[tile(D_MODEL), tile(D_MODEL),
                  pl.BlockSpec((None, D_MODEL, tf), lambda i, j: (j, 0, 0)),
                  pl.BlockSpec((None, D_MODEL, tf), lambda i, j: (j, 0, 0)),
                  pl.BlockSpec((tf, D_MODEL), lambda i, j: (j, 0))] + cast_in,
        out_specs=[tile(D_MODEL)] + cast_out,
        out_shape=[jax.ShapeDtypeStruct((TOKENS, D_MODEL), F32)] + cast_shapes,
        compiler_params=_params(("parallel", "arbitrary")),
        name="dense_ffn",
    )(x, h, w1, w3, w2, *[w for w, _ in side_casts])
    return res[0], res[1:]


ROW_TILE = D_MODEL // LANES


def _to_row_tiled(dst_ref, lead, val):
    rows = val.shape[0]
    for s in range(ROW_TILE):
        dst_ref[(*lead, pl.ds(s, rows, stride=ROW_TILE), slice(None))] = val[:, s * LANES:(s + 1) * LANES]


def _from_row_tiled(src_ref, lead, rows):
    return jnp.concatenate([src_ref[(*lead, pl.ds(s, rows, stride=ROW_TILE), slice(None))]
                            for s in range(ROW_TILE)], axis=-1)


def _row_tile(idx):
    return pl.ds(pl.multiple_of(idx * ROW_TILE, ROW_TILE), ROW_TILE)


def _dispatch_body(pos_ref, last_tile_ref, h_ref, xs_hbm, stage_ref, zero_ref, sem, zero_sem):
    i = pl.program_id(0)
    n = pl.num_programs(0)
    tm = DISPATCH_CHUNK
    slot = i % 2

    def drain(s):
        for _ in range(2):
            pltpu.make_async_copy(stage_ref.at[s], xs_hbm.at[pl.ds(0, tm * ROW_TILE)], sem.at[s]).wait()

    @pl.when(i == 0)
    def _():
        zero_ref[...] = jnp.zeros_like(zero_ref)

        def zero_copy(e):
            start = pl.multiple_of(last_tile_ref[e] * (TM_MOE * ROW_TILE), TM_MOE * ROW_TILE)
            return pltpu.make_async_copy(zero_ref, xs_hbm.at[pl.ds(start, TM_MOE * ROW_TILE)], zero_sem)

        for e in range(2 * N_EXPERTS):
            @pl.when(last_tile_ref[e] >= 0)
            def _():
                zero_copy(e).start()
        for e in range(2 * N_EXPERTS):
            @pl.when(last_tile_ref[e] >= 0)
            def _():
                zero_copy(e).wait()

    @pl.when(i >= 2)
    def _():
        drain(slot)

    _to_row_tiled(stage_ref, (slot,), h_ref[...])

    def body(t, carry):
        for k in range(2):
            dst = pos_ref[2 * (i * tm + t) + k]
            pltpu.make_async_copy(stage_ref.at[slot, _row_tile(t)], xs_hbm.at[_row_tile(dst)],
                                  sem.at[slot]).start(priority=k)
        return carry
    lax.fori_loop(0, tm, body, 0, unroll=8)

    @pl.when(i == n - 1)
    def _():
        drain(1 - slot)
        drain(slot)


def _dispatch(pos, last_tile, h):
    tm = DISPATCH_CHUNK
    return pl.pallas_call(
        _dispatch_body,
        grid=(TOKENS // tm,),
        in_specs=[pl.BlockSpec(memory_space=pltpu.SMEM),
                  pl.BlockSpec(memory_space=pltpu.SMEM),
                  pl.BlockSpec((tm, D_MODEL), lambda i: (i, 0))],
        out_specs=pl.BlockSpec(memory_space=pl.ANY),
        out_shape=jax.ShapeDtypeStruct((N_SORTED * ROW_TILE, LANES), F32),
        scratch_shapes=[pltpu.VMEM((2, tm * ROW_TILE, LANES), F32),
                        pltpu.VMEM((TM_MOE * ROW_TILE, LANES), F32),
                        pltpu.SemaphoreType.DMA((2,)), pltpu.SemaphoreType.DMA(())],
        compiler_params=_params(("arbitrary",)),
        name="moe_dispatch",
    )(pos, last_tile, h)


def _gffn_body(te_ref, nu_ref, xs_ref, w1_ref, w3_ref, w2_ref, o_ref, hb_ref, acc_ref):
    del te_ref
    i = pl.program_id(0)
    j = pl.program_id(1)

    used = i < nu_ref[0]

    @pl.when(j == 0)
    def _():
        acc_ref[...] = jnp.zeros_like(acc_ref)

    @pl.when(used & (j == 0))
    def _():
        hb_ref[...] = _from_row_tiled(xs_ref, (), TM_MOE).astype(BF16)

    @pl.when(used)
    def _():
        _swiglu_accumulate(hb_ref[...], w1_ref, w3_ref, w2_ref, acc_ref, TF_MOE)

    @pl.when(j == pl.num_programs(1) - 1)
    def _():
        _to_row_tiled(o_ref, (), acc_ref[...])


def _grouped_ffn(tile_expert, n_used, xs, w1, w3, w2):
    nj = FFN_EXPERT // TF_MOE
    col = lambda i, j, nu: jnp.where(i < nu[0], j, nj - 1)
    grid_spec = pltpu.PrefetchScalarGridSpec(
        num_scalar_prefetch=2,
        grid=(N_TILES_MOE, nj),
        in_specs=[
            pl.BlockSpec((TM_MOE * ROW_TILE, LANES), lambda i, j, te, nu: (jnp.minimum(i, nu[0] - 1), 0)),
            pl.BlockSpec((None, None, D_MODEL, TF_MOE), lambda i, j, te, nu: (te[i], col(i, j, nu), 0, 0)),
            pl.BlockSpec((None, None, D_MODEL, TF_MOE), lambda i, j, te, nu: (te[i], col(i, j, nu), 0, 0)),
            pl.BlockSpec((None, TF_MOE, D_MODEL), lambda i, j, te, nu: (te[i], col(i, j, nu), 0)),
        ],
        out_specs=pl.BlockSpec((TM_MOE * ROW_TILE, LANES), lambda i, j, te, nu: (i, 0)),
        scratch_shapes=[pltpu.VMEM((TM_MOE, D_MODEL), BF16), pltpu.VMEM((TM_MOE, D_MODEL), F32)],
    )
    return pl.pallas_call(
        _gffn_body,
        grid_spec=grid_spec,
        out_shape=jax.ShapeDtypeStruct((N_SORTED * ROW_TILE, LANES), F32),
        compiler_params=_params(("arbitrary", "arbitrary")),
        name="moe_ffn",
    )(tile_expert, n_used, xs, w1, w3, w2)


def _combine_body(pos_ref, x_ref, gate_ref, g_ref, ys_hbm, o_ref, buf_ref, sem):
    i = pl.program_id(0)
    n = pl.num_programs(0)
    tm = TM_COMBINE

    def issue(tile, slot):
        def body(t, carry):
            for k in range(2):
                src = pos_ref[2 * (tile * tm + t) + k]
                pltpu.make_async_copy(ys_hbm.at[_row_tile(src)], buf_ref.at[slot, k, _row_tile(t)],
                                      sem.at[slot]).start(priority=k)
            return carry
        lax.fori_loop(0, tm, body, 0, unroll=8)

    @pl.when(i == 0)
    def _():
        issue(0, 0)

    @pl.when(i + 1 < n)
    def _():
        issue(i + 1, (i + 1) % 2)

    slot = i % 2
    for k in range(2):
        pltpu.make_async_copy(ys_hbm.at[pl.ds(0, tm * ROW_TILE)], buf_ref.at[slot, k], sem.at[slot]).wait()
    w1 = gate_ref[:, 0:1]
    w2 = gate_ref[:, 1:2]
    y = x_ref[...] + w1 * _from_row_tiled(buf_ref, (slot, 0), tm) + w2 * _from_row_tiled(buf_ref, (slot, 1), tm)
    o_ref[...] = _rms(y, g_ref[...])


def _combine(pos, x, gates, g, ys):
    tm = TM_COMBINE
    return pl.pallas_call(
        _combine_body,
        grid=(TOKENS // tm,),
        in_specs=[pl.BlockSpec(memory_space=pltpu.SMEM),
                  pl.BlockSpec((tm, D_MODEL), lambda i: (i, 0)),
                  pl.BlockSpec((tm, 2), lambda i: (i, 0)),
                  pl.BlockSpec((1, D_MODEL), lambda i: (0, 0)),
                  pl.BlockSpec(memory_space=pl.ANY)],
        out_specs=pl.BlockSpec((tm, D_MODEL), lambda i: (i, 0)),
        out_shape=jax.ShapeDtypeStruct((TOKENS, D_MODEL), F32),
        scratch_shapes=[pltpu.VMEM((2, 2, tm * ROW_TILE, LANES), F32), pltpu.SemaphoreType.DMA((2,))],
        compiler_params=_params(("arbitrary",)),
        name="moe_combine",
    )(pos, x, gates, g, ys)


def _routing_tables(route, counts):
    cnt = counts[:, 0].astype(jnp.int32)
    tiles = (cnt + TM_MOE - 1) // TM_MOE
    tile_end = jnp.cumsum(tiles)
    tile_start = tile_end - tiles
    n_used = tile_end[-1]
    expert = route[ROUTE_I1:ROUTE_I2 + 1].astype(jnp.int32)
    rank = route[ROUTE_R1:ROUTE_R2 + 1].astype(jnp.int32)
    group_start = jnp.sum(jnp.where(expert[..., None] == jnp.arange(N_EXPERTS), tile_start * TM_MOE, 0), axis=-1)
    pos = (group_start + rank).T
    tile_id = jnp.minimum(jnp.arange(N_TILES_MOE, dtype=jnp.int32), n_used - 1)
    tile_expert = jnp.sum(tile_id[:, None] >= tile_end[None, :], axis=1).astype(jnp.int32)
    last_tile = jnp.where(tiles > 0, tile_end - 1, -1)
    spare = n_used + jnp.arange(N_EXPERTS)
    zero_tiles = jnp.concatenate([last_tile, jnp.where(spare < N_TILES_MOE, spare, -1)]).astype(jnp.int32)
    return pos.reshape(2 * TOKENS), tile_expert, n_used.reshape(1), zero_tiles


def _moe(x, h, route, counts, w1, w3, w2, g_final):
    pos, tile_expert, n_used, last_tile = _routing_tables(route, counts)
    xs = _dispatch(pos, last_tile, h)
    ys = _grouped_ffn(tile_expert, n_used, xs, w1, w3, w2)
    return _combine(pos, x, route[ROUTE_W1:ROUTE_W2 + 1].T, g_final, ys)


def _prep_w_in(w):
    aq, ak, av, gq, gk, gv, gr, glr, c_in, c_b, c_c = jnp.split(w, np.cumsum(SPLIT_SIZES)[:-1].tolist(), axis=2)
    pad = jnp.zeros(w.shape[:2] + (LANES - GLA_RANK,), w.dtype)
    return jnp.concatenate([aq, ak, av, gv, gr, gq, gk, c_in, c_b, c_c, glr, pad], axis=2).astype(BF16)


def _prep_router(w):
    wt = w.T
    hi = wt.astype(BF16)
    lo = (wt - hi.astype(F32)).astype(BF16)
    return jnp.stack([jnp.concatenate([hi, lo]), jnp.concatenate([hi, jnp.zeros_like(hi)])])


def kernel(x, w_mix_in, w_mix_out, g_mix, rel_bias, gla_w_gate, gla_b_gate, gla_g_norm, conv_w,
           g_ffn, ffn_w1, ffn_w3, ffn_w2, moe_router, moe_w1, moe_w3, moe_w2, g_final):
    assert DEPTH == 2
    x = x.reshape(TOKENS, D_MODEL)
    up_job = lambda w, tf: (w, w.shape[-1] // tf)
    down = lambda c: c.reshape(c.shape[0], c.shape[2], c.shape[3])
    gla_jobs = {0: up_job(moe_w1[0], TF_MOE), 1: up_job(moe_w3[0], TF_MOE)}
    experts = {}
    w_in = _prep_w_in(w_mix_in)
    for layer in range(DEPTH):
        u, *qkvs = _inproj(x, g_mix[layer].reshape(1, D_MODEL), w_in, layer)
        dense_jobs = [up_job(ffn_w1[:1], TF_DENSE), up_job(ffn_w3[:1], TF_DENSE), (ffn_w2[:1], 1)] if layer == 0 else []
        att, dense_bf16 = _attention(qkvs, rel_bias, dense_jobs)
        if layer == 0:
            ffn_bf16 = (dense_bf16[0][0], dense_bf16[1][0], down(dense_bf16[2])[0])
        wg = jnp.pad(gla_w_gate[layer], ((0, LANES - GLA_RANK), (0, 0)))
        go, cv, (experts[layer],) = _gla_conv(u, wg, gla_b_gate[layer].reshape(1, GLA_QK),
                                              gla_g_norm[layer].reshape(1, GLA_DV),
                                              jnp.pad(conv_w[layer], ((0, 8 - CONV_WIDTH), (0, 0))),
                                              side_casts=[gla_jobs[layer]])
        w_out = w_mix_out[layer].astype(BF16)
        g2 = g_ffn[layer].reshape(1, D_MODEL)
        i = layer // 2
        if layer % 2 == 0:
            x, h = _outproj(att, go, cv, x, w_out, g2)
            x, (cast,) = _ffn(x, h, *ffn_bf16, side_casts=[(moe_w2[i], 1)])
            experts["w2"] = down(cast)
        else:
            x, h, route, counts = _outproj(att, go, cv, x, w_out, g2, _prep_router(moe_router[i]))
            x = _moe(x, h, route, counts, experts[0], experts[1], experts["w2"], g_final.reshape(1, D_MODEL))
    return x.reshape(BATCH, SEQ, D_MODEL)
```

```python
import functools
import math

import jax
import jax.numpy as jnp
import numpy as np
from jax import lax
from jax.experimental import pallas as pl
from jax.experimental.pallas import tpu as pltpu

F32 = jnp.float32
BF16 = jnp.bfloat16

D_MODEL = 1024
BATCH = 8
SEQ = 2048
TOKENS = BATCH * SEQ
DEPTH = 2
EPS = 1e-6

HEAD_DIM = 64
ATT_HEADS = 4
ATT_WIDTH = ATT_HEADS * HEAD_DIM
DILATED_PATTERNS = ((128, 1), (512, 4), (2048, 16))
ATT_BLOCK = 128
REL_BUCKETS = 32
REL_MAX_DISTANCE = 2048

GLA_HEADS = 4
GLA_DK = 64
GLA_DV = 128
GLA_RANK = 16
GLA_CHUNK = 64
GLA_QK = GLA_HEADS * GLA_DK
GLA_V = GLA_HEADS * GLA_DV

CONV_CH = 256
CONV_WIDTH = 3
MIX_WIDTH = ATT_WIDTH + GLA_V + CONV_CH

SPLIT_SIZES = (ATT_WIDTH, ATT_WIDTH, ATT_WIDTH, GLA_QK, GLA_QK, GLA_V, GLA_V, GLA_RANK,
               CONV_CH, CONV_CH, CONV_CH)

FFN_DENSE = 2816
N_EXPERTS = 8
FFN_EXPERT = 3584

LANES = 128
MXU_WIDTH = 256
VMEM_LIMIT = 56 * 1024 * 1024

QKV_COLS = 3 * ATT_WIDTH
COL_GV, COL_GR, COL_GQ, COL_GK = 0, 512, 1024, 1280
COL_CIN, COL_CB, COL_CC, COL_GLR = 1536, 1792, 2048, 2304
U_COLS = COL_GLR + LANES

NEG_BIG = -1e30

ATT_UNROLL = 15
TM_PROJ = 512
TC_GLA = 512
TM_FFN = 512
TF_DENSE = 1408
FFN_SUB = 256
TM_MOE = 512
TF_MOE = 1792
N_TILES_MOE = 2 * TOKENS // TM_MOE + N_EXPERTS
N_SORTED = N_TILES_MOE * TM_MOE
DISPATCH_CHUNK = 512
TM_COMBINE = 512

ROUTE_I1, ROUTE_I2, ROUTE_W1, ROUTE_W2, ROUTE_R1, ROUTE_R2 = range(6)


def _params(sem):
    return pltpu.CompilerParams(dimension_semantics=sem, vmem_limit_bytes=VMEM_LIMIT)


def _split_bf16(a):
    hi = a.astype(BF16)
    lo = (a - hi.astype(F32)).astype(BF16)
    return hi, lo


def _dot(a, b):
    return jnp.dot(a, b, preferred_element_type=F32)


def _dot3(a, b):
    a_hi, a_lo = _split_bf16(a)
    b_hi, b_lo = _split_bf16(b)
    return _dot(a_hi, b_hi) + _dot(a_lo, b_hi) + _dot(a_hi, b_lo)


def _dot_nt(a, b):
    return lax.dot_general(a, b, (((1,), (1,)), ((), ())), preferred_element_type=F32)


def _dot_tn(a, b):
    return lax.dot_general(a, b, (((0,), (0,)), ((), ())), preferred_element_type=F32)


def _rms(x, g):
    ms = jnp.mean(x * x, axis=-1, keepdims=True)
    return x * lax.rsqrt(ms + EPS) * g


def _sigmoid(x):
    return 1.0 / (1.0 + jnp.exp(-x))


def _side_cast(srcs, dsts):
    for src, dst in zip(srcs, dsts):
        width = dst.shape[-1]
        for s in range(dst.shape[0]):
            dst[s] = src[:, s * width:(s + 1) * width].astype(dst.dtype)


def _side_cast_io(jobs, steps, step_of):
    in_specs, out_specs, out_shapes = [], [], []
    for w, splits in jobs:
        g, r, c = w.shape
        rb = g * r // steps
        per_group = r // rb
        in_specs.append(pl.BlockSpec(
            (None, rb, c), lambda *ids, pg=per_group: (step_of(*ids) // pg, step_of(*ids) % pg, 0)))
        out_specs.append(pl.BlockSpec(
            (None, splits, rb, c // splits),
            lambda *ids, pg=per_group: (step_of(*ids) // pg, 0, step_of(*ids) % pg, 0)))
        out_shapes.append(jax.ShapeDtypeStruct((g, splits, r, c // splits), BF16))
    return in_specs, out_specs, out_shapes


def _inproj_body(x_ref, g_ref, w_ref, o_ref, *rest):
    qkv_refs, qkv_f32 = rest[:-1], rest[-1]
    h = _rms(x_ref[...], g_ref[...]).astype(BF16)
    for c0 in range(0, QKV_COLS, MXU_WIDTH):
        res = _dot(h, w_ref[:, c0:c0 + MXU_WIDTH])
        for t in range(MXU_WIDTH // LANES):
            qkv_f32[c0 // LANES + t] = res[:, t * LANES:(t + 1) * LANES]
    for (_, dilation), ref in zip(DILATED_PATTERNS, qkv_refs):
        for r in range(dilation):
            rows = pl.ds(r, TM_PROJ // dilation, stride=dilation)
            ref[r] = jnp.concatenate([qkv_f32[t, rows, :] for t in range(QKV_COLS // LANES)],
                                     axis=-1).astype(ref.dtype)
    for c0 in range(0, U_COLS, MXU_WIDTH):
        c1 = min(c0 + MXU_WIDTH, U_COLS)
        o_ref[:, c0:c1] = _dot(h, w_ref[:, QKV_COLS + c0:QKV_COLS + c1])


def _subseq_spec(dilation, cols):
    tiles = SEQ // TM_PROJ
    return pl.BlockSpec((None, dilation, TM_PROJ // dilation, cols), lambda i: (i // tiles, 0, i % tiles, 0))


def _inproj(x, g, w_all, layer):
    qkv_shapes = [jax.ShapeDtypeStruct((BATCH, d, SEQ // d, QKV_COLS), BF16) for _, d in DILATED_PATTERNS]
    return pl.pallas_call(
        _inproj_body,
        grid=(TOKENS // TM_PROJ,),
        in_specs=[
            pl.BlockSpec((TM_PROJ, D_MODEL), lambda i: (i, 0)),
            pl.BlockSpec((1, D_MODEL), lambda i: (0, 0)),
            pl.BlockSpec((None, D_MODEL, QKV_COLS + U_COLS), lambda i: (layer, 0, 0)),
        ],
        out_specs=[pl.BlockSpec((TM_PROJ, U_COLS), lambda i: (i, 0))]
                  + [_subseq_spec(d, QKV_COLS) for _, d in DILATED_PATTERNS],
        out_shape=[jax.ShapeDtypeStruct((TOKENS, U_COLS), F32)] + qkv_shapes,
        scratch_shapes=[pltpu.VMEM((QKV_COLS // LANES, TM_PROJ, LANES), F32)],
        compiler_params=_params(("parallel",)),
        name="inproj",
    )(x, g, w_all)


def _rel_bucket(dist):
    max_exact = REL_BUCKETS // 2
    d = jnp.maximum(dist, 0)
    log_ratio = jnp.log(jnp.maximum(d, 1).astype(F32) / max_exact) / math.log(REL_MAX_DISTANCE / max_exact)
    large = jnp.minimum(max_exact + (log_ratio * (REL_BUCKETS - max_exact)).astype(jnp.int32), REL_BUCKETS - 1)
    return jnp.where(d < max_exact, d, large)


def _bucket_table(window, dilation):
    span = window // dilation
    qi = jnp.arange(ATT_BLOCK)[:, None]
    kj = jnp.arange(2 * ATT_BLOCK)[None, :]
    sub_dist = qi - kj + ATT_BLOCK
    band = (sub_dist >= 0) & (sub_dist <= span)
    return jnp.where(band, _rel_bucket(sub_dist * dilation), -1).astype(jnp.int32)


def _attn_body(rb_ref, bidx_ref, qkv_ref, *rest, sub_blocks, unroll):
    n_cast = (len(rest) - 3) // 2
    o_ref, lse_ref = rest[n_cast:n_cast + 2]
    bias_ref = rest[-1]
    _side_cast(rest[:n_cast], rest[n_cast + 2:-1])
    nblk = SEQ // ATT_BLOCK

    @pl.when(pl.program_id(0) == 0)
    def _():
        bidx = bidx_ref[...]
        in_prev = lax.broadcasted_iota(jnp.int32, bidx.shape, 1) < ATT_BLOCK
        for h in range(ATT_HEADS):
            acc = jnp.full(bidx.shape, NEG_BIG, F32)
            for b in range(REL_BUCKETS):
                acc = jnp.where(bidx == b, rb_ref[b, h], acc)
            bias_ref[0, h] = acc
            bias_ref[1, h] = jnp.where(in_prev, NEG_BIG, acc)
            bias_ref[2, h] = jnp.concatenate([acc[:, ATT_BLOCK:], jnp.full_like(acc[:, ATT_BLOCK:], NEG_BIG)], axis=1)

    def block(n, first):
        if first:
            rows, krows, variant = slice(0, ATT_BLOCK), slice(0, 2 * ATT_BLOCK), 2
        else:
            r0 = pl.multiple_of(n * ATT_BLOCK, ATT_BLOCK)
            rows = pl.ds(r0, ATT_BLOCK)
            krows = pl.ds(r0 - ATT_BLOCK, 2 * ATT_BLOCK)
            if sub_blocks == nblk:
                variant = 0
            elif sub_blocks == 1:
                variant = 1
            else:
                variant = jnp.where(n % sub_blocks == 0, 1, 0)
        q = qkv_ref[rows, 0:ATT_WIDTH]
        kk = qkv_ref[krows, ATT_WIDTH:2 * ATT_WIDTH]
        vv = qkv_ref[krows, 2 * ATT_WIDTH:3 * ATT_WIDTH]
        q = q * jnp.asarray(HEAD_DIM ** -0.5, BF16)
        head_of_lane = lax.broadcasted_iota(jnp.int32, (ATT_BLOCK, ATT_WIDTH), 1) // HEAD_DIM
        ones = jnp.ones((kk.shape[0], LANES), BF16)
        num = den = mx = None
        for h in range(ATT_HEADS):
            mine = head_of_lane == h
            bias = bias_ref[variant, h]
            s = _dot_nt(jnp.where(mine, q, jnp.zeros_like(q)), kk) + bias
            m = jnp.max(s, axis=-1, keepdims=True)
            p = jnp.exp(s - m).astype(BF16)
            num_h = _dot(p, vv)
            den_h = jnp.tile(_dot(p, ones), (1, ATT_WIDTH // LANES))
            m_h = jnp.broadcast_to(m, (ATT_BLOCK, ATT_WIDTH))
            num = num_h if h == 0 else jnp.where(mine, num_h, num)
            den = den_h if h == 0 else jnp.where(mine, den_h, den)
            mx = m_h if h == 0 else jnp.where(mine, m_h, mx)
        o_ref[rows, :] = (num / den).astype(o_ref.dtype)
        lse_ref[rows, :] = mx + jnp.log(den)

    block(0, True)

    def loop_body(n, carry):
        block(n, False)
        return carry
    lax.fori_loop(1, nblk, loop_body, 0, unroll=unroll)


def _attention_pattern(ua, rel_bias, window, dilation, side_casts=()):
    L = SEQ // dilation
    shape = (BATCH, dilation, L, ATT_WIDTH)
    qkv_spec = pl.BlockSpec((None, SEQ, QKV_COLS), lambda b: (b, 0, 0))
    out_spec = pl.BlockSpec((None, SEQ, ATT_WIDTH), lambda b: (b, 0, 0))
    cast_in, cast_out, cast_shapes = _side_cast_io(side_casts, BATCH, lambda b: b)
    o, lse, *casts = pl.pallas_call(
        functools.partial(_attn_body, sub_blocks=L // ATT_BLOCK, unroll=ATT_UNROLL),
        grid=(BATCH,),
        in_specs=[
            pl.BlockSpec(memory_space=pltpu.SMEM),
            pl.BlockSpec((ATT_BLOCK, 2 * ATT_BLOCK), lambda b: (0, 0)),
            qkv_spec,
        ] + cast_in,
        out_specs=[out_spec, out_spec] + cast_out,
        out_shape=[jax.ShapeDtypeStruct((BATCH, SEQ, ATT_WIDTH), BF16),
                   jax.ShapeDtypeStruct((BATCH, SEQ, ATT_WIDTH), F32)] + cast_shapes,
        scratch_shapes=[pltpu.VMEM((3, ATT_HEADS, ATT_BLOCK, 2 * ATT_BLOCK), F32)],
        compiler_params=_params(("arbitrary",)),
        name=f"attn_d{dilation}",
    )(rel_bias, _bucket_table(window, dilation), ua.reshape(BATCH, SEQ, QKV_COLS), *[w for w, _ in side_casts])
    return (o.reshape(shape), lse.reshape(shape)), casts


def _attention(qkvs, rel_bias, side_casts=()):
    side_casts = list(side_casts) + [None] * (len(DILATED_PATTERNS) - len(side_casts))
    res = [_attention_pattern(ua, rel_bias, window, dilation, [] if w is None else [w])
           for ua, (window, dilation), w in zip(qkvs, DILATED_PATTERNS, side_casts)]
    return [r[0] for r in res], [c for r in res for c in r[1]]


def _short_conv(cin_ref, cb_ref, cc_ref, w_ref, o_ref, tail_ref):
    uu = cc_ref[...] * cin_ref[...]
    t = lax.broadcasted_iota(jnp.int32, uu.shape, 0)
    y = uu * w_ref[CONV_WIDTH - 1:CONV_WIDTH, :]
    for shift in range(1, CONV_WIDTH):
        prev = pltpu.roll(uu, shift, axis=0)
        for r in range(shift):
            prev = jnp.where(t == r, tail_ref[8 - shift + r:8 - shift + r + 1, :], prev)
        y = y + prev * w_ref[CONV_WIDTH - 1 - shift:CONV_WIDTH - shift, :]
    o_ref[...] = (cb_ref[...] * y).astype(o_ref.dtype)
    tail_ref[...] = uu[uu.shape[0] - 8:, :]


def _gla_body(u_ref, wg_ref, bg_ref, gn_ref, ctril_ref, cw_ref, *rest):
    cols = lambda col, width: u_ref.at[:, col:col + width]
    q_ref, k_ref = cols(COL_GQ, GLA_QK), cols(COL_GK, GLA_QK)
    v_ref, gr_ref, glr_ref = cols(COL_GV, GLA_V), cols(COL_GR, GLA_V), cols(COL_GLR, LANES)
    cin_ref, cb_ref, cc_ref = cols(COL_CIN, CONV_CH), cols(COL_CB, CONV_CH), cols(COL_CC, CONV_CH)
    _gla_tile(q_ref, k_ref, v_ref, gr_ref, glr_ref, wg_ref, bg_ref, gn_ref, ctril_ref,
              cin_ref, cb_ref, cc_ref, cw_ref, *rest)


def _gla_tile(q_ref, k_ref, v_ref, gr_ref, glr_ref, wg_ref, bg_ref, gn_ref, ctril_ref,
              cin_ref, cb_ref, cc_ref, cw_ref, *rest):
    n_cast = (len(rest) - 4) // 2
    o_ref, cv_ref = rest[n_cast:n_cast + 2]
    s_ref, tail_ref = rest[-2:]
    _side_cast(rest[:n_cast], rest[n_cast + 2:-2])

    @pl.when(pl.program_id(1) == 0)
    def _():
        s_ref[...] = jnp.zeros_like(s_ref)
        tail_ref[...] = jnp.zeros_like(tail_ref)

    _short_conv(cin_ref, cb_ref, cc_ref, cw_ref, cv_ref, tail_ref)

    C = GLA_CHUNK
    row = lax.broadcasted_iota(jnp.int32, (C, C), 0)
    col = lax.broadcasted_iota(jnp.int32, (C, C), 1)
    tril = row >= col
    n_chunks = TC_GLA // C

    xg = _dot3(glr_ref[...], wg_ref[...]) + bg_ref[...]
    la_all = (jnp.minimum(xg, 0.0) - jnp.log(1.0 + jnp.exp(-jnp.abs(xg)))) * (1.0 / 16.0)

    la_hi, la_lo = _split_bf16(la_all)
    cum_all = _dot(ctril_ref[...], la_hi) + _dot(ctril_ref[...], la_lo)
    totals = jnp.concatenate([cum_all[(c + 1) * C - 1:(c + 1) * C, :] for c in range(n_chunks)]
                             + [jnp.zeros((LANES - n_chunks, GLA_QK), F32)], axis=0)
    decay_cols = jnp.exp(totals.T)

    for c in range(n_chunks):
        rows = slice(c * C, (c + 1) * C)
        cum = cum_all[rows]
        last = cum[C - 1:C, :]
        q = q_ref[rows, :]
        k = k_ref[rows, :]
        qt = (q * jnp.exp(cum) * (GLA_DK ** -0.5)).astype(BF16)
        kt = (k * jnp.exp(-cum)).astype(BF16)
        kl_t = (k * jnp.exp(last - cum)).T.astype(BF16)
        for h in range(GLA_HEADS):
            sl = slice(h * GLA_DK, (h + 1) * GLA_DK)
            vs = slice(h * GLA_DV, (h + 1) * GLA_DV)
            vh = v_ref[rows, vs].astype(BF16)
            state = s_ref[h]
            st_hi, st_lo = _split_bf16(state)
            sc = jnp.where(tril, _dot_nt(qt[:, sl], kt[:, sl]), 0.0).astype(BF16)
            o = _dot(qt[:, sl], st_hi) + _dot(qt[:, sl], st_lo) + _dot(sc, vh)
            decay = jnp.broadcast_to(decay_cols[sl, c:c + 1], state.shape)
            s_ref[h] = decay * state + _dot(kl_t[sl, :], vh)
            g = gr_ref[rows, vs]
            o_ref[rows, vs] = (_rms(o, gn_ref[...]) * (g * _sigmoid(g))).astype(o_ref.dtype)


def _gla_conv(u, wg, bg, gn, conv_w, side_casts=()):
    nj = SEQ // TC_GLA
    row = lambda b, j: b * nj + j
    full = lambda a: pl.BlockSpec(a.shape, lambda b, j: (0, 0))
    cast_in, cast_out, cast_shapes = _side_cast_io(side_casts, BATCH * nj, row)
    t = np.arange(TC_GLA)
    same_chunk = (t[:, None] // GLA_CHUNK) == (t[None, :] // GLA_CHUNK)
    chunk_tril = jnp.asarray(same_chunk & (t[:, None] >= t[None, :]), BF16)
    go, cv, *casts = pl.pallas_call(
        _gla_body,
        grid=(BATCH, nj),
        in_specs=[pl.BlockSpec((TC_GLA, U_COLS), lambda b, j: (row(b, j), 0)),
                  full(wg), full(bg), full(gn), full(chunk_tril), full(conv_w)] + cast_in,
        out_specs=[pl.BlockSpec((TC_GLA, GLA_V), lambda b, j: (row(b, j), 0)),
                   pl.BlockSpec((TC_GLA, CONV_CH), lambda b, j: (row(b, j), 0))] + cast_out,
        out_shape=[jax.ShapeDtypeStruct((TOKENS, GLA_V), BF16), jax.ShapeDtypeStruct((TOKENS, CONV_CH), BF16)]
                  + cast_shapes,
        scratch_shapes=[pltpu.VMEM((GLA_HEADS, GLA_DK, GLA_DV), F32), pltpu.VMEM((8, CONV_CH), F32)],
        compiler_params=_params(("arbitrary", "arbitrary")),
        name="gla_conv",
    )(u, wg, bg, gn, chunk_tril, conv_w, *[w for w, _ in side_casts])
    return go, cv, casts


def _outproj_body(*refs, route):
    (o1, o4, o16, l1, l4, l16, go_ref, cv_ref, x_ref, w_ref, g_ref) = refs[:11]
    perm_ref = refs[-1]
    if route:
        wr_ref, xo_ref, ho_ref, route_ref, counts_ref, carry_ref = refs[11:-1]
    else:
        xo_ref, ho_ref = refs[11:-1]

    def token_order(ref, slot):
        dilation, rows, _ = ref.shape
        if dilation == 1:
            return ref[0].astype(F32)
        tiles = range(ATT_WIDTH // LANES)
        for r in range(dilation):
            val = ref[r].astype(F32)
            for t in tiles:
                perm_ref[slot, t, pl.ds(r, rows, stride=dilation), :] = val[:, t * LANES:(t + 1) * LANES]
        return jnp.concatenate([perm_ref[slot, t] for t in tiles], axis=-1)

    la, lb, lc = token_order(l1, 0), token_order(l4, 0), token_order(l16, 1)
    oa, ob, oc = token_order(o1, 0), token_order(o4, 2), token_order(o16, 3)
    m = jnp.maximum(jnp.maximum(la, lb), lc)
    ea, eb, ec = jnp.exp(la - m), jnp.exp(lb - m), jnp.exp(lc - m)
    att = (ea * oa + eb * ob + ec * oc) / (ea + eb + ec)
    y = (x_ref[...]
         + _dot(att.astype(BF16), w_ref[0:ATT_WIDTH, :])
         + _dot(go_ref[...], w_ref[ATT_WIDTH:ATT_WIDTH + GLA_V, :])
         + _dot(cv_ref[...], w_ref[ATT_WIDTH + GLA_V:MIX_WIDTH, :]))
    xo_ref[...] = y
    hf = _rms(y, g_ref[...])
    ho_ref[...] = hf.astype(ho_ref.dtype)
    if route:
        @pl.when(pl.program_id(0) == 0)
        def _():
            carry_ref[...] = jnp.zeros_like(carry_ref)

        tm = hf.shape[0]
        ne = N_EXPERTS
        hf_hi, hf_lo = _split_bf16(hf)
        part = _dot_nt(wr_ref[0], hf_hi) + _dot_nt(wr_ref[1], hf_lo)
        logits = part[0:ne] + part[ne:2 * ne]
        eidx = lax.broadcasted_iota(jnp.int32, logits.shape, 0).astype(F32)
        v1 = jnp.max(logits, axis=0, keepdims=True)
        i1 = jnp.min(jnp.where(logits == v1, eidx, float(ne)), axis=0, keepdims=True)
        lg2 = jnp.where(eidx == i1, -jnp.inf, logits)
        v2 = jnp.max(lg2, axis=0, keepdims=True)
        i2 = jnp.min(jnp.where(lg2 == v2, eidx, float(ne)), axis=0, keepdims=True)
        e2 = jnp.exp(v2 - v1)
        w1 = 1.0 / (1.0 + e2)
        w2 = e2 * w1
        sel1 = eidx == i1
        sel2 = eidx == i2
        onehot = jnp.where(sel1, 1.0, jnp.where(sel2, 1.0, 0.0))
        tri = (lax.broadcasted_iota(jnp.int32, (tm, tm), 0) <= lax.broadcasted_iota(jnp.int32, (tm, tm), 1))
        onehot16 = jnp.concatenate([onehot, jnp.zeros_like(onehot)], axis=0).astype(BF16)
        csum = _dot(onehot16, jnp.where(tri, 1.0, 0.0).astype(BF16))[0:ne]
        carry = carry_ref[:, 0:1]
        rank = csum - onehot + carry
        r1 = jnp.sum(jnp.where(sel1, rank, 0.0), axis=0, keepdims=True)
        r2 = jnp.sum(jnp.where(sel2, rank, 0.0), axis=0, keepdims=True)
        total = jnp.broadcast_to(carry + csum[:, tm - 1:tm], carry_ref.shape)
        carry_ref[...] = total
        counts_ref[...] = total
        rows = {ROUTE_I1: i1, ROUTE_I2: i2, ROUTE_W1: w1, ROUTE_W2: w2, ROUTE_R1: r1, ROUTE_R2: r2}
        zero = jnp.zeros_like(i1)
        route_ref[...] = jnp.concatenate([rows.get(r, zero) for r in range(8)], axis=0)


def _outproj(att, go, cv, x, w, g, w_router=None):
    route = w_router is not None
    tm = TM_PROJ
    tile = lambda cols: pl.BlockSpec((tm, cols), lambda i: (i, 0))
    full = lambda a: pl.BlockSpec(a.shape, lambda i: (0, 0))
    (o1, l1), (o4, l4), (o16, l16) = att
    args = [o1, o4, o16, l1, l4, l16, go, cv, x, w, g]
    att_specs = [_subseq_spec(d, ATT_WIDTH) for _, d in DILATED_PATTERNS]
    in_specs = att_specs * 2 + [tile(GLA_V), tile(CONV_CH), tile(D_MODEL), full(w), full(g)]
    out_specs = [tile(D_MODEL), tile(D_MODEL)]
    out_shape = [jax.ShapeDtypeStruct((TOKENS, D_MODEL), F32),
                 jax.ShapeDtypeStruct((TOKENS, D_MODEL), F32 if route else BF16)]
    scratch = []
    if route:
        args.append(w_router)
        in_specs.append(pl.BlockSpec(w_router.shape, lambda i: (0, 0, 0)))
        out_specs += [pl.BlockSpec((8, tm), lambda i: (0, i)), pl.BlockSpec((N_EXPERTS, LANES), lambda i: (0, 0))]
        out_shape += [jax.ShapeDtypeStruct((8, TOKENS), F32), jax.ShapeDtypeStruct((N_EXPERTS, LANES), F32)]
        scratch = [pltpu.VMEM((N_EXPERTS, LANES), F32)]
    scratch.append(pltpu.VMEM((4, ATT_WIDTH // LANES, tm, LANES), F32))
    return pl.pallas_call(
        functools.partial(_outproj_body, route=route),
        grid=(TOKENS // tm,),
        in_specs=in_specs,
        out_specs=out_specs,
        out_shape=out_shape,
        scratch_shapes=scratch,
        compiler_params=_params(("arbitrary",) if route else ("parallel",)),
        name="outproj_route" if route else "outproj",
    )(*args)


def _swiglu_accumulate(h, w1_ref, w3_ref, w2_ref, acc_ref, tf):
    for c0 in range(0, tf, FFN_SUB):
        c1 = min(c0 + FFN_SUB, tf)
        a = _dot(h, w1_ref[:, c0:c1])
        b = _dot(h, w3_ref[:, c0:c1])
        act = a * _sigmoid(a) * b
        acc_ref[...] += _dot(act.astype(BF16), w2_ref[c0:c1, :])


def _ffn_body(x_ref, h_ref, w1_ref, w3_ref, w2_ref, *rest, tf):
    n_cast = (len(rest) - 1) // 2
    o_ref = rest[n_cast]

    @pl.when(pl.program_id(1) == 0)
    def _():
        o_ref[...] = x_ref[...]

    _swiglu_accumulate(h_ref[...], w1_ref, w3_ref, w2_ref, o_ref, tf)
    _side_cast(rest[:n_cast], rest[n_cast + 1:])


def _ffn(x, h, w1, w3, w2, *, side_casts=()):
    nj, _, tf = w1.shape
    tm = TM_FFN
    tile = lambda cols: pl.BlockSpec((tm, cols), lambda i, j: (i, 0))
    cast_in, cast_out, cast_shapes = _side_cast_io(side_casts, (TOKENS // tm) * nj, lambda i, j: i * nj + j)
    res = pl.pallas_call(
        functools.partial(_ffn_body, tf=tf),
        grid=(TOKENS // tm, nj),
        in_specs=[tile(D_MODEL), tile(D_MODEL),
                  pl.BlockSpec((None, D_MODEL, tf), lambda i, j: (j, 0, 0)),
                  pl.BlockSpec((None, D_MODEL, tf), lambda i, j: (j, 0, 0)),
                  pl.BlockSpec((tf, D_MODEL), lambda i, j: (j, 0))] + cast_in,
        out_specs=[tile(D_MODEL)] + cast_out,
        out_shape=[jax.ShapeDtypeStruct((TOKENS, D_MODEL), F32)] + cast_shapes,
        compiler_params=_params(("parallel", "arbitrary")),
        name="dense_ffn",
    )(x, h, w1, w3, w2, *[w for w, _ in side_casts])
    return res[0], res[1:]


ROW_TILE = D_MODEL // LANES


def _to_row_tiled(dst_ref, lead, val):
    rows = val.shape[0]
    for s in range(ROW_TILE):
        dst_ref[(*lead, pl.ds(s, rows, stride=ROW_TILE), slice(None))] = val[:, s * LANES:(s + 1) * LANES]


def _from_row_tiled(src_ref, lead, rows):
    return jnp.concatenate([src_ref[(*lead, pl.ds(s, rows, stride=ROW_TILE), slice(None))]
                            for s in range(ROW_TILE)], axis=-1)


def _row_tile(idx):
    return pl.ds(pl.multiple_of(idx * ROW_TILE, ROW_TILE), ROW_TILE)


def _dispatch_body(pos_ref, last_tile_ref, h_ref, xs_hbm, stage_ref, zero_ref, sem, zero_sem):
    i = pl.program_id(0)
    n = pl.num_programs(0)
    tm = DISPATCH_CHUNK
    slot = i % 2

    def drain(s):
        for _ in range(2):
            pltpu.make_async_copy(stage_ref.at[s], xs_hbm.at[pl.ds(0, tm * ROW_TILE)], sem.at[s]).wait()

    @pl.when(i == 0)
    def _():
        zero_ref[...] = jnp.zeros_like(zero_ref)

        def zero_copy(e):
            start = pl.multiple_of(last_tile_ref[e] * (TM_MOE * ROW_TILE), TM_MOE * ROW_TILE)
            return pltpu.make_async_copy(zero_ref, xs_hbm.at[pl.ds(start, TM_MOE * ROW_TILE)], zero_sem)

        for e in range(2 * N_EXPERTS):
            @pl.when(last_tile_ref[e] >= 0)
            def _():
                zero_copy(e).start()
        for e in range(2 * N_EXPERTS):
            @pl.when(last_tile_ref[e] >= 0)
            def _():
                zero_copy(e).wait()

    @pl.when(i >= 2)
    def _():
        drain(slot)

    _to_row_tiled(stage_ref, (slot,), h_ref[...])

    def body(t, carry):
        for k in range(2):
            dst = pos_ref[2 * (i * tm + t) + k]
            pltpu.make_async_copy(stage_ref.at[slot, _row_tile(t)], xs_hbm.at[_row_tile(dst)],
                                  sem.at[slot]).start(priority=k)
        return carry
    lax.fori_loop(0, tm, body, 0, unroll=8)

    @pl.when(i == n - 1)
    def _():
        drain(1 - slot)
        drain(slot)


def _dispatch(pos, last_tile, h):
    tm = DISPATCH_CHUNK
    return pl.pallas_call(
        _dispatch_body,
        grid=(TOKENS // tm,),
        in_specs=[pl.BlockSpec(memory_space=pltpu.SMEM),
                  pl.BlockSpec(memory_space=pltpu.SMEM),
                  pl.BlockSpec((tm, D_MODEL), lambda i: (i, 0))],
        out_specs=pl.BlockSpec(memory_space=pl.ANY),
        out_shape=jax.ShapeDtypeStruct((N_SORTED * ROW_TILE, LANES), F32),
        scratch_shapes=[pltpu.VMEM((2, tm * ROW_TILE, LANES), F32),
                        pltpu.VMEM((TM_MOE * ROW_TILE, LANES), F32),
                        pltpu.SemaphoreType.DMA((2,)), pltpu.SemaphoreType.DMA(())],
        compiler_params=_params(("arbitrary",)),
        name="moe_dispatch",
    )(pos, last_tile, h)


def _gffn_body(te_ref, nu_ref, xs_ref, w1_ref, w3_ref, w2_ref, o_ref, hb_ref, acc_ref):
    del te_ref
    i = pl.program_id(0)
    j = pl.program_id(1)

    used = i < nu_ref[0]

    @pl.when(j == 0)
    def _():
        acc_ref[...] = jnp.zeros_like(acc_ref)

    @pl.when(used & (j == 0))
    def _():
        hb_ref[...] = _from_row_tiled(xs_ref, (), TM_MOE).astype(BF16)

    @pl.when(used)
    def _():
        _swiglu_accumulate(hb_ref[...], w1_ref, w3_ref, w2_ref, acc_ref, TF_MOE)

    @pl.when(j == pl.num_programs(1) - 1)
    def _():
        _to_row_tiled(o_ref, (), acc_ref[...])


def _grouped_ffn(tile_expert, n_used, xs, w1, w3, w2):
    nj = FFN_EXPERT // TF_MOE
    col = lambda i, j, nu: jnp.where(i < nu[0], j, nj - 1)
    grid_spec = pltpu.PrefetchScalarGridSpec(
        num_scalar_prefetch=2,
        grid=(N_TILES_MOE, nj),
        in_specs=[
            pl.BlockSpec((TM_MOE * ROW_TILE, LANES), lambda i, j, te, nu: (jnp.minimum(i, nu[0] - 1), 0)),
            pl.BlockSpec((None, None, D_MODEL, TF_MOE), lambda i, j, te, nu: (te[i], col(i, j, nu), 0, 0)),
            pl.BlockSpec((None, None, D_MODEL, TF_MOE), lambda i, j, te, nu: (te[i], col(i, j, nu), 0, 0)),
            pl.BlockSpec((None, TF_MOE, D_MODEL), lambda i, j, te, nu: (te[i], col(i, j, nu), 0)),
        ],
        out_specs=pl.BlockSpec((TM_MOE * ROW_TILE, LANES), lambda i, j, te, nu: (i, 0)),
        scratch_shapes=[pltpu.VMEM((TM_MOE, D_MODEL), BF16), pltpu.VMEM((TM_MOE, D_MODEL), F32)],
    )
    return pl.pallas_call(
        _gffn_body,
        grid_spec=grid_spec,
        out_shape=jax.ShapeDtypeStruct((N_SORTED * ROW_TILE, LANES), F32),
        compiler_params=_params(("arbitrary", "arbitrary")),
        name="moe_ffn",
    )(tile_expert, n_used, xs, w1, w3, w2)


def _combine_body(pos_ref, x_ref, gate_ref, g_ref, ys_hbm, o_ref, buf_ref, sem):
    i = pl.program_id(0)
    n = pl.num_programs(0)
    tm = TM_COMBINE

    def issue(tile, slot):
        def body(t, carry):
            for k in range(2):
                src = pos_ref[2 * (tile * tm + t) + k]
                pltpu.make_async_copy(ys_hbm.at[_row_tile(src)], buf_ref.at[slot, k, _row_tile(t)],
                                      sem.at[slot]).start(priority=k)
            return carry
        lax.fori_loop(0, tm, body, 0, unroll=8)

    @pl.when(i == 0)
    def _():
        issue(0, 0)

    @pl.when(i + 1 < n)
    def _():
        issue(i + 1, (i + 1) % 2)

    slot = i % 2
    for k in range(2):
        pltpu.make_async_copy(ys_hbm.at[pl.ds(0, tm * ROW_TILE)], buf_ref.at[slot, k], sem.at[slot]).wait()
    w1 = gate_ref[:, 0:1]
    w2 = gate_ref[:, 1:2]
    y = x_ref[...] + w1 * _from_row_tiled(buf_ref, (slot, 0), tm) + w2 * _from_row_tiled(buf_ref, (slot, 1), tm)
    o_ref[...] = _rms(y, g_ref[...])


def _combine(pos, x, gates, g, ys):
    tm = TM_COMBINE
    return pl.pallas_call(
        _combine_body,
        grid=(TOKENS // tm,),
        in_specs=[pl.BlockSpec(memory_space=pltpu.SMEM),
                  pl.BlockSpec((tm, D_MODEL), lambda i: (i, 0)),
                  pl.BlockSpec((tm, 2), lambda i: (i, 0)),
                  pl.BlockSpec((1, D_MODEL), lambda i: (0, 0)),
                  pl.BlockSpec(memory_space=pl.ANY)],
        out_specs=pl.BlockSpec((tm, D_MODEL), lambda i: (i, 0)),
        out_shape=jax.ShapeDtypeStruct((TOKENS, D_MODEL), F32),
        scratch_shapes=[pltpu.VMEM((2, 2, tm * ROW_TILE, LANES), F32), pltpu.SemaphoreType.DMA((2,))],
        compiler_params=_params(("arbitrary",)),
        name="moe_combine",
    )(pos, x, gates, g, ys)


def _routing_tables(route, counts):
    cnt = counts[:, 0].astype(jnp.int32)
    tiles = (cnt + TM_MOE - 1) // TM_MOE
    tile_end = jnp.cumsum(tiles)
    tile_start = tile_end - tiles
    n_used = tile_end[-1]
    expert = route[ROUTE_I1:ROUTE_I2 + 1].astype(jnp.int32)
    rank = route[ROUTE_R1:ROUTE_R2 + 1].astype(jnp.int32)
    group_start = jnp.sum(jnp.where(expert[..., None] == jnp.arange(N_EXPERTS), tile_start * TM_MOE, 0), axis=-1)
    pos = (group_start + rank).T
    tile_id = jnp.minimum(jnp.arange(N_TILES_MOE, dtype=jnp.int32), n_used - 1)
    tile_expert = jnp.sum(tile_id[:, None] >= tile_end[None, :], axis=1).astype(jnp.int32)
    last_tile = jnp.where(tiles > 0, tile_end - 1, -1)
    spare = n_used + jnp.arange(N_EXPERTS)
    zero_tiles = jnp.concatenate([last_tile, jnp.where(spare < N_TILES_MOE, spare, -1)]).astype(jnp.int32)
    return pos.reshape(2 * TOKENS), tile_expert, n_used.reshape(1), zero_tiles


def _moe(x, h, route, counts, w1, w3, w2, g_final):
    pos, tile_expert, n_used, last_tile = _routing_tables(route, counts)
    xs = _dispatch(pos, last_tile, h)
    ys = _grouped_ffn(tile_expert, n_used, xs, w1, w3, w2)
    return _combine(pos, x, route[ROUTE_W1:ROUTE_W2 + 1].T, g_final, ys)


def _prep_w_in(w):
    aq, ak, av, gq, gk, gv, gr, glr, c_in, c_b, c_c = jnp.split(w, np.cumsum(SPLIT_SIZES)[:-1].tolist(), axis=2)
    pad = jnp.zeros(w.shape[:2] + (LANES - GLA_RANK,), w.dtype)
    return jnp.concatenate([p.astype(BF16) for p in (aq, ak, av, gv, gr, gq, gk, c_in, c_b, c_c, glr, pad)], axis=2)


def _prep_router(w):
    wt = w.T
    hi = wt.astype(BF16)
    lo = (wt - hi.astype(F32)).astype(BF16)
    return jnp.stack([jnp.concatenate([hi, lo]), jnp.concatenate([hi, jnp.zeros_like(hi)])])


def kernel(x, w_mix_in, w_mix_out, g_mix, rel_bias, gla_w_gate, gla_b_gate, gla_g_norm, conv_w,
           g_ffn, ffn_w1, ffn_w3, ffn_w2, moe_router, moe_w1, moe_w3, moe_w2, g_final):
    assert DEPTH == 2
    x = x.reshape(TOKENS, D_MODEL)
    up_job = lambda w, tf: (w, w.shape[-1] // tf)
    down = lambda c: c.reshape(c.shape[0], c.shape[2], c.shape[3])
    gla_jobs = {0: up_job(moe_w1[0], TF_MOE), 1: up_job(moe_w3[0], TF_MOE)}
    experts = {}
    w_in = _prep_w_in(w_mix_in)
    for layer in range(DEPTH):
        u, *qkvs = _inproj(x, g_mix[layer].reshape(1, D_MODEL), w_in, layer)
        dense_jobs = [up_job(ffn_w1[:1], TF_DENSE), up_job(ffn_w3[:1], TF_DENSE), (ffn_w2[:1], 1)] if layer == 0 else []
        att, dense_bf16 = _attention(qkvs, rel_bias, dense_jobs)
        if layer == 0:
            ffn_bf16 = (dense_bf16[0][0], dense_bf16[1][0], down(dense_bf16[2])[0])
        wg = jnp.pad(gla_w_gate[layer], ((0, LANES - GLA_RANK), (0, 0)))
        go, cv, (experts[layer],) = _gla_conv(u, wg, gla_b_gate[layer].reshape(1, GLA_QK),
                                              gla_g_norm[layer].reshape(1, GLA_DV),
                                              jnp.pad(conv_w[layer], ((0, 8 - CONV_WIDTH), (0, 0))),
                                              side_casts=[gla_jobs[layer]])
        w_out = w_mix_out[layer].astype(BF16)
        g2 = g_ffn[layer].reshape(1, D_MODEL)
        i = layer // 2
        if layer % 2 == 0:
            x, h = _outproj(att, go, cv, x, w_out, g2)
            x, (cast,) = _ffn(x, h, *ffn_bf16, side_casts=[(moe_w2[i], 1)])
            experts["w2"] = down(cast)
        else:
            x, h, route, counts = _outproj(att, go, cv, x, w_out, g2, _prep_router(moe_router[i]))
            x = _moe(x, h, route, counts, experts[0], experts[1], experts["w2"], g_final.reshape(1, D_MODEL))
    return x.reshape(BATCH, SEQ, D_MODEL)
```

```python
import functools
import math

import jax
import jax.numpy as jnp
import numpy as np
from jax import lax
from jax.experimental import pallas as pl
from jax.experimental.pallas import tpu as pltpu

F32 = jnp.float32
BF16 = jnp.bfloat16

D_MODEL = 1024
BATCH = 8
SEQ = 2048
TOKENS = BATCH * SEQ
DEPTH = 2
EPS = 1e-6

HEAD_DIM = 64
ATT_HEADS = 4
ATT_WIDTH = ATT_HEADS * HEAD_DIM
DILATED_PATTERNS = ((128, 1), (512, 4), (2048, 16))
ATT_BLOCK = 128
REL_BUCKETS = 32
REL_MAX_DISTANCE = 2048

GLA_HEADS = 4
GLA_DK = 64
GLA_DV = 128
GLA_RANK = 16
GLA_CHUNK = 64
GLA_QK = GLA_HEADS * GLA_DK
GLA_V = GLA_HEADS * GLA_DV

CONV_CH = 256
CONV_WIDTH = 3
MIX_WIDTH = ATT_WIDTH + GLA_V + CONV_CH

SPLIT_SIZES = (ATT_WIDTH, ATT_WIDTH, ATT_WIDTH, GLA_QK, GLA_QK, GLA_V, GLA_V, GLA_RANK,
               CONV_CH, CONV_CH, CONV_CH)

FFN_DENSE = 2816
N_EXPERTS = 8
FFN_EXPERT = 3584

LANES = 128
MXU_WIDTH = 256
VMEM_LIMIT = 56 * 1024 * 1024

QKV_COLS = 3 * ATT_WIDTH
COL_GV, COL_GR, COL_GQ, COL_GK = 0, 512, 1024, 1280
COL_CIN, COL_CB, COL_CC, COL_GLR = 1536, 1792, 2048, 2304
U_COLS = COL_GLR + LANES

NEG_BIG = -1e30

ATT_UNROLL = 15
TM_PROJ = 512
TC_GLA = 512
TM_FFN = 512
TF_DENSE = 2816
FFN_SUB = 256
TM_MOE = 512
TF_MOE = 1792
N_TILES_MOE = 2 * TOKENS // TM_MOE + N_EXPERTS
N_SORTED = N_TILES_MOE * TM_MOE
DISPATCH_CHUNK = 512
TM_COMBINE = 512

ROUTE_I1, ROUTE_I2, ROUTE_W1, ROUTE_W2, ROUTE_R1, ROUTE_R2 = range(6)


def _params(sem):
    return pltpu.CompilerParams(dimension_semantics=sem, vmem_limit_bytes=VMEM_LIMIT)


def _split_bf16(a):
    hi = a.astype(BF16)
    lo = (a - hi.astype(F32)).astype(BF16)
    return hi, lo


def _dot(a, b):
    return jnp.dot(a, b, preferred_element_type=F32)


def _dot3(a, b):
    a_hi, a_lo = _split_bf16(a)
    b_hi, b_lo = _split_bf16(b)
    return _dot(a_hi, b_hi) + _dot(a_lo, b_hi) + _dot(a_hi, b_lo)


def _dot_nt(a, b):
    return lax.dot_general(a, b, (((1,), (1,)), ((), ())), preferred_element_type=F32)


def _rms(x, g):
    ms = jnp.mean(x * x, axis=-1, keepdims=True)
    return x * lax.rsqrt(ms + EPS) * g


def _sigmoid(x):
    return 1.0 / (1.0 + jnp.exp(-x))


def _side_cast(srcs, dsts):
    for src, dst in zip(srcs, dsts):
        width = dst.shape[-1]
        for s in range(dst.shape[0]):
            dst[s] = src[:, s * width:(s + 1) * width].astype(dst.dtype)


def _side_cast_io(jobs, steps, step_of):
    in_specs, out_specs, out_shapes = [], [], []
    for w, splits in jobs:
        g, r, c = w.shape
        rb = g * r // steps
        per_group = r // rb
        in_specs.append(pl.BlockSpec(
            (None, rb, c), lambda *ids, pg=per_group: (step_of(*ids) // pg, step_of(*ids) % pg, 0)))
        out_specs.append(pl.BlockSpec(
            (None, splits, rb, c // splits),
            lambda *ids, pg=per_group: (step_of(*ids) // pg, 0, step_of(*ids) % pg, 0)))
        out_shapes.append(jax.ShapeDtypeStruct((g, splits, r, c // splits), BF16))
    return in_specs, out_specs, out_shapes


def _inproj_body(x_ref, g_ref, w_ref, o_ref, *rest):
    qkv_refs, qkv_f32 = rest[:-1], rest[-1]
    h = _rms(x_ref[...], g_ref[...]).astype(BF16)
    for c0 in range(0, QKV_COLS, MXU_WIDTH):
        res = _dot(h, w_ref[:, c0:c0 + MXU_WIDTH])
        for t in range(MXU_WIDTH // LANES):
            qkv_f32[c0 // LANES + t] = res[:, t * LANES:(t + 1) * LANES]
    for (_, dilation), ref in zip(DILATED_PATTERNS, qkv_refs):
        for r in range(dilation):
            rows = pl.ds(r, TM_PROJ // dilation, stride=dilation)
            ref[r] = jnp.concatenate([qkv_f32[t, rows, :] for t in range(QKV_COLS // LANES)],
                                     axis=-1).astype(ref.dtype)
    for c0 in range(0, U_COLS, MXU_WIDTH):
        c1 = min(c0 + MXU_WIDTH, U_COLS)
        o_ref[:, c0:c1] = _dot(h, w_ref[:, QKV_COLS + c0:QKV_COLS + c1])


def _subseq_spec(dilation, cols):
    tiles = SEQ // TM_PROJ
    return pl.BlockSpec((None, dilation, TM_PROJ // dilation, cols), lambda i: (i // tiles, 0, i % tiles, 0))


def _inproj(x, g, w_all, layer):
    qkv_shapes = [jax.ShapeDtypeStruct((BATCH, d, SEQ // d, QKV_COLS), BF16) for _, d in DILATED_PATTERNS]
    return pl.pallas_call(
        _inproj_body,
        grid=(TOKENS // TM_PROJ,),
        in_specs=[
            pl.BlockSpec((TM_PROJ, D_MODEL), lambda i: (i, 0)),
            pl.BlockSpec((1, D_MODEL), lambda i: (0, 0)),
            pl.BlockSpec((None, D_MODEL, QKV_COLS + U_COLS), lambda i: (layer, 0, 0)),
        ],
        out_specs=[pl.BlockSpec((TM_PROJ, U_COLS), lambda i: (i, 0))]
                  + [_subseq_spec(d, QKV_COLS) for _, d in DILATED_PATTERNS],
        out_shape=[jax.ShapeDtypeStruct((TOKENS, U_COLS), F32)] + qkv_shapes,
        scratch_shapes=[pltpu.VMEM((QKV_COLS // LANES, TM_PROJ, LANES), F32)],
        compiler_params=_params(("parallel",)),
        name="inproj",
    )(x, g, w_all)


def _rel_bucket(dist):
    max_exact = REL_BUCKETS // 2
    d = jnp.maximum(dist, 0)
    log_ratio = jnp.log(jnp.maximum(d, 1).astype(F32) / max_exact) / math.log(REL_MAX_DISTANCE / max_exact)
    large = jnp.minimum(max_exact + (log_ratio * (REL_BUCKETS - max_exact)).astype(jnp.int32), REL_BUCKETS - 1)
    return jnp.where(d < max_exact, d, large)


def _bucket_table(window, dilation):
    span = window // dilation
    qi = jnp.arange(ATT_BLOCK)[:, None]
    kj = jnp.arange(2 * ATT_BLOCK)[None, :]
    sub_dist = qi - kj + ATT_BLOCK
    band = (sub_dist >= 0) & (sub_dist <= span)
    return jnp.where(band, _rel_bucket(sub_dist * dilation), -1).astype(jnp.int32)


def _attn_body(rb_ref, bidx_ref, qkv_ref, *rest, sub_blocks, unroll):
    n_cast = (len(rest) - 3) // 2
    o_ref, lse_ref = rest[n_cast:n_cast + 2]
    bias_ref = rest[-1]
    _side_cast(rest[:n_cast], rest[n_cast + 2:-1])
    nblk = SEQ // ATT_BLOCK

    @pl.when(pl.program_id(0) == 0)
    def _():
        bidx = bidx_ref[...]
        in_prev = lax.broadcasted_iota(jnp.int32, bidx.shape, 1) < ATT_BLOCK
        for h in range(ATT_HEADS):
            acc = jnp.full(bidx.shape, NEG_BIG, F32)
            for b in range(REL_BUCKETS):
                acc = jnp.where(bidx == b, rb_ref[b, h], acc)
            bias_ref[0, h] = acc
            bias_ref[1, h] = jnp.where(in_prev, NEG_BIG, acc)
            bias_ref[2, h] = jnp.concatenate([acc[:, ATT_BLOCK:], jnp.full_like(acc[:, ATT_BLOCK:], NEG_BIG)], axis=1)

    def block(n, first):
        if first:
            rows, krows, variant = slice(0, ATT_BLOCK), slice(0, 2 * ATT_BLOCK), 2
        else:
            r0 = pl.multiple_of(n * ATT_BLOCK, ATT_BLOCK)
            rows = pl.ds(r0, ATT_BLOCK)
            krows = pl.ds(r0 - ATT_BLOCK, 2 * ATT_BLOCK)
            if sub_blocks == nblk:
                variant = 0
            elif sub_blocks == 1:
                variant = 1
            else:
                variant = jnp.where(n % sub_blocks == 0, 1, 0)
        q = qkv_ref[rows, 0:ATT_WIDTH]
        kk = qkv_ref[krows, ATT_WIDTH:2 * ATT_WIDTH]
        vv = qkv_ref[krows, 2 * ATT_WIDTH:3 * ATT_WIDTH]
        q = q * jnp.asarray(HEAD_DIM ** -0.5, BF16)
        head_of_lane = lax.broadcasted_iota(jnp.int32, (ATT_BLOCK, ATT_WIDTH), 1) // HEAD_DIM
        ones = jnp.ones((kk.shape[0], LANES), BF16)
        num = den = mx = None
        for h in range(ATT_HEADS):
            mine = head_of_lane == h
            bias = bias_ref[variant, h]
            s = _dot_nt(jnp.where(mine, q, jnp.zeros_like(q)), kk) + bias
            m = jnp.max(s, axis=-1, keepdims=True)
            p = jnp.exp(s - m).astype(BF16)
            num_h = _dot(p, vv)
            den_h = jnp.tile(_dot(p, ones), (1, ATT_WIDTH // LANES))
            m_h = jnp.broadcast_to(m, (ATT_BLOCK, ATT_WIDTH))
            num = num_h if h == 0 else jnp.where(mine, num_h, num)
            den = den_h if h == 0 else jnp.where(mine, den_h, den)
            mx = m_h if h == 0 else jnp.where(mine, m_h, mx)
        o_ref[rows, :] = (num / den).astype(o_ref.dtype)
        lse_ref[rows, :] = mx + jnp.log(den)

    block(0, True)

    def loop_body(n, carry):
        block(n, False)
        return carry
    lax.fori_loop(1, nblk, loop_body, 0, unroll=unroll)


def _attention_pattern(ua, rel_bias, window, dilation, side_casts=()):
    L = SEQ // dilation
    shape = (BATCH, dilation, L, ATT_WIDTH)
    qkv_spec = pl.BlockSpec((None, SEQ, QKV_COLS), lambda b: (b, 0, 0))
    out_spec = pl.BlockSpec((None, SEQ, ATT_WIDTH), lambda b: (b, 0, 0))
    cast_in, cast_out, cast_shapes = _side_cast_io(side_casts, BATCH, lambda b: b)
    o, lse, *casts = pl.pallas_call(
        functools.partial(_attn_body, sub_blocks=L // ATT_BLOCK, unroll=ATT_UNROLL),
        grid=(BATCH,),
        in_specs=[
            pl.BlockSpec(memory_space=pltpu.SMEM),
            pl.BlockSpec((ATT_BLOCK, 2 * ATT_BLOCK), lambda b: (0, 0)),
            qkv_spec,
        ] + cast_in,
        out_specs=[out_spec, out_spec] + cast_out,
        out_shape=[jax.ShapeDtypeStruct((BATCH, SEQ, ATT_WIDTH), BF16),
                   jax.ShapeDtypeStruct((BATCH, SEQ, ATT_WIDTH), F32)] + cast_shapes,
        scratch_shapes=[pltpu.VMEM((3, ATT_HEADS, ATT_BLOCK, 2 * ATT_BLOCK), F32)],
        compiler_params=_params(("arbitrary",)),
        name=f"attn_d{dilation}",
    )(rel_bias, _bucket_table(window, dilation), ua.reshape(BATCH, SEQ, QKV_COLS), *[w for w, _ in side_casts])
    return (o.reshape(shape), lse.reshape(shape)), casts


def _attention(qkvs, rel_bias, side_casts=()):
    side_casts = list(side_casts) + [None] * (len(DILATED_PATTERNS) - len(side_casts))
    res = [_attention_pattern(ua, rel_bias, window, dilation, [] if w is None else [w])
           for ua, (window, dilation), w in zip(qkvs, DILATED_PATTERNS, side_casts)]
    return [r[0] for r in res], [c for r in res for c in r[1]]


def _short_conv(cin_ref, cb_ref, cc_ref, w_ref, o_ref, tail_ref):
    uu = cc_ref[...] * cin_ref[...]
    t = lax.broadcasted_iota(jnp.int32, uu.shape, 0)
    y = uu * w_ref[CONV_WIDTH - 1:CONV_WIDTH, :]
    for shift in range(1, CONV_WIDTH):
        prev = pltpu.roll(uu, shift, axis=0)
        for r in range(shift):
            prev = jnp.where(t == r, tail_ref[8 - shift + r:8 - shift + r + 1, :], prev)
        y = y + prev * w_ref[CONV_WIDTH - 1 - shift:CONV_WIDTH - shift, :]
    o_ref[...] = (cb_ref[...] * y).astype(o_ref.dtype)
    tail_ref[...] = uu[uu.shape[0] - 8:, :]


def _gla_body(u_ref, wg_ref, bg_ref, gn_ref, ctril_ref, cw_ref, *rest):
    cols = lambda col, width: u_ref.at[:, col:col + width]
    q_ref, k_ref = cols(COL_GQ, GLA_QK), cols(COL_GK, GLA_QK)
    v_ref, gr_ref, glr_ref = cols(COL_GV, GLA_V), cols(COL_GR, GLA_V), cols(COL_GLR, LANES)
    cin_ref, cb_ref, cc_ref = cols(COL_CIN, CONV_CH), cols(COL_CB, CONV_CH), cols(COL_CC, CONV_CH)
    _gla_tile(q_ref, k_ref, v_ref, gr_ref, glr_ref, wg_ref, bg_ref, gn_ref, ctril_ref,
              cin_ref, cb_ref, cc_ref, cw_ref, *rest)


def _gla_tile(q_ref, k_ref, v_ref, gr_ref, glr_ref, wg_ref, bg_ref, gn_ref, ctril_ref,
              cin_ref, cb_ref, cc_ref, cw_ref, *rest):
    n_cast = (len(rest) - 4) // 2
    o_ref, cv_ref = rest[n_cast:n_cast + 2]
    s_ref, tail_ref = rest[-2:]
    _side_cast(rest[:n_cast], rest[n_cast + 2:-2])

    @pl.when(pl.program_id(1) == 0)
    def _():
        s_ref[...] = jnp.zeros_like(s_ref)
        tail_ref[...] = jnp.zeros_like(tail_ref)

    _short_conv(cin_ref, cb_ref, cc_ref, cw_ref, cv_ref, tail_ref)

    C = GLA_CHUNK
    row = lax.broadcasted_iota(jnp.int32, (C, C), 0)
    col = lax.broadcasted_iota(jnp.int32, (C, C), 1)
    tril = row >= col
    n_chunks = TC_GLA // C

    xg = _dot3(glr_ref[...], wg_ref[...]) + bg_ref[...]
    la_all = (jnp.minimum(xg, 0.0) - jnp.log(1.0 + jnp.exp(-jnp.abs(xg)))) * (1.0 / 16.0)

    la_hi, la_lo = _split_bf16(la_all)
    cum_all = _dot(ctril_ref[...], la_hi) + _dot(ctril_ref[...], la_lo)
    totals = jnp.concatenate([cum_all[(c + 1) * C - 1:(c + 1) * C, :] for c in range(n_chunks)]
                             + [jnp.zeros((LANES - n_chunks, GLA_QK), F32)], axis=0)
    decay_cols = jnp.exp(totals.T)

    for c in range(n_chunks):
        rows = slice(c * C, (c + 1) * C)
        cum = cum_all[rows]
        last = cum[C - 1:C, :]
        q = q_ref[rows, :]
        k = k_ref[rows, :]
        qt = (q * jnp.exp(cum) * (GLA_DK ** -0.5)).astype(BF16)
        kt = (k * jnp.exp(-cum)).astype(BF16)
        kl_t = (k * jnp.exp(last - cum)).T.astype(BF16)
        for h in range(GLA_HEADS):
            sl = slice(h * GLA_DK, (h + 1) * GLA_DK)
            vs = slice(h * GLA_DV, (h + 1) * GLA_DV)
            vh = v_ref[rows, vs].astype(BF16)
            state = s_ref[h]
            st_hi, st_lo = _split_bf16(state)
            sc = jnp.where(tril, _dot_nt(qt[:, sl], kt[:, sl]), 0.0).astype(BF16)
            o = _dot(qt[:, sl], st_hi) + _dot(qt[:, sl], st_lo) + _dot(sc, vh)
            decay = jnp.broadcast_to(decay_cols[sl, c:c + 1], state.shape)
            s_ref[h] = decay * state + _dot(kl_t[sl, :], vh)
            g = gr_ref[rows, vs]
            o_ref[rows, vs] = (_rms(o, gn_ref[...]) * (g * _sigmoid(g))).astype(o_ref.dtype)


def _gla_conv(u, wg, bg, gn, conv_w, side_casts=()):
    nj = SEQ // TC_GLA
    row = lambda b, j: b * nj + j
    full = lambda a: pl.BlockSpec(a.shape, lambda b, j: (0, 0))
    cast_in, cast_out, cast_shapes = _side_cast_io(side_casts, BATCH * nj, row)
    t = np.arange(TC_GLA)
    same_chunk = (t[:, None] // GLA_CHUNK) == (t[None, :] // GLA_CHUNK)
    chunk_tril = jnp.asarray(same_chunk & (t[:, None] >= t[None, :]), BF16)
    go, cv, *casts = pl.pallas_call(
        _gla_body,
        grid=(BATCH, nj),
        in_specs=[pl.BlockSpec((TC_GLA, U_COLS), lambda b, j: (row(b, j), 0)),
                  full(wg), full(bg), full(gn), full(chunk_tril), full(conv_w)] + cast_in,
        out_specs=[pl.BlockSpec((TC_GLA, GLA_V), lambda b, j: (row(b, j), 0)),
                   pl.BlockSpec((TC_GLA, CONV_CH), lambda b, j: (row(b, j), 0))] + cast_out,
        out_shape=[jax.ShapeDtypeStruct((TOKENS, GLA_V), BF16), jax.ShapeDtypeStruct((TOKENS, CONV_CH), BF16)]
                  + cast_shapes,
        scratch_shapes=[pltpu.VMEM((GLA_HEADS, GLA_DK, GLA_DV), F32), pltpu.VMEM((8, CONV_CH), F32)],
        compiler_params=_params(("arbitrary", "arbitrary")),
        name="gla_conv",
    )(u, wg, bg, gn, chunk_tril, conv_w, *[w for w, _ in side_casts])
    return go, cv, casts


def _outproj_body(*refs, route):
    (o1, o4, o16, l1, l4, l16, go_ref, cv_ref, x_ref, w_ref, g_ref) = refs[:11]
    perm_ref = refs[-1]
    if route:
        wr_ref, xo_ref, ho_ref, route_ref, counts_ref, carry_ref = refs[11:-1]
    else:
        xo_ref, ho_ref = refs[11:-1]

    def token_order(ref, slot):
        dilation, rows, _ = ref.shape
        if dilation == 1:
            return ref[0].astype(F32)
        tiles = range(ATT_WIDTH // LANES)
        for r in range(dilation):
            val = ref[r].astype(F32)
            for t in tiles:
                perm_ref[slot, t, pl.ds(r, rows, stride=dilation), :] = val[:, t * LANES:(t + 1) * LANES]
        return jnp.concatenate([perm_ref[slot, t] for t in tiles], axis=-1)

    la, lb, lc = token_order(l1, 0), token_order(l4, 0), token_order(l16, 1)
    oa, ob, oc = token_order(o1, 0), token_order(o4, 2), token_order(o16, 3)
    m = jnp.maximum(jnp.maximum(la, lb), lc)
    ea, eb, ec = jnp.exp(la - m), jnp.exp(lb - m), jnp.exp(lc - m)
    att = (ea * oa + eb * ob + ec * oc) / (ea + eb + ec)
    y = (x_ref[...]
         + _dot(att.astype(BF16), w_ref[0:ATT_WIDTH, :])
         + _dot(go_ref[...], w_ref[ATT_WIDTH:ATT_WIDTH + GLA_V, :])
         + _dot(cv_ref[...], w_ref[ATT_WIDTH + GLA_V:MIX_WIDTH, :]))
    xo_ref[...] = y
    hf = _rms(y, g_ref[...])
    ho_ref[...] = hf.astype(ho_ref.dtype)
    if route:
        @pl.when(pl.program_id(0) == 0)
        def _():
            carry_ref[...] = jnp.zeros_like(carry_ref)

        tm = hf.shape[0]
        ne = N_EXPERTS
        hf_hi, hf_lo = _split_bf16(hf)
        part = _dot_nt(wr_ref[0], hf_hi) + _dot_nt(wr_ref[1], hf_lo)
        logits = part[0:ne] + part[ne:2 * ne]
        eidx = lax.broadcasted_iota(jnp.int32, logits.shape, 0).astype(F32)
        v1 = jnp.max(logits, axis=0, keepdims=True)
        i1 = jnp.min(jnp.where(logits == v1, eidx, float(ne)), axis=0, keepdims=True)
        lg2 = jnp.where(eidx == i1, -jnp.inf, logits)
        v2 = jnp.max(lg2, axis=0, keepdims=True)
        i2 = jnp.min(jnp.where(lg2 == v2, eidx, float(ne)), axis=0, keepdims=True)
        e2 = jnp.exp(v2 - v1)
        w1 = 1.0 / (1.0 + e2)
        w2 = e2 * w1
        sel1 = eidx == i1
        sel2 = eidx == i2
        onehot = jnp.where(sel1, 1.0, jnp.where(sel2, 1.0, 0.0))
        tri = (lax.broadcasted_iota(jnp.int32, (tm, tm), 0) <= lax.broadcasted_iota(jnp.int32, (tm, tm), 1))
        onehot16 = jnp.concatenate([onehot, jnp.zeros_like(onehot)], axis=0).astype(BF16)
        csum = _dot(onehot16, jnp.where(tri, 1.0, 0.0).astype(BF16))[0:ne]
        carry = carry_ref[:, 0:1]
        rank = csum - onehot + carry
        r1 = jnp.sum(jnp.where(sel1, rank, 0.0), axis=0, keepdims=True)
        r2 = jnp.sum(jnp.where(sel2, rank, 0.0), axis=0, keepdims=True)
        total = jnp.broadcast_to(carry + csum[:, tm - 1:tm], carry_ref.shape)
        carry_ref[...] = total
        counts_ref[...] = total
        rows = {ROUTE_I1: i1, ROUTE_I2: i2, ROUTE_W1: w1, ROUTE_W2: w2, ROUTE_R1: r1, ROUTE_R2: r2}
        zero = jnp.zeros_like(i1)
        route_ref[...] = jnp.concatenate([rows.get(r, zero) for r in range(8)], axis=0)


def _outproj(att, go, cv, x, w, g, w_router=None):
    route = w_router is not None
    tm = TM_PROJ
    tile = lambda cols: pl.BlockSpec((tm, cols), lambda i: (i, 0))
    full = lambda a: pl.BlockSpec(a.shape, lambda i: (0, 0))
    (o1, l1), (o4, l4), (o16, l16) = att
    args = [o1, o4, o16, l1, l4, l16, go, cv, x, w, g]
    att_specs = [_subseq_spec(d, ATT_WIDTH) for _, d in DILATED_PATTERNS]
    in_specs = att_specs * 2 + [tile(GLA_V), tile(CONV_CH), tile(D_MODEL), full(w), full(g)]
    out_specs = [tile(D_MODEL), tile(D_MODEL)]
    out_shape = [jax.ShapeDtypeStruct((TOKENS, D_MODEL), F32),
                 jax.ShapeDtypeStruct((TOKENS, D_MODEL), F32 if route else BF16)]
    scratch = []
    if route:
        args.append(w_router)
        in_specs.append(pl.BlockSpec(w_router.shape, lambda i: (0, 0, 0)))
        out_specs += [pl.BlockSpec((8, tm), lambda i: (0, i)), pl.BlockSpec((N_EXPERTS, LANES), lambda i: (0, 0))]
        out_shape += [jax.ShapeDtypeStruct((8, TOKENS), F32), jax.ShapeDtypeStruct((N_EXPERTS, LANES), F32)]
        scratch = [pltpu.VMEM((N_EXPERTS, LANES), F32)]
    scratch.append(pltpu.VMEM((4, ATT_WIDTH // LANES, tm, LANES), F32))
    return pl.pallas_call(
        functools.partial(_outproj_body, route=route),
        grid=(TOKENS // tm,),
        in_specs=in_specs,
        out_specs=out_specs,
        out_shape=out_shape,
        scratch_shapes=scratch,
        compiler_params=_params(("arbitrary",) if route else ("parallel",)),
        name="outproj_route" if route else "outproj",
    )(*args)


def _swiglu_accumulate(h, w1_ref, w3_ref, w2_ref, acc_ref, tf):
    for c0 in range(0, tf, FFN_SUB):
        c1 = min(c0 + FFN_SUB, tf)
        a = _dot(h, w1_ref[:, c0:c1])
        b = _dot(h, w3_ref[:, c0:c1])
        act = a * _sigmoid(a) * b
        acc_ref[...] += _dot(act.astype(BF16), w2_ref[c0:c1, :])


def _ffn_body(x_ref, h_ref, w1_ref, w3_ref, w2_ref, *rest, tf):
    n_cast = (len(rest) - 1) // 2
    o_ref = rest[n_cast]

    @pl.when(pl.program_id(1) == 0)
    def _():
        o_ref[...] = x_ref[...]

    _swiglu_accumulate(h_ref[...], w1_ref, w3_ref, w2_ref, o_ref, tf)
    _side_cast(rest[:n_cast], rest[n_cast + 1:])


def _ffn(x, h, w1, w3, w2, *, side_casts=()):
    nj, _, tf = w1.shape
    tm = TM_FFN
    tile = lambda cols: pl.BlockSpec((tm, cols), lambda i, j: (i, 0))
    cast_in, cast_out, cast_shapes = _side_cast_io(side_casts, (TOKENS // tm) * nj, lambda i, j: i * nj + j)
    resident = dict(pipeline_mode=pl.Buffered(1)) if nj == 1 else {}
    res = pl.pallas_call(
        functools.partial(_ffn_body, tf=tf),
        grid=(TOKENS // tm, nj),
        in_specs=[tile(D_MODEL), tile(D_MODEL),
                  pl.BlockSpec((None, D_MODEL, tf), lambda i, j: (j, 0, 0), **resident),
                  pl.BlockSpec((None, D_MODEL, tf), lambda i, j: (j, 0, 0), **resident),
                  pl.BlockSpec((tf, D_MODEL), lambda i, j: (j, 0), **resident)] + cast_in,
        out_specs=[tile(D_MODEL)] + cast_out,
        out_shape=[jax.ShapeDtypeStruct((TOKENS, D_MODEL), F32)] + cast_shapes,
        compiler_params=_params(("parallel", "arbitrary")),
        name="dense_ffn",
    )(x, h, w1, w3, w2, *[w for w, _ in side_casts])
    return res[0], res[1:]


ROW_TILE = D_MODEL // LANES


def _to_row_tiled(dst_ref, lead, val):
    rows = val.shape[0]
    for s in range(ROW_TILE):
        dst_ref[(*lead, pl.ds(s, rows, stride=ROW_TILE), slice(None))] = val[:, s * LANES:(s + 1) * LANES]


def _from_row_tiled(src_ref, lead, rows):
    return jnp.concatenate([src_ref[(*lead, pl.ds(s, rows, stride=ROW_TILE), slice(None))]
                            for s in range(ROW_TILE)], axis=-1)


def _row_tile(idx):
    return pl.ds(pl.multiple_of(idx * ROW_TILE, ROW_TILE), ROW_TILE)


def _dispatch_body(pos_ref, last_tile_ref, h_ref, xs_hbm, stage_ref, zero_ref, sem, zero_sem):
    i = pl.program_id(0)
    n = pl.num_programs(0)
    tm = DISPATCH_CHUNK
    slot = i % 2

    def drain(s):
        for _ in range(2):
            pltpu.make_async_copy(stage_ref.at[s], xs_hbm.at[pl.ds(0, tm * ROW_TILE)], sem.at[s]).wait()

    @pl.when(i == 0)
    def _():
        zero_ref[...] = jnp.zeros_like(zero_ref)

        def zero_copy(e):
            start = pl.multiple_of(last_tile_ref[e] * (TM_MOE * ROW_TILE), TM_MOE * ROW_TILE)
            return pltpu.make_async_copy(zero_ref, xs_hbm.at[pl.ds(start, TM_MOE * ROW_TILE)], zero_sem)

        for e in range(2 * N_EXPERTS):
            @pl.when(last_tile_ref[e] >= 0)
            def _():
                zero_copy(e).start()
        for e in range(2 * N_EXPERTS):
            @pl.when(last_tile_ref[e] >= 0)
            def _():
                zero_copy(e).wait()

    @pl.when(i >= 2)
    def _():
        drain(slot)

    _to_row_tiled(stage_ref, (slot,), h_ref[...])

    def body(t, carry):
        for k in range(2):
            dst = pos_ref[2 * (i * tm + t) + k]
            pltpu.make_async_copy(stage_ref.at[slot, _row_tile(t)], xs_hbm.at[_row_tile(dst)],
                                  sem.at[slot]).start(priority=k)
        return carry
    lax.fori_loop(0, tm, body, 0, unroll=8)

    @pl.when(i == n - 1)
    def _():
        drain(1 - slot)
        drain(slot)


def _dispatch(pos, last_tile, h):
    tm = DISPATCH_CHUNK
    return pl.pallas_call(
        _dispatch_body,
        grid=(TOKENS // tm,),
        in_specs=[pl.BlockSpec(memory_space=pltpu.SMEM),
                  pl.BlockSpec(memory_space=pltpu.SMEM),
                  pl.BlockSpec((tm, D_MODEL), lambda i: (i, 0))],
        out_specs=pl.BlockSpec(memory_space=pl.ANY),
        out_shape=jax.ShapeDtypeStruct((N_SORTED * ROW_TILE, LANES), F32),
        scratch_shapes=[pltpu.VMEM((2, tm * ROW_TILE, LANES), F32),
                        pltpu.VMEM((TM_MOE * ROW_TILE, LANES), F32),
                        pltpu.SemaphoreType.DMA((2,)), pltpu.SemaphoreType.DMA(())],
        compiler_params=_params(("arbitrary",)),
        name="moe_dispatch",
    )(pos, last_tile, h)


def _gffn_body(te_ref, nu_ref, xs_ref, w1_ref, w3_ref, w2_ref, o_ref, hb_ref, acc_ref):
    del te_ref
    i = pl.program_id(0)
    j = pl.program_id(1)

    used = i < nu_ref[0]

    @pl.when(j == 0)
    def _():
        acc_ref[...] = jnp.zeros_like(acc_ref)

    @pl.when(used & (j == 0))
    def _():
        hb_ref[...] = _from_row_tiled(xs_ref, (), TM_MOE).astype(BF16)

    @pl.when(used)
    def _():
        _swiglu_accumulate(hb_ref[...], w1_ref, w3_ref, w2_ref, acc_ref, TF_MOE)

    @pl.when(j == pl.num_programs(1) - 1)
    def _():
        _to_row_tiled(o_ref, (), acc_ref[...])


def _grouped_ffn(tile_expert, n_used, xs, w1, w3, w2):
    nj = FFN_EXPERT // TF_MOE
    col = lambda i, j, nu: jnp.where(i < nu[0], j, nj - 1)
    grid_spec = pltpu.PrefetchScalarGridSpec(
        num_scalar_prefetch=2,
        grid=(N_TILES_MOE, nj),
        in_specs=[
            pl.BlockSpec((TM_MOE * ROW_TILE, LANES), lambda i, j, te, nu: (jnp.minimum(i, nu[0] - 1), 0)),
            pl.BlockSpec((None, None, D_MODEL, TF_MOE), lambda i, j, te, nu: (te[i], col(i, j, nu), 0, 0)),
            pl.BlockSpec((None, None, D_MODEL, TF_MOE), lambda i, j, te, nu: (te[i], col(i, j, nu), 0, 0)),
            pl.BlockSpec((None, TF_MOE, D_MODEL), lambda i, j, te, nu: (te[i], col(i, j, nu), 0)),
        ],
        out_specs=pl.BlockSpec((TM_MOE * ROW_TILE, LANES), lambda i, j, te, nu: (i, 0)),
        scratch_shapes=[pltpu.VMEM((TM_MOE, D_MODEL), BF16), pltpu.VMEM((TM_MOE, D_MODEL), F32)],
    )
    return pl.pallas_call(
        _gffn_body,
        grid_spec=grid_spec,
        out_shape=jax.ShapeDtypeStruct((N_SORTED * ROW_TILE, LANES), F32),
        compiler_params=_params(("arbitrary", "arbitrary")),
        name="moe_ffn",
    )(tile_expert, n_used, xs, w1, w3, w2)


def _combine_body(pos_ref, x_ref, gate_ref, g_ref, ys_hbm, o_ref, buf_ref, sem):
    i = pl.program_id(0)
    n = pl.num_programs(0)
    tm = TM_COMBINE

    def issue(tile, slot):
        def body(t, carry):
            for k in range(2):
                src = pos_ref[2 * (tile * tm + t) + k]
                pltpu.make_async_copy(ys_hbm.at[_row_tile(src)], buf_ref.at[slot, k, _row_tile(t)],
                                      sem.at[slot]).start(priority=k)
            return carry
        lax.fori_loop(0, tm, body, 0, unroll=8)

    @pl.when(i == 0)
    def _():
        issue(0, 0)

    @pl.when(i + 1 < n)
    def _():
        issue(i + 1, (i + 1) % 2)

    slot = i % 2
    for k in range(2):
        pltpu.make_async_copy(ys_hbm.at[pl.ds(0, tm * ROW_TILE)], buf_ref.at[slot, k], sem.at[slot]).wait()
    w1 = gate_ref[:, 0:1]
    w2 = gate_ref[:, 1:2]
    y = x_ref[...] + w1 * _from_row_tiled(buf_ref, (slot, 0), tm) + w2 * _from_row_tiled(buf_ref, (slot, 1), tm)
    o_ref[...] = _rms(y, g_ref[...])


def _combine(pos, x, gates, g, ys):
    tm = TM_COMBINE
    return pl.pallas_call(
        _combine_body,
        grid=(TOKENS // tm,),
        in_specs=[pl.BlockSpec(memory_space=pltpu.SMEM),
                  pl.BlockSpec((tm, D_MODEL), lambda i: (i, 0)),
                  pl.BlockSpec((tm, 2), lambda i: (i, 0)),
                  pl.BlockSpec((1, D_MODEL), lambda i: (0, 0)),
                  pl.BlockSpec(memory_space=pl.ANY)],
        out_specs=pl.BlockSpec((tm, D_MODEL), lambda i: (i, 0)),
        out_shape=jax.ShapeDtypeStruct((TOKENS, D_MODEL), F32),
        scratch_shapes=[pltpu.VMEM((2, 2, tm * ROW_TILE, LANES), F32), pltpu.SemaphoreType.DMA((2,))],
        compiler_params=_params(("arbitrary",)),
        name="moe_combine",
    )(pos, x, gates, g, ys)


def _routing_tables(route, counts):
    cnt = counts[:, 0].astype(jnp.int32)
    tiles = (cnt + TM_MOE - 1) // TM_MOE
    tile_end = jnp.cumsum(tiles)
    tile_start = tile_end - tiles
    n_used = tile_end[-1]
    expert = route[ROUTE_I1:ROUTE_I2 + 1].astype(jnp.int32)
    rank = route[ROUTE_R1:ROUTE_R2 + 1].astype(jnp.int32)
    group_start = jnp.sum(jnp.where(expert[..., None] == jnp.arange(N_EXPERTS), tile_start * TM_MOE, 0), axis=-1)
    pos = (group_start + rank).T
    tile_id = jnp.minimum(jnp.arange(N_TILES_MOE, dtype=jnp.int32), n_used - 1)
    tile_expert = jnp.sum(tile_id[:, None] >= tile_end[None, :], axis=1).astype(jnp.int32)
    last_tile = jnp.where(tiles > 0, tile_end - 1, -1)
    spare = n_used + jnp.arange(N_EXPERTS)
    zero_tiles = jnp.concatenate([last_tile, jnp.where(spare < N_TILES_MOE, spare, -1)]).astype(jnp.int32)
    return pos.reshape(2 * TOKENS), tile_expert, n_used.reshape(1), zero_tiles


def _moe(x, h, route, counts, w1, w3, w2, g_final):
    pos, tile_expert, n_used, last_tile = _routing_tables(route, counts)
    xs = _dispatch(pos, last_tile, h)
    ys = _grouped_ffn(tile_expert, n_used, xs, w1, w3, w2)
    return _combine(pos, x, route[ROUTE_W1:ROUTE_W2 + 1].T, g_final, ys)


def _prep_w_in(w):
    aq, ak, av, gq, gk, gv, gr, glr, c_in, c_b, c_c = jnp.split(w, np.cumsum(SPLIT_SIZES)[:-1].tolist(), axis=2)
    pad = jnp.zeros(w.shape[:2] + (LANES - GLA_RANK,), w.dtype)
    return jnp.concatenate([aq, ak, av, gv, gr, gq, gk, c_in, c_b, c_c, glr, pad], axis=2).astype(BF16)


def _prep_router(w):
    wt = w.T
    hi = wt.astype(BF16)
    lo = (wt - hi.astype(F32)).astype(BF16)
    return jnp.stack([jnp.concatenate([hi, lo]), jnp.concatenate([hi, jnp.zeros_like(hi)])])


def kernel(x, w_mix_in, w_mix_out, g_mix, rel_bias, gla_w_gate, gla_b_gate, gla_g_norm, conv_w,
           g_ffn, ffn_w1, ffn_w3, ffn_w2, moe_router, moe_w1, moe_w3, moe_w2, g_final):
    assert DEPTH == 2
    x = x.reshape(TOKENS, D_MODEL)
    up_job = lambda w, tf: (w, w.shape[-1] // tf)
    down = lambda c: c.reshape(c.shape[0], c.shape[2], c.shape[3])
    gla_jobs = {0: up_job(moe_w1[0], TF_MOE), 1: up_job(moe_w3[0], TF_MOE)}
    experts = {}
    w_in = _prep_w_in(w_mix_in)
    for layer in range(DEPTH):
        u, *qkvs = _inproj(x, g_mix[layer].reshape(1, D_MODEL), w_in, layer)
        dense_jobs = [up_job(ffn_w1[:1], TF_DENSE), up_job(ffn_w3[:1], TF_DENSE), (ffn_w2[:1], 1)] if layer == 0 else []
        att, dense_bf16 = _attention(qkvs, rel_bias, dense_jobs)
        if layer == 0:
            ffn_bf16 = (dense_bf16[0][0], dense_bf16[1][0], down(dense_bf16[2])[0])
        wg = jnp.pad(gla_w_gate[layer], ((0, LANES - GLA_RANK), (0, 0)))
        go, cv, (experts[layer],) = _gla_conv(u, wg, gla_b_gate[layer].reshape(1, GLA_QK),
                                              gla_g_norm[layer].reshape(1, GLA_DV),
                                              jnp.pad(conv_w[layer], ((0, 8 - CONV_WIDTH), (0, 0))),
                                              side_casts=[gla_jobs[layer]])
        w_out = w_mix_out[layer].astype(BF16)
        g2 = g_ffn[layer].reshape(1, D_MODEL)
        i = layer // 2
        if layer % 2 == 0:
            x, h = _outproj(att, go, cv, x, w_out, g2)
            x, (cast,) = _ffn(x, h, *ffn_bf16, side_casts=[(moe_w2[i], 1)])
            experts["w2"] = down(cast)
        else:
            x, h, route, counts = _outproj(att, go, cv, x, w_out, g2, _prep_router(moe_router[i]))
            x = _moe(x, h, route, counts, experts[0], experts[1], experts["w2"], g_final.reshape(1, D_MODEL))
    return x.reshape(BATCH, SEQ, D_MODEL)
```

```python
import functools
import math

import jax
import jax.numpy as jnp
import numpy as np
from jax import lax
from jax.experimental import pallas as pl
from jax.experimental.pallas import tpu as pltpu

F32 = jnp.float32
BF16 = jnp.bfloat16

D_MODEL = 1024
BATCH = 8
SEQ = 2048
TOKENS = BATCH * SEQ
DEPTH = 2
EPS = 1e-6

HEAD_DIM = 64
ATT_HEADS = 4
ATT_WIDTH = ATT_HEADS * HEAD_DIM
DILATED_PATTERNS = ((128, 1), (512, 4), (2048, 16))
ATT_BLOCK = 128
REL_BUCKETS = 32
REL_MAX_DISTANCE = 2048

GLA_HEADS = 4
GLA_DK = 64
GLA_DV = 128
GLA_RANK = 16
GLA_CHUNK = 64
GLA_QK = GLA_HEADS * GLA_DK
GLA_V = GLA_HEADS * GLA_DV

CONV_CH = 256
CONV_WIDTH = 3
MIX_WIDTH = ATT_WIDTH + GLA_V + CONV_CH

SPLIT_SIZES = (ATT_WIDTH, ATT_WIDTH, ATT_WIDTH, GLA_QK, GLA_QK, GLA_V, GLA_V, GLA_RANK,
               CONV_CH, CONV_CH, CONV_CH)

FFN_DENSE = 2816
N_EXPERTS = 8
FFN_EXPERT = 3584

LANES = 128
MXU_WIDTH = 256
VMEM_LIMIT = 56 * 1024 * 1024

QKV_COLS = 3 * ATT_WIDTH
COL_GV, COL_GR, COL_GQ, COL_GK = 0, 512, 1024, 1280
COL_CIN, COL_CB, COL_CC, COL_GLR = 1536, 1792, 2048, 2304
U_COLS = COL_GLR + LANES

NEG_BIG = -1e30

ATT_UNROLL = 15
TM_PROJ = 512
TC_GLA = 512
TM_FFN = 512
TF_DENSE = 2816
FFN_SUB = 256
TM_MOE = 512
TF_MOE = 1792
N_TILES_MOE = 2 * TOKENS // TM_MOE + N_EXPERTS
N_SORTED = N_TILES_MOE * TM_MOE
DISPATCH_CHUNK = 512
TM_COMBINE = 512

ROUTE_I1, ROUTE_I2, ROUTE_W1, ROUTE_W2, ROUTE_R1, ROUTE_R2 = range(6)


def _params(sem):
    return pltpu.CompilerParams(dimension_semantics=sem, vmem_limit_bytes=VMEM_LIMIT)


def _split_bf16(a):
    hi = a.astype(BF16)
    lo = (a - hi.astype(F32)).astype(BF16)
    return hi, lo


def _dot(a, b):
    return jnp.dot(a, b, preferred_element_type=F32)


def _dot3(a, b):
    a_hi, a_lo = _split_bf16(a)
    b_hi, b_lo = _split_bf16(b)
    return _dot(a_hi, b_hi) + _dot(a_lo, b_hi) + _dot(a_hi, b_lo)


def _dot_nt(a, b):
    return lax.dot_general(a, b, (((1,), (1,)), ((), ())), preferred_element_type=F32)


def _rms(x, g):
    ms = jnp.mean(x * x, axis=-1, keepdims=True)
    return x * lax.rsqrt(ms + EPS) * g


def _sigmoid(x):
    return 1.0 / (1.0 + jnp.exp(-x))


def _side_cast(srcs, dsts):
    for src, dst in zip(srcs, dsts):
        width = dst.shape[-1]
        for s in range(dst.shape[0]):
            dst[s] = src[:, s * width:(s + 1) * width].astype(dst.dtype)


def _side_cast_io(jobs, steps, step_of):
    in_specs, out_specs, out_shapes = [], [], []
    for w, splits in jobs:
        g, r, c = w.shape
        rb = g * r // steps
        per_group = r // rb
        in_specs.append(pl.BlockSpec(
            (None, rb, c), lambda *ids, pg=per_group: (step_of(*ids) // pg, step_of(*ids) % pg, 0)))
        out_specs.append(pl.BlockSpec(
            (None, splits, rb, c // splits),
            lambda *ids, pg=per_group: (step_of(*ids) // pg, 0, step_of(*ids) % pg, 0)))
        out_shapes.append(jax.ShapeDtypeStruct((g, splits, r, c // splits), BF16))
    return in_specs, out_specs, out_shapes


def _inproj_body(x_ref, g_ref, w_ref, o_ref, *rest):
    qkv_refs, qkv_f32 = rest[:-1], rest[-1]
    h = _rms(x_ref[...], g_ref[...]).astype(BF16)
    for c0 in range(0, QKV_COLS, MXU_WIDTH):
        res = _dot(h, w_ref[:, c0:c0 + MXU_WIDTH])
        for t in range(MXU_WIDTH // LANES):
            qkv_f32[c0 // LANES + t] = res[:, t * LANES:(t + 1) * LANES]
    for (_, dilation), ref in zip(DILATED_PATTERNS, qkv_refs):
        for r in range(dilation):
            rows = pl.ds(r, TM_PROJ // dilation, stride=dilation)
            ref[r] = jnp.concatenate([qkv_f32[t, rows, :] for t in range(QKV_COLS // LANES)],
                                     axis=-1).astype(ref.dtype)
    for c0 in range(0, U_COLS, MXU_WIDTH):
        c1 = min(c0 + MXU_WIDTH, U_COLS)
        o_ref[:, c0:c1] = _dot(h, w_ref[:, QKV_COLS + c0:QKV_COLS + c1])


def _subseq_spec(dilation, cols):
    tiles = SEQ // TM_PROJ
    return pl.BlockSpec((None, dilation, TM_PROJ // dilation, cols), lambda i: (i // tiles, 0, i % tiles, 0))


def _inproj(x, g, w_all, layer):
    qkv_shapes = [jax.ShapeDtypeStruct((BATCH, d, SEQ // d, QKV_COLS), BF16) for _, d in DILATED_PATTERNS]
    return pl.pallas_call(
        _inproj_body,
        grid=(TOKENS // TM_PROJ,),
        in_specs=[
            pl.BlockSpec((TM_PROJ, D_MODEL), lambda i: (i, 0)),
            pl.BlockSpec((1, D_MODEL), lambda i: (0, 0)),
            pl.BlockSpec((None, D_MODEL, QKV_COLS + U_COLS), lambda i: (layer, 0, 0)),
        ],
        out_specs=[pl.BlockSpec((TM_PROJ, U_COLS), lambda i: (i, 0))]
                  + [_subseq_spec(d, QKV_COLS) for _, d in DILATED_PATTERNS],
        out_shape=[jax.ShapeDtypeStruct((TOKENS, U_COLS), F32)] + qkv_shapes,
        scratch_shapes=[pltpu.VMEM((QKV_COLS // LANES, TM_PROJ, LANES), F32)],
        compiler_params=_params(("parallel",)),
        name="inproj",
    )(x, g, w_all)


def _rel_bucket(dist):
    max_exact = REL_BUCKETS // 2
    d = jnp.maximum(dist, 0)
    log_ratio = jnp.log(jnp.maximum(d, 1).astype(F32) / max_exact) / math.log(REL_MAX_DISTANCE / max_exact)
    large = jnp.minimum(max_exact + (log_ratio * (REL_BUCKETS - max_exact)).astype(jnp.int32), REL_BUCKETS - 1)
    return jnp.where(d < max_exact, d, large)


def _bucket_table(window, dilation):
    span = window // dilation
    qi = jnp.arange(ATT_BLOCK)[:, None]
    kj = jnp.arange(2 * ATT_BLOCK)[None, :]
    sub_dist = qi - kj + ATT_BLOCK
    band = (sub_dist >= 0) & (sub_dist <= span)
    return jnp.where(band, _rel_bucket(sub_dist * dilation), -1).astype(jnp.int32)


def _attn_body(rb_ref, bidx_ref, qkv_ref, *rest, sub_blocks, unroll):
    n_cast = (len(rest) - 3) // 2
    o_ref, lse_ref = rest[n_cast:n_cast + 2]
    bias_ref = rest[-1]
    _side_cast(rest[:n_cast], rest[n_cast + 2:-1])
    nblk = SEQ // ATT_BLOCK

    @pl.when(pl.program_id(0) == 0)
    def _():
        bidx = bidx_ref[...]
        in_prev = lax.broadcasted_iota(jnp.int32, bidx.shape, 1) < ATT_BLOCK
        for h in range(ATT_HEADS):
            acc = jnp.full(bidx.shape, NEG_BIG, F32)
            for b in range(REL_BUCKETS):
                acc = jnp.where(bidx == b, rb_ref[b, h], acc)
            bias_ref[0, h] = acc
            bias_ref[1, h] = jnp.where(in_prev, NEG_BIG, acc)
            bias_ref[2, h] = jnp.concatenate([acc[:, ATT_BLOCK:], jnp.full_like(acc[:, ATT_BLOCK:], NEG_BIG)], axis=1)

    def block(n, first):
        if first:
            rows, krows, variant = slice(0, ATT_BLOCK), slice(0, 2 * ATT_BLOCK), 2
        else:
            r0 = pl.multiple_of(n * ATT_BLOCK, ATT_BLOCK)
            rows = pl.ds(r0, ATT_BLOCK)
            krows = pl.ds(r0 - ATT_BLOCK, 2 * ATT_BLOCK)
            if sub_blocks == nblk:
                variant = 0
            elif sub_blocks == 1:
                variant = 1
            else:
                variant = jnp.where(n % sub_blocks == 0, 1, 0)
        q = qkv_ref[rows, 0:ATT_WIDTH]
        kk = qkv_ref[krows, ATT_WIDTH:2 * ATT_WIDTH]
        vv = qkv_ref[krows, 2 * ATT_WIDTH:3 * ATT_WIDTH]
        q = q * jnp.asarray(HEAD_DIM ** -0.5, BF16)
        head_of_lane = lax.broadcasted_iota(jnp.int32, (ATT_BLOCK, ATT_WIDTH), 1) // HEAD_DIM
        ones = jnp.ones((kk.shape[0], LANES), BF16)
        num = den = mx = None
        for h in range(ATT_HEADS):
            mine = head_of_lane == h
            bias = bias_ref[variant, h]
            s = _dot_nt(jnp.where(mine, q, jnp.zeros_like(q)), kk) + bias
            m = jnp.max(s, axis=-1, keepdims=True)
            p = jnp.exp(s - m).astype(BF16)
            num_h = _dot(p, vv)
            den_h = jnp.tile(_dot(p, ones), (1, ATT_WIDTH // LANES))
            m_h = jnp.broadcast_to(m, (ATT_BLOCK, ATT_WIDTH))
            num = num_h if h == 0 else jnp.where(mine, num_h, num)
            den = den_h if h == 0 else jnp.where(mine, den_h, den)
            mx = m_h if h == 0 else jnp.where(mine, m_h, mx)
        o_ref[rows, :] = (num / den).astype(o_ref.dtype)
        lse_ref[rows, :] = mx + jnp.log(den)

    block(0, True)

    def loop_body(n, carry):
        block(n, False)
        return carry
    lax.fori_loop(1, nblk, loop_body, 0, unroll=unroll)


def _attention_pattern(ua, rel_bias, window, dilation, side_casts=()):
    L = SEQ // dilation
    shape = (BATCH, dilation, L, ATT_WIDTH)
    qkv_spec = pl.BlockSpec((None, SEQ, QKV_COLS), lambda b: (b, 0, 0))
    out_spec = pl.BlockSpec((None, SEQ, ATT_WIDTH), lambda b: (b, 0, 0))
    cast_in, cast_out, cast_shapes = _side_cast_io(side_casts, BATCH, lambda b: b)
    o, lse, *casts = pl.pallas_call(
        functools.partial(_attn_body, sub_blocks=L // ATT_BLOCK, unroll=ATT_UNROLL),
        grid=(BATCH,),
        in_specs=[
            pl.BlockSpec(memory_space=pltpu.SMEM),
            pl.BlockSpec((ATT_BLOCK, 2 * ATT_BLOCK), lambda b: (0, 0)),
            qkv_spec,
        ] + cast_in,
        out_specs=[out_spec, out_spec] + cast_out,
        out_shape=[jax.ShapeDtypeStruct((BATCH, SEQ, ATT_WIDTH), BF16),
                   jax.ShapeDtypeStruct((BATCH, SEQ, ATT_WIDTH), F32)] + cast_shapes,
        scratch_shapes=[pltpu.VMEM((3, ATT_HEADS, ATT_BLOCK, 2 * ATT_BLOCK), F32)],
        compiler_params=_params(("arbitrary",)),
        name=f"attn_d{dilation}",
    )(rel_bias, _bucket_table(window, dilation), ua.reshape(BATCH, SEQ, QKV_COLS), *[w for w, _ in side_casts])
    return (o.reshape(shape), lse.reshape(shape)), casts


def _attention(qkvs, rel_bias, side_casts=()):
    side_casts = list(side_casts) + [None] * (len(DILATED_PATTERNS) - len(side_casts))
    res = [_attention_pattern(ua, rel_bias, window, dilation, [] if w is None else [w])
           for ua, (window, dilation), w in zip(qkvs, DILATED_PATTERNS, side_casts)]
    return [r[0] for r in res], [c for r in res for c in r[1]]


def _short_conv(cin_ref, cb_ref, cc_ref, w_ref, o_ref, tail_ref):
    uu = cc_ref[...] * cin_ref[...]
    t = lax.broadcasted_iota(jnp.int32, uu.shape, 0)
    y = uu * w_ref[CONV_WIDTH - 1:CONV_WIDTH, :]
    for shift in range(1, CONV_WIDTH):
        prev = pltpu.roll(uu, shift, axis=0)
        for r in range(shift):
            prev = jnp.where(t == r, tail_ref[8 - shift + r:8 - shift + r + 1, :], prev)
        y = y + prev * w_ref[CONV_WIDTH - 1 - shift:CONV_WIDTH - shift, :]
    o_ref[...] = (cb_ref[...] * y).astype(o_ref.dtype)
    tail_ref[...] = uu[uu.shape[0] - 8:, :]


def _gla_body(u_ref, wg_ref, bg_ref, gn_ref, ctril_ref, cw_ref, *rest):
    cols = lambda col, width: u_ref.at[:, col:col + width]
    q_ref, k_ref = cols(COL_GQ, GLA_QK), cols(COL_GK, GLA_QK)
    v_ref, gr_ref, glr_ref = cols(COL_GV, GLA_V), cols(COL_GR, GLA_V), cols(COL_GLR, LANES)
    cin_ref, cb_ref, cc_ref = cols(COL_CIN, CONV_CH), cols(COL_CB, CONV_CH), cols(COL_CC, CONV_CH)
    _gla_tile(q_ref, k_ref, v_ref, gr_ref, glr_ref, wg_ref, bg_ref, gn_ref, ctril_ref,
              cin_ref, cb_ref, cc_ref, cw_ref, *rest)


def _gla_tile(q_ref, k_ref, v_ref, gr_ref, glr_ref, wg_ref, bg_ref, gn_ref, ctril_ref,
              cin_ref, cb_ref, cc_ref, cw_ref, *rest):
    n_cast = (len(rest) - 4) // 2
    o_ref, cv_ref = rest[n_cast:n_cast + 2]
    s_ref, tail_ref = rest[-2:]
    _side_cast(rest[:n_cast], rest[n_cast + 2:-2])

    @pl.when(pl.program_id(1) == 0)
    def _():
        s_ref[...] = jnp.zeros_like(s_ref)
        tail_ref[...] = jnp.zeros_like(tail_ref)

    _short_conv(cin_ref, cb_ref, cc_ref, cw_ref, cv_ref, tail_ref)

    C = GLA_CHUNK
    row = lax.broadcasted_iota(jnp.int32, (C, C), 0)
    col = lax.broadcasted_iota(jnp.int32, (C, C), 1)
    tril = row >= col
    n_chunks = TC_GLA // C

    xg = _dot3(glr_ref[...], wg_ref[...]) + bg_ref[...]
    la_all = (jnp.minimum(xg, 0.0) - jnp.log(1.0 + jnp.exp(-jnp.abs(xg)))) * (1.0 / 16.0)

    la_hi, la_lo = _split_bf16(la_all)
    cum_all = _dot(ctril_ref[...], la_hi) + _dot(ctril_ref[...], la_lo)
    totals = jnp.concatenate([cum_all[(c + 1) * C - 1:(c + 1) * C, :] for c in range(n_chunks)]
                             + [jnp.zeros((LANES - n_chunks, GLA_QK), F32)], axis=0)
    decay_cols = jnp.exp(totals.T)

    for c in range(n_chunks):
        rows = slice(c * C, (c + 1) * C)
        cum = cum_all[rows]
        last = cum[C - 1:C, :]
        q = q_ref[rows, :]
        k = k_ref[rows, :]
        qs = q * (GLA_DK ** -0.5)
        qt = (qs * jnp.exp(cum)).astype(BF16)
        mid = cum[C // 2 - 1:C // 2, :]
        qm = (qs * jnp.exp(cum - mid)).astype(BF16)
        kt = (k * jnp.exp(mid - cum)).astype(BF16)
        kl_t = (k * jnp.exp(last - cum)).T.astype(BF16)
        for h in range(GLA_HEADS):
            sl = slice(h * GLA_DK, (h + 1) * GLA_DK)
            vs = slice(h * GLA_DV, (h + 1) * GLA_DV)
            vh = v_ref[rows, vs].astype(BF16)
            state = s_ref[h]
            st_hi, st_lo = _split_bf16(state)
            sc = jnp.where(tril, _dot_nt(qm[:, sl], kt[:, sl]), 0.0).astype(BF16)
            o = _dot(qt[:, sl], st_hi) + _dot(qt[:, sl], st_lo) + _dot(sc, vh)
            decay = jnp.broadcast_to(decay_cols[sl, c:c + 1], state.shape)
            s_ref[h] = decay * state + _dot(kl_t[sl, :], vh)
            g = gr_ref[rows, vs]
            o_ref[rows, vs] = (_rms(o, gn_ref[...]) * (g * _sigmoid(g))).astype(o_ref.dtype)


def _gla_conv(u, wg, bg, gn, conv_w, side_casts=()):
    nj = SEQ // TC_GLA
    row = lambda b, j: b * nj + j
    full = lambda a: pl.BlockSpec(a.shape, lambda b, j: (0, 0))
    cast_in, cast_out, cast_shapes = _side_cast_io(side_casts, BATCH * nj, row)
    t = np.arange(TC_GLA)
    same_chunk = (t[:, None] // GLA_CHUNK) == (t[None, :] // GLA_CHUNK)
    chunk_tril = jnp.asarray(same_chunk & (t[:, None] >= t[None, :]), BF16)
    go, cv, *casts = pl.pallas_call(
        _gla_body,
        grid=(BATCH, nj),
        in_specs=[pl.BlockSpec((TC_GLA, U_COLS), lambda b, j: (row(b, j), 0)),
                  full(wg), full(bg), full(gn), full(chunk_tril), full(conv_w)] + cast_in,
        out_specs=[pl.BlockSpec((TC_GLA, GLA_V), lambda b, j: (row(b, j), 0)),
                   pl.BlockSpec((TC_GLA, CONV_CH), lambda b, j: (row(b, j), 0))] + cast_out,
        out_shape=[jax.ShapeDtypeStruct((TOKENS, GLA_V), BF16), jax.ShapeDtypeStruct((TOKENS, CONV_CH), BF16)]
                  + cast_shapes,
        scratch_shapes=[pltpu.VMEM((GLA_HEADS, GLA_DK, GLA_DV), F32), pltpu.VMEM((8, CONV_CH), F32)],
        compiler_params=_params(("arbitrary", "arbitrary")),
        name="gla_conv",
    )(u, wg, bg, gn, chunk_tril, conv_w, *[w for w, _ in side_casts])
    return go, cv, casts


def _outproj_body(*refs, route):
    (o1, o4, o16, l1, l4, l16, go_ref, cv_ref, x_ref, w_ref, g_ref) = refs[:11]
    perm_ref = refs[-1]
    if route:
        wr_ref, xo_ref, ho_ref, route_ref, counts_ref, carry_ref = refs[11:-1]
    else:
        xo_ref, ho_ref = refs[11:-1]

    def token_order(ref, slot):
        dilation, rows, _ = ref.shape
        if dilation == 1:
            return ref[0].astype(F32)
        tiles = range(ATT_WIDTH // LANES)
        for r in range(dilation):
            val = ref[r].astype(F32)
            for t in tiles:
                perm_ref[slot, t, pl.ds(r, rows, stride=dilation), :] = val[:, t * LANES:(t + 1) * LANES]
        return jnp.concatenate([perm_ref[slot, t] for t in tiles], axis=-1)

    la, lb, lc = token_order(l1, 0), token_order(l4, 0), token_order(l16, 1)
    oa, ob, oc = token_order(o1, 0), token_order(o4, 2), token_order(o16, 3)
    m = jnp.maximum(jnp.maximum(la, lb), lc)
    ea, eb, ec = jnp.exp(la - m), jnp.exp(lb - m), jnp.exp(lc - m)
    att = (ea * oa + eb * ob + ec * oc) / (ea + eb + ec)
    y = (x_ref[...]
         + _dot(att.astype(BF16), w_ref[0:ATT_WIDTH, :])
         + _dot(go_ref[...], w_ref[ATT_WIDTH:ATT_WIDTH + GLA_V, :])
         + _dot(cv_ref[...], w_ref[ATT_WIDTH + GLA_V:MIX_WIDTH, :]))
    xo_ref[...] = y
    hf = _rms(y, g_ref[...])
    ho_ref[...] = hf.astype(ho_ref.dtype)
    if route:
        @pl.when(pl.program_id(0) == 0)
        def _():
            carry_ref[...] = jnp.zeros_like(carry_ref)

        tm = hf.shape[0]
        ne = N_EXPERTS
        hf_hi, hf_lo = _split_bf16(hf)
        part = _dot_nt(wr_ref[0], hf_hi) + _dot_nt(wr_ref[1], hf_lo)
        logits = part[0:ne] + part[ne:2 * ne]
        eidx = lax.broadcasted_iota(jnp.int32, logits.shape, 0).astype(F32)
        v1 = jnp.max(logits, axis=0, keepdims=True)
        i1 = jnp.min(jnp.where(logits == v1, eidx, float(ne)), axis=0, keepdims=True)
        lg2 = jnp.where(eidx == i1, -jnp.inf, logits)
        v2 = jnp.max(lg2, axis=0, keepdims=True)
        i2 = jnp.min(jnp.where(lg2 == v2, eidx, float(ne)), axis=0, keepdims=True)
        e2 = jnp.exp(v2 - v1)
        w1 = 1.0 / (1.0 + e2)
        w2 = e2 * w1
        sel1 = eidx == i1
        sel2 = eidx == i2
        onehot = jnp.where(sel1, 1.0, jnp.where(sel2, 1.0, 0.0))
        tri = (lax.broadcasted_iota(jnp.int32, (tm, tm), 0) <= lax.broadcasted_iota(jnp.int32, (tm, tm), 1))
        onehot16 = jnp.concatenate([onehot, jnp.zeros_like(onehot)], axis=0).astype(BF16)
        csum = _dot(onehot16, jnp.where(tri, 1.0, 0.0).astype(BF16))[0:ne]
        carry = carry_ref[:, 0:1]
        rank = csum - onehot + carry
        r1 = jnp.sum(jnp.where(sel1, rank, 0.0), axis=0, keepdims=True)
        r2 = jnp.sum(jnp.where(sel2, rank, 0.0), axis=0, keepdims=True)
        total = jnp.broadcast_to(carry + csum[:, tm - 1:tm], carry_ref.shape)
        carry_ref[...] = total
        counts_ref[...] = total
        rows = {ROUTE_I1: i1, ROUTE_I2: i2, ROUTE_W1: w1, ROUTE_W2: w2, ROUTE_R1: r1, ROUTE_R2: r2}
        zero = jnp.zeros_like(i1)
        route_ref[...] = jnp.concatenate([rows.get(r, zero) for r in range(8)], axis=0)


def _outproj(att, go, cv, x, w, g, w_router=None):
    route = w_router is not None
    tm = TM_PROJ
    tile = lambda cols: pl.BlockSpec((tm, cols), lambda i: (i, 0))
    full = lambda a: pl.BlockSpec(a.shape, lambda i: (0, 0))
    (o1, l1), (o4, l4), (o16, l16) = att
    args = [o1, o4, o16, l1, l4, l16, go, cv, x, w, g]
    att_specs = [_subseq_spec(d, ATT_WIDTH) for _, d in DILATED_PATTERNS]
    in_specs = att_specs * 2 + [tile(GLA_V), tile(CONV_CH), tile(D_MODEL), full(w), full(g)]
    out_specs = [tile(D_MODEL), tile(D_MODEL)]
    out_shape = [jax.ShapeDtypeStruct((TOKENS, D_MODEL), F32),
                 jax.ShapeDtypeStruct((TOKENS, D_MODEL), F32 if route else BF16)]
    scratch = []
    if route:
        args.append(w_router)
        in_specs.append(pl.BlockSpec(w_router.shape, lambda i: (0, 0, 0)))
        out_specs += [pl.BlockSpec((8, tm), lambda i: (0, i)), pl.BlockSpec((N_EXPERTS, LANES), lambda i: (0, 0))]
        out_shape += [jax.ShapeDtypeStruct((8, TOKENS), F32), jax.ShapeDtypeStruct((N_EXPERTS, LANES), F32)]
        scratch = [pltpu.VMEM((N_EXPERTS, LANES), F32)]
    scratch.append(pltpu.VMEM((4, ATT_WIDTH // LANES, tm, LANES), F32))
    return pl.pallas_call(
        functools.partial(_outproj_body, route=route),
        grid=(TOKENS // tm,),
        in_specs=in_specs,
        out_specs=out_specs,
        out_shape=out_shape,
        scratch_shapes=scratch,
        compiler_params=_params(("arbitrary",) if route else ("parallel",)),
        name="outproj_route" if route else "outproj",
    )(*args)


def _swiglu_accumulate(h, w1_ref, w3_ref, w2_ref, acc_ref, tf):
    for c0 in range(0, tf, FFN_SUB):
        c1 = min(c0 + FFN_SUB, tf)
        a = _dot(h, w1_ref[:, c0:c1])
        b = _dot(h, w3_ref[:, c0:c1])
        act = a * _sigmoid(a) * b
        acc_ref[...] += _dot(act.astype(BF16), w2_ref[c0:c1, :])


def _ffn_body(x_ref, h_ref, w1_ref, w3_ref, w2_ref, *rest, tf):
    n_cast = (len(rest) - 1) // 2
    o_ref = rest[n_cast]

    @pl.when(pl.program_id(1) == 0)
    def _():
        o_ref[...] = x_ref[...]

    _swiglu_accumulate(h_ref[...], w1_ref, w3_ref, w2_ref, o_ref, tf)
    _side_cast(rest[:n_cast], rest[n_cast + 1:])


def _ffn(x, h, w1, w3, w2, *, side_casts=()):
    nj, _, tf = w1.shape
    tm = TM_FFN
    tile = lambda cols: pl.BlockSpec((tm, cols), lambda i, j: (i, 0))
    cast_in, cast_out, cast_shapes = _side_cast_io(side_casts, (TOKENS // tm) * nj, lambda i, j: i * nj + j)
    resident = dict(pipeline_mode=pl.Buffered(1)) if nj == 1 else {}
    res = pl.pallas_call(
        functools.partial(_ffn_body, tf=tf),
        grid=(TOKENS // tm, nj),
        in_specs=[tile(D_MODEL), tile(D_MODEL),
                  pl.BlockSpec((None, D_MODEL, tf), lambda i, j: (j, 0, 0), **resident),
                  pl.BlockSpec((None, D_MODEL, tf), lambda i, j: (j, 0, 0), **resident),
                  pl.BlockSpec((tf, D_MODEL), lambda i, j: (j, 0), **resident)] + cast_in,
        out_specs=[tile(D_MODEL)] + cast_out,
        out_shape=[jax.ShapeDtypeStruct((TOKENS, D_MODEL), F32)] + cast_shapes,
        compiler_params=_params(("parallel", "arbitrary")),
        name="dense_ffn",
    )(x, h, w1, w3, w2, *[w for w, _ in side_casts])
    return res[0], res[1:]


ROW_TILE = D_MODEL // LANES


def _to_row_tiled(dst_ref, lead, val):
    rows = val.shape[0]
    for s in range(ROW_TILE):
        dst_ref[(*lead, pl.ds(s, rows, stride=ROW_TILE), slice(None))] = val[:, s * LANES:(s + 1) * LANES]


def _from_row_tiled(src_ref, lead, rows):
    return jnp.concatenate([src_ref[(*lead, pl.ds(s, rows, stride=ROW_TILE), slice(None))]
                            for s in range(ROW_TILE)], axis=-1)


def _row_tile(idx):
    return pl.ds(pl.multiple_of(idx * ROW_TILE, ROW_TILE), ROW_TILE)


def _dispatch_body(pos_ref, last_tile_ref, h_ref, xs_hbm, stage_ref, zero_ref, sem, zero_sem):
    i = pl.program_id(0)
    n = pl.num_programs(0)
    tm = DISPATCH_CHUNK
    slot = i % 2

    def drain(s):
        for _ in range(2):
            pltpu.make_async_copy(stage_ref.at[s], xs_hbm.at[pl.ds(0, tm * ROW_TILE)], sem.at[s]).wait()

    @pl.when(i == 0)
    def _():
        zero_ref[...] = jnp.zeros_like(zero_ref)

        def zero_copy(e):
            start = pl.multiple_of(last_tile_ref[e] * (TM_MOE * ROW_TILE), TM_MOE * ROW_TILE)
            return pltpu.make_async_copy(zero_ref, xs_hbm.at[pl.ds(start, TM_MOE * ROW_TILE)], zero_sem)

        for e in range(2 * N_EXPERTS):
            @pl.when(last_tile_ref[e] >= 0)
            def _():
                zero_copy(e).start()
        for e in range(2 * N_EXPERTS):
            @pl.when(last_tile_ref[e] >= 0)
            def _():
                zero_copy(e).wait()

    @pl.when(i >= 2)
    def _():
        drain(slot)

    _to_row_tiled(stage_ref, (slot,), h_ref[...])

    def body(t, carry):
        for k in range(2):
            dst = pos_ref[2 * (i * tm + t) + k]
            pltpu.make_async_copy(stage_ref.at[slot, _row_tile(t)], xs_hbm.at[_row_tile(dst)],
                                  sem.at[slot]).start(priority=k)
        return carry
    lax.fori_loop(0, tm, body, 0, unroll=8)

    @pl.when(i == n - 1)
    def _():
        drain(1 - slot)
        drain(slot)


def _dispatch(pos, last_tile, h):
    tm = DISPATCH_CHUNK
    return pl.pallas_call(
        _dispatch_body,
        grid=(TOKENS // tm,),
        in_specs=[pl.BlockSpec(memory_space=pltpu.SMEM),
                  pl.BlockSpec(memory_space=pltpu.SMEM),
                  pl.BlockSpec((tm, D_MODEL), lambda i: (i, 0))],
        out_specs=pl.BlockSpec(memory_space=pl.ANY),
        out_shape=jax.ShapeDtypeStruct((N_SORTED * ROW_TILE, LANES), F32),
        scratch_shapes=[pltpu.VMEM((2, tm * ROW_TILE, LANES), F32),
                        pltpu.VMEM((TM_MOE * ROW_TILE, LANES), F32),
                        pltpu.SemaphoreType.DMA((2,)), pltpu.SemaphoreType.DMA(())],
        compiler_params=_params(("arbitrary",)),
        name="moe_dispatch",
    )(pos, last_tile, h)


def _gffn_body(te_ref, nu_ref, xs_ref, w1_ref, w3_ref, w2_ref, o_ref, hb_ref, acc_ref):
    del te_ref
    i = pl.program_id(0)
    j = pl.program_id(1)

    used = i < nu_ref[0]

    @pl.when(j == 0)
    def _():
        acc_ref[...] = jnp.zeros_like(acc_ref)

    @pl.when(used & (j == 0))
    def _():
        hb_ref[...] = _from_row_tiled(xs_ref, (), TM_MOE).astype(BF16)

    @pl.when(used)
    def _():
        _swiglu_accumulate(hb_ref[...], w1_ref, w3_ref, w2_ref, acc_ref, TF_MOE)

    @pl.when(j == pl.num_programs(1) - 1)
    def _():
        _to_row_tiled(o_ref, (), acc_ref[...])


def _grouped_ffn(tile_expert, n_used, xs, w1, w3, w2):
    nj = FFN_EXPERT // TF_MOE
    col = lambda i, j, nu: jnp.where(i < nu[0], j, nj - 1)
    grid_spec = pltpu.PrefetchScalarGridSpec(
        num_scalar_prefetch=2,
        grid=(N_TILES_MOE, nj),
        in_specs=[
            pl.BlockSpec((TM_MOE * ROW_TILE, LANES), lambda i, j, te, nu: (jnp.minimum(i, nu[0] - 1), 0)),
            pl.BlockSpec((None, None, D_MODEL, TF_MOE), lambda i, j, te, nu: (te[i], col(i, j, nu), 0, 0)),
            pl.BlockSpec((None, None, D_MODEL, TF_MOE), lambda i, j, te, nu: (te[i], col(i, j, nu), 0, 0)),
            pl.BlockSpec((None, TF_MOE, D_MODEL), lambda i, j, te, nu: (te[i], col(i, j, nu), 0)),
        ],
        out_specs=pl.BlockSpec((TM_MOE * ROW_TILE, LANES), lambda i, j, te, nu: (i, 0)),
        scratch_shapes=[pltpu.VMEM((TM_MOE, D_MODEL), BF16), pltpu.VMEM((TM_MOE, D_MODEL), F32)],
    )
    return pl.pallas_call(
        _gffn_body,
        grid_spec=grid_spec,
        out_shape=jax.ShapeDtypeStruct((N_SORTED * ROW_TILE, LANES), F32),
        compiler_params=_params(("arbitrary", "arbitrary")),
        name="moe_ffn",
    )(tile_expert, n_used, xs, w1, w3, w2)


def _combine_body(pos_ref, x_ref, gate_ref, g_ref, ys_hbm, o_ref, buf_ref, sem):
    i = pl.program_id(0)
    n = pl.num_programs(0)
    tm = TM_COMBINE

    def issue(tile, slot):
        def body(t, carry):
            for k in range(2):
                src = pos_ref[2 * (tile * tm + t) + k]
                pltpu.make_async_copy(ys_hbm.at[_row_tile(src)], buf_ref.at[slot, k, _row_tile(t)],
                                      sem.at[slot]).start(priority=k)
            return carry
        lax.fori_loop(0, tm, body, 0, unroll=8)

    @pl.when(i == 0)
    def _():
        issue(0, 0)

    @pl.when(i + 1 < n)
    def _():
        issue(i + 1, (i + 1) % 2)

    slot = i % 2
    for k in range(2):
        pltpu.make_async_copy(ys_hbm.at[pl.ds(0, tm * ROW_TILE)], buf_ref.at[slot, k], sem.at[slot]).wait()
    w1 = gate_ref[:, 0:1]
    w2 = gate_ref[:, 1:2]
    y = x_ref[...] + w1 * _from_row_tiled(buf_ref, (slot, 0), tm) + w2 * _from_row_tiled(buf_ref, (slot, 1), tm)
    o_ref[...] = _rms(y, g_ref[...])


def _combine(pos, x, gates, g, ys):
    tm = TM_COMBINE
    return pl.pallas_call(
        _combine_body,
        grid=(TOKENS // tm,),
        in_specs=[pl.BlockSpec(memory_space=pltpu.SMEM),
                  pl.BlockSpec((tm, D_MODEL), lambda i: (i, 0)),
                  pl.BlockSpec((tm, 2), lambda i: (i, 0)),
                  pl.BlockSpec((1, D_MODEL), lambda i: (0, 0)),
                  pl.BlockSpec(memory_space=pl.ANY)],
        out_specs=pl.BlockSpec((tm, D_MODEL), lambda i: (i, 0)),
        out_shape=jax.ShapeDtypeStruct((TOKENS, D_MODEL), F32),
        scratch_shapes=[pltpu.VMEM((2, 2, tm * ROW_TILE, LANES), F32), pltpu.SemaphoreType.DMA((2,))],
        compiler_params=_params(("arbitrary",)),
        name="moe_combine",
    )(pos, x, gates, g, ys)


def _routing_tables(route, counts):
    cnt = counts[:, 0].astype(jnp.int32)
    tiles = (cnt + TM_MOE - 1) // TM_MOE
    tile_end = jnp.cumsum(tiles)
    tile_start = tile_end - tiles
    n_used = tile_end[-1]
    expert = route[ROUTE_I1:ROUTE_I2 + 1].astype(jnp.int32)
    rank = route[ROUTE_R1:ROUTE_R2 + 1].astype(jnp.int32)
    group_start = jnp.sum(jnp.where(expert[..., None] == jnp.arange(N_EXPERTS), tile_start * TM_MOE, 0), axis=-1)
    pos = (group_start + rank).T
    tile_id = jnp.minimum(jnp.arange(N_TILES_MOE, dtype=jnp.int32), n_used - 1)
    tile_expert = jnp.sum(tile_id[:, None] >= tile_end[None, :], axis=1).astype(jnp.int32)
    last_tile = jnp.where(tiles > 0, tile_end - 1, -1)
    spare = n_used + jnp.arange(N_EXPERTS)
    zero_tiles = jnp.concatenate([last_tile, jnp.where(spare < N_TILES_MOE, spare, -1)]).astype(jnp.int32)
    return pos.reshape(2 * TOKENS), tile_expert, n_used.reshape(1), zero_tiles


def _moe(x, h, route, counts, w1, w3, w2, g_final):
    pos, tile_expert, n_used, last_tile = _routing_tables(route, counts)
    xs = _dispatch(pos, last_tile, h)
    ys = _grouped_ffn(tile_expert, n_used, xs, w1, w3, w2)
    return _combine(pos, x, route[ROUTE_W1:ROUTE_W2 + 1].T, g_final, ys)


def _prep_w_in(w):
    aq, ak, av, gq, gk, gv, gr, glr, c_in, c_b, c_c = jnp.split(w, np.cumsum(SPLIT_SIZES)[:-1].tolist(), axis=2)
    pad = jnp.zeros(w.shape[:2] + (LANES - GLA_RANK,), w.dtype)
    return jnp.concatenate([aq, ak, av, gv, gr, gq, gk, c_in, c_b, c_c, glr, pad], axis=2).astype(BF16)


def _prep_router(w):
    wt = w.T
    hi = wt.astype(BF16)
    lo = (wt - hi.astype(F32)).astype(BF16)
    return jnp.stack([jnp.concatenate([hi, lo]), jnp.concatenate([hi, jnp.zeros_like(hi)])])


def kernel(x, w_mix_in, w_mix_out, g_mix, rel_bias, gla_w_gate, gla_b_gate, gla_g_norm, conv_w,
           g_ffn, ffn_w1, ffn_w3, ffn_w2, moe_router, moe_w1, moe_w3, moe_w2, g_final):
    assert DEPTH == 2
    x = x.reshape(TOKENS, D_MODEL)
    up_job = lambda w, tf: (w, w.shape[-1] // tf)
    down = lambda c: c.reshape(c.shape[0], c.shape[2], c.shape[3])
    gla_jobs = {0: up_job(moe_w1[0], TF_MOE), 1: up_job(moe_w3[0], TF_MOE)}
    experts = {}
    w_in = _prep_w_in(w_mix_in)
    for layer in range(DEPTH):
        u, *qkvs = _inproj(x, g_mix[layer].reshape(1, D_MODEL), w_in, layer)
        dense_jobs = [up_job(ffn_w1[:1], TF_DENSE), up_job(ffn_w3[:1], TF_DENSE), (ffn_w2[:1], 1)] if layer == 0 else []
        att, dense_bf16 = _attention(qkvs, rel_bias, dense_jobs)
        if layer == 0:
            ffn_bf16 = (dense_bf16[0][0], dense_bf16[1][0], down(dense_bf16[2])[0])
        wg = jnp.pad(gla_w_gate[layer], ((0, LANES - GLA_RANK), (0, 0)))
        go, cv, (experts[layer],) = _gla_conv(u, wg, gla_b_gate[layer].reshape(1, GLA_QK),
                                              gla_g_norm[layer].reshape(1, GLA_DV),
                                              jnp.pad(conv_w[layer], ((0, 8 - CONV_WIDTH), (0, 0))),
                                              side_casts=[gla_jobs[layer]])
        w_out = w_mix_out[layer].astype(BF16)
        g2 = g_ffn[layer].reshape(1, D_MODEL)
        i = layer // 2
        if layer % 2 == 0:
            x, h = _outproj(att, go, cv, x, w_out, g2)
            x, (cast,) = _ffn(x, h, *ffn_bf16, side_casts=[(moe_w2[i], 1)])
            experts["w2"] = down(cast)
        else:
            x, h, route, counts = _outproj(att, go, cv, x, w_out, g2, _prep_router(moe_router[i]))
            x = _moe(x, h, route, counts, experts[0], experts[1], experts["w2"], g_final.reshape(1, D_MODEL))
    return x.reshape(BATCH, SEQ, D_MODEL)
```

```python
import functools
import math

import jax
import jax.numpy as jnp
import numpy as np
from jax import lax
from jax.experimental import pallas as pl
from jax.experimental.pallas import tpu as pltpu

F32 = jnp.float32
BF16 = jnp.bfloat16

D_MODEL = 1024
BATCH = 8
SEQ = 2048
TOKENS = BATCH * SEQ
DEPTH = 2
EPS = 1e-6

HEAD_DIM = 64
ATT_HEADS = 4
ATT_WIDTH = ATT_HEADS * HEAD_DIM
DILATED_PATTERNS = ((128, 1), (512, 4), (2048, 16))
ATT_BLOCK = 128
REL_BUCKETS = 32
REL_MAX_DISTANCE = 2048

GLA_HEADS = 4
GLA_DK = 64
GLA_DV = 128
GLA_RANK = 16
GLA_CHUNK = 64
GLA_QK = GLA_HEADS * GLA_DK
GLA_V = GLA_HEADS * GLA_DV

CONV_CH = 256
CONV_WIDTH = 3
MIX_WIDTH = ATT_WIDTH + GLA_V + CONV_CH

SPLIT_SIZES = (ATT_WIDTH, ATT_WIDTH, ATT_WIDTH, GLA_QK, GLA_QK, GLA_V, GLA_V, GLA_RANK,
               CONV_CH, CONV_CH, CONV_CH)

FFN_DENSE = 2816
N_EXPERTS = 8
FFN_EXPERT = 3584

LANES = 128
MXU_WIDTH = 256
VMEM_LIMIT = 56 * 1024 * 1024

QKV_COLS = 3 * ATT_WIDTH
COL_GV, COL_GR, COL_GQ, COL_GK = 0, 512, 1024, 1280
COL_CIN, COL_CB, COL_CC, COL_GLR = 1536, 1792, 2048, 2304
U_COLS = COL_GLR + LANES

NEG_BIG = -1e30

ATT_UNROLL = 15
TM_PROJ = 512
TC_GLA = 512
TM_FFN = 512
TF_DENSE = 2816
FFN_SUB = 256
TM_MOE = 512
TF_MOE = 1792
N_TILES_MOE = 2 * TOKENS // TM_MOE + N_EXPERTS
N_SORTED = N_TILES_MOE * TM_MOE
DISPATCH_CHUNK = 512
TM_COMBINE = 512

ROUTE_I1, ROUTE_I2, ROUTE_W1, ROUTE_W2, ROUTE_R1, ROUTE_R2 = range(6)


def _params(sem):
    return pltpu.CompilerParams(dimension_semantics=sem, vmem_limit_bytes=VMEM_LIMIT)


def _split_bf16(a):
    hi = a.astype(BF16)
    lo = (a - hi.astype(F32)).astype(BF16)
    return hi, lo


def _dot(a, b):
    return jnp.dot(a, b, preferred_element_type=F32)


def _dot3(a, b):
    a_hi, a_lo = _split_bf16(a)
    b_hi, b_lo = _split_bf16(b)
    return _dot(a_hi, b_hi) + _dot(a_lo, b_hi) + _dot(a_hi, b_lo)


def _dot_nt(a, b):
    return lax.dot_general(a, b, (((1,), (1,)), ((), ())), preferred_element_type=F32)


def _rms(x, g):
    ms = jnp.mean(x * x, axis=-1, keepdims=True)
    return x * lax.rsqrt(ms + EPS) * g


def _sigmoid(x):
    return 1.0 / (1.0 + jnp.exp(-x))


def _side_cast(srcs, dsts):
    for src, dst in zip(srcs, dsts):
        width = dst.shape[-1]
        for s in range(dst.shape[0]):
            dst[s] = src[:, s * width:(s + 1) * width].astype(dst.dtype)


def _side_cast_io(jobs, steps, step_of):
    in_specs, out_specs, out_shapes = [], [], []
    for w, splits in jobs:
        g, r, c = w.shape
        rb = g * r // steps
        per_group = r // rb
        in_specs.append(pl.BlockSpec(
            (None, rb, c), lambda *ids, pg=per_group: (step_of(*ids) // pg, step_of(*ids) % pg, 0)))
        out_specs.append(pl.BlockSpec(
            (None, splits, rb, c // splits),
            lambda *ids, pg=per_group: (step_of(*ids) // pg, 0, step_of(*ids) % pg, 0)))
        out_shapes.append(jax.ShapeDtypeStruct((g, splits, r, c // splits), BF16))
    return in_specs, out_specs, out_shapes


def _inproj_body(x_ref, g_ref, w_ref, o_ref, *rest):
    qkv_refs, qkv_f32 = rest[:-1], rest[-1]
    h = _rms(x_ref[...], g_ref[...]).astype(BF16)
    for c0 in range(0, QKV_COLS, MXU_WIDTH):
        res = _dot(h, w_ref[:, c0:c0 + MXU_WIDTH])
        for t in range(MXU_WIDTH // LANES):
            qkv_f32[c0 // LANES + t] = res[:, t * LANES:(t + 1) * LANES]
    for (_, dilation), ref in zip(DILATED_PATTERNS, qkv_refs):
        for r in range(dilation):
            rows = pl.ds(r, TM_PROJ // dilation, stride=dilation)
            ref[r] = jnp.concatenate([qkv_f32[t, rows, :] for t in range(QKV_COLS // LANES)],
                                     axis=-1).astype(ref.dtype)
    for c0 in range(0, U_COLS, MXU_WIDTH):
        c1 = min(c0 + MXU_WIDTH, U_COLS)
        o_ref[:, c0:c1] = _dot(h, w_ref[:, QKV_COLS + c0:QKV_COLS + c1])


def _subseq_spec(dilation, cols):
    tiles = SEQ // TM_PROJ
    return pl.BlockSpec((None, dilation, TM_PROJ // dilation, cols), lambda i: (i // tiles, 0, i % tiles, 0))


def _inproj(x, g, w_all, layer):
    qkv_shapes = [jax.ShapeDtypeStruct((BATCH, d, SEQ // d, QKV_COLS), BF16) for _, d in DILATED_PATTERNS]
    return pl.pallas_call(
        _inproj_body,
        grid=(TOKENS // TM_PROJ,),
        in_specs=[
            pl.BlockSpec((TM_PROJ, D_MODEL), lambda i: (i, 0)),
            pl.BlockSpec((1, D_MODEL), lambda i: (0, 0)),
            pl.BlockSpec((None, D_MODEL, QKV_COLS + U_COLS), lambda i: (layer, 0, 0)),
        ],
        out_specs=[pl.BlockSpec((TM_PROJ, U_COLS), lambda i: (i, 0))]
                  + [_subseq_spec(d, QKV_COLS) for _, d in DILATED_PATTERNS],
        out_shape=[jax.ShapeDtypeStruct((TOKENS, U_COLS), F32)] + qkv_shapes,
        scratch_shapes=[pltpu.VMEM((QKV_COLS // LANES, TM_PROJ, LANES), F32)],
        compiler_params=_params(("parallel",)),
        name="inproj",
    )(x, g, w_all)


def _rel_bucket(dist):
    max_exact = REL_BUCKETS // 2
    d = jnp.maximum(dist, 0)
    log_ratio = jnp.log(jnp.maximum(d, 1).astype(F32) / max_exact) / math.log(REL_MAX_DISTANCE / max_exact)
    large = jnp.minimum(max_exact + (log_ratio * (REL_BUCKETS - max_exact)).astype(jnp.int32), REL_BUCKETS - 1)
    return jnp.where(d < max_exact, d, large)


def _bucket_table(window, dilation):
    span = window // dilation
    qi = jnp.arange(ATT_BLOCK)[:, None]
    kj = jnp.arange(2 * ATT_BLOCK)[None, :]
    sub_dist = qi - kj + ATT_BLOCK
    band = (sub_dist >= 0) & (sub_dist <= span)
    return jnp.where(band, _rel_bucket(sub_dist * dilation), -1).astype(jnp.int32)


def _attn_body(rb_ref, bidx_ref, *rest):
    n_pat = len(DILATED_PATTERNS)
    qkv_refs, rest = rest[:n_pat], rest[n_pat:]
    n_cast = (len(rest) - 4) // 2
    o_ref = rest[n_cast]
    bias_ref, out_s, lse_s = rest[-3:]
    _side_cast(rest[:n_cast], rest[n_cast + 1:-3])
    nblk = SEQ // ATT_BLOCK
    lane_tiles = range(ATT_WIDTH // LANES)

    @pl.when(pl.program_id(0) == 0)
    def _():
        for p in range(n_pat):
            bidx = bidx_ref[p]
            in_prev = lax.broadcasted_iota(jnp.int32, bidx.shape, 1) < ATT_BLOCK
            for h in range(ATT_HEADS):
                acc = jnp.full(bidx.shape, NEG_BIG, F32)
                for b in range(REL_BUCKETS):
                    acc = jnp.where(bidx == b, rb_ref[b, h], acc)
                bias_ref[p, 0, h] = acc
                bias_ref[p, 1, h] = jnp.where(in_prev, NEG_BIG, acc)
                bias_ref[p, 2, h] = jnp.concatenate(
                    [acc[:, ATT_BLOCK:], jnp.full_like(acc[:, ATT_BLOCK:], NEG_BIG)], axis=1)

    def block(p, n):
        dilation = DILATED_PATTERNS[p][1]
        sub_blocks = SEQ // dilation // ATT_BLOCK
        qkv_ref = qkv_refs[p]
        rows = slice(n * ATT_BLOCK, (n + 1) * ATT_BLOCK)
        if n == 0:
            krows, variant = slice(0, 2 * ATT_BLOCK), 2
        else:
            krows = slice((n - 1) * ATT_BLOCK, (n + 1) * ATT_BLOCK)
            variant = 1 if n % sub_blocks == 0 else 0
        q = qkv_ref[rows, 0:ATT_WIDTH]
        kk = qkv_ref[krows, ATT_WIDTH:2 * ATT_WIDTH]
        vv = qkv_ref[krows, 2 * ATT_WIDTH:3 * ATT_WIDTH]
        q = q * jnp.asarray(HEAD_DIM ** -0.5, BF16)
        head_of_lane = lax.broadcasted_iota(jnp.int32, (ATT_BLOCK, ATT_WIDTH), 1) // HEAD_DIM
        ones = jnp.ones((kk.shape[0], LANES), BF16)
        num = den = mx = None
        for h in range(ATT_HEADS):
            mine = head_of_lane == h
            bias = bias_ref[p, variant, h]
            s = _dot_nt(jnp.where(mine, q, jnp.zeros_like(q)), kk) + bias
            m = jnp.max(s, axis=-1, keepdims=True)
            prob = jnp.exp(s - m).astype(BF16)
            num_h = _dot(prob, vv)
            den_h = jnp.tile(_dot(prob, ones), (1, ATT_WIDTH // LANES))
            m_h = jnp.broadcast_to(m, (ATT_BLOCK, ATT_WIDTH))
            num = num_h if h == 0 else jnp.where(mine, num_h, num)
            den = den_h if h == 0 else jnp.where(mine, den_h, den)
            mx = m_h if h == 0 else jnp.where(mine, m_h, mx)
        r, m0 = divmod(n, sub_blocks)
        dst = pl.ds(r + dilation * m0 * ATT_BLOCK, ATT_BLOCK, stride=dilation) if dilation > 1 else rows
        out = num / den
        lse = mx + jnp.log(den)
        for t in lane_tiles:
            out_s[p, t, dst, :] = out[:, t * LANES:(t + 1) * LANES]
            lse_s[p, t, dst, :] = lse[:, t * LANES:(t + 1) * LANES]

    for p in range(n_pat):
        for n in range(nblk):
            block(p, n)

    chunk = 2 * ATT_BLOCK
    for c0 in range(0, SEQ, chunk):
        part = slice(c0, c0 + chunk)
        for t in lane_tiles:
            ls = [lse_s[p, t, part, :] for p in range(n_pat)]
            top = functools.reduce(jnp.maximum, ls)
            es = [jnp.exp(l - top) for l in ls]
            mixed = sum(e * out_s[p, t, part, :] for p, e in enumerate(es)) / sum(es)
            o_ref[part, t * LANES:(t + 1) * LANES] = mixed.astype(o_ref.dtype)


def _attention(qkvs, rel_bias, side_casts=()):
    n_pat = len(DILATED_PATTERNS)
    qkv_spec = pl.BlockSpec((None, SEQ, QKV_COLS), lambda b: (b, 0, 0))
    cast_in, cast_out, cast_shapes = _side_cast_io(side_casts, BATCH, lambda b: b)
    tables = jnp.stack([_bucket_table(window, dilation) for window, dilation in DILATED_PATTERNS])
    att, *casts = pl.pallas_call(
        _attn_body,
        grid=(BATCH,),
        in_specs=[pl.BlockSpec(memory_space=pltpu.SMEM),
                  pl.BlockSpec(tables.shape, lambda b: (0, 0, 0))] + [qkv_spec] * n_pat + cast_in,
        out_specs=[pl.BlockSpec((None, SEQ, ATT_WIDTH), lambda b: (b, 0, 0))] + cast_out,
        out_shape=[jax.ShapeDtypeStruct((BATCH, SEQ, ATT_WIDTH), BF16)] + cast_shapes,
        scratch_shapes=[pltpu.VMEM((n_pat, 3, ATT_HEADS, ATT_BLOCK, 2 * ATT_BLOCK), F32),
                        pltpu.VMEM((n_pat, ATT_WIDTH // LANES, SEQ, LANES), F32),
                        pltpu.VMEM((n_pat, ATT_WIDTH // LANES, SEQ, LANES), F32)],
        compiler_params=_params(("arbitrary",)),
        name="attention",
    )(rel_bias, tables, *[ua.reshape(BATCH, SEQ, QKV_COLS) for ua in qkvs], *[w for w, _ in side_casts])
    return att.reshape(TOKENS, ATT_WIDTH), casts


def _short_conv(cin_ref, cb_ref, cc_ref, w_ref, o_ref, tail_ref):
    uu = cc_ref[...] * cin_ref[...]
    t = lax.broadcasted_iota(jnp.int32, uu.shape, 0)
    y = uu * w_ref[CONV_WIDTH - 1:CONV_WIDTH, :]
    for shift in range(1, CONV_WIDTH):
        prev = pltpu.roll(uu, shift, axis=0)
        for r in range(shift):
            prev = jnp.where(t == r, tail_ref[8 - shift + r:8 - shift + r + 1, :], prev)
        y = y + prev * w_ref[CONV_WIDTH - 1 - shift:CONV_WIDTH - shift, :]
    o_ref[...] = (cb_ref[...] * y).astype(o_ref.dtype)
    tail_ref[...] = uu[uu.shape[0] - 8:, :]


def _gla_body(u_ref, wg_ref, bg_ref, gn_ref, ctril_ref, cw_ref, *rest):
    cols = lambda col, width: u_ref.at[:, col:col + width]
    q_ref, k_ref = cols(COL_GQ, GLA_QK), cols(COL_GK, GLA_QK)
    v_ref, gr_ref, glr_ref = cols(COL_GV, GLA_V), cols(COL_GR, GLA_V), cols(COL_GLR, LANES)
    cin_ref, cb_ref, cc_ref = cols(COL_CIN, CONV_CH), cols(COL_CB, CONV_CH), cols(COL_CC, CONV_CH)
    _gla_tile(q_ref, k_ref, v_ref, gr_ref, glr_ref, wg_ref, bg_ref, gn_ref, ctril_ref,
              cin_ref, cb_ref, cc_ref, cw_ref, *rest)


def _gla_tile(q_ref, k_ref, v_ref, gr_ref, glr_ref, wg_ref, bg_ref, gn_ref, ctril_ref,
              cin_ref, cb_ref, cc_ref, cw_ref, *rest):
    n_cast = (len(rest) - 4) // 2
    o_ref, cv_ref = rest[n_cast:n_cast + 2]
    s_ref, tail_ref = rest[-2:]
    _side_cast(rest[:n_cast], rest[n_cast + 2:-2])

    @pl.when(pl.program_id(1) == 0)
    def _():
        s_ref[...] = jnp.zeros_like(s_ref)
        tail_ref[...] = jnp.zeros_like(tail_ref)

    _short_conv(cin_ref, cb_ref, cc_ref, cw_ref, cv_ref, tail_ref)

    C = GLA_CHUNK
    row = lax.broadcasted_iota(jnp.int32, (C, C), 0)
    col = lax.broadcasted_iota(jnp.int32, (C, C), 1)
    tril = row >= col
    n_chunks = TC_GLA // C

    xg = _dot3(glr_ref[...], wg_ref[...]) + bg_ref[...]
    la_all = (jnp.minimum(xg, 0.0) - jnp.log(1.0 + jnp.exp(-jnp.abs(xg)))) * (1.0 / 16.0)

    la_hi, la_lo = _split_bf16(la_all)
    cum_all = _dot(ctril_ref[...], la_hi) + _dot(ctril_ref[...], la_lo)
    totals = jnp.concatenate([cum_all[(c + 1) * C - 1:(c + 1) * C, :] for c in range(n_chunks)]
                             + [jnp.zeros((LANES - n_chunks, GLA_QK), F32)], axis=0)
    decay_cols = jnp.exp(totals.T)

    for c in range(n_chunks):
        rows = slice(c * C, (c + 1) * C)
        cum = cum_all[rows]
        last = cum[C - 1:C, :]
        q = q_ref[rows, :]
        k = k_ref[rows, :]
        qs = q * (GLA_DK ** -0.5)
        qt = (qs * jnp.exp(cum)).astype(BF16)
        mid = cum[C // 2 - 1:C // 2, :]
        qm = (qs * jnp.exp(cum - mid)).astype(BF16)
        kt = (k * jnp.exp(mid - cum)).astype(BF16)
        kl_t = (k * jnp.exp(last - cum)).T.astype(BF16)
        for h in range(GLA_HEADS):
            sl = slice(h * GLA_DK, (h + 1) * GLA_DK)
            vs = slice(h * GLA_DV, (h + 1) * GLA_DV)
            vh = v_ref[rows, vs].astype(BF16)
            state = s_ref[h]
            st_hi, st_lo = _split_bf16(state)
            sc = jnp.where(tril, _dot_nt(qm[:, sl], kt[:, sl]), 0.0).astype(BF16)
            o = _dot(qt[:, sl], st_hi) + _dot(qt[:, sl], st_lo) + _dot(sc, vh)
            decay = jnp.broadcast_to(decay_cols[sl, c:c + 1], state.shape)
            s_ref[h] = decay * state + _dot(kl_t[sl, :], vh)
            g = gr_ref[rows, vs]
            o_ref[rows, vs] = (_rms(o, gn_ref[...]) * (g * _sigmoid(g))).astype(o_ref.dtype)


def _gla_conv(u, wg, bg, gn, conv_w, side_casts=()):
    nj = SEQ // TC_GLA
    row = lambda b, j: b * nj + j
    full = lambda a: pl.BlockSpec(a.shape, lambda b, j: (0, 0))
    cast_in, cast_out, cast_shapes = _side_cast_io(side_casts, BATCH * nj, row)
    t = np.arange(TC_GLA)
    same_chunk = (t[:, None] // GLA_CHUNK) == (t[None, :] // GLA_CHUNK)
    chunk_tril = jnp.asarray(same_chunk & (t[:, None] >= t[None, :]), BF16)
    go, cv, *casts = pl.pallas_call(
        _gla_body,
        grid=(BATCH, nj),
        in_specs=[pl.BlockSpec((TC_GLA, U_COLS), lambda b, j: (row(b, j), 0)),
                  full(wg), full(bg), full(gn), full(chunk_tril), full(conv_w)] + cast_in,
        out_specs=[pl.BlockSpec((TC_GLA, GLA_V), lambda b, j: (row(b, j), 0)),
                   pl.BlockSpec((TC_GLA, CONV_CH), lambda b, j: (row(b, j), 0))] + cast_out,
        out_shape=[jax.ShapeDtypeStruct((TOKENS, GLA_V), BF16), jax.ShapeDtypeStruct((TOKENS, CONV_CH), BF16)]
                  + cast_shapes,
        scratch_shapes=[pltpu.VMEM((GLA_HEADS, GLA_DK, GLA_DV), F32), pltpu.VMEM((8, CONV_CH), F32)],
        compiler_params=_params(("arbitrary", "arbitrary")),
        name="gla_conv",
    )(u, wg, bg, gn, chunk_tril, conv_w, *[w for w, _ in side_casts])
    return go, cv, casts


def _outproj_body(*refs, route):
    (att_ref, go_ref, cv_ref, x_ref, w_ref, g_ref) = refs[:6]
    if route:
        wr_ref, xo_ref, ho_ref, route_ref, counts_ref, carry_ref = refs[6:]
    else:
        xo_ref, ho_ref = refs[6:]
    y = (x_ref[...]
         + _dot(att_ref[...], w_ref[0:ATT_WIDTH, :])
         + _dot(go_ref[...], w_ref[ATT_WIDTH:ATT_WIDTH + GLA_V, :])
         + _dot(cv_ref[...], w_ref[ATT_WIDTH + GLA_V:MIX_WIDTH, :]))
    xo_ref[...] = y
    hf = _rms(y, g_ref[...])
    ho_ref[...] = hf.astype(ho_ref.dtype)
    if route:
        @pl.when(pl.program_id(0) == 0)
        def _():
            carry_ref[...] = jnp.zeros_like(carry_ref)

        tm = hf.shape[0]
        ne = N_EXPERTS
        hf_hi, hf_lo = _split_bf16(hf)
        part = _dot_nt(wr_ref[0], hf_hi) + _dot_nt(wr_ref[1], hf_lo)
        logits = part[0:ne] + part[ne:2 * ne]
        eidx = lax.broadcasted_iota(jnp.int32, logits.shape, 0).astype(F32)
        v1 = jnp.max(logits, axis=0, keepdims=True)
        i1 = jnp.min(jnp.where(logits == v1, eidx, float(ne)), axis=0, keepdims=True)
        lg2 = jnp.where(eidx == i1, -jnp.inf, logits)
        v2 = jnp.max(lg2, axis=0, keepdims=True)
        i2 = jnp.min(jnp.where(lg2 == v2, eidx, float(ne)), axis=0, keepdims=True)
        e2 = jnp.exp(v2 - v1)
        w1 = 1.0 / (1.0 + e2)
        w2 = e2 * w1
        sel1 = eidx == i1
        sel2 = eidx == i2
        onehot = jnp.where(sel1, 1.0, jnp.where(sel2, 1.0, 0.0))
        tri = (lax.broadcasted_iota(jnp.int32, (tm, tm), 0) <= lax.broadcasted_iota(jnp.int32, (tm, tm), 1))
        onehot16 = jnp.concatenate([onehot, jnp.zeros_like(onehot)], axis=0).astype(BF16)
        csum = _dot(onehot16, jnp.where(tri, 1.0, 0.0).astype(BF16))[0:ne]
        carry = carry_ref[:, 0:1]
        rank = csum - onehot + carry
        r1 = jnp.sum(jnp.where(sel1, rank, 0.0), axis=0, keepdims=True)
        r2 = jnp.sum(jnp.where(sel2, rank, 0.0), axis=0, keepdims=True)
        total = jnp.broadcast_to(carry + csum[:, tm - 1:tm], carry_ref.shape)
        carry_ref[...] = total
        counts_ref[...] = total
        rows = {ROUTE_I1: i1, ROUTE_I2: i2, ROUTE_W1: w1, ROUTE_W2: w2, ROUTE_R1: r1, ROUTE_R2: r2}
        zero = jnp.zeros_like(i1)
        route_ref[...] = jnp.concatenate([rows.get(r, zero) for r in range(8)], axis=0)


def _outproj(att, go, cv, x, w, g, w_router=None):
    route = w_router is not None
    tm = TM_PROJ
    tile = lambda cols: pl.BlockSpec((tm, cols), lambda i: (i, 0))
    full = lambda a: pl.BlockSpec(a.shape, lambda i: (0, 0))
    args = [att, go, cv, x, w, g]
    in_specs = [tile(ATT_WIDTH), tile(GLA_V), tile(CONV_CH), tile(D_MODEL), full(w), full(g)]
    out_specs = [tile(D_MODEL), tile(D_MODEL)]
    out_shape = [jax.ShapeDtypeStruct((TOKENS, D_MODEL), F32),
                 jax.ShapeDtypeStruct((TOKENS, D_MODEL), F32 if route else BF16)]
    scratch = []
    if route:
        args.append(w_router)
        in_specs.append(pl.BlockSpec(w_router.shape, lambda i: (0, 0, 0)))
        out_specs += [pl.BlockSpec((8, tm), lambda i: (0, i)), pl.BlockSpec((N_EXPERTS, LANES), lambda i: (0, 0))]
        out_shape += [jax.ShapeDtypeStruct((8, TOKENS), F32), jax.ShapeDtypeStruct((N_EXPERTS, LANES), F32)]
        scratch = [pltpu.VMEM((N_EXPERTS, LANES), F32)]
    return pl.pallas_call(
        functools.partial(_outproj_body, route=route),
        grid=(TOKENS // tm,),
        in_specs=in_specs,
        out_specs=out_specs,
        out_shape=out_shape,
        scratch_shapes=scratch,
        compiler_params=_params(("arbitrary",) if route else ("parallel",)),
        name="outproj_route" if route else "outproj",
    )(*args)


def _swiglu_accumulate(h, w1_ref, w3_ref, w2_ref, acc_ref, tf):
    for c0 in range(0, tf, FFN_SUB):
        c1 = min(c0 + FFN_SUB, tf)
        a = _dot(h, w1_ref[:, c0:c1])
        b = _dot(h, w3_ref[:, c0:c1])
        act = a * _sigmoid(a) * b
        acc_ref[...] += _dot(act.astype(BF16), w2_ref[c0:c1, :])


def _ffn_body(x_ref, h_ref, w1_ref, w3_ref, w2_ref, *rest, tf):
    n_cast = (len(rest) - 1) // 2
    o_ref = rest[n_cast]

    @pl.when(pl.program_id(1) == 0)
    def _():
        o_ref[...] = x_ref[...]

    _swiglu_accumulate(h_ref[...], w1_ref, w3_ref, w2_ref, o_ref, tf)
    _side_cast(rest[:n_cast], rest[n_cast + 1:])


def _ffn(x, h, w1, w3, w2, *, side_casts=()):
    nj, _, tf = w1.shape
    tm = TM_FFN
    tile = lambda cols: pl.BlockSpec((tm, cols), lambda i, j: (i, 0))
    cast_in, cast_out, cast_shapes = _side_cast_io(side_casts, (TOKENS // tm) * nj, lambda i, j: i * nj + j)
    resident = dict(pipeline_mode=pl.Buffered(1)) if nj == 1 else {}
    res = pl.pallas_call(
        functools.partial(_ffn_body, tf=tf),
        grid=(TOKENS // tm, nj),
        in_specs=[tile(D_MODEL), tile(D_MODEL),
                  pl.BlockSpec((None, D_MODEL, tf), lambda i, j: (j, 0, 0), **resident),
                  pl.BlockSpec((None, D_MODEL, tf), lambda i, j: (j, 0, 0), **resident),
                  pl.BlockSpec((tf, D_MODEL), lambda i, j: (j, 0), **resident)] + cast_in,
        out_specs=[tile(D_MODEL)] + cast_out,
        out_shape=[jax.ShapeDtypeStruct((TOKENS, D_MODEL), F32)] + cast_shapes,
        compiler_params=_params(("parallel", "arbitrary")),
        name="dense_ffn",
    )(x, h, w1, w3, w2, *[w for w, _ in side_casts])
    return res[0], res[1:]


ROW_TILE = D_MODEL // LANES


def _to_row_tiled(dst_ref, lead, val):
    rows = val.shape[0]
    for s in range(ROW_TILE):
        dst_ref[(*lead, pl.ds(s, rows, stride=ROW_TILE), slice(None))] = val[:, s * LANES:(s + 1) * LANES]


def _from_row_tiled(src_ref, lead, rows):
    return jnp.concatenate([src_ref[(*lead, pl.ds(s, rows, stride=ROW_TILE), slice(None))]
                            for s in range(ROW_TILE)], axis=-1)


def _row_tile(idx):
    return pl.ds(pl.multiple_of(idx * ROW_TILE, ROW_TILE), ROW_TILE)


def _dispatch_body(pos_ref, last_tile_ref, h_ref, xs_hbm, stage_ref, zero_ref, sem, zero_sem):
    i = pl.program_id(0)
    n = pl.num_programs(0)
    tm = DISPATCH_CHUNK
    slot = i % 2

    def drain(s):
        for _ in range(2):
            pltpu.make_async_copy(stage_ref.at[s], xs_hbm.at[pl.ds(0, tm * ROW_TILE)], sem.at[s]).wait()

    @pl.when(i == 0)
    def _():
        zero_ref[...] = jnp.zeros_like(zero_ref)

        def zero_copy(e):
            start = pl.multiple_of(last_tile_ref[e] * (TM_MOE * ROW_TILE), TM_MOE * ROW_TILE)
            return pltpu.make_async_copy(zero_ref, xs_hbm.at[pl.ds(start, TM_MOE * ROW_TILE)], zero_sem)

        for e in range(2 * N_EXPERTS):
            @pl.when(last_tile_ref[e] >= 0)
            def _():
                zero_copy(e).start()
        for e in range(2 * N_EXPERTS):
            @pl.when(last_tile_ref[e] >= 0)
            def _():
                zero_copy(e).wait()

    @pl.when(i >= 2)
    def _():
        drain(slot)

    _to_row_tiled(stage_ref, (slot,), h_ref[...])

    def body(t, carry):
        for k in range(2):
            dst = pos_ref[2 * (i * tm + t) + k]
            pltpu.make_async_copy(stage_ref.at[slot, _row_tile(t)], xs_hbm.at[_row_tile(dst)],
                                  sem.at[slot]).start(priority=k)
        return carry
    lax.fori_loop(0, tm, body, 0, unroll=8)

    @pl.when(i == n - 1)
    def _():
        drain(1 - slot)
        drain(slot)


def _dispatch(pos, last_tile, h):
    tm = DISPATCH_CHUNK
    return pl.pallas_call(
        _dispatch_body,
        grid=(TOKENS // tm,),
        in_specs=[pl.BlockSpec(memory_space=pltpu.SMEM),
                  pl.BlockSpec(memory_space=pltpu.SMEM),
                  pl.BlockSpec((tm, D_MODEL), lambda i: (i, 0))],
        out_specs=pl.BlockSpec(memory_space=pl.ANY),
        out_shape=jax.ShapeDtypeStruct((N_SORTED * ROW_TILE, LANES), F32),
        scratch_shapes=[pltpu.VMEM((2, tm * ROW_TILE, LANES), F32),
                        pltpu.VMEM((TM_MOE * ROW_TILE, LANES), F32),
                        pltpu.SemaphoreType.DMA((2,)), pltpu.SemaphoreType.DMA(())],
        compiler_params=_params(("arbitrary",)),
        name="moe_dispatch",
    )(pos, last_tile, h)


def _gffn_body(te_ref, nu_ref, xs_ref, w1_ref, w3_ref, w2_ref, o_ref, hb_ref, acc_ref):
    del te_ref
    i = pl.program_id(0)
    j = pl.program_id(1)

    used = i < nu_ref[0]

    @pl.when(j == 0)
    def _():
        acc_ref[...] = jnp.zeros_like(acc_ref)

    @pl.when(used & (j == 0))
    def _():
        hb_ref[...] = _from_row_tiled(xs_ref, (), TM_MOE).astype(BF16)

    @pl.when(used)
    def _():
        _swiglu_accumulate(hb_ref[...], w1_ref, w3_ref, w2_ref, acc_ref, TF_MOE)

    @pl.when(j == pl.num_programs(1) - 1)
    def _():
        _to_row_tiled(o_ref, (), acc_ref[...])


def _grouped_ffn(tile_expert, n_used, xs, w1, w3, w2):
    nj = FFN_EXPERT // TF_MOE
    col = lambda i, j, nu: jnp.where(i < nu[0], j, nj - 1)
    grid_spec = pltpu.PrefetchScalarGridSpec(
        num_scalar_prefetch=2,
        grid=(N_TILES_MOE, nj),
        in_specs=[
            pl.BlockSpec((TM_MOE * ROW_TILE, LANES), lambda i, j, te, nu: (jnp.minimum(i, nu[0] - 1), 0)),
            pl.BlockSpec((None, None, D_MODEL, TF_MOE), lambda i, j, te, nu: (te[i], col(i, j, nu), 0, 0)),
            pl.BlockSpec((None, None, D_MODEL, TF_MOE), lambda i, j, te, nu: (te[i], col(i, j, nu), 0, 0)),
            pl.BlockSpec((None, TF_MOE, D_MODEL), lambda i, j, te, nu: (te[i], col(i, j, nu), 0)),
        ],
        out_specs=pl.BlockSpec((TM_MOE * ROW_TILE, LANES), lambda i, j, te, nu: (i, 0)),
        scratch_shapes=[pltpu.VMEM((TM_MOE, D_MODEL), BF16), pltpu.VMEM((TM_MOE, D_MODEL), F32)],
    )
    return pl.pallas_call(
        _gffn_body,
        grid_spec=grid_spec,
        out_shape=jax.ShapeDtypeStruct((N_SORTED * ROW_TILE, LANES), F32),
        compiler_params=_params(("arbitrary", "arbitrary")),
        name="moe_ffn",
    )(tile_expert, n_used, xs, w1, w3, w2)


def _combine_body(pos_ref, x_ref, gate_ref, g_ref, ys_hbm, o_ref, buf_ref, sem):
    i = pl.program_id(0)
    n = pl.num_programs(0)
    tm = TM_COMBINE

    def issue(tile, slot):
        def body(t, carry):
            for k in range(2):
                src = pos_ref[2 * (tile * tm + t) + k]
                pltpu.make_async_copy(ys_hbm.at[_row_tile(src)], buf_ref.at[slot, k, _row_tile(t)],
                                      sem.at[slot]).start(priority=k)
            return carry
        lax.fori_loop(0, tm, body, 0, unroll=8)

    @pl.when(i == 0)
    def _():
        issue(0, 0)

    @pl.when(i + 1 < n)
    def _():
        issue(i + 1, (i + 1) % 2)

    slot = i % 2
    for k in range(2):
        pltpu.make_async_copy(ys_hbm.at[pl.ds(0, tm * ROW_TILE)], buf_ref.at[slot, k], sem.at[slot]).wait()
    w1 = gate_ref[:, 0:1]
    w2 = gate_ref[:, 1:2]
    y = x_ref[...] + w1 * _from_row_tiled(buf_ref, (slot, 0), tm) + w2 * _from_row_tiled(buf_ref, (slot, 1), tm)
    o_ref[...] = _rms(y, g_ref[...])


def _combine(pos, x, gates, g, ys):
    tm = TM_COMBINE
    return pl.pallas_call(
        _combine_body,
        grid=(TOKENS // tm,),
        in_specs=[pl.BlockSpec(memory_space=pltpu.SMEM),
                  pl.BlockSpec((tm, D_MODEL), lambda i: (i, 0)),
                  pl.BlockSpec((tm, 2), lambda i: (i, 0)),
                  pl.BlockSpec((1, D_MODEL), lambda i: (0, 0)),
                  pl.BlockSpec(memory_space=pl.ANY)],
        out_specs=pl.BlockSpec((tm, D_MODEL), lambda i: (i, 0)),
        out_shape=jax.ShapeDtypeStruct((TOKENS, D_MODEL), F32),
        scratch_shapes=[pltpu.VMEM((2, 2, tm * ROW_TILE, LANES), F32), pltpu.SemaphoreType.DMA((2,))],
        compiler_params=_params(("arbitrary",)),
        name="moe_combine",
    )(pos, x, gates, g, ys)


def _routing_tables(route, counts):
    cnt = counts[:, 0].astype(jnp.int32)
    tiles = (cnt + TM_MOE - 1) // TM_MOE
    tile_end = jnp.cumsum(tiles)
    tile_start = tile_end - tiles
    n_used = tile_end[-1]
    expert = route[ROUTE_I1:ROUTE_I2 + 1].astype(jnp.int32)
    rank = route[ROUTE_R1:ROUTE_R2 + 1].astype(jnp.int32)
    group_start = jnp.sum(jnp.where(expert[..., None] == jnp.arange(N_EXPERTS), tile_start * TM_MOE, 0), axis=-1)
    pos = (group_start + rank).T
    tile_id = jnp.minimum(jnp.arange(N_TILES_MOE, dtype=jnp.int32), n_used - 1)
    tile_expert = jnp.sum(tile_id[:, None] >= tile_end[None, :], axis=1).astype(jnp.int32)
    last_tile = jnp.where(tiles > 0, tile_end - 1, -1)
    spare = n_used + jnp.arange(N_EXPERTS)
    zero_tiles = jnp.concatenate([last_tile, jnp.where(spare < N_TILES_MOE, spare, -1)]).astype(jnp.int32)
    return pos.reshape(2 * TOKENS), tile_expert, n_used.reshape(1), zero_tiles


def _moe(x, h, route, counts, w1, w3, w2, g_final):
    pos, tile_expert, n_used, last_tile = _routing_tables(route, counts)
    xs = _dispatch(pos, last_tile, h)
    ys = _grouped_ffn(tile_expert, n_used, xs, w1, w3, w2)
    return _combine(pos, x, route[ROUTE_W1:ROUTE_W2 + 1].T, g_final, ys)


def _prep_w_in(w):
    aq, ak, av, gq, gk, gv, gr, glr, c_in, c_b, c_c = jnp.split(w, np.cumsum(SPLIT_SIZES)[:-1].tolist(), axis=2)
    pad = jnp.zeros(w.shape[:2] + (LANES - GLA_RANK,), w.dtype)
    return jnp.concatenate([aq, ak, av, gv, gr, gq, gk, c_in, c_b, c_c, glr, pad], axis=2).astype(BF16)


def _prep_router(w):
    wt = w.T
    hi = wt.astype(BF16)
    lo = (wt - hi.astype(F32)).astype(BF16)
    return jnp.stack([jnp.concatenate([hi, lo]), jnp.concatenate([hi, jnp.zeros_like(hi)])])


def kernel(x, w_mix_in, w_mix_out, g_mix, rel_bias, gla_w_gate, gla_b_gate, gla_g_norm, conv_w,
           g_ffn, ffn_w1, ffn_w3, ffn_w2, moe_router, moe_w1, moe_w3, moe_w2, g_final):
    assert DEPTH == 2
    x = x.reshape(TOKENS, D_MODEL)
    up_job = lambda w, tf: (w, w.shape[-1] // tf)
    down = lambda c: c.reshape(c.shape[0], c.shape[2], c.shape[3])
    gla_jobs = {0: up_job(moe_w1[0], TF_MOE), 1: up_job(moe_w3[0], TF_MOE)}
    experts = {}
    w_in = _prep_w_in(w_mix_in)
    for layer in range(DEPTH):
        u, *qkvs = _inproj(x, g_mix[layer].reshape(1, D_MODEL), w_in, layer)
        dense_jobs = [up_job(ffn_w1[:1], TF_DENSE), up_job(ffn_w3[:1], TF_DENSE), (ffn_w2[:1], 1)] if layer == 0 else []
        att, dense_bf16 = _attention(qkvs, rel_bias, dense_jobs)
        if layer == 0:
            ffn_bf16 = (dense_bf16[0][0], dense_bf16[1][0], down(dense_bf16[2])[0])
        wg = jnp.pad(gla_w_gate[layer], ((0, LANES - GLA_RANK), (0, 0)))
        go, cv, (experts[layer],) = _gla_conv(u, wg, gla_b_gate[layer].reshape(1, GLA_QK),
                                              gla_g_norm[layer].reshape(1, GLA_DV),
                                              jnp.pad(conv_w[layer], ((0, 8 - CONV_WIDTH), (0, 0))),
                                              side_casts=[gla_jobs[layer]])
        w_out = w_mix_out[layer].astype(BF16)
        g2 = g_ffn[layer].reshape(1, D_MODEL)
        i = layer // 2
        if layer % 2 == 0:
            x, h = _outproj(att, go, cv, x, w_out, g2)
            x, (cast,) = _ffn(x, h, *ffn_bf16, side_casts=[(moe_w2[i], 1)])
            experts["w2"] = down(cast)
        else:
            x, h, route, counts = _outproj(att, go, cv, x, w_out, g2, _prep_router(moe_router[i]))
            x = _moe(x, h, route, counts, experts[0], experts[1], experts["w2"], g_final.reshape(1, D_MODEL))
    return x.reshape(BATCH, SEQ, D_MODEL)
```

```python
import functools
import math

import jax
import jax.numpy as jnp
import numpy as np
from jax import lax
from jax.experimental import pallas as pl
from jax.experimental.pallas import tpu as pltpu

F32 = jnp.float32
BF16 = jnp.bfloat16

D_MODEL = 1024
BATCH = 8
SEQ = 2048
TOKENS = BATCH * SEQ
DEPTH = 2
EPS = 1e-6

HEAD_DIM = 64
ATT_HEADS = 4
ATT_WIDTH = ATT_HEADS * HEAD_DIM
DILATED_PATTERNS = ((128, 1), (512, 4), (2048, 16))
ATT_BLOCK = 128
REL_BUCKETS = 32
REL_MAX_DISTANCE = 2048

GLA_HEADS = 4
GLA_DK = 64
GLA_DV = 128
GLA_RANK = 16
GLA_CHUNK = 64
GLA_QK = GLA_HEADS * GLA_DK
GLA_V = GLA_HEADS * GLA_DV

CONV_CH = 256
CONV_WIDTH = 3
MIX_WIDTH = ATT_WIDTH + GLA_V + CONV_CH

SPLIT_SIZES = (ATT_WIDTH, ATT_WIDTH, ATT_WIDTH, GLA_QK, GLA_QK, GLA_V, GLA_V, GLA_RANK,
               CONV_CH, CONV_CH, CONV_CH)

FFN_DENSE = 2816
N_EXPERTS = 8
FFN_EXPERT = 3584

LANES = 128
MXU_WIDTH = 256
VMEM_LIMIT = 56 * 1024 * 1024

QKV_COLS = 3 * ATT_WIDTH
COL_GQ, COL_GK, COL_GV, COL_GR = 0, 256, 512, 1024
COL_CIN, COL_CB, COL_CC, COL_GLR = 1536, 1792, 2048, 2304
U_COLS = COL_GLR + LANES
W_TAIL = COL_CIN

NEG_BIG = -1e30

ATT_UNROLL = 15
TM_PROJ = 512
TC_GLA = 512
TM_FFN = 512
TF_DENSE = 2816
FFN_SUB = 256
TM_MOE = 512
TF_MOE = 1792
N_TILES_MOE = 2 * TOKENS // TM_MOE + N_EXPERTS
N_SORTED = N_TILES_MOE * TM_MOE
DISPATCH_CHUNK = 512
TM_COMBINE = 512

ROUTE_I1, ROUTE_I2, ROUTE_W1, ROUTE_W2, ROUTE_R1, ROUTE_R2 = range(6)


def _params(sem):
    return pltpu.CompilerParams(dimension_semantics=sem, vmem_limit_bytes=VMEM_LIMIT)


def _split_bf16(a):
    hi = a.astype(BF16)
    lo = (a - hi.astype(F32)).astype(BF16)
    return hi, lo


def _dot(a, b):
    return jnp.dot(a, b, preferred_element_type=F32)


def _dot3(a, b):
    a_hi, a_lo = _split_bf16(a)
    b_hi, b_lo = _split_bf16(b)
    return _dot(a_hi, b_hi) + _dot(a_lo, b_hi) + _dot(a_hi, b_lo)


def _dot_nt(a, b):
    return lax.dot_general(a, b, (((1,), (1,)), ((), ())), preferred_element_type=F32)


def _rms(x, g):
    ms = jnp.mean(x * x, axis=-1, keepdims=True)
    return x * lax.rsqrt(ms + EPS) * g


def _sigmoid(x):
    return 1.0 / (1.0 + jnp.exp(-x))


def _side_cast(srcs, dsts):
    for src, dst in zip(srcs, dsts):
        width = dst.shape[-1]
        for s in range(dst.shape[0]):
            dst[s] = src[:, s * width:(s + 1) * width].astype(dst.dtype)


def _side_cast_io(jobs, steps, step_of):
    in_specs, out_specs, out_shapes = [], [], []
    for w, splits in jobs:
        g, r, c = w.shape
        rb = g * r // steps
        per_group = r // rb
        in_specs.append(pl.BlockSpec(
            (None, rb, c), lambda *ids, pg=per_group: (step_of(*ids) // pg, step_of(*ids) % pg, 0)))
        out_specs.append(pl.BlockSpec(
            (None, splits, rb, c // splits),
            lambda *ids, pg=per_group: (step_of(*ids) // pg, 0, step_of(*ids) % pg, 0)))
        out_shapes.append(jax.ShapeDtypeStruct((g, splits, r, c // splits), BF16))
    return in_specs, out_specs, out_shapes


def _inproj_body(x_ref, g_ref, w_ref, o_ref, *rest):
    qkv_refs, qkv_f32 = rest[:-1], rest[-1]
    h = _rms(x_ref[...], g_ref[...]).astype(BF16)
    for c0 in range(0, QKV_COLS, MXU_WIDTH):
        res = _dot(h, w_ref[:, c0:c0 + MXU_WIDTH])
        for t in range(MXU_WIDTH // LANES):
            qkv_f32[c0 // LANES + t] = res[:, t * LANES:(t + 1) * LANES]
    for (_, dilation), ref in zip(DILATED_PATTERNS, qkv_refs):
        for r in range(dilation):
            rows = pl.ds(r, TM_PROJ // dilation, stride=dilation)
            ref[r] = jnp.concatenate([qkv_f32[t, rows, :] for t in range(QKV_COLS // LANES)],
                                     axis=-1).astype(ref.dtype)
    for c0 in range(0, W_TAIL, MXU_WIDTH):
        o_ref[:, c0:c0 + MXU_WIDTH] = _dot(h, w_ref[:, QKV_COLS + c0:QKV_COLS + c0 + MXU_WIDTH])
    tail = _dot(h, w_ref[:, QKV_COLS + W_TAIL:])
    for i, col in enumerate((COL_CIN, COL_CB, COL_CC)):
        o_ref[:, col:col + CONV_CH] = tail[:, GLA_RANK + i * CONV_CH:GLA_RANK + (i + 1) * CONV_CH]
    o_ref[:, COL_GLR:COL_GLR + LANES] = tail[:, 0:LANES]


def _subseq_spec(dilation, cols):
    tiles = SEQ // TM_PROJ
    return pl.BlockSpec((None, dilation, TM_PROJ // dilation, cols), lambda i: (i // tiles, 0, i % tiles, 0))


def _inproj(x, g, w_all, layer):
    qkv_shapes = [jax.ShapeDtypeStruct((BATCH, d, SEQ // d, QKV_COLS), BF16) for _, d in DILATED_PATTERNS]
    return pl.pallas_call(
        _inproj_body,
        grid=(TOKENS // TM_PROJ,),
        in_specs=[
            pl.BlockSpec((TM_PROJ, D_MODEL), lambda i: (i, 0)),
            pl.BlockSpec((1, D_MODEL), lambda i: (0, 0)),
            pl.BlockSpec((None, D_MODEL, QKV_COLS + U_COLS), lambda i: (layer, 0, 0)),
        ],
        out_specs=[pl.BlockSpec((TM_PROJ, U_COLS), lambda i: (i, 0))]
                  + [_subseq_spec(d, QKV_COLS) for _, d in DILATED_PATTERNS],
        out_shape=[jax.ShapeDtypeStruct((TOKENS, U_COLS), F32)] + qkv_shapes,
        scratch_shapes=[pltpu.VMEM((QKV_COLS // LANES, TM_PROJ, LANES), F32)],
        compiler_params=_params(("parallel",)),
        name="inproj",
    )(x, g, w_all)


def _rel_bucket(dist):
    max_exact = REL_BUCKETS // 2
    d = jnp.maximum(dist, 0)
    log_ratio = jnp.log(jnp.maximum(d, 1).astype(F32) / max_exact) / math.log(REL_MAX_DISTANCE / max_exact)
    large = jnp.minimum(max_exact + (log_ratio * (REL_BUCKETS - max_exact)).astype(jnp.int32), REL_BUCKETS - 1)
    return jnp.where(d < max_exact, d, large)


def _bucket_table(window, dilation):
    span = window // dilation
    qi = jnp.arange(ATT_BLOCK)[:, None]
    kj = jnp.arange(2 * ATT_BLOCK)[None, :]
    sub_dist = qi - kj + ATT_BLOCK
    band = (sub_dist >= 0) & (sub_dist <= span)
    return jnp.where(band, _rel_bucket(sub_dist * dilation), -1).astype(jnp.int32)


def _attn_body(rb_ref, bidx_ref, qkv_ref, *rest, sub_blocks, unroll):
    n_cast = (len(rest) - 3) // 2
    o_ref, lse_ref = rest[n_cast:n_cast + 2]
    bias_ref = rest[-1]
    _side_cast(rest[:n_cast], rest[n_cast + 2:-1])
    nblk = SEQ // ATT_BLOCK

    @pl.when(pl.program_id(0) == 0)
    def _():
        bidx = bidx_ref[...]
        in_prev = lax.broadcasted_iota(jnp.int32, bidx.shape, 1) < ATT_BLOCK
        for h in range(ATT_HEADS):
            acc = jnp.full(bidx.shape, NEG_BIG, F32)
            for b in range(REL_BUCKETS):
                acc = jnp.where(bidx == b, rb_ref[b, h], acc)
            bias_ref[0, h] = acc
            bias_ref[1, h] = jnp.where(in_prev, NEG_BIG, acc)
            bias_ref[2, h] = jnp.concatenate([acc[:, ATT_BLOCK:], jnp.full_like(acc[:, ATT_BLOCK:], NEG_BIG)], axis=1)

    def block(n, first):
        if first:
            rows, krows, variant = slice(0, ATT_BLOCK), slice(0, 2 * ATT_BLOCK), 2
        else:
            r0 = pl.multiple_of(n * ATT_BLOCK, ATT_BLOCK)
            rows = pl.ds(r0, ATT_BLOCK)
            krows = pl.ds(r0 - ATT_BLOCK, 2 * ATT_BLOCK)
            if sub_blocks == nblk:
                variant = 0
            elif sub_blocks == 1:
                variant = 1
            else:
                variant = jnp.where(n % sub_blocks == 0, 1, 0)
        q = qkv_ref[rows, 0:ATT_WIDTH]
        kk = qkv_ref[krows, ATT_WIDTH:2 * ATT_WIDTH]
        vv = qkv_ref[krows, 2 * ATT_WIDTH:3 * ATT_WIDTH]
        q = q * jnp.asarray(HEAD_DIM ** -0.5, BF16)
        head_of_lane = lax.broadcasted_iota(jnp.int32, (ATT_BLOCK, ATT_WIDTH), 1) // HEAD_DIM
        ones = jnp.ones((kk.shape[0], LANES), BF16)
        num = den = mx = None
        for h in range(ATT_HEADS):
            mine = head_of_lane == h
            bias = bias_ref[variant, h]
            s = _dot_nt(jnp.where(mine, q, jnp.zeros_like(q)), kk) + bias
            m = jnp.max(s, axis=-1, keepdims=True)
            p = jnp.exp(s - m).astype(BF16)
            num_h = _dot(p, vv)
            den_h = jnp.tile(_dot(p, ones), (1, ATT_WIDTH // LANES))
            m_h = jnp.broadcast_to(m, (ATT_BLOCK, ATT_WIDTH))
            num = num_h if h == 0 else jnp.where(mine, num_h, num)
            den = den_h if h == 0 else jnp.where(mine, den_h, den)
            mx = m_h if h == 0 else jnp.where(mine, m_h, mx)
        o_ref[rows, :] = (num / den).astype(o_ref.dtype)
        lse_ref[rows, :] = mx + jnp.log(den)

    block(0, True)

    def loop_body(n, carry):
        block(n, False)
        return carry
    lax.fori_loop(1, nblk, loop_body, 0, unroll=unroll)


def _attention_pattern(ua, rel_bias, window, dilation, side_casts=()):
    L = SEQ // dilation
    shape = (BATCH, dilation, L, ATT_WIDTH)
    qkv_spec = pl.BlockSpec((None, SEQ, QKV_COLS), lambda b: (b, 0, 0))
    out_spec = pl.BlockSpec((None, SEQ, ATT_WIDTH), lambda b: (b, 0, 0))
    cast_in, cast_out, cast_shapes = _side_cast_io(side_casts, BATCH, lambda b: b)
    o, lse, *casts = pl.pallas_call(
        functools.partial(_attn_body, sub_blocks=L // ATT_BLOCK, unroll=ATT_UNROLL),
        grid=(BATCH,),
        in_specs=[
            pl.BlockSpec(memory_space=pltpu.SMEM),
            pl.BlockSpec((ATT_BLOCK, 2 * ATT_BLOCK), lambda b: (0, 0)),
            qkv_spec,
        ] + cast_in,
        out_specs=[out_spec, out_spec] + cast_out,
        out_shape=[jax.ShapeDtypeStruct((BATCH, SEQ, ATT_WIDTH), BF16),
                   jax.ShapeDtypeStruct((BATCH, SEQ, ATT_WIDTH), F32)] + cast_shapes,
        scratch_shapes=[pltpu.VMEM((3, ATT_HEADS, ATT_BLOCK, 2 * ATT_BLOCK), F32)],
        compiler_params=_params(("arbitrary",)),
        name=f"attn_d{dilation}",
    )(rel_bias, _bucket_table(window, dilation), ua.reshape(BATCH, SEQ, QKV_COLS), *[w for w, _ in side_casts])
    return (o.reshape(shape), lse.reshape(shape)), casts


def _attention(qkvs, rel_bias, side_casts=()):
    side_casts = list(side_casts) + [None] * (len(DILATED_PATTERNS) - len(side_casts))
    res = [_attention_pattern(ua, rel_bias, window, dilation, [] if w is None else [w])
           for ua, (window, dilation), w in zip(qkvs, DILATED_PATTERNS, side_casts)]
    return [r[0] for r in res], [c for r in res for c in r[1]]


def _short_conv(cin_ref, cb_ref, cc_ref, w_ref, o_ref, tail_ref):
    uu = cc_ref[...] * cin_ref[...]
    t = lax.broadcasted_iota(jnp.int32, uu.shape, 0)
    y = uu * w_ref[CONV_WIDTH - 1:CONV_WIDTH, :]
    for shift in range(1, CONV_WIDTH):
        prev = pltpu.roll(uu, shift, axis=0)
        for r in range(shift):
            prev = jnp.where(t == r, tail_ref[8 - shift + r:8 - shift + r + 1, :], prev)
        y = y + prev * w_ref[CONV_WIDTH - 1 - shift:CONV_WIDTH - shift, :]
    o_ref[...] = (cb_ref[...] * y).astype(o_ref.dtype)
    tail_ref[...] = uu[uu.shape[0] - 8:, :]


def _gla_body(u_ref, wg_ref, bg_ref, gn_ref, ctril_ref, cw_ref, *rest):
    cols = lambda col, width: u_ref.at[:, col:col + width]
    q_ref, k_ref = cols(COL_GQ, GLA_QK), cols(COL_GK, GLA_QK)
    v_ref, gr_ref, glr_ref = cols(COL_GV, GLA_V), cols(COL_GR, GLA_V), cols(COL_GLR, LANES)
    cin_ref, cb_ref, cc_ref = cols(COL_CIN, CONV_CH), cols(COL_CB, CONV_CH), cols(COL_CC, CONV_CH)
    _gla_tile(q_ref, k_ref, v_ref, gr_ref, glr_ref, wg_ref, bg_ref, gn_ref, ctril_ref,
              cin_ref, cb_ref, cc_ref, cw_ref, *rest)


def _gla_tile(q_ref, k_ref, v_ref, gr_ref, glr_ref, wg_ref, bg_ref, gn_ref, ctril_ref,
              cin_ref, cb_ref, cc_ref, cw_ref, *rest):
    n_cast = (len(rest) - 4) // 2
    o_ref, cv_ref = rest[n_cast:n_cast + 2]
    s_ref, tail_ref = rest[-2:]
    _side_cast(rest[:n_cast], rest[n_cast + 2:-2])

    @pl.when(pl.program_id(1) == 0)
    def _():
        s_ref[...] = jnp.zeros_like(s_ref)
        tail_ref[...] = jnp.zeros_like(tail_ref)

    _short_conv(cin_ref, cb_ref, cc_ref, cw_ref, cv_ref, tail_ref)

    C = GLA_CHUNK
    row = lax.broadcasted_iota(jnp.int32, (C, C), 0)
    col = lax.broadcasted_iota(jnp.int32, (C, C), 1)
    tril = row >= col
    n_chunks = TC_GLA // C

    xg = _dot3(glr_ref[...], wg_ref[...]) + bg_ref[...]
    la_all = (jnp.minimum(xg, 0.0) - jnp.log(1.0 + jnp.exp(-jnp.abs(xg)))) * (1.0 / 16.0)

    la_hi, la_lo = _split_bf16(la_all)
    cum_all = _dot(ctril_ref[...], la_hi) + _dot(ctril_ref[...], la_lo)
    totals = jnp.concatenate([cum_all[(c + 1) * C - 1:(c + 1) * C, :] for c in range(n_chunks)]
                             + [jnp.zeros((LANES - n_chunks, GLA_QK), F32)], axis=0)
    decay_cols = jnp.exp(totals.T)

    for c in range(n_chunks):
        rows = slice(c * C, (c + 1) * C)
        cum = cum_all[rows]
        last = cum[C - 1:C, :]
        q = q_ref[rows, :]
        k = k_ref[rows, :]
        qs = q * (GLA_DK ** -0.5)
        qt = (qs * jnp.exp(cum)).astype(BF16)
        mid = cum[C // 2 - 1:C // 2, :]
        qm = (qs * jnp.exp(cum - mid)).astype(BF16)
        kt = (k * jnp.exp(mid - cum)).astype(BF16)
        kl_t = (k * jnp.exp(last - cum)).T.astype(BF16)
        for h in range(GLA_HEADS):
            sl = slice(h * GLA_DK, (h + 1) * GLA_DK)
            vs = slice(h * GLA_DV, (h + 1) * GLA_DV)
            vh = v_ref[rows, vs].astype(BF16)
            state = s_ref[h]
            st_hi, st_lo = _split_bf16(state)
            sc = jnp.where(tril, _dot_nt(qm[:, sl], kt[:, sl]), 0.0).astype(BF16)
            o = _dot(qt[:, sl], st_hi) + _dot(qt[:, sl], st_lo) + _dot(sc, vh)
            decay = jnp.broadcast_to(decay_cols[sl, c:c + 1], state.shape)
            s_ref[h] = decay * state + _dot(kl_t[sl, :], vh)
            g = gr_ref[rows, vs]
            o_ref[rows, vs] = (_rms(o, gn_ref[...]) * (g * _sigmoid(g))).astype(o_ref.dtype)


def _gla_conv(u, wg, bg, gn, conv_w, side_casts=()):
    nj = SEQ // TC_GLA
    row = lambda b, j: b * nj + j
    full = lambda a: pl.BlockSpec(a.shape, lambda b, j: (0, 0))
    cast_in, cast_out, cast_shapes = _side_cast_io(side_casts, BATCH * nj, row)
    t = np.arange(TC_GLA)
    same_chunk = (t[:, None] // GLA_CHUNK) == (t[None, :] // GLA_CHUNK)
    chunk_tril = jnp.asarray(same_chunk & (t[:, None] >= t[None, :]), BF16)
    go, cv, *casts = pl.pallas_call(
        _gla_body,
        grid=(BATCH, nj),
        in_specs=[pl.BlockSpec((TC_GLA, U_COLS), lambda b, j: (row(b, j), 0)),
                  full(wg), full(bg), full(gn), full(chunk_tril), full(conv_w)] + cast_in,
        out_specs=[pl.BlockSpec((TC_GLA, GLA_V), lambda b, j: (row(b, j), 0)),
                   pl.BlockSpec((TC_GLA, CONV_CH), lambda b, j: (row(b, j), 0))] + cast_out,
        out_shape=[jax.ShapeDtypeStruct((TOKENS, GLA_V), BF16), jax.ShapeDtypeStruct((TOKENS, CONV_CH), BF16)]
                  + cast_shapes,
        scratch_shapes=[pltpu.VMEM((GLA_HEADS, GLA_DK, GLA_DV), F32), pltpu.VMEM((8, CONV_CH), F32)],
        compiler_params=_params(("arbitrary", "arbitrary")),
        name="gla_conv",
    )(u, wg, bg, gn, chunk_tril, conv_w, *[w for w, _ in side_casts])
    return go, cv, casts


def _outproj_body(*refs, route):
    (o1, o4, o16, l1, l4, l16, go_ref, cv_ref, x_ref, w_ref, g_ref) = refs[:11]
    perm_ref = refs[-1]
    if route:
        wr_ref, xo_ref, ho_ref, route_ref, counts_ref, carry_ref = refs[11:-1]
    else:
        xo_ref, ho_ref = refs[11:-1]

    def token_order(ref, slot):
        dilation, rows, _ = ref.shape
        if dilation == 1:
            return ref[0].astype(F32)
        tiles = range(ATT_WIDTH // LANES)
        for r in range(dilation):
            val = ref[r].astype(F32)
            for t in tiles:
                perm_ref[slot, t, pl.ds(r, rows, stride=dilation), :] = val[:, t * LANES:(t + 1) * LANES]
        return jnp.concatenate([perm_ref[slot, t] for t in tiles], axis=-1)

    la, lb, lc = token_order(l1, 0), token_order(l4, 0), token_order(l16, 1)
    oa, ob, oc = token_order(o1, 0), token_order(o4, 2), token_order(o16, 3)
    m = jnp.maximum(jnp.maximum(la, lb), lc)
    ea, eb, ec = jnp.exp(la - m), jnp.exp(lb - m), jnp.exp(lc - m)
    att = (ea * oa + eb * ob + ec * oc) / (ea + eb + ec)
    y = (x_ref[...]
         + _dot(att.astype(BF16), w_ref[0:ATT_WIDTH, :])
         + _dot(go_ref[...], w_ref[ATT_WIDTH:ATT_WIDTH + GLA_V, :])
         + _dot(cv_ref[...], w_ref[ATT_WIDTH + GLA_V:MIX_WIDTH, :]))
    xo_ref[...] = y
    hf = _rms(y, g_ref[...])
    ho_ref[...] = hf.astype(ho_ref.dtype)
    if route:
        @pl.when(pl.program_id(0) == 0)
        def _():
            carry_ref[...] = jnp.zeros_like(carry_ref)

        tm = hf.shape[0]
        ne = N_EXPERTS
        hf_hi, hf_lo = _split_bf16(hf)
        part = _dot_nt(wr_ref[0], hf_hi) + _dot_nt(wr_ref[1], hf_lo)
        logits = part[0:ne] + part[ne:2 * ne]
        eidx = lax.broadcasted_iota(jnp.int32, logits.shape, 0).astype(F32)
        v1 = jnp.max(logits, axis=0, keepdims=True)
        i1 = jnp.min(jnp.where(logits == v1, eidx, float(ne)), axis=0, keepdims=True)
        lg2 = jnp.where(eidx == i1, -jnp.inf, logits)
        v2 = jnp.max(lg2, axis=0, keepdims=True)
        i2 = jnp.min(jnp.where(lg2 == v2, eidx, float(ne)), axis=0, keepdims=True)
        e2 = jnp.exp(v2 - v1)
        w1 = 1.0 / (1.0 + e2)
        w2 = e2 * w1
        sel1 = eidx == i1
        sel2 = eidx == i2
        onehot = jnp.where(sel1, 1.0, jnp.where(sel2, 1.0, 0.0))
        tri = (lax.broadcasted_iota(jnp.int32, (tm, tm), 0) <= lax.broadcasted_iota(jnp.int32, (tm, tm), 1))
        onehot16 = jnp.concatenate([onehot, jnp.zeros_like(onehot)], axis=0).astype(BF16)
        csum = _dot(onehot16, jnp.where(tri, 1.0, 0.0).astype(BF16))[0:ne]
        carry = carry_ref[:, 0:1]
        rank = csum - onehot + carry
        r1 = jnp.sum(jnp.where(sel1, rank, 0.0), axis=0, keepdims=True)
        r2 = jnp.sum(jnp.where(sel2, rank, 0.0), axis=0, keepdims=True)
        total = jnp.broadcast_to(carry + csum[:, tm - 1:tm], carry_ref.shape)
        carry_ref[...] = total
        counts_ref[...] = total
        rows = {ROUTE_I1: i1, ROUTE_I2: i2, ROUTE_W1: w1, ROUTE_W2: w2, ROUTE_R1: r1, ROUTE_R2: r2}
        zero = jnp.zeros_like(i1)
        route_ref[...] = jnp.concatenate([rows.get(r, zero) for r in range(8)], axis=0)


def _outproj(att, go, cv, x, w, g, w_router=None):
    route = w_router is not None
    tm = TM_PROJ
    tile = lambda cols: pl.BlockSpec((tm, cols), lambda i: (i, 0))
    full = lambda a: pl.BlockSpec(a.shape, lambda i: (0, 0))
    (o1, l1), (o4, l4), (o16, l16) = att
    args = [o1, o4, o16, l1, l4, l16, go, cv, x, w, g]
    att_specs = [_subseq_spec(d, ATT_WIDTH) for _, d in DILATED_PATTERNS]
    in_specs = att_specs * 2 + [tile(GLA_V), tile(CONV_CH), tile(D_MODEL), full(w), full(g)]
    out_specs = [tile(D_MODEL), tile(D_MODEL)]
    out_shape = [jax.ShapeDtypeStruct((TOKENS, D_MODEL), F32),
                 jax.ShapeDtypeStruct((TOKENS, D_MODEL), F32 if route else BF16)]
    scratch = []
    if route:
        args.append(w_router)
        in_specs.append(pl.BlockSpec(w_router.shape, lambda i: (0, 0, 0)))
        out_specs += [pl.BlockSpec((8, tm), lambda i: (0, i)), pl.BlockSpec((N_EXPERTS, LANES), lambda i: (0, 0))]
        out_shape += [jax.ShapeDtypeStruct((8, TOKENS), F32), jax.ShapeDtypeStruct((N_EXPERTS, LANES), F32)]
        scratch = [pltpu.VMEM((N_EXPERTS, LANES), F32)]
    scratch.append(pltpu.VMEM((4, ATT_WIDTH // LANES, tm, LANES), F32))
    return pl.pallas_call(
        functools.partial(_outproj_body, route=route),
        grid=(TOKENS // tm,),
        in_specs=in_specs,
        out_specs=out_specs,
        out_shape=out_shape,
        scratch_shapes=scratch,
        compiler_params=_params(("arbitrary",) if route else ("parallel",)),
        name="outproj_route" if route else "outproj",
    )(*args)


def _swiglu_accumulate(h, w1_ref, w3_ref, w2_ref, acc_ref, tf):
    for c0 in range(0, tf, FFN_SUB):
        c1 = min(c0 + FFN_SUB, tf)
        a = _dot(h, w1_ref[:, c0:c1])
        b = _dot(h, w3_ref[:, c0:c1])
        act = a * _sigmoid(a) * b
        acc_ref[...] += _dot(act.astype(BF16), w2_ref[c0:c1, :])


def _ffn_body(x_ref, h_ref, w1_ref, w3_ref, w2_ref, *rest, tf):
    n_cast = (len(rest) - 1) // 2
    o_ref = rest[n_cast]

    @pl.when(pl.program_id(1) == 0)
    def _():
        o_ref[...] = x_ref[...]

    _swiglu_accumulate(h_ref[...], w1_ref, w3_ref, w2_ref, o_ref, tf)
    _side_cast(rest[:n_cast], rest[n_cast + 1:])


def _ffn(x, h, w1, w3, w2, *, side_casts=()):
    nj, _, tf = w1.shape
    tm = TM_FFN
    tile = lambda cols: pl.BlockSpec((tm, cols), lambda i, j: (i, 0))
    cast_in, cast_out, cast_shapes = _side_cast_io(side_casts, (TOKENS // tm) * nj, lambda i, j: i * nj + j)
    resident = dict(pipeline_mode=pl.Buffered(1)) if nj == 1 else {}
    res = pl.pallas_call(
        functools.partial(_ffn_body, tf=tf),
        grid=(TOKENS // tm, nj),
        in_specs=[tile(D_MODEL), tile(D_MODEL),
                  pl.BlockSpec((None, D_MODEL, tf), lambda i, j: (j, 0, 0), **resident),
                  pl.BlockSpec((None, D_MODEL, tf), lambda i, j: (j, 0, 0), **resident),
                  pl.BlockSpec((tf, D_MODEL), lambda i, j: (j, 0), **resident)] + cast_in,
        out_specs=[tile(D_MODEL)] + cast_out,
        out_shape=[jax.ShapeDtypeStruct((TOKENS, D_MODEL), F32)] + cast_shapes,
        compiler_params=_params(("parallel", "arbitrary")),
        name="dense_ffn",
    )(x, h, w1, w3, w2, *[w for w, _ in side_casts])
    return res[0], res[1:]


ROW_TILE = D_MODEL // LANES


def _to_row_tiled(dst_ref, lead, val):
    rows = val.shape[0]
    for s in range(ROW_TILE):
        dst_ref[(*lead, pl.ds(s, rows, stride=ROW_TILE), slice(None))] = val[:, s * LANES:(s + 1) * LANES]


def _from_row_tiled(src_ref, lead, rows):
    return jnp.concatenate([src_ref[(*lead, pl.ds(s, rows, stride=ROW_TILE), slice(None))]
                            for s in range(ROW_TILE)], axis=-1)


def _row_tile(idx):
    return pl.ds(pl.multiple_of(idx * ROW_TILE, ROW_TILE), ROW_TILE)


def _dispatch_body(pos_ref, last_tile_ref, h_ref, xs_hbm, stage_ref, zero_ref, sem, zero_sem):
    i = pl.program_id(0)
    n = pl.num_programs(0)
    tm = DISPATCH_CHUNK
    slot = i % 2

    def drain(s):
        for _ in range(2):
            pltpu.make_async_copy(stage_ref.at[s], xs_hbm.at[pl.ds(0, tm * ROW_TILE)], sem.at[s]).wait()

    @pl.when(i == 0)
    def _():
        zero_ref[...] = jnp.zeros_like(zero_ref)

        def zero_copy(e):
            start = pl.multiple_of(last_tile_ref[e] * (TM_MOE * ROW_TILE), TM_MOE * ROW_TILE)
            return pltpu.make_async_copy(zero_ref, xs_hbm.at[pl.ds(start, TM_MOE * ROW_TILE)], zero_sem)

        for e in range(2 * N_EXPERTS):
            @pl.when(last_tile_ref[e] >= 0)
            def _():
                zero_copy(e).start()
        for e in range(2 * N_EXPERTS):
            @pl.when(last_tile_ref[e] >= 0)
            def _():
                zero_copy(e).wait()

    @pl.when(i >= 2)
    def _():
        drain(slot)

    _to_row_tiled(stage_ref, (slot,), h_ref[...])

    def body(t, carry):
        for k in range(2):
            dst = pos_ref[2 * (i * tm + t) + k]
            pltpu.make_async_copy(stage_ref.at[slot, _row_tile(t)], xs_hbm.at[_row_tile(dst)],
                                  sem.at[slot]).start(priority=k)
        return carry
    lax.fori_loop(0, tm, body, 0, unroll=8)

    @pl.when(i == n - 1)
    def _():
        drain(1 - slot)
        drain(slot)


def _dispatch(pos, last_tile, h):
    tm = DISPATCH_CHUNK
    return pl.pallas_call(
        _dispatch_body,
        grid=(TOKENS // tm,),
        in_specs=[pl.BlockSpec(memory_space=pltpu.SMEM),
                  pl.BlockSpec(memory_space=pltpu.SMEM),
                  pl.BlockSpec((tm, D_MODEL), lambda i: (i, 0))],
        out_specs=pl.BlockSpec(memory_space=pl.ANY),
        out_shape=jax.ShapeDtypeStruct((N_SORTED * ROW_TILE, LANES), F32),
        scratch_shapes=[pltpu.VMEM((2, tm * ROW_TILE, LANES), F32),
                        pltpu.VMEM((TM_MOE * ROW_TILE, LANES), F32),
                        pltpu.SemaphoreType.DMA((2,)), pltpu.SemaphoreType.DMA(())],
        compiler_params=_params(("arbitrary",)),
        name="moe_dispatch",
    )(pos, last_tile, h)


def _gffn_body(te_ref, nu_ref, xs_ref, w1_ref, w3_ref, w2_ref, o_ref, hb_ref, acc_ref):
    del te_ref
    i = pl.program_id(0)
    j = pl.program_id(1)

    used = i < nu_ref[0]

    @pl.when(j == 0)
    def _():
        acc_ref[...] = jnp.zeros_like(acc_ref)

    @pl.when(used & (j == 0))
    def _():
        hb_ref[...] = _from_row_tiled(xs_ref, (), TM_MOE).astype(BF16)

    @pl.when(used)
    def _():
        _swiglu_accumulate(hb_ref[...], w1_ref, w3_ref, w2_ref, acc_ref, TF_MOE)

    @pl.when(j == pl.num_programs(1) - 1)
    def _():
        _to_row_tiled(o_ref, (), acc_ref[...])


def _grouped_ffn(tile_expert, n_used, xs, w1, w3, w2):
    nj = FFN_EXPERT // TF_MOE
    col = lambda i, j, nu: jnp.where(i < nu[0], j, nj - 1)
    grid_spec = pltpu.PrefetchScalarGridSpec(
        num_scalar_prefetch=2,
        grid=(N_TILES_MOE, nj),
        in_specs=[
            pl.BlockSpec((TM_MOE * ROW_TILE, LANES), lambda i, j, te, nu: (jnp.minimum(i, nu[0] - 1), 0)),
            pl.BlockSpec((None, None, D_MODEL, TF_MOE), lambda i, j, te, nu: (te[i], col(i, j, nu), 0, 0)),
            pl.BlockSpec((None, None, D_MODEL, TF_MOE), lambda i, j, te, nu: (te[i], col(i, j, nu), 0, 0)),
            pl.BlockSpec((None, TF_MOE, D_MODEL), lambda i, j, te, nu: (te[i], col(i, j, nu), 0)),
        ],
        out_specs=pl.BlockSpec((TM_MOE * ROW_TILE, LANES), lambda i, j, te, nu: (i, 0)),
        scratch_shapes=[pltpu.VMEM((TM_MOE, D_MODEL), BF16), pltpu.VMEM((TM_MOE, D_MODEL), F32)],
    )
    return pl.pallas_call(
        _gffn_body,
        grid_spec=grid_spec,
        out_shape=jax.ShapeDtypeStruct((N_SORTED * ROW_TILE, LANES), F32),
        compiler_params=_params(("arbitrary", "arbitrary")),
        name="moe_ffn",
    )(tile_expert, n_used, xs, w1, w3, w2)


def _combine_body(pos_ref, x_ref, gate_ref, g_ref, ys_hbm, o_ref, buf_ref, sem):
    i = pl.program_id(0)
    n = pl.num_programs(0)
    tm = TM_COMBINE

    def issue(tile, slot):
        def body(t, carry):
            for k in range(2):
                src = pos_ref[2 * (tile * tm + t) + k]
                pltpu.make_async_copy(ys_hbm.at[_row_tile(src)], buf_ref.at[slot, k, _row_tile(t)],
                                      sem.at[slot]).start(priority=k)
            return carry
        lax.fori_loop(0, tm, body, 0, unroll=8)

    @pl.when(i == 0)
    def _():
        issue(0, 0)

    @pl.when(i + 1 < n)
    def _():
        issue(i + 1, (i + 1) % 2)

    slot = i % 2
    for k in range(2):
        pltpu.make_async_copy(ys_hbm.at[pl.ds(0, tm * ROW_TILE)], buf_ref.at[slot, k], sem.at[slot]).wait()
    w1 = gate_ref[:, 0:1]
    w2 = gate_ref[:, 1:2]
    y = x_ref[...] + w1 * _from_row_tiled(buf_ref, (slot, 0), tm) + w2 * _from_row_tiled(buf_ref, (slot, 1), tm)
    o_ref[...] = _rms(y, g_ref[...])


def _combine(pos, x, gates, g, ys):
    tm = TM_COMBINE
    return pl.pallas_call(
        _combine_body,
        grid=(TOKENS // tm,),
        in_specs=[pl.BlockSpec(memory_space=pltpu.SMEM),
                  pl.BlockSpec((tm, D_MODEL), lambda i: (i, 0)),
                  pl.BlockSpec((tm, 2), lambda i: (i, 0)),
                  pl.BlockSpec((1, D_MODEL), lambda i: (0, 0)),
                  pl.BlockSpec(memory_space=pl.ANY)],
        out_specs=pl.BlockSpec((tm, D_MODEL), lambda i: (i, 0)),
        out_shape=jax.ShapeDtypeStruct((TOKENS, D_MODEL), F32),
        scratch_shapes=[pltpu.VMEM((2, 2, tm * ROW_TILE, LANES), F32), pltpu.SemaphoreType.DMA((2,))],
        compiler_params=_params(("arbitrary",)),
        name="moe_combine",
    )(pos, x, gates, g, ys)


def _routing_tables(route, counts):
    cnt = counts[:, 0].astype(jnp.int32)
    tiles = (cnt + TM_MOE - 1) // TM_MOE
    tile_end = jnp.cumsum(tiles)
    tile_start = tile_end - tiles
    n_used = tile_end[-1]
    expert = route[ROUTE_I1:ROUTE_I2 + 1].astype(jnp.int32)
    rank = route[ROUTE_R1:ROUTE_R2 + 1].astype(jnp.int32)
    group_start = jnp.sum(jnp.where(expert[..., None] == jnp.arange(N_EXPERTS), tile_start * TM_MOE, 0), axis=-1)
    pos = (group_start + rank).T
    tile_id = jnp.minimum(jnp.arange(N_TILES_MOE, dtype=jnp.int32), n_used - 1)
    tile_expert = jnp.sum(tile_id[:, None] >= tile_end[None, :], axis=1).astype(jnp.int32)
    last_tile = jnp.where(tiles > 0, tile_end - 1, -1)
    spare = n_used + jnp.arange(N_EXPERTS)
    zero_tiles = jnp.concatenate([last_tile, jnp.where(spare < N_TILES_MOE, spare, -1)]).astype(jnp.int32)
    return pos.reshape(2 * TOKENS), tile_expert, n_used.reshape(1), zero_tiles


def _moe(x, h, route, counts, w1, w3, w2, g_final):
    pos, tile_expert, n_used, last_tile = _routing_tables(route, counts)
    xs = _dispatch(pos, last_tile, h)
    ys = _grouped_ffn(tile_expert, n_used, xs, w1, w3, w2)
    return _combine(pos, x, route[ROUTE_W1:ROUTE_W2 + 1].T, g_final, ys)


def _prep_w_in(w):
    assert sum(SPLIT_SIZES[:7]) == QKV_COLS + W_TAIL and SPLIT_SIZES[7] == GLA_RANK
    return jnp.pad(w, ((0, 0), (0, 0), (0, QKV_COLS + U_COLS - w.shape[-1]))).astype(BF16)


def _prep_router(w):
    wt = w.T
    hi = wt.astype(BF16)
    lo = (wt - hi.astype(F32)).astype(BF16)
    return jnp.stack([jnp.concatenate([hi, lo]), jnp.concatenate([hi, jnp.zeros_like(hi)])])


def kernel(x, w_mix_in, w_mix_out, g_mix, rel_bias, gla_w_gate, gla_b_gate, gla_g_norm, conv_w,
           g_ffn, ffn_w1, ffn_w3, ffn_w2, moe_router, moe_w1, moe_w3, moe_w2, g_final):
    assert DEPTH == 2
    x = x.reshape(TOKENS, D_MODEL)
    up_job = lambda w, tf: (w, w.shape[-1] // tf)
    down = lambda c: c.reshape(c.shape[0], c.shape[2], c.shape[3])
    gla_jobs = {0: up_job(moe_w1[0], TF_MOE), 1: up_job(moe_w3[0], TF_MOE)}
    experts = {}
    w_in = _prep_w_in(w_mix_in)
    for layer in range(DEPTH):
        u, *qkvs = _inproj(x, g_mix[layer].reshape(1, D_MODEL), w_in, layer)
        dense_jobs = [up_job(ffn_w1[:1], TF_DENSE), up_job(ffn_w3[:1], TF_DENSE), (ffn_w2[:1], 1)] if layer == 0 else []
        att, dense_bf16 = _attention(qkvs, rel_bias, dense_jobs)
        if layer == 0:
            ffn_bf16 = (dense_bf16[0][0], dense_bf16[1][0], down(dense_bf16[2])[0])
        wg = jnp.pad(gla_w_gate[layer], ((0, LANES - GLA_RANK), (0, 0)))
        go, cv, (experts[layer],) = _gla_conv(u, wg, gla_b_gate[layer].reshape(1, GLA_QK),
                                              gla_g_norm[layer].reshape(1, GLA_DV),
                                              jnp.pad(conv_w[layer], ((0, 8 - CONV_WIDTH), (0, 0))),
                                              side_casts=[gla_jobs[layer]])
        w_out = w_mix_out[layer].astype(BF16)
        g2 = g_ffn[layer].reshape(1, D_MODEL)
        i = layer // 2
        if layer % 2 == 0:
            x, h = _outproj(att, go, cv, x, w_out, g2)
            x, (cast,) = _ffn(x, h, *ffn_bf16, side_casts=[(moe_w2[i], 1)])
            experts["w2"] = down(cast)
        else:
            x, h, route, counts = _outproj(att, go, cv, x, w_out, g2, _prep_router(moe_router[i]))
            x = _moe(x, h, route, counts, experts[0], experts[1], experts["w2"], g_final.reshape(1, D_MODEL))
    return x.reshape(BATCH, SEQ, D_MODEL)
```

```python
import functools
import math

import jax
import jax.numpy as jnp
import numpy as np
from jax import lax
from jax.experimental import pallas as pl
from jax.experimental.pallas import tpu as pltpu

F32 = jnp.float32
BF16 = jnp.bfloat16

D_MODEL = 1024
BATCH = 8
SEQ = 2048
TOKENS = BATCH * SEQ
DEPTH = 2
EPS = 1e-6

HEAD_DIM = 64
ATT_HEADS = 4
ATT_WIDTH = ATT_HEADS * HEAD_DIM
DILATED_PATTERNS = ((128, 1), (512, 4), (2048, 16))
ATT_BLOCK = 128
REL_BUCKETS = 32
REL_MAX_DISTANCE = 2048

GLA_HEADS = 4
GLA_DK = 64
GLA_DV = 128
GLA_RANK = 16
GLA_CHUNK = 64
GLA_QK = GLA_HEADS * GLA_DK
GLA_V = GLA_HEADS * GLA_DV

CONV_CH = 256
CONV_WIDTH = 3
MIX_WIDTH = ATT_WIDTH + GLA_V + CONV_CH

SPLIT_SIZES = (ATT_WIDTH, ATT_WIDTH, ATT_WIDTH, GLA_QK, GLA_QK, GLA_V, GLA_V, GLA_RANK,
               CONV_CH, CONV_CH, CONV_CH)

FFN_DENSE = 2816
N_EXPERTS = 8
FFN_EXPERT = 3584

LANES = 128
MXU_WIDTH = 256
VMEM_LIMIT = 56 * 1024 * 1024

QKV_COLS = 3 * ATT_WIDTH
COL_GV, COL_GR, COL_GQ, COL_GK = 0, 512, 1024, 1280
COL_CIN, COL_CB, COL_CC, COL_GLR = 1536, 1792, 2048, 2304
U_COLS = COL_GLR + LANES

NEG_BIG = -1e30

ATT_UNROLL = 15
TM_PROJ = 512
TC_GLA = 512
GLA_CUMSUM_SPAN = 256
TM_FFN = 512
TF_DENSE = 2816
FFN_SUB = 256
TM_MOE = 512
TF_MOE = 1792
N_TILES_MOE = 2 * TOKENS // TM_MOE + N_EXPERTS
N_SORTED = N_TILES_MOE * TM_MOE
DISPATCH_CHUNK = 512
TM_COMBINE = 512

ROUTE_I1, ROUTE_I2, ROUTE_W1, ROUTE_W2, ROUTE_R1, ROUTE_R2 = range(6)


def _params(sem):
    return pltpu.CompilerParams(dimension_semantics=sem, vmem_limit_bytes=VMEM_LIMIT)


def _split_bf16(a):
    hi = a.astype(BF16)
    lo = (a - hi.astype(F32)).astype(BF16)
    return hi, lo


def _dot(a, b):
    return jnp.dot(a, b, preferred_element_type=F32)


def _dot3(a, b):
    a_hi, a_lo = _split_bf16(a)
    b_hi, b_lo = _split_bf16(b)
    return _dot(a_hi, b_hi) + _dot(a_lo, b_hi) + _dot(a_hi, b_lo)


def _dot_nt(a, b):
    return lax.dot_general(a, b, (((1,), (1,)), ((), ())), preferred_element_type=F32)


def _rms(x, g):
    ms = jnp.mean(x * x, axis=-1, keepdims=True)
    return x * lax.rsqrt(ms + EPS) * g


def _sigmoid(x):
    return 1.0 / (1.0 + jnp.exp(-x))


def _side_cast(srcs, dsts):
    for src, dst in zip(srcs, dsts):
        width = dst.shape[-1]
        for s in range(dst.shape[0]):
            dst[s] = src[:, s * width:(s + 1) * width].astype(dst.dtype)


def _side_cast_io(jobs, steps, step_of):
    in_specs, out_specs, out_shapes = [], [], []
    for w, splits in jobs:
        g, r, c = w.shape
        rb = g * r // steps
        per_group = r // rb
        in_specs.append(pl.BlockSpec(
            (None, rb, c), lambda *ids, pg=per_group: (step_of(*ids) // pg, step_of(*ids) % pg, 0)))
        out_specs.append(pl.BlockSpec(
            (None, splits, rb, c // splits),
            lambda *ids, pg=per_group: (step_of(*ids) // pg, 0, step_of(*ids) % pg, 0)))
        out_shapes.append(jax.ShapeDtypeStruct((g, splits, r, c // splits), BF16))
    return in_specs, out_specs, out_shapes


def _inproj_body(x_ref, g_ref, w_ref, o_ref, *rest):
    qkv_refs, qkv_f32 = rest[:-1], rest[-1]
    h = _rms(x_ref[...], g_ref[...]).astype(BF16)
    for c0 in range(0, QKV_COLS, MXU_WIDTH):
        res = _dot(h, w_ref[:, c0:c0 + MXU_WIDTH])
        for t in range(MXU_WIDTH // LANES):
            qkv_f32[c0 // LANES + t] = res[:, t * LANES:(t + 1) * LANES]
    for (_, dilation), ref in zip(DILATED_PATTERNS, qkv_refs):
        for r in range(dilation):
            rows = pl.ds(r, TM_PROJ // dilation, stride=dilation)
            ref[r] = jnp.concatenate([qkv_f32[t, rows, :] for t in range(QKV_COLS // LANES)],
                                     axis=-1).astype(ref.dtype)
    for c0 in range(0, U_COLS, MXU_WIDTH):
        c1 = min(c0 + MXU_WIDTH, U_COLS)
        o_ref[:, c0:c1] = _dot(h, w_ref[:, QKV_COLS + c0:QKV_COLS + c1])


def _subseq_spec(dilation, cols):
    tiles = SEQ // TM_PROJ
    return pl.BlockSpec((None, dilation, TM_PROJ // dilation, cols), lambda i: (i // tiles, 0, i % tiles, 0))


def _inproj(x, g, w_all, layer):
    qkv_shapes = [jax.ShapeDtypeStruct((BATCH, d, SEQ // d, QKV_COLS), BF16) for _, d in DILATED_PATTERNS]
    return pl.pallas_call(
        _inproj_body,
        grid=(TOKENS // TM_PROJ,),
        in_specs=[
            pl.BlockSpec((TM_PROJ, D_MODEL), lambda i: (i, 0)),
            pl.BlockSpec((1, D_MODEL), lambda i: (0, 0)),
            pl.BlockSpec((None, D_MODEL, QKV_COLS + U_COLS), lambda i: (layer, 0, 0)),
        ],
        out_specs=[pl.BlockSpec((TM_PROJ, U_COLS), lambda i: (i, 0))]
                  + [_subseq_spec(d, QKV_COLS) for _, d in DILATED_PATTERNS],
        out_shape=[jax.ShapeDtypeStruct((TOKENS, U_COLS), F32)] + qkv_shapes,
        scratch_shapes=[pltpu.VMEM((QKV_COLS // LANES, TM_PROJ, LANES), F32)],
        compiler_params=_params(("parallel",)),
        name="inproj",
    )(x, g, w_all)


def _rel_bucket(dist):
    max_exact = REL_BUCKETS // 2
    d = jnp.maximum(dist, 0)
    log_ratio = jnp.log(jnp.maximum(d, 1).astype(F32) / max_exact) / math.log(REL_MAX_DISTANCE / max_exact)
    large = jnp.minimum(max_exact + (log_ratio * (REL_BUCKETS - max_exact)).astype(jnp.int32), REL_BUCKETS - 1)
    return jnp.where(d < max_exact, d, large)


def _bucket_table(window, dilation):
    span = window // dilation
    qi = jnp.arange(ATT_BLOCK)[:, None]
    kj = jnp.arange(2 * ATT_BLOCK)[None, :]
    sub_dist = qi - kj + ATT_BLOCK
    band = (sub_dist >= 0) & (sub_dist <= span)
    return jnp.where(band, _rel_bucket(sub_dist * dilation), -1).astype(jnp.int32)


def _attn_body(rb_ref, bidx_ref, qkv_ref, *rest, sub_blocks, unroll):
    n_cast = (len(rest) - 3) // 2
    o_ref, lse_ref = rest[n_cast:n_cast + 2]
    bias_ref = rest[-1]
    _side_cast(rest[:n_cast], rest[n_cast + 2:-1])
    nblk = SEQ // ATT_BLOCK

    @pl.when(pl.program_id(0) == 0)
    def _():
        bidx = bidx_ref[...]
        in_prev = lax.broadcasted_iota(jnp.int32, bidx.shape, 1) < ATT_BLOCK
        for h in range(ATT_HEADS):
            acc = jnp.full(bidx.shape, NEG_BIG, F32)
            for b in range(REL_BUCKETS):
                acc = jnp.where(bidx == b, rb_ref[b, h], acc)
            bias_ref[0, h] = acc
            bias_ref[1, h] = jnp.where(in_prev, NEG_BIG, acc)
            bias_ref[2, h] = jnp.concatenate([acc[:, ATT_BLOCK:], jnp.full_like(acc[:, ATT_BLOCK:], NEG_BIG)], axis=1)

    def block(n, first):
        if first:
            rows, krows, variant = slice(0, ATT_BLOCK), slice(0, 2 * ATT_BLOCK), 2
        else:
            r0 = pl.multiple_of(n * ATT_BLOCK, ATT_BLOCK)
            rows = pl.ds(r0, ATT_BLOCK)
            krows = pl.ds(r0 - ATT_BLOCK, 2 * ATT_BLOCK)
            if sub_blocks == nblk:
                variant = 0
            elif sub_blocks == 1:
                variant = 1
            else:
                variant = jnp.where(n % sub_blocks == 0, 1, 0)
        q = qkv_ref[rows, 0:ATT_WIDTH]
        kk = qkv_ref[krows, ATT_WIDTH:2 * ATT_WIDTH]
        vv = qkv_ref[krows, 2 * ATT_WIDTH:3 * ATT_WIDTH]
        q = q * jnp.asarray(HEAD_DIM ** -0.5, BF16)
        head_of_lane = lax.broadcasted_iota(jnp.int32, (ATT_BLOCK, ATT_WIDTH), 1) // HEAD_DIM
        ones = jnp.ones((kk.shape[0], LANES), BF16)
        num = den = mx = None
        for h in range(ATT_HEADS):
            mine = head_of_lane == h
            bias = bias_ref[variant, h]
            s = _dot_nt(jnp.where(mine, q, jnp.zeros_like(q)), kk) + bias
            m = jnp.max(s, axis=-1, keepdims=True)
            p = jnp.exp(s - m).astype(BF16)
            num_h = _dot(p, vv)
            den_h = jnp.tile(_dot(p, ones), (1, ATT_WIDTH // LANES))
            m_h = jnp.broadcast_to(m, (ATT_BLOCK, ATT_WIDTH))
            num = num_h if h == 0 else jnp.where(mine, num_h, num)
            den = den_h if h == 0 else jnp.where(mine, den_h, den)
            mx = m_h if h == 0 else jnp.where(mine, m_h, mx)
        o_ref[rows, :] = (num / den).astype(o_ref.dtype)
        lse_ref[rows, :] = mx + jnp.log(den)

    block(0, True)

    def loop_body(n, carry):
        block(n, False)
        return carry
    lax.fori_loop(1, nblk, loop_body, 0, unroll=unroll)


def _attention_pattern(ua, rel_bias, window, dilation, side_casts=()):
    L = SEQ // dilation
    shape = (BATCH, dilation, L, ATT_WIDTH)
    qkv_spec = pl.BlockSpec((None, SEQ, QKV_COLS), lambda b: (b, 0, 0))
    out_spec = pl.BlockSpec((None, SEQ, ATT_WIDTH), lambda b: (b, 0, 0))
    cast_in, cast_out, cast_shapes = _side_cast_io(side_casts, BATCH, lambda b: b)
    o, lse, *casts = pl.pallas_call(
        functools.partial(_attn_body, sub_blocks=L // ATT_BLOCK, unroll=ATT_UNROLL),
        grid=(BATCH,),
        in_specs=[
            pl.BlockSpec(memory_space=pltpu.SMEM),
            pl.BlockSpec((ATT_BLOCK, 2 * ATT_BLOCK), lambda b: (0, 0)),
            qkv_spec,
        ] + cast_in,
        out_specs=[out_spec, out_spec] + cast_out,
        out_shape=[jax.ShapeDtypeStruct((BATCH, SEQ, ATT_WIDTH), BF16),
                   jax.ShapeDtypeStruct((BATCH, SEQ, ATT_WIDTH), F32)] + cast_shapes,
        scratch_shapes=[pltpu.VMEM((3, ATT_HEADS, ATT_BLOCK, 2 * ATT_BLOCK), F32)],
        compiler_params=_params(("arbitrary",)),
        name=f"attn_d{dilation}",
    )(rel_bias, _bucket_table(window, dilation), ua.reshape(BATCH, SEQ, QKV_COLS), *[w for w, _ in side_casts])
    return (o.reshape(shape), lse.reshape(shape)), casts


def _attention(qkvs, rel_bias, side_casts=()):
    side_casts = list(side_casts) + [None] * (len(DILATED_PATTERNS) - len(side_casts))
    res = [_attention_pattern(ua, rel_bias, window, dilation, [] if w is None else [w])
           for ua, (window, dilation), w in zip(qkvs, DILATED_PATTERNS, side_casts)]
    return [r[0] for r in res], [c for r in res for c in r[1]]


def _short_conv(cin_ref, cb_ref, cc_ref, w_ref, o_ref, tail_ref):
    uu = cc_ref[...] * cin_ref[...]
    t = lax.broadcasted_iota(jnp.int32, uu.shape, 0)
    y = uu * w_ref[CONV_WIDTH - 1:CONV_WIDTH, :]
    for shift in range(1, CONV_WIDTH):
        prev = pltpu.roll(uu, shift, axis=0)
        for r in range(shift):
            prev = jnp.where(t == r, tail_ref[8 - shift + r:8 - shift + r + 1, :], prev)
        y = y + prev * w_ref[CONV_WIDTH - 1 - shift:CONV_WIDTH - shift, :]
    o_ref[...] = (cb_ref[...] * y).astype(o_ref.dtype)
    tail_ref[...] = uu[uu.shape[0] - 8:, :]


def _gla_body(u_ref, wg_ref, bg_ref, gn_ref, ctril_ref, cw_ref, *rest):
    cols = lambda col, width: u_ref.at[:, col:col + width]
    q_ref, k_ref = cols(COL_GQ, GLA_QK), cols(COL_GK, GLA_QK)
    v_ref, gr_ref, glr_ref = cols(COL_GV, GLA_V), cols(COL_GR, GLA_V), cols(COL_GLR, LANES)
    cin_ref, cb_ref, cc_ref = cols(COL_CIN, CONV_CH), cols(COL_CB, CONV_CH), cols(COL_CC, CONV_CH)
    _gla_tile(q_ref, k_ref, v_ref, gr_ref, glr_ref, wg_ref, bg_ref, gn_ref, ctril_ref,
              cin_ref, cb_ref, cc_ref, cw_ref, *rest)


def _gla_tile(q_ref, k_ref, v_ref, gr_ref, glr_ref, wg_ref, bg_ref, gn_ref, ctril_ref,
              cin_ref, cb_ref, cc_ref, cw_ref, *rest):
    n_cast = (len(rest) - 4) // 2
    o_ref, cv_ref = rest[n_cast:n_cast + 2]
    s_ref, tail_ref = rest[-2:]
    _side_cast(rest[:n_cast], rest[n_cast + 2:-2])

    @pl.when(pl.program_id(1) == 0)
    def _():
        s_ref[...] = jnp.zeros_like(s_ref)
        tail_ref[...] = jnp.zeros_like(tail_ref)

    _short_conv(cin_ref, cb_ref, cc_ref, cw_ref, cv_ref, tail_ref)

    C = GLA_CHUNK
    row = lax.broadcasted_iota(jnp.int32, (C, C), 0)
    col = lax.broadcasted_iota(jnp.int32, (C, C), 1)
    tril = row >= col
    n_chunks = TC_GLA // C

    xg = _dot3(glr_ref[...], wg_ref[...]) + bg_ref[...]
    la_all = (jnp.minimum(xg, 0.0) - jnp.log(1.0 + jnp.exp(-jnp.abs(xg)))) * (1.0 / 16.0)

    la_hi, la_lo = _split_bf16(la_all)
    span = ctril_ref.shape[0]
    cum_all = jnp.concatenate(
        [_dot(ctril_ref[...], la_hi[r0:r0 + span]) + _dot(ctril_ref[...], la_lo[r0:r0 + span])
         for r0 in range(0, TC_GLA, span)], axis=0)
    totals = jnp.concatenate([cum_all[(c + 1) * C - 1:(c + 1) * C, :] for c in range(n_chunks)]
                             + [jnp.zeros((LANES - n_chunks, GLA_QK), F32)], axis=0)
    decay_cols = jnp.exp(totals.T)

    for c in range(n_chunks):
        rows = slice(c * C, (c + 1) * C)
        cum = cum_all[rows]
        last = cum[C - 1:C, :]
        q = q_ref[rows, :]
        k = k_ref[rows, :]
        qs = q * (GLA_DK ** -0.5)
        qt = (qs * jnp.exp(cum)).astype(BF16)
        mid = cum[C // 2 - 1:C // 2, :]
        qm = (qs * jnp.exp(cum - mid)).astype(BF16)
        kt = (k * jnp.exp(mid - cum)).astype(BF16)
        kl_t = (k * jnp.exp(last - cum)).T.astype(BF16)
        for h in range(GLA_HEADS):
            sl = slice(h * GLA_DK, (h + 1) * GLA_DK)
            vs = slice(h * GLA_DV, (h + 1) * GLA_DV)
            vh = v_ref[rows, vs].astype(BF16)
            state = s_ref[h]
            st_hi, st_lo = _split_bf16(state)
            sc = jnp.where(tril, _dot_nt(qm[:, sl], kt[:, sl]), 0.0).astype(BF16)
            o = _dot(qt[:, sl], st_hi) + _dot(qt[:, sl], st_lo) + _dot(sc, vh)
            decay = jnp.broadcast_to(decay_cols[sl, c:c + 1], state.shape)
            s_ref[h] = decay * state + _dot(kl_t[sl, :], vh)
            g = gr_ref[rows, vs]
            o_ref[rows, vs] = (_rms(o, gn_ref[...]) * (g * _sigmoid(g))).astype(o_ref.dtype)


def _gla_conv(u, wg, bg, gn, conv_w, side_casts=()):
    nj = SEQ // TC_GLA
    row = lambda b, j: b * nj + j
    full = lambda a: pl.BlockSpec(a.shape, lambda b, j: (0, 0))
    cast_in, cast_out, cast_shapes = _side_cast_io(side_casts, BATCH * nj, row)
    t = np.arange(GLA_CUMSUM_SPAN)
    same_chunk = (t[:, None] // GLA_CHUNK) == (t[None, :] // GLA_CHUNK)
    chunk_tril = jnp.asarray(same_chunk & (t[:, None] >= t[None, :]), BF16)
    go, cv, *casts = pl.pallas_call(
        _gla_body,
        grid=(BATCH, nj),
        in_specs=[pl.BlockSpec((TC_GLA, U_COLS), lambda b, j: (row(b, j), 0)),
                  full(wg), full(bg), full(gn), full(chunk_tril), full(conv_w)] + cast_in,
        out_specs=[pl.BlockSpec((TC_GLA, GLA_V), lambda b, j: (row(b, j), 0)),
                   pl.BlockSpec((TC_GLA, CONV_CH), lambda b, j: (row(b, j), 0))] + cast_out,
        out_shape=[jax.ShapeDtypeStruct((TOKENS, GLA_V), BF16), jax.ShapeDtypeStruct((TOKENS, CONV_CH), BF16)]
                  + cast_shapes,
        scratch_shapes=[pltpu.VMEM((GLA_HEADS, GLA_DK, GLA_DV), F32), pltpu.VMEM((8, CONV_CH), F32)],
        compiler_params=_params(("arbitrary", "arbitrary")),
        name="gla_conv",
    )(u, wg, bg, gn, chunk_tril, conv_w, *[w for w, _ in side_casts])
    return go, cv, casts


def _outproj_body(*refs, route):
    (o1, o4, o16, l1, l4, l16, go_ref, cv_ref, x_ref, w_ref, g_ref) = refs[:11]
    perm_ref = refs[-1]
    if route:
        wr_ref, xo_ref, ho_ref, route_ref, counts_ref, carry_ref = refs[11:-1]
    else:
        xo_ref, ho_ref = refs[11:-1]

    def token_order(ref, slot):
        dilation, rows, _ = ref.shape
        if dilation == 1:
            return ref[0].astype(F32)
        tiles = range(ATT_WIDTH // LANES)
        for r in range(dilation):
            val = ref[r].astype(F32)
            for t in tiles:
                perm_ref[slot, t, pl.ds(r, rows, stride=dilation), :] = val[:, t * LANES:(t + 1) * LANES]
        return jnp.concatenate([perm_ref[slot, t] for t in tiles], axis=-1)

    la, lb, lc = token_order(l1, 0), token_order(l4, 0), token_order(l16, 1)
    oa, ob, oc = token_order(o1, 0), token_order(o4, 2), token_order(o16, 3)
    m = jnp.maximum(jnp.maximum(la, lb), lc)
    ea, eb, ec = jnp.exp(la - m), jnp.exp(lb - m), jnp.exp(lc - m)
    att = (ea * oa + eb * ob + ec * oc) / (ea + eb + ec)
    y = (x_ref[...]
         + _dot(att.astype(BF16), w_ref[0:ATT_WIDTH, :])
         + _dot(go_ref[...], w_ref[ATT_WIDTH:ATT_WIDTH + GLA_V, :])
         + _dot(cv_ref[...], w_ref[ATT_WIDTH + GLA_V:MIX_WIDTH, :]))
    xo_ref[...] = y
    hf = _rms(y, g_ref[...])
    ho_ref[...] = hf.astype(ho_ref.dtype)
    if route:
        @pl.when(pl.program_id(0) == 0)
        def _():
            carry_ref[...] = jnp.zeros_like(carry_ref)

        tm = hf.shape[0]
        ne = N_EXPERTS
        hf_hi, hf_lo = _split_bf16(hf)
        part = _dot_nt(wr_ref[0], hf_hi) + _dot_nt(wr_ref[1], hf_lo)
        logits = part[0:ne] + part[ne:2 * ne]
        eidx = lax.broadcasted_iota(jnp.int32, logits.shape, 0).astype(F32)
        v1 = jnp.max(logits, axis=0, keepdims=True)
        i1 = jnp.min(jnp.where(logits == v1, eidx, float(ne)), axis=0, keepdims=True)
        lg2 = jnp.where(eidx == i1, -jnp.inf, logits)
        v2 = jnp.max(lg2, axis=0, keepdims=True)
        i2 = jnp.min(jnp.where(lg2 == v2, eidx, float(ne)), axis=0, keepdims=True)
        e2 = jnp.exp(v2 - v1)
        w1 = 1.0 / (1.0 + e2)
        w2 = e2 * w1
        sel1 = eidx == i1
        sel2 = eidx == i2
        onehot = jnp.where(sel1, 1.0, jnp.where(sel2, 1.0, 0.0))
        tri = (lax.broadcasted_iota(jnp.int32, (tm, tm), 0) <= lax.broadcasted_iota(jnp.int32, (tm, tm), 1))
        onehot16 = jnp.concatenate([onehot, jnp.zeros_like(onehot)], axis=0).astype(BF16)
        csum = _dot(onehot16, jnp.where(tri, 1.0, 0.0).astype(BF16))[0:ne]
        carry = carry_ref[:, 0:1]
        rank = csum - onehot + carry
        r1 = jnp.sum(jnp.where(sel1, rank, 0.0), axis=0, keepdims=True)
        r2 = jnp.sum(jnp.where(sel2, rank, 0.0), axis=0, keepdims=True)
        total = jnp.broadcast_to(carry + csum[:, tm - 1:tm], carry_ref.shape)
        carry_ref[...] = total
        counts_ref[...] = total
        rows = {ROUTE_I1: i1, ROUTE_I2: i2, ROUTE_W1: w1, ROUTE_W2: w2, ROUTE_R1: r1, ROUTE_R2: r2}
        zero = jnp.zeros_like(i1)
        route_ref[...] = jnp.concatenate([rows.get(r, zero) for r in range(8)], axis=0)


def _outproj(att, go, cv, x, w, g, w_router=None):
    route = w_router is not None
    tm = TM_PROJ
    tile = lambda cols: pl.BlockSpec((tm, cols), lambda i: (i, 0))
    full = lambda a: pl.BlockSpec(a.shape, lambda i: (0, 0))
    (o1, l1), (o4, l4), (o16, l16) = att
    args = [o1, o4, o16, l1, l4, l16, go, cv, x, w, g]
    att_specs = [_subseq_spec(d, ATT_WIDTH) for _, d in DILATED_PATTERNS]
    in_specs = att_specs * 2 + [tile(GLA_V), tile(CONV_CH), tile(D_MODEL), full(w), full(g)]
    out_specs = [tile(D_MODEL), tile(D_MODEL)]
    out_shape = [jax.ShapeDtypeStruct((TOKENS, D_MODEL), F32),
                 jax.ShapeDtypeStruct((TOKENS, D_MODEL), F32 if route else BF16)]
    scratch = []
    if route:
        args.append(w_router)
        in_specs.append(pl.BlockSpec(w_router.shape, lambda i: (0, 0, 0)))
        out_specs += [pl.BlockSpec((8, tm), lambda i: (0, i)), pl.BlockSpec((N_EXPERTS, LANES), lambda i: (0, 0))]
        out_shape += [jax.ShapeDtypeStruct((8, TOKENS), F32), jax.ShapeDtypeStruct((N_EXPERTS, LANES), F32)]
        scratch = [pltpu.VMEM((N_EXPERTS, LANES), F32)]
    scratch.append(pltpu.VMEM((4, ATT_WIDTH // LANES, tm, LANES), F32))
    return pl.pallas_call(
        functools.partial(_outproj_body, route=route),
        grid=(TOKENS // tm,),
        in_specs=in_specs,
        out_specs=out_specs,
        out_shape=out_shape,
        scratch_shapes=scratch,
        compiler_params=_params(("arbitrary",) if route else ("parallel",)),
        name="outproj_route" if route else "outproj",
    )(*args)


def _swiglu_accumulate(h, w1_ref, w3_ref, w2_ref, acc_ref, tf):
    for c0 in range(0, tf, FFN_SUB):
        c1 = min(c0 + FFN_SUB, tf)
        a = _dot(h, w1_ref[:, c0:c1])
        b = _dot(h, w3_ref[:, c0:c1])
        act = a * _sigmoid(a) * b
        acc_ref[...] += _dot(act.astype(BF16), w2_ref[c0:c1, :])


def _ffn_body(x_ref, h_ref, w1_ref, w3_ref, w2_ref, *rest, tf):
    n_cast = (len(rest) - 1) // 2
    o_ref = rest[n_cast]

    @pl.when(pl.program_id(1) == 0)
    def _():
        o_ref[...] = x_ref[...]

    _swiglu_accumulate(h_ref[...], w1_ref, w3_ref, w2_ref, o_ref, tf)
    _side_cast(rest[:n_cast], rest[n_cast + 1:])


def _ffn(x, h, w1, w3, w2, *, side_casts=()):
    nj, _, tf = w1.shape
    tm = TM_FFN
    tile = lambda cols: pl.BlockSpec((tm, cols), lambda i, j: (i, 0))
    cast_in, cast_out, cast_shapes = _side_cast_io(side_casts, (TOKENS // tm) * nj, lambda i, j: i * nj + j)
    resident = dict(pipeline_mode=pl.Buffered(1)) if nj == 1 else {}
    res = pl.pallas_call(
        functools.partial(_ffn_body, tf=tf),
        grid=(TOKENS // tm, nj),
        in_specs=[tile(D_MODEL), tile(D_MODEL),
                  pl.BlockSpec((None, D_MODEL, tf), lambda i, j: (j, 0, 0), **resident),
                  pl.BlockSpec((None, D_MODEL, tf), lambda i, j: (j, 0, 0), **resident),
                  pl.BlockSpec((tf, D_MODEL), lambda i, j: (j, 0), **resident)] + cast_in,
        out_specs=[tile(D_MODEL)] + cast_out,
        out_shape=[jax.ShapeDtypeStruct((TOKENS, D_MODEL), F32)] + cast_shapes,
        compiler_params=_params(("parallel", "arbitrary")),
        name="dense_ffn",
    )(x, h, w1, w3, w2, *[w for w, _ in side_casts])
    return res[0], res[1:]


ROW_TILE = D_MODEL // LANES


def _to_row_tiled(dst_ref, lead, val):
    rows = val.shape[0]
    for s in range(ROW_TILE):
        dst_ref[(*lead, pl.ds(s, rows, stride=ROW_TILE), slice(None))] = val[:, s * LANES:(s + 1) * LANES]


def _from_row_tiled(src_ref, lead, rows):
    return jnp.concatenate([src_ref[(*lead, pl.ds(s, rows, stride=ROW_TILE), slice(None))]
                            for s in range(ROW_TILE)], axis=-1)


def _row_tile(idx):
    return pl.ds(pl.multiple_of(idx * ROW_TILE, ROW_TILE), ROW_TILE)


def _dispatch_body(pos_ref, last_tile_ref, h_ref, xs_hbm, stage_ref, zero_ref, sem, zero_sem):
    i = pl.program_id(0)
    n = pl.num_programs(0)
    tm = DISPATCH_CHUNK
    slot = i % 2

    def drain(s):
        for _ in range(2):
            pltpu.make_async_copy(stage_ref.at[s], xs_hbm.at[pl.ds(0, tm * ROW_TILE)], sem.at[s]).wait()

    @pl.when(i == 0)
    def _():
        zero_ref[...] = jnp.zeros_like(zero_ref)

        def zero_copy(e):
            start = pl.multiple_of(last_tile_ref[e] * (TM_MOE * ROW_TILE), TM_MOE * ROW_TILE)
            return pltpu.make_async_copy(zero_ref, xs_hbm.at[pl.ds(start, TM_MOE * ROW_TILE)], zero_sem)

        for e in range(2 * N_EXPERTS):
            @pl.when(last_tile_ref[e] >= 0)
            def _():
                zero_copy(e).start()
        for e in range(2 * N_EXPERTS):
            @pl.when(last_tile_ref[e] >= 0)
            def _():
                zero_copy(e).wait()

    @pl.when(i >= 2)
    def _():
        drain(slot)

    _to_row_tiled(stage_ref, (slot,), h_ref[...])

    def body(t, carry):
        for k in range(2):
            dst = pos_ref[2 * (i * tm + t) + k]
            pltpu.make_async_copy(stage_ref.at[slot, _row_tile(t)], xs_hbm.at[_row_tile(dst)],
                                  sem.at[slot]).start(priority=k)
        return carry
    lax.fori_loop(0, tm, body, 0, unroll=8)

    @pl.when(i == n - 1)
    def _():
        drain(1 - slot)
        drain(slot)


def _dispatch(pos, last_tile, h):
    tm = DISPATCH_CHUNK
    return pl.pallas_call(
        _dispatch_body,
        grid=(TOKENS // tm,),
        in_specs=[pl.BlockSpec(memory_space=pltpu.SMEM),
                  pl.BlockSpec(memory_space=pltpu.SMEM),
                  pl.BlockSpec((tm, D_MODEL), lambda i: (i, 0))],
        out_specs=pl.BlockSpec(memory_space=pl.ANY),
        out_shape=jax.ShapeDtypeStruct((N_SORTED * ROW_TILE, LANES), F32),
        scratch_shapes=[pltpu.VMEM((2, tm * ROW_TILE, LANES), F32),
                        pltpu.VMEM((TM_MOE * ROW_TILE, LANES), F32),
                        pltpu.SemaphoreType.DMA((2,)), pltpu.SemaphoreType.DMA(())],
        compiler_params=_params(("arbitrary",)),
        name="moe_dispatch",
    )(pos, last_tile, h)


def _gffn_body(te_ref, nu_ref, xs_ref, w1_ref, w3_ref, w2_ref, o_ref, hb_ref, acc_ref):
    del te_ref
    i = pl.program_id(0)
    j = pl.program_id(1)

    used = i < nu_ref[0]

    @pl.when(j == 0)
    def _():
        acc_ref[...] = jnp.zeros_like(acc_ref)

    @pl.when(used & (j == 0))
    def _():
        hb_ref[...] = _from_row_tiled(xs_ref, (), TM_MOE).astype(BF16)

    @pl.when(used)
    def _():
        _swiglu_accumulate(hb_ref[...], w1_ref, w3_ref, w2_ref, acc_ref, TF_MOE)

    @pl.when(j == pl.num_programs(1) - 1)
    def _():
        _to_row_tiled(o_ref, (), acc_ref[...])


def _grouped_ffn(tile_expert, n_used, xs, w1, w3, w2):
    nj = FFN_EXPERT // TF_MOE
    col = lambda i, j, nu: jnp.where(i < nu[0], j, nj - 1)
    grid_spec = pltpu.PrefetchScalarGridSpec(
        num_scalar_prefetch=2,
        grid=(N_TILES_MOE, nj),
        in_specs=[
            pl.BlockSpec((TM_MOE * ROW_TILE, LANES), lambda i, j, te, nu: (jnp.minimum(i, nu[0] - 1), 0)),
            pl.BlockSpec((None, None, D_MODEL, TF_MOE), lambda i, j, te, nu: (te[i], col(i, j, nu), 0, 0)),
            pl.BlockSpec((None, None, D_MODEL, TF_MOE), lambda i, j, te, nu: (te[i], col(i, j, nu), 0, 0)),
            pl.BlockSpec((None, TF_MOE, D_MODEL), lambda i, j, te, nu: (te[i], col(i, j, nu), 0)),
        ],
        out_specs=pl.BlockSpec((TM_MOE * ROW_TILE, LANES), lambda i, j, te, nu: (i, 0)),
        scratch_shapes=[pltpu.VMEM((TM_MOE, D_MODEL), BF16), pltpu.VMEM((TM_MOE, D_MODEL), F32)],
    )
    return pl.pallas_call(
        _gffn_body,
        grid_spec=grid_spec,
        out_shape=jax.ShapeDtypeStruct((N_SORTED * ROW_TILE, LANES), F32),
        compiler_params=_params(("arbitrary", "arbitrary")),
        name="moe_ffn",
    )(tile_expert, n_used, xs, w1, w3, w2)


def _combine_body(pos_ref, x_ref, gate_ref, g_ref, ys_hbm, o_ref, buf_ref, sem):
    i = pl.program_id(0)
    n = pl.num_programs(0)
    tm = TM_COMBINE

    def issue(tile, slot):
        def body(t, carry):
            for k in range(2):
                src = pos_ref[2 * (tile * tm + t) + k]
                pltpu.make_async_copy(ys_hbm.at[_row_tile(src)], buf_ref.at[slot, k, _row_tile(t)],
                                      sem.at[slot]).start(priority=k)
            return carry
        lax.fori_loop(0, tm, body, 0, unroll=8)

    @pl.when(i == 0)
    def _():
        issue(0, 0)

    @pl.when(i + 1 < n)
    def _():
        issue(i + 1, (i + 1) % 2)

    slot = i % 2
    for k in range(2):
        pltpu.make_async_copy(ys_hbm.at[pl.ds(0, tm * ROW_TILE)], buf_ref.at[slot, k], sem.at[slot]).wait()
    w1 = gate_ref[:, 0:1]
    w2 = gate_ref[:, 1:2]
    y = x_ref[...] + w1 * _from_row_tiled(buf_ref, (slot, 0), tm) + w2 * _from_row_tiled(buf_ref, (slot, 1), tm)
    o_ref[...] = _rms(y, g_ref[...])


def _combine(pos, x, gates, g, ys):
    tm = TM_COMBINE
    return pl.pallas_call(
        _combine_body,
        grid=(TOKENS // tm,),
        in_specs=[pl.BlockSpec(memory_space=pltpu.SMEM),
                  pl.BlockSpec((tm, D_MODEL), lambda i: (i, 0)),
                  pl.BlockSpec((tm, 2), lambda i: (i, 0)),
                  pl.BlockSpec((1, D_MODEL), lambda i: (0, 0)),
                  pl.BlockSpec(memory_space=pl.ANY)],
        out_specs=pl.BlockSpec((tm, D_MODEL), lambda i: (i, 0)),
        out_shape=jax.ShapeDtypeStruct((TOKENS, D_MODEL), F32),
        scratch_shapes=[pltpu.VMEM((2, 2, tm * ROW_TILE, LANES), F32), pltpu.SemaphoreType.DMA((2,))],
        compiler_params=_params(("arbitrary",)),
        name="moe_combine",
    )(pos, x, gates, g, ys)


def _routing_tables(route, counts):
    cnt = counts[:, 0].astype(jnp.int32)
    tiles = (cnt + TM_MOE - 1) // TM_MOE
    tile_end = jnp.cumsum(tiles)
    tile_start = tile_end - tiles
    n_used = tile_end[-1]
    expert = route[ROUTE_I1:ROUTE_I2 + 1].astype(jnp.int32)
    rank = route[ROUTE_R1:ROUTE_R2 + 1].astype(jnp.int32)
    group_start = jnp.sum(jnp.where(expert[..., None] == jnp.arange(N_EXPERTS), tile_start * TM_MOE, 0), axis=-1)
    pos = (group_start + rank).T
    tile_id = jnp.minimum(jnp.arange(N_TILES_MOE, dtype=jnp.int32), n_used - 1)
    tile_expert = jnp.sum(tile_id[:, None] >= tile_end[None, :], axis=1).astype(jnp.int32)
    last_tile = jnp.where(tiles > 0, tile_end - 1, -1)
    spare = n_used + jnp.arange(N_EXPERTS)
    zero_tiles = jnp.concatenate([last_tile, jnp.where(spare < N_TILES_MOE, spare, -1)]).astype(jnp.int32)
    return pos.reshape(2 * TOKENS), tile_expert, n_used.reshape(1), zero_tiles


def _moe(x, h, route, counts, w1, w3, w2, g_final):
    pos, tile_expert, n_used, last_tile = _routing_tables(route, counts)
    xs = _dispatch(pos, last_tile, h)
    ys = _grouped_ffn(tile_expert, n_used, xs, w1, w3, w2)
    return _combine(pos, x, route[ROUTE_W1:ROUTE_W2 + 1].T, g_final, ys)


def _prep_w_in(w):
    aq, ak, av, gq, gk, gv, gr, glr, c_in, c_b, c_c = jnp.split(w, np.cumsum(SPLIT_SIZES)[:-1].tolist(), axis=2)
    pad = jnp.zeros(w.shape[:2] + (LANES - GLA_RANK,), w.dtype)
    return jnp.concatenate([aq, ak, av, gv, gr, gq, gk, c_in, c_b, c_c, glr, pad], axis=2).astype(BF16)


def _prep_router(w):
    wt = w.T
    hi = wt.astype(BF16)
    lo = (wt - hi.astype(F32)).astype(BF16)
    return jnp.stack([jnp.concatenate([hi, lo]), jnp.concatenate([hi, jnp.zeros_like(hi)])])


def kernel(x, w_mix_in, w_mix_out, g_mix, rel_bias, gla_w_gate, gla_b_gate, gla_g_norm, conv_w,
           g_ffn, ffn_w1, ffn_w3, ffn_w2, moe_router, moe_w1, moe_w3, moe_w2, g_final):
    assert DEPTH == 2
    x = x.reshape(TOKENS, D_MODEL)
    up_job = lambda w, tf: (w, w.shape[-1] // tf)
    down = lambda c: c.reshape(c.shape[0], c.shape[2], c.shape[3])
    gla_jobs = {0: up_job(moe_w1[0], TF_MOE), 1: up_job(moe_w3[0], TF_MOE)}
    experts = {}
    w_in = _prep_w_in(w_mix_in)
    for layer in range(DEPTH):
        u, *qkvs = _inproj(x, g_mix[layer].reshape(1, D_MODEL), w_in, layer)
        dense_jobs = [up_job(ffn_w1[:1], TF_DENSE), up_job(ffn_w3[:1], TF_DENSE), (ffn_w2[:1], 1)] if layer == 0 else []
        att, dense_bf16 = _attention(qkvs, rel_bias, dense_jobs)
        if layer == 0:
            ffn_bf16 = (dense_bf16[0][0], dense_bf16[1][0], down(dense_bf16[2])[0])
        wg = jnp.pad(gla_w_gate[layer], ((0, LANES - GLA_RANK), (0, 0)))
        go, cv, (experts[layer],) = _gla_conv(u, wg, gla_b_gate[layer].reshape(1, GLA_QK),
                                              gla_g_norm[layer].reshape(1, GLA_DV),
                                              jnp.pad(conv_w[layer], ((0, 8 - CONV_WIDTH), (0, 0))),
                                              side_casts=[gla_jobs[layer]])
        w_out = w_mix_out[layer].astype(BF16)
        g2 = g_ffn[layer].reshape(1, D_MODEL)
        i = layer // 2
        if layer % 2 == 0:
            x, h = _outproj(att, go, cv, x, w_out, g2)
            x, (cast,) = _ffn(x, h, *ffn_bf16, side_casts=[(moe_w2[i], 1)])
            experts["w2"] = down(cast)
        else:
            x, h, route, counts = _outproj(att, go, cv, x, w_out, g2, _prep_router(moe_router[i]))
            x = _moe(x, h, route, counts, experts[0], experts[1], experts["w2"], g_final.reshape(1, D_MODEL))
    return x.reshape(BATCH, SEQ, D_MODEL)
```

```python
import functools
import math

import jax
import jax.numpy as jnp
import numpy as np
from jax import lax
from jax.experimental import pallas as pl
from jax.experimental.pallas import tpu as pltpu

F32 = jnp.float32
BF16 = jnp.bfloat16

D_MODEL = 1024
BATCH = 8
SEQ = 2048
TOKENS = BATCH * SEQ
DEPTH = 2
EPS = 1e-6

HEAD_DIM = 64
ATT_HEADS = 4
ATT_WIDTH = ATT_HEADS * HEAD_DIM
DILATED_PATTERNS = ((128, 1), (512, 4), (2048, 16))
ATT_BLOCK = 128
REL_BUCKETS = 32
REL_MAX_DISTANCE = 2048

GLA_HEADS = 4
GLA_DK = 64
GLA_DV = 128
GLA_RANK = 16
GLA_CHUNK = 64
GLA_QK = GLA_HEADS * GLA_DK
GLA_V = GLA_HEADS * GLA_DV

CONV_CH = 256
CONV_WIDTH = 3
MIX_WIDTH = ATT_WIDTH + GLA_V + CONV_CH

SPLIT_SIZES = (ATT_WIDTH, ATT_WIDTH, ATT_WIDTH, GLA_QK, GLA_QK, GLA_V, GLA_V, GLA_RANK,
               CONV_CH, CONV_CH, CONV_CH)

FFN_DENSE = 2816
N_EXPERTS = 8
FFN_EXPERT = 3584

LANES = 128
MXU_WIDTH = 256
VMEM_LIMIT = 56 * 1024 * 1024

QKV_COLS = 3 * ATT_WIDTH
COL_GV, COL_GR, COL_GQ, COL_GK = 0, 512, 1024, 1280
COL_CIN, COL_CB, COL_CC, COL_GLR = 1536, 1792, 2048, 2304
U_COLS = COL_GLR + LANES

NEG_BIG = -1e30

ATT_UNROLL = 15
TM_PROJ = 512
TC_GLA = 512
GLA_CUMSUM_SPAN = 256
TM_FFN = 512
TF_DENSE = 2816
FFN_SUB = 256
TM_MOE = 512
TF_MOE = 1792
N_TILES_MOE = 2 * TOKENS // TM_MOE + N_EXPERTS
N_SORTED = N_TILES_MOE * TM_MOE
DISPATCH_CHUNK = 512
TM_COMBINE = 512

ROUTE_I1, ROUTE_I2, ROUTE_W1, ROUTE_W2, ROUTE_R1, ROUTE_R2 = range(6)


def _params(sem):
    return pltpu.CompilerParams(dimension_semantics=sem, vmem_limit_bytes=VMEM_LIMIT)


def _split_bf16(a):
    hi = a.astype(BF16)
    lo = (a - hi.astype(F32)).astype(BF16)
    return hi, lo


def _dot(a, b):
    return jnp.dot(a, b, preferred_element_type=F32)


def _dot3(a, b):
    a_hi, a_lo = _split_bf16(a)
    b_hi, b_lo = _split_bf16(b)
    return _dot(a_hi, b_hi) + _dot(a_lo, b_hi) + _dot(a_hi, b_lo)


def _dot_nt(a, b):
    return lax.dot_general(a, b, (((1,), (1,)), ((), ())), preferred_element_type=F32)


def _rms(x, g):
    ms = jnp.mean(x * x, axis=-1, keepdims=True)
    return x * lax.rsqrt(ms + EPS) * g


def _sigmoid(x):
    return 1.0 / (1.0 + jnp.exp(-x))


def _side_cast(srcs, dsts):
    for src, dst in zip(srcs, dsts):
        width = dst.shape[-1]
        for s in range(dst.shape[0]):
            dst[s] = src[:, s * width:(s + 1) * width].astype(dst.dtype)


def _side_cast_io(jobs, steps, step_of):
    in_specs, out_specs, out_shapes = [], [], []
    for w, splits in jobs:
        g, r, c = w.shape
        rb = g * r // steps
        per_group = r // rb
        in_specs.append(pl.BlockSpec(
            (None, rb, c), lambda *ids, pg=per_group: (step_of(*ids) // pg, step_of(*ids) % pg, 0)))
        out_specs.append(pl.BlockSpec(
            (None, splits, rb, c // splits),
            lambda *ids, pg=per_group: (step_of(*ids) // pg, 0, step_of(*ids) % pg, 0)))
        out_shapes.append(jax.ShapeDtypeStruct((g, splits, r, c // splits), BF16))
    return in_specs, out_specs, out_shapes


def _inproj_body(x_ref, g_ref, w_ref, o_ref, *rest):
    qkv_refs, qkv_f32 = rest[:-1], rest[-1]
    h = _rms(x_ref[...], g_ref[...]).astype(BF16)
    for c0 in range(0, QKV_COLS, MXU_WIDTH):
        res = _dot(h, w_ref[:, c0:c0 + MXU_WIDTH])
        for t in range(MXU_WIDTH // LANES):
            qkv_f32[c0 // LANES + t] = res[:, t * LANES:(t + 1) * LANES]
    for (_, dilation), ref in zip(DILATED_PATTERNS, qkv_refs):
        for r in range(dilation):
            rows = pl.ds(r, TM_PROJ // dilation, stride=dilation)
            ref[r] = jnp.concatenate([qkv_f32[t, rows, :] for t in range(QKV_COLS // LANES)],
                                     axis=-1).astype(ref.dtype)
    for c0 in range(0, U_COLS, MXU_WIDTH):
        c1 = min(c0 + MXU_WIDTH, U_COLS)
        o_ref[:, c0:c1] = _dot(h, w_ref[:, QKV_COLS + c0:QKV_COLS + c1])


def _subseq_spec(dilation, cols):
    tiles = SEQ // TM_PROJ
    return pl.BlockSpec((None, dilation, TM_PROJ // dilation, cols), lambda i: (i // tiles, 0, i % tiles, 0))


def _inproj(x, g, w_all, layer):
    qkv_shapes = [jax.ShapeDtypeStruct((BATCH, d, SEQ // d, QKV_COLS), BF16) for _, d in DILATED_PATTERNS]
    return pl.pallas_call(
        _inproj_body,
        grid=(TOKENS // TM_PROJ,),
        in_specs=[
            pl.BlockSpec((TM_PROJ, D_MODEL), lambda i: (i, 0)),
            pl.BlockSpec((1, D_MODEL), lambda i: (0, 0)),
            pl.BlockSpec((None, D_MODEL, QKV_COLS + U_COLS), lambda i: (layer, 0, 0)),
        ],
        out_specs=[pl.BlockSpec((TM_PROJ, U_COLS), lambda i: (i, 0))]
                  + [_subseq_spec(d, QKV_COLS) for _, d in DILATED_PATTERNS],
        out_shape=[jax.ShapeDtypeStruct((TOKENS, U_COLS), F32)] + qkv_shapes,
        scratch_shapes=[pltpu.VMEM((QKV_COLS // LANES, TM_PROJ, LANES), F32)],
        compiler_params=_params(("parallel",)),
        name="inproj",
    )(x, g, w_all)


def _rel_bucket(dist):
    max_exact = REL_BUCKETS // 2
    d = jnp.maximum(dist, 0)
    log_ratio = jnp.log(jnp.maximum(d, 1).astype(F32) / max_exact) / math.log(REL_MAX_DISTANCE / max_exact)
    large = jnp.minimum(max_exact + (log_ratio * (REL_BUCKETS - max_exact)).astype(jnp.int32), REL_BUCKETS - 1)
    return jnp.where(d < max_exact, d, large)


def _bucket_table(window, dilation):
    span = window // dilation
    qi = jnp.arange(ATT_BLOCK)[:, None]
    kj = jnp.arange(2 * ATT_BLOCK)[None, :]
    sub_dist = qi - kj + ATT_BLOCK
    band = (sub_dist >= 0) & (sub_dist <= span)
    return jnp.where(band, _rel_bucket(sub_dist * dilation), -1).astype(jnp.int32)


def _attn_body(rb_ref, bidx_ref, qkv_ref, *rest, sub_blocks, unroll):
    n_cast = (len(rest) - 3) // 2
    o_ref, lse_ref = rest[n_cast:n_cast + 2]
    bias_ref = rest[-1]
    _side_cast(rest[:n_cast], rest[n_cast + 2:-1])
    nblk = SEQ // ATT_BLOCK

    @pl.when(pl.program_id(0) == 0)
    def _():
        bidx = bidx_ref[...]
        in_prev = lax.broadcasted_iota(jnp.int32, bidx.shape, 1) < ATT_BLOCK
        for h in range(ATT_HEADS):
            acc = jnp.full(bidx.shape, NEG_BIG, F32)
            for b in range(REL_BUCKETS):
                acc = jnp.where(bidx == b, rb_ref[b, h], acc)
            bias_ref[0, h] = acc
            bias_ref[1, h] = jnp.where(in_prev, NEG_BIG, acc)
            bias_ref[2, h] = jnp.concatenate([acc[:, ATT_BLOCK:], jnp.full_like(acc[:, ATT_BLOCK:], NEG_BIG)], axis=1)

    def block(n, first):
        if first:
            rows, krows, variant = slice(0, ATT_BLOCK), slice(0, 2 * ATT_BLOCK), 2
        else:
            r0 = pl.multiple_of(n * ATT_BLOCK, ATT_BLOCK)
            rows = pl.ds(r0, ATT_BLOCK)
            krows = pl.ds(r0 - ATT_BLOCK, 2 * ATT_BLOCK)
            if sub_blocks == nblk:
                variant = 0
            elif sub_blocks == 1:
                variant = 1
            else:
                variant = jnp.where(n % sub_blocks == 0, 1, 0)
        q = qkv_ref[rows, 0:ATT_WIDTH]
        kk = qkv_ref[krows, ATT_WIDTH:2 * ATT_WIDTH]
        vv = qkv_ref[krows, 2 * ATT_WIDTH:3 * ATT_WIDTH]
        q = q * jnp.asarray(HEAD_DIM ** -0.5, BF16)
        head_of_lane = lax.broadcasted_iota(jnp.int32, (ATT_BLOCK, ATT_WIDTH), 1) // HEAD_DIM
        ones = jnp.ones((kk.shape[0], LANES), BF16)
        num = den = mx = None
        for h in range(ATT_HEADS):
            mine = head_of_lane == h
            bias = bias_ref[variant, h]
            s = _dot_nt(jnp.where(mine, q, jnp.zeros_like(q)), kk) + bias
            m = jnp.max(s, axis=-1, keepdims=True)
            p = jnp.exp(s - m).astype(BF16)
            num_h = _dot(p, vv)
            den_h = jnp.tile(_dot(p, ones), (1, ATT_WIDTH // LANES))
            m_h = jnp.broadcast_to(m, (ATT_BLOCK, ATT_WIDTH))
            num = num_h if h == 0 else jnp.where(mine, num_h, num)
            den = den_h if h == 0 else jnp.where(mine, den_h, den)
            mx = m_h if h == 0 else jnp.where(mine, m_h, mx)
        o_ref[rows, :] = (num / den).astype(o_ref.dtype)
        lse_ref[rows, :] = mx + jnp.log(den)

    block(0, True)

    def loop_body(n, carry):
        block(n, False)
        return carry
    lax.fori_loop(1, nblk, loop_body, 0, unroll=unroll)


def _attention_pattern(ua, rel_bias, window, dilation, side_casts=()):
    L = SEQ // dilation
    shape = (BATCH, dilation, L, ATT_WIDTH)
    qkv_spec = pl.BlockSpec((None, SEQ, QKV_COLS), lambda b: (b, 0, 0))
    out_spec = pl.BlockSpec((None, SEQ, ATT_WIDTH), lambda b: (b, 0, 0))
    cast_in, cast_out, cast_shapes = _side_cast_io(side_casts, BATCH, lambda b: b)
    o, lse, *casts = pl.pallas_call(
        functools.partial(_attn_body, sub_blocks=L // ATT_BLOCK, unroll=ATT_UNROLL),
        grid=(BATCH,),
        in_specs=[
            pl.BlockSpec(memory_space=pltpu.SMEM),
            pl.BlockSpec((ATT_BLOCK, 2 * ATT_BLOCK), lambda b: (0, 0)),
            qkv_spec,
        ] + cast_in,
        out_specs=[out_spec, out_spec] + cast_out,
        out_shape=[jax.ShapeDtypeStruct((BATCH, SEQ, ATT_WIDTH), BF16),
                   jax.ShapeDtypeStruct((BATCH, SEQ, ATT_WIDTH), F32)] + cast_shapes,
        scratch_shapes=[pltpu.VMEM((3, ATT_HEADS, ATT_BLOCK, 2 * ATT_BLOCK), F32)],
        compiler_params=_params(("arbitrary",)),
        name=f"attn_d{dilation}",
    )(rel_bias, _bucket_table(window, dilation), ua.reshape(BATCH, SEQ, QKV_COLS), *[w for w, _ in side_casts])
    return (o.reshape(shape), lse.reshape(shape)), casts


def _attention(qkvs, rel_bias, side_casts=()):
    side_casts = list(side_casts) + [None] * (len(DILATED_PATTERNS) - len(side_casts))
    res = [_attention_pattern(ua, rel_bias, window, dilation, [] if w is None else [w])
           for ua, (window, dilation), w in zip(qkvs, DILATED_PATTERNS, side_casts)]
    return [r[0] for r in res], [c for r in res for c in r[1]]


def _short_conv(cin_ref, cb_ref, cc_ref, w_ref, o_ref, tail_ref):
    uu = cc_ref[...] * cin_ref[...]
    t = lax.broadcasted_iota(jnp.int32, uu.shape, 0)
    y = uu * w_ref[CONV_WIDTH - 1:CONV_WIDTH, :]
    for shift in range(1, CONV_WIDTH):
        prev = pltpu.roll(uu, shift, axis=0)
        for r in range(shift):
            prev = jnp.where(t == r, tail_ref[8 - shift + r:8 - shift + r + 1, :], prev)
        y = y + prev * w_ref[CONV_WIDTH - 1 - shift:CONV_WIDTH - shift, :]
    o_ref[...] = (cb_ref[...] * y).astype(o_ref.dtype)
    tail_ref[...] = uu[uu.shape[0] - 8:, :]


def _gla_body(u_ref, wg_ref, bg_ref, gn_ref, ctril_ref, cw_ref, *rest):
    cols = lambda col, width: u_ref.at[:, col:col + width]
    q_ref, k_ref = cols(COL_GQ, GLA_QK), cols(COL_GK, GLA_QK)
    v_ref, gr_ref, glr_ref = cols(COL_GV, GLA_V), cols(COL_GR, GLA_V), cols(COL_GLR, LANES)
    cin_ref, cb_ref, cc_ref = cols(COL_CIN, CONV_CH), cols(COL_CB, CONV_CH), cols(COL_CC, CONV_CH)
    _gla_tile(q_ref, k_ref, v_ref, gr_ref, glr_ref, wg_ref, bg_ref, gn_ref, ctril_ref,
              cin_ref, cb_ref, cc_ref, cw_ref, *rest)


def _gla_tile(q_ref, k_ref, v_ref, gr_ref, glr_ref, wg_ref, bg_ref, gn_ref, ctril_ref,
              cin_ref, cb_ref, cc_ref, cw_ref, *rest):
    n_cast = (len(rest) - 4) // 2
    o_ref, cv_ref = rest[n_cast:n_cast + 2]
    s_ref, tail_ref = rest[-2:]
    _side_cast(rest[:n_cast], rest[n_cast + 2:-2])

    @pl.when(pl.program_id(1) == 0)
    def _():
        s_ref[...] = jnp.zeros_like(s_ref)
        tail_ref[...] = jnp.zeros_like(tail_ref)

    _short_conv(cin_ref, cb_ref, cc_ref, cw_ref, cv_ref, tail_ref)

    C = GLA_CHUNK
    row = lax.broadcasted_iota(jnp.int32, (C, C), 0)
    col = lax.broadcasted_iota(jnp.int32, (C, C), 1)
    tril = row >= col
    n_chunks = TC_GLA // C

    xg = _dot3(glr_ref[...], wg_ref[...]) + bg_ref[...]
    la_all = (jnp.minimum(xg, 0.0) - jnp.log(1.0 + jnp.exp(-jnp.abs(xg)))) * (1.0 / 16.0)

    la_hi, la_lo = _split_bf16(la_all)
    span = ctril_ref.shape[0]
    cum_all = jnp.concatenate(
        [_dot(ctril_ref[...], la_hi[r0:r0 + span]) + _dot(ctril_ref[...], la_lo[r0:r0 + span])
         for r0 in range(0, TC_GLA, span)], axis=0)
    totals = jnp.concatenate([cum_all[(c + 1) * C - 1:(c + 1) * C, :] for c in range(n_chunks)]
                             + [jnp.zeros((LANES - n_chunks, GLA_QK), F32)], axis=0)
    decay_cols = jnp.exp(totals.T)

    for c in range(n_chunks):
        rows = slice(c * C, (c + 1) * C)
        cum = cum_all[rows]
        last = cum[C - 1:C, :]
        q = q_ref[rows, :]
        k = k_ref[rows, :]
        qs = q * (GLA_DK ** -0.5)
        qt = (qs * jnp.exp(cum)).astype(BF16)
        mid = cum[C // 2 - 1:C // 2, :]
        qm = (qs * jnp.exp(cum - mid)).astype(BF16)
        kt = (k * jnp.exp(mid - cum)).astype(BF16)
        kl_t = (k * jnp.exp(last - cum)).T.astype(BF16)
        for h in range(GLA_HEADS):
            sl = slice(h * GLA_DK, (h + 1) * GLA_DK)
            vs = slice(h * GLA_DV, (h + 1) * GLA_DV)
            vh = v_ref[rows, vs].astype(BF16)
            state = s_ref[h]
            st_hi, st_lo = _split_bf16(state)
            sc = jnp.where(tril, _dot_nt(qm[:, sl], kt[:, sl]), 0.0).astype(BF16)
            o = _dot(qt[:, sl], st_hi) + _dot(qt[:, sl], st_lo) + _dot(sc, vh)
            decay = jnp.broadcast_to(decay_cols[sl, c:c + 1], state.shape)
            s_ref[h] = decay * state + _dot(kl_t[sl, :], vh)
            g = gr_ref[rows, vs]
            o_ref[rows, vs] = (_rms(o, gn_ref[...]) * (g * _sigmoid(g))).astype(o_ref.dtype)


def _gla_conv(u, wg, bg, gn, conv_w, side_casts=()):
    nj = SEQ // TC_GLA
    row = lambda b, j: b * nj + j
    full = lambda a: pl.BlockSpec(a.shape, lambda b, j: (0, 0))
    cast_in, cast_out, cast_shapes = _side_cast_io(side_casts, BATCH * nj, row)
    t = np.arange(GLA_CUMSUM_SPAN)
    same_chunk = (t[:, None] // GLA_CHUNK) == (t[None, :] // GLA_CHUNK)
    chunk_tril = jnp.asarray(same_chunk & (t[:, None] >= t[None, :]), BF16)
    go, cv, *casts = pl.pallas_call(
        _gla_body,
        grid=(BATCH, nj),
        in_specs=[pl.BlockSpec((TC_GLA, U_COLS), lambda b, j: (row(b, j), 0)),
                  full(wg), full(bg), full(gn), full(chunk_tril), full(conv_w)] + cast_in,
        out_specs=[pl.BlockSpec((TC_GLA, GLA_V), lambda b, j: (row(b, j), 0)),
                   pl.BlockSpec((TC_GLA, CONV_CH), lambda b, j: (row(b, j), 0))] + cast_out,
        out_shape=[jax.ShapeDtypeStruct((TOKENS, GLA_V), BF16), jax.ShapeDtypeStruct((TOKENS, CONV_CH), BF16)]
                  + cast_shapes,
        scratch_shapes=[pltpu.VMEM((GLA_HEADS, GLA_DK, GLA_DV), F32), pltpu.VMEM((8, CONV_CH), F32)],
        compiler_params=_params(("arbitrary", "arbitrary")),
        name="gla_conv",
    )(u, wg, bg, gn, chunk_tril, conv_w, *[w for w, _ in side_casts])
    return go, cv, casts


def _outproj_body(*refs, route):
    (o1, o4, o16, l1, l4, l16, go_ref, cv_ref, x_ref, w_ref, g_ref) = refs[:11]
    perm_ref = refs[-1]
    if route:
        wr_ref, xo_ref, ho_ref, route_ref, counts_ref, carry_ref = refs[11:-1]
    else:
        xo_ref, ho_ref = refs[11:-1]

    def token_order(ref, slot):
        dilation, rows, _ = ref.shape
        if dilation == 1:
            return ref[0].astype(F32)
        tiles = range(ATT_WIDTH // LANES)
        for r in range(dilation):
            val = ref[r].astype(F32)
            for t in tiles:
                perm_ref[slot, t, pl.ds(r, rows, stride=dilation), :] = val[:, t * LANES:(t + 1) * LANES]
        return jnp.concatenate([perm_ref[slot, t] for t in tiles], axis=-1)

    la, lb, lc = token_order(l1, 0), token_order(l4, 0), token_order(l16, 1)
    oa, ob, oc = token_order(o1, 0), token_order(o4, 2), token_order(o16, 3)
    m = jnp.maximum(jnp.maximum(la, lb), lc)
    ea, eb, ec = jnp.exp(la - m), jnp.exp(lb - m), jnp.exp(lc - m)
    att = (ea * oa + eb * ob + ec * oc) / (ea + eb + ec)
    y = (x_ref[...]
         + _dot(att.astype(BF16), w_ref[0:ATT_WIDTH, :])
         + _dot(go_ref[...], w_ref[ATT_WIDTH:ATT_WIDTH + GLA_V, :])
         + _dot(cv_ref[...], w_ref[ATT_WIDTH + GLA_V:MIX_WIDTH, :]))
    xo_ref[...] = y
    hf = _rms(y, g_ref[...])
    ho_ref[...] = hf.astype(ho_ref.dtype)
    if route:
        @pl.when(pl.program_id(0) == 0)
        def _():
            carry_ref[...] = jnp.zeros_like(carry_ref)

        tm = hf.shape[0]
        ne = N_EXPERTS
        hf_hi, hf_lo = _split_bf16(hf)
        part = _dot_nt(wr_ref[0], hf_hi) + _dot_nt(wr_ref[1], hf_lo)
        logits = part[0:ne] + part[ne:2 * ne]
        eidx = lax.broadcasted_iota(jnp.int32, logits.shape, 0).astype(F32)
        v1 = jnp.max(logits, axis=0, keepdims=True)
        i1 = jnp.min(jnp.where(logits == v1, eidx, float(ne)), axis=0, keepdims=True)
        lg2 = jnp.where(eidx == i1, -jnp.inf, logits)
        v2 = jnp.max(lg2, axis=0, keepdims=True)
        i2 = jnp.min(jnp.where(lg2 == v2, eidx, float(ne)), axis=0, keepdims=True)
        e2 = jnp.exp(v2 - v1)
        w1 = 1.0 / (1.0 + e2)
        w2 = e2 * w1
        sel1 = eidx == i1
        sel2 = eidx == i2
        onehot = jnp.where(sel1, 1.0, jnp.where(sel2, 1.0, 0.0))
        tri = (lax.broadcasted_iota(jnp.int32, (tm, tm), 0) <= lax.broadcasted_iota(jnp.int32, (tm, tm), 1))
        onehot16 = jnp.concatenate([onehot, jnp.zeros_like(onehot)], axis=0).astype(BF16)
        csum = _dot(onehot16, jnp.where(tri, 1.0, 0.0).astype(BF16))[0:ne]
        carry = carry_ref[:, 0:1]
        rank = csum - onehot + carry
        r1 = jnp.sum(jnp.where(sel1, rank, 0.0), axis=0, keepdims=True)
        r2 = jnp.sum(jnp.where(sel2, rank, 0.0), axis=0, keepdims=True)
        total = jnp.broadcast_to(carry + csum[:, tm - 1:tm], carry_ref.shape)
        carry_ref[...] = total
        counts_ref[...] = total
        rows = {ROUTE_I1: i1, ROUTE_I2: i2, ROUTE_W1: w1, ROUTE_W2: w2, ROUTE_R1: r1, ROUTE_R2: r2}
        zero = jnp.zeros_like(i1)
        route_ref[...] = jnp.concatenate([rows.get(r, zero) for r in range(8)], axis=0)


def _outproj(att, go, cv, x, w, g, w_router=None):
    route = w_router is not None
    tm = TM_PROJ
    tile = lambda cols: pl.BlockSpec((tm, cols), lambda i: (i, 0))
    full = lambda a: pl.BlockSpec(a.shape, lambda i: (0, 0))
    (o1, l1), (o4, l4), (o16, l16) = att
    args = [o1, o4, o16, l1, l4, l16, go, cv, x, w, g]
    att_specs = [_subseq_spec(d, ATT_WIDTH) for _, d in DILATED_PATTERNS]
    in_specs = att_specs * 2 + [tile(GLA_V), tile(CONV_CH), tile(D_MODEL), full(w), full(g)]
    out_specs = [tile(D_MODEL), tile(D_MODEL)]
    out_shape = [jax.ShapeDtypeStruct((TOKENS, D_MODEL), F32),
                 jax.ShapeDtypeStruct((TOKENS, D_MODEL), F32 if route else BF16)]
    scratch = []
    if route:
        args.append(w_router)
        in_specs.append(pl.BlockSpec(w_router.shape, lambda i: (0, 0, 0)))
        out_specs += [pl.BlockSpec((8, tm), lambda i: (0, i)), pl.BlockSpec((N_EXPERTS, LANES), lambda i: (0, 0))]
        out_shape += [jax.ShapeDtypeStruct((8, TOKENS), F32), jax.ShapeDtypeStruct((N_EXPERTS, LANES), F32)]
        scratch = [pltpu.VMEM((N_EXPERTS, LANES), F32)]
    scratch.append(pltpu.VMEM((4, ATT_WIDTH // LANES, tm, LANES), F32))
    return pl.pallas_call(
        functools.partial(_outproj_body, route=route),
        grid=(TOKENS // tm,),
        in_specs=in_specs,
        out_specs=out_specs,
        out_shape=out_shape,
        scratch_shapes=scratch,
        compiler_params=_params(("arbitrary",) if route else ("parallel",)),
        name="outproj_route" if route else "outproj",
    )(*args)


def _swiglu_accumulate(h, w1_ref, w3_ref, w2_ref, acc_ref, tf):
    for c0 in range(0, tf, FFN_SUB):
        c1 = min(c0 + FFN_SUB, tf)
        a = _dot(h, w1_ref[:, c0:c1])
        b = _dot(h, w3_ref[:, c0:c1])
        act = a * _sigmoid(a) * b
        acc_ref[...] += _dot(act.astype(BF16), w2_ref[c0:c1, :])


def _ffn_body(x_ref, h_ref, w1_ref, w3_ref, w2_ref, *rest, tf):
    n_cast = (len(rest) - 1) // 2
    o_ref = rest[n_cast]

    @pl.when(pl.program_id(1) == 0)
    def _():
        o_ref[...] = x_ref[...]

    _swiglu_accumulate(h_ref[...], w1_ref, w3_ref, w2_ref, o_ref, tf)
    _side_cast(rest[:n_cast], rest[n_cast + 1:])


def _ffn(x, h, w1, w3, w2, *, side_casts=()):
    nj, _, tf = w1.shape
    tm = TM_FFN
    tile = lambda cols: pl.BlockSpec((tm, cols), lambda i, j: (i, 0))
    cast_in, cast_out, cast_shapes = _side_cast_io(side_casts, (TOKENS // tm) * nj, lambda i, j: i * nj + j)
    resident = dict(pipeline_mode=pl.Buffered(1)) if nj == 1 else {}
    res = pl.pallas_call(
        functools.partial(_ffn_body, tf=tf),
        grid=(TOKENS // tm, nj),
        in_specs=[tile(D_MODEL), tile(D_MODEL),
                  pl.BlockSpec((None, D_MODEL, tf), lambda i, j: (j, 0, 0), **resident),
                  pl.BlockSpec((None, D_MODEL, tf), lambda i, j: (j, 0, 0), **resident),
                  pl.BlockSpec((tf, D_MODEL), lambda i, j: (j, 0), **resident)] + cast_in,
        out_specs=[tile(D_MODEL)] + cast_out,
        out_shape=[jax.ShapeDtypeStruct((TOKENS, D_MODEL), F32)] + cast_shapes,
        compiler_params=_params(("parallel", "arbitrary")),
        name="dense_ffn",
    )(x, h, w1, w3, w2, *[w for w, _ in side_casts])
    return res[0], res[1:]


ROW_TILE = D_MODEL // LANES


def _to_row_tiled(dst_ref, lead, val):
    rows = val.shape[0]
    for s in range(ROW_TILE):
        dst_ref[(*lead, pl.ds(s, rows, stride=ROW_TILE), slice(None))] = val[:, s * LANES:(s + 1) * LANES]


def _from_row_tiled(src_ref, lead, rows):
    return jnp.concatenate([src_ref[(*lead, pl.ds(s, rows, stride=ROW_TILE), slice(None))]
                            for s in range(ROW_TILE)], axis=-1)


def _row_tile(idx):
    return pl.ds(pl.multiple_of(idx * ROW_TILE, ROW_TILE), ROW_TILE)


def _dispatch_body(pos_ref, last_tile_ref, h_ref, xs_hbm, stage_ref, zero_ref, sem, zero_sem):
    i = pl.program_id(0)
    n = pl.num_programs(0)
    tm = DISPATCH_CHUNK
    slot = i % 2

    def drain(s):
        for _ in range(2):
            pltpu.make_async_copy(stage_ref.at[s], xs_hbm.at[pl.ds(0, tm * ROW_TILE)], sem.at[s]).wait()

    @pl.when(i == 0)
    def _():
        zero_ref[...] = jnp.zeros_like(zero_ref)

        def zero_copy(e):
            start = pl.multiple_of(last_tile_ref[e] * (TM_MOE * ROW_TILE), TM_MOE * ROW_TILE)
            return pltpu.make_async_copy(zero_ref, xs_hbm.at[pl.ds(start, TM_MOE * ROW_TILE)], zero_sem)

        for e in range(2 * N_EXPERTS):
            @pl.when(last_tile_ref[e] >= 0)
            def _():
                zero_copy(e).start()
        for e in range(2 * N_EXPERTS):
            @pl.when(last_tile_ref[e] >= 0)
            def _():
                zero_copy(e).wait()

    @pl.when(i >= 2)
    def _():
        drain(slot)

    _to_row_tiled(stage_ref, (slot,), h_ref[...])

    def body(t, carry):
        for k in range(2):
            dst = pos_ref[2 * (i * tm + t) + k]
            pltpu.make_async_copy(stage_ref.at[slot, _row_tile(t)], xs_hbm.at[_row_tile(dst)],
                                  sem.at[slot]).start(priority=k)
        return carry
    lax.fori_loop(0, tm, body, 0, unroll=8)

    @pl.when(i == n - 1)
    def _():
        drain(1 - slot)
        drain(slot)


def _dispatch(pos, last_tile, h):
    tm = DISPATCH_CHUNK
    return pl.pallas_call(
        _dispatch_body,
        grid=(TOKENS // tm,),
        in_specs=[pl.BlockSpec(memory_space=pltpu.SMEM),
                  pl.BlockSpec(memory_space=pltpu.SMEM),
                  pl.BlockSpec((tm, D_MODEL), lambda i: (i, 0))],
        out_specs=pl.BlockSpec(memory_space=pl.ANY),
        out_shape=jax.ShapeDtypeStruct((N_SORTED * ROW_TILE, LANES), F32),
        scratch_shapes=[pltpu.VMEM((2, tm * ROW_TILE, LANES), F32),
                        pltpu.VMEM((TM_MOE * ROW_TILE, LANES), F32),
                        pltpu.SemaphoreType.DMA((2,)), pltpu.SemaphoreType.DMA(())],
        compiler_params=_params(("arbitrary",)),
        name="moe_dispatch",
    )(pos, last_tile, h)


def _gffn_body(te_ref, nu_ref, xs_ref, w1_ref, w3_ref, w2_ref, o_ref, hb_ref, acc_ref):
    del te_ref
    i = pl.program_id(0)
    j = pl.program_id(1)

    used = i < nu_ref[0]

    @pl.when(j == 0)
    def _():
        acc_ref[...] = jnp.zeros_like(acc_ref)

    @pl.when(used & (j == 0))
    def _():
        hb_ref[...] = _from_row_tiled(xs_ref, (), TM_MOE).astype(BF16)

    @pl.when(used)
    def _():
        _swiglu_accumulate(hb_ref[...], w1_ref, w3_ref, w2_ref, acc_ref, TF_MOE)

    @pl.when(j == pl.num_programs(1) - 1)
    def _():
        _to_row_tiled(o_ref, (), acc_ref[...])


def _grouped_ffn(tile_expert, n_used, xs, w1, w3, w2):
    nj = FFN_EXPERT // TF_MOE
    col = lambda i, j, nu: jnp.where(i < nu[0], j, nj - 1)
    grid_spec = pltpu.PrefetchScalarGridSpec(
        num_scalar_prefetch=2,
        grid=(N_TILES_MOE, nj),
        in_specs=[
            pl.BlockSpec((TM_MOE * ROW_TILE, LANES), lambda i, j, te, nu: (jnp.minimum(i, nu[0] - 1), 0)),
            pl.BlockSpec((None, None, D_MODEL, TF_MOE), lambda i, j, te, nu: (te[i], col(i, j, nu), 0, 0)),
            pl.BlockSpec((None, None, D_MODEL, TF_MOE), lambda i, j, te, nu: (te[i], col(i, j, nu), 0, 0)),
            pl.BlockSpec((None, TF_MOE, D_MODEL), lambda i, j, te, nu: (te[i], col(i, j, nu), 0)),
        ],
        out_specs=pl.BlockSpec((TM_MOE * ROW_TILE, LANES), lambda i, j, te, nu: (i, 0)),
        scratch_shapes=[pltpu.VMEM((TM_MOE, D_MODEL), BF16), pltpu.VMEM((TM_MOE, D_MODEL), F32)],
    )
    return pl.pallas_call(
        _gffn_body,
        grid_spec=grid_spec,
        out_shape=jax.ShapeDtypeStruct((N_SORTED * ROW_TILE, LANES), F32),
        compiler_params=_params(("arbitrary", "arbitrary")),
        name="moe_ffn",
    )(tile_expert, n_used, xs, w1, w3, w2)


def _combine_body(pos_ref, x_ref, gate_ref, g_ref, ys_hbm, o_ref, buf_ref, sem):
    i = pl.program_id(0)
    n = pl.num_programs(0)
    tm = TM_COMBINE

    def issue(tile, slot):
        def body(t, carry):
            for k in range(2):
                src = pos_ref[2 * (tile * tm + t) + k]
                pltpu.make_async_copy(ys_hbm.at[_row_tile(src)], buf_ref.at[slot, k, _row_tile(t)],
                                      sem.at[slot]).start(priority=k)
            return carry
        lax.fori_loop(0, tm, body, 0, unroll=8)

    @pl.when(i == 0)
    def _():
        issue(0, 0)

    @pl.when(i + 1 < n)
    def _():
        issue(i + 1, (i + 1) % 2)

    slot = i % 2
    for k in range(2):
        pltpu.make_async_copy(ys_hbm.at[pl.ds(0, tm * ROW_TILE)], buf_ref.at[slot, k], sem.at[slot]).wait()
    w1 = gate_ref[:, 0:1]
    w2 = gate_ref[:, 1:2]
    y = x_ref[...] + w1 * _from_row_tiled(buf_ref, (slot, 0), tm) + w2 * _from_row_tiled(buf_ref, (slot, 1), tm)
    o_ref[...] = _rms(y, g_ref[...])


def _combine(pos, x, gates, g, ys):
    tm = TM_COMBINE
    return pl.pallas_call(
        _combine_body,
        grid=(TOKENS // tm,),
        in_specs=[pl.BlockSpec(memory_space=pltpu.SMEM),
                  pl.BlockSpec((tm, D_MODEL), lambda i: (i, 0)),
                  pl.BlockSpec((tm, 2), lambda i: (i, 0)),
                  pl.BlockSpec((1, D_MODEL), lambda i: (0, 0)),
                  pl.BlockSpec(memory_space=pl.ANY)],
        out_specs=pl.BlockSpec((tm, D_MODEL), lambda i: (i, 0)),
        out_shape=jax.ShapeDtypeStruct((TOKENS, D_MODEL), F32),
        scratch_shapes=[pltpu.VMEM((2, 2, tm * ROW_TILE, LANES), F32), pltpu.SemaphoreType.DMA((2,))],
        compiler_params=_params(("arbitrary",)),
        name="moe_combine",
    )(pos, x, gates, g, ys)


def _routing_tables(route, counts):
    cnt = counts[:, 0].astype(jnp.int32)
    tiles = (cnt + TM_MOE - 1) // TM_MOE
    tile_end = jnp.cumsum(tiles)
    tile_start = tile_end - tiles
    n_used = tile_end[-1]
    expert = route[ROUTE_I1:ROUTE_I2 + 1].astype(jnp.int32)
    rank = route[ROUTE_R1:ROUTE_R2 + 1].astype(jnp.int32)
    group_start = jnp.sum(jnp.where(expert[..., None] == jnp.arange(N_EXPERTS), tile_start * TM_MOE, 0), axis=-1)
    pos = (group_start + rank).T
    tile_id = jnp.minimum(jnp.arange(N_TILES_MOE, dtype=jnp.int32), n_used - 1)
    tile_expert = jnp.sum(tile_id[:, None] >= tile_end[None, :], axis=1).astype(jnp.int32)
    last_tile = jnp.where(tiles > 0, tile_end - 1, -1)
    spare = n_used + jnp.arange(N_EXPERTS)
    zero_tiles = jnp.concatenate([last_tile, jnp.where(spare < N_TILES_MOE, spare, -1)]).astype(jnp.int32)
    return pos.reshape(2 * TOKENS), tile_expert, n_used.reshape(1), zero_tiles


def _moe(x, h, route, counts, w1, w3, w2, g_final):
    pos, tile_expert, n_used, last_tile = _routing_tables(route, counts)
    xs = _dispatch(pos, last_tile, h)
    ys = _grouped_ffn(tile_expert, n_used, xs, w1, w3, w2)
    return _combine(pos, x, route[ROUTE_W1:ROUTE_W2 + 1].T, g_final, ys)


def _prep_w_in(w):
    aq, ak, av, gq, gk, gv, gr, glr, c_in, c_b, c_c = jnp.split(w, np.cumsum(SPLIT_SIZES)[:-1].tolist(), axis=2)
    pad = jnp.zeros(w.shape[:2] + (LANES - GLA_RANK,), w.dtype)
    return jnp.concatenate([aq, ak, av, gv, gr, gq, gk, c_in, c_b, c_c, glr, pad], axis=2).astype(BF16)


def _prep_router(w):
    wt = w.T
    hi = wt.astype(BF16)
    lo = (wt - hi.astype(F32)).astype(BF16)
    return jnp.stack([jnp.concatenate([hi, lo]), jnp.concatenate([hi, jnp.zeros_like(hi)])])


def kernel(x, w_mix_in, w_mix_out, g_mix, rel_bias, gla_w_gate, gla_b_gate, gla_g_norm, conv_w,
           g_ffn, ffn_w1, ffn_w3, ffn_w2, moe_router, moe_w1, moe_w3, moe_w2, g_final):
    assert DEPTH == 2
    x = x.reshape(TOKENS, D_MODEL)
    up_job = lambda w, tf: (w, w.shape[-1] // tf)
    down = lambda c: c.reshape(c.shape[0], c.shape[2], c.shape[3])
    experts = {}
    w_in = _prep_w_in(w_mix_in)
    for layer in range(DEPTH):
        u, *qkvs = _inproj(x, g_mix[layer].reshape(1, D_MODEL), w_in, layer)
        dense_jobs = [up_job(ffn_w1[:1], TF_DENSE), up_job(ffn_w3[:1], TF_DENSE), (ffn_w2[:1], 1)] if layer == 0 else []
        att, dense_bf16 = _attention(qkvs, rel_bias, dense_jobs)
        if layer == 0:
            ffn_bf16 = (dense_bf16[0][0], dense_bf16[1][0], down(dense_bf16[2])[0])
        wg = jnp.pad(gla_w_gate[layer], ((0, LANES - GLA_RANK), (0, 0)))
        go, cv, casts = _gla_conv(u, wg, gla_b_gate[layer].reshape(1, GLA_QK),
                                  gla_g_norm[layer].reshape(1, GLA_DV),
                                  jnp.pad(conv_w[layer], ((0, 8 - CONV_WIDTH), (0, 0))),
                                  side_casts=[up_job(moe_w1[0], TF_MOE)] if layer == 0 else [])
        if layer == 0:
            experts["w1"] = casts[0]
        w_out = w_mix_out[layer].astype(BF16)
        g2 = g_ffn[layer].reshape(1, D_MODEL)
        i = layer // 2
        if layer % 2 == 0:
            x, h = _outproj(att, go, cv, x, w_out, g2)
            x, (experts["w3"], cast) = _ffn(x, h, *ffn_bf16, side_casts=[up_job(moe_w3[i], TF_MOE), (moe_w2[i], 1)])
            experts["w2"] = down(cast)
        else:
            x, h, route, counts = _outproj(att, go, cv, x, w_out, g2, _prep_router(moe_router[i]))
            x = _moe(x, h, route, counts, experts["w1"], experts["w3"], experts["w2"], g_final.reshape(1, D_MODEL))
    return x.reshape(BATCH, SEQ, D_MODEL)
```

```python
import functools
import math

import jax
import jax.numpy as jnp
import numpy as np
from jax import lax
from jax.experimental import pallas as pl
from jax.experimental.pallas import tpu as pltpu

F32 = jnp.float32
BF16 = jnp.bfloat16

D_MODEL = 1024
BATCH = 8
SEQ = 2048
TOKENS = BATCH * SEQ
DEPTH = 2
EPS = 1e-6

HEAD_DIM = 64
ATT_HEADS = 4
ATT_WIDTH = ATT_HEADS * HEAD_DIM
DILATED_PATTERNS = ((128, 1), (512, 4), (2048, 16))
ATT_BLOCK = 128
REL_BUCKETS = 32
REL_MAX_DISTANCE = 2048

GLA_HEADS = 4
GLA_DK = 64
GLA_DV = 128
GLA_RANK = 16
GLA_CHUNK = 64
GLA_QK = GLA_HEADS * GLA_DK
GLA_V = GLA_HEADS * GLA_DV

CONV_CH = 256
CONV_WIDTH = 3
MIX_WIDTH = ATT_WIDTH + GLA_V + CONV_CH

SPLIT_SIZES = (ATT_WIDTH, ATT_WIDTH, ATT_WIDTH, GLA_QK, GLA_QK, GLA_V, GLA_V, GLA_RANK,
               CONV_CH, CONV_CH, CONV_CH)

FFN_DENSE = 2816
N_EXPERTS = 8
FFN_EXPERT = 3584

LANES = 128
MXU_WIDTH = 256
VMEM_LIMIT = 56 * 1024 * 1024

QKV_COLS = 3 * ATT_WIDTH
COL_GV, COL_GR, COL_GQ, COL_GK = 0, 512, 1024, 1280
COL_CIN, COL_CB, COL_CC, COL_GLR = 1536, 1792, 2048, 2304
U_COLS = COL_GLR + LANES

NEG_BIG = -1e30

ATT_UNROLL = 15
TM_PROJ = 512
TC_GLA = 512
GLA_CUMSUM_SPAN = 256
TM_FFN = 512
TF_DENSE = 2816
FFN_SUB = 256
TM_MOE = 512
TF_MOE = 1792
N_TILES_MOE = 2 * TOKENS // TM_MOE + N_EXPERTS
N_SORTED = N_TILES_MOE * TM_MOE
DISPATCH_CHUNK = 512
TM_COMBINE = 512

ROUTE_I1, ROUTE_I2, ROUTE_W1, ROUTE_W2, ROUTE_R1, ROUTE_R2 = range(6)


def _params(sem):
    return pltpu.CompilerParams(dimension_semantics=sem, vmem_limit_bytes=VMEM_LIMIT)


def _split_bf16(a):
    hi = a.astype(BF16)
    lo = (a - hi.astype(F32)).astype(BF16)
    return hi, lo


def _dot(a, b):
    return jnp.dot(a, b, preferred_element_type=F32)


def _dot3(a, b):
    a_hi, a_lo = _split_bf16(a)
    b_hi, b_lo = _split_bf16(b)
    return _dot(a_hi, b_hi) + _dot(a_lo, b_hi) + _dot(a_hi, b_lo)


def _dot_nt(a, b):
    return lax.dot_general(a, b, (((1,), (1,)), ((), ())), preferred_element_type=F32)


def _rms(x, g):
    ms = jnp.mean(x * x, axis=-1, keepdims=True)
    return x * lax.rsqrt(ms + EPS) * g


def _sigmoid(x):
    return 1.0 / (1.0 + jnp.exp(-x))


def _side_cast(srcs, dsts):
    for src, dst in zip(srcs, dsts):
        width = dst.shape[-1]
        for s in range(dst.shape[0]):
            dst[s] = src[:, s * width:(s + 1) * width].astype(dst.dtype)


def _side_cast_io(jobs, steps, step_of):
    in_specs, out_specs, out_shapes = [], [], []
    for w, splits in jobs:
        g, r, c = w.shape
        rb = g * r // steps
        per_group = r // rb
        in_specs.append(pl.BlockSpec(
            (None, rb, c), lambda *ids, pg=per_group: (step_of(*ids) // pg, step_of(*ids) % pg, 0)))
        out_specs.append(pl.BlockSpec(
            (None, splits, rb, c // splits),
            lambda *ids, pg=per_group: (step_of(*ids) // pg, 0, step_of(*ids) % pg, 0)))
        out_shapes.append(jax.ShapeDtypeStruct((g, splits, r, c // splits), BF16))
    return in_specs, out_specs, out_shapes


def _inproj_body(x_ref, g_ref, w_ref, *rest):
    n_pat = len(DILATED_PATTERNS)
    n_cast = (len(rest) - n_pat - 2) // 2
    o_ref = rest[n_cast]
    qkv_refs = rest[n_cast + 1:n_cast + 1 + n_pat]
    qkv_f32 = rest[-1]
    _side_cast(rest[:n_cast], rest[n_cast + 1 + n_pat:-1])
    h = _rms(x_ref[...], g_ref[...]).astype(BF16)
    for c0 in range(0, QKV_COLS, MXU_WIDTH):
        res = _dot(h, w_ref[:, c0:c0 + MXU_WIDTH])
        for t in range(MXU_WIDTH // LANES):
            qkv_f32[c0 // LANES + t] = res[:, t * LANES:(t + 1) * LANES]
    for (_, dilation), ref in zip(DILATED_PATTERNS, qkv_refs):
        for r in range(dilation):
            rows = pl.ds(r, TM_PROJ // dilation, stride=dilation)
            ref[r] = jnp.concatenate([qkv_f32[t, rows, :] for t in range(QKV_COLS // LANES)],
                                     axis=-1).astype(ref.dtype)
    for c0 in range(0, U_COLS, MXU_WIDTH):
        c1 = min(c0 + MXU_WIDTH, U_COLS)
        o_ref[:, c0:c1] = _dot(h, w_ref[:, QKV_COLS + c0:QKV_COLS + c1])


def _subseq_spec(dilation, cols):
    tiles = SEQ // TM_PROJ
    return pl.BlockSpec((None, dilation, TM_PROJ // dilation, cols), lambda i: (i // tiles, 0, i % tiles, 0))


def _inproj(x, g, w_all, layer, side_casts=()):
    n_pat = len(DILATED_PATTERNS)
    qkv_shapes = [jax.ShapeDtypeStruct((BATCH, d, SEQ // d, QKV_COLS), BF16) for _, d in DILATED_PATTERNS]
    cast_in, cast_out, cast_shapes = _side_cast_io(side_casts, TOKENS // TM_PROJ, lambda i: i)
    u, *rest = pl.pallas_call(
        _inproj_body,
        grid=(TOKENS // TM_PROJ,),
        in_specs=[
            pl.BlockSpec((TM_PROJ, D_MODEL), lambda i: (i, 0)),
            pl.BlockSpec((1, D_MODEL), lambda i: (0, 0)),
            pl.BlockSpec((None, D_MODEL, QKV_COLS + U_COLS), lambda i: (layer, 0, 0)),
        ] + cast_in,
        out_specs=[pl.BlockSpec((TM_PROJ, U_COLS), lambda i: (i, 0))]
                  + [_subseq_spec(d, QKV_COLS) for _, d in DILATED_PATTERNS] + cast_out,
        out_shape=[jax.ShapeDtypeStruct((TOKENS, U_COLS), F32)] + qkv_shapes + cast_shapes,
        scratch_shapes=[pltpu.VMEM((QKV_COLS // LANES, TM_PROJ, LANES), F32)],
        compiler_params=_params(("parallel",)),
        name="inproj",
    )(x, g, w_all, *[w for w, _ in side_casts])
    return u, rest[:n_pat], rest[n_pat:]


def _rel_bucket(dist):
    max_exact = REL_BUCKETS // 2
    d = jnp.maximum(dist, 0)
    log_ratio = jnp.log(jnp.maximum(d, 1).astype(F32) / max_exact) / math.log(REL_MAX_DISTANCE / max_exact)
    large = jnp.minimum(max_exact + (log_ratio * (REL_BUCKETS - max_exact)).astype(jnp.int32), REL_BUCKETS - 1)
    return jnp.where(d < max_exact, d, large)


def _bucket_table(window, dilation):
    span = window // dilation
    qi = jnp.arange(ATT_BLOCK)[:, None]
    kj = jnp.arange(2 * ATT_BLOCK)[None, :]
    sub_dist = qi - kj + ATT_BLOCK
    band = (sub_dist >= 0) & (sub_dist <= span)
    return jnp.where(band, _rel_bucket(sub_dist * dilation), -1).astype(jnp.int32)


def _attn_body(rb_ref, bidx_ref, qkv_ref, *rest, sub_blocks, unroll):
    n_cast = (len(rest) - 3) // 2
    o_ref, lse_ref = rest[n_cast:n_cast + 2]
    bias_ref = rest[-1]
    _side_cast(rest[:n_cast], rest[n_cast + 2:-1])
    nblk = SEQ // ATT_BLOCK

    @pl.when(pl.program_id(0) == 0)
    def _():
        bidx = bidx_ref[...]
        in_prev = lax.broadcasted_iota(jnp.int32, bidx.shape, 1) < ATT_BLOCK
        for h in range(ATT_HEADS):
            acc = jnp.full(bidx.shape, NEG_BIG, F32)
            for b in range(REL_BUCKETS):
                acc = jnp.where(bidx == b, rb_ref[b, h], acc)
            bias_ref[0, h] = acc
            bias_ref[1, h] = jnp.where(in_prev, NEG_BIG, acc)
            bias_ref[2, h] = jnp.concatenate([acc[:, ATT_BLOCK:], jnp.full_like(acc[:, ATT_BLOCK:], NEG_BIG)], axis=1)

    def block(n, first):
        if first:
            rows, krows, variant = slice(0, ATT_BLOCK), slice(0, 2 * ATT_BLOCK), 2
        else:
            r0 = pl.multiple_of(n * ATT_BLOCK, ATT_BLOCK)
            rows = pl.ds(r0, ATT_BLOCK)
            krows = pl.ds(r0 - ATT_BLOCK, 2 * ATT_BLOCK)
            if sub_blocks == nblk:
                variant = 0
            elif sub_blocks == 1:
                variant = 1
            else:
                variant = jnp.where(n % sub_blocks == 0, 1, 0)
        q = qkv_ref[rows, 0:ATT_WIDTH]
        kk = qkv_ref[krows, ATT_WIDTH:2 * ATT_WIDTH]
        vv = qkv_ref[krows, 2 * ATT_WIDTH:3 * ATT_WIDTH]
        q = q * jnp.asarray(HEAD_DIM ** -0.5, BF16)
        head_of_lane = lax.broadcasted_iota(jnp.int32, (ATT_BLOCK, ATT_WIDTH), 1) // HEAD_DIM
        ones = jnp.ones((kk.shape[0], LANES), BF16)
        num = den = mx = None
        for h in range(ATT_HEADS):
            mine = head_of_lane == h
            bias = bias_ref[variant, h]
            s = _dot_nt(jnp.where(mine, q, jnp.zeros_like(q)), kk) + bias
            m = jnp.max(s, axis=-1, keepdims=True)
            p = jnp.exp(s - m).astype(BF16)
            num_h = _dot(p, vv)
            den_h = jnp.tile(_dot(p, ones), (1, ATT_WIDTH // LANES))
            m_h = jnp.broadcast_to(m, (ATT_BLOCK, ATT_WIDTH))
            num = num_h if h == 0 else jnp.where(mine, num_h, num)
            den = den_h if h == 0 else jnp.where(mine, den_h, den)
            mx = m_h if h == 0 else jnp.where(mine, m_h, mx)
        o_ref[rows, :] = (num / den).astype(o_ref.dtype)
        lse_ref[rows, :] = mx + jnp.log(den)

    block(0, True)

    def loop_body(n, carry):
        block(n, False)
        return carry
    lax.fori_loop(1, nblk, loop_body, 0, unroll=unroll)


def _attention_pattern(ua, rel_bias, window, dilation, side_casts=()):
    L = SEQ // dilation
    shape = (BATCH, dilation, L, ATT_WIDTH)
    qkv_spec = pl.BlockSpec((None, SEQ, QKV_COLS), lambda b: (b, 0, 0))
    out_spec = pl.BlockSpec((None, SEQ, ATT_WIDTH), lambda b: (b, 0, 0))
    cast_in, cast_out, cast_shapes = _side_cast_io(side_casts, BATCH, lambda b: b)
    o, lse, *casts = pl.pallas_call(
        functools.partial(_attn_body, sub_blocks=L // ATT_BLOCK, unroll=ATT_UNROLL),
        grid=(BATCH,),
        in_specs=[
            pl.BlockSpec(memory_space=pltpu.SMEM),
            pl.BlockSpec((ATT_BLOCK, 2 * ATT_BLOCK), lambda b: (0, 0)),
            qkv_spec,
        ] + cast_in,
        out_specs=[out_spec, out_spec] + cast_out,
        out_shape=[jax.ShapeDtypeStruct((BATCH, SEQ, ATT_WIDTH), BF16),
                   jax.ShapeDtypeStruct((BATCH, SEQ, ATT_WIDTH), F32)] + cast_shapes,
        scratch_shapes=[pltpu.VMEM((3, ATT_HEADS, ATT_BLOCK, 2 * ATT_BLOCK), F32)],
        compiler_params=_params(("arbitrary",)),
        name=f"attn_d{dilation}",
    )(rel_bias, _bucket_table(window, dilation), ua.reshape(BATCH, SEQ, QKV_COLS), *[w for w, _ in side_casts])
    return (o.reshape(shape), lse.reshape(shape)), casts


def _attention(qkvs, rel_bias, side_casts=()):
    side_casts = list(side_casts) + [None] * (len(DILATED_PATTERNS) - len(side_casts))
    res = [_attention_pattern(ua, rel_bias, window, dilation, [] if w is None else [w])
           for ua, (window, dilation), w in zip(qkvs, DILATED_PATTERNS, side_casts)]
    return [r[0] for r in res], [c for r in res for c in r[1]]


def _short_conv(cin_ref, cb_ref, cc_ref, w_ref, o_ref, tail_ref):
    uu = cc_ref[...] * cin_ref[...]
    t = lax.broadcasted_iota(jnp.int32, uu.shape, 0)
    y = uu * w_ref[CONV_WIDTH - 1:CONV_WIDTH, :]
    for shift in range(1, CONV_WIDTH):
        prev = pltpu.roll(uu, shift, axis=0)
        for r in range(shift):
            prev = jnp.where(t == r, tail_ref[8 - shift + r:8 - shift + r + 1, :], prev)
        y = y + prev * w_ref[CONV_WIDTH - 1 - shift:CONV_WIDTH - shift, :]
    o_ref[...] = (cb_ref[...] * y).astype(o_ref.dtype)
    tail_ref[...] = uu[uu.shape[0] - 8:, :]


def _gla_body(u_ref, wg_ref, bg_ref, gn_ref, ctril_ref, cw_ref, *rest):
    cols = lambda col, width: u_ref.at[:, col:col + width]
    q_ref, k_ref = cols(COL_GQ, GLA_QK), cols(COL_GK, GLA_QK)
    v_ref, gr_ref, glr_ref = cols(COL_GV, GLA_V), cols(COL_GR, GLA_V), cols(COL_GLR, LANES)
    cin_ref, cb_ref, cc_ref = cols(COL_CIN, CONV_CH), cols(COL_CB, CONV_CH), cols(COL_CC, CONV_CH)
    _gla_tile(q_ref, k_ref, v_ref, gr_ref, glr_ref, wg_ref, bg_ref, gn_ref, ctril_ref,
              cin_ref, cb_ref, cc_ref, cw_ref, *rest)


def _gla_tile(q_ref, k_ref, v_ref, gr_ref, glr_ref, wg_ref, bg_ref, gn_ref, ctril_ref,
              cin_ref, cb_ref, cc_ref, cw_ref, *rest):
    n_cast = (len(rest) - 4) // 2
    o_ref, cv_ref = rest[n_cast:n_cast + 2]
    s_ref, tail_ref = rest[-2:]
    _side_cast(rest[:n_cast], rest[n_cast + 2:-2])

    @pl.when(pl.program_id(1) == 0)
    def _():
        s_ref[...] = jnp.zeros_like(s_ref)
        tail_ref[...] = jnp.zeros_like(tail_ref)

    _short_conv(cin_ref, cb_ref, cc_ref, cw_ref, cv_ref, tail_ref)

    C = GLA_CHUNK
    row = lax.broadcasted_iota(jnp.int32, (C, C), 0)
    col = lax.broadcasted_iota(jnp.int32, (C, C), 1)
    tril = row >= col
    n_chunks = TC_GLA // C

    xg = _dot3(glr_ref[...], wg_ref[...]) + bg_ref[...]
    la_all = (jnp.minimum(xg, 0.0) - jnp.log(1.0 + jnp.exp(-jnp.abs(xg)))) * (1.0 / 16.0)

    la_hi, la_lo = _split_bf16(la_all)
    span = ctril_ref.shape[0]
    cum_all = jnp.concatenate(
        [_dot(ctril_ref[...], la_hi[r0:r0 + span]) + _dot(ctril_ref[...], la_lo[r0:r0 + span])
         for r0 in range(0, TC_GLA, span)], axis=0)
    totals = jnp.concatenate([cum_all[(c + 1) * C - 1:(c + 1) * C, :] for c in range(n_chunks)]
                             + [jnp.zeros((LANES - n_chunks, GLA_QK), F32)], axis=0)
    decay_cols = jnp.exp(totals.T)

    for c in range(n_chunks):
        rows = slice(c * C, (c + 1) * C)
        cum = cum_all[rows]
        last = cum[C - 1:C, :]
        q = q_ref[rows, :]
        k = k_ref[rows, :]
        qs = q * (GLA_DK ** -0.5)
        qt = (qs * jnp.exp(cum)).astype(BF16)
        mid = cum[C // 2 - 1:C // 2, :]
        qm = (qs * jnp.exp(cum - mid)).astype(BF16)
        kt = (k * jnp.exp(mid - cum)).astype(BF16)
        kl_t = (k * jnp.exp(last - cum)).T.astype(BF16)
        for h in range(GLA_HEADS):
            sl = slice(h * GLA_DK, (h + 1) * GLA_DK)
            vs = slice(h * GLA_DV, (h + 1) * GLA_DV)
            vh = v_ref[rows, vs].astype(BF16)
            state = s_ref[h]
            st_hi, st_lo = _split_bf16(state)
            sc = jnp.where(tril, _dot_nt(qm[:, sl], kt[:, sl]), 0.0).astype(BF16)
            o = _dot(qt[:, sl], st_hi) + _dot(qt[:, sl], st_lo) + _dot(sc, vh)
            decay = jnp.broadcast_to(decay_cols[sl, c:c + 1], state.shape)
            s_ref[h] = decay * state + _dot(kl_t[sl, :], vh)
            g = gr_ref[rows, vs]
            o_ref[rows, vs] = (_rms(o, gn_ref[...]) * (g * _sigmoid(g))).astype(o_ref.dtype)


def _gla_conv(u, wg, bg, gn, conv_w, side_casts=()):
    nj = SEQ // TC_GLA
    row = lambda b, j: b * nj + j
    full = lambda a: pl.BlockSpec(a.shape, lambda b, j: (0, 0))
    cast_in, cast_out, cast_shapes = _side_cast_io(side_casts, BATCH * nj, row)
    t = np.arange(GLA_CUMSUM_SPAN)
    same_chunk = (t[:, None] // GLA_CHUNK) == (t[None, :] // GLA_CHUNK)
    chunk_tril = jnp.asarray(same_chunk & (t[:, None] >= t[None, :]), BF16)
    go, cv, *casts = pl.pallas_call(
        _gla_body,
        grid=(BATCH, nj),
        in_specs=[pl.BlockSpec((TC_GLA, U_COLS), lambda b, j: (row(b, j), 0)),
                  full(wg), full(bg), full(gn), full(chunk_tril), full(conv_w)] + cast_in,
        out_specs=[pl.BlockSpec((TC_GLA, GLA_V), lambda b, j: (row(b, j), 0)),
                   pl.BlockSpec((TC_GLA, CONV_CH), lambda b, j: (row(b, j), 0))] + cast_out,
        out_shape=[jax.ShapeDtypeStruct((TOKENS, GLA_V), BF16), jax.ShapeDtypeStruct((TOKENS, CONV_CH), BF16)]
                  + cast_shapes,
        scratch_shapes=[pltpu.VMEM((GLA_HEADS, GLA_DK, GLA_DV), F32), pltpu.VMEM((8, CONV_CH), F32)],
        compiler_params=_params(("arbitrary", "arbitrary")),
        name="gla_conv",
    )(u, wg, bg, gn, chunk_tril, conv_w, *[w for w, _ in side_casts])
    return go, cv, casts


def _outproj_body(*refs, route):
    (o1, o4, o16, l1, l4, l16, go_ref, cv_ref, x_ref, w_ref, g_ref) = refs[:11]
    perm_ref = refs[-1]
    if route:
        wr_ref, xo_ref, ho_ref, route_ref, counts_ref, carry_ref = refs[11:-1]
    else:
        xo_ref, ho_ref = refs[11:-1]

    def token_order(ref, slot):
        dilation, rows, _ = ref.shape
        if dilation == 1:
            return ref[0].astype(F32)
        tiles = range(ATT_WIDTH // LANES)
        for r in range(dilation):
            val = ref[r].astype(F32)
            for t in tiles:
                perm_ref[slot, t, pl.ds(r, rows, stride=dilation), :] = val[:, t * LANES:(t + 1) * LANES]
        return jnp.concatenate([perm_ref[slot, t] for t in tiles], axis=-1)

    la, lb, lc = token_order(l1, 0), token_order(l4, 0), token_order(l16, 1)
    oa, ob, oc = token_order(o1, 0), token_order(o4, 2), token_order(o16, 3)
    m = jnp.maximum(jnp.maximum(la, lb), lc)
    ea, eb, ec = jnp.exp(la - m), jnp.exp(lb - m), jnp.exp(lc - m)
    att = (ea * oa + eb * ob + ec * oc) / (ea + eb + ec)
    y = (x_ref[...]
         + _dot(att.astype(BF16), w_ref[0:ATT_WIDTH, :])
         + _dot(go_ref[...], w_ref[ATT_WIDTH:ATT_WIDTH + GLA_V, :])
         + _dot(cv_ref[...], w_ref[ATT_WIDTH + GLA_V:MIX_WIDTH, :]))
    xo_ref[...] = y
    hf = _rms(y, g_ref[...])
    ho_ref[...] = hf.astype(ho_ref.dtype)
    if route:
        @pl.when(pl.program_id(0) == 0)
        def _():
            carry_ref[...] = jnp.zeros_like(carry_ref)

        tm = hf.shape[0]
        ne = N_EXPERTS
        hf_hi, hf_lo = _split_bf16(hf)
        part = _dot_nt(wr_ref[0], hf_hi) + _dot_nt(wr_ref[1], hf_lo)
        logits = part[0:ne] + part[ne:2 * ne]
        eidx = lax.broadcasted_iota(jnp.int32, logits.shape, 0).astype(F32)
        v1 = jnp.max(logits, axis=0, keepdims=True)
        i1 = jnp.min(jnp.where(logits == v1, eidx, float(ne)), axis=0, keepdims=True)
        lg2 = jnp.where(eidx == i1, -jnp.inf, logits)
        v2 = jnp.max(lg2, axis=0, keepdims=True)
        i2 = jnp.min(jnp.where(lg2 == v2, eidx, float(ne)), axis=0, keepdims=True)
        e2 = jnp.exp(v2 - v1)
        w1 = 1.0 / (1.0 + e2)
        w2 = e2 * w1
        sel1 = eidx == i1
        sel2 = eidx == i2
        onehot = jnp.where(sel1, 1.0, jnp.where(sel2, 1.0, 0.0))
        tri = (lax.broadcasted_iota(jnp.int32, (tm, tm), 0) <= lax.broadcasted_iota(jnp.int32, (tm, tm), 1))
        onehot16 = jnp.concatenate([onehot, jnp.zeros_like(onehot)], axis=0).astype(BF16)
        csum = _dot(onehot16, jnp.where(tri, 1.0, 0.0).astype(BF16))[0:ne]
        carry = carry_ref[:, 0:1]
        rank = csum - onehot + carry
        r1 = jnp.sum(jnp.where(sel1, rank, 0.0), axis=0, keepdims=True)
        r2 = jnp.sum(jnp.where(sel2, rank, 0.0), axis=0, keepdims=True)
        total = jnp.broadcast_to(carry + csum[:, tm - 1:tm], carry_ref.shape)
        carry_ref[...] = total
        counts_ref[...] = total
        rows = {ROUTE_I1: i1, ROUTE_I2: i2, ROUTE_W1: w1, ROUTE_W2: w2, ROUTE_R1: r1, ROUTE_R2: r2}
        zero = jnp.zeros_like(i1)
        route_ref[...] = jnp.concatenate([rows.get(r, zero) for r in range(8)], axis=0)


def _outproj(att, go, cv, x, w, g, w_router=None):
    route = w_router is not None
    tm = TM_PROJ
    tile = lambda cols: pl.BlockSpec((tm, cols), lambda i: (i, 0))
    full = lambda a: pl.BlockSpec(a.shape, lambda i: (0, 0))
    (o1, l1), (o4, l4), (o16, l16) = att
    args = [o1, o4, o16, l1, l4, l16, go, cv, x, w, g]
    att_specs = [_subseq_spec(d, ATT_WIDTH) for _, d in DILATED_PATTERNS]
    in_specs = att_specs * 2 + [tile(GLA_V), tile(CONV_CH), tile(D_MODEL), full(w), full(g)]
    out_specs = [tile(D_MODEL), tile(D_MODEL)]
    out_shape = [jax.ShapeDtypeStruct((TOKENS, D_MODEL), F32),
                 jax.ShapeDtypeStruct((TOKENS, D_MODEL), F32 if route else BF16)]
    scratch = []
    if route:
        args.append(w_router)
        in_specs.append(pl.BlockSpec(w_router.shape, lambda i: (0, 0, 0)))
        out_specs += [pl.BlockSpec((8, tm), lambda i: (0, i)), pl.BlockSpec((N_EXPERTS, LANES), lambda i: (0, 0))]
        out_shape += [jax.ShapeDtypeStruct((8, TOKENS), F32), jax.ShapeDtypeStruct((N_EXPERTS, LANES), F32)]
        scratch = [pltpu.VMEM((N_EXPERTS, LANES), F32)]
    scratch.append(pltpu.VMEM((4, ATT_WIDTH // LANES, tm, LANES), F32))
    return pl.pallas_call(
        functools.partial(_outproj_body, route=route),
        grid=(TOKENS // tm,),
        in_specs=in_specs,
        out_specs=out_specs,
        out_shape=out_shape,
        scratch_shapes=scratch,
        compiler_params=_params(("arbitrary",) if route else ("parallel",)),
        name="outproj_route" if route else "outproj",
    )(*args)


def _swiglu_accumulate(h, w1_ref, w3_ref, w2_ref, acc_ref, tf):
    for c0 in range(0, tf, FFN_SUB):
        c1 = min(c0 + FFN_SUB, tf)
        a = _dot(h, w1_ref[:, c0:c1])
        b = _dot(h, w3_ref[:, c0:c1])
        act = a * _sigmoid(a) * b
        acc_ref[...] += _dot(act.astype(BF16), w2_ref[c0:c1, :])


def _ffn_body(x_ref, h_ref, w1_ref, w3_ref, w2_ref, *rest, tf):
    n_cast = (len(rest) - 1) // 2
    o_ref = rest[n_cast]

    @pl.when(pl.program_id(1) == 0)
    def _():
        o_ref[...] = x_ref[...]

    _swiglu_accumulate(h_ref[...], w1_ref, w3_ref, w2_ref, o_ref, tf)
    _side_cast(rest[:n_cast], rest[n_cast + 1:])


def _ffn(x, h, w1, w3, w2, *, side_casts=()):
    nj, _, tf = w1.shape
    tm = TM_FFN
    tile = lambda cols: pl.BlockSpec((tm, cols), lambda i, j: (i, 0))
    cast_in, cast_out, cast_shapes = _side_cast_io(side_casts, (TOKENS // tm) * nj, lambda i, j: i * nj + j)
    resident = dict(pipeline_mode=pl.Buffered(1)) if nj == 1 else {}
    res = pl.pallas_call(
        functools.partial(_ffn_body, tf=tf),
        grid=(TOKENS // tm, nj),
        in_specs=[tile(D_MODEL), tile(D_MODEL),
                  pl.BlockSpec((None, D_MODEL, tf), lambda i, j: (j, 0, 0), **resident),
                  pl.BlockSpec((None, D_MODEL, tf), lambda i, j: (j, 0, 0), **resident),
                  pl.BlockSpec((tf, D_MODEL), lambda i, j: (j, 0), **resident)] + cast_in,
        out_specs=[tile(D_MODEL)] + cast_out,
        out_shape=[jax.ShapeDtypeStruct((TOKENS, D_MODEL), F32)] + cast_shapes,
        compiler_params=_params(("parallel", "arbitrary")),
        name="dense_ffn",
    )(x, h, w1, w3, w2, *[w for w, _ in side_casts])
    return res[0], res[1:]


ROW_TILE = D_MODEL // LANES


def _to_row_tiled(dst_ref, lead, val):
    rows = val.shape[0]
    for s in range(ROW_TILE):
        dst_ref[(*lead, pl.ds(s, rows, stride=ROW_TILE), slice(None))] = val[:, s * LANES:(s + 1) * LANES]


def _from_row_tiled(src_ref, lead, rows):
    return jnp.concatenate([src_ref[(*lead, pl.ds(s, rows, stride=ROW_TILE), slice(None))]
                            for s in range(ROW_TILE)], axis=-1)


def _row_tile(idx):
    return pl.ds(pl.multiple_of(idx * ROW_TILE, ROW_TILE), ROW_TILE)


def _dispatch_body(pos_ref, last_tile_ref, h_ref, xs_hbm, stage_ref, zero_ref, sem, zero_sem):
    i = pl.program_id(0)
    n = pl.num_programs(0)
    tm = DISPATCH_CHUNK
    slot = i % 2

    def drain(s):
        for _ in range(2):
            pltpu.make_async_copy(stage_ref.at[s], xs_hbm.at[pl.ds(0, tm * ROW_TILE)], sem.at[s]).wait()

    @pl.when(i == 0)
    def _():
        zero_ref[...] = jnp.zeros_like(zero_ref)

        def zero_copy(e):
            start = pl.multiple_of(last_tile_ref[e] * (TM_MOE * ROW_TILE), TM_MOE * ROW_TILE)
            return pltpu.make_async_copy(zero_ref, xs_hbm.at[pl.ds(start, TM_MOE * ROW_TILE)], zero_sem)

        for e in range(2 * N_EXPERTS):
            @pl.when(last_tile_ref[e] >= 0)
            def _():
                zero_copy(e).start()
        for e in range(2 * N_EXPERTS):
            @pl.when(last_tile_ref[e] >= 0)
            def _():
                zero_copy(e).wait()

    @pl.when(i >= 2)
    def _():
        drain(slot)

    _to_row_tiled(stage_ref, (slot,), h_ref[...])

    def body(t, carry):
        for k in range(2):
            dst = pos_ref[2 * (i * tm + t) + k]
            pltpu.make_async_copy(stage_ref.at[slot, _row_tile(t)], xs_hbm.at[_row_tile(dst)],
                                  sem.at[slot]).start(priority=k)
        return carry
    lax.fori_loop(0, tm, body, 0, unroll=8)

    @pl.when(i == n - 1)
    def _():
        drain(1 - slot)
        drain(slot)


def _dispatch(pos, last_tile, h):
    tm = DISPATCH_CHUNK
    return pl.pallas_call(
        _dispatch_body,
        grid=(TOKENS // tm,),
        in_specs=[pl.BlockSpec(memory_space=pltpu.SMEM),
                  pl.BlockSpec(memory_space=pltpu.SMEM),
                  pl.BlockSpec((tm, D_MODEL), lambda i: (i, 0))],
        out_specs=pl.BlockSpec(memory_space=pl.ANY),
        out_shape=jax.ShapeDtypeStruct((N_SORTED * ROW_TILE, LANES), F32),
        scratch_shapes=[pltpu.VMEM((2, tm * ROW_TILE, LANES), F32),
                        pltpu.VMEM((TM_MOE * ROW_TILE, LANES), F32),
                        pltpu.SemaphoreType.DMA((2,)), pltpu.SemaphoreType.DMA(())],
        compiler_params=_params(("arbitrary",)),
        name="moe_dispatch",
    )(pos, last_tile, h)


def _gffn_body(te_ref, nu_ref, xs_ref, w1_ref, w3_ref, w2_ref, o_ref, hb_ref, acc_ref):
    del te_ref
    i = pl.program_id(0)
    j = pl.program_id(1)

    used = i < nu_ref[0]

    @pl.when(j == 0)
    def _():
        acc_ref[...] = jnp.zeros_like(acc_ref)

    @pl.when(used & (j == 0))
    def _():
        hb_ref[...] = _from_row_tiled(xs_ref, (), TM_MOE).astype(BF16)

    @pl.when(used)
    def _():
        _swiglu_accumulate(hb_ref[...], w1_ref, w3_ref, w2_ref, acc_ref, TF_MOE)

    @pl.when(j == pl.num_programs(1) - 1)
    def _():
        _to_row_tiled(o_ref, (), acc_ref[...])


def _grouped_ffn(tile_expert, n_used, xs, w1, w3, w2):
    nj = FFN_EXPERT // TF_MOE
    col = lambda i, j, nu: jnp.where(i < nu[0], j, nj - 1)
    grid_spec = pltpu.PrefetchScalarGridSpec(
        num_scalar_prefetch=2,
        grid=(N_TILES_MOE, nj),
        in_specs=[
            pl.BlockSpec((TM_MOE * ROW_TILE, LANES), lambda i, j, te, nu: (jnp.minimum(i, nu[0] - 1), 0)),
            pl.BlockSpec((None, None, D_MODEL, TF_MOE), lambda i, j, te, nu: (te[i], col(i, j, nu), 0, 0)),
            pl.BlockSpec((None, None, D_MODEL, TF_MOE), lambda i, j, te, nu: (te[i], col(i, j, nu), 0, 0)),
            pl.BlockSpec((None, TF_MOE, D_MODEL), lambda i, j, te, nu: (te[i], col(i, j, nu), 0)),
        ],
        out_specs=pl.BlockSpec((TM_MOE * ROW_TILE, LANES), lambda i, j, te, nu: (i, 0)),
        scratch_shapes=[pltpu.VMEM((TM_MOE, D_MODEL), BF16), pltpu.VMEM((TM_MOE, D_MODEL), F32)],
    )
    return pl.pallas_call(
        _gffn_body,
        grid_spec=grid_spec,
        out_shape=jax.ShapeDtypeStruct((N_SORTED * ROW_TILE, LANES), F32),
        compiler_params=_params(("arbitrary", "arbitrary")),
        name="moe_ffn",
    )(tile_expert, n_used, xs, w1, w3, w2)


def _combine_body(pos_ref, x_ref, gate_ref, g_ref, ys_hbm, o_ref, buf_ref, sem):
    i = pl.program_id(0)
    n = pl.num_programs(0)
    tm = TM_COMBINE

    def issue(tile, slot):
        def body(t, carry):
            for k in range(2):
                src = pos_ref[2 * (tile * tm + t) + k]
                pltpu.make_async_copy(ys_hbm.at[_row_tile(src)], buf_ref.at[slot, k, _row_tile(t)],
                                      sem.at[slot]).start(priority=k)
            return carry
        lax.fori_loop(0, tm, body, 0, unroll=8)

    @pl.when(i == 0)
    def _():
        issue(0, 0)

    @pl.when(i + 1 < n)
    def _():
        issue(i + 1, (i + 1) % 2)

    slot = i % 2
    for k in range(2):
        pltpu.make_async_copy(ys_hbm.at[pl.ds(0, tm * ROW_TILE)], buf_ref.at[slot, k], sem.at[slot]).wait()
    w1 = gate_ref[:, 0:1]
    w2 = gate_ref[:, 1:2]
    y = x_ref[...] + w1 * _from_row_tiled(buf_ref, (slot, 0), tm) + w2 * _from_row_tiled(buf_ref, (slot, 1), tm)
    o_ref[...] = _rms(y, g_ref[...])


def _combine(pos, x, gates, g, ys):
    tm = TM_COMBINE
    return pl.pallas_call(
        _combine_body,
        grid=(TOKENS // tm,),
        in_specs=[pl.BlockSpec(memory_space=pltpu.SMEM),
                  pl.BlockSpec((tm, D_MODEL), lambda i: (i, 0)),
                  pl.BlockSpec((tm, 2), lambda i: (i, 0)),
                  pl.BlockSpec((1, D_MODEL), lambda i: (0, 0)),
                  pl.BlockSpec(memory_space=pl.ANY)],
        out_specs=pl.BlockSpec((tm, D_MODEL), lambda i: (i, 0)),
        out_shape=jax.ShapeDtypeStruct((TOKENS, D_MODEL), F32),
        scratch_shapes=[pltpu.VMEM((2, 2, tm * ROW_TILE, LANES), F32), pltpu.SemaphoreType.DMA((2,))],
        compiler_params=_params(("arbitrary",)),
        name="moe_combine",
    )(pos, x, gates, g, ys)


def _routing_tables(route, counts):
    cnt = counts[:, 0].astype(jnp.int32)
    tiles = (cnt + TM_MOE - 1) // TM_MOE
    tile_end = jnp.cumsum(tiles)
    tile_start = tile_end - tiles
    n_used = tile_end[-1]
    expert = route[ROUTE_I1:ROUTE_I2 + 1].astype(jnp.int32)
    rank = route[ROUTE_R1:ROUTE_R2 + 1].astype(jnp.int32)
    group_start = jnp.sum(jnp.where(expert[..., None] == jnp.arange(N_EXPERTS), tile_start * TM_MOE, 0), axis=-1)
    pos = (group_start + rank).T
    tile_id = jnp.minimum(jnp.arange(N_TILES_MOE, dtype=jnp.int32), n_used - 1)
    tile_expert = jnp.sum(tile_id[:, None] >= tile_end[None, :], axis=1).astype(jnp.int32)
    last_tile = jnp.where(tiles > 0, tile_end - 1, -1)
    spare = n_used + jnp.arange(N_EXPERTS)
    zero_tiles = jnp.concatenate([last_tile, jnp.where(spare < N_TILES_MOE, spare, -1)]).astype(jnp.int32)
    return pos.reshape(2 * TOKENS), tile_expert, n_used.reshape(1), zero_tiles


def _moe(x, h, route, counts, w1, w3, w2, g_final):
    pos, tile_expert, n_used, last_tile = _routing_tables(route, counts)
    xs = _dispatch(pos, last_tile, h)
    ys = _grouped_ffn(tile_expert, n_used, xs, w1, w3, w2)
    return _combine(pos, x, route[ROUTE_W1:ROUTE_W2 + 1].T, g_final, ys)


def _prep_w_in(w):
    aq, ak, av, gq, gk, gv, gr, glr, c_in, c_b, c_c = jnp.split(w, np.cumsum(SPLIT_SIZES)[:-1].tolist(), axis=2)
    pad = jnp.zeros(w.shape[:2] + (LANES - GLA_RANK,), w.dtype)
    return jnp.concatenate([aq, ak, av, gv, gr, gq, gk, c_in, c_b, c_c, glr, pad], axis=2).astype(BF16)


def _prep_router(w):
    wt = w.T
    hi = wt.astype(BF16)
    lo = (wt - hi.astype(F32)).astype(BF16)
    return jnp.stack([jnp.concatenate([hi, lo]), jnp.concatenate([hi, jnp.zeros_like(hi)])])


def kernel(x, w_mix_in, w_mix_out, g_mix, rel_bias, gla_w_gate, gla_b_gate, gla_g_norm, conv_w,
           g_ffn, ffn_w1, ffn_w3, ffn_w2, moe_router, moe_w1, moe_w3, moe_w2, g_final):
    assert DEPTH == 2
    x = x.reshape(TOKENS, D_MODEL)
    up_job = lambda w, tf: (w, w.shape[-1] // tf)
    down = lambda c: c.reshape(c.shape[0], c.shape[2], c.shape[3])
    experts = {}
    w_in = _prep_w_in(w_mix_in)
    for layer in range(DEPTH):
        u, qkvs, casts = _inproj(x, g_mix[layer].reshape(1, D_MODEL), w_in, layer,
                                 side_casts=[up_job(moe_w1[0], TF_MOE)] if layer == 1 else [])
        if layer == 1:
            experts["w1"] = casts[0]
        dense_jobs = [up_job(ffn_w1[:1], TF_DENSE), up_job(ffn_w3[:1], TF_DENSE), (ffn_w2[:1], 1)] if layer == 0 else []
        att, dense_bf16 = _attention(qkvs, rel_bias, dense_jobs)
        if layer == 0:
            ffn_bf16 = (dense_bf16[0][0], dense_bf16[1][0], down(dense_bf16[2])[0])
        wg = jnp.pad(gla_w_gate[layer], ((0, LANES - GLA_RANK), (0, 0)))
        go, cv, _ = _gla_conv(u, wg, gla_b_gate[layer].reshape(1, GLA_QK), gla_g_norm[layer].reshape(1, GLA_DV),
                              jnp.pad(conv_w[layer], ((0, 8 - CONV_WIDTH), (0, 0))))
        w_out = w_mix_out[layer].astype(BF16)
        g2 = g_ffn[layer].reshape(1, D_MODEL)
        i = layer // 2
        if layer % 2 == 0:
            x, h = _outproj(att, go, cv, x, w_out, g2)
            x, (experts["w3"], cast) = _ffn(x, h, *ffn_bf16, side_casts=[up_job(moe_w3[i], TF_MOE), (moe_w2[i], 1)])
            experts["w2"] = down(cast)
        else:
            x, h, route, counts = _outproj(att, go, cv, x, w_out, g2, _prep_router(moe_router[i]))
            x = _moe(x, h, route, counts, experts["w1"], experts["w3"], experts["w2"], g_final.reshape(1, D_MODEL))
    return x.reshape(BATCH, SEQ, D_MODEL)
```

```python
import functools
import math

import jax
import jax.numpy as jnp
import numpy as np
from jax import lax
from jax.experimental import pallas as pl
from jax.experimental.pallas import tpu as pltpu

F32 = jnp.float32
BF16 = jnp.bfloat16

D_MODEL = 1024
BATCH = 8
SEQ = 2048
TOKENS = BATCH * SEQ
DEPTH = 2
EPS = 1e-6

HEAD_DIM = 64
ATT_HEADS = 4
ATT_WIDTH = ATT_HEADS * HEAD_DIM
DILATED_PATTERNS = ((128, 1), (512, 4), (2048, 16))
ATT_BLOCK = 128
REL_BUCKETS = 32
REL_MAX_DISTANCE = 2048

GLA_HEADS = 4
GLA_DK = 64
GLA_DV = 128
GLA_RANK = 16
GLA_CHUNK = 64
GLA_QK = GLA_HEADS * GLA_DK
GLA_V = GLA_HEADS * GLA_DV

CONV_CH = 256
CONV_WIDTH = 3
MIX_WIDTH = ATT_WIDTH + GLA_V + CONV_CH

SPLIT_SIZES = (ATT_WIDTH, ATT_WIDTH, ATT_WIDTH, GLA_QK, GLA_QK, GLA_V, GLA_V, GLA_RANK,
               CONV_CH, CONV_CH, CONV_CH)

FFN_DENSE = 2816
N_EXPERTS = 8
FFN_EXPERT = 3584

LANES = 128
MXU_WIDTH = 256
VMEM_LIMIT = 56 * 1024 * 1024

QKV_COLS = 3 * ATT_WIDTH
COL_GV, COL_GR, COL_GQ, COL_GK = 0, 512, 1024, 1280
COL_CIN, COL_CB, COL_CC, COL_GLR = 1536, 1792, 2048, 2304
U_COLS = COL_GLR + LANES

NEG_BIG = -1e30

ATT_UNROLL = 15
TM_PROJ = 512
TC_GLA = 512
GLA_CUMSUM_SPAN = 256
TM_FFN = 512
TF_DENSE = 2816
FFN_SUB = 256
TM_MOE = 512
TF_MOE = 1792
N_TILES_MOE = 2 * TOKENS // TM_MOE + N_EXPERTS
N_SORTED = N_TILES_MOE * TM_MOE
DISPATCH_CHUNK = 1024
TM_COMBINE = 1024

ROUTE_I1, ROUTE_I2, ROUTE_W1, ROUTE_W2, ROUTE_R1, ROUTE_R2 = range(6)


def _params(sem):
    return pltpu.CompilerParams(dimension_semantics=sem, vmem_limit_bytes=VMEM_LIMIT)


def _split_bf16(a):
    hi = a.astype(BF16)
    lo = (a - hi.astype(F32)).astype(BF16)
    return hi, lo


def _dot(a, b):
    return jnp.dot(a, b, preferred_element_type=F32)


def _dot3(a, b):
    a_hi, a_lo = _split_bf16(a)
    b_hi, b_lo = _split_bf16(b)
    return _dot(a_hi, b_hi) + _dot(a_lo, b_hi) + _dot(a_hi, b_lo)


def _dot_nt(a, b):
    return lax.dot_general(a, b, (((1,), (1,)), ((), ())), preferred_element_type=F32)


def _rms(x, g):
    ms = jnp.mean(x * x, axis=-1, keepdims=True)
    return x * lax.rsqrt(ms + EPS) * g


def _sigmoid(x):
    return 1.0 / (1.0 + jnp.exp(-x))


def _side_cast(srcs, dsts):
    for src, dst in zip(srcs, dsts):
        width = dst.shape[-1]
        for s in range(dst.shape[0]):
            dst[s] = src[:, s * width:(s + 1) * width].astype(dst.dtype)


def _side_cast_io(jobs, steps, step_of):
    in_specs, out_specs, out_shapes = [], [], []
    for w, splits in jobs:
        g, r, c = w.shape
        rb = g * r // steps
        per_group = r // rb
        in_specs.append(pl.BlockSpec(
            (None, rb, c), lambda *ids, pg=per_group: (step_of(*ids) // pg, step_of(*ids) % pg, 0)))
        out_specs.append(pl.BlockSpec(
            (None, splits, rb, c // splits),
            lambda *ids, pg=per_group: (step_of(*ids) // pg, 0, step_of(*ids) % pg, 0)))
        out_shapes.append(jax.ShapeDtypeStruct((g, splits, r, c // splits), BF16))
    return in_specs, out_specs, out_shapes


def _inproj_body(x_ref, g_ref, w_ref, *rest):
    n_pat = len(DILATED_PATTERNS)
    n_cast = (len(rest) - n_pat - 2) // 2
    o_ref = rest[n_cast]
    qkv_refs = rest[n_cast + 1:n_cast + 1 + n_pat]
    qkv_f32 = rest[-1]
    _side_cast(rest[:n_cast], rest[n_cast + 1 + n_pat:-1])
    h = _rms(x_ref[...], g_ref[...]).astype(BF16)
    for c0 in range(0, QKV_COLS, MXU_WIDTH):
        res = _dot(h, w_ref[:, c0:c0 + MXU_WIDTH])
        for t in range(MXU_WIDTH // LANES):
            qkv_f32[c0 // LANES + t] = res[:, t * LANES:(t + 1) * LANES]
    for (_, dilation), ref in zip(DILATED_PATTERNS, qkv_refs):
        for r in range(dilation):
            rows = pl.ds(r, TM_PROJ // dilation, stride=dilation)
            ref[r] = jnp.concatenate([qkv_f32[t, rows, :] for t in range(QKV_COLS // LANES)],
                                     axis=-1).astype(ref.dtype)
    for c0 in range(0, U_COLS, MXU_WIDTH):
        c1 = min(c0 + MXU_WIDTH, U_COLS)
        o_ref[:, c0:c1] = _dot(h, w_ref[:, QKV_COLS + c0:QKV_COLS + c1])


def _subseq_spec(dilation, cols):
    tiles = SEQ // TM_PROJ
    return pl.BlockSpec((None, dilation, TM_PROJ // dilation, cols), lambda i: (i // tiles, 0, i % tiles, 0))


def _inproj(x, g, w_all, layer, side_casts=()):
    n_pat = len(DILATED_PATTERNS)
    qkv_shapes = [jax.ShapeDtypeStruct((BATCH, d, SEQ // d, QKV_COLS), BF16) for _, d in DILATED_PATTERNS]
    cast_in, cast_out, cast_shapes = _side_cast_io(side_casts, TOKENS // TM_PROJ, lambda i: i)
    u, *rest = pl.pallas_call(
        _inproj_body,
        grid=(TOKENS // TM_PROJ,),
        in_specs=[
            pl.BlockSpec((TM_PROJ, D_MODEL), lambda i: (i, 0)),
            pl.BlockSpec((1, D_MODEL), lambda i: (0, 0)),
            pl.BlockSpec((None, D_MODEL, QKV_COLS + U_COLS), lambda i: (layer, 0, 0)),
        ] + cast_in,
        out_specs=[pl.BlockSpec((TM_PROJ, U_COLS), lambda i: (i, 0))]
                  + [_subseq_spec(d, QKV_COLS) for _, d in DILATED_PATTERNS] + cast_out,
        out_shape=[jax.ShapeDtypeStruct((TOKENS, U_COLS), F32)] + qkv_shapes + cast_shapes,
        scratch_shapes=[pltpu.VMEM((QKV_COLS // LANES, TM_PROJ, LANES), F32)],
        compiler_params=_params(("parallel",)),
        name="inproj",
    )(x, g, w_all, *[w for w, _ in side_casts])
    return u, rest[:n_pat], rest[n_pat:]


def _rel_bucket(dist):
    max_exact = REL_BUCKETS // 2
    d = jnp.maximum(dist, 0)
    log_ratio = jnp.log(jnp.maximum(d, 1).astype(F32) / max_exact) / math.log(REL_MAX_DISTANCE / max_exact)
    large = jnp.minimum(max_exact + (log_ratio * (REL_BUCKETS - max_exact)).astype(jnp.int32), REL_BUCKETS - 1)
    return jnp.where(d < max_exact, d, large)


def _bucket_table(window, dilation):
    span = window // dilation
    qi = jnp.arange(ATT_BLOCK)[:, None]
    kj = jnp.arange(2 * ATT_BLOCK)[None, :]
    sub_dist = qi - kj + ATT_BLOCK
    band = (sub_dist >= 0) & (sub_dist <= span)
    return jnp.where(band, _rel_bucket(sub_dist * dilation), -1).astype(jnp.int32)


def _attn_body(rb_ref, bidx_ref, qkv_ref, *rest, sub_blocks, unroll):
    n_cast = (len(rest) - 3) // 2
    o_ref, lse_ref = rest[n_cast:n_cast + 2]
    bias_ref = rest[-1]
    _side_cast(rest[:n_cast], rest[n_cast + 2:-1])
    nblk = SEQ // ATT_BLOCK

    @pl.when(pl.program_id(0) == 0)
    def _():
        bidx = bidx_ref[...]
        in_prev = lax.broadcasted_iota(jnp.int32, bidx.shape, 1) < ATT_BLOCK
        for h in range(ATT_HEADS):
            acc = jnp.full(bidx.shape, NEG_BIG, F32)
            for b in range(REL_BUCKETS):
                acc = jnp.where(bidx == b, rb_ref[b, h], acc)
            bias_ref[0, h] = acc
            bias_ref[1, h] = jnp.where(in_prev, NEG_BIG, acc)
            bias_ref[2, h] = jnp.concatenate([acc[:, ATT_BLOCK:], jnp.full_like(acc[:, ATT_BLOCK:], NEG_BIG)], axis=1)

    def block(n, first):
        if first:
            rows, krows, variant = slice(0, ATT_BLOCK), slice(0, 2 * ATT_BLOCK), 2
        else:
            r0 = pl.multiple_of(n * ATT_BLOCK, ATT_BLOCK)
            rows = pl.ds(r0, ATT_BLOCK)
            krows = pl.ds(r0 - ATT_BLOCK, 2 * ATT_BLOCK)
            if sub_blocks == nblk:
                variant = 0
            elif sub_blocks == 1:
                variant = 1
            else:
                variant = jnp.where(n % sub_blocks == 0, 1, 0)
        q = qkv_ref[rows, 0:ATT_WIDTH]
        kk = qkv_ref[krows, ATT_WIDTH:2 * ATT_WIDTH]
        vv = qkv_ref[krows, 2 * ATT_WIDTH:3 * ATT_WIDTH]
        q = q * jnp.asarray(HEAD_DIM ** -0.5, BF16)
        head_of_lane = lax.broadcasted_iota(jnp.int32, (ATT_BLOCK, ATT_WIDTH), 1) // HEAD_DIM
        ones = jnp.ones((kk.shape[0], LANES), BF16)
        num = den = mx = None
        for h in range(ATT_HEADS):
            mine = head_of_lane == h
            bias = bias_ref[variant, h]
            s = _dot_nt(jnp.where(mine, q, jnp.zeros_like(q)), kk) + bias
            m = jnp.max(s, axis=-1, keepdims=True)
            p = jnp.exp(s - m).astype(BF16)
            num_h = _dot(p, vv)
            den_h = jnp.tile(_dot(p, ones), (1, ATT_WIDTH // LANES))
            m_h = jnp.broadcast_to(m, (ATT_BLOCK, ATT_WIDTH))
            num = num_h if h == 0 else jnp.where(mine, num_h, num)
            den = den_h if h == 0 else jnp.where(mine, den_h, den)
            mx = m_h if h == 0 else jnp.where(mine, m_h, mx)
        o_ref[rows, :] = (num / den).astype(o_ref.dtype)
        lse_ref[rows, :] = mx + jnp.log(den)

    block(0, True)

    def loop_body(n, carry):
        block(n, False)
        return carry
    lax.fori_loop(1, nblk, loop_body, 0, unroll=unroll)


def _attention_pattern(ua, rel_bias, window, dilation, side_casts=()):
    L = SEQ // dilation
    shape = (BATCH, dilation, L, ATT_WIDTH)
    qkv_spec = pl.BlockSpec((None, SEQ, QKV_COLS), lambda b: (b, 0, 0))
    out_spec = pl.BlockSpec((None, SEQ, ATT_WIDTH), lambda b: (b, 0, 0))
    cast_in, cast_out, cast_shapes = _side_cast_io(side_casts, BATCH, lambda b: b)
    o, lse, *casts = pl.pallas_call(
        functools.partial(_attn_body, sub_blocks=L // ATT_BLOCK, unroll=ATT_UNROLL),
        grid=(BATCH,),
        in_specs=[
            pl.BlockSpec(memory_space=pltpu.SMEM),
            pl.BlockSpec((ATT_BLOCK, 2 * ATT_BLOCK), lambda b: (0, 0)),
            qkv_spec,
        ] + cast_in,
        out_specs=[out_spec, out_spec] + cast_out,
        out_shape=[jax.ShapeDtypeStruct((BATCH, SEQ, ATT_WIDTH), BF16),
                   jax.ShapeDtypeStruct((BATCH, SEQ, ATT_WIDTH), F32)] + cast_shapes,
        scratch_shapes=[pltpu.VMEM((3, ATT_HEADS, ATT_BLOCK, 2 * ATT_BLOCK), F32)],
        compiler_params=_params(("arbitrary",)),
        name=f"attn_d{dilation}",
    )(rel_bias, _bucket_table(window, dilation), ua.reshape(BATCH, SEQ, QKV_COLS), *[w for w, _ in side_casts])
    return (o.reshape(shape), lse.reshape(shape)), casts


def _attention(qkvs, rel_bias, side_casts=()):
    side_casts = list(side_casts) + [None] * (len(DILATED_PATTERNS) - len(side_casts))
    res = [_attention_pattern(ua, rel_bias, window, dilation, [] if w is None else [w])
           for ua, (window, dilation), w in zip(qkvs, DILATED_PATTERNS, side_casts)]
    return [r[0] for r in res], [c for r in res for c in r[1]]


def _short_conv(cin_ref, cb_ref, cc_ref, w_ref, o_ref, tail_ref):
    uu = cc_ref[...] * cin_ref[...]
    t = lax.broadcasted_iota(jnp.int32, uu.shape, 0)
    y = uu * w_ref[CONV_WIDTH - 1:CONV_WIDTH, :]
    for shift in range(1, CONV_WIDTH):
        prev = pltpu.roll(uu, shift, axis=0)
        for r in range(shift):
            prev = jnp.where(t == r, tail_ref[8 - shift + r:8 - shift + r + 1, :], prev)
        y = y + prev * w_ref[CONV_WIDTH - 1 - shift:CONV_WIDTH - shift, :]
    o_ref[...] = (cb_ref[...] * y).astype(o_ref.dtype)
    tail_ref[...] = uu[uu.shape[0] - 8:, :]


def _gla_body(u_ref, wg_ref, bg_ref, gn_ref, ctril_ref, cw_ref, *rest):
    cols = lambda col, width: u_ref.at[:, col:col + width]
    q_ref, k_ref = cols(COL_GQ, GLA_QK), cols(COL_GK, GLA_QK)
    v_ref, gr_ref, glr_ref = cols(COL_GV, GLA_V), cols(COL_GR, GLA_V), cols(COL_GLR, LANES)
    cin_ref, cb_ref, cc_ref = cols(COL_CIN, CONV_CH), cols(COL_CB, CONV_CH), cols(COL_CC, CONV_CH)
    _gla_tile(q_ref, k_ref, v_ref, gr_ref, glr_ref, wg_ref, bg_ref, gn_ref, ctril_ref,
              cin_ref, cb_ref, cc_ref, cw_ref, *rest)


def _gla_tile(q_ref, k_ref, v_ref, gr_ref, glr_ref, wg_ref, bg_ref, gn_ref, ctril_ref,
              cin_ref, cb_ref, cc_ref, cw_ref, *rest):
    n_cast = (len(rest) - 4) // 2
    o_ref, cv_ref = rest[n_cast:n_cast + 2]
    s_ref, tail_ref = rest[-2:]
    _side_cast(rest[:n_cast], rest[n_cast + 2:-2])

    @pl.when(pl.program_id(1) == 0)
    def _():
        s_ref[...] = jnp.zeros_like(s_ref)
        tail_ref[...] = jnp.zeros_like(tail_ref)

    _short_conv(cin_ref, cb_ref, cc_ref, cw_ref, cv_ref, tail_ref)

    C = GLA_CHUNK
    row = lax.broadcasted_iota(jnp.int32, (C, C), 0)
    col = lax.broadcasted_iota(jnp.int32, (C, C), 1)
    tril = row >= col
    n_chunks = TC_GLA // C

    xg = _dot3(glr_ref[...], wg_ref[...]) + bg_ref[...]
    la_all = (jnp.minimum(xg, 0.0) - jnp.log(1.0 + jnp.exp(-jnp.abs(xg)))) * (1.0 / 16.0)

    la_hi, la_lo = _split_bf16(la_all)
    span = ctril_ref.shape[0]
    cum_all = jnp.concatenate(
        [_dot(ctril_ref[...], la_hi[r0:r0 + span]) + _dot(ctril_ref[...], la_lo[r0:r0 + span])
         for r0 in range(0, TC_GLA, span)], axis=0)
    totals = jnp.concatenate([cum_all[(c + 1) * C - 1:(c + 1) * C, :] for c in range(n_chunks)]
                             + [jnp.zeros((LANES - n_chunks, GLA_QK), F32)], axis=0)
    decay_cols = jnp.exp(totals.T)

    for c in range(n_chunks):
        rows = slice(c * C, (c + 1) * C)
        cum = cum_all[rows]
        last = cum[C - 1:C, :]
        q = q_ref[rows, :]
        k = k_ref[rows, :]
        qs = q * (GLA_DK ** -0.5)
        qt = (qs * jnp.exp(cum)).astype(BF16)
        mid = cum[C // 2 - 1:C // 2, :]
        qm = (qs * jnp.exp(cum - mid)).astype(BF16)
        kt = (k * jnp.exp(mid - cum)).astype(BF16)
        kl_t = (k * jnp.exp(last - cum)).T.astype(BF16)
        for h in range(GLA_HEADS):
            sl = slice(h * GLA_DK, (h + 1) * GLA_DK)
            vs = slice(h * GLA_DV, (h + 1) * GLA_DV)
            vh = v_ref[rows, vs].astype(BF16)
            state = s_ref[h]
            st_hi, st_lo = _split_bf16(state)
            sc = jnp.where(tril, _dot_nt(qm[:, sl], kt[:, sl]), 0.0).astype(BF16)
            o = _dot(qt[:, sl], st_hi) + _dot(qt[:, sl], st_lo) + _dot(sc, vh)
            decay = jnp.broadcast_to(decay_cols[sl, c:c + 1], state.shape)
            s_ref[h] = decay * state + _dot(kl_t[sl, :], vh)
            g = gr_ref[rows, vs]
            o_ref[rows, vs] = (_rms(o, gn_ref[...]) * (g * _sigmoid(g))).astype(o_ref.dtype)


def _gla_conv(u, wg, bg, gn, conv_w, side_casts=()):
    nj = SEQ // TC_GLA
    row = lambda b, j: b * nj + j
    full = lambda a: pl.BlockSpec(a.shape, lambda b, j: (0, 0))
    cast_in, cast_out, cast_shapes = _side_cast_io(side_casts, BATCH * nj, row)
    t = np.arange(GLA_CUMSUM_SPAN)
    same_chunk = (t[:, None] // GLA_CHUNK) == (t[None, :] // GLA_CHUNK)
    chunk_tril = jnp.asarray(same_chunk & (t[:, None] >= t[None, :]), BF16)
    go, cv, *casts = pl.pallas_call(
        _gla_body,
        grid=(BATCH, nj),
        in_specs=[pl.BlockSpec((TC_GLA, U_COLS), lambda b, j: (row(b, j), 0)),
                  full(wg), full(bg), full(gn), full(chunk_tril), full(conv_w)] + cast_in,
        out_specs=[pl.BlockSpec((TC_GLA, GLA_V), lambda b, j: (row(b, j), 0)),
                   pl.BlockSpec((TC_GLA, CONV_CH), lambda b, j: (row(b, j), 0))] + cast_out,
        out_shape=[jax.ShapeDtypeStruct((TOKENS, GLA_V), BF16), jax.ShapeDtypeStruct((TOKENS, CONV_CH), BF16)]
                  + cast_shapes,
        scratch_shapes=[pltpu.VMEM((GLA_HEADS, GLA_DK, GLA_DV), F32), pltpu.VMEM((8, CONV_CH), F32)],
        compiler_params=_params(("arbitrary", "arbitrary")),
        name="gla_conv",
    )(u, wg, bg, gn, chunk_tril, conv_w, *[w for w, _ in side_casts])
    return go, cv, casts


def _outproj_body(*refs, route):
    (o1, o4, o16, l1, l4, l16, go_ref, cv_ref, x_ref, w_ref, g_ref) = refs[:11]
    perm_ref = refs[-1]
    if route:
        wr_ref, xo_ref, ho_ref, route_ref, counts_ref, carry_ref = refs[11:-1]
    else:
        xo_ref, ho_ref = refs[11:-1]

    def token_order(ref, slot):
        dilation, rows, _ = ref.shape
        if dilation == 1:
            return ref[0].astype(F32)
        tiles = range(ATT_WIDTH // LANES)
        for r in range(dilation):
            val = ref[r].astype(F32)
            for t in tiles:
                perm_ref[slot, t, pl.ds(r, rows, stride=dilation), :] = val[:, t * LANES:(t + 1) * LANES]
        return jnp.concatenate([perm_ref[slot, t] for t in tiles], axis=-1)

    la, lb, lc = token_order(l1, 0), token_order(l4, 0), token_order(l16, 1)
    oa, ob, oc = token_order(o1, 0), token_order(o4, 2), token_order(o16, 3)
    m = jnp.maximum(jnp.maximum(la, lb), lc)
    ea, eb, ec = jnp.exp(la - m), jnp.exp(lb - m), jnp.exp(lc - m)
    att = (ea * oa + eb * ob + ec * oc) / (ea + eb + ec)
    y = (x_ref[...]
         + _dot(att.astype(BF16), w_ref[0:ATT_WIDTH, :])
         + _dot(go_ref[...], w_ref[ATT_WIDTH:ATT_WIDTH + GLA_V, :])
         + _dot(cv_ref[...], w_ref[ATT_WIDTH + GLA_V:MIX_WIDTH, :]))
    xo_ref[...] = y
    hf = _rms(y, g_ref[...])
    ho_ref[...] = hf.astype(ho_ref.dtype)
    if route:
        @pl.when(pl.program_id(0) == 0)
        def _():
            carry_ref[...] = jnp.zeros_like(carry_ref)

        tm = hf.shape[0]
        ne = N_EXPERTS
        hf_hi, hf_lo = _split_bf16(hf)
        part = _dot_nt(wr_ref[0], hf_hi) + _dot_nt(wr_ref[1], hf_lo)
        logits = part[0:ne] + part[ne:2 * ne]
        eidx = lax.broadcasted_iota(jnp.int32, logits.shape, 0).astype(F32)
        v1 = jnp.max(logits, axis=0, keepdims=True)
        i1 = jnp.min(jnp.where(logits == v1, eidx, float(ne)), axis=0, keepdims=True)
        lg2 = jnp.where(eidx == i1, -jnp.inf, logits)
        v2 = jnp.max(lg2, axis=0, keepdims=True)
        i2 = jnp.min(jnp.where(lg2 == v2, eidx, float(ne)), axis=0, keepdims=True)
        e2 = jnp.exp(v2 - v1)
        w1 = 1.0 / (1.0 + e2)
        w2 = e2 * w1
        sel1 = eidx == i1
        sel2 = eidx == i2
        onehot = jnp.where(sel1, 1.0, jnp.where(sel2, 1.0, 0.0))
        tri = (lax.broadcasted_iota(jnp.int32, (tm, tm), 0) <= lax.broadcasted_iota(jnp.int32, (tm, tm), 1))
        onehot16 = jnp.concatenate([onehot, jnp.zeros_like(onehot)], axis=0).astype(BF16)
        csum = _dot(onehot16, jnp.where(tri, 1.0, 0.0).astype(BF16))[0:ne]
        carry = carry_ref[:, 0:1]
        rank = csum - onehot + carry
        r1 = jnp.sum(jnp.where(sel1, rank, 0.0), axis=0, keepdims=True)
        r2 = jnp.sum(jnp.where(sel2, rank, 0.0), axis=0, keepdims=True)
        total = jnp.broadcast_to(carry + csum[:, tm - 1:tm], carry_ref.shape)
        carry_ref[...] = total
        counts_ref[...] = total
        rows = {ROUTE_I1: i1, ROUTE_I2: i2, ROUTE_W1: w1, ROUTE_W2: w2, ROUTE_R1: r1, ROUTE_R2: r2}
        zero = jnp.zeros_like(i1)
        route_ref[...] = jnp.concatenate([rows.get(r, zero) for r in range(8)], axis=0)


def _outproj(att, go, cv, x, w, g, w_router=None):
    route = w_router is not None
    tm = TM_PROJ
    tile = lambda cols: pl.BlockSpec((tm, cols), lambda i: (i, 0))
    full = lambda a: pl.BlockSpec(a.shape, lambda i: (0, 0))
    (o1, l1), (o4, l4), (o16, l16) = att
    args = [o1, o4, o16, l1, l4, l16, go, cv, x, w, g]
    att_specs = [_subseq_spec(d, ATT_WIDTH) for _, d in DILATED_PATTERNS]
    in_specs = att_specs * 2 + [tile(GLA_V), tile(CONV_CH), tile(D_MODEL), full(w), full(g)]
    out_specs = [tile(D_MODEL), tile(D_MODEL)]
    out_shape = [jax.ShapeDtypeStruct((TOKENS, D_MODEL), F32),
                 jax.ShapeDtypeStruct((TOKENS, D_MODEL), F32 if route else BF16)]
    scratch = []
    if route:
        args.append(w_router)
        in_specs.append(pl.BlockSpec(w_router.shape, lambda i: (0, 0, 0)))
        out_specs += [pl.BlockSpec((8, tm), lambda i: (0, i)), pl.BlockSpec((N_EXPERTS, LANES), lambda i: (0, 0))]
        out_shape += [jax.ShapeDtypeStruct((8, TOKENS), F32), jax.ShapeDtypeStruct((N_EXPERTS, LANES), F32)]
        scratch = [pltpu.VMEM((N_EXPERTS, LANES), F32)]
    scratch.append(pltpu.VMEM((4, ATT_WIDTH // LANES, tm, LANES), F32))
    return pl.pallas_call(
        functools.partial(_outproj_body, route=route),
        grid=(TOKENS // tm,),
        in_specs=in_specs,
        out_specs=out_specs,
        out_shape=out_shape,
        scratch_shapes=scratch,
        compiler_params=_params(("arbitrary",) if route else ("parallel",)),
        name="outproj_route" if route else "outproj",
    )(*args)


def _swiglu_accumulate(h, w1_ref, w3_ref, w2_ref, acc_ref, tf):
    for c0 in range(0, tf, FFN_SUB):
        c1 = min(c0 + FFN_SUB, tf)
        a = _dot(h, w1_ref[:, c0:c1])
        b = _dot(h, w3_ref[:, c0:c1])
        act = a * _sigmoid(a) * b
        acc_ref[...] += _dot(act.astype(BF16), w2_ref[c0:c1, :])


def _ffn_body(x_ref, h_ref, w1_ref, w3_ref, w2_ref, *rest, tf):
    n_cast = (len(rest) - 1) // 2
    o_ref = rest[n_cast]

    @pl.when(pl.program_id(1) == 0)
    def _():
        o_ref[...] = x_ref[...]

    _swiglu_accumulate(h_ref[...], w1_ref, w3_ref, w2_ref, o_ref, tf)
    _side_cast(rest[:n_cast], rest[n_cast + 1:])


def _ffn(x, h, w1, w3, w2, *, side_casts=()):
    nj, _, tf = w1.shape
    tm = TM_FFN
    tile = lambda cols: pl.BlockSpec((tm, cols), lambda i, j: (i, 0))
    cast_in, cast_out, cast_shapes = _side_cast_io(side_casts, (TOKENS // tm) * nj, lambda i, j: i * nj + j)
    resident = dict(pipeline_mode=pl.Buffered(1)) if nj == 1 else {}
    res = pl.pallas_call(
        functools.partial(_ffn_body, tf=tf),
        grid=(TOKENS // tm, nj),
        in_specs=[tile(D_MODEL), tile(D_MODEL),
                  pl.BlockSpec((None, D_MODEL, tf), lambda i, j: (j, 0, 0), **resident),
                  pl.BlockSpec((None, D_MODEL, tf), lambda i, j: (j, 0, 0), **resident),
                  pl.BlockSpec((tf, D_MODEL), lambda i, j: (j, 0), **resident)] + cast_in,
        out_specs=[tile(D_MODEL)] + cast_out,
        out_shape=[jax.ShapeDtypeStruct((TOKENS, D_MODEL), F32)] + cast_shapes,
        compiler_params=_params(("parallel", "arbitrary")),
        name="dense_ffn",
    )(x, h, w1, w3, w2, *[w for w, _ in side_casts])
    return res[0], res[1:]


ROW_TILE = D_MODEL // LANES


def _to_row_tiled(dst_ref, lead, val):
    rows = val.shape[0]
    for s in range(ROW_TILE):
        dst_ref[(*lead, pl.ds(s, rows, stride=ROW_TILE), slice(None))] = val[:, s * LANES:(s + 1) * LANES]


def _from_row_tiled(src_ref, lead, rows):
    return jnp.concatenate([src_ref[(*lead, pl.ds(s, rows, stride=ROW_TILE), slice(None))]
                            for s in range(ROW_TILE)], axis=-1)


def _row_tile(idx):
    return pl.ds(pl.multiple_of(idx * ROW_TILE, ROW_TILE), ROW_TILE)


def _dispatch_body(pos_ref, last_tile_ref, h_ref, xs_hbm, stage_ref, zero_ref, sem, zero_sem):
    i = pl.program_id(0)
    n = pl.num_programs(0)
    tm = DISPATCH_CHUNK
    slot = i % 2

    def drain(s):
        for _ in range(2):
            pltpu.make_async_copy(stage_ref.at[s], xs_hbm.at[pl.ds(0, tm * ROW_TILE)], sem.at[s]).wait()

    @pl.when(i == 0)
    def _():
        zero_ref[...] = jnp.zeros_like(zero_ref)

        def zero_copy(e):
            start = pl.multiple_of(last_tile_ref[e] * (TM_MOE * ROW_TILE), TM_MOE * ROW_TILE)
            return pltpu.make_async_copy(zero_ref, xs_hbm.at[pl.ds(start, TM_MOE * ROW_TILE)], zero_sem)

        for e in range(2 * N_EXPERTS):
            @pl.when(last_tile_ref[e] >= 0)
            def _():
                zero_copy(e).start()
        for e in range(2 * N_EXPERTS):
            @pl.when(last_tile_ref[e] >= 0)
            def _():
                zero_copy(e).wait()

    @pl.when(i >= 2)
    def _():
        drain(slot)

    _to_row_tiled(stage_ref, (slot,), h_ref[...])

    def body(t, carry):
        for k in range(2):
            dst = pos_ref[2 * (i * tm + t) + k]
            pltpu.make_async_copy(stage_ref.at[slot, _row_tile(t)], xs_hbm.at[_row_tile(dst)],
                                  sem.at[slot]).start(priority=k)
        return carry
    lax.fori_loop(0, tm, body, 0, unroll=8)

    @pl.when(i == n - 1)
    def _():
        drain(1 - slot)
        drain(slot)


def _dispatch(pos, last_tile, h):
    tm = DISPATCH_CHUNK
    return pl.pallas_call(
        _dispatch_body,
        grid=(TOKENS // tm,),
        in_specs=[pl.BlockSpec(memory_space=pltpu.SMEM),
                  pl.BlockSpec(memory_space=pltpu.SMEM),
                  pl.BlockSpec((tm, D_MODEL), lambda i: (i, 0))],
        out_specs=pl.BlockSpec(memory_space=pl.ANY),
        out_shape=jax.ShapeDtypeStruct((N_SORTED * ROW_TILE, LANES), F32),
        scratch_shapes=[pltpu.VMEM((2, tm * ROW_TILE, LANES), F32),
                        pltpu.VMEM((TM_MOE * ROW_TILE, LANES), F32),
                        pltpu.SemaphoreType.DMA((2,)), pltpu.SemaphoreType.DMA(())],
        compiler_params=_params(("arbitrary",)),
        name="moe_dispatch",
    )(pos, last_tile, h)


def _gffn_body(te_ref, nu_ref, xs_ref, w1_ref, w3_ref, w2_ref, o_ref, hb_ref, acc_ref):
    del te_ref
    i = pl.program_id(0)
    j = pl.program_id(1)

    used = i < nu_ref[0]

    @pl.when(j == 0)
    def _():
        acc_ref[...] = jnp.zeros_like(acc_ref)

    @pl.when(used & (j == 0))
    def _():
        hb_ref[...] = _from_row_tiled(xs_ref, (), TM_MOE).astype(BF16)

    @pl.when(used)
    def _():
        _swiglu_accumulate(hb_ref[...], w1_ref, w3_ref, w2_ref, acc_ref, TF_MOE)

    @pl.when(j == pl.num_programs(1) - 1)
    def _():
        _to_row_tiled(o_ref, (), acc_ref[...])


def _grouped_ffn(tile_expert, n_used, xs, w1, w3, w2):
    nj = FFN_EXPERT // TF_MOE
    col = lambda i, j, nu: jnp.where(i < nu[0], j, nj - 1)
    grid_spec = pltpu.PrefetchScalarGridSpec(
        num_scalar_prefetch=2,
        grid=(N_TILES_MOE, nj),
        in_specs=[
            pl.BlockSpec((TM_MOE * ROW_TILE, LANES), lambda i, j, te, nu: (jnp.minimum(i, nu[0] - 1), 0)),
            pl.BlockSpec((None, None, D_MODEL, TF_MOE), lambda i, j, te, nu: (te[i], col(i, j, nu), 0, 0)),
            pl.BlockSpec((None, None, D_MODEL, TF_MOE), lambda i, j, te, nu: (te[i], col(i, j, nu), 0, 0)),
            pl.BlockSpec((None, TF_MOE, D_MODEL), lambda i, j, te, nu: (te[i], col(i, j, nu), 0)),
        ],
        out_specs=pl.BlockSpec((TM_MOE * ROW_TILE, LANES), lambda i, j, te, nu: (i, 0)),
        scratch_shapes=[pltpu.VMEM((TM_MOE, D_MODEL), BF16), pltpu.VMEM((TM_MOE, D_MODEL), F32)],
    )
    return pl.pallas_call(
        _gffn_body,
        grid_spec=grid_spec,
        out_shape=jax.ShapeDtypeStruct((N_SORTED * ROW_TILE, LANES), F32),
        compiler_params=_params(("arbitrary", "arbitrary")),
        name="moe_ffn",
    )(tile_expert, n_used, xs, w1, w3, w2)


def _combine_body(pos_ref, x_ref, gate_ref, g_ref, ys_hbm, o_ref, buf_ref, sem):
    i = pl.program_id(0)
    n = pl.num_programs(0)
    tm = TM_COMBINE

    def issue(tile, slot):
        def body(t, carry):
            for k in range(2):
                src = pos_ref[2 * (tile * tm + t) + k]
                pltpu.make_async_copy(ys_hbm.at[_row_tile(src)], buf_ref.at[slot, k, _row_tile(t)],
                                      sem.at[slot]).start(priority=k)
            return carry
        lax.fori_loop(0, tm, body, 0, unroll=8)

    @pl.when(i == 0)
    def _():
        issue(0, 0)

    @pl.when(i + 1 < n)
    def _():
        issue(i + 1, (i + 1) % 2)

    slot = i % 2
    for k in range(2):
        pltpu.make_async_copy(ys_hbm.at[pl.ds(0, tm * ROW_TILE)], buf_ref.at[slot, k], sem.at[slot]).wait()
    w1 = gate_ref[:, 0:1]
    w2 = gate_ref[:, 1:2]
    y = x_ref[...] + w1 * _from_row_tiled(buf_ref, (slot, 0), tm) + w2 * _from_row_tiled(buf_ref, (slot, 1), tm)
    o_ref[...] = _rms(y, g_ref[...])


def _combine(pos, x, gates, g, ys):
    tm = TM_COMBINE
    return pl.pallas_call(
        _combine_body,
        grid=(TOKENS // tm,),
        in_specs=[pl.BlockSpec(memory_space=pltpu.SMEM),
                  pl.BlockSpec((tm, D_MODEL), lambda i: (i, 0)),
                  pl.BlockSpec((tm, 2), lambda i: (i, 0)),
                  pl.BlockSpec((1, D_MODEL), lambda i: (0, 0)),
                  pl.BlockSpec(memory_space=pl.ANY)],
        out_specs=pl.BlockSpec((tm, D_MODEL), lambda i: (i, 0)),
        out_shape=jax.ShapeDtypeStruct((TOKENS, D_MODEL), F32),
        scratch_shapes=[pltpu.VMEM((2, 2, tm * ROW_TILE, LANES), F32), pltpu.SemaphoreType.DMA((2,))],
        compiler_params=_params(("arbitrary",)),
        name="moe_combine",
    )(pos, x, gates, g, ys)


def _routing_tables(route, counts):
    cnt = counts[:, 0].astype(jnp.int32)
    tiles = (cnt + TM_MOE - 1) // TM_MOE
    tile_end = jnp.cumsum(tiles)
    tile_start = tile_end - tiles
    n_used = tile_end[-1]
    expert = route[ROUTE_I1:ROUTE_I2 + 1].astype(jnp.int32)
    rank = route[ROUTE_R1:ROUTE_R2 + 1].astype(jnp.int32)
    group_start = jnp.sum(jnp.where(expert[..., None] == jnp.arange(N_EXPERTS), tile_start * TM_MOE, 0), axis=-1)
    pos = (group_start + rank).T
    tile_id = jnp.minimum(jnp.arange(N_TILES_MOE, dtype=jnp.int32), n_used - 1)
    tile_expert = jnp.sum(tile_id[:, None] >= tile_end[None, :], axis=1).astype(jnp.int32)
    last_tile = jnp.where(tiles > 0, tile_end - 1, -1)
    spare = n_used + jnp.arange(N_EXPERTS)
    zero_tiles = jnp.concatenate([last_tile, jnp.where(spare < N_TILES_MOE, spare, -1)]).astype(jnp.int32)
    return pos.reshape(2 * TOKENS), tile_expert, n_used.reshape(1), zero_tiles


def _moe(x, h, route, counts, w1, w3, w2, g_final):
    pos, tile_expert, n_used, last_tile = _routing_tables(route, counts)
    xs = _dispatch(pos, last_tile, h)
    ys = _grouped_ffn(tile_expert, n_used, xs, w1, w3, w2)
    return _combine(pos, x, route[ROUTE_W1:ROUTE_W2 + 1].T, g_final, ys)


def _prep_w_in(w):
    aq, ak, av, gq, gk, gv, gr, glr, c_in, c_b, c_c = jnp.split(w, np.cumsum(SPLIT_SIZES)[:-1].tolist(), axis=2)
    pad = jnp.zeros(w.shape[:2] + (LANES - GLA_RANK,), w.dtype)
    return jnp.concatenate([aq, ak, av, gv, gr, gq, gk, c_in, c_b, c_c, glr, pad], axis=2).astype(BF16)


def _prep_router(w):
    wt = w.T
    hi = wt.astype(BF16)
    lo = (wt - hi.astype(F32)).astype(BF16)
    return jnp.stack([jnp.concatenate([hi, lo]), jnp.concatenate([hi, jnp.zeros_like(hi)])])


def kernel(x, w_mix_in, w_mix_out, g_mix, rel_bias, gla_w_gate, gla_b_gate, gla_g_norm, conv_w,
           g_ffn, ffn_w1, ffn_w3, ffn_w2, moe_router, moe_w1, moe_w3, moe_w2, g_final):
    assert DEPTH == 2
    x = x.reshape(TOKENS, D_MODEL)
    up_job = lambda w, tf: (w, w.shape[-1] // tf)
    down = lambda c: c.reshape(c.shape[0], c.shape[2], c.shape[3])
    experts = {}
    w_in = _prep_w_in(w_mix_in)
    for layer in range(DEPTH):
        u, qkvs, casts = _inproj(x, g_mix[layer].reshape(1, D_MODEL), w_in, layer,
                                 side_casts=[up_job(moe_w1[0], TF_MOE)] if layer == 1 else [])
        if layer == 1:
            experts["w1"] = casts[0]
        dense_jobs = [up_job(ffn_w1[:1], TF_DENSE), up_job(ffn_w3[:1], TF_DENSE), (ffn_w2[:1], 1)] if layer == 0 else []
        att, dense_bf16 = _attention(qkvs, rel_bias, dense_jobs)
        if layer == 0:
            ffn_bf16 = (dense_bf16[0][0], dense_bf16[1][0], down(dense_bf16[2])[0])
        wg = jnp.pad(gla_w_gate[layer], ((0, LANES - GLA_RANK), (0, 0)))
        go, cv, _ = _gla_conv(u, wg, gla_b_gate[layer].reshape(1, GLA_QK), gla_g_norm[layer].reshape(1, GLA_DV),
                              jnp.pad(conv_w[layer], ((0, 8 - CONV_WIDTH), (0, 0))))
        w_out = w_mix_out[layer].astype(BF16)
        g2 = g_ffn[layer].reshape(1, D_MODEL)
        i = layer // 2
        if layer % 2 == 0:
            x, h = _outproj(att, go, cv, x, w_out, g2)
            x, (experts["w3"], cast) = _ffn(x, h, *ffn_bf16, side_casts=[up_job(moe_w3[i], TF_MOE), (moe_w2[i], 1)])
            experts["w2"] = down(cast)
        else:
            x, h, route, counts = _outproj(att, go, cv, x, w_out, g2, _prep_router(moe_router[i]))
            x = _moe(x, h, route, counts, experts["w1"], experts["w3"], experts["w2"], g_final.reshape(1, D_MODEL))
    return x.reshape(BATCH, SEQ, D_MODEL)
```

```python
import functools
import math

import jax
import jax.numpy as jnp
import numpy as np
from jax import lax
from jax.experimental import pallas as pl
from jax.experimental.pallas import tpu as pltpu

F32 = jnp.float32
BF16 = jnp.bfloat16

D_MODEL = 1024
BATCH = 8
SEQ = 2048
TOKENS = BATCH * SEQ
DEPTH = 2
EPS = 1e-6

HEAD_DIM = 64
ATT_HEADS = 4
ATT_WIDTH = ATT_HEADS * HEAD_DIM
DILATED_PATTERNS = ((128, 1), (512, 4), (2048, 16))
ATT_BLOCK = 128
REL_BUCKETS = 32
REL_MAX_DISTANCE = 2048

GLA_HEADS = 4
GLA_DK = 64
GLA_DV = 128
GLA_RANK = 16
GLA_CHUNK = 64
GLA_QK = GLA_HEADS * GLA_DK
GLA_V = GLA_HEADS * GLA_DV

CONV_CH = 256
CONV_WIDTH = 3
MIX_WIDTH = ATT_WIDTH + GLA_V + CONV_CH

SPLIT_SIZES = (ATT_WIDTH, ATT_WIDTH, ATT_WIDTH, GLA_QK, GLA_QK, GLA_V, GLA_V, GLA_RANK,
               CONV_CH, CONV_CH, CONV_CH)

FFN_DENSE = 2816
N_EXPERTS = 8
FFN_EXPERT = 3584

LANES = 128
MXU_WIDTH = 256
VMEM_LIMIT = 56 * 1024 * 1024

QKV_COLS = 3 * ATT_WIDTH
COL_GV, COL_GR, COL_GQ, COL_GK = 0, 512, 1024, 1280
COL_CIN, COL_CB, COL_CC, COL_GLR = 1536, 1792, 2048, 2304
U_COLS = COL_GLR + LANES

NEG_BIG = -1e30

ATT_UNROLL = 15
TM_PROJ = 512
TC_GLA = 1024
GLA_CUMSUM_SPAN = 256
TM_FFN = 512
TF_DENSE = 2816
FFN_SUB = 256
TM_MOE = 512
TF_MOE = 1792
N_TILES_MOE = 2 * TOKENS // TM_MOE + N_EXPERTS
N_SORTED = N_TILES_MOE * TM_MOE
DISPATCH_CHUNK = 512
TM_COMBINE = 512

ROUTE_I1, ROUTE_I2, ROUTE_W1, ROUTE_W2, ROUTE_R1, ROUTE_R2 = range(6)


def _params(sem):
    return pltpu.CompilerParams(dimension_semantics=sem, vmem_limit_bytes=VMEM_LIMIT)


def _split_bf16(a):
    hi = a.astype(BF16)
    lo = (a - hi.astype(F32)).astype(BF16)
    return hi, lo


def _dot(a, b):
    return jnp.dot(a, b, preferred_element_type=F32)


def _dot3(a, b):
    a_hi, a_lo = _split_bf16(a)
    b_hi, b_lo = _split_bf16(b)
    return _dot(a_hi, b_hi) + _dot(a_lo, b_hi) + _dot(a_hi, b_lo)


def _dot_nt(a, b):
    return lax.dot_general(a, b, (((1,), (1,)), ((), ())), preferred_element_type=F32)


def _rms(x, g):
    ms = jnp.mean(x * x, axis=-1, keepdims=True)
    return x * lax.rsqrt(ms + EPS) * g


def _sigmoid(x):
    return 1.0 / (1.0 + jnp.exp(-x))


def _side_cast(srcs, dsts):
    for src, dst in zip(srcs, dsts):
        width = dst.shape[-1]
        for s in range(dst.shape[0]):
            dst[s] = src[:, s * width:(s + 1) * width].astype(dst.dtype)


def _side_cast_io(jobs, steps, step_of):
    in_specs, out_specs, out_shapes = [], [], []
    for w, splits in jobs:
        g, r, c = w.shape
        rb = g * r // steps
        per_group = r // rb
        in_specs.append(pl.BlockSpec(
            (None, rb, c), lambda *ids, pg=per_group: (step_of(*ids) // pg, step_of(*ids) % pg, 0)))
        out_specs.append(pl.BlockSpec(
            (None, splits, rb, c // splits),
            lambda *ids, pg=per_group: (step_of(*ids) // pg, 0, step_of(*ids) % pg, 0)))
        out_shapes.append(jax.ShapeDtypeStruct((g, splits, r, c // splits), BF16))
    return in_specs, out_specs, out_shapes


def _inproj_body(x_ref, g_ref, w_ref, *rest):
    n_pat = len(DILATED_PATTERNS)
    n_cast = (len(rest) - n_pat - 2) // 2
    o_ref = rest[n_cast]
    qkv_refs = rest[n_cast + 1:n_cast + 1 + n_pat]
    qkv_f32 = rest[-1]
    _side_cast(rest[:n_cast], rest[n_cast + 1 + n_pat:-1])
    h = _rms(x_ref[...], g_ref[...]).astype(BF16)
    for c0 in range(0, QKV_COLS, MXU_WIDTH):
        res = _dot(h, w_ref[:, c0:c0 + MXU_WIDTH])
        for t in range(MXU_WIDTH // LANES):
            qkv_f32[c0 // LANES + t] = res[:, t * LANES:(t + 1) * LANES]
    for (_, dilation), ref in zip(DILATED_PATTERNS, qkv_refs):
        for r in range(dilation):
            rows = pl.ds(r, TM_PROJ // dilation, stride=dilation)
            ref[r] = jnp.concatenate([qkv_f32[t, rows, :] for t in range(QKV_COLS // LANES)],
                                     axis=-1).astype(ref.dtype)
    for c0 in range(0, U_COLS, MXU_WIDTH):
        c1 = min(c0 + MXU_WIDTH, U_COLS)
        o_ref[:, c0:c1] = _dot(h, w_ref[:, QKV_COLS + c0:QKV_COLS + c1])


def _subseq_spec(dilation, cols):
    tiles = SEQ // TM_PROJ
    return pl.BlockSpec((None, dilation, TM_PROJ // dilation, cols), lambda i: (i // tiles, 0, i % tiles, 0))


def _inproj(x, g, w_all, layer, side_casts=()):
    n_pat = len(DILATED_PATTERNS)
    qkv_shapes = [jax.ShapeDtypeStruct((BATCH, d, SEQ // d, QKV_COLS), BF16) for _, d in DILATED_PATTERNS]
    cast_in, cast_out, cast_shapes = _side_cast_io(side_casts, TOKENS // TM_PROJ, lambda i: i)
    u, *rest = pl.pallas_call(
        _inproj_body,
        grid=(TOKENS // TM_PROJ,),
        in_specs=[
            pl.BlockSpec((TM_PROJ, D_MODEL), lambda i: (i, 0)),
            pl.BlockSpec((1, D_MODEL), lambda i: (0, 0)),
            pl.BlockSpec((None, D_MODEL, QKV_COLS + U_COLS), lambda i: (layer, 0, 0)),
        ] + cast_in,
        out_specs=[pl.BlockSpec((TM_PROJ, U_COLS), lambda i: (i, 0))]
                  + [_subseq_spec(d, QKV_COLS) for _, d in DILATED_PATTERNS] + cast_out,
        out_shape=[jax.ShapeDtypeStruct((TOKENS, U_COLS), F32)] + qkv_shapes + cast_shapes,
        scratch_shapes=[pltpu.VMEM((QKV_COLS // LANES, TM_PROJ, LANES), F32)],
        compiler_params=_params(("parallel",)),
        name="inproj",
    )(x, g, w_all, *[w for w, _ in side_casts])
    return u, rest[:n_pat], rest[n_pat:]


def _rel_bucket(dist):
    max_exact = REL_BUCKETS // 2
    d = jnp.maximum(dist, 0)
    log_ratio = jnp.log(jnp.maximum(d, 1).astype(F32) / max_exact) / math.log(REL_MAX_DISTANCE / max_exact)
    large = jnp.minimum(max_exact + (log_ratio * (REL_BUCKETS - max_exact)).astype(jnp.int32), REL_BUCKETS - 1)
    return jnp.where(d < max_exact, d, large)


def _bucket_table(window, dilation):
    span = window // dilation
    qi = jnp.arange(ATT_BLOCK)[:, None]
    kj = jnp.arange(2 * ATT_BLOCK)[None, :]
    sub_dist = qi - kj + ATT_BLOCK
    band = (sub_dist >= 0) & (sub_dist <= span)
    return jnp.where(band, _rel_bucket(sub_dist * dilation), -1).astype(jnp.int32)


def _attn_body(rb_ref, bidx_ref, qkv_ref, *rest, sub_blocks, unroll):
    n_cast = (len(rest) - 3) // 2
    o_ref, lse_ref = rest[n_cast:n_cast + 2]
    bias_ref = rest[-1]
    _side_cast(rest[:n_cast], rest[n_cast + 2:-1])
    nblk = SEQ // ATT_BLOCK

    @pl.when(pl.program_id(0) == 0)
    def _():
        bidx = bidx_ref[...]
        in_prev = lax.broadcasted_iota(jnp.int32, bidx.shape, 1) < ATT_BLOCK
        for h in range(ATT_HEADS):
            acc = jnp.full(bidx.shape, NEG_BIG, F32)
            for b in range(REL_BUCKETS):
                acc = jnp.where(bidx == b, rb_ref[b, h], acc)
            bias_ref[0, h] = acc
            bias_ref[1, h] = jnp.where(in_prev, NEG_BIG, acc)
            bias_ref[2, h] = jnp.concatenate([acc[:, ATT_BLOCK:], jnp.full_like(acc[:, ATT_BLOCK:], NEG_BIG)], axis=1)

    def block(n, first):
        if first:
            rows, krows, variant = slice(0, ATT_BLOCK), slice(0, 2 * ATT_BLOCK), 2
        else:
            r0 = pl.multiple_of(n * ATT_BLOCK, ATT_BLOCK)
            rows = pl.ds(r0, ATT_BLOCK)
            krows = pl.ds(r0 - ATT_BLOCK, 2 * ATT_BLOCK)
            if sub_blocks == nblk:
                variant = 0
            elif sub_blocks == 1:
                variant = 1
            else:
                variant = jnp.where(n % sub_blocks == 0, 1, 0)
        q = qkv_ref[rows, 0:ATT_WIDTH]
        kk = qkv_ref[krows, ATT_WIDTH:2 * ATT_WIDTH]
        vv = qkv_ref[krows, 2 * ATT_WIDTH:3 * ATT_WIDTH]
        q = q * jnp.asarray(HEAD_DIM ** -0.5, BF16)
        head_of_lane = lax.broadcasted_iota(jnp.int32, (ATT_BLOCK, ATT_WIDTH), 1) // HEAD_DIM
        ones = jnp.ones((kk.shape[0], LANES), BF16)
        num = den = mx = None
        for h in range(ATT_HEADS):
            mine = head_of_lane == h
            bias = bias_ref[variant, h]
            s = _dot_nt(jnp.where(mine, q, jnp.zeros_like(q)), kk) + bias
            m = jnp.max(s, axis=-1, keepdims=True)
            p = jnp.exp(s - m).astype(BF16)
            num_h = _dot(p, vv)
            den_h = jnp.tile(_dot(p, ones), (1, ATT_WIDTH // LANES))
            m_h = jnp.broadcast_to(m, (ATT_BLOCK, ATT_WIDTH))
            num = num_h if h == 0 else jnp.where(mine, num_h, num)
            den = den_h if h == 0 else jnp.where(mine, den_h, den)
            mx = m_h if h == 0 else jnp.where(mine, m_h, mx)
        o_ref[rows, :] = (num / den).astype(o_ref.dtype)
        lse_ref[rows, :] = mx + jnp.log(den)

    block(0, True)

    def loop_body(n, carry):
        block(n, False)
        return carry
    lax.fori_loop(1, nblk, loop_body, 0, unroll=unroll)


def _attention_pattern(ua, rel_bias, window, dilation, side_casts=()):
    L = SEQ // dilation
    shape = (BATCH, dilation, L, ATT_WIDTH)
    qkv_spec = pl.BlockSpec((None, SEQ, QKV_COLS), lambda b: (b, 0, 0))
    out_spec = pl.BlockSpec((None, SEQ, ATT_WIDTH), lambda b: (b, 0, 0))
    cast_in, cast_out, cast_shapes = _side_cast_io(side_casts, BATCH, lambda b: b)
    o, lse, *casts = pl.pallas_call(
        functools.partial(_attn_body, sub_blocks=L // ATT_BLOCK, unroll=ATT_UNROLL),
        grid=(BATCH,),
        in_specs=[
            pl.BlockSpec(memory_space=pltpu.SMEM),
            pl.BlockSpec((ATT_BLOCK, 2 * ATT_BLOCK), lambda b: (0, 0)),
            qkv_spec,
        ] + cast_in,
        out_specs=[out_spec, out_spec] + cast_out,
        out_shape=[jax.ShapeDtypeStruct((BATCH, SEQ, ATT_WIDTH), BF16),
                   jax.ShapeDtypeStruct((BATCH, SEQ, ATT_WIDTH), F32)] + cast_shapes,
        scratch_shapes=[pltpu.VMEM((3, ATT_HEADS, ATT_BLOCK, 2 * ATT_BLOCK), F32)],
        compiler_params=_params(("arbitrary",)),
        name=f"attn_d{dilation}",
    )(rel_bias, _bucket_table(window, dilation), ua.reshape(BATCH, SEQ, QKV_COLS), *[w for w, _ in side_casts])
    return (o.reshape(shape), lse.reshape(shape)), casts


def _attention(qkvs, rel_bias, side_casts=()):
    side_casts = list(side_casts) + [None] * (len(DILATED_PATTERNS) - len(side_casts))
    res = [_attention_pattern(ua, rel_bias, window, dilation, [] if w is None else [w])
           for ua, (window, dilation), w in zip(qkvs, DILATED_PATTERNS, side_casts)]
    return [r[0] for r in res], [c for r in res for c in r[1]]


def _short_conv(cin_ref, cb_ref, cc_ref, w_ref, o_ref, tail_ref):
    uu = cc_ref[...] * cin_ref[...]
    t = lax.broadcasted_iota(jnp.int32, uu.shape, 0)
    y = uu * w_ref[CONV_WIDTH - 1:CONV_WIDTH, :]
    for shift in range(1, CONV_WIDTH):
        prev = pltpu.roll(uu, shift, axis=0)
        for r in range(shift):
            prev = jnp.where(t == r, tail_ref[8 - shift + r:8 - shift + r + 1, :], prev)
        y = y + prev * w_ref[CONV_WIDTH - 1 - shift:CONV_WIDTH - shift, :]
    o_ref[...] = (cb_ref[...] * y).astype(o_ref.dtype)
    tail_ref[...] = uu[uu.shape[0] - 8:, :]


def _gla_body(u_ref, wg_ref, bg_ref, gn_ref, ctril_ref, cw_ref, *rest):
    cols = lambda col, width: u_ref.at[:, col:col + width]
    q_ref, k_ref = cols(COL_GQ, GLA_QK), cols(COL_GK, GLA_QK)
    v_ref, gr_ref, glr_ref = cols(COL_GV, GLA_V), cols(COL_GR, GLA_V), cols(COL_GLR, LANES)
    cin_ref, cb_ref, cc_ref = cols(COL_CIN, CONV_CH), cols(COL_CB, CONV_CH), cols(COL_CC, CONV_CH)
    _gla_tile(q_ref, k_ref, v_ref, gr_ref, glr_ref, wg_ref, bg_ref, gn_ref, ctril_ref,
              cin_ref, cb_ref, cc_ref, cw_ref, *rest)


def _gla_tile(q_ref, k_ref, v_ref, gr_ref, glr_ref, wg_ref, bg_ref, gn_ref, ctril_ref,
              cin_ref, cb_ref, cc_ref, cw_ref, *rest):
    n_cast = (len(rest) - 4) // 2
    o_ref, cv_ref = rest[n_cast:n_cast + 2]
    s_ref, tail_ref = rest[-2:]
    _side_cast(rest[:n_cast], rest[n_cast + 2:-2])

    @pl.when(pl.program_id(1) == 0)
    def _():
        s_ref[...] = jnp.zeros_like(s_ref)
        tail_ref[...] = jnp.zeros_like(tail_ref)

    _short_conv(cin_ref, cb_ref, cc_ref, cw_ref, cv_ref, tail_ref)

    C = GLA_CHUNK
    row = lax.broadcasted_iota(jnp.int32, (C, C), 0)
    col = lax.broadcasted_iota(jnp.int32, (C, C), 1)
    tril = row >= col
    n_chunks = TC_GLA // C

    xg = _dot3(glr_ref[...], wg_ref[...]) + bg_ref[...]
    la_all = (jnp.minimum(xg, 0.0) - jnp.log(1.0 + jnp.exp(-jnp.abs(xg)))) * (1.0 / 16.0)

    la_hi, la_lo = _split_bf16(la_all)
    span = ctril_ref.shape[0]
    cum_all = jnp.concatenate(
        [_dot(ctril_ref[...], la_hi[r0:r0 + span]) + _dot(ctril_ref[...], la_lo[r0:r0 + span])
         for r0 in range(0, TC_GLA, span)], axis=0)
    totals = jnp.concatenate([cum_all[(c + 1) * C - 1:(c + 1) * C, :] for c in range(n_chunks)]
                             + [jnp.zeros((LANES - n_chunks, GLA_QK), F32)], axis=0)
    decay_cols = jnp.exp(totals.T)

    for c in range(n_chunks):
        rows = slice(c * C, (c + 1) * C)
        cum = cum_all[rows]
        last = cum[C - 1:C, :]
        q = q_ref[rows, :]
        k = k_ref[rows, :]
        qs = q * (GLA_DK ** -0.5)
        qt = (qs * jnp.exp(cum)).astype(BF16)
        mid = cum[C // 2 - 1:C // 2, :]
        qm = (qs * jnp.exp(cum - mid)).astype(BF16)
        kt = (k * jnp.exp(mid - cum)).astype(BF16)
        kl_t = (k * jnp.exp(last - cum)).T.astype(BF16)
        for h in range(GLA_HEADS):
            sl = slice(h * GLA_DK, (h + 1) * GLA_DK)
            vs = slice(h * GLA_DV, (h + 1) * GLA_DV)
            vh = v_ref[rows, vs].astype(BF16)
            state = s_ref[h]
            st_hi, st_lo = _split_bf16(state)
            sc = jnp.where(tril, _dot_nt(qm[:, sl], kt[:, sl]), 0.0).astype(BF16)
            o = _dot(qt[:, sl], st_hi) + _dot(qt[:, sl], st_lo) + _dot(sc, vh)
            decay = jnp.broadcast_to(decay_cols[sl, c:c + 1], state.shape)
            s_ref[h] = decay * state + _dot(kl_t[sl, :], vh)
            g = gr_ref[rows, vs]
            o_ref[rows, vs] = (_rms(o, gn_ref[...]) * (g * _sigmoid(g))).astype(o_ref.dtype)


def _gla_conv(u, wg, bg, gn, conv_w, side_casts=()):
    nj = SEQ // TC_GLA
    row = lambda b, j: b * nj + j
    full = lambda a: pl.BlockSpec(a.shape, lambda b, j: (0, 0))
    cast_in, cast_out, cast_shapes = _side_cast_io(side_casts, BATCH * nj, row)
    t = np.arange(GLA_CUMSUM_SPAN)
    same_chunk = (t[:, None] // GLA_CHUNK) == (t[None, :] // GLA_CHUNK)
    chunk_tril = jnp.asarray(same_chunk & (t[:, None] >= t[None, :]), BF16)
    go, cv, *casts = pl.pallas_call(
        _gla_body,
        grid=(BATCH, nj),
        in_specs=[pl.BlockSpec((TC_GLA, U_COLS), lambda b, j: (row(b, j), 0)),
                  full(wg), full(bg), full(gn), full(chunk_tril), full(conv_w)] + cast_in,
        out_specs=[pl.BlockSpec((TC_GLA, GLA_V), lambda b, j: (row(b, j), 0)),
                   pl.BlockSpec((TC_GLA, CONV_CH), lambda b, j: (row(b, j), 0))] + cast_out,
        out_shape=[jax.ShapeDtypeStruct((TOKENS, GLA_V), BF16), jax.ShapeDtypeStruct((TOKENS, CONV_CH), BF16)]
                  + cast_shapes,
        scratch_shapes=[pltpu.VMEM((GLA_HEADS, GLA_DK, GLA_DV), F32), pltpu.VMEM((8, CONV_CH), F32)],
        compiler_params=_params(("arbitrary", "arbitrary")),
        name="gla_conv",
    )(u, wg, bg, gn, chunk_tril, conv_w, *[w for w, _ in side_casts])
    return go, cv, casts


def _outproj_body(*refs, route):
    (o1, o4, o16, l1, l4, l16, go_ref, cv_ref, x_ref, w_ref, g_ref) = refs[:11]
    perm_ref = refs[-1]
    if route:
        wr_ref, xo_ref, ho_ref, route_ref, counts_ref, carry_ref = refs[11:-1]
    else:
        xo_ref, ho_ref = refs[11:-1]

    def token_order(ref, slot):
        dilation, rows, _ = ref.shape
        if dilation == 1:
            return ref[0].astype(F32)
        tiles = range(ATT_WIDTH // LANES)
        for r in range(dilation):
            val = ref[r].astype(F32)
            for t in tiles:
                perm_ref[slot, t, pl.ds(r, rows, stride=dilation), :] = val[:, t * LANES:(t + 1) * LANES]
        return jnp.concatenate([perm_ref[slot, t] for t in tiles], axis=-1)

    la, lb, lc = token_order(l1, 0), token_order(l4, 0), token_order(l16, 1)
    oa, ob, oc = token_order(o1, 0), token_order(o4, 2), token_order(o16, 3)
    m = jnp.maximum(jnp.maximum(la, lb), lc)
    ea, eb, ec = jnp.exp(la - m), jnp.exp(lb - m), jnp.exp(lc - m)
    att = (ea * oa + eb * ob + ec * oc) / (ea + eb + ec)
    y = (x_ref[...]
         + _dot(att.astype(BF16), w_ref[0:ATT_WIDTH, :])
         + _dot(go_ref[...], w_ref[ATT_WIDTH:ATT_WIDTH + GLA_V, :])
         + _dot(cv_ref[...], w_ref[ATT_WIDTH + GLA_V:MIX_WIDTH, :]))
    xo_ref[...] = y
    hf = _rms(y, g_ref[...])
    ho_ref[...] = hf.astype(ho_ref.dtype)
    if route:
        @pl.when(pl.program_id(0) == 0)
        def _():
            carry_ref[...] = jnp.zeros_like(carry_ref)

        tm = hf.shape[0]
        ne = N_EXPERTS
        hf_hi, hf_lo = _split_bf16(hf)
        part = _dot_nt(wr_ref[0], hf_hi) + _dot_nt(wr_ref[1], hf_lo)
        logits = part[0:ne] + part[ne:2 * ne]
        eidx = lax.broadcasted_iota(jnp.int32, logits.shape, 0).astype(F32)
        v1 = jnp.max(logits, axis=0, keepdims=True)
        i1 = jnp.min(jnp.where(logits == v1, eidx, float(ne)), axis=0, keepdims=True)
        lg2 = jnp.where(eidx == i1, -jnp.inf, logits)
        v2 = jnp.max(lg2, axis=0, keepdims=True)
        i2 = jnp.min(jnp.where(lg2 == v2, eidx, float(ne)), axis=0, keepdims=True)
        e2 = jnp.exp(v2 - v1)
        w1 = 1.0 / (1.0 + e2)
        w2 = e2 * w1
        sel1 = eidx == i1
        sel2 = eidx == i2
        onehot = jnp.where(sel1, 1.0, jnp.where(sel2, 1.0, 0.0))
        tri = (lax.broadcasted_iota(jnp.int32, (tm, tm), 0) <= lax.broadcasted_iota(jnp.int32, (tm, tm), 1))
        onehot16 = jnp.concatenate([onehot, jnp.zeros_like(onehot)], axis=0).astype(BF16)
        csum = _dot(onehot16, jnp.where(tri, 1.0, 0.0).astype(BF16))[0:ne]
        carry = carry_ref[:, 0:1]
        rank = csum - onehot + carry
        r1 = jnp.sum(jnp.where(sel1, rank, 0.0), axis=0, keepdims=True)
        r2 = jnp.sum(jnp.where(sel2, rank, 0.0), axis=0, keepdims=True)
        total = jnp.broadcast_to(carry + csum[:, tm - 1:tm], carry_ref.shape)
        carry_ref[...] = total
        counts_ref[...] = total
        rows = {ROUTE_I1: i1, ROUTE_I2: i2, ROUTE_W1: w1, ROUTE_W2: w2, ROUTE_R1: r1, ROUTE_R2: r2}
        zero = jnp.zeros_like(i1)
        route_ref[...] = jnp.concatenate([rows.get(r, zero) for r in range(8)], axis=0)


def _outproj(att, go, cv, x, w, g, w_router=None):
    route = w_router is not None
    tm = TM_PROJ
    tile = lambda cols: pl.BlockSpec((tm, cols), lambda i: (i, 0))
    full = lambda a: pl.BlockSpec(a.shape, lambda i: (0, 0))
    (o1, l1), (o4, l4), (o16, l16) = att
    args = [o1, o4, o16, l1, l4, l16, go, cv, x, w, g]
    att_specs = [_subseq_spec(d, ATT_WIDTH) for _, d in DILATED_PATTERNS]
    in_specs = att_specs * 2 + [tile(GLA_V), tile(CONV_CH), tile(D_MODEL), full(w), full(g)]
    out_specs = [tile(D_MODEL), tile(D_MODEL)]
    out_shape = [jax.ShapeDtypeStruct((TOKENS, D_MODEL), F32),
                 jax.ShapeDtypeStruct((TOKENS, D_MODEL), F32 if route else BF16)]
    scratch = []
    if route:
        args.append(w_router)
        in_specs.append(pl.BlockSpec(w_router.shape, lambda i: (0, 0, 0)))
        out_specs += [pl.BlockSpec((8, tm), lambda i: (0, i)), pl.BlockSpec((N_EXPERTS, LANES), lambda i: (0, 0))]
        out_shape += [jax.ShapeDtypeStruct((8, TOKENS), F32), jax.ShapeDtypeStruct((N_EXPERTS, LANES), F32)]
        scratch = [pltpu.VMEM((N_EXPERTS, LANES), F32)]
    scratch.append(pltpu.VMEM((4, ATT_WIDTH // LANES, tm, LANES), F32))
    return pl.pallas_call(
        functools.partial(_outproj_body, route=route),
        grid=(TOKENS // tm,),
        in_specs=in_specs,
        out_specs=out_specs,
        out_shape=out_shape,
        scratch_shapes=scratch,
        compiler_params=_params(("arbitrary",) if route else ("parallel",)),
        name="outproj_route" if route else "outproj",
    )(*args)


def _swiglu_accumulate(h, w1_ref, w3_ref, w2_ref, acc_ref, tf):
    for c0 in range(0, tf, FFN_SUB):
        c1 = min(c0 + FFN_SUB, tf)
        a = _dot(h, w1_ref[:, c0:c1])
        b = _dot(h, w3_ref[:, c0:c1])
        act = a * _sigmoid(a) * b
        acc_ref[...] += _dot(act.astype(BF16), w2_ref[c0:c1, :])


def _ffn_body(x_ref, h_ref, w1_ref, w3_ref, w2_ref, *rest, tf):
    n_cast = (len(rest) - 1) // 2
    o_ref = rest[n_cast]

    @pl.when(pl.program_id(1) == 0)
    def _():
        o_ref[...] = x_ref[...]

    _swiglu_accumulate(h_ref[...], w1_ref, w3_ref, w2_ref, o_ref, tf)
    _side_cast(rest[:n_cast], rest[n_cast + 1:])


def _ffn(x, h, w1, w3, w2, *, side_casts=()):
    nj, _, tf = w1.shape
    tm = TM_FFN
    tile = lambda cols: pl.BlockSpec((tm, cols), lambda i, j: (i, 0))
    cast_in, cast_out, cast_shapes = _side_cast_io(side_casts, (TOKENS // tm) * nj, lambda i, j: i * nj + j)
    resident = dict(pipeline_mode=pl.Buffered(1)) if nj == 1 else {}
    res = pl.pallas_call(
        functools.partial(_ffn_body, tf=tf),
        grid=(TOKENS // tm, nj),
        in_specs=[tile(D_MODEL), tile(D_MODEL),
                  pl.BlockSpec((None, D_MODEL, tf), lambda i, j: (j, 0, 0), **resident),
                  pl.BlockSpec((None, D_MODEL, tf), lambda i, j: (j, 0, 0), **resident),
                  pl.BlockSpec((tf, D_MODEL), lambda i, j: (j, 0), **resident)] + cast_in,
        out_specs=[tile(D_MODEL)] + cast_out,
        out_shape=[jax.ShapeDtypeStruct((TOKENS, D_MODEL), F32)] + cast_shapes,
        compiler_params=_params(("parallel", "arbitrary")),
        name="dense_ffn",
    )(x, h, w1, w3, w2, *[w for w, _ in side_casts])
    return res[0], res[1:]


ROW_TILE = D_MODEL // LANES


def _to_row_tiled(dst_ref, lead, val):
    rows = val.shape[0]
    for s in range(ROW_TILE):
        dst_ref[(*lead, pl.ds(s, rows, stride=ROW_TILE), slice(None))] = val[:, s * LANES:(s + 1) * LANES]


def _from_row_tiled(src_ref, lead, rows):
    return jnp.concatenate([src_ref[(*lead, pl.ds(s, rows, stride=ROW_TILE), slice(None))]
                            for s in range(ROW_TILE)], axis=-1)


def _row_tile(idx):
    return pl.ds(pl.multiple_of(idx * ROW_TILE, ROW_TILE), ROW_TILE)


def _dispatch_body(pos_ref, last_tile_ref, h_ref, xs_hbm, stage_ref, zero_ref, sem, zero_sem):
    i = pl.program_id(0)
    n = pl.num_programs(0)
    tm = DISPATCH_CHUNK
    slot = i % 2

    def drain(s):
        for _ in range(2):
            pltpu.make_async_copy(stage_ref.at[s], xs_hbm.at[pl.ds(0, tm * ROW_TILE)], sem.at[s]).wait()

    @pl.when(i == 0)
    def _():
        zero_ref[...] = jnp.zeros_like(zero_ref)

        def zero_copy(e):
            start = pl.multiple_of(last_tile_ref[e] * (TM_MOE * ROW_TILE), TM_MOE * ROW_TILE)
            return pltpu.make_async_copy(zero_ref, xs_hbm.at[pl.ds(start, TM_MOE * ROW_TILE)], zero_sem)

        for e in range(2 * N_EXPERTS):
            @pl.when(last_tile_ref[e] >= 0)
            def _():
                zero_copy(e).start()
        for e in range(2 * N_EXPERTS):
            @pl.when(last_tile_ref[e] >= 0)
            def _():
                zero_copy(e).wait()

    @pl.when(i >= 2)
    def _():
        drain(slot)

    _to_row_tiled(stage_ref, (slot,), h_ref[...])

    def body(t, carry):
        for k in range(2):
            dst = pos_ref[2 * (i * tm + t) + k]
            pltpu.make_async_copy(stage_ref.at[slot, _row_tile(t)], xs_hbm.at[_row_tile(dst)],
                                  sem.at[slot]).start(priority=k)
        return carry
    lax.fori_loop(0, tm, body, 0, unroll=8)

    @pl.when(i == n - 1)
    def _():
        drain(1 - slot)
        drain(slot)


def _dispatch(pos, last_tile, h):
    tm = DISPATCH_CHUNK
    return pl.pallas_call(
        _dispatch_body,
        grid=(TOKENS // tm,),
        in_specs=[pl.BlockSpec(memory_space=pltpu.SMEM),
                  pl.BlockSpec(memory_space=pltpu.SMEM),
                  pl.BlockSpec((tm, D_MODEL), lambda i: (i, 0))],
        out_specs=pl.BlockSpec(memory_space=pl.ANY),
        out_shape=jax.ShapeDtypeStruct((N_SORTED * ROW_TILE, LANES), F32),
        scratch_shapes=[pltpu.VMEM((2, tm * ROW_TILE, LANES), F32),
                        pltpu.VMEM((TM_MOE * ROW_TILE, LANES), F32),
                        pltpu.SemaphoreType.DMA((2,)), pltpu.SemaphoreType.DMA(())],
        compiler_params=_params(("arbitrary",)),
        name="moe_dispatch",
    )(pos, last_tile, h)


def _gffn_body(te_ref, nu_ref, xs_ref, w1_ref, w3_ref, w2_ref, o_ref, hb_ref, acc_ref):
    del te_ref
    i = pl.program_id(0)
    j = pl.program_id(1)

    used = i < nu_ref[0]

    @pl.when(j == 0)
    def _():
        acc_ref[...] = jnp.zeros_like(acc_ref)

    @pl.when(used & (j == 0))
    def _():
        hb_ref[...] = _from_row_tiled(xs_ref, (), TM_MOE).astype(BF16)

    @pl.when(used)
    def _():
        _swiglu_accumulate(hb_ref[...], w1_ref, w3_ref, w2_ref, acc_ref, TF_MOE)

    @pl.when(j == pl.num_programs(1) - 1)
    def _():
        _to_row_tiled(o_ref, (), acc_ref[...])


def _grouped_ffn(tile_expert, n_used, xs, w1, w3, w2):
    nj = FFN_EXPERT // TF_MOE
    col = lambda i, j, nu: jnp.where(i < nu[0], j, nj - 1)
    grid_spec = pltpu.PrefetchScalarGridSpec(
        num_scalar_prefetch=2,
        grid=(N_TILES_MOE, nj),
        in_specs=[
            pl.BlockSpec((TM_MOE * ROW_TILE, LANES), lambda i, j, te, nu: (jnp.minimum(i, nu[0] - 1), 0)),
            pl.BlockSpec((None, None, D_MODEL, TF_MOE), lambda i, j, te, nu: (te[i], col(i, j, nu), 0, 0)),
            pl.BlockSpec((None, None, D_MODEL, TF_MOE), lambda i, j, te, nu: (te[i], col(i, j, nu), 0, 0)),
            pl.BlockSpec((None, TF_MOE, D_MODEL), lambda i, j, te, nu: (te[i], col(i, j, nu), 0)),
        ],
        out_specs=pl.BlockSpec((TM_MOE * ROW_TILE, LANES), lambda i, j, te, nu: (i, 0)),
        scratch_shapes=[pltpu.VMEM((TM_MOE, D_MODEL), BF16), pltpu.VMEM((TM_MOE, D_MODEL), F32)],
    )
    return pl.pallas_call(
        _gffn_body,
        grid_spec=grid_spec,
        out_shape=jax.ShapeDtypeStruct((N_SORTED * ROW_TILE, LANES), F32),
        compiler_params=_params(("arbitrary", "arbitrary")),
        name="moe_ffn",
    )(tile_expert, n_used, xs, w1, w3, w2)


def _combine_body(pos_ref, x_ref, gate_ref, g_ref, ys_hbm, o_ref, buf_ref, sem):
    i = pl.program_id(0)
    n = pl.num_programs(0)
    tm = TM_COMBINE

    def issue(tile, slot):
        def body(t, carry):
            for k in range(2):
                src = pos_ref[2 * (tile * tm + t) + k]
                pltpu.make_async_copy(ys_hbm.at[_row_tile(src)], buf_ref.at[slot, k, _row_tile(t)],
                                      sem.at[slot]).start(priority=k)
            return carry
        lax.fori_loop(0, tm, body, 0, unroll=8)

    @pl.when(i == 0)
    def _():
        issue(0, 0)

    @pl.when(i + 1 < n)
    def _():
        issue(i + 1, (i + 1) % 2)

    slot = i % 2
    for k in range(2):
        pltpu.make_async_copy(ys_hbm.at[pl.ds(0, tm * ROW_TILE)], buf_ref.at[slot, k], sem.at[slot]).wait()
    w1 = gate_ref[:, 0:1]
    w2 = gate_ref[:, 1:2]
    y = x_ref[...] + w1 * _from_row_tiled(buf_ref, (slot, 0), tm) + w2 * _from_row_tiled(buf_ref, (slot, 1), tm)
    o_ref[...] = _rms(y, g_ref[...])


def _combine(pos, x, gates, g, ys):
    tm = TM_COMBINE
    return pl.pallas_call(
        _combine_body,
        grid=(TOKENS // tm,),
        in_specs=[pl.BlockSpec(memory_space=pltpu.SMEM),
                  pl.BlockSpec((tm, D_MODEL), lambda i: (i, 0)),
                  pl.BlockSpec((tm, 2), lambda i: (i, 0)),
                  pl.BlockSpec((1, D_MODEL), lambda i: (0, 0)),
                  pl.BlockSpec(memory_space=pl.ANY)],
        out_specs=pl.BlockSpec((tm, D_MODEL), lambda i: (i, 0)),
        out_shape=jax.ShapeDtypeStruct((TOKENS, D_MODEL), F32),
        scratch_shapes=[pltpu.VMEM((2, 2, tm * ROW_TILE, LANES), F32), pltpu.SemaphoreType.DMA((2,))],
        compiler_params=_params(("arbitrary",)),
        name="moe_combine",
    )(pos, x, gates, g, ys)


def _routing_tables(route, counts):
    cnt = counts[:, 0].astype(jnp.int32)
    tiles = (cnt + TM_MOE - 1) // TM_MOE
    tile_end = jnp.cumsum(tiles)
    tile_start = tile_end - tiles
    n_used = tile_end[-1]
    expert = route[ROUTE_I1:ROUTE_I2 + 1].astype(jnp.int32)
    rank = route[ROUTE_R1:ROUTE_R2 + 1].astype(jnp.int32)
    group_start = jnp.sum(jnp.where(expert[..., None] == jnp.arange(N_EXPERTS), tile_start * TM_MOE, 0), axis=-1)
    pos = (group_start + rank).T
    tile_id = jnp.minimum(jnp.arange(N_TILES_MOE, dtype=jnp.int32), n_used - 1)
    tile_expert = jnp.sum(tile_id[:, None] >= tile_end[None, :], axis=1).astype(jnp.int32)
    last_tile = jnp.where(tiles > 0, tile_end - 1, -1)
    spare = n_used + jnp.arange(N_EXPERTS)
    zero_tiles = jnp.concatenate([last_tile, jnp.where(spare < N_TILES_MOE, spare, -1)]).astype(jnp.int32)
    return pos.reshape(2 * TOKENS), tile_expert, n_used.reshape(1), zero_tiles


def _moe(x, h, route, counts, w1, w3, w2, g_final):
    pos, tile_expert, n_used, last_tile = _routing_tables(route, counts)
    xs = _dispatch(pos, last_tile, h)
    ys = _grouped_ffn(tile_expert, n_used, xs, w1, w3, w2)
    return _combine(pos, x, route[ROUTE_W1:ROUTE_W2 + 1].T, g_final, ys)


def _prep_w_in(w):
    aq, ak, av, gq, gk, gv, gr, glr, c_in, c_b, c_c = jnp.split(w, np.cumsum(SPLIT_SIZES)[:-1].tolist(), axis=2)
    pad = jnp.zeros(w.shape[:2] + (LANES - GLA_RANK,), w.dtype)
    return jnp.concatenate([aq, ak, av, gv, gr, gq, gk, c_in, c_b, c_c, glr, pad], axis=2).astype(BF16)


def _prep_router(w):
    wt = w.T
    hi = wt.astype(BF16)
    lo = (wt - hi.astype(F32)).astype(BF16)
    return jnp.stack([jnp.concatenate([hi, lo]), jnp.concatenate([hi, jnp.zeros_like(hi)])])


def kernel(x, w_mix_in, w_mix_out, g_mix, rel_bias, gla_w_gate, gla_b_gate, gla_g_norm, conv_w,
           g_ffn, ffn_w1, ffn_w3, ffn_w2, moe_router, moe_w1, moe_w3, moe_w2, g_final):
    assert DEPTH == 2
    x = x.reshape(TOKENS, D_MODEL)
    up_job = lambda w, tf: (w, w.shape[-1] // tf)
    down = lambda c: c.reshape(c.shape[0], c.shape[2], c.shape[3])
    experts = {}
    w_in = _prep_w_in(w_mix_in)
    for layer in range(DEPTH):
        u, qkvs, casts = _inproj(x, g_mix[layer].reshape(1, D_MODEL), w_in, layer,
                                 side_casts=[up_job(moe_w1[0], TF_MOE)] if layer == 1 else [])
        if layer == 1:
            experts["w1"] = casts[0]
        dense_jobs = [up_job(ffn_w1[:1], TF_DENSE), up_job(ffn_w3[:1], TF_DENSE), (ffn_w2[:1], 1)] if layer == 0 else []
        att, dense_bf16 = _attention(qkvs, rel_bias, dense_jobs)
        if layer == 0:
            ffn_bf16 = (dense_bf16[0][0], dense_bf16[1][0], down(dense_bf16[2])[0])
        wg = jnp.pad(gla_w_gate[layer], ((0, LANES - GLA_RANK), (0, 0)))
        go, cv, _ = _gla_conv(u, wg, gla_b_gate[layer].reshape(1, GLA_QK), gla_g_norm[layer].reshape(1, GLA_DV),
                              jnp.pad(conv_w[layer], ((0, 8 - CONV_WIDTH), (0, 0))))
        w_out = w_mix_out[layer].astype(BF16)
        g2 = g_ffn[layer].reshape(1, D_MODEL)
        i = layer // 2
        if layer % 2 == 0:
            x, h = _outproj(att, go, cv, x, w_out, g2)
            x, (experts["w3"], cast) = _ffn(x, h, *ffn_bf16, side_casts=[up_job(moe_w3[i], TF_MOE), (moe_w2[i], 1)])
            experts["w2"] = down(cast)
        else:
            x, h, route, counts = _outproj(att, go, cv, x, w_out, g2, _prep_router(moe_router[i]))
            x = _moe(x, h, route, counts, experts["w1"], experts["w3"], experts["w2"], g_final.reshape(1, D_MODEL))
    return x.reshape(BATCH, SEQ, D_MODEL)
```

```python
import functools
import math

import jax
import jax.numpy as jnp
import numpy as np
from jax import lax
from jax.experimental import pallas as pl
from jax.experimental.pallas import tpu as pltpu

F32 = jnp.float32
BF16 = jnp.bfloat16

D_MODEL = 1024
BATCH = 8
SEQ = 2048
TOKENS = BATCH * SEQ
DEPTH = 2
EPS = 1e-6

HEAD_DIM = 64
ATT_HEADS = 4
ATT_WIDTH = ATT_HEADS * HEAD_DIM
DILATED_PATTERNS = ((128, 1), (512, 4), (2048, 16))
ATT_BLOCK = 128
REL_BUCKETS = 32
REL_MAX_DISTANCE = 2048

GLA_HEADS = 4
GLA_DK = 64
GLA_DV = 128
GLA_RANK = 16
GLA_CHUNK = 64
GLA_QK = GLA_HEADS * GLA_DK
GLA_V = GLA_HEADS * GLA_DV

CONV_CH = 256
CONV_WIDTH = 3
MIX_WIDTH = ATT_WIDTH + GLA_V + CONV_CH

SPLIT_SIZES = (ATT_WIDTH, ATT_WIDTH, ATT_WIDTH, GLA_QK, GLA_QK, GLA_V, GLA_V, GLA_RANK,
               CONV_CH, CONV_CH, CONV_CH)

FFN_DENSE = 2816
N_EXPERTS = 8
FFN_EXPERT = 3584

LANES = 128
MXU_WIDTH = 256
VMEM_LIMIT = 56 * 1024 * 1024

QKV_COLS = 3 * ATT_WIDTH
COL_GV, COL_GR, COL_GQ, COL_GK = 0, 512, 1024, 1280
COL_CIN, COL_CB, COL_CC, COL_GLR = 1536, 1792, 2048, 2304
U_COLS = COL_GLR + LANES

NEG_BIG = -1e30

ATT_UNROLL = 15
TM_PROJ = 512
TC_GLA = 1024
GLA_CUMSUM_SPAN = 256
TM_FFN = 512
TF_DENSE = 2816
FFN_SUB = 256
TM_MOE = 512
TF_MOE = 1792
N_TILES_MOE = 2 * TOKENS // TM_MOE + N_EXPERTS
N_SORTED = N_TILES_MOE * TM_MOE
DISPATCH_CHUNK = 512
TM_COMBINE = 512

ROUTE_I1, ROUTE_I2, ROUTE_W1, ROUTE_W2, ROUTE_R1, ROUTE_R2 = range(6)


def _params(sem):
    return pltpu.CompilerParams(dimension_semantics=sem, vmem_limit_bytes=VMEM_LIMIT)


def _split_bf16(a):
    hi = a.astype(BF16)
    lo = (a - hi.astype(F32)).astype(BF16)
    return hi, lo


def _dot(a, b):
    return jnp.dot(a, b, preferred_element_type=F32)


def _dot3(a, b):
    a_hi, a_lo = _split_bf16(a)
    b_hi, b_lo = _split_bf16(b)
    return _dot(a_hi, b_hi) + _dot(a_lo, b_hi) + _dot(a_hi, b_lo)


def _dot_nt(a, b):
    return lax.dot_general(a, b, (((1,), (1,)), ((), ())), preferred_element_type=F32)


def _rms(x, g):
    ms = jnp.mean(x * x, axis=-1, keepdims=True)
    return x * lax.rsqrt(ms + EPS) * g


def _sigmoid(x):
    return 1.0 / (1.0 + jnp.exp(-x))


def _side_cast(srcs, dsts):
    for src, dst in zip(srcs, dsts):
        width = dst.shape[-1]
        for s in range(dst.shape[0]):
            dst[s] = src[:, s * width:(s + 1) * width].astype(dst.dtype)


def _side_cast_io(jobs, steps, step_of):
    in_specs, out_specs, out_shapes = [], [], []
    for w, splits in jobs:
        g, r, c = w.shape
        rb = g * r // steps
        per_group = r // rb
        in_specs.append(pl.BlockSpec(
            (None, rb, c), lambda *ids, pg=per_group: (step_of(*ids) // pg, step_of(*ids) % pg, 0)))
        out_specs.append(pl.BlockSpec(
            (None, splits, rb, c // splits),
            lambda *ids, pg=per_group: (step_of(*ids) // pg, 0, step_of(*ids) % pg, 0)))
        out_shapes.append(jax.ShapeDtypeStruct((g, splits, r, c // splits), BF16))
    return in_specs, out_specs, out_shapes


def _inproj_body(x_ref, g_ref, w_ref, *rest):
    n_pat = len(DILATED_PATTERNS)
    n_cast = (len(rest) - n_pat - 2) // 2
    o_ref = rest[n_cast]
    qkv_refs = rest[n_cast + 1:n_cast + 1 + n_pat]
    qkv_f32 = rest[-1]
    _side_cast(rest[:n_cast], rest[n_cast + 1 + n_pat:-1])
    h = _rms(x_ref[...], g_ref[...]).astype(BF16)
    for c0 in range(0, QKV_COLS, MXU_WIDTH):
        res = _dot(h, w_ref[:, c0:c0 + MXU_WIDTH])
        for t in range(MXU_WIDTH // LANES):
            qkv_f32[c0 // LANES + t] = res[:, t * LANES:(t + 1) * LANES]
    for (_, dilation), ref in zip(DILATED_PATTERNS, qkv_refs):
        for r in range(dilation):
            rows = pl.ds(r, TM_PROJ // dilation, stride=dilation)
            ref[r] = jnp.concatenate([qkv_f32[t, rows, :] for t in range(QKV_COLS // LANES)],
                                     axis=-1).astype(ref.dtype)
    for c0 in range(0, U_COLS, MXU_WIDTH):
        c1 = min(c0 + MXU_WIDTH, U_COLS)
        o_ref[:, c0:c1] = _dot(h, w_ref[:, QKV_COLS + c0:QKV_COLS + c1])


def _subseq_spec(dilation, cols):
    tiles = SEQ // TM_PROJ
    return pl.BlockSpec((None, dilation, TM_PROJ // dilation, cols), lambda i: (i // tiles, 0, i % tiles, 0))


def _inproj(x, g, w_all, layer, side_casts=()):
    n_pat = len(DILATED_PATTERNS)
    qkv_shapes = [jax.ShapeDtypeStruct((BATCH, d, SEQ // d, QKV_COLS), BF16) for _, d in DILATED_PATTERNS]
    cast_in, cast_out, cast_shapes = _side_cast_io(side_casts, TOKENS // TM_PROJ, lambda i: i)
    u, *rest = pl.pallas_call(
        _inproj_body,
        grid=(TOKENS // TM_PROJ,),
        in_specs=[
            pl.BlockSpec((TM_PROJ, D_MODEL), lambda i: (i, 0)),
            pl.BlockSpec((1, D_MODEL), lambda i: (0, 0)),
            pl.BlockSpec((None, D_MODEL, QKV_COLS + U_COLS), lambda i: (layer, 0, 0)),
        ] + cast_in,
        out_specs=[pl.BlockSpec((TM_PROJ, U_COLS), lambda i: (i, 0))]
                  + [_subseq_spec(d, QKV_COLS) for _, d in DILATED_PATTERNS] + cast_out,
        out_shape=[jax.ShapeDtypeStruct((TOKENS, U_COLS), F32)] + qkv_shapes + cast_shapes,
        scratch_shapes=[pltpu.VMEM((QKV_COLS // LANES, TM_PROJ, LANES), F32)],
        compiler_params=_params(("parallel",)),
        name="inproj",
    )(x, g, w_all, *[w for w, _ in side_casts])
    return u, rest[:n_pat], rest[n_pat:]


def _rel_bucket(dist):
    max_exact = REL_BUCKETS // 2
    d = jnp.maximum(dist, 0)
    log_ratio = jnp.log(jnp.maximum(d, 1).astype(F32) / max_exact) / math.log(REL_MAX_DISTANCE / max_exact)
    large = jnp.minimum(max_exact + (log_ratio * (REL_BUCKETS - max_exact)).astype(jnp.int32), REL_BUCKETS - 1)
    return jnp.where(d < max_exact, d, large)


def _bucket_table(window, dilation):
    span = window // dilation
    qi = jnp.arange(ATT_BLOCK)[:, None]
    kj = jnp.arange(2 * ATT_BLOCK)[None, :]
    sub_dist = qi - kj + ATT_BLOCK
    band = (sub_dist >= 0) & (sub_dist <= span)
    return jnp.where(band, _rel_bucket(sub_dist * dilation), -1).astype(jnp.int32)


def _attn_body(rb_ref, bidx_ref, qkv_ref, *rest, sub_blocks, unroll):
    n_cast = (len(rest) - 3) // 2
    o_ref, lse_ref = rest[n_cast:n_cast + 2]
    bias_ref = rest[-1]
    _side_cast(rest[:n_cast], rest[n_cast + 2:-1])
    nblk = SEQ // ATT_BLOCK

    @pl.when(pl.program_id(0) == 0)
    def _():
        bidx = bidx_ref[...]
        in_prev = lax.broadcasted_iota(jnp.int32, bidx.shape, 1) < ATT_BLOCK
        for h in range(ATT_HEADS):
            acc = jnp.full(bidx.shape, NEG_BIG, F32)
            for b in range(REL_BUCKETS):
                acc = jnp.where(bidx == b, rb_ref[b, h], acc)
            bias_ref[0, h] = acc
            bias_ref[1, h] = jnp.where(in_prev, NEG_BIG, acc)
            bias_ref[2, h] = jnp.concatenate([acc[:, ATT_BLOCK:], jnp.full_like(acc[:, ATT_BLOCK:], NEG_BIG)], axis=1)

    def block(n, first):
        if first:
            rows, krows, variant = slice(0, ATT_BLOCK), slice(0, 2 * ATT_BLOCK), 2
        else:
            r0 = pl.multiple_of(n * ATT_BLOCK, ATT_BLOCK)
            rows = pl.ds(r0, ATT_BLOCK)
            krows = pl.ds(r0 - ATT_BLOCK, 2 * ATT_BLOCK)
            if sub_blocks == nblk:
                variant = 0
            elif sub_blocks == 1:
                variant = 1
            else:
                variant = jnp.where(n % sub_blocks == 0, 1, 0)
        q = qkv_ref[rows, 0:ATT_WIDTH]
        kk = qkv_ref[krows, ATT_WIDTH:2 * ATT_WIDTH]
        vv = qkv_ref[krows, 2 * ATT_WIDTH:3 * ATT_WIDTH]
        q = q * jnp.asarray(HEAD_DIM ** -0.5, BF16)
        head_of_lane = lax.broadcasted_iota(jnp.int32, (ATT_BLOCK, ATT_WIDTH), 1) // HEAD_DIM
        ones = jnp.ones((kk.shape[0], LANES), BF16)
        num = den = mx = None
        for h in range(ATT_HEADS):
            mine = head_of_lane == h
            bias = bias_ref[variant, h]
            s = _dot_nt(jnp.where(mine, q, jnp.zeros_like(q)), kk) + bias
            m = jnp.max(s, axis=-1, keepdims=True)
            p = jnp.exp(s - m).astype(BF16)
            num_h = _dot(p, vv)
            den_h = jnp.tile(_dot(p, ones), (1, ATT_WIDTH // LANES))
            m_h = jnp.broadcast_to(m, (ATT_BLOCK, ATT_WIDTH))
            num = num_h if h == 0 else jnp.where(mine, num_h, num)
            den = den_h if h == 0 else jnp.where(mine, den_h, den)
            mx = m_h if h == 0 else jnp.where(mine, m_h, mx)
        o_ref[rows, :] = (num / den).astype(o_ref.dtype)
        lse_ref[rows, :] = mx + jnp.log(den)

    block(0, True)

    def loop_body(n, carry):
        block(n, False)
        return carry
    lax.fori_loop(1, nblk, loop_body, 0, unroll=unroll)


def _attention_pattern(ua, rel_bias, window, dilation, side_casts=()):
    L = SEQ // dilation
    shape = (BATCH, dilation, L, ATT_WIDTH)
    qkv_spec = pl.BlockSpec((None, SEQ, QKV_COLS), lambda b: (b, 0, 0))
    out_spec = pl.BlockSpec((None, SEQ, ATT_WIDTH), lambda b: (b, 0, 0))
    cast_in, cast_out, cast_shapes = _side_cast_io(side_casts, BATCH, lambda b: b)
    o, lse, *casts = pl.pallas_call(
        functools.partial(_attn_body, sub_blocks=L // ATT_BLOCK, unroll=ATT_UNROLL),
        grid=(BATCH,),
        in_specs=[
            pl.BlockSpec(memory_space=pltpu.SMEM),
            pl.BlockSpec((ATT_BLOCK, 2 * ATT_BLOCK), lambda b: (0, 0)),
            qkv_spec,
        ] + cast_in,
        out_specs=[out_spec, out_spec] + cast_out,
        out_shape=[jax.ShapeDtypeStruct((BATCH, SEQ, ATT_WIDTH), BF16),
                   jax.ShapeDtypeStruct((BATCH, SEQ, ATT_WIDTH), F32)] + cast_shapes,
        scratch_shapes=[pltpu.VMEM((3, ATT_HEADS, ATT_BLOCK, 2 * ATT_BLOCK), F32)],
        compiler_params=_params(("arbitrary",)),
        name=f"attn_d{dilation}",
    )(rel_bias, _bucket_table(window, dilation), ua.reshape(BATCH, SEQ, QKV_COLS), *[w for w, _ in side_casts])
    return (o.reshape(shape), lse.reshape(shape)), casts


def _attention(qkvs, rel_bias, side_casts=()):
    side_casts = list(side_casts) + [None] * (len(DILATED_PATTERNS) - len(side_casts))
    res = [_attention_pattern(ua, rel_bias, window, dilation, [] if w is None else [w])
           for ua, (window, dilation), w in zip(qkvs, DILATED_PATTERNS, side_casts)]
    return [r[0] for r in res], [c for r in res for c in r[1]]


def _short_conv(cin_ref, cb_ref, cc_ref, w_ref, o_ref, tail_ref):
    uu = cc_ref[...] * cin_ref[...]
    t = lax.broadcasted_iota(jnp.int32, uu.shape, 0)
    y = uu * w_ref[CONV_WIDTH - 1:CONV_WIDTH, :]
    for shift in range(1, CONV_WIDTH):
        prev = pltpu.roll(uu, shift, axis=0)
        for r in range(shift):
            prev = jnp.where(t == r, tail_ref[8 - shift + r:8 - shift + r + 1, :], prev)
        y = y + prev * w_ref[CONV_WIDTH - 1 - shift:CONV_WIDTH - shift, :]
    o_ref[...] = (cb_ref[...] * y).astype(o_ref.dtype)
    tail_ref[...] = uu[uu.shape[0] - 8:, :]


def _gla_body(u_ref, wg_ref, bg_ref, gn_ref, ctril_ref, cw_ref, *rest):
    cols = lambda col, width: u_ref.at[:, col:col + width]
    q_ref, k_ref = cols(COL_GQ, GLA_QK), cols(COL_GK, GLA_QK)
    v_ref, gr_ref, glr_ref = cols(COL_GV, GLA_V), cols(COL_GR, GLA_V), cols(COL_GLR, LANES)
    cin_ref, cb_ref, cc_ref = cols(COL_CIN, CONV_CH), cols(COL_CB, CONV_CH), cols(COL_CC, CONV_CH)
    _gla_tile(q_ref, k_ref, v_ref, gr_ref, glr_ref, wg_ref, bg_ref, gn_ref, ctril_ref,
              cin_ref, cb_ref, cc_ref, cw_ref, *rest)


def _gla_tile(q_ref, k_ref, v_ref, gr_ref, glr_ref, wg_ref, bg_ref, gn_ref, ctril_ref,
              cin_ref, cb_ref, cc_ref, cw_ref, o_ref, cv_ref, s_ref, tail_ref):
    @pl.when(pl.program_id(1) == 0)
    def _():
        s_ref[...] = jnp.zeros_like(s_ref)
        tail_ref[...] = jnp.zeros_like(tail_ref)

    _short_conv(cin_ref, cb_ref, cc_ref, cw_ref, cv_ref, tail_ref)

    C = GLA_CHUNK
    row = lax.broadcasted_iota(jnp.int32, (C, C), 0)
    col = lax.broadcasted_iota(jnp.int32, (C, C), 1)
    tril = row >= col
    n_chunks = TC_GLA // C

    xg = _dot3(glr_ref[...], wg_ref[...]) + bg_ref[...]
    la_all = (jnp.minimum(xg, 0.0) - jnp.log(1.0 + jnp.exp(-jnp.abs(xg)))) * (1.0 / 16.0)

    la_hi, la_lo = _split_bf16(la_all)
    span = ctril_ref.shape[0]
    cum_all = jnp.concatenate(
        [_dot(ctril_ref[...], la_hi[r0:r0 + span]) + _dot(ctril_ref[...], la_lo[r0:r0 + span])
         for r0 in range(0, TC_GLA, span)], axis=0)
    totals = jnp.concatenate([cum_all[(c + 1) * C - 1:(c + 1) * C, :] for c in range(n_chunks)]
                             + [jnp.zeros((LANES - n_chunks, GLA_QK), F32)], axis=0)
    decay_cols = jnp.exp(totals.T)

    for c in range(n_chunks):
        rows = slice(c * C, (c + 1) * C)
        cum = cum_all[rows]
        last = cum[C - 1:C, :]
        q = q_ref[rows, :]
        k = k_ref[rows, :]
        qs = q * (GLA_DK ** -0.5)
        qt = (qs * jnp.exp(cum)).astype(BF16)
        mid = cum[C // 2 - 1:C // 2, :]
        qm = (qs * jnp.exp(cum - mid)).astype(BF16)
        kt = (k * jnp.exp(mid - cum)).astype(BF16)
        kl_t = (k * jnp.exp(last - cum)).T.astype(BF16)
        for h in range(GLA_HEADS):
            sl = slice(h * GLA_DK, (h + 1) * GLA_DK)
            vs = slice(h * GLA_DV, (h + 1) * GLA_DV)
            vh = v_ref[rows, vs].astype(BF16)
            state = s_ref[h]
            st_hi, st_lo = _split_bf16(state)
            sc = jnp.where(tril, _dot_nt(qm[:, sl], kt[:, sl]), 0.0).astype(BF16)
            o = _dot(qt[:, sl], st_hi) + _dot(qt[:, sl], st_lo) + _dot(sc, vh)
            decay = jnp.broadcast_to(decay_cols[sl, c:c + 1], state.shape)
            s_ref[h] = decay * state + _dot(kl_t[sl, :], vh)
            g = gr_ref[rows, vs]
            o_ref[rows, vs] = (_rms(o, gn_ref[...]) * (g * _sigmoid(g))).astype(o_ref.dtype)


def _gla_conv(u, wg, bg, gn, conv_w):
    nj = SEQ // TC_GLA
    row = lambda b, j: b * nj + j
    full = lambda a: pl.BlockSpec(a.shape, lambda b, j: (0, 0))
    t = np.arange(GLA_CUMSUM_SPAN)
    same_chunk = (t[:, None] // GLA_CHUNK) == (t[None, :] // GLA_CHUNK)
    chunk_tril = jnp.asarray(same_chunk & (t[:, None] >= t[None, :]), BF16)
    return pl.pallas_call(
        _gla_body,
        grid=(BATCH, nj),
        in_specs=[pl.BlockSpec((TC_GLA, U_COLS), lambda b, j: (row(b, j), 0)),
                  full(wg), full(bg), full(gn), full(chunk_tril), full(conv_w)],
        out_specs=[pl.BlockSpec((TC_GLA, GLA_V), lambda b, j: (row(b, j), 0)),
                   pl.BlockSpec((TC_GLA, CONV_CH), lambda b, j: (row(b, j), 0))],
        out_shape=[jax.ShapeDtypeStruct((TOKENS, GLA_V), BF16), jax.ShapeDtypeStruct((TOKENS, CONV_CH), BF16)],
        scratch_shapes=[pltpu.VMEM((GLA_HEADS, GLA_DK, GLA_DV), F32), pltpu.VMEM((8, CONV_CH), F32)],
        compiler_params=_params(("arbitrary", "arbitrary")),
        name="gla_conv",
    )(u, wg, bg, gn, chunk_tril, conv_w)


def _outproj_body(*refs, route):
    (o1, o4, o16, l1, l4, l16, go_ref, cv_ref, x_ref, w_ref, g_ref) = refs[:11]
    perm_ref = refs[-1]
    if route:
        wr_ref, xo_ref, ho_ref, route_ref, counts_ref, carry_ref = refs[11:-1]
    else:
        xo_ref, ho_ref = refs[11:-1]

    def token_order(ref, slot):
        dilation, rows, _ = ref.shape
        if dilation == 1:
            return ref[0].astype(F32)
        tiles = range(ATT_WIDTH // LANES)
        for r in range(dilation):
            val = ref[r].astype(F32)
            for t in tiles:
                perm_ref[slot, t, pl.ds(r, rows, stride=dilation), :] = val[:, t * LANES:(t + 1) * LANES]
        return jnp.concatenate([perm_ref[slot, t] for t in tiles], axis=-1)

    la, lb, lc = token_order(l1, 0), token_order(l4, 0), token_order(l16, 1)
    oa, ob, oc = token_order(o1, 0), token_order(o4, 2), token_order(o16, 3)
    m = jnp.maximum(jnp.maximum(la, lb), lc)
    ea, eb, ec = jnp.exp(la - m), jnp.exp(lb - m), jnp.exp(lc - m)
    att = (ea * oa + eb * ob + ec * oc) / (ea + eb + ec)
    y = (x_ref[...]
         + _dot(att.astype(BF16), w_ref[0:ATT_WIDTH, :])
         + _dot(go_ref[...], w_ref[ATT_WIDTH:ATT_WIDTH + GLA_V, :])
         + _dot(cv_ref[...], w_ref[ATT_WIDTH + GLA_V:MIX_WIDTH, :]))
    xo_ref[...] = y
    hf = _rms(y, g_ref[...])
    ho_ref[...] = hf.astype(ho_ref.dtype)
    if route:
        @pl.when(pl.program_id(0) == 0)
        def _():
            carry_ref[...] = jnp.zeros_like(carry_ref)

        tm = hf.shape[0]
        ne = N_EXPERTS
        hf_hi, hf_lo = _split_bf16(hf)
        part = _dot_nt(wr_ref[0], hf_hi) + _dot_nt(wr_ref[1], hf_lo)
        logits = part[0:ne] + part[ne:2 * ne]
        eidx = lax.broadcasted_iota(jnp.int32, logits.shape, 0).astype(F32)
        v1 = jnp.max(logits, axis=0, keepdims=True)
        i1 = jnp.min(jnp.where(logits == v1, eidx, float(ne)), axis=0, keepdims=True)
        lg2 = jnp.where(eidx == i1, -jnp.inf, logits)
        v2 = jnp.max(lg2, axis=0, keepdims=True)
        i2 = jnp.min(jnp.where(lg2 == v2, eidx, float(ne)), axis=0, keepdims=True)
        e2 = jnp.exp(v2 - v1)
        w1 = 1.0 / (1.0 + e2)
        w2 = e2 * w1
        sel1 = eidx == i1
        sel2 = eidx == i2
        onehot = jnp.where(sel1, 1.0, jnp.where(sel2, 1.0, 0.0))
        tri = (lax.broadcasted_iota(jnp.int32, (tm, tm), 0) <= lax.broadcasted_iota(jnp.int32, (tm, tm), 1))
        onehot16 = jnp.concatenate([onehot, jnp.zeros_like(onehot)], axis=0).astype(BF16)
        csum = _dot(onehot16, jnp.where(tri, 1.0, 0.0).astype(BF16))[0:ne]
        carry = carry_ref[:, 0:1]
        rank = csum - onehot + carry
        r1 = jnp.sum(jnp.where(sel1, rank, 0.0), axis=0, keepdims=True)
        r2 = jnp.sum(jnp.where(sel2, rank, 0.0), axis=0, keepdims=True)
        total = jnp.broadcast_to(carry + csum[:, tm - 1:tm], carry_ref.shape)
        carry_ref[...] = total
        counts_ref[...] = total
        rows = {ROUTE_I1: i1, ROUTE_I2: i2, ROUTE_W1: w1, ROUTE_W2: w2, ROUTE_R1: r1, ROUTE_R2: r2}
        zero = jnp.zeros_like(i1)
        route_ref[...] = jnp.concatenate([rows.get(r, zero) for r in range(8)], axis=0)


def _outproj(att, go, cv, x, w, g, w_router=None):
    route = w_router is not None
    tm = TM_PROJ
    tile = lambda cols: pl.BlockSpec((tm, cols), lambda i: (i, 0))
    full = lambda a: pl.BlockSpec(a.shape, lambda i: (0, 0))
    (o1, l1), (o4, l4), (o16, l16) = att
    args = [o1, o4, o16, l1, l4, l16, go, cv, x, w, g]
    att_specs = [_subseq_spec(d, ATT_WIDTH) for _, d in DILATED_PATTERNS]
    in_specs = att_specs * 2 + [tile(GLA_V), tile(CONV_CH), tile(D_MODEL), full(w), full(g)]
    out_specs = [tile(D_MODEL), tile(D_MODEL)]
    out_shape = [jax.ShapeDtypeStruct((TOKENS, D_MODEL), F32),
                 jax.ShapeDtypeStruct((TOKENS, D_MODEL), F32 if route else BF16)]
    scratch = []
    if route:
        args.append(w_router)
        in_specs.append(pl.BlockSpec(w_router.shape, lambda i: (0, 0, 0)))
        out_specs += [pl.BlockSpec((8, tm), lambda i: (0, i)), pl.BlockSpec((N_EXPERTS, LANES), lambda i: (0, 0))]
        out_shape += [jax.ShapeDtypeStruct((8, TOKENS), F32), jax.ShapeDtypeStruct((N_EXPERTS, LANES), F32)]
        scratch = [pltpu.VMEM((N_EXPERTS, LANES), F32)]
    scratch.append(pltpu.VMEM((4, ATT_WIDTH // LANES, tm, LANES), F32))
    return pl.pallas_call(
        functools.partial(_outproj_body, route=route),
        grid=(TOKENS // tm,),
        in_specs=in_specs,
        out_specs=out_specs,
        out_shape=out_shape,
        scratch_shapes=scratch,
        compiler_params=_params(("arbitrary",) if route else ("parallel",)),
        name="outproj_route" if route else "outproj",
    )(*args)


def _swiglu_accumulate(h, w1_ref, w3_ref, w2_ref, acc_ref, tf):
    for c0 in range(0, tf, FFN_SUB):
        c1 = min(c0 + FFN_SUB, tf)
        a = _dot(h, w1_ref[:, c0:c1])
        b = _dot(h, w3_ref[:, c0:c1])
        act = a * _sigmoid(a) * b
        acc_ref[...] += _dot(act.astype(BF16), w2_ref[c0:c1, :])


def _ffn_body(x_ref, h_ref, w1_ref, w3_ref, w2_ref, *rest, tf):
    n_cast = (len(rest) - 1) // 2
    o_ref = rest[n_cast]

    @pl.when(pl.program_id(1) == 0)
    def _():
        o_ref[...] = x_ref[...]

    _swiglu_accumulate(h_ref[...], w1_ref, w3_ref, w2_ref, o_ref, tf)
    _side_cast(rest[:n_cast], rest[n_cast + 1:])


def _ffn(x, h, w1, w3, w2, *, side_casts=()):
    nj, _, tf = w1.shape
    tm = TM_FFN
    tile = lambda cols: pl.BlockSpec((tm, cols), lambda i, j: (i, 0))
    cast_in, cast_out, cast_shapes = _side_cast_io(side_casts, (TOKENS // tm) * nj, lambda i, j: i * nj + j)
    resident = dict(pipeline_mode=pl.Buffered(1)) if nj == 1 else {}
    res = pl.pallas_call(
        functools.partial(_ffn_body, tf=tf),
        grid=(TOKENS // tm, nj),
        in_specs=[tile(D_MODEL), tile(D_MODEL),
                  pl.BlockSpec((None, D_MODEL, tf), lambda i, j: (j, 0, 0), **resident),
                  pl.BlockSpec((None, D_MODEL, tf), lambda i, j: (j, 0, 0), **resident),
                  pl.BlockSpec((tf, D_MODEL), lambda i, j: (j, 0), **resident)] + cast_in,
        out_specs=[tile(D_MODEL)] + cast_out,
        out_shape=[jax.ShapeDtypeStruct((TOKENS, D_MODEL), F32)] + cast_shapes,
        compiler_params=_params(("parallel", "arbitrary")),
        name="dense_ffn",
    )(x, h, w1, w3, w2, *[w for w, _ in side_casts])
    return res[0], res[1:]


ROW_TILE = D_MODEL // LANES


def _to_row_tiled(dst_ref, lead, val):
    rows = val.shape[0]
    for s in range(ROW_TILE):
        dst_ref[(*lead, pl.ds(s, rows, stride=ROW_TILE), slice(None))] = val[:, s * LANES:(s + 1) * LANES]


def _from_row_tiled(src_ref, lead, rows):
    return jnp.concatenate([src_ref[(*lead, pl.ds(s, rows, stride=ROW_TILE), slice(None))]
                            for s in range(ROW_TILE)], axis=-1)


def _row_tile(idx):
    return pl.ds(pl.multiple_of(idx * ROW_TILE, ROW_TILE), ROW_TILE)


def _dispatch_body(pos_ref, last_tile_ref, h_ref, xs_hbm, stage_ref, zero_ref, sem, zero_sem):
    i = pl.program_id(0)
    n = pl.num_programs(0)
    tm = DISPATCH_CHUNK
    slot = i % 2

    def drain(s):
        for _ in range(2):
            pltpu.make_async_copy(stage_ref.at[s], xs_hbm.at[pl.ds(0, tm * ROW_TILE)], sem.at[s]).wait()

    @pl.when(i == 0)
    def _():
        zero_ref[...] = jnp.zeros_like(zero_ref)

        def zero_copy(e):
            start = pl.multiple_of(last_tile_ref[e] * (TM_MOE * ROW_TILE), TM_MOE * ROW_TILE)
            return pltpu.make_async_copy(zero_ref, xs_hbm.at[pl.ds(start, TM_MOE * ROW_TILE)], zero_sem)

        for e in range(2 * N_EXPERTS):
            @pl.when(last_tile_ref[e] >= 0)
            def _():
                zero_copy(e).start()
        for e in range(2 * N_EXPERTS):
            @pl.when(last_tile_ref[e] >= 0)
            def _():
                zero_copy(e).wait()

    @pl.when(i >= 2)
    def _():
        drain(slot)

    _to_row_tiled(stage_ref, (slot,), h_ref[...])

    def body(t, carry):
        for k in range(2):
            dst = pos_ref[2 * (i * tm + t) + k]
            pltpu.make_async_copy(stage_ref.at[slot, _row_tile(t)], xs_hbm.at[_row_tile(dst)],
                                  sem.at[slot]).start(priority=k)
        return carry
    lax.fori_loop(0, tm, body, 0, unroll=8)

    @pl.when(i == n - 1)
    def _():
        drain(1 - slot)
        drain(slot)


def _dispatch(pos, last_tile, h):
    tm = DISPATCH_CHUNK
    return pl.pallas_call(
        _dispatch_body,
        grid=(TOKENS // tm,),
        in_specs=[pl.BlockSpec(memory_space=pltpu.SMEM),
                  pl.BlockSpec(memory_space=pltpu.SMEM),
                  pl.BlockSpec((tm, D_MODEL), lambda i: (i, 0))],
        out_specs=pl.BlockSpec(memory_space=pl.ANY),
        out_shape=jax.ShapeDtypeStruct((N_SORTED * ROW_TILE, LANES), F32),
        scratch_shapes=[pltpu.VMEM((2, tm * ROW_TILE, LANES), F32),
                        pltpu.VMEM((TM_MOE * ROW_TILE, LANES), F32),
                        pltpu.SemaphoreType.DMA((2,)), pltpu.SemaphoreType.DMA(())],
        compiler_params=_params(("arbitrary",)),
        name="moe_dispatch",
    )(pos, last_tile, h)


def _gffn_body(te_ref, nu_ref, xs_ref, w1_ref, w3_ref, w2_ref, o_ref, hb_ref, acc_ref):
    del te_ref
    i = pl.program_id(0)
    j = pl.program_id(1)

    used = i < nu_ref[0]

    @pl.when(j == 0)
    def _():
        acc_ref[...] = jnp.zeros_like(acc_ref)

    @pl.when(used & (j == 0))
    def _():
        hb_ref[...] = _from_row_tiled(xs_ref, (), TM_MOE).astype(BF16)

    @pl.when(used)
    def _():
        _swiglu_accumulate(hb_ref[...], w1_ref, w3_ref, w2_ref, acc_ref, TF_MOE)

    @pl.when(j == pl.num_programs(1) - 1)
    def _():
        _to_row_tiled(o_ref, (), acc_ref[...])


def _grouped_ffn(tile_expert, n_used, xs, w1, w3, w2):
    nj = FFN_EXPERT // TF_MOE
    col = lambda i, j, nu: jnp.where(i < nu[0], j, nj - 1)
    grid_spec = pltpu.PrefetchScalarGridSpec(
        num_scalar_prefetch=2,
        grid=(N_TILES_MOE, nj),
        in_specs=[
            pl.BlockSpec((TM_MOE * ROW_TILE, LANES), lambda i, j, te, nu: (jnp.minimum(i, nu[0] - 1), 0)),
            pl.BlockSpec((None, None, D_MODEL, TF_MOE), lambda i, j, te, nu: (te[i], col(i, j, nu), 0, 0)),
            pl.BlockSpec((None, None, D_MODEL, TF_MOE), lambda i, j, te, nu: (te[i], col(i, j, nu), 0, 0)),
            pl.BlockSpec((None, TF_MOE, D_MODEL), lambda i, j, te, nu: (te[i], col(i, j, nu), 0)),
        ],
        out_specs=pl.BlockSpec((TM_MOE * ROW_TILE, LANES), lambda i, j, te, nu: (i, 0)),
        scratch_shapes=[pltpu.VMEM((TM_MOE, D_MODEL), BF16), pltpu.VMEM((TM_MOE, D_MODEL), F32)],
    )
    return pl.pallas_call(
        _gffn_body,
        grid_spec=grid_spec,
        out_shape=jax.ShapeDtypeStruct((N_SORTED * ROW_TILE, LANES), F32),
        compiler_params=_params(("arbitrary", "arbitrary")),
        name="moe_ffn",
    )(tile_expert, n_used, xs, w1, w3, w2)


def _combine_body(pos_ref, x_ref, gate_ref, g_ref, ys_hbm, o_ref, buf_ref, sem):
    i = pl.program_id(0)
    n = pl.num_programs(0)
    tm = TM_COMBINE

    def issue(tile, slot):
        def body(t, carry):
            for k in range(2):
                src = pos_ref[2 * (tile * tm + t) + k]
                pltpu.make_async_copy(ys_hbm.at[_row_tile(src)], buf_ref.at[slot, k, _row_tile(t)],
                                      sem.at[slot]).start(priority=k)
            return carry
        lax.fori_loop(0, tm, body, 0, unroll=8)

    @pl.when(i == 0)
    def _():
        issue(0, 0)

    @pl.when(i + 1 < n)
    def _():
        issue(i + 1, (i + 1) % 2)

    slot = i % 2
    for k in range(2):
        pltpu.make_async_copy(ys_hbm.at[pl.ds(0, tm * ROW_TILE)], buf_ref.at[slot, k], sem.at[slot]).wait()
    w1 = gate_ref[:, 0:1]
    w2 = gate_ref[:, 1:2]
    y = x_ref[...] + w1 * _from_row_tiled(buf_ref, (slot, 0), tm) + w2 * _from_row_tiled(buf_ref, (slot, 1), tm)
    o_ref[...] = _rms(y, g_ref[...])


def _combine(pos, x, gates, g, ys):
    tm = TM_COMBINE
    return pl.pallas_call(
        _combine_body,
        grid=(TOKENS // tm,),
        in_specs=[pl.BlockSpec(memory_space=pltpu.SMEM),
                  pl.BlockSpec((tm, D_MODEL), lambda i: (i, 0)),
                  pl.BlockSpec((tm, 2), lambda i: (i, 0)),
                  pl.BlockSpec((1, D_MODEL), lambda i: (0, 0)),
                  pl.BlockSpec(memory_space=pl.ANY)],
        out_specs=pl.BlockSpec((tm, D_MODEL), lambda i: (i, 0)),
        out_shape=jax.ShapeDtypeStruct((TOKENS, D_MODEL), F32),
        scratch_shapes=[pltpu.VMEM((2, 2, tm * ROW_TILE, LANES), F32), pltpu.SemaphoreType.DMA((2,))],
        compiler_params=_params(("arbitrary",)),
        name="moe_combine",
    )(pos, x, gates, g, ys)


def _routing_tables(route, counts):
    cnt = counts[:, 0].astype(jnp.int32)
    tiles = (cnt + TM_MOE - 1) // TM_MOE
    tile_end = jnp.cumsum(tiles)
    tile_start = tile_end - tiles
    n_used = tile_end[-1]
    expert = route[ROUTE_I1:ROUTE_I2 + 1].astype(jnp.int32)
    rank = route[ROUTE_R1:ROUTE_R2 + 1].astype(jnp.int32)
    group_start = jnp.sum(jnp.where(expert[..., None] == jnp.arange(N_EXPERTS), tile_start * TM_MOE, 0), axis=-1)
    pos = (group_start + rank).T
    tile_id = jnp.minimum(jnp.arange(N_TILES_MOE, dtype=jnp.int32), n_used - 1)
    tile_expert = jnp.sum(tile_id[:, None] >= tile_end[None, :], axis=1).astype(jnp.int32)
    last_tile = jnp.where(tiles > 0, tile_end - 1, -1)
    spare = n_used + jnp.arange(N_EXPERTS)
    zero_tiles = jnp.concatenate([last_tile, jnp.where(spare < N_TILES_MOE, spare, -1)]).astype(jnp.int32)
    return pos.reshape(2 * TOKENS), tile_expert, n_used.reshape(1), zero_tiles


def _moe(x, h, route, counts, w1, w3, w2, g_final):
    pos, tile_expert, n_used, last_tile = _routing_tables(route, counts)
    xs = _dispatch(pos, last_tile, h)
    ys = _grouped_ffn(tile_expert, n_used, xs, w1, w3, w2)
    return _combine(pos, x, route[ROUTE_W1:ROUTE_W2 + 1].T, g_final, ys)


def _prep_w_in(w):
    aq, ak, av, gq, gk, gv, gr, glr, c_in, c_b, c_c = jnp.split(w, np.cumsum(SPLIT_SIZES)[:-1].tolist(), axis=2)
    pad = jnp.zeros(w.shape[:2] + (LANES - GLA_RANK,), w.dtype)
    return jnp.concatenate([aq, ak, av, gv, gr, gq, gk, c_in, c_b, c_c, glr, pad], axis=2).astype(BF16)


def _prep_router(w):
    wt = w.T
    hi = wt.astype(BF16)
    lo = (wt - hi.astype(F32)).astype(BF16)
    return jnp.stack([jnp.concatenate([hi, lo]), jnp.concatenate([hi, jnp.zeros_like(hi)])])


def kernel(x, w_mix_in, w_mix_out, g_mix, rel_bias, gla_w_gate, gla_b_gate, gla_g_norm, conv_w,
           g_ffn, ffn_w1, ffn_w3, ffn_w2, moe_router, moe_w1, moe_w3, moe_w2, g_final):
    assert DEPTH == 2
    x = x.reshape(TOKENS, D_MODEL)
    up_job = lambda w, tf: (w, w.shape[-1] // tf)
    down = lambda c: c.reshape(c.shape[0], c.shape[2], c.shape[3])
    experts = {}
    w_in = _prep_w_in(w_mix_in)
    for layer in range(DEPTH):
        u, qkvs, casts = _inproj(x, g_mix[layer].reshape(1, D_MODEL), w_in, layer,
                                 side_casts=[up_job(moe_w1[0], TF_MOE)] if layer == 1 else [])
        if layer == 1:
            experts["w1"] = casts[0]
        dense_jobs = [up_job(ffn_w1[:1], TF_DENSE), up_job(ffn_w3[:1], TF_DENSE), (ffn_w2[:1], 1)] if layer == 0 else []
        att, dense_bf16 = _attention(qkvs, rel_bias, dense_jobs)
        if layer == 0:
            ffn_bf16 = (dense_bf16[0][0], dense_bf16[1][0], down(dense_bf16[2])[0])
        wg = jnp.pad(gla_w_gate[layer], ((0, LANES - GLA_RANK), (0, 0)))
        go, cv = _gla_conv(u, wg, gla_b_gate[layer].reshape(1, GLA_QK), gla_g_norm[layer].reshape(1, GLA_DV),
                           jnp.pad(conv_w[layer], ((0, 8 - CONV_WIDTH), (0, 0))))
        w_out = w_mix_out[layer].astype(BF16)
        g2 = g_ffn[layer].reshape(1, D_MODEL)
        i = layer // 2
        if layer % 2 == 0:
            x, h = _outproj(att, go, cv, x, w_out, g2)
            x, (experts["w3"], cast) = _ffn(x, h, *ffn_bf16, side_casts=[up_job(moe_w3[i], TF_MOE), (moe_w2[i], 1)])
            experts["w2"] = down(cast)
        else:
            x, h, route, counts = _outproj(att, go, cv, x, w_out, g2, _prep_router(moe_router[i]))
            x = _moe(x, h, route, counts, experts["w1"], experts["w3"], experts["w2"], g_final.reshape(1, D_MODEL))
    return x.reshape(BATCH, SEQ, D_MODEL)
```

```python
import functools
import math

import jax
import jax.numpy as jnp
import numpy as np
from jax import lax
from jax.experimental import pallas as pl
from jax.experimental.pallas import tpu as pltpu

F32 = jnp.float32
BF16 = jnp.bfloat16

D_MODEL = 1024
BATCH = 8
SEQ = 2048
TOKENS = BATCH * SEQ
DEPTH = 2
EPS = 1e-6

HEAD_DIM = 64
ATT_HEADS = 4
ATT_WIDTH = ATT_HEADS * HEAD_DIM
DILATED_PATTERNS = ((128, 1), (512, 4), (2048, 16))
ATT_BLOCK = 128
REL_BUCKETS = 32
REL_MAX_DISTANCE = 2048

GLA_HEADS = 4
GLA_DK = 64
GLA_DV = 128
GLA_RANK = 16
GLA_CHUNK = 64
GLA_QK = GLA_HEADS * GLA_DK
GLA_V = GLA_HEADS * GLA_DV

CONV_CH = 256
CONV_WIDTH = 3
MIX_WIDTH = ATT_WIDTH + GLA_V + CONV_CH

SPLIT_SIZES = (ATT_WIDTH, ATT_WIDTH, ATT_WIDTH, GLA_QK, GLA_QK, GLA_V, GLA_V, GLA_RANK,
               CONV_CH, CONV_CH, CONV_CH)

FFN_DENSE = 2816
N_EXPERTS = 8
FFN_EXPERT = 3584

LANES = 128
MXU_WIDTH = 256
VMEM_LIMIT = 56 * 1024 * 1024

QKV_COLS = 3 * ATT_WIDTH
COL_GV, COL_GR, COL_GQ, COL_GK = 0, 512, 1024, 1280
COL_CIN, COL_CB, COL_CC, COL_GLR = 1536, 1792, 2048, 2304
U_COLS = COL_GLR + LANES

NEG_BIG = -1e30

ATT_UNROLL = 15
TM_PROJ = 512
TC_GLA = 1024
GLA_CUMSUM_SPAN = 256
TM_FFN = 512
TF_DENSE = 2816
FFN_SUB = 256
TM_MOE = 512
TF_MOE = 1792
N_TILES_MOE = 2 * TOKENS // TM_MOE + N_EXPERTS
N_SORTED = N_TILES_MOE * TM_MOE
DISPATCH_CHUNK = 512
TM_COMBINE = 512

ROUTE_I1, ROUTE_I2, ROUTE_W1, ROUTE_W2, ROUTE_R1, ROUTE_R2 = range(6)


def _params(sem):
    return pltpu.CompilerParams(dimension_semantics=sem, vmem_limit_bytes=VMEM_LIMIT)


def _split_bf16(a):
    hi = a.astype(BF16)
    lo = (a - hi.astype(F32)).astype(BF16)
    return hi, lo


def _dot(a, b):
    return jnp.dot(a, b, preferred_element_type=F32)


def _dot3(a, b):
    a_hi, a_lo = _split_bf16(a)
    b_hi, b_lo = _split_bf16(b)
    return _dot(a_hi, b_hi) + _dot(a_lo, b_hi) + _dot(a_hi, b_lo)


def _dot_nt(a, b):
    return lax.dot_general(a, b, (((1,), (1,)), ((), ())), preferred_element_type=F32)


def _rms(x, g):
    ms = jnp.mean(x * x, axis=-1, keepdims=True)
    return x * lax.rsqrt(ms + EPS) * g


def _sigmoid(x):
    return 1.0 / (1.0 + jnp.exp(-x))


def _side_cast(srcs, dsts):
    for src, dst in zip(srcs, dsts):
        width = dst.shape[-1]
        for s in range(dst.shape[0]):
            dst[s] = src[:, s * width:(s + 1) * width].astype(dst.dtype)


def _side_cast_io(jobs, steps, step_of):
    in_specs, out_specs, out_shapes = [], [], []
    for w, splits in jobs:
        g, r, c = w.shape
        rb = g * r // steps
        per_group = r // rb
        in_specs.append(pl.BlockSpec(
            (None, rb, c), lambda *ids, pg=per_group: (step_of(*ids) // pg, step_of(*ids) % pg, 0)))
        out_specs.append(pl.BlockSpec(
            (None, splits, rb, c // splits),
            lambda *ids, pg=per_group: (step_of(*ids) // pg, 0, step_of(*ids) % pg, 0)))
        out_shapes.append(jax.ShapeDtypeStruct((g, splits, r, c // splits), BF16))
    return in_specs, out_specs, out_shapes


def _w_in_moves():
    names = ("aq", "ak", "av", "gq", "gk", "gv", "gr", "glr", "c_in", "c_b", "c_c")
    src = dict(zip(names, zip(np.cumsum((0,) + SPLIT_SIZES[:-1]).tolist(), SPLIT_SIZES)))
    dst = {"aq": 0, "ak": ATT_WIDTH, "av": 2 * ATT_WIDTH, "gv": QKV_COLS + COL_GV, "gr": QKV_COLS + COL_GR,
           "gq": QKV_COLS + COL_GQ, "gk": QKV_COLS + COL_GK, "c_in": QKV_COLS + COL_CIN,
           "c_b": QKV_COLS + COL_CB, "c_c": QKV_COLS + COL_CC}
    return [(src[n][0], d, src[n][1]) for n, d in dst.items()], src["glr"][0]


def _stage_w_in(w_ref, wb_ref):
    moves, glr = _w_in_moves()
    for s, d, width in moves:
        wb_ref[:, d:d + width] = w_ref[:, s:s + width].astype(BF16)
    assert glr % LANES == 0
    lane = lax.broadcasted_iota(jnp.int32, (D_MODEL, LANES), 1)
    wb_ref[:, QKV_COLS + COL_GLR:] = jnp.where(lane < GLA_RANK, w_ref[:, glr:glr + LANES], 0.0).astype(BF16)


def _inproj_body(x_ref, g_ref, wf_ref, *rest):
    n_pat = len(DILATED_PATTERNS)
    n_cast = (len(rest) - n_pat - 3) // 2
    o_ref = rest[n_cast]
    qkv_refs = rest[n_cast + 1:n_cast + 1 + n_pat]
    qkv_f32, w_ref = rest[-2:]

    @pl.when(pl.program_id(0) == 0)
    def _():
        _stage_w_in(wf_ref, w_ref)

    _side_cast(rest[:n_cast], rest[n_cast + 1 + n_pat:-2])
    h = _rms(x_ref[...], g_ref[...]).astype(BF16)
    for c0 in range(0, QKV_COLS, MXU_WIDTH):
        res = _dot(h, w_ref[:, c0:c0 + MXU_WIDTH])
        for t in range(MXU_WIDTH // LANES):
            qkv_f32[c0 // LANES + t] = res[:, t * LANES:(t + 1) * LANES]
    for (_, dilation), ref in zip(DILATED_PATTERNS, qkv_refs):
        for r in range(dilation):
            rows = pl.ds(r, TM_PROJ // dilation, stride=dilation)
            ref[r] = jnp.concatenate([qkv_f32[t, rows, :] for t in range(QKV_COLS // LANES)],
                                     axis=-1).astype(ref.dtype)
    for c0 in range(0, U_COLS, MXU_WIDTH):
        c1 = min(c0 + MXU_WIDTH, U_COLS)
        o_ref[:, c0:c1] = _dot(h, w_ref[:, QKV_COLS + c0:QKV_COLS + c1])


def _subseq_spec(dilation, cols):
    tiles = SEQ // TM_PROJ
    return pl.BlockSpec((None, dilation, TM_PROJ // dilation, cols), lambda i: (i // tiles, 0, i % tiles, 0))


def _inproj(x, g, w_all, layer, side_casts=()):
    n_pat = len(DILATED_PATTERNS)
    qkv_shapes = [jax.ShapeDtypeStruct((BATCH, d, SEQ // d, QKV_COLS), BF16) for _, d in DILATED_PATTERNS]
    cast_in, cast_out, cast_shapes = _side_cast_io(side_casts, TOKENS // TM_PROJ, lambda i: i)
    u, *rest = pl.pallas_call(
        _inproj_body,
        grid=(TOKENS // TM_PROJ,),
        in_specs=[
            pl.BlockSpec((TM_PROJ, D_MODEL), lambda i: (i, 0)),
            pl.BlockSpec((1, D_MODEL), lambda i: (0, 0)),
            pl.BlockSpec((None, D_MODEL, sum(SPLIT_SIZES)), lambda i: (layer, 0, 0), pipeline_mode=pl.Buffered(1)),
        ] + cast_in,
        out_specs=[pl.BlockSpec((TM_PROJ, U_COLS), lambda i: (i, 0))]
                  + [_subseq_spec(d, QKV_COLS) for _, d in DILATED_PATTERNS] + cast_out,
        out_shape=[jax.ShapeDtypeStruct((TOKENS, U_COLS), F32)] + qkv_shapes + cast_shapes,
        scratch_shapes=[pltpu.VMEM((QKV_COLS // LANES, TM_PROJ, LANES), F32),
                        pltpu.VMEM((D_MODEL, QKV_COLS + U_COLS), BF16)],
        compiler_params=_params(("arbitrary",)),
        name="inproj",
    )(x, g, w_all, *[w for w, _ in side_casts])
    return u, rest[:n_pat], rest[n_pat:]


def _rel_bucket(dist):
    max_exact = REL_BUCKETS // 2
    d = jnp.maximum(dist, 0)
    log_ratio = jnp.log(jnp.maximum(d, 1).astype(F32) / max_exact) / math.log(REL_MAX_DISTANCE / max_exact)
    large = jnp.minimum(max_exact + (log_ratio * (REL_BUCKETS - max_exact)).astype(jnp.int32), REL_BUCKETS - 1)
    return jnp.where(d < max_exact, d, large)


def _bucket_table(window, dilation):
    span = window // dilation
    qi = jnp.arange(ATT_BLOCK)[:, None]
    kj = jnp.arange(2 * ATT_BLOCK)[None, :]
    sub_dist = qi - kj + ATT_BLOCK
    band = (sub_dist >= 0) & (sub_dist <= span)
    return jnp.where(band, _rel_bucket(sub_dist * dilation), -1).astype(jnp.int32)


def _attn_body(rb_ref, bidx_ref, qkv_ref, *rest, sub_blocks, unroll):
    n_cast = (len(rest) - 3) // 2
    o_ref, lse_ref = rest[n_cast:n_cast + 2]
    bias_ref = rest[-1]
    _side_cast(rest[:n_cast], rest[n_cast + 2:-1])
    nblk = SEQ // ATT_BLOCK

    @pl.when(pl.program_id(0) == 0)
    def _():
        bidx = bidx_ref[...]
        in_prev = lax.broadcasted_iota(jnp.int32, bidx.shape, 1) < ATT_BLOCK
        for h in range(ATT_HEADS):
            acc = jnp.full(bidx.shape, NEG_BIG, F32)
            for b in range(REL_BUCKETS):
                acc = jnp.where(bidx == b, rb_ref[b, h], acc)
            bias_ref[0, h] = acc
            bias_ref[1, h] = jnp.where(in_prev, NEG_BIG, acc)
            bias_ref[2, h] = jnp.concatenate([acc[:, ATT_BLOCK:], jnp.full_like(acc[:, ATT_BLOCK:], NEG_BIG)], axis=1)

    def block(n, first):
        if first:
            rows, krows, variant = slice(0, ATT_BLOCK), slice(0, 2 * ATT_BLOCK), 2
        else:
            r0 = pl.multiple_of(n * ATT_BLOCK, ATT_BLOCK)
            rows = pl.ds(r0, ATT_BLOCK)
            krows = pl.ds(r0 - ATT_BLOCK, 2 * ATT_BLOCK)
            if sub_blocks == nblk:
                variant = 0
            elif sub_blocks == 1:
                variant = 1
            else:
                variant = jnp.where(n % sub_blocks == 0, 1, 0)
        q = qkv_ref[rows, 0:ATT_WIDTH]
        kk = qkv_ref[krows, ATT_WIDTH:2 * ATT_WIDTH]
        vv = qkv_ref[krows, 2 * ATT_WIDTH:3 * ATT_WIDTH]
        q = q * jnp.asarray(HEAD_DIM ** -0.5, BF16)
        head_of_lane = lax.broadcasted_iota(jnp.int32, (ATT_BLOCK, ATT_WIDTH), 1) // HEAD_DIM
        ones = jnp.ones((kk.shape[0], LANES), BF16)
        num = den = mx = None
        for h in range(ATT_HEADS):
            mine = head_of_lane == h
            bias = bias_ref[variant, h]
            s = _dot_nt(jnp.where(mine, q, jnp.zeros_like(q)), kk) + bias
            m = jnp.max(s, axis=-1, keepdims=True)
            p = jnp.exp(s - m).astype(BF16)
            num_h = _dot(p, vv)
            den_h = jnp.tile(_dot(p, ones), (1, ATT_WIDTH // LANES))
            m_h = jnp.broadcast_to(m, (ATT_BLOCK, ATT_WIDTH))
            num = num_h if h == 0 else jnp.where(mine, num_h, num)
            den = den_h if h == 0 else jnp.where(mine, den_h, den)
            mx = m_h if h == 0 else jnp.where(mine, m_h, mx)
        o_ref[rows, :] = (num / den).astype(o_ref.dtype)
        lse_ref[rows, :] = mx + jnp.log(den)

    block(0, True)

    def loop_body(n, carry):
        block(n, False)
        return carry
    lax.fori_loop(1, nblk, loop_body, 0, unroll=unroll)


def _attention_pattern(ua, rel_bias, window, dilation, side_casts=()):
    L = SEQ // dilation
    shape = (BATCH, dilation, L, ATT_WIDTH)
    qkv_spec = pl.BlockSpec((None, SEQ, QKV_COLS), lambda b: (b, 0, 0))
    out_spec = pl.BlockSpec((None, SEQ, ATT_WIDTH), lambda b: (b, 0, 0))
    cast_in, cast_out, cast_shapes = _side_cast_io(side_casts, BATCH, lambda b: b)
    o, lse, *casts = pl.pallas_call(
        functools.partial(_attn_body, sub_blocks=L // ATT_BLOCK, unroll=ATT_UNROLL),
        grid=(BATCH,),
        in_specs=[
            pl.BlockSpec(memory_space=pltpu.SMEM),
            pl.BlockSpec((ATT_BLOCK, 2 * ATT_BLOCK), lambda b: (0, 0)),
            qkv_spec,
        ] + cast_in,
        out_specs=[out_spec, out_spec] + cast_out,
        out_shape=[jax.ShapeDtypeStruct((BATCH, SEQ, ATT_WIDTH), BF16),
                   jax.ShapeDtypeStruct((BATCH, SEQ, ATT_WIDTH), F32)] + cast_shapes,
        scratch_shapes=[pltpu.VMEM((3, ATT_HEADS, ATT_BLOCK, 2 * ATT_BLOCK), F32)],
        compiler_params=_params(("arbitrary",)),
        name=f"attn_d{dilation}",
    )(rel_bias, _bucket_table(window, dilation), ua.reshape(BATCH, SEQ, QKV_COLS), *[w for w, _ in side_casts])
    return (o.reshape(shape), lse.reshape(shape)), casts


def _attention(qkvs, rel_bias, side_casts=()):
    side_casts = list(side_casts) + [None] * (len(DILATED_PATTERNS) - len(side_casts))
    res = [_attention_pattern(ua, rel_bias, window, dilation, [] if w is None else [w])
           for ua, (window, dilation), w in zip(qkvs, DILATED_PATTERNS, side_casts)]
    return [r[0] for r in res], [c for r in res for c in r[1]]


def _short_conv(cin_ref, cb_ref, cc_ref, w_ref, o_ref, tail_ref):
    uu = cc_ref[...] * cin_ref[...]
    t = lax.broadcasted_iota(jnp.int32, uu.shape, 0)
    y = uu * w_ref[CONV_WIDTH - 1:CONV_WIDTH, :]
    for shift in range(1, CONV_WIDTH):
        prev = pltpu.roll(uu, shift, axis=0)
        for r in range(shift):
            prev = jnp.where(t == r, tail_ref[8 - shift + r:8 - shift + r + 1, :], prev)
        y = y + prev * w_ref[CONV_WIDTH - 1 - shift:CONV_WIDTH - shift, :]
    o_ref[...] = (cb_ref[...] * y).astype(o_ref.dtype)
    tail_ref[...] = uu[uu.shape[0] - 8:, :]


def _gla_body(u_ref, wg_ref, bg_ref, gn_ref, ctril_ref, cw_ref, *rest):
    cols = lambda col, width: u_ref.at[:, col:col + width]
    q_ref, k_ref = cols(COL_GQ, GLA_QK), cols(COL_GK, GLA_QK)
    v_ref, gr_ref, glr_ref = cols(COL_GV, GLA_V), cols(COL_GR, GLA_V), cols(COL_GLR, LANES)
    cin_ref, cb_ref, cc_ref = cols(COL_CIN, CONV_CH), cols(COL_CB, CONV_CH), cols(COL_CC, CONV_CH)
    _gla_tile(q_ref, k_ref, v_ref, gr_ref, glr_ref, wg_ref, bg_ref, gn_ref, ctril_ref,
              cin_ref, cb_ref, cc_ref, cw_ref, *rest)


def _gla_tile(q_ref, k_ref, v_ref, gr_ref, glr_ref, wg_ref, bg_ref, gn_ref, ctril_ref,
              cin_ref, cb_ref, cc_ref, cw_ref, o_ref, cv_ref, s_ref, tail_ref):
    @pl.when(pl.program_id(1) == 0)
    def _():
        s_ref[...] = jnp.zeros_like(s_ref)
        tail_ref[...] = jnp.zeros_like(tail_ref)

    _short_conv(cin_ref, cb_ref, cc_ref, cw_ref, cv_ref, tail_ref)

    C = GLA_CHUNK
    row = lax.broadcasted_iota(jnp.int32, (C, C), 0)
    col = lax.broadcasted_iota(jnp.int32, (C, C), 1)
    tril = row >= col
    n_chunks = TC_GLA // C

    xg = _dot3(glr_ref[...], wg_ref[...]) + bg_ref[...]
    la_all = (jnp.minimum(xg, 0.0) - jnp.log(1.0 + jnp.exp(-jnp.abs(xg)))) * (1.0 / 16.0)

    la_hi, la_lo = _split_bf16(la_all)
    span = ctril_ref.shape[0]
    cum_all = jnp.concatenate(
        [_dot(ctril_ref[...], la_hi[r0:r0 + span]) + _dot(ctril_ref[...], la_lo[r0:r0 + span])
         for r0 in range(0, TC_GLA, span)], axis=0)
    totals = jnp.concatenate([cum_all[(c + 1) * C - 1:(c + 1) * C, :] for c in range(n_chunks)]
                             + [jnp.zeros((LANES - n_chunks, GLA_QK), F32)], axis=0)
    decay_cols = jnp.exp(totals.T)

    for c in range(n_chunks):
        rows = slice(c * C, (c + 1) * C)
        cum = cum_all[rows]
        last = cum[C - 1:C, :]
        q = q_ref[rows, :]
        k = k_ref[rows, :]
        qs = q * (GLA_DK ** -0.5)
        qt = (qs * jnp.exp(cum)).astype(BF16)
        mid = cum[C // 2 - 1:C // 2, :]
        qm = (qs * jnp.exp(cum - mid)).astype(BF16)
        kt = (k * jnp.exp(mid - cum)).astype(BF16)
        kl_t = (k * jnp.exp(last - cum)).T.astype(BF16)
        for h in range(GLA_HEADS):
            sl = slice(h * GLA_DK, (h + 1) * GLA_DK)
            vs = slice(h * GLA_DV, (h + 1) * GLA_DV)
            vh = v_ref[rows, vs].astype(BF16)
            state = s_ref[h]
            st_hi, st_lo = _split_bf16(state)
            sc = jnp.where(tril, _dot_nt(qm[:, sl], kt[:, sl]), 0.0).astype(BF16)
            o = _dot(qt[:, sl], st_hi) + _dot(qt[:, sl], st_lo) + _dot(sc, vh)
            decay = jnp.broadcast_to(decay_cols[sl, c:c + 1], state.shape)
            s_ref[h] = decay * state + _dot(kl_t[sl, :], vh)
            g = gr_ref[rows, vs]
            o_ref[rows, vs] = (_rms(o, gn_ref[...]) * (g * _sigmoid(g))).astype(o_ref.dtype)


def _gla_conv(u, wg, bg, gn, conv_w):
    nj = SEQ // TC_GLA
    row = lambda b, j: b * nj + j
    full = lambda a: pl.BlockSpec(a.shape, lambda b, j: (0, 0))
    t = np.arange(GLA_CUMSUM_SPAN)
    same_chunk = (t[:, None] // GLA_CHUNK) == (t[None, :] // GLA_CHUNK)
    chunk_tril = jnp.asarray(same_chunk & (t[:, None] >= t[None, :]), BF16)
    return pl.pallas_call(
        _gla_body,
        grid=(BATCH, nj),
        in_specs=[pl.BlockSpec((TC_GLA, U_COLS), lambda b, j: (row(b, j), 0)),
                  full(wg), full(bg), full(gn), full(chunk_tril), full(conv_w)],
        out_specs=[pl.BlockSpec((TC_GLA, GLA_V), lambda b, j: (row(b, j), 0)),
                   pl.BlockSpec((TC_GLA, CONV_CH), lambda b, j: (row(b, j), 0))],
        out_shape=[jax.ShapeDtypeStruct((TOKENS, GLA_V), BF16), jax.ShapeDtypeStruct((TOKENS, CONV_CH), BF16)],
        scratch_shapes=[pltpu.VMEM((GLA_HEADS, GLA_DK, GLA_DV), F32), pltpu.VMEM((8, CONV_CH), F32)],
        compiler_params=_params(("arbitrary", "arbitrary")),
        name="gla_conv",
    )(u, wg, bg, gn, chunk_tril, conv_w)


def _outproj_body(*refs, route):
    (o1, o4, o16, l1, l4, l16, go_ref, cv_ref, x_ref, w_ref, g_ref) = refs[:11]
    perm_ref = refs[-1]
    if route:
        wr_ref, xo_ref, ho_ref, route_ref, counts_ref, carry_ref = refs[11:-1]
    else:
        xo_ref, ho_ref = refs[11:-1]

    def token_order(ref, slot):
        dilation, rows, _ = ref.shape
        if dilation == 1:
            return ref[0].astype(F32)
        tiles = range(ATT_WIDTH // LANES)
        for r in range(dilation):
            val = ref[r].astype(F32)
            for t in tiles:
                perm_ref[slot, t, pl.ds(r, rows, stride=dilation), :] = val[:, t * LANES:(t + 1) * LANES]
        return jnp.concatenate([perm_ref[slot, t] for t in tiles], axis=-1)

    la, lb, lc = token_order(l1, 0), token_order(l4, 0), token_order(l16, 1)
    oa, ob, oc = token_order(o1, 0), token_order(o4, 2), token_order(o16, 3)
    m = jnp.maximum(jnp.maximum(la, lb), lc)
    ea, eb, ec = jnp.exp(la - m), jnp.exp(lb - m), jnp.exp(lc - m)
    att = (ea * oa + eb * ob + ec * oc) / (ea + eb + ec)
    y = (x_ref[...]
         + _dot(att.astype(BF16), w_ref[0:ATT_WIDTH, :])
         + _dot(go_ref[...], w_ref[ATT_WIDTH:ATT_WIDTH + GLA_V, :])
         + _dot(cv_ref[...], w_ref[ATT_WIDTH + GLA_V:MIX_WIDTH, :]))
    xo_ref[...] = y
    hf = _rms(y, g_ref[...])
    ho_ref[...] = hf.astype(ho_ref.dtype)
    if route:
        @pl.when(pl.program_id(0) == 0)
        def _():
            carry_ref[...] = jnp.zeros_like(carry_ref)

        tm = hf.shape[0]
        ne = N_EXPERTS
        hf_hi, hf_lo = _split_bf16(hf)
        part = _dot_nt(wr_ref[0], hf_hi) + _dot_nt(wr_ref[1], hf_lo)
        logits = part[0:ne] + part[ne:2 * ne]
        eidx = lax.broadcasted_iota(jnp.int32, logits.shape, 0).astype(F32)
        v1 = jnp.max(logits, axis=0, keepdims=True)
        i1 = jnp.min(jnp.where(logits == v1, eidx, float(ne)), axis=0, keepdims=True)
        lg2 = jnp.where(eidx == i1, -jnp.inf, logits)
        v2 = jnp.max(lg2, axis=0, keepdims=True)
        i2 = jnp.min(jnp.where(lg2 == v2, eidx, float(ne)), axis=0, keepdims=True)
        e2 = jnp.exp(v2 - v1)
        w1 = 1.0 / (1.0 + e2)
        w2 = e2 * w1
        sel1 = eidx == i1
        sel2 = eidx == i2
        onehot = jnp.where(sel1, 1.0, jnp.where(sel2, 1.0, 0.0))
        tri = (lax.broadcasted_iota(jnp.int32, (tm, tm), 0) <= lax.broadcasted_iota(jnp.int32, (tm, tm), 1))
        onehot16 = jnp.concatenate([onehot, jnp.zeros_like(onehot)], axis=0).astype(BF16)
        csum = _dot(onehot16, jnp.where(tri, 1.0, 0.0).astype(BF16))[0:ne]
        carry = carry_ref[:, 0:1]
        rank = csum - onehot + carry
        r1 = jnp.sum(jnp.where(sel1, rank, 0.0), axis=0, keepdims=True)
        r2 = jnp.sum(jnp.where(sel2, rank, 0.0), axis=0, keepdims=True)
        total = jnp.broadcast_to(carry + csum[:, tm - 1:tm], carry_ref.shape)
        carry_ref[...] = total
        counts_ref[...] = total
        rows = {ROUTE_I1: i1, ROUTE_I2: i2, ROUTE_W1: w1, ROUTE_W2: w2, ROUTE_R1: r1, ROUTE_R2: r2}
        zero = jnp.zeros_like(i1)
        route_ref[...] = jnp.concatenate([rows.get(r, zero) for r in range(8)], axis=0)


def _outproj(att, go, cv, x, w, g, w_router=None):
    route = w_router is not None
    tm = TM_PROJ
    tile = lambda cols: pl.BlockSpec((tm, cols), lambda i: (i, 0))
    full = lambda a: pl.BlockSpec(a.shape, lambda i: (0, 0))
    (o1, l1), (o4, l4), (o16, l16) = att
    args = [o1, o4, o16, l1, l4, l16, go, cv, x, w, g]
    att_specs = [_subseq_spec(d, ATT_WIDTH) for _, d in DILATED_PATTERNS]
    in_specs = att_specs * 2 + [tile(GLA_V), tile(CONV_CH), tile(D_MODEL), full(w), full(g)]
    out_specs = [tile(D_MODEL), tile(D_MODEL)]
    out_shape = [jax.ShapeDtypeStruct((TOKENS, D_MODEL), F32),
                 jax.ShapeDtypeStruct((TOKENS, D_MODEL), F32 if route else BF16)]
    scratch = []
    if route:
        args.append(w_router)
        in_specs.append(pl.BlockSpec(w_router.shape, lambda i: (0, 0, 0)))
        out_specs += [pl.BlockSpec((8, tm), lambda i: (0, i)), pl.BlockSpec((N_EXPERTS, LANES), lambda i: (0, 0))]
        out_shape += [jax.ShapeDtypeStruct((8, TOKENS), F32), jax.ShapeDtypeStruct((N_EXPERTS, LANES), F32)]
        scratch = [pltpu.VMEM((N_EXPERTS, LANES), F32)]
    scratch.append(pltpu.VMEM((4, ATT_WIDTH // LANES, tm, LANES), F32))
    return pl.pallas_call(
        functools.partial(_outproj_body, route=route),
        grid=(TOKENS // tm,),
        in_specs=in_specs,
        out_specs=out_specs,
        out_shape=out_shape,
        scratch_shapes=scratch,
        compiler_params=_params(("arbitrary",) if route else ("parallel",)),
        name="outproj_route" if route else "outproj",
    )(*args)


def _swiglu_accumulate(h, w1_ref, w3_ref, w2_ref, acc_ref, tf):
    for c0 in range(0, tf, FFN_SUB):
        c1 = min(c0 + FFN_SUB, tf)
        a = _dot(h, w1_ref[:, c0:c1])
        b = _dot(h, w3_ref[:, c0:c1])
        act = a * _sigmoid(a) * b
        acc_ref[...] += _dot(act.astype(BF16), w2_ref[c0:c1, :])


def _ffn_body(x_ref, h_ref, w1_ref, w3_ref, w2_ref, *rest, tf):
    n_cast = (len(rest) - 1) // 2
    o_ref = rest[n_cast]

    @pl.when(pl.program_id(1) == 0)
    def _():
        o_ref[...] = x_ref[...]

    _swiglu_accumulate(h_ref[...], w1_ref, w3_ref, w2_ref, o_ref, tf)
    _side_cast(rest[:n_cast], rest[n_cast + 1:])


def _ffn(x, h, w1, w3, w2, *, side_casts=()):
    nj, _, tf = w1.shape
    tm = TM_FFN
    tile = lambda cols: pl.BlockSpec((tm, cols), lambda i, j: (i, 0))
    cast_in, cast_out, cast_shapes = _side_cast_io(side_casts, (TOKENS // tm) * nj, lambda i, j: i * nj + j)
    resident = dict(pipeline_mode=pl.Buffered(1)) if nj == 1 else {}
    res = pl.pallas_call(
        functools.partial(_ffn_body, tf=tf),
        grid=(TOKENS // tm, nj),
        in_specs=[tile(D_MODEL), tile(D_MODEL),
                  pl.BlockSpec((None, D_MODEL, tf), lambda i, j: (j, 0, 0), **resident),
                  pl.BlockSpec((None, D_MODEL, tf), lambda i, j: (j, 0, 0), **resident),
                  pl.BlockSpec((tf, D_MODEL), lambda i, j: (j, 0), **resident)] + cast_in,
        out_specs=[tile(D_MODEL)] + cast_out,
        out_shape=[jax.ShapeDtypeStruct((TOKENS, D_MODEL), F32)] + cast_shapes,
        compiler_params=_params(("parallel", "arbitrary")),
        name="dense_ffn",
    )(x, h, w1, w3, w2, *[w for w, _ in side_casts])
    return res[0], res[1:]


ROW_TILE = D_MODEL // LANES


def _to_row_tiled(dst_ref, lead, val):
    rows = val.shape[0]
    for s in range(ROW_TILE):
        dst_ref[(*lead, pl.ds(s, rows, stride=ROW_TILE), slice(None))] = val[:, s * LANES:(s + 1) * LANES]


def _from_row_tiled(src_ref, lead, rows):
    return jnp.concatenate([src_ref[(*lead, pl.ds(s, rows, stride=ROW_TILE), slice(None))]
                            for s in range(ROW_TILE)], axis=-1)


def _row_tile(idx):
    return pl.ds(pl.multiple_of(idx * ROW_TILE, ROW_TILE), ROW_TILE)


def _dispatch_body(pos_ref, last_tile_ref, h_ref, xs_hbm, stage_ref, zero_ref, sem, zero_sem):
    i = pl.program_id(0)
    n = pl.num_programs(0)
    tm = DISPATCH_CHUNK
    slot = i % 2

    def drain(s):
        for _ in range(2):
            pltpu.make_async_copy(stage_ref.at[s], xs_hbm.at[pl.ds(0, tm * ROW_TILE)], sem.at[s]).wait()

    @pl.when(i == 0)
    def _():
        zero_ref[...] = jnp.zeros_like(zero_ref)

        def zero_copy(e):
            start = pl.multiple_of(last_tile_ref[e] * (TM_MOE * ROW_TILE), TM_MOE * ROW_TILE)
            return pltpu.make_async_copy(zero_ref, xs_hbm.at[pl.ds(start, TM_MOE * ROW_TILE)], zero_sem)

        for e in range(2 * N_EXPERTS):
            @pl.when(last_tile_ref[e] >= 0)
            def _():
                zero_copy(e).start()
        for e in range(2 * N_EXPERTS):
            @pl.when(last_tile_ref[e] >= 0)
            def _():
                zero_copy(e).wait()

    @pl.when(i >= 2)
    def _():
        drain(slot)

    _to_row_tiled(stage_ref, (slot,), h_ref[...])

    def body(t, carry):
        for k in range(2):
            dst = pos_ref[2 * (i * tm + t) + k]
            pltpu.make_async_copy(stage_ref.at[slot, _row_tile(t)], xs_hbm.at[_row_tile(dst)],
                                  sem.at[slot]).start(priority=k)
        return carry
    lax.fori_loop(0, tm, body, 0, unroll=8)

    @pl.when(i == n - 1)
    def _():
        drain(1 - slot)
        drain(slot)


def _dispatch(pos, last_tile, h):
    tm = DISPATCH_CHUNK
    return pl.pallas_call(
        _dispatch_body,
        grid=(TOKENS // tm,),
        in_specs=[pl.BlockSpec(memory_space=pltpu.SMEM),
                  pl.BlockSpec(memory_space=pltpu.SMEM),
                  pl.BlockSpec((tm, D_MODEL), lambda i: (i, 0))],
        out_specs=pl.BlockSpec(memory_space=pl.ANY),
        out_shape=jax.ShapeDtypeStruct((N_SORTED * ROW_TILE, LANES), F32),
        scratch_shapes=[pltpu.VMEM((2, tm * ROW_TILE, LANES), F32),
                        pltpu.VMEM((TM_MOE * ROW_TILE, LANES), F32),
                        pltpu.SemaphoreType.DMA((2,)), pltpu.SemaphoreType.DMA(())],
        compiler_params=_params(("arbitrary",)),
        name="moe_dispatch",
    )(pos, last_tile, h)


def _gffn_body(te_ref, nu_ref, xs_ref, w1_ref, w3_ref, w2_ref, o_ref, hb_ref, acc_ref):
    del te_ref
    i = pl.program_id(0)
    j = pl.program_id(1)

    used = i < nu_ref[0]

    @pl.when(j == 0)
    def _():
        acc_ref[...] = jnp.zeros_like(acc_ref)

    @pl.when(used & (j == 0))
    def _():
        hb_ref[...] = _from_row_tiled(xs_ref, (), TM_MOE).astype(BF16)

    @pl.when(used)
    def _():
        _swiglu_accumulate(hb_ref[...], w1_ref, w3_ref, w2_ref, acc_ref, TF_MOE)

    @pl.when(j == pl.num_programs(1) - 1)
    def _():
        _to_row_tiled(o_ref, (), acc_ref[...])


def _grouped_ffn(tile_expert, n_used, xs, w1, w3, w2):
    nj = FFN_EXPERT // TF_MOE
    col = lambda i, j, nu: jnp.where(i < nu[0], j, nj - 1)
    grid_spec = pltpu.PrefetchScalarGridSpec(
        num_scalar_prefetch=2,
        grid=(N_TILES_MOE, nj),
        in_specs=[
            pl.BlockSpec((TM_MOE * ROW_TILE, LANES), lambda i, j, te, nu: (jnp.minimum(i, nu[0] - 1), 0)),
            pl.BlockSpec((None, None, D_MODEL, TF_MOE), lambda i, j, te, nu: (te[i], col(i, j, nu), 0, 0)),
            pl.BlockSpec((None, None, D_MODEL, TF_MOE), lambda i, j, te, nu: (te[i], col(i, j, nu), 0, 0)),
            pl.BlockSpec((None, TF_MOE, D_MODEL), lambda i, j, te, nu: (te[i], col(i, j, nu), 0)),
        ],
        out_specs=pl.BlockSpec((TM_MOE * ROW_TILE, LANES), lambda i, j, te, nu: (i, 0)),
        scratch_shapes=[pltpu.VMEM((TM_MOE, D_MODEL), BF16), pltpu.VMEM((TM_MOE, D_MODEL), F32)],
    )
    return pl.pallas_call(
        _gffn_body,
        grid_spec=grid_spec,
        out_shape=jax.ShapeDtypeStruct((N_SORTED * ROW_TILE, LANES), F32),
        compiler_params=_params(("arbitrary", "arbitrary")),
        name="moe_ffn",
    )(tile_expert, n_used, xs, w1, w3, w2)


def _combine_body(pos_ref, x_ref, gate_ref, g_ref, ys_hbm, o_ref, buf_ref, sem):
    i = pl.program_id(0)
    n = pl.num_programs(0)
    tm = TM_COMBINE

    def issue(tile, slot):
        def body(t, carry):
            for k in range(2):
                src = pos_ref[2 * (tile * tm + t) + k]
                pltpu.make_async_copy(ys_hbm.at[_row_tile(src)], buf_ref.at[slot, k, _row_tile(t)],
                                      sem.at[slot]).start(priority=k)
            return carry
        lax.fori_loop(0, tm, body, 0, unroll=8)

    @pl.when(i == 0)
    def _():
        issue(0, 0)

    @pl.when(i + 1 < n)
    def _():
        issue(i + 1, (i + 1) % 2)

    slot = i % 2
    for k in range(2):
        pltpu.make_async_copy(ys_hbm.at[pl.ds(0, tm * ROW_TILE)], buf_ref.at[slot, k], sem.at[slot]).wait()
    w1 = gate_ref[:, 0:1]
    w2 = gate_ref[:, 1:2]
    y = x_ref[...] + w1 * _from_row_tiled(buf_ref, (slot, 0), tm) + w2 * _from_row_tiled(buf_ref, (slot, 1), tm)
    o_ref[...] = _rms(y, g_ref[...])


def _combine(pos, x, gates, g, ys):
    tm = TM_COMBINE
    return pl.pallas_call(
        _combine_body,
        grid=(TOKENS // tm,),
        in_specs=[pl.BlockSpec(memory_space=pltpu.SMEM),
                  pl.BlockSpec((tm, D_MODEL), lambda i: (i, 0)),
                  pl.BlockSpec((tm, 2), lambda i: (i, 0)),
                  pl.BlockSpec((1, D_MODEL), lambda i: (0, 0)),
                  pl.BlockSpec(memory_space=pl.ANY)],
        out_specs=pl.BlockSpec((tm, D_MODEL), lambda i: (i, 0)),
        out_shape=jax.ShapeDtypeStruct((TOKENS, D_MODEL), F32),
        scratch_shapes=[pltpu.VMEM((2, 2, tm * ROW_TILE, LANES), F32), pltpu.SemaphoreType.DMA((2,))],
        compiler_params=_params(("arbitrary",)),
        name="moe_combine",
    )(pos, x, gates, g, ys)


def _routing_tables(route, counts):
    cnt = counts[:, 0].astype(jnp.int32)
    tiles = (cnt + TM_MOE - 1) // TM_MOE
    tile_end = jnp.cumsum(tiles)
    tile_start = tile_end - tiles
    n_used = tile_end[-1]
    expert = route[ROUTE_I1:ROUTE_I2 + 1].astype(jnp.int32)
    rank = route[ROUTE_R1:ROUTE_R2 + 1].astype(jnp.int32)
    group_start = jnp.sum(jnp.where(expert[..., None] == jnp.arange(N_EXPERTS), tile_start * TM_MOE, 0), axis=-1)
    pos = (group_start + rank).T
    tile_id = jnp.minimum(jnp.arange(N_TILES_MOE, dtype=jnp.int32), n_used - 1)
    tile_expert = jnp.sum(tile_id[:, None] >= tile_end[None, :], axis=1).astype(jnp.int32)
    last_tile = jnp.where(tiles > 0, tile_end - 1, -1)
    spare = n_used + jnp.arange(N_EXPERTS)
    zero_tiles = jnp.concatenate([last_tile, jnp.where(spare < N_TILES_MOE, spare, -1)]).astype(jnp.int32)
    return pos.reshape(2 * TOKENS), tile_expert, n_used.reshape(1), zero_tiles


def _moe(x, h, route, counts, w1, w3, w2, g_final):
    pos, tile_expert, n_used, last_tile = _routing_tables(route, counts)
    xs = _dispatch(pos, last_tile, h)
    ys = _grouped_ffn(tile_expert, n_used, xs, w1, w3, w2)
    return _combine(pos, x, route[ROUTE_W1:ROUTE_W2 + 1].T, g_final, ys)


def _prep_router(w):
    wt = w.T
    hi = wt.astype(BF16)
    lo = (wt - hi.astype(F32)).astype(BF16)
    return jnp.stack([jnp.concatenate([hi, lo]), jnp.concatenate([hi, jnp.zeros_like(hi)])])


def kernel(x, w_mix_in, w_mix_out, g_mix, rel_bias, gla_w_gate, gla_b_gate, gla_g_norm, conv_w,
           g_ffn, ffn_w1, ffn_w3, ffn_w2, moe_router, moe_w1, moe_w3, moe_w2, g_final):
    assert DEPTH == 2
    x = x.reshape(TOKENS, D_MODEL)
    up_job = lambda w, tf: (w, w.shape[-1] // tf)
    down = lambda c: c.reshape(c.shape[0], c.shape[2], c.shape[3])
    experts = {}
    for layer in range(DEPTH):
        u, qkvs, casts = _inproj(x, g_mix[layer].reshape(1, D_MODEL), w_mix_in, layer,
                                 side_casts=[up_job(moe_w1[0], TF_MOE)] if layer == 1 else [])
        if layer == 1:
            experts["w1"] = casts[0]
        dense_jobs = [up_job(ffn_w1[:1], TF_DENSE), up_job(ffn_w3[:1], TF_DENSE), (ffn_w2[:1], 1)] if layer == 0 else []
        att, dense_bf16 = _attention(qkvs, rel_bias, dense_jobs)
        if layer == 0:
            ffn_bf16 = (dense_bf16[0][0], dense_bf16[1][0], down(dense_bf16[2])[0])
        wg = jnp.pad(gla_w_gate[layer], ((0, LANES - GLA_RANK), (0, 0)))
        go, cv = _gla_conv(u, wg, gla_b_gate[layer].reshape(1, GLA_QK), gla_g_norm[layer].reshape(1, GLA_DV),
                           jnp.pad(conv_w[layer], ((0, 8 - CONV_WIDTH), (0, 0))))
        w_out = w_mix_out[layer].astype(BF16)
        g2 = g_ffn[layer].reshape(1, D_MODEL)
        i = layer // 2
        if layer % 2 == 0:
            x, h = _outproj(att, go, cv, x, w_out, g2)
            x, (experts["w3"], cast) = _ffn(x, h, *ffn_bf16, side_casts=[up_job(moe_w3[i], TF_MOE), (moe_w2[i], 1)])
            experts["w2"] = down(cast)
        else:
            x, h, route, counts = _outproj(att, go, cv, x, w_out, g2, _prep_router(moe_router[i]))
            x = _moe(x, h, route, counts, experts["w1"], experts["w3"], experts["w2"], g_final.reshape(1, D_MODEL))
    return x.reshape(BATCH, SEQ, D_MODEL)
```

```python
import functools
import math

import jax
import jax.numpy as jnp
import numpy as np
from jax import lax
from jax.experimental import pallas as pl
from jax.experimental.pallas import tpu as pltpu

F32 = jnp.float32
BF16 = jnp.bfloat16

D_MODEL = 1024
BATCH = 8
SEQ = 2048
TOKENS = BATCH * SEQ
DEPTH = 2
EPS = 1e-6

HEAD_DIM = 64
ATT_HEADS = 4
ATT_WIDTH = ATT_HEADS * HEAD_DIM
DILATED_PATTERNS = ((128, 1), (512, 4), (2048, 16))
ATT_BLOCK = 128
REL_BUCKETS = 32
REL_MAX_DISTANCE = 2048

GLA_HEADS = 4
GLA_DK = 64
GLA_DV = 128
GLA_RANK = 16
GLA_CHUNK = 64
GLA_QK = GLA_HEADS * GLA_DK
GLA_V = GLA_HEADS * GLA_DV

CONV_CH = 256
CONV_WIDTH = 3
MIX_WIDTH = ATT_WIDTH + GLA_V + CONV_CH

SPLIT_SIZES = (ATT_WIDTH, ATT_WIDTH, ATT_WIDTH, GLA_QK, GLA_QK, GLA_V, GLA_V, GLA_RANK,
               CONV_CH, CONV_CH, CONV_CH)

FFN_DENSE = 2816
N_EXPERTS = 8
FFN_EXPERT = 3584

LANES = 128
MXU_WIDTH = 256
VMEM_LIMIT = 56 * 1024 * 1024

QKV_COLS = 3 * ATT_WIDTH
COL_GV, COL_GR, COL_GQ, COL_GK = 0, 512, 1024, 1280
COL_CIN, COL_CB, COL_CC, COL_GLR = 1536, 1792, 2048, 2304
U_COLS = COL_GLR + LANES

NEG_BIG = -1e30

ATT_UNROLL = 15
TM_PROJ = 512
TC_GLA = 1024
GLA_CUMSUM_SPAN = 256
TM_FFN = 512
TF_DENSE = 2816
FFN_SUB = 256
TM_MOE = 512
TF_MOE = 1792
N_TILES_MOE = 2 * TOKENS // TM_MOE + N_EXPERTS
N_SORTED = N_TILES_MOE * TM_MOE
DISPATCH_CHUNK = 512
TM_COMBINE = 512

ROUTE_I1, ROUTE_I2, ROUTE_W1, ROUTE_W2, ROUTE_R1, ROUTE_R2 = range(6)


def _params(sem):
    return pltpu.CompilerParams(dimension_semantics=sem, vmem_limit_bytes=VMEM_LIMIT)


def _split_bf16(a):
    hi = a.astype(BF16)
    lo = (a - hi.astype(F32)).astype(BF16)
    return hi, lo


def _dot(a, b):
    return jnp.dot(a, b, preferred_element_type=F32)


def _dot3(a, b):
    a_hi, a_lo = _split_bf16(a)
    b_hi, b_lo = _split_bf16(b)
    return _dot(a_hi, b_hi) + _dot(a_lo, b_hi) + _dot(a_hi, b_lo)


def _dot_nt(a, b):
    return lax.dot_general(a, b, (((1,), (1,)), ((), ())), preferred_element_type=F32)


def _rms(x, g):
    ms = jnp.mean(x * x, axis=-1, keepdims=True)
    return x * lax.rsqrt(ms + EPS) * g


def _sigmoid(x):
    return 1.0 / (1.0 + jnp.exp(-x))


def _side_cast(srcs, dsts):
    for src, dst in zip(srcs, dsts):
        width = dst.shape[-1]
        for s in range(dst.shape[0]):
            dst[s] = src[:, s * width:(s + 1) * width].astype(dst.dtype)


def _side_cast_io(jobs, steps, step_of):
    in_specs, out_specs, out_shapes = [], [], []
    for w, splits in jobs:
        g, r, c = w.shape
        rb = g * r // steps
        per_group = r // rb
        in_specs.append(pl.BlockSpec(
            (None, rb, c), lambda *ids, pg=per_group: (step_of(*ids) // pg, step_of(*ids) % pg, 0)))
        out_specs.append(pl.BlockSpec(
            (None, splits, rb, c // splits),
            lambda *ids, pg=per_group: (step_of(*ids) // pg, 0, step_of(*ids) % pg, 0)))
        out_shapes.append(jax.ShapeDtypeStruct((g, splits, r, c // splits), BF16))
    return in_specs, out_specs, out_shapes


def _w_in_moves():
    names = ("aq", "ak", "av", "gq", "gk", "gv", "gr", "glr", "c_in", "c_b", "c_c")
    src = dict(zip(names, zip(np.cumsum((0,) + SPLIT_SIZES[:-1]).tolist(), SPLIT_SIZES)))
    dst = {"aq": 0, "ak": ATT_WIDTH, "av": 2 * ATT_WIDTH, "gv": QKV_COLS + COL_GV, "gr": QKV_COLS + COL_GR,
           "gq": QKV_COLS + COL_GQ, "gk": QKV_COLS + COL_GK, "c_in": QKV_COLS + COL_CIN,
           "c_b": QKV_COLS + COL_CB, "c_c": QKV_COLS + COL_CC}
    return [(src[n][0], d, src[n][1]) for n, d in dst.items()], src["glr"][0]


def _stage_w_in(wt_ref, wb_ref):
    moves, glr = _w_in_moves()
    for s, d, width in moves:
        for c in range(0, width, MXU_WIDTH):
            wb_ref[:, d + c:d + c + MXU_WIDTH] = wt_ref[s + c:s + c + MXU_WIDTH, :].T.astype(BF16)
    row = lax.broadcasted_iota(jnp.int32, (LANES, D_MODEL), 0)
    wb_ref[:, QKV_COLS + COL_GLR:] = jnp.where(row < GLA_RANK, wt_ref[glr:glr + LANES, :], 0.0).T.astype(BF16)


def _inproj_body(x_ref, g_ref, wf_ref, *rest):
    n_pat = len(DILATED_PATTERNS)
    n_cast = (len(rest) - n_pat - 3) // 2
    o_ref = rest[n_cast]
    qkv_refs = rest[n_cast + 1:n_cast + 1 + n_pat]
    qkv_f32, w_ref = rest[-2:]

    @pl.when(pl.program_id(0) == 0)
    def _():
        _stage_w_in(wf_ref, w_ref)

    _side_cast(rest[:n_cast], rest[n_cast + 1 + n_pat:-2])
    h = _rms(x_ref[...], g_ref[...]).astype(BF16)
    for c0 in range(0, QKV_COLS, MXU_WIDTH):
        res = _dot(h, w_ref[:, c0:c0 + MXU_WIDTH])
        for t in range(MXU_WIDTH // LANES):
            qkv_f32[c0 // LANES + t] = res[:, t * LANES:(t + 1) * LANES]
    for (_, dilation), ref in zip(DILATED_PATTERNS, qkv_refs):
        for r in range(dilation):
            rows = pl.ds(r, TM_PROJ // dilation, stride=dilation)
            ref[r] = jnp.concatenate([qkv_f32[t, rows, :] for t in range(QKV_COLS // LANES)],
                                     axis=-1).astype(ref.dtype)
    for c0 in range(0, U_COLS, MXU_WIDTH):
        c1 = min(c0 + MXU_WIDTH, U_COLS)
        o_ref[:, c0:c1] = _dot(h, w_ref[:, QKV_COLS + c0:QKV_COLS + c1])


def _subseq_spec(dilation, cols):
    tiles = SEQ // TM_PROJ
    return pl.BlockSpec((None, dilation, TM_PROJ // dilation, cols), lambda i: (i // tiles, 0, i % tiles, 0))


def _inproj(x, g, w_all, layer, side_casts=()):
    n_pat = len(DILATED_PATTERNS)
    qkv_shapes = [jax.ShapeDtypeStruct((BATCH, d, SEQ // d, QKV_COLS), BF16) for _, d in DILATED_PATTERNS]
    cast_in, cast_out, cast_shapes = _side_cast_io(side_casts, TOKENS // TM_PROJ, lambda i: i)
    u, *rest = pl.pallas_call(
        _inproj_body,
        grid=(TOKENS // TM_PROJ,),
        in_specs=[
            pl.BlockSpec((TM_PROJ, D_MODEL), lambda i: (i, 0)),
            pl.BlockSpec((1, D_MODEL), lambda i: (0, 0)),
            pl.BlockSpec((None, sum(SPLIT_SIZES), D_MODEL), lambda i: (layer, 0, 0), pipeline_mode=pl.Buffered(1)),
        ] + cast_in,
        out_specs=[pl.BlockSpec((TM_PROJ, U_COLS), lambda i: (i, 0))]
                  + [_subseq_spec(d, QKV_COLS) for _, d in DILATED_PATTERNS] + cast_out,
        out_shape=[jax.ShapeDtypeStruct((TOKENS, U_COLS), F32)] + qkv_shapes + cast_shapes,
        scratch_shapes=[pltpu.VMEM((QKV_COLS // LANES, TM_PROJ, LANES), F32),
                        pltpu.VMEM((D_MODEL, QKV_COLS + U_COLS), BF16)],
        compiler_params=_params(("arbitrary",)),
        name="inproj",
    )(x, g, w_all, *[w for w, _ in side_casts])
    return u, rest[:n_pat], rest[n_pat:]


def _rel_bucket(dist):
    max_exact = REL_BUCKETS // 2
    d = jnp.maximum(dist, 0)
    log_ratio = jnp.log(jnp.maximum(d, 1).astype(F32) / max_exact) / math.log(REL_MAX_DISTANCE / max_exact)
    large = jnp.minimum(max_exact + (log_ratio * (REL_BUCKETS - max_exact)).astype(jnp.int32), REL_BUCKETS - 1)
    return jnp.where(d < max_exact, d, large)


def _bucket_table(window, dilation):
    span = window // dilation
    qi = jnp.arange(ATT_BLOCK)[:, None]
    kj = jnp.arange(2 * ATT_BLOCK)[None, :]
    sub_dist = qi - kj + ATT_BLOCK
    band = (sub_dist >= 0) & (sub_dist <= span)
    return jnp.where(band, _rel_bucket(sub_dist * dilation), -1).astype(jnp.int32)


def _attn_body(rb_ref, bidx_ref, qkv_ref, *rest, sub_blocks, unroll):
    n_cast = (len(rest) - 3) // 2
    o_ref, lse_ref = rest[n_cast:n_cast + 2]
    bias_ref = rest[-1]
    _side_cast(rest[:n_cast], rest[n_cast + 2:-1])
    nblk = SEQ // ATT_BLOCK

    @pl.when(pl.program_id(0) == 0)
    def _():
        bidx = bidx_ref[...]
        in_prev = lax.broadcasted_iota(jnp.int32, bidx.shape, 1) < ATT_BLOCK
        for h in range(ATT_HEADS):
            acc = jnp.full(bidx.shape, NEG_BIG, F32)
            for b in range(REL_BUCKETS):
                acc = jnp.where(bidx == b, rb_ref[b, h], acc)
            bias_ref[0, h] = acc
            bias_ref[1, h] = jnp.where(in_prev, NEG_BIG, acc)
            bias_ref[2, h] = jnp.concatenate([acc[:, ATT_BLOCK:], jnp.full_like(acc[:, ATT_BLOCK:], NEG_BIG)], axis=1)

    def block(n, first):
        if first:
            rows, krows, variant = slice(0, ATT_BLOCK), slice(0, 2 * ATT_BLOCK), 2
        else:
            r0 = pl.multiple_of(n * ATT_BLOCK, ATT_BLOCK)
            rows = pl.ds(r0, ATT_BLOCK)
            krows = pl.ds(r0 - ATT_BLOCK, 2 * ATT_BLOCK)
            if sub_blocks == nblk:
                variant = 0
            elif sub_blocks == 1:
                variant = 1
            else:
                variant = jnp.where(n % sub_blocks == 0, 1, 0)
        q = qkv_ref[rows, 0:ATT_WIDTH]
        kk = qkv_ref[krows, ATT_WIDTH:2 * ATT_WIDTH]
        vv = qkv_ref[krows, 2 * ATT_WIDTH:3 * ATT_WIDTH]
        q = q * jnp.asarray(HEAD_DIM ** -0.5, BF16)
        head_of_lane = lax.broadcasted_iota(jnp.int32, (ATT_BLOCK, ATT_WIDTH), 1) // HEAD_DIM
        ones = jnp.ones((kk.shape[0], LANES), BF16)
        num = den = mx = None
        for h in range(ATT_HEADS):
            mine = head_of_lane == h
            bias = bias_ref[variant, h]
            s = _dot_nt(jnp.where(mine, q, jnp.zeros_like(q)), kk) + bias
            m = jnp.max(s, axis=-1, keepdims=True)
            p = jnp.exp(s - m).astype(BF16)
            num_h = _dot(p, vv)
            den_h = jnp.tile(_dot(p, ones), (1, ATT_WIDTH // LANES))
            m_h = jnp.broadcast_to(m, (ATT_BLOCK, ATT_WIDTH))
            num = num_h if h == 0 else jnp.where(mine, num_h, num)
            den = den_h if h == 0 else jnp.where(mine, den_h, den)
            mx = m_h if h == 0 else jnp.where(mine, m_h, mx)
        o_ref[rows, :] = (num / den).astype(o_ref.dtype)
        lse_ref[rows, :] = mx + jnp.log(den)

    block(0, True)

    def loop_body(n, carry):
        block(n, False)
        return carry
    lax.fori_loop(1, nblk, loop_body, 0, unroll=unroll)


def _attention_pattern(ua, rel_bias, window, dilation, side_casts=()):
    L = SEQ // dilation
    shape = (BATCH, dilation, L, ATT_WIDTH)
    qkv_spec = pl.BlockSpec((None, SEQ, QKV_COLS), lambda b: (b, 0, 0))
    out_spec = pl.BlockSpec((None, SEQ, ATT_WIDTH), lambda b: (b, 0, 0))
    cast_in, cast_out, cast_shapes = _side_cast_io(side_casts, BATCH, lambda b: b)
    o, lse, *casts = pl.pallas_call(
        functools.partial(_attn_body, sub_blocks=L // ATT_BLOCK, unroll=ATT_UNROLL),
        grid=(BATCH,),
        in_specs=[
            pl.BlockSpec(memory_space=pltpu.SMEM),
            pl.BlockSpec((ATT_BLOCK, 2 * ATT_BLOCK), lambda b: (0, 0)),
            qkv_spec,
        ] + cast_in,
        out_specs=[out_spec, out_spec] + cast_out,
        out_shape=[jax.ShapeDtypeStruct((BATCH, SEQ, ATT_WIDTH), BF16),
                   jax.ShapeDtypeStruct((BATCH, SEQ, ATT_WIDTH), F32)] + cast_shapes,
        scratch_shapes=[pltpu.VMEM((3, ATT_HEADS, ATT_BLOCK, 2 * ATT_BLOCK), F32)],
        compiler_params=_params(("arbitrary",)),
        name=f"attn_d{dilation}",
    )(rel_bias, _bucket_table(window, dilation), ua.reshape(BATCH, SEQ, QKV_COLS), *[w for w, _ in side_casts])
    return (o.reshape(shape), lse.reshape(shape)), casts


def _attention(qkvs, rel_bias, side_casts=()):
    side_casts = list(side_casts) + [None] * (len(DILATED_PATTERNS) - len(side_casts))
    res = [_attention_pattern(ua, rel_bias, window, dilation, [] if w is None else [w])
           for ua, (window, dilation), w in zip(qkvs, DILATED_PATTERNS, side_casts)]
    return [r[0] for r in res], [c for r in res for c in r[1]]


def _short_conv(cin_ref, cb_ref, cc_ref, w_ref, o_ref, tail_ref):
    uu = cc_ref[...] * cin_ref[...]
    t = lax.broadcasted_iota(jnp.int32, uu.shape, 0)
    y = uu * w_ref[CONV_WIDTH - 1:CONV_WIDTH, :]
    for shift in range(1, CONV_WIDTH):
        prev = pltpu.roll(uu, shift, axis=0)
        for r in range(shift):
            prev = jnp.where(t == r, tail_ref[8 - shift + r:8 - shift + r + 1, :], prev)
        y = y + prev * w_ref[CONV_WIDTH - 1 - shift:CONV_WIDTH - shift, :]
    o_ref[...] = (cb_ref[...] * y).astype(o_ref.dtype)
    tail_ref[...] = uu[uu.shape[0] - 8:, :]


def _gla_body(u_ref, wg_ref, bg_ref, gn_ref, ctril_ref, cw_ref, *rest):
    cols = lambda col, width: u_ref.at[:, col:col + width]
    q_ref, k_ref = cols(COL_GQ, GLA_QK), cols(COL_GK, GLA_QK)
    v_ref, gr_ref, glr_ref = cols(COL_GV, GLA_V), cols(COL_GR, GLA_V), cols(COL_GLR, LANES)
    cin_ref, cb_ref, cc_ref = cols(COL_CIN, CONV_CH), cols(COL_CB, CONV_CH), cols(COL_CC, CONV_CH)
    _gla_tile(q_ref, k_ref, v_ref, gr_ref, glr_ref, wg_ref, bg_ref, gn_ref, ctril_ref,
              cin_ref, cb_ref, cc_ref, cw_ref, *rest)


def _gla_tile(q_ref, k_ref, v_ref, gr_ref, glr_ref, wg_ref, bg_ref, gn_ref, ctril_ref,
              cin_ref, cb_ref, cc_ref, cw_ref, o_ref, cv_ref, s_ref, tail_ref):
    @pl.when(pl.program_id(1) == 0)
    def _():
        s_ref[...] = jnp.zeros_like(s_ref)
        tail_ref[...] = jnp.zeros_like(tail_ref)

    _short_conv(cin_ref, cb_ref, cc_ref, cw_ref, cv_ref, tail_ref)

    C = GLA_CHUNK
    row = lax.broadcasted_iota(jnp.int32, (C, C), 0)
    col = lax.broadcasted_iota(jnp.int32, (C, C), 1)
    tril = row >= col
    n_chunks = TC_GLA // C

    xg = _dot3(glr_ref[...], wg_ref[...]) + bg_ref[...]
    la_all = (jnp.minimum(xg, 0.0) - jnp.log(1.0 + jnp.exp(-jnp.abs(xg)))) * (1.0 / 16.0)

    la_hi, la_lo = _split_bf16(la_all)
    span = ctril_ref.shape[0]
    cum_all = jnp.concatenate(
        [_dot(ctril_ref[...], la_hi[r0:r0 + span]) + _dot(ctril_ref[...], la_lo[r0:r0 + span])
         for r0 in range(0, TC_GLA, span)], axis=0)
    totals = jnp.concatenate([cum_all[(c + 1) * C - 1:(c + 1) * C, :] for c in range(n_chunks)]
                             + [jnp.zeros((LANES - n_chunks, GLA_QK), F32)], axis=0)
    decay_cols = jnp.exp(totals.T)

    for c in range(n_chunks):
        rows = slice(c * C, (c + 1) * C)
        cum = cum_all[rows]
        last = cum[C - 1:C, :]
        q = q_ref[rows, :]
        k = k_ref[rows, :]
        qs = q * (GLA_DK ** -0.5)
        qt = (qs * jnp.exp(cum)).astype(BF16)
        mid = cum[C // 2 - 1:C // 2, :]
        qm = (qs * jnp.exp(cum - mid)).astype(BF16)
        kt = (k * jnp.exp(mid - cum)).astype(BF16)
        kl_t = (k * jnp.exp(last - cum)).T.astype(BF16)
        for h in range(GLA_HEADS):
            sl = slice(h * GLA_DK, (h + 1) * GLA_DK)
            vs = slice(h * GLA_DV, (h + 1) * GLA_DV)
            vh = v_ref[rows, vs].astype(BF16)
            state = s_ref[h]
            st_hi, st_lo = _split_bf16(state)
            sc = jnp.where(tril, _dot_nt(qm[:, sl], kt[:, sl]), 0.0).astype(BF16)
            o = _dot(qt[:, sl], st_hi) + _dot(qt[:, sl], st_lo) + _dot(sc, vh)
            decay = jnp.broadcast_to(decay_cols[sl, c:c + 1], state.shape)
            s_ref[h] = decay * state + _dot(kl_t[sl, :], vh)
            g = gr_ref[rows, vs]
            o_ref[rows, vs] = (_rms(o, gn_ref[...]) * (g * _sigmoid(g))).astype(o_ref.dtype)


def _gla_conv(u, wg, bg, gn, conv_w):
    nj = SEQ // TC_GLA
    row = lambda b, j: b * nj + j
    full = lambda a: pl.BlockSpec(a.shape, lambda b, j: (0, 0))
    t = np.arange(GLA_CUMSUM_SPAN)
    same_chunk = (t[:, None] // GLA_CHUNK) == (t[None, :] // GLA_CHUNK)
    chunk_tril = jnp.asarray(same_chunk & (t[:, None] >= t[None, :]), BF16)
    return pl.pallas_call(
        _gla_body,
        grid=(BATCH, nj),
        in_specs=[pl.BlockSpec((TC_GLA, U_COLS), lambda b, j: (row(b, j), 0)),
                  full(wg), full(bg), full(gn), full(chunk_tril), full(conv_w)],
        out_specs=[pl.BlockSpec((TC_GLA, GLA_V), lambda b, j: (row(b, j), 0)),
                   pl.BlockSpec((TC_GLA, CONV_CH), lambda b, j: (row(b, j), 0))],
        out_shape=[jax.ShapeDtypeStruct((TOKENS, GLA_V), BF16), jax.ShapeDtypeStruct((TOKENS, CONV_CH), BF16)],
        scratch_shapes=[pltpu.VMEM((GLA_HEADS, GLA_DK, GLA_DV), F32), pltpu.VMEM((8, CONV_CH), F32)],
        compiler_params=_params(("arbitrary", "arbitrary")),
        name="gla_conv",
    )(u, wg, bg, gn, chunk_tril, conv_w)


def _outproj_body(*refs, route):
    (o1, o4, o16, l1, l4, l16, go_ref, cv_ref, x_ref, w_ref, g_ref) = refs[:11]
    perm_ref = refs[-1]
    if route:
        wr_ref, xo_ref, ho_ref, route_ref, counts_ref, carry_ref = refs[11:-1]
    else:
        xo_ref, ho_ref = refs[11:-1]

    def token_order(ref, slot):
        dilation, rows, _ = ref.shape
        if dilation == 1:
            return ref[0].astype(F32)
        tiles = range(ATT_WIDTH // LANES)
        for r in range(dilation):
            val = ref[r].astype(F32)
            for t in tiles:
                perm_ref[slot, t, pl.ds(r, rows, stride=dilation), :] = val[:, t * LANES:(t + 1) * LANES]
        return jnp.concatenate([perm_ref[slot, t] for t in tiles], axis=-1)

    la, lb, lc = token_order(l1, 0), token_order(l4, 0), token_order(l16, 1)
    oa, ob, oc = token_order(o1, 0), token_order(o4, 2), token_order(o16, 3)
    m = jnp.maximum(jnp.maximum(la, lb), lc)
    ea, eb, ec = jnp.exp(la - m), jnp.exp(lb - m), jnp.exp(lc - m)
    att = (ea * oa + eb * ob + ec * oc) / (ea + eb + ec)
    y = (x_ref[...]
         + _dot(att.astype(BF16), w_ref[0:ATT_WIDTH, :])
         + _dot(go_ref[...], w_ref[ATT_WIDTH:ATT_WIDTH + GLA_V, :])
         + _dot(cv_ref[...], w_ref[ATT_WIDTH + GLA_V:MIX_WIDTH, :]))
    xo_ref[...] = y
    hf = _rms(y, g_ref[...])
    ho_ref[...] = hf.astype(ho_ref.dtype)
    if route:
        @pl.when(pl.program_id(0) == 0)
        def _():
            carry_ref[...] = jnp.zeros_like(carry_ref)

        tm = hf.shape[0]
        ne = N_EXPERTS
        hf_hi, hf_lo = _split_bf16(hf)
        part = _dot_nt(wr_ref[0], hf_hi) + _dot_nt(wr_ref[1], hf_lo)
        logits = part[0:ne] + part[ne:2 * ne]
        eidx = lax.broadcasted_iota(jnp.int32, logits.shape, 0).astype(F32)
        v1 = jnp.max(logits, axis=0, keepdims=True)
        i1 = jnp.min(jnp.where(logits == v1, eidx, float(ne)), axis=0, keepdims=True)
        lg2 = jnp.where(eidx == i1, -jnp.inf, logits)
        v2 = jnp.max(lg2, axis=0, keepdims=True)
        i2 = jnp.min(jnp.where(lg2 == v2, eidx, float(ne)), axis=0, keepdims=True)
        e2 = jnp.exp(v2 - v1)
        w1 = 1.0 / (1.0 + e2)
        w2 = e2 * w1
        sel1 = eidx == i1
        sel2 = eidx == i2
        onehot = jnp.where(sel1, 1.0, jnp.where(sel2, 1.0, 0.0))
        tri = (lax.broadcasted_iota(jnp.int32, (tm, tm), 0) <= lax.broadcasted_iota(jnp.int32, (tm, tm), 1))
        onehot16 = jnp.concatenate([onehot, jnp.zeros_like(onehot)], axis=0).astype(BF16)
        csum = _dot(onehot16, jnp.where(tri, 1.0, 0.0).astype(BF16))[0:ne]
        carry = carry_ref[:, 0:1]
        rank = csum - onehot + carry
        r1 = jnp.sum(jnp.where(sel1, rank, 0.0), axis=0, keepdims=True)
        r2 = jnp.sum(jnp.where(sel2, rank, 0.0), axis=0, keepdims=True)
        total = jnp.broadcast_to(carry + csum[:, tm - 1:tm], carry_ref.shape)
        carry_ref[...] = total
        counts_ref[...] = total
        rows = {ROUTE_I1: i1, ROUTE_I2: i2, ROUTE_W1: w1, ROUTE_W2: w2, ROUTE_R1: r1, ROUTE_R2: r2}
        zero = jnp.zeros_like(i1)
        route_ref[...] = jnp.concatenate([rows.get(r, zero) for r in range(8)], axis=0)


def _outproj(att, go, cv, x, w, g, w_router=None):
    route = w_router is not None
    tm = TM_PROJ
    tile = lambda cols: pl.BlockSpec((tm, cols), lambda i: (i, 0))
    full = lambda a: pl.BlockSpec(a.shape, lambda i: (0, 0))
    (o1, l1), (o4, l4), (o16, l16) = att
    args = [o1, o4, o16, l1, l4, l16, go, cv, x, w, g]
    att_specs = [_subseq_spec(d, ATT_WIDTH) for _, d in DILATED_PATTERNS]
    in_specs = att_specs * 2 + [tile(GLA_V), tile(CONV_CH), tile(D_MODEL), full(w), full(g)]
    out_specs = [tile(D_MODEL), tile(D_MODEL)]
    out_shape = [jax.ShapeDtypeStruct((TOKENS, D_MODEL), F32),
                 jax.ShapeDtypeStruct((TOKENS, D_MODEL), F32 if route else BF16)]
    scratch = []
    if route:
        args.append(w_router)
        in_specs.append(pl.BlockSpec(w_router.shape, lambda i: (0, 0, 0)))
        out_specs += [pl.BlockSpec((8, tm), lambda i: (0, i)), pl.BlockSpec((N_EXPERTS, LANES), lambda i: (0, 0))]
        out_shape += [jax.ShapeDtypeStruct((8, TOKENS), F32), jax.ShapeDtypeStruct((N_EXPERTS, LANES), F32)]
        scratch = [pltpu.VMEM((N_EXPERTS, LANES), F32)]
    scratch.append(pltpu.VMEM((4, ATT_WIDTH // LANES, tm, LANES), F32))
    return pl.pallas_call(
        functools.partial(_outproj_body, route=route),
        grid=(TOKENS // tm,),
        in_specs=in_specs,
        out_specs=out_specs,
        out_shape=out_shape,
        scratch_shapes=scratch,
        compiler_params=_params(("arbitrary",) if route else ("parallel",)),
        name="outproj_route" if route else "outproj",
    )(*args)


def _swiglu_accumulate(h, w1_ref, w3_ref, w2_ref, acc_ref, tf):
    for c0 in range(0, tf, FFN_SUB):
        c1 = min(c0 + FFN_SUB, tf)
        a = _dot(h, w1_ref[:, c0:c1])
        b = _dot(h, w3_ref[:, c0:c1])
        act = a * _sigmoid(a) * b
        acc_ref[...] += _dot(act.astype(BF16), w2_ref[c0:c1, :])


def _ffn_body(x_ref, h_ref, w1_ref, w3_ref, w2_ref, *rest, tf):
    n_cast = (len(rest) - 1) // 2
    o_ref = rest[n_cast]

    @pl.when(pl.program_id(1) == 0)
    def _():
        o_ref[...] = x_ref[...]

    _swiglu_accumulate(h_ref[...], w1_ref, w3_ref, w2_ref, o_ref, tf)
    _side_cast(rest[:n_cast], rest[n_cast + 1:])


def _ffn(x, h, w1, w3, w2, *, side_casts=()):
    nj, _, tf = w1.shape
    tm = TM_FFN
    tile = lambda cols: pl.BlockSpec((tm, cols), lambda i, j: (i, 0))
    cast_in, cast_out, cast_shapes = _side_cast_io(side_casts, (TOKENS // tm) * nj, lambda i, j: i * nj + j)
    resident = dict(pipeline_mode=pl.Buffered(1)) if nj == 1 else {}
    res = pl.pallas_call(
        functools.partial(_ffn_body, tf=tf),
        grid=(TOKENS // tm, nj),
        in_specs=[tile(D_MODEL), tile(D_MODEL),
                  pl.BlockSpec((None, D_MODEL, tf), lambda i, j: (j, 0, 0), **resident),
                  pl.BlockSpec((None, D_MODEL, tf), lambda i, j: (j, 0, 0), **resident),
                  pl.BlockSpec((tf, D_MODEL), lambda i, j: (j, 0), **resident)] + cast_in,
        out_specs=[tile(D_MODEL)] + cast_out,
        out_shape=[jax.ShapeDtypeStruct((TOKENS, D_MODEL), F32)] + cast_shapes,
        compiler_params=_params(("parallel", "arbitrary")),
        name="dense_ffn",
    )(x, h, w1, w3, w2, *[w for w, _ in side_casts])
    return res[0], res[1:]


ROW_TILE = D_MODEL // LANES


def _to_row_tiled(dst_ref, lead, val):
    rows = val.shape[0]
    for s in range(ROW_TILE):
        dst_ref[(*lead, pl.ds(s, rows, stride=ROW_TILE), slice(None))] = val[:, s * LANES:(s + 1) * LANES]


def _from_row_tiled(src_ref, lead, rows):
    return jnp.concatenate([src_ref[(*lead, pl.ds(s, rows, stride=ROW_TILE), slice(None))]
                            for s in range(ROW_TILE)], axis=-1)


def _row_tile(idx):
    return pl.ds(pl.multiple_of(idx * ROW_TILE, ROW_TILE), ROW_TILE)


def _dispatch_body(pos_ref, last_tile_ref, h_ref, xs_hbm, stage_ref, zero_ref, sem, zero_sem):
    i = pl.program_id(0)
    n = pl.num_programs(0)
    tm = DISPATCH_CHUNK
    slot = i % 2

    def drain(s):
        for _ in range(2):
            pltpu.make_async_copy(stage_ref.at[s], xs_hbm.at[pl.ds(0, tm * ROW_TILE)], sem.at[s]).wait()

    @pl.when(i == 0)
    def _():
        zero_ref[...] = jnp.zeros_like(zero_ref)

        def zero_copy(e):
            start = pl.multiple_of(last_tile_ref[e] * (TM_MOE * ROW_TILE), TM_MOE * ROW_TILE)
            return pltpu.make_async_copy(zero_ref, xs_hbm.at[pl.ds(start, TM_MOE * ROW_TILE)], zero_sem)

        for e in range(2 * N_EXPERTS):
            @pl.when(last_tile_ref[e] >= 0)
            def _():
                zero_copy(e).start()
        for e in range(2 * N_EXPERTS):
            @pl.when(last_tile_ref[e] >= 0)
            def _():
                zero_copy(e).wait()

    @pl.when(i >= 2)
    def _():
        drain(slot)

    _to_row_tiled(stage_ref, (slot,), h_ref[...])

    def body(t, carry):
        for k in range(2):
            dst = pos_ref[2 * (i * tm + t) + k]
            pltpu.make_async_copy(stage_ref.at[slot, _row_tile(t)], xs_hbm.at[_row_tile(dst)],
                                  sem.at[slot]).start(priority=k)
        return carry
    lax.fori_loop(0, tm, body, 0, unroll=8)

    @pl.when(i == n - 1)
    def _():
        drain(1 - slot)
        drain(slot)


def _dispatch(pos, last_tile, h):
    tm = DISPATCH_CHUNK
    return pl.pallas_call(
        _dispatch_body,
        grid=(TOKENS // tm,),
        in_specs=[pl.BlockSpec(memory_space=pltpu.SMEM),
                  pl.BlockSpec(memory_space=pltpu.SMEM),
                  pl.BlockSpec((tm, D_MODEL), lambda i: (i, 0))],
        out_specs=pl.BlockSpec(memory_space=pl.ANY),
        out_shape=jax.ShapeDtypeStruct((N_SORTED * ROW_TILE, LANES), F32),
        scratch_shapes=[pltpu.VMEM((2, tm * ROW_TILE, LANES), F32),
                        pltpu.VMEM((TM_MOE * ROW_TILE, LANES), F32),
                        pltpu.SemaphoreType.DMA((2,)), pltpu.SemaphoreType.DMA(())],
        compiler_params=_params(("arbitrary",)),
        name="moe_dispatch",
    )(pos, last_tile, h)


def _gffn_body(te_ref, nu_ref, xs_ref, w1_ref, w3_ref, w2_ref, o_ref, hb_ref, acc_ref):
    del te_ref
    i = pl.program_id(0)
    j = pl.program_id(1)

    used = i < nu_ref[0]

    @pl.when(j == 0)
    def _():
        acc_ref[...] = jnp.zeros_like(acc_ref)

    @pl.when(used & (j == 0))
    def _():
        hb_ref[...] = _from_row_tiled(xs_ref, (), TM_MOE).astype(BF16)

    @pl.when(used)
    def _():
        _swiglu_accumulate(hb_ref[...], w1_ref, w3_ref, w2_ref, acc_ref, TF_MOE)

    @pl.when(j == pl.num_programs(1) - 1)
    def _():
        _to_row_tiled(o_ref, (), acc_ref[...])


def _grouped_ffn(tile_expert, n_used, xs, w1, w3, w2):
    nj = FFN_EXPERT // TF_MOE
    col = lambda i, j, nu: jnp.where(i < nu[0], j, nj - 1)
    grid_spec = pltpu.PrefetchScalarGridSpec(
        num_scalar_prefetch=2,
        grid=(N_TILES_MOE, nj),
        in_specs=[
            pl.BlockSpec((TM_MOE * ROW_TILE, LANES), lambda i, j, te, nu: (jnp.minimum(i, nu[0] - 1), 0)),
            pl.BlockSpec((None, None, D_MODEL, TF_MOE), lambda i, j, te, nu: (te[i], col(i, j, nu), 0, 0)),
            pl.BlockSpec((None, None, D_MODEL, TF_MOE), lambda i, j, te, nu: (te[i], col(i, j, nu), 0, 0)),
            pl.BlockSpec((None, TF_MOE, D_MODEL), lambda i, j, te, nu: (te[i], col(i, j, nu), 0)),
        ],
        out_specs=pl.BlockSpec((TM_MOE * ROW_TILE, LANES), lambda i, j, te, nu: (i, 0)),
        scratch_shapes=[pltpu.VMEM((TM_MOE, D_MODEL), BF16), pltpu.VMEM((TM_MOE, D_MODEL), F32)],
    )
    return pl.pallas_call(
        _gffn_body,
        grid_spec=grid_spec,
        out_shape=jax.ShapeDtypeStruct((N_SORTED * ROW_TILE, LANES), F32),
        compiler_params=_params(("arbitrary", "arbitrary")),
        name="moe_ffn",
    )(tile_expert, n_used, xs, w1, w3, w2)


def _combine_body(pos_ref, x_ref, gate_ref, g_ref, ys_hbm, o_ref, buf_ref, sem):
    i = pl.program_id(0)
    n = pl.num_programs(0)
    tm = TM_COMBINE

    def issue(tile, slot):
        def body(t, carry):
            for k in range(2):
                src = pos_ref[2 * (tile * tm + t) + k]
                pltpu.make_async_copy(ys_hbm.at[_row_tile(src)], buf_ref.at[slot, k, _row_tile(t)],
                                      sem.at[slot]).start(priority=k)
            return carry
        lax.fori_loop(0, tm, body, 0, unroll=8)

    @pl.when(i == 0)
    def _():
        issue(0, 0)

    @pl.when(i + 1 < n)
    def _():
        issue(i + 1, (i + 1) % 2)

    slot = i % 2
    for k in range(2):
        pltpu.make_async_copy(ys_hbm.at[pl.ds(0, tm * ROW_TILE)], buf_ref.at[slot, k], sem.at[slot]).wait()
    w1 = gate_ref[:, 0:1]
    w2 = gate_ref[:, 1:2]
    y = x_ref[...] + w1 * _from_row_tiled(buf_ref, (slot, 0), tm) + w2 * _from_row_tiled(buf_ref, (slot, 1), tm)
    o_ref[...] = _rms(y, g_ref[...])


def _combine(pos, x, gates, g, ys):
    tm = TM_COMBINE
    return pl.pallas_call(
        _combine_body,
        grid=(TOKENS // tm,),
        in_specs=[pl.BlockSpec(memory_space=pltpu.SMEM),
                  pl.BlockSpec((tm, D_MODEL), lambda i: (i, 0)),
                  pl.BlockSpec((tm, 2), lambda i: (i, 0)),
                  pl.BlockSpec((1, D_MODEL), lambda i: (0, 0)),
                  pl.BlockSpec(memory_space=pl.ANY)],
        out_specs=pl.BlockSpec((tm, D_MODEL), lambda i: (i, 0)),
        out_shape=jax.ShapeDtypeStruct((TOKENS, D_MODEL), F32),
        scratch_shapes=[pltpu.VMEM((2, 2, tm * ROW_TILE, LANES), F32), pltpu.SemaphoreType.DMA((2,))],
        compiler_params=_params(("arbitrary",)),
        name="moe_combine",
    )(pos, x, gates, g, ys)


def _routing_tables(route, counts):
    cnt = counts[:, 0].astype(jnp.int32)
    tiles = (cnt + TM_MOE - 1) // TM_MOE
    tile_end = jnp.cumsum(tiles)
    tile_start = tile_end - tiles
    n_used = tile_end[-1]
    expert = route[ROUTE_I1:ROUTE_I2 + 1].astype(jnp.int32)
    rank = route[ROUTE_R1:ROUTE_R2 + 1].astype(jnp.int32)
    group_start = jnp.sum(jnp.where(expert[..., None] == jnp.arange(N_EXPERTS), tile_start * TM_MOE, 0), axis=-1)
    pos = (group_start + rank).T
    tile_id = jnp.minimum(jnp.arange(N_TILES_MOE, dtype=jnp.int32), n_used - 1)
    tile_expert = jnp.sum(tile_id[:, None] >= tile_end[None, :], axis=1).astype(jnp.int32)
    last_tile = jnp.where(tiles > 0, tile_end - 1, -1)
    spare = n_used + jnp.arange(N_EXPERTS)
    zero_tiles = jnp.concatenate([last_tile, jnp.where(spare < N_TILES_MOE, spare, -1)]).astype(jnp.int32)
    return pos.reshape(2 * TOKENS), tile_expert, n_used.reshape(1), zero_tiles


def _moe(x, h, route, counts, w1, w3, w2, g_final):
    pos, tile_expert, n_used, last_tile = _routing_tables(route, counts)
    xs = _dispatch(pos, last_tile, h)
    ys = _grouped_ffn(tile_expert, n_used, xs, w1, w3, w2)
    return _combine(pos, x, route[ROUTE_W1:ROUTE_W2 + 1].T, g_final, ys)


def _prep_router(w):
    wt = w.T
    hi = wt.astype(BF16)
    lo = (wt - hi.astype(F32)).astype(BF16)
    return jnp.stack([jnp.concatenate([hi, lo]), jnp.concatenate([hi, jnp.zeros_like(hi)])])


def kernel(x, w_mix_in, w_mix_out, g_mix, rel_bias, gla_w_gate, gla_b_gate, gla_g_norm, conv_w,
           g_ffn, ffn_w1, ffn_w3, ffn_w2, moe_router, moe_w1, moe_w3, moe_w2, g_final):
    assert DEPTH == 2
    x = x.reshape(TOKENS, D_MODEL)
    up_job = lambda w, tf: (w, w.shape[-1] // tf)
    down = lambda c: c.reshape(c.shape[0], c.shape[2], c.shape[3])
    experts = {}
    for layer in range(DEPTH):
        u, qkvs, casts = _inproj(x, g_mix[layer].reshape(1, D_MODEL), jnp.swapaxes(w_mix_in, 1, 2), layer,
                                 side_casts=[up_job(moe_w1[0], TF_MOE)] if layer == 1 else [])
        if layer == 1:
            experts["w1"] = casts[0]
        dense_jobs = [up_job(ffn_w1[:1], TF_DENSE), up_job(ffn_w3[:1], TF_DENSE), (ffn_w2[:1], 1)] if layer == 0 else []
        att, dense_bf16 = _attention(qkvs, rel_bias, dense_jobs)
        if layer == 0:
            ffn_bf16 = (dense_bf16[0][0], dense_bf16[1][0], down(dense_bf16[2])[0])
        wg = jnp.pad(gla_w_gate[layer], ((0, LANES - GLA_RANK), (0, 0)))
        go, cv = _gla_conv(u, wg, gla_b_gate[layer].reshape(1, GLA_QK), gla_g_norm[layer].reshape(1, GLA_DV),
                           jnp.pad(conv_w[layer], ((0, 8 - CONV_WIDTH), (0, 0))))
        w_out = w_mix_out[layer].astype(BF16)
        g2 = g_ffn[layer].reshape(1, D_MODEL)
        i = layer // 2
        if layer % 2 == 0:
            x, h = _outproj(att, go, cv, x, w_out, g2)
            x, (experts["w3"], cast) = _ffn(x, h, *ffn_bf16, side_casts=[up_job(moe_w3[i], TF_MOE), (moe_w2[i], 1)])
            experts["w2"] = down(cast)
        else:
            x, h, route, counts = _outproj(att, go, cv, x, w_out, g2, _prep_router(moe_router[i]))
            x = _moe(x, h, route, counts, experts["w1"], experts["w3"], experts["w2"], g_final.reshape(1, D_MODEL))
    return x.reshape(BATCH, SEQ, D_MODEL)
```

```python
import functools
import math

import jax
import jax.numpy as jnp
import numpy as np
from jax import lax
from jax.experimental import pallas as pl
from jax.experimental.pallas import tpu as pltpu

F32 = jnp.float32
BF16 = jnp.bfloat16

D_MODEL = 1024
BATCH = 8
SEQ = 2048
TOKENS = BATCH * SEQ
DEPTH = 2
EPS = 1e-6

HEAD_DIM = 64
ATT_HEADS = 4
ATT_WIDTH = ATT_HEADS * HEAD_DIM
DILATED_PATTERNS = ((128, 1), (512, 4), (2048, 16))
ATT_BLOCK = 128
REL_BUCKETS = 32
REL_MAX_DISTANCE = 2048

GLA_HEADS = 4
GLA_DK = 64
GLA_DV = 128
GLA_RANK = 16
GLA_CHUNK = 64
GLA_QK = GLA_HEADS * GLA_DK
GLA_V = GLA_HEADS * GLA_DV

CONV_CH = 256
CONV_WIDTH = 3
MIX_WIDTH = ATT_WIDTH + GLA_V + CONV_CH

SPLIT_SIZES = (ATT_WIDTH, ATT_WIDTH, ATT_WIDTH, GLA_QK, GLA_QK, GLA_V, GLA_V, GLA_RANK,
               CONV_CH, CONV_CH, CONV_CH)
IN_COLS = sum(SPLIT_SIZES)

FFN_DENSE = 2816
N_EXPERTS = 8
FFN_EXPERT = 3584

LANES = 128
MXU_WIDTH = 256
VMEM_LIMIT = 56 * 1024 * 1024

QKV_COLS = 3 * ATT_WIDTH
COL_GV, COL_GR, COL_GQ, COL_GK = 0, 512, 1024, 1280
COL_CIN, COL_CB, COL_CC, COL_GLR = 1536, 1792, 2048, 2304
U_COLS = COL_GLR + LANES

NEG_BIG = -1e30

ATT_UNROLL = 15
TM_PROJ = 512
TC_GLA = 1024
GLA_CUMSUM_SPAN = 256
TM_FFN = 512
TF_DENSE = 2816
FFN_SUB = 256
TM_MOE = 512
TF_MOE = 1792
N_TILES_MOE = 2 * TOKENS // TM_MOE + N_EXPERTS
N_SORTED = N_TILES_MOE * TM_MOE
DISPATCH_CHUNK = 512
TM_COMBINE = 512

ROUTE_I1, ROUTE_I2, ROUTE_W1, ROUTE_W2, ROUTE_R1, ROUTE_R2 = range(6)


def _params(sem):
    return pltpu.CompilerParams(dimension_semantics=sem, vmem_limit_bytes=VMEM_LIMIT)


def _split_bf16(a):
    hi = a.astype(BF16)
    lo = (a - hi.astype(F32)).astype(BF16)
    return hi, lo


def _dot(a, b):
    return jnp.dot(a, b, preferred_element_type=F32)


def _dot3(a, b):
    a_hi, a_lo = _split_bf16(a)
    b_hi, b_lo = _split_bf16(b)
    return _dot(a_hi, b_hi) + _dot(a_lo, b_hi) + _dot(a_hi, b_lo)


def _dot_nt(a, b):
    return lax.dot_general(a, b, (((1,), (1,)), ((), ())), preferred_element_type=F32)


def _rms(x, g):
    ms = jnp.mean(x * x, axis=-1, keepdims=True)
    return x * lax.rsqrt(ms + EPS) * g


def _sigmoid(x):
    return 1.0 / (1.0 + jnp.exp(-x))


def _side_cast(srcs, dsts):
    for src, dst in zip(srcs, dsts):
        width = dst.shape[-1]
        for s in range(dst.shape[0]):
            dst[s] = src[:, s * width:(s + 1) * width].astype(dst.dtype)


def _side_cast_io(jobs, steps, step_of):
    in_specs, out_specs, out_shapes = [], [], []
    for w, splits in jobs:
        g, r, c = w.shape
        rb = g * r // steps
        per_group = r // rb
        in_specs.append(pl.BlockSpec(
            (None, rb, c), lambda *ids, pg=per_group: (step_of(*ids) // pg, step_of(*ids) % pg, 0)))
        out_specs.append(pl.BlockSpec(
            (None, splits, rb, c // splits),
            lambda *ids, pg=per_group: (step_of(*ids) // pg, 0, step_of(*ids) % pg, 0)))
        out_shapes.append(jax.ShapeDtypeStruct((g, splits, r, c // splits), BF16))
    return in_specs, out_specs, out_shapes


def _w_in_moves():
    names = ("aq", "ak", "av", "gq", "gk", "gv", "gr", "glr", "c_in", "c_b", "c_c")
    src = dict(zip(names, zip(np.cumsum((0,) + SPLIT_SIZES[:-1]).tolist(), SPLIT_SIZES)))
    dst = {"aq": 0, "ak": ATT_WIDTH, "av": 2 * ATT_WIDTH, "gv": QKV_COLS + COL_GV, "gr": QKV_COLS + COL_GR,
           "gq": QKV_COLS + COL_GQ, "gk": QKV_COLS + COL_GK, "c_in": QKV_COLS + COL_CIN,
           "c_b": QKV_COLS + COL_CB, "c_c": QKV_COLS + COL_CC}
    return [(src[n][0], d, src[n][1]) for n, d in dst.items()], src["glr"][0]


def _stage_w_in(wt_ref, wb_ref):
    moves, glr = _w_in_moves()
    for s, d, width in moves:
        for c in range(0, width, MXU_WIDTH):
            wb_ref[:, d + c:d + c + MXU_WIDTH] = wt_ref[s + c:s + c + MXU_WIDTH, :].T.astype(BF16)
    row = lax.broadcasted_iota(jnp.int32, (LANES, D_MODEL), 0)
    wb_ref[:, QKV_COLS + COL_GLR:] = jnp.where(row < GLA_RANK, wt_ref[glr:glr + LANES, :], 0.0).T.astype(BF16)


def _inproj_body(x_ref, g_ref, wf_ref, *rest):
    n_pat = len(DILATED_PATTERNS)
    n_cast = (len(rest) - n_pat - 3) // 2
    o_ref = rest[n_cast]
    qkv_refs = rest[n_cast + 1:n_cast + 1 + n_pat]
    qkv_f32, w_ref = rest[-2:]

    @pl.when(pl.program_id(0) == 0)
    def _():
        _stage_w_in(wf_ref, w_ref)

    _side_cast(rest[:n_cast], rest[n_cast + 1 + n_pat:-2])
    h = _rms(x_ref[...], g_ref[...]).astype(BF16)
    for c0 in range(0, QKV_COLS, MXU_WIDTH):
        res = _dot(h, w_ref[:, c0:c0 + MXU_WIDTH])
        for t in range(MXU_WIDTH // LANES):
            qkv_f32[c0 // LANES + t] = res[:, t * LANES:(t + 1) * LANES]
    for (_, dilation), ref in zip(DILATED_PATTERNS, qkv_refs):
        for r in range(dilation):
            rows = pl.ds(r, TM_PROJ // dilation, stride=dilation)
            ref[r] = jnp.concatenate([qkv_f32[t, rows, :] for t in range(QKV_COLS // LANES)],
                                     axis=-1).astype(ref.dtype)
    for c0 in range(0, U_COLS, MXU_WIDTH):
        c1 = min(c0 + MXU_WIDTH, U_COLS)
        o_ref[:, c0:c1] = _dot(h, w_ref[:, QKV_COLS + c0:QKV_COLS + c1])


def _subseq_spec(dilation, cols):
    tiles = SEQ // TM_PROJ
    return pl.BlockSpec((None, dilation, TM_PROJ // dilation, cols), lambda i: (i // tiles, 0, i % tiles, 0))


def _inproj(x, g, w_all, layer, side_casts=()):
    n_pat = len(DILATED_PATTERNS)
    qkv_shapes = [jax.ShapeDtypeStruct((BATCH, d, SEQ // d, QKV_COLS), BF16) for _, d in DILATED_PATTERNS]
    cast_in, cast_out, cast_shapes = _side_cast_io(side_casts, TOKENS // TM_PROJ, lambda i: i)
    u, *rest = pl.pallas_call(
        _inproj_body,
        grid=(TOKENS // TM_PROJ,),
        in_specs=[
            pl.BlockSpec((TM_PROJ, D_MODEL), lambda i: (i, 0)),
            pl.BlockSpec((1, D_MODEL), lambda i: (0, 0)),
            pl.BlockSpec((None, IN_COLS, D_MODEL), lambda i: (layer, 0, 0), pipeline_mode=pl.Buffered(1)),
        ] + cast_in,
        out_specs=[pl.BlockSpec((TM_PROJ, U_COLS), lambda i: (i, 0))]
                  + [_subseq_spec(d, QKV_COLS) for _, d in DILATED_PATTERNS] + cast_out,
        out_shape=[jax.ShapeDtypeStruct((TOKENS, U_COLS), F32)] + qkv_shapes + cast_shapes,
        scratch_shapes=[pltpu.VMEM((QKV_COLS // LANES, TM_PROJ, LANES), F32),
                        pltpu.VMEM((D_MODEL, QKV_COLS + U_COLS), BF16)],
        compiler_params=_params(("arbitrary",)),
        name="inproj",
    )(x, g, w_all, *[w for w, _ in side_casts])
    return u, rest[:n_pat], rest[n_pat:]


def _rel_bucket(dist):
    max_exact = REL_BUCKETS // 2
    d = jnp.maximum(dist, 0)
    log_ratio = jnp.log(jnp.maximum(d, 1).astype(F32) / max_exact) / math.log(REL_MAX_DISTANCE / max_exact)
    large = jnp.minimum(max_exact + (log_ratio * (REL_BUCKETS - max_exact)).astype(jnp.int32), REL_BUCKETS - 1)
    return jnp.where(d < max_exact, d, large)


def _bucket_table(window, dilation):
    span = window // dilation
    qi = jnp.arange(ATT_BLOCK)[:, None]
    kj = jnp.arange(2 * ATT_BLOCK)[None, :]
    sub_dist = qi - kj + ATT_BLOCK
    band = (sub_dist >= 0) & (sub_dist <= span)
    return jnp.where(band, _rel_bucket(sub_dist * dilation), -1).astype(jnp.int32)


def _attn_body(rb_ref, bidx_ref, qkv_ref, *rest, sub_blocks, unroll):
    n_cast = (len(rest) - 3) // 2
    o_ref, lse_ref = rest[n_cast:n_cast + 2]
    bias_ref = rest[-1]
    _side_cast(rest[:n_cast], rest[n_cast + 2:-1])
    nblk = SEQ // ATT_BLOCK

    @pl.when(pl.program_id(0) == 0)
    def _():
        bidx = bidx_ref[...]
        in_prev = lax.broadcasted_iota(jnp.int32, bidx.shape, 1) < ATT_BLOCK
        for h in range(ATT_HEADS):
            acc = jnp.full(bidx.shape, NEG_BIG, F32)
            for b in range(REL_BUCKETS):
                acc = jnp.where(bidx == b, rb_ref[b, h], acc)
            bias_ref[0, h] = acc
            bias_ref[1, h] = jnp.where(in_prev, NEG_BIG, acc)
            bias_ref[2, h] = jnp.concatenate([acc[:, ATT_BLOCK:], jnp.full_like(acc[:, ATT_BLOCK:], NEG_BIG)], axis=1)

    def block(n, first):
        if first:
            rows, krows, variant = slice(0, ATT_BLOCK), slice(0, 2 * ATT_BLOCK), 2
        else:
            r0 = pl.multiple_of(n * ATT_BLOCK, ATT_BLOCK)
            rows = pl.ds(r0, ATT_BLOCK)
            krows = pl.ds(r0 - ATT_BLOCK, 2 * ATT_BLOCK)
            if sub_blocks == nblk:
                variant = 0
            elif sub_blocks == 1:
                variant = 1
            else:
                variant = jnp.where(n % sub_blocks == 0, 1, 0)
        q = qkv_ref[rows, 0:ATT_WIDTH]
        kk = qkv_ref[krows, ATT_WIDTH:2 * ATT_WIDTH]
        vv = qkv_ref[krows, 2 * ATT_WIDTH:3 * ATT_WIDTH]
        q = q * jnp.asarray(HEAD_DIM ** -0.5, BF16)
        head_of_lane = lax.broadcasted_iota(jnp.int32, (ATT_BLOCK, ATT_WIDTH), 1) // HEAD_DIM
        ones = jnp.ones((kk.shape[0], LANES), BF16)
        num = den = mx = None
        for h in range(ATT_HEADS):
            mine = head_of_lane == h
            bias = bias_ref[variant, h]
            s = _dot_nt(jnp.where(mine, q, jnp.zeros_like(q)), kk) + bias
            m = jnp.max(s, axis=-1, keepdims=True)
            p = jnp.exp(s - m).astype(BF16)
            num_h = _dot(p, vv)
            den_h = jnp.tile(_dot(p, ones), (1, ATT_WIDTH // LANES))
            m_h = jnp.broadcast_to(m, (ATT_BLOCK, ATT_WIDTH))
            num = num_h if h == 0 else jnp.where(mine, num_h, num)
            den = den_h if h == 0 else jnp.where(mine, den_h, den)
            mx = m_h if h == 0 else jnp.where(mine, m_h, mx)
        o_ref[rows, :] = (num / den).astype(o_ref.dtype)
        lse_ref[rows, :] = mx + jnp.log(den)

    block(0, True)

    def loop_body(n, carry):
        block(n, False)
        return carry
    lax.fori_loop(1, nblk, loop_body, 0, unroll=unroll)


def _attention_pattern(ua, rel_bias, window, dilation, side_casts=()):
    L = SEQ // dilation
    shape = (BATCH, dilation, L, ATT_WIDTH)
    qkv_spec = pl.BlockSpec((None, SEQ, QKV_COLS), lambda b: (b, 0, 0))
    out_spec = pl.BlockSpec((None, SEQ, ATT_WIDTH), lambda b: (b, 0, 0))
    cast_in, cast_out, cast_shapes = _side_cast_io(side_casts, BATCH, lambda b: b)
    o, lse, *casts = pl.pallas_call(
        functools.partial(_attn_body, sub_blocks=L // ATT_BLOCK, unroll=ATT_UNROLL),
        grid=(BATCH,),
        in_specs=[
            pl.BlockSpec(memory_space=pltpu.SMEM),
            pl.BlockSpec((ATT_BLOCK, 2 * ATT_BLOCK), lambda b: (0, 0)),
            qkv_spec,
        ] + cast_in,
        out_specs=[out_spec, out_spec] + cast_out,
        out_shape=[jax.ShapeDtypeStruct((BATCH, SEQ, ATT_WIDTH), BF16),
                   jax.ShapeDtypeStruct((BATCH, SEQ, ATT_WIDTH), F32)] + cast_shapes,
        scratch_shapes=[pltpu.VMEM((3, ATT_HEADS, ATT_BLOCK, 2 * ATT_BLOCK), F32)],
        compiler_params=_params(("arbitrary",)),
        name=f"attn_d{dilation}",
    )(rel_bias, _bucket_table(window, dilation), ua.reshape(BATCH, SEQ, QKV_COLS), *[w for w, _ in side_casts])
    return (o.reshape(shape), lse.reshape(shape)), casts


def _attention(qkvs, rel_bias, side_casts=()):
    side_casts = list(side_casts) + [None] * (len(DILATED_PATTERNS) - len(side_casts))
    res = [_attention_pattern(ua, rel_bias, window, dilation, [] if w is None else [w])
           for ua, (window, dilation), w in zip(qkvs, DILATED_PATTERNS, side_casts)]
    return [r[0] for r in res], [c for r in res for c in r[1]]


def _short_conv(cin_ref, cb_ref, cc_ref, w_ref, o_ref, tail_ref):
    uu = cc_ref[...] * cin_ref[...]
    t = lax.broadcasted_iota(jnp.int32, uu.shape, 0)
    y = uu * w_ref[CONV_WIDTH - 1:CONV_WIDTH, :]
    for shift in range(1, CONV_WIDTH):
        prev = pltpu.roll(uu, shift, axis=0)
        for r in range(shift):
            prev = jnp.where(t == r, tail_ref[8 - shift + r:8 - shift + r + 1, :], prev)
        y = y + prev * w_ref[CONV_WIDTH - 1 - shift:CONV_WIDTH - shift, :]
    o_ref[...] = (cb_ref[...] * y).astype(o_ref.dtype)
    tail_ref[...] = uu[uu.shape[0] - 8:, :]


def _gla_body(u_ref, wg_ref, bg_ref, gn_ref, ctril_ref, cw_ref, *rest):
    cols = lambda col, width: u_ref.at[:, col:col + width]
    q_ref, k_ref = cols(COL_GQ, GLA_QK), cols(COL_GK, GLA_QK)
    v_ref, gr_ref, glr_ref = cols(COL_GV, GLA_V), cols(COL_GR, GLA_V), cols(COL_GLR, LANES)
    cin_ref, cb_ref, cc_ref = cols(COL_CIN, CONV_CH), cols(COL_CB, CONV_CH), cols(COL_CC, CONV_CH)
    _gla_tile(q_ref, k_ref, v_ref, gr_ref, glr_ref, wg_ref, bg_ref, gn_ref, ctril_ref,
              cin_ref, cb_ref, cc_ref, cw_ref, *rest)


def _gla_tile(q_ref, k_ref, v_ref, gr_ref, glr_ref, wg_ref, bg_ref, gn_ref, ctril_ref,
              cin_ref, cb_ref, cc_ref, cw_ref, o_ref, cv_ref, s_ref, tail_ref):
    @pl.when(pl.program_id(1) == 0)
    def _():
        s_ref[...] = jnp.zeros_like(s_ref)
        tail_ref[...] = jnp.zeros_like(tail_ref)

    _short_conv(cin_ref, cb_ref, cc_ref, cw_ref, cv_ref, tail_ref)

    C = GLA_CHUNK
    row = lax.broadcasted_iota(jnp.int32, (C, C), 0)
    col = lax.broadcasted_iota(jnp.int32, (C, C), 1)
    tril = row >= col
    n_chunks = TC_GLA // C

    xg = _dot3(glr_ref[...], wg_ref[...]) + bg_ref[...]
    la_all = (jnp.minimum(xg, 0.0) - jnp.log(1.0 + jnp.exp(-jnp.abs(xg)))) * (1.0 / 16.0)

    la_hi, la_lo = _split_bf16(la_all)
    span = ctril_ref.shape[0]
    cum_all = jnp.concatenate(
        [_dot(ctril_ref[...], la_hi[r0:r0 + span]) + _dot(ctril_ref[...], la_lo[r0:r0 + span])
         for r0 in range(0, TC_GLA, span)], axis=0)
    totals = jnp.concatenate([cum_all[(c + 1) * C - 1:(c + 1) * C, :] for c in range(n_chunks)]
                             + [jnp.zeros((LANES - n_chunks, GLA_QK), F32)], axis=0)
    decay_cols = jnp.exp(totals.T)

    for c in range(n_chunks):
        rows = slice(c * C, (c + 1) * C)
        cum = cum_all[rows]
        last = cum[C - 1:C, :]
        q = q_ref[rows, :]
        k = k_ref[rows, :]
        qs = q * (GLA_DK ** -0.5)
        qt = (qs * jnp.exp(cum)).astype(BF16)
        mid = cum[C // 2 - 1:C // 2, :]
        qm = (qs * jnp.exp(cum - mid)).astype(BF16)
        kt = (k * jnp.exp(mid - cum)).astype(BF16)
        kl_t = (k * jnp.exp(last - cum)).T.astype(BF16)
        for h in range(GLA_HEADS):
            sl = slice(h * GLA_DK, (h + 1) * GLA_DK)
            vs = slice(h * GLA_DV, (h + 1) * GLA_DV)
            vh = v_ref[rows, vs].astype(BF16)
            state = s_ref[h]
            st_hi, st_lo = _split_bf16(state)
            sc = jnp.where(tril, _dot_nt(qm[:, sl], kt[:, sl]), 0.0).astype(BF16)
            o = _dot(qt[:, sl], st_hi) + _dot(qt[:, sl], st_lo) + _dot(sc, vh)
            decay = jnp.broadcast_to(decay_cols[sl, c:c + 1], state.shape)
            s_ref[h] = decay * state + _dot(kl_t[sl, :], vh)
            g = gr_ref[rows, vs]
            o_ref[rows, vs] = (_rms(o, gn_ref[...]) * (g * _sigmoid(g))).astype(o_ref.dtype)


def _gla_conv(u, wg, bg, gn, conv_w):
    nj = SEQ // TC_GLA
    row = lambda b, j: b * nj + j
    full = lambda a: pl.BlockSpec(a.shape, lambda b, j: (0, 0))
    t = np.arange(GLA_CUMSUM_SPAN)
    same_chunk = (t[:, None] // GLA_CHUNK) == (t[None, :] // GLA_CHUNK)
    chunk_tril = jnp.asarray(same_chunk & (t[:, None] >= t[None, :]), BF16)
    return pl.pallas_call(
        _gla_body,
        grid=(BATCH, nj),
        in_specs=[pl.BlockSpec((TC_GLA, U_COLS), lambda b, j: (row(b, j), 0)),
                  full(wg), full(bg), full(gn), full(chunk_tril), full(conv_w)],
        out_specs=[pl.BlockSpec((TC_GLA, GLA_V), lambda b, j: (row(b, j), 0)),
                   pl.BlockSpec((TC_GLA, CONV_CH), lambda b, j: (row(b, j), 0))],
        out_shape=[jax.ShapeDtypeStruct((TOKENS, GLA_V), BF16), jax.ShapeDtypeStruct((TOKENS, CONV_CH), BF16)],
        scratch_shapes=[pltpu.VMEM((GLA_HEADS, GLA_DK, GLA_DV), F32), pltpu.VMEM((8, CONV_CH), F32)],
        compiler_params=_params(("arbitrary", "arbitrary")),
        name="gla_conv",
    )(u, wg, bg, gn, chunk_tril, conv_w)


def _outproj_body(*refs, route):
    (o1, o4, o16, l1, l4, l16, go_ref, cv_ref, x_ref, w_ref, g_ref) = refs[:11]
    perm_ref = refs[-1]
    if route:
        wr_ref, xo_ref, ho_ref, route_ref, counts_ref, carry_ref = refs[11:-1]
    else:
        xo_ref, ho_ref = refs[11:-1]

    def token_order(ref, slot):
        dilation, rows, _ = ref.shape
        if dilation == 1:
            return ref[0].astype(F32)
        tiles = range(ATT_WIDTH // LANES)
        for r in range(dilation):
            val = ref[r].astype(F32)
            for t in tiles:
                perm_ref[slot, t, pl.ds(r, rows, stride=dilation), :] = val[:, t * LANES:(t + 1) * LANES]
        return jnp.concatenate([perm_ref[slot, t] for t in tiles], axis=-1)

    la, lb, lc = token_order(l1, 0), token_order(l4, 0), token_order(l16, 1)
    oa, ob, oc = token_order(o1, 0), token_order(o4, 2), token_order(o16, 3)
    m = jnp.maximum(jnp.maximum(la, lb), lc)
    ea, eb, ec = jnp.exp(la - m), jnp.exp(lb - m), jnp.exp(lc - m)
    att = (ea * oa + eb * ob + ec * oc) / (ea + eb + ec)
    y = (x_ref[...]
         + _dot(att.astype(BF16), w_ref[0:ATT_WIDTH, :])
         + _dot(go_ref[...], w_ref[ATT_WIDTH:ATT_WIDTH + GLA_V, :])
         + _dot(cv_ref[...], w_ref[ATT_WIDTH + GLA_V:MIX_WIDTH, :]))
    xo_ref[...] = y
    hf = _rms(y, g_ref[...])
    ho_ref[...] = hf.astype(ho_ref.dtype)
    if route:
        @pl.when(pl.program_id(0) == 0)
        def _():
            carry_ref[...] = jnp.zeros_like(carry_ref)

        tm = hf.shape[0]
        ne = N_EXPERTS
        hf_hi, hf_lo = _split_bf16(hf)
        part = _dot_nt(wr_ref[0], hf_hi) + _dot_nt(wr_ref[1], hf_lo)
        logits = part[0:ne] + part[ne:2 * ne]
        eidx = lax.broadcasted_iota(jnp.int32, logits.shape, 0).astype(F32)
        v1 = jnp.max(logits, axis=0, keepdims=True)
        i1 = jnp.min(jnp.where(logits == v1, eidx, float(ne)), axis=0, keepdims=True)
        lg2 = jnp.where(eidx == i1, -jnp.inf, logits)
        v2 = jnp.max(lg2, axis=0, keepdims=True)
        i2 = jnp.min(jnp.where(lg2 == v2, eidx, float(ne)), axis=0, keepdims=True)
        e2 = jnp.exp(v2 - v1)
        w1 = 1.0 / (1.0 + e2)
        w2 = e2 * w1
        sel1 = eidx == i1
        sel2 = eidx == i2
        onehot = jnp.where(sel1, 1.0, jnp.where(sel2, 1.0, 0.0))
        tri = (lax.broadcasted_iota(jnp.int32, (tm, tm), 0) <= lax.broadcasted_iota(jnp.int32, (tm, tm), 1))
        onehot16 = jnp.concatenate([onehot, jnp.zeros_like(onehot)], axis=0).astype(BF16)
        csum = _dot(onehot16, jnp.where(tri, 1.0, 0.0).astype(BF16))[0:ne]
        carry = carry_ref[:, 0:1]
        rank = csum - onehot + carry
        r1 = jnp.sum(jnp.where(sel1, rank, 0.0), axis=0, keepdims=True)
        r2 = jnp.sum(jnp.where(sel2, rank, 0.0), axis=0, keepdims=True)
        total = jnp.broadcast_to(carry + csum[:, tm - 1:tm], carry_ref.shape)
        carry_ref[...] = total
        counts_ref[...] = total
        rows = {ROUTE_I1: i1, ROUTE_I2: i2, ROUTE_W1: w1, ROUTE_W2: w2, ROUTE_R1: r1, ROUTE_R2: r2}
        zero = jnp.zeros_like(i1)
        route_ref[...] = jnp.concatenate([rows.get(r, zero) for r in range(8)], axis=0)


def _outproj(att, go, cv, x, w, g, w_router=None):
    route = w_router is not None
    tm = TM_PROJ
    tile = lambda cols: pl.BlockSpec((tm, cols), lambda i: (i, 0))
    full = lambda a: pl.BlockSpec(a.shape, lambda i: (0, 0))
    (o1, l1), (o4, l4), (o16, l16) = att
    args = [o1, o4, o16, l1, l4, l16, go, cv, x, w, g]
    att_specs = [_subseq_spec(d, ATT_WIDTH) for _, d in DILATED_PATTERNS]
    in_specs = att_specs * 2 + [tile(GLA_V), tile(CONV_CH), tile(D_MODEL), full(w), full(g)]
    out_specs = [tile(D_MODEL), tile(D_MODEL)]
    out_shape = [jax.ShapeDtypeStruct((TOKENS, D_MODEL), F32),
                 jax.ShapeDtypeStruct((TOKENS, D_MODEL), F32 if route else BF16)]
    scratch = []
    if route:
        args.append(w_router)
        in_specs.append(pl.BlockSpec(w_router.shape, lambda i: (0, 0, 0)))
        out_specs += [pl.BlockSpec((8, tm), lambda i: (0, i)), pl.BlockSpec((N_EXPERTS, LANES), lambda i: (0, 0))]
        out_shape += [jax.ShapeDtypeStruct((8, TOKENS), F32), jax.ShapeDtypeStruct((N_EXPERTS, LANES), F32)]
        scratch = [pltpu.VMEM((N_EXPERTS, LANES), F32)]
    scratch.append(pltpu.VMEM((4, ATT_WIDTH // LANES, tm, LANES), F32))
    return pl.pallas_call(
        functools.partial(_outproj_body, route=route),
        grid=(TOKENS // tm,),
        in_specs=in_specs,
        out_specs=out_specs,
        out_shape=out_shape,
        scratch_shapes=scratch,
        compiler_params=_params(("arbitrary",) if route else ("parallel",)),
        name="outproj_route" if route else "outproj",
    )(*args)


def _swiglu_accumulate(h, w1_ref, w3_ref, w2_ref, acc_ref, tf):
    for c0 in range(0, tf, FFN_SUB):
        c1 = min(c0 + FFN_SUB, tf)
        a = _dot(h, w1_ref[:, c0:c1])
        b = _dot(h, w3_ref[:, c0:c1])
        act = a * _sigmoid(a) * b
        acc_ref[...] += _dot(act.astype(BF16), w2_ref[c0:c1, :])


def _ffn_body(x_ref, h_ref, w1_ref, w3_ref, w2_ref, *rest, tf):
    n_cast = (len(rest) - 1) // 2
    o_ref = rest[n_cast]

    @pl.when(pl.program_id(1) == 0)
    def _():
        o_ref[...] = x_ref[...]

    _swiglu_accumulate(h_ref[...], w1_ref, w3_ref, w2_ref, o_ref, tf)
    _side_cast(rest[:n_cast], rest[n_cast + 1:])


def _ffn(x, h, w1, w3, w2, *, side_casts=()):
    nj, _, tf = w1.shape
    tm = TM_FFN
    tile = lambda cols: pl.BlockSpec((tm, cols), lambda i, j: (i, 0))
    cast_in, cast_out, cast_shapes = _side_cast_io(side_casts, (TOKENS // tm) * nj, lambda i, j: i * nj + j)
    resident = dict(pipeline_mode=pl.Buffered(1)) if nj == 1 else {}
    res = pl.pallas_call(
        functools.partial(_ffn_body, tf=tf),
        grid=(TOKENS // tm, nj),
        in_specs=[tile(D_MODEL), tile(D_MODEL),
                  pl.BlockSpec((None, D_MODEL, tf), lambda i, j: (j, 0, 0), **resident),
                  pl.BlockSpec((None, D_MODEL, tf), lambda i, j: (j, 0, 0), **resident),
                  pl.BlockSpec((tf, D_MODEL), lambda i, j: (j, 0), **resident)] + cast_in,
        out_specs=[tile(D_MODEL)] + cast_out,
        out_shape=[jax.ShapeDtypeStruct((TOKENS, D_MODEL), F32)] + cast_shapes,
        compiler_params=_params(("parallel", "arbitrary")),
        name="dense_ffn",
    )(x, h, w1, w3, w2, *[w for w, _ in side_casts])
    return res[0], res[1:]


ROW_TILE = D_MODEL // LANES


def _to_row_tiled(dst_ref, lead, val):
    rows = val.shape[0]
    for s in range(ROW_TILE):
        dst_ref[(*lead, pl.ds(s, rows, stride=ROW_TILE), slice(None))] = val[:, s * LANES:(s + 1) * LANES]


def _from_row_tiled(src_ref, lead, rows):
    return jnp.concatenate([src_ref[(*lead, pl.ds(s, rows, stride=ROW_TILE), slice(None))]
                            for s in range(ROW_TILE)], axis=-1)


def _row_tile(idx):
    return pl.ds(pl.multiple_of(idx * ROW_TILE, ROW_TILE), ROW_TILE)


def _dispatch_body(pos_ref, last_tile_ref, h_ref, xs_hbm, stage_ref, zero_ref, sem, zero_sem):
    i = pl.program_id(0)
    n = pl.num_programs(0)
    tm = DISPATCH_CHUNK
    slot = i % 2

    def drain(s):
        for _ in range(2):
            pltpu.make_async_copy(stage_ref.at[s], xs_hbm.at[pl.ds(0, tm * ROW_TILE)], sem.at[s]).wait()

    @pl.when(i == 0)
    def _():
        zero_ref[...] = jnp.zeros_like(zero_ref)

        def zero_copy(e):
            start = pl.multiple_of(last_tile_ref[e] * (TM_MOE * ROW_TILE), TM_MOE * ROW_TILE)
            return pltpu.make_async_copy(zero_ref, xs_hbm.at[pl.ds(start, TM_MOE * ROW_TILE)], zero_sem)

        for e in range(2 * N_EXPERTS):
            @pl.when(last_tile_ref[e] >= 0)
            def _():
                zero_copy(e).start()
        for e in range(2 * N_EXPERTS):
            @pl.when(last_tile_ref[e] >= 0)
            def _():
                zero_copy(e).wait()

    @pl.when(i >= 2)
    def _():
        drain(slot)

    _to_row_tiled(stage_ref, (slot,), h_ref[...])

    def body(t, carry):
        for k in range(2):
            dst = pos_ref[k * TOKENS + i * tm + t]
            pltpu.make_async_copy(stage_ref.at[slot, _row_tile(t)], xs_hbm.at[_row_tile(dst)],
                                  sem.at[slot]).start(priority=k)
        return carry
    lax.fori_loop(0, tm, body, 0, unroll=8)

    @pl.when(i == n - 1)
    def _():
        drain(1 - slot)
        drain(slot)


def _dispatch(pos, last_tile, h):
    tm = DISPATCH_CHUNK
    return pl.pallas_call(
        _dispatch_body,
        grid=(TOKENS // tm,),
        in_specs=[pl.BlockSpec(memory_space=pltpu.SMEM),
                  pl.BlockSpec(memory_space=pltpu.SMEM),
                  pl.BlockSpec((tm, D_MODEL), lambda i: (i, 0))],
        out_specs=pl.BlockSpec(memory_space=pl.ANY),
        out_shape=jax.ShapeDtypeStruct((N_SORTED * ROW_TILE, LANES), F32),
        scratch_shapes=[pltpu.VMEM((2, tm * ROW_TILE, LANES), F32),
                        pltpu.VMEM((TM_MOE * ROW_TILE, LANES), F32),
                        pltpu.SemaphoreType.DMA((2,)), pltpu.SemaphoreType.DMA(())],
        compiler_params=_params(("arbitrary",)),
        name="moe_dispatch",
    )(pos, last_tile, h)


def _gffn_body(te_ref, nu_ref, xs_ref, w1_ref, w3_ref, w2_ref, o_ref, hb_ref, acc_ref):
    del te_ref
    i = pl.program_id(0)
    j = pl.program_id(1)

    used = i < nu_ref[0]

    @pl.when(j == 0)
    def _():
        acc_ref[...] = jnp.zeros_like(acc_ref)

    @pl.when(used & (j == 0))
    def _():
        hb_ref[...] = _from_row_tiled(xs_ref, (), TM_MOE).astype(BF16)

    @pl.when(used)
    def _():
        _swiglu_accumulate(hb_ref[...], w1_ref, w3_ref, w2_ref, acc_ref, TF_MOE)

    @pl.when(j == pl.num_programs(1) - 1)
    def _():
        _to_row_tiled(o_ref, (), acc_ref[...])


def _grouped_ffn(tile_expert, n_used, xs, w1, w3, w2):
    nj = FFN_EXPERT // TF_MOE
    col = lambda i, j, nu: jnp.where(i < nu[0], j, nj - 1)
    grid_spec = pltpu.PrefetchScalarGridSpec(
        num_scalar_prefetch=2,
        grid=(N_TILES_MOE, nj),
        in_specs=[
            pl.BlockSpec((TM_MOE * ROW_TILE, LANES), lambda i, j, te, nu: (jnp.minimum(i, nu[0] - 1), 0)),
            pl.BlockSpec((None, None, D_MODEL, TF_MOE), lambda i, j, te, nu: (te[i], col(i, j, nu), 0, 0)),
            pl.BlockSpec((None, None, D_MODEL, TF_MOE), lambda i, j, te, nu: (te[i], col(i, j, nu), 0, 0)),
            pl.BlockSpec((None, TF_MOE, D_MODEL), lambda i, j, te, nu: (te[i], col(i, j, nu), 0)),
        ],
        out_specs=pl.BlockSpec((TM_MOE * ROW_TILE, LANES), lambda i, j, te, nu: (i, 0)),
        scratch_shapes=[pltpu.VMEM((TM_MOE, D_MODEL), BF16), pltpu.VMEM((TM_MOE, D_MODEL), F32)],
    )
    return pl.pallas_call(
        _gffn_body,
        grid_spec=grid_spec,
        out_shape=jax.ShapeDtypeStruct((N_SORTED * ROW_TILE, LANES), F32),
        compiler_params=_params(("arbitrary", "arbitrary")),
        name="moe_ffn",
    )(tile_expert, n_used, xs, w1, w3, w2)


def _combine_body(pos_ref, x_ref, gate_ref, g_ref, ys_hbm, o_ref, buf_ref, sem):
    i = pl.program_id(0)
    n = pl.num_programs(0)
    tm = TM_COMBINE

    def issue(tile, slot):
        def body(t, carry):
            for k in range(2):
                src = pos_ref[k * TOKENS + tile * tm + t]
                pltpu.make_async_copy(ys_hbm.at[_row_tile(src)], buf_ref.at[slot, k, _row_tile(t)],
                                      sem.at[slot]).start(priority=k)
            return carry
        lax.fori_loop(0, tm, body, 0, unroll=8)

    @pl.when(i == 0)
    def _():
        issue(0, 0)

    @pl.when(i + 1 < n)
    def _():
        issue(i + 1, (i + 1) % 2)

    slot = i % 2
    for k in range(2):
        pltpu.make_async_copy(ys_hbm.at[pl.ds(0, tm * ROW_TILE)], buf_ref.at[slot, k], sem.at[slot]).wait()
    w1 = gate_ref[:, 0:1]
    w2 = gate_ref[:, 1:2]
    y = x_ref[...] + w1 * _from_row_tiled(buf_ref, (slot, 0), tm) + w2 * _from_row_tiled(buf_ref, (slot, 1), tm)
    o_ref[...] = _rms(y, g_ref[...])


def _combine(pos, x, gates, g, ys):
    tm = TM_COMBINE
    return pl.pallas_call(
        _combine_body,
        grid=(TOKENS // tm,),
        in_specs=[pl.BlockSpec(memory_space=pltpu.SMEM),
                  pl.BlockSpec((tm, D_MODEL), lambda i: (i, 0)),
                  pl.BlockSpec((tm, 2), lambda i: (i, 0)),
                  pl.BlockSpec((1, D_MODEL), lambda i: (0, 0)),
                  pl.BlockSpec(memory_space=pl.ANY)],
        out_specs=pl.BlockSpec((tm, D_MODEL), lambda i: (i, 0)),
        out_shape=jax.ShapeDtypeStruct((TOKENS, D_MODEL), F32),
        scratch_shapes=[pltpu.VMEM((2, 2, tm * ROW_TILE, LANES), F32), pltpu.SemaphoreType.DMA((2,))],
        compiler_params=_params(("arbitrary",)),
        name="moe_combine",
    )(pos, x, gates, g, ys)


def _routing_tables(route, counts):
    cnt = counts[:, 0].astype(jnp.int32)
    tiles = (cnt + TM_MOE - 1) // TM_MOE
    tile_end = jnp.cumsum(tiles)
    tile_start = tile_end - tiles
    n_used = tile_end[-1]
    expert = route[ROUTE_I1:ROUTE_I2 + 1].astype(jnp.int32)
    rank = route[ROUTE_R1:ROUTE_R2 + 1].astype(jnp.int32)
    group_start = jnp.sum(jnp.where(expert[..., None] == jnp.arange(N_EXPERTS), tile_start * TM_MOE, 0), axis=-1)
    pos = group_start + rank
    tile_id = jnp.minimum(jnp.arange(N_TILES_MOE, dtype=jnp.int32), n_used - 1)
    tile_expert = jnp.sum(tile_id[:, None] >= tile_end[None, :], axis=1).astype(jnp.int32)
    last_tile = jnp.where(tiles > 0, tile_end - 1, -1)
    spare = n_used + jnp.arange(N_EXPERTS)
    zero_tiles = jnp.concatenate([last_tile, jnp.where(spare < N_TILES_MOE, spare, -1)]).astype(jnp.int32)
    return pos.reshape(2 * TOKENS), tile_expert, n_used.reshape(1), zero_tiles


def _moe(x, h, route, counts, w1, w3, w2, g_final):
    pos, tile_expert, n_used, last_tile = _routing_tables(route, counts)
    xs = _dispatch(pos, last_tile, h)
    ys = _grouped_ffn(tile_expert, n_used, xs, w1, w3, w2)
    return _combine(pos, x, route[ROUTE_W1:ROUTE_W2 + 1].T, g_final, ys)


def _prep_router(w):
    wt = w.T
    hi = wt.astype(BF16)
    lo = (wt - hi.astype(F32)).astype(BF16)
    return jnp.stack([jnp.concatenate([hi, lo]), jnp.concatenate([hi, jnp.zeros_like(hi)])])


def kernel(x, w_mix_in, w_mix_out, g_mix, rel_bias, gla_w_gate, gla_b_gate, gla_g_norm, conv_w,
           g_ffn, ffn_w1, ffn_w3, ffn_w2, moe_router, moe_w1, moe_w3, moe_w2, g_final):
    assert DEPTH == 2
    x = x.reshape(TOKENS, D_MODEL)
    up_job = lambda w, tf: (w, w.shape[-1] // tf)
    down = lambda c: c.reshape(c.shape[0], c.shape[2], c.shape[3])
    experts = {}
    w_in_t = jnp.swapaxes(w_mix_in, 1, 2)
    for layer in range(DEPTH):
        u, qkvs, casts = _inproj(x, g_mix[layer].reshape(1, D_MODEL), w_in_t, layer,
                                 side_casts=[up_job(moe_w1[0], TF_MOE)] if layer == 1 else [])
        if layer == 1:
            experts["w1"] = casts[0]
        dense_jobs = [up_job(ffn_w1[:1], TF_DENSE), up_job(ffn_w3[:1], TF_DENSE), (ffn_w2[:1], 1)] if layer == 0 else []
        att, dense_bf16 = _attention(qkvs, rel_bias, dense_jobs)
        if layer == 0:
            ffn_bf16 = (dense_bf16[0][0], dense_bf16[1][0], down(dense_bf16[2])[0])
        wg = jnp.pad(gla_w_gate[layer], ((0, LANES - GLA_RANK), (0, 0)))
        go, cv = _gla_conv(u, wg, gla_b_gate[layer].reshape(1, GLA_QK), gla_g_norm[layer].reshape(1, GLA_DV),
                           jnp.pad(conv_w[layer], ((0, 8 - CONV_WIDTH), (0, 0))))
        w_out = w_mix_out[layer].astype(BF16)
        g2 = g_ffn[layer].reshape(1, D_MODEL)
        i = layer // 2
        if layer % 2 == 0:
            x, h = _outproj(att, go, cv, x, w_out, g2)
            x, (experts["w3"], cast) = _ffn(x, h, *ffn_bf16, side_casts=[up_job(moe_w3[i], TF_MOE), (moe_w2[i], 1)])
            experts["w2"] = down(cast)
        else:
            x, h, route, counts = _outproj(att, go, cv, x, w_out, g2, _prep_router(moe_router[i]))
            x = _moe(x, h, route, counts, experts["w1"], experts["w3"], experts["w2"], g_final.reshape(1, D_MODEL))
    return x.reshape(BATCH, SEQ, D_MODEL)
```
